```python
import math
import jax, jax.numpy as jnp
from jax import lax
import numpy as np

D_MODEL = 1024
BATCH = 8
SEQ = 2048
DEPTH = 4

SSM_WIDTH = 512
SSM_GROUP = 16
SSM_GROUPS = SSM_WIDTH // SSM_GROUP
SSM_STATE = 64
DT_MIN = 1e-3
DT_MAX = 1e-1
HEAD_DIM = 128
HEADS_PER_GROUP = 4
DILATION_PATTERNS = ((128, 1), (512, 4), (2048, 16))
N_ATTN_GROUPS = len(DILATION_PATTERNS)
ATTN_HEADS = N_ATTN_GROUPS * HEADS_PER_GROUP
ATTN_QKV_WIDTH = ATTN_HEADS * HEAD_DIM
ATTN_WIDTH = HEADS_PER_GROUP * HEAD_DIM
IN_SPLITS = (SSM_WIDTH, SSM_WIDTH, ATTN_QKV_WIDTH, ATTN_QKV_WIDTH, ATTN_QKV_WIDTH,
             ATTN_WIDTH, D_MODEL, D_MODEL)
IN_COLS = sum(IN_SPLITS)
SPLIT_POINTS = tuple(int(s) for s in np.cumsum(IN_SPLITS)[:-1])
RMS_EPS = 1e-6

kernel_name = "hybrid_s5_dilated_attn_gated_block"


def rmsnorm(x, g):
    xf = x.astype(jnp.float32)
    inv = lax.rsqrt(jnp.mean(xf * xf, axis=-1, keepdims=True) + RMS_EPS)
    return (xf * inv * g.astype(jnp.float32)).astype(x.dtype)


def _ssm_combine(left, right):
    a_l, b_l = left
    a_r, b_r = right
    return a_r * a_l, a_r * b_l + b_r


def s5_branch(u, lam_re, lam_im, log_dt, b_re, b_im, c_re, c_im, d_skip, w_glu, b_glu):
    bsz, L, _ = u.shape
    uf = u.astype(jnp.float32)
    ug = uf.reshape(bsz, L, SSM_GROUPS, SSM_GROUP)
    lam = lax.complex(jnp.minimum(lam_re.astype(jnp.float32), -1e-4), lam_im.astype(jnp.float32))
    dt = jnp.exp(log_dt.astype(jnp.float32))[:, None]
    lam_bar = jnp.exp(lam * dt)
    b = lax.complex(b_re.astype(jnp.float32), b_im.astype(jnp.float32))
    b_bar = ((lam_bar - 1.0) / lam)[..., None] * b
    c = lax.complex(c_re.astype(jnp.float32), c_im.astype(jnp.float32))
    drive = jnp.einsum('gpc,blgc->blgp', b_bar, ug)
    decay = jnp.broadcast_to(lam_bar, drive.shape)
    _, states = lax.associative_scan(_ssm_combine, (decay, drive), axis=1)
    y = jnp.einsum('gcp,blgp->blgc', c, states).real.reshape(bsz, L, SSM_WIDTH)
    y = y + d_skip.astype(jnp.float32) * uf
    y = jax.nn.gelu(y)
    y = y * jax.nn.sigmoid(y @ w_glu.astype(jnp.float32) + b_glu.astype(jnp.float32))
    return y.astype(u.dtype)


def dilated_group_attention(q, k, v, window, dilation):
    bsz, L, hg, hd = q.shape
    span = window // dilation
    n = L // dilation
    nb = -(-n // span)
    pad = nb * span - n

    def to_sub(t):
        return t.reshape(bsz, n, dilation, hg, hd).transpose(0, 2, 3, 1, 4)

    qb = jnp.pad(to_sub(q), ((0, 0),) * 3 + ((0, pad), (0, 0))).reshape(bsz, dilation, hg, nb, span, hd)

    def kv_blocks(t):
        tp = jnp.pad(to_sub(t), ((0, 0),) * 3 + ((span, pad), (0, 0))).reshape(bsz, dilation, hg, nb + 1, span, hd)
        return jnp.concatenate([tp[:, :, :, :-1], tp[:, :, :, 1:]], axis=4)

    kb, vb = kv_blocks(k), kv_blocks(v)
    scores = jnp.einsum('brhnqd,brhnkd->brhnqk', qb, kb).astype(jnp.float32) * (hd ** -0.5)
    qi = jnp.arange(span)[:, None]
    ki = jnp.arange(2 * span)[None, :]
    blk = jnp.arange(nb)[:, None, None]
    dist = span + qi - ki
    valid = (dist >= 0) & (dist <= span) & (blk * span + ki - span >= 0)
    scores = jnp.where(valid, scores, -jnp.inf)
    lse = jax.nn.logsumexp(scores, axis=-1)
    p = jnp.exp(scores - lse[..., None])
    out = jnp.einsum('brhnqk,brhnkd->brhnqd', p.astype(v.dtype), vb)

    def from_sub(t):
        rest = t.shape[5:]
        t = t.reshape(bsz, dilation, hg, nb * span, *rest)[:, :, :, :n]
        t = jnp.moveaxis(t, 3, 1)
        return t.reshape(bsz, L, hg, *rest)

    return from_sub(out), from_sub(lse)


def dilated_attention_branch(q, k, v):
    bsz, L, _ = q.shape
    shp = (bsz, L, N_ATTN_GROUPS, HEADS_PER_GROUP, HEAD_DIM)
    q, k, v = q.reshape(shp), k.reshape(shp), v.reshape(shp)
    outs, lses = [], []
    for gi, (window, dilation) in enumerate(DILATION_PATTERNS):
        o, s = dilated_group_attention(q[:, :, gi], k[:, :, gi], v[:, :, gi], window, dilation)
        outs.append(o)
        lses.append(s)
    outs = jnp.stack(outs, axis=0)
    alpha = jax.nn.softmax(jnp.stack(lses, axis=0), axis=0)
    y = jnp.sum(alpha[..., None] * outs.astype(jnp.float32), axis=0)
    return y.reshape(bsz, L, ATTN_WIDTH).astype(q.dtype)


def _fwd_setup_inputs(seed: int = 0) -> dict:
    key = jax.random.key(seed)
    ks = jax.random.split(key, 20)
    f32 = jnp.float32
    nrm = lambda k, shape, s: jax.random.normal(k, shape, f32) * s
    x = jax.random.normal(ks[0], (BATCH, SEQ, D_MODEL), f32)
    pre_norm_g = 1.0 + nrm(ks[1], (DEPTH, D_MODEL), 0.02)
    w_in = nrm(ks[2], (DEPTH, D_MODEL, IN_COLS), D_MODEL ** -0.5)
    lambda_re = -0.5 + nrm(ks[3], (DEPTH, SSM_GROUPS, SSM_STATE), 0.01)
    lambda_im = (math.pi * jnp.arange(SSM_STATE, dtype=f32))[None, None, :] + nrm(ks[4], (DEPTH, SSM_GROUPS, SSM_STATE), 0.01)
    log_dt = jax.random.uniform(ks[5], (DEPTH, SSM_GROUPS), f32, math.log(DT_MIN), math.log(DT_MAX))
    b_scale = (2.0 * SSM_GROUP) ** -0.5
    b_re = nrm(ks[6], (DEPTH, SSM_GROUPS, SSM_STATE, SSM_GROUP), b_scale)
    b_im = nrm(ks[7], (DEPTH, SSM_GROUPS, SSM_STATE, SSM_GROUP), b_scale)
    c_scale = (2.0 * SSM_STATE) ** -0.5
    c_re = nrm(ks[8], (DEPTH, SSM_GROUPS, SSM_GROUP, SSM_STATE), c_scale)
    c_im = nrm(ks[9], (DEPTH, SSM_GROUPS, SSM_GROUP, SSM_STATE), c_scale)
    d_skip = nrm(ks[10], (DEPTH, SSM_WIDTH), 1.0)
    w_glu = nrm(ks[11], (DEPTH, SSM_WIDTH, SSM_WIDTH), SSM_WIDTH ** -0.5)
    b_glu = nrm(ks[12], (DEPTH, SSM_WIDTH), 0.01)
    w_branch_s = nrm(ks[13], (DEPTH, SSM_WIDTH, D_MODEL), SSM_WIDTH ** -0.5)
    w_branch_a = nrm(ks[14], (DEPTH, ATTN_WIDTH, D_MODEL), ATTN_WIDTH ** -0.5)
    w_out = nrm(ks[15], (DEPTH, D_MODEL, D_MODEL), D_MODEL ** -0.5)
    post_norm_g = 1.0 + nrm(ks[16], (DEPTH, D_MODEL), 0.02)
    return {"x": x, "pre_norm_g": pre_norm_g, "w_in": w_in, "lambda_re": lambda_re,
            "lambda_im": lambda_im, "log_dt": log_dt, "b_re": b_re, "b_im": b_im,
            "c_re": c_re, "c_im": c_im, "d_skip": d_skip, "w_glu": w_glu, "b_glu": b_glu,
            "w_branch_s": w_branch_s, "w_branch_a": w_branch_a, "w_out": w_out,
            "post_norm_g": post_norm_g}


def _fwd_reference(x, pre_norm_g, w_in, lambda_re, lambda_im, log_dt, b_re, b_im, c_re, c_im,
              d_skip, w_glu, b_glu, w_branch_s, w_branch_a, w_out, post_norm_g):
    for l in range(DEPTH):
        h = rmsnorm(x, pre_norm_g[l])
        proj = h @ w_in[l]
        u_s, z_s, q, k, v, z_a, g_s, g_a = jnp.split(proj, SPLIT_POINTS, axis=-1)
        y_s = s5_branch(u_s, lambda_re[l], lambda_im[l], log_dt[l], b_re[l], b_im[l],
                        c_re[l], c_im[l], d_skip[l], w_glu[l], b_glu[l]) * jax.nn.silu(z_s)
        y_a = dilated_attention_branch(q, k, v) * jax.nn.silu(z_a)
        merged = (jax.nn.sigmoid(g_s) * (y_s @ w_branch_s[l])
                  + jax.nn.sigmoid(g_a) * (y_a @ w_branch_a[l]))
        out = merged @ w_out[l]
        x = x + rmsnorm(out, post_norm_g[l]).astype(x.dtype)
    return x


import jax as _jax
import jax.numpy as _jnp

TWIN_FORMAT = 'train_step'
FWD_PARAMS = ['x', 'pre_norm_g', 'w_in', 'lambda_re', 'lambda_im', 'log_dt', 'b_re', 'b_im', 'c_re', 'c_im', 'd_skip', 'w_glu', 'b_glu', 'w_branch_s', 'w_branch_a', 'w_out', 'post_norm_g']
TWIN_WEIGHTS = ['pre_norm_g', 'w_in', 'lambda_re', 'lambda_im', 'log_dt', 'b_re', 'b_im', 'c_re', 'c_im', 'd_skip', 'w_glu', 'b_glu', 'w_branch_s', 'w_branch_a', 'w_out', 'post_norm_g']
TWIN_DIFF_INPUT = 'x'
TWIN_INPUTS = ['x', 'pre_norm_g', 'w_in', 'lambda_re', 'lambda_im', 'log_dt', 'b_re', 'b_im', 'c_re', 'c_im', 'd_skip', 'w_glu', 'b_glu', 'w_branch_s', 'w_branch_a', 'w_out', 'post_norm_g', 'loss_target', 'm_pre_norm_g', 'm_w_in', 'm_lambda_re', 'm_lambda_im', 'm_log_dt', 'm_b_re', 'm_b_im', 'm_c_re', 'm_c_im', 'm_d_skip', 'm_w_glu', 'm_b_glu', 'm_w_branch_s', 'm_w_branch_a', 'm_w_out', 'm_post_norm_g', 'v_pre_norm_g', 'v_w_in', 'v_lambda_re', 'v_lambda_im', 'v_log_dt', 'v_b_re', 'v_b_im', 'v_c_re', 'v_c_im', 'v_d_skip', 'v_w_glu', 'v_b_glu', 'v_w_branch_s', 'v_w_branch_a', 'v_w_out', 'v_post_norm_g']
TWIN_OUTPUTS = ['loss', 'grad_x', 'grad_pre_norm_g', 'grad_w_in', 'grad_lambda_re', 'grad_lambda_im', 'grad_log_dt', 'grad_b_re', 'grad_b_im', 'grad_c_re', 'grad_c_im', 'grad_d_skip', 'grad_w_glu', 'grad_b_glu', 'grad_w_branch_s', 'grad_w_branch_a', 'grad_w_out', 'grad_post_norm_g', 'delta_pre_norm_g', 'delta_w_in', 'delta_lambda_re', 'delta_lambda_im', 'delta_log_dt', 'delta_b_re', 'delta_b_im', 'delta_c_re', 'delta_c_im', 'delta_d_skip', 'delta_w_glu', 'delta_b_glu', 'delta_w_branch_s', 'delta_w_branch_a', 'delta_w_out', 'delta_post_norm_g', 'new_m_pre_norm_g', 'new_m_w_in', 'new_m_lambda_re', 'new_m_lambda_im', 'new_m_log_dt', 'new_m_b_re', 'new_m_b_im', 'new_m_c_re', 'new_m_c_im', 'new_m_d_skip', 'new_m_w_glu', 'new_m_b_glu', 'new_m_w_branch_s', 'new_m_w_branch_a', 'new_m_w_out', 'new_m_post_norm_g', 'new_v_pre_norm_g', 'new_v_w_in', 'new_v_lambda_re', 'new_v_lambda_im', 'new_v_log_dt', 'new_v_b_re', 'new_v_b_im', 'new_v_c_re', 'new_v_c_im', 'new_v_d_skip', 'new_v_w_glu', 'new_v_b_glu', 'new_v_w_branch_s', 'new_v_w_branch_a', 'new_v_w_out', 'new_v_post_norm_g']
TWIN_LEAF_KINDS = {'loss': 'loss', 'grad_x': 'grad_x', 'grad_pre_norm_g': 'grad_w', 'grad_w_in': 'grad_w', 'grad_lambda_re': 'grad_w', 'grad_lambda_im': 'grad_w', 'grad_log_dt': 'grad_w', 'grad_b_re': 'grad_w', 'grad_b_im': 'grad_w', 'grad_c_re': 'grad_w', 'grad_c_im': 'grad_w', 'grad_d_skip': 'grad_w', 'grad_w_glu': 'grad_w', 'grad_b_glu': 'grad_w', 'grad_w_branch_s': 'grad_w', 'grad_w_branch_a': 'grad_w', 'grad_w_out': 'grad_w', 'grad_post_norm_g': 'grad_w', 'delta_pre_norm_g': 'delta_w', 'delta_w_in': 'delta_w', 'delta_lambda_re': 'delta_w', 'delta_lambda_im': 'delta_w', 'delta_log_dt': 'delta_w', 'delta_b_re': 'delta_w', 'delta_b_im': 'delta_w', 'delta_c_re': 'delta_w', 'delta_c_im': 'delta_w', 'delta_d_skip': 'delta_w', 'delta_w_glu': 'delta_w', 'delta_b_glu': 'delta_w', 'delta_w_branch_s': 'delta_w', 'delta_w_branch_a': 'delta_w', 'delta_w_out': 'delta_w', 'delta_post_norm_g': 'delta_w', 'new_m_pre_norm_g': 'new_m', 'new_m_w_in': 'new_m', 'new_m_lambda_re': 'new_m', 'new_m_lambda_im': 'new_m', 'new_m_log_dt': 'new_m', 'new_m_b_re': 'new_m', 'new_m_b_im': 'new_m', 'new_m_c_re': 'new_m', 'new_m_c_im': 'new_m', 'new_m_d_skip': 'new_m', 'new_m_w_glu': 'new_m', 'new_m_b_glu': 'new_m', 'new_m_w_branch_s': 'new_m', 'new_m_w_branch_a': 'new_m', 'new_m_w_out': 'new_m', 'new_m_post_norm_g': 'new_m', 'new_v_pre_norm_g': 'new_v', 'new_v_w_in': 'new_v', 'new_v_lambda_re': 'new_v', 'new_v_lambda_im': 'new_v', 'new_v_log_dt': 'new_v', 'new_v_b_re': 'new_v', 'new_v_b_im': 'new_v', 'new_v_c_re': 'new_v', 'new_v_c_im': 'new_v', 'new_v_d_skip': 'new_v', 'new_v_w_glu': 'new_v', 'new_v_b_glu': 'new_v', 'new_v_w_branch_s': 'new_v', 'new_v_w_branch_a': 'new_v', 'new_v_w_out': 'new_v', 'new_v_post_norm_g': 'new_v'}


def _forward(args):
    return _fwd_reference(*[args[k] for k in FWD_PARAMS])


def _output_shape():
    out = _jax.eval_shape(lambda: _forward(_fwd_setup_inputs(0)))
    return out.shape, out.dtype

N_MICROBATCH = 1
ADAM_LR = 0.001
ADAM_B1 = 0.9
ADAM_B2 = 0.999
ADAM_EPS = 1e-08
ADAM_WD = 0.01
ADAM_STEP = 10
PER_EXAMPLE_BATCH_AXIS = {'x': 0, 'loss_target': 0}
SHARED_INPUTS = []
_WEIGHT_DTYPES = {'pre_norm_g': _jnp.float32, 'w_in': _jnp.float32, 'lambda_re': _jnp.float32, 'lambda_im': _jnp.float32, 'log_dt': _jnp.float32, 'b_re': _jnp.float32, 'b_im': _jnp.float32, 'c_re': _jnp.float32, 'c_im': _jnp.float32, 'd_skip': _jnp.float32, 'w_glu': _jnp.float32, 'b_glu': _jnp.float32, 'w_branch_s': _jnp.float32, 'w_branch_a': _jnp.float32, 'w_out': _jnp.float32, 'post_norm_g': _jnp.float32}
MOMENT_SCALE = {'pre_norm_g': 9.267974e-01, 'w_in': 3.264321e-01, 'lambda_re': 4.226707e-02, 'lambda_im': 4.635904e-02, 'log_dt': 3.343302e+01, 'b_re': 2.822857e-02, 'b_im': 2.798125e-02, 'c_re': 5.558426e-02, 'c_im': 5.775638e-02, 'd_skip': 1.315681e+00, 'w_glu': 2.438073e-01, 'b_glu': 5.380552e-01, 'w_branch_s': 9.579136e-01, 'w_branch_a': 2.785224e-01, 'w_out': 9.979738e-01, 'post_norm_g': 1.603588e+01}


def _to_microbatches(a, axis):
    t = _jnp.moveaxis(a, axis, 0)
    t = t.reshape((N_MICROBATCH, t.shape[0] // N_MICROBATCH) + t.shape[1:])
    return _jnp.moveaxis(t, 1, axis + 1)


def setup_inputs(seed: int = 0) -> dict:
    inp = _fwd_setup_inputs(seed)
    key = _jax.random.fold_in(_jax.random.key(seed), 7919)
    shape, _ = _output_shape()
    out = dict(inp)
    out["loss_target"] = _jax.random.normal(_jax.random.fold_in(key, 0), shape, _jnp.float32)
    for i, name in enumerate(TWIN_WEIGHTS):
        w = inp[name].astype(_jnp.float32)
        if MOMENT_SCALE is None:
            s = _jnp.sqrt(_jnp.mean(_jnp.square(w)) + 1e-30)
        else:
            s = MOMENT_SCALE[name]
        km, kv = _jax.random.split(_jax.random.fold_in(key, i + 1))
        out[name] = w
        out["m_" + name] = s * _jax.random.normal(km, w.shape, _jnp.float32)
        out["v_" + name] = (s * s) * _jax.random.uniform(kv, w.shape, _jnp.float32, 0.5, 1.5)
    if N_MICROBATCH > 1:
        for name, axis in PER_EXAMPLE_BATCH_AXIS.items():
            out[name] = _to_microbatches(out[name], axis)
    return {'x': out['x'], 'pre_norm_g': out['pre_norm_g'], 'w_in': out['w_in'], 'lambda_re': out['lambda_re'], 'lambda_im': out['lambda_im'], 'log_dt': out['log_dt'], 'b_re': out['b_re'], 'b_im': out['b_im'], 'c_re': out['c_re'], 'c_im': out['c_im'], 'd_skip': out['d_skip'], 'w_glu': out['w_glu'], 'b_glu': out['b_glu'], 'w_branch_s': out['w_branch_s'], 'w_branch_a': out['w_branch_a'], 'w_out': out['w_out'], 'post_norm_g': out['post_norm_g'], 'loss_target': out['loss_target'], 'm_pre_norm_g': out['m_pre_norm_g'], 'm_w_in': out['m_w_in'], 'm_lambda_re': out['m_lambda_re'], 'm_lambda_im': out['m_lambda_im'], 'm_log_dt': out['m_log_dt'], 'm_b_re': out['m_b_re'], 'm_b_im': out['m_b_im'], 'm_c_re': out['m_c_re'], 'm_c_im': out['m_c_im'], 'm_d_skip': out['m_d_skip'], 'm_w_glu': out['m_w_glu'], 'm_b_glu': out['m_b_glu'], 'm_w_branch_s': out['m_w_branch_s'], 'm_w_branch_a': out['m_w_branch_a'], 'm_w_out': out['m_w_out'], 'm_post_norm_g': out['m_post_norm_g'], 'v_pre_norm_g': out['v_pre_norm_g'], 'v_w_in': out['v_w_in'], 'v_lambda_re': out['v_lambda_re'], 'v_lambda_im': out['v_lambda_im'], 'v_log_dt': out['v_log_dt'], 'v_b_re': out['v_b_re'], 'v_b_im': out['v_b_im'], 'v_c_re': out['v_c_re'], 'v_c_im': out['v_c_im'], 'v_d_skip': out['v_d_skip'], 'v_w_glu': out['v_w_glu'], 'v_b_glu': out['v_b_glu'], 'v_w_branch_s': out['v_w_branch_s'], 'v_w_branch_a': out['v_w_branch_a'], 'v_w_out': out['v_w_out'], 'v_post_norm_g': out['v_post_norm_g']}


def _loss(weights, diff, rest, loss_target):
    with _jax.named_scope("forward"):
        args = {**rest, TWIN_DIFF_INPUT: diff, **{k: w.astype(_WEIGHT_DTYPES[k]) for k, w in weights.items()}}
        y = _forward(args)
    with _jax.named_scope("loss_head"):
        err = _jnp.square(y.astype(_jnp.float32) - loss_target)
        return 0.5 * _jnp.sum(_jnp.mean(err, axis=-1)) if err.ndim else 0.5 * err


def _adamw(w, g, m, v):
    m = ADAM_B1 * m + (1.0 - ADAM_B1) * g
    v = ADAM_B2 * v + (1.0 - ADAM_B2) * _jnp.square(g)
    m_hat = m / (1.0 - ADAM_B1 ** ADAM_STEP)
    v_hat = v / (1.0 - ADAM_B2 ** ADAM_STEP)
    delta = -ADAM_LR * (m_hat / (_jnp.sqrt(v_hat) + ADAM_EPS) + ADAM_WD * w)
    return delta, m, v


def reference(x, pre_norm_g, w_in, lambda_re, lambda_im, log_dt, b_re, b_im, c_re, c_im, d_skip, w_glu, b_glu, w_branch_s, w_branch_a, w_out, post_norm_g, loss_target, m_pre_norm_g, m_w_in, m_lambda_re, m_lambda_im, m_log_dt, m_b_re, m_b_im, m_c_re, m_c_im, m_d_skip, m_w_glu, m_b_glu, m_w_branch_s, m_w_branch_a, m_w_out, m_post_norm_g, v_pre_norm_g, v_w_in, v_lambda_re, v_lambda_im, v_log_dt, v_b_re, v_b_im, v_c_re, v_c_im, v_d_skip, v_w_glu, v_b_glu, v_w_branch_s, v_w_branch_a, v_w_out, v_post_norm_g):
    given = dict(x=x, pre_norm_g=pre_norm_g, w_in=w_in, lambda_re=lambda_re, lambda_im=lambda_im, log_dt=log_dt, b_re=b_re, b_im=b_im, c_re=c_re, c_im=c_im, d_skip=d_skip, w_glu=w_glu, b_glu=b_glu, w_branch_s=w_branch_s, w_branch_a=w_branch_a, w_out=w_out, post_norm_g=post_norm_g, loss_target=loss_target, m_pre_norm_g=m_pre_norm_g, m_w_in=m_w_in, m_lambda_re=m_lambda_re, m_lambda_im=m_lambda_im, m_log_dt=m_log_dt, m_b_re=m_b_re, m_b_im=m_b_im, m_c_re=m_c_re, m_c_im=m_c_im, m_d_skip=m_d_skip, m_w_glu=m_w_glu, m_b_glu=m_b_glu, m_w_branch_s=m_w_branch_s, m_w_branch_a=m_w_branch_a, m_w_out=m_w_out, m_post_norm_g=m_post_norm_g, v_pre_norm_g=v_pre_norm_g, v_w_in=v_w_in, v_lambda_re=v_lambda_re, v_lambda_im=v_lambda_im, v_log_dt=v_log_dt, v_b_re=v_b_re, v_b_im=v_b_im, v_c_re=v_c_re, v_c_im=v_c_im, v_d_skip=v_d_skip, v_w_glu=v_w_glu, v_b_glu=v_b_glu, v_w_branch_s=v_w_branch_s, v_w_branch_a=v_w_branch_a, v_w_out=v_w_out, v_post_norm_g=v_post_norm_g)
    weights = {n: given[n] for n in TWIN_WEIGHTS}
    shared = {n: given[n] for n in SHARED_INPUTS}
    per_example = {n: given[n] for n in ['x']}
    grad_fn = _jax.value_and_grad(_loss, argnums=(0, 1))

    def one_microbatch(ex, loss_target):
        ex = dict(ex)
        diff = ex.pop(TWIN_DIFF_INPUT)
        return grad_fn(weights, diff, {**shared, **ex}, loss_target)

    if N_MICROBATCH == 1:
        loss, (grad_w, grad_x) = one_microbatch(per_example, given["loss_target"])
    else:
        def body(carry, xs):
            loss_sum, grad_sum = carry
            l_k, (gw_k, gx_k) = one_microbatch(xs[0], xs[1])
            with _jax.named_scope("update"):
                return (loss_sum + l_k, _jax.tree.map(_jnp.add, grad_sum, gw_k)), gx_k

        init = (_jnp.zeros((), _jnp.float32), _jax.tree.map(_jnp.zeros_like, weights))
        (loss, grad_w), grad_x = _jax.lax.scan(body, init, (per_example, given["loss_target"]))
    with _jax.named_scope("update"):
        delta_w, new_m, new_v = {}, {}, {}
        for n in TWIN_WEIGHTS:
            delta_w[n], new_m[n], new_v[n] = _adamw(weights[n], grad_w[n], given["m_" + n], given["v_" + n])
    return (loss, grad_x, *[grad_w[n] for n in TWIN_WEIGHTS], *[delta_w[n] for n in TWIN_WEIGHTS],
            *[new_m[n] for n in TWIN_WEIGHTS], *[new_v[n] for n in TWIN_WEIGHTS])
```

```python
import math

import jax
import jax.numpy as jnp
from jax import lax
from jax.experimental import pallas as pl
from jax.experimental.pallas import tpu as pltpu

F32 = jnp.float32
BF16 = jnp.bfloat16
MESH = pl.DeviceIdType.MESH

DEPTH = 4
L = 2048
D = 1024
NCOL = 8192
SSM = 512
AW = 512
QKV = 1536
RMS_EPS = 1e-6
NCHIP = 4

COL_U, COL_ZS, COL_Q, COL_K, COL_V, COL_ZA, COL_GS, COL_GA = 0, 4, 8, 20, 32, 44, 48, 56

NSEG = 8
SEG = L // NSEG
NCB = 4
HW = 512
SW = 2 * HW

HEAD = 128
DILATIONS = (1, 4, 16)
SCALE = HEAD ** -0.5
NEG = -1e30

ADAM_LR, ADAM_B1, ADAM_B2, ADAM_EPS, ADAM_WD, ADAM_STEP = 0.001, 0.9, 0.999, 1e-08, 0.01, 10


def _sds(shape, dtype=F32):
    return jax.ShapeDtypeStruct(shape, dtype)


def _pcall(body, *, name, out_shape, grid=None, in_specs=None, out_specs=None, scratch=(), vmem_mb=None,
           aliases=None):
    params = {}
    if vmem_mb is not None:
        params["vmem_limit_bytes"] = vmem_mb << 20
    kw = {}
    if grid is not None:
        kw["grid"] = grid
    if in_specs is not None:
        kw["in_specs"] = in_specs
    if out_specs is not None:
        kw["out_specs"] = out_specs
    return pl.pallas_call(body, name=name, out_shape=out_shape, scratch_shapes=list(scratch),
                          compiler_params=pltpu.CompilerParams(**params),
                          input_output_aliases=aliases or {}, **kw)


def _dot(a, b):
    return jnp.dot(a, b, preferred_element_type=F32)


def _dot_nt(a, b):
    return lax.dot_general(a, b, (((1,), (1,)), ((), ())), preferred_element_type=F32)


def _dot_tn(a, b):
    return lax.dot_general(a, b, (((0,), (0,)), ((), ())), preferred_element_type=F32)


def _sigmoid(x):
    return jax.nn.sigmoid(x)


_GELU_K = math.sqrt(2.0 / math.pi)


def _gelu(x):
    return 0.5 * x * (1.0 + jnp.tanh(_GELU_K * (x + 0.044715 * (x * x * x))))


def _gelu_grad(x):
    t = jnp.tanh(_GELU_K * (x + 0.044715 * (x * x * x)))
    return 0.5 * (1.0 + t) + 0.5 * x * (1.0 - t * t) * (_GELU_K * (1.0 + 3.0 * 0.044715 * (x * x)))


def _rms_proj_fwd(x, g, w4):
    tl, tn = 512, 1024

    def body(x_ref, g_ref, w_ref, o_ref):
        xx = x_ref[...]
        inv = lax.rsqrt(jnp.mean(xx * xx, axis=-1, keepdims=True) + RMS_EPS)
        h = (xx * inv * g_ref[...]).astype(BF16)
        o_ref[...] = _dot(h, w_ref[...])

    return _pcall(
        body, name="rms_proj_fwd", grid=(NCOL // tn, L // tl),
        in_specs=[pl.BlockSpec((tl, D), lambda j, i: (i, 0)),
                  pl.BlockSpec((1, D), lambda j, i: (0, 0)),
                  pl.BlockSpec((None, D, tn), lambda j, i: (lax.div(j, 2), 0, lax.rem(j, 2)))],
        out_specs=pl.BlockSpec((tl, tn), lambda j, i: (i, j)),
        out_shape=_sds((L, NCOL)), vmem_mb=40)(x, g, w4)


def _proj_bwd(dproj, x, g, w4):
    tl, tn = 512, 1024
    ni = L // tl

    def body(dp_ref, x_ref, g_ref, w_ref, dh_ref, dw_ref, acc_ref):
        j, i = pl.program_id(0), pl.program_id(1)
        xx = x_ref[...]
        inv = lax.rsqrt(jnp.mean(xx * xx, axis=-1, keepdims=True) + RMS_EPS)
        h = (xx * inv * g_ref[...]).astype(BF16)
        dp = dp_ref[...].astype(BF16)
        part = _dot_tn(h, dp)

        @pl.when(i == 0)
        def _():
            acc_ref[...] = part

        @pl.when(i > 0)
        def _():
            acc_ref[...] += part

        @pl.when(i == ni - 1)
        def _():
            dw_ref[...] = acc_ref[...].astype(BF16)

        rows = pl.ds(pl.multiple_of(i * tl, tl), tl)
        dh = _dot_nt(dp, w_ref[...])

        @pl.when(j == 0)
        def _():
            dh_ref[rows, :] = dh

        @pl.when(j > 0)
        def _():
            dh_ref[rows, :] += dh

    return _pcall(
        body, name="proj_bwd", grid=(NCOL // tn, ni),
        in_specs=[pl.BlockSpec((tl, tn), lambda j, i: (i, j)),
                  pl.BlockSpec((tl, D), lambda j, i: (i, 0)),
                  pl.BlockSpec((1, D), lambda j, i: (0, 0)),
                  pl.BlockSpec((None, D, tn), lambda j, i: (lax.div(j, 2), 0, lax.rem(j, 2)))],
        out_specs=[pl.BlockSpec((L, D), lambda j, i: (0, 0)),
                   pl.BlockSpec((None, D, tn), lambda j, i: (lax.div(j, 2), 0, lax.rem(j, 2)))],
        out_shape=[_sds((L, D)), _sds((NCHIP, D, NCOL // NCHIP), BF16)],
        scratch=[pltpu.VMEM((D, tn), F32)], vmem_mb=56)(dproj, x, g, w4)


def _rms_bwd(dh, x, g, dxn):
    tl = 512

    def body(dh_ref, x_ref, g_ref, dxn_ref, dx_ref, dg_ref):
        i = pl.program_id(0)
        xx = x_ref[...]
        inv = lax.rsqrt(jnp.mean(xx * xx, axis=-1, keepdims=True) + RMS_EPS)
        nrm = xx * inv
        dh_v = dh_ref[...]
        dn = dh_v * g_ref[...]
        dx_ref[...] = dxn_ref[...] + inv * (dn - nrm * jnp.mean(dn * nrm, axis=-1, keepdims=True))
        part = jnp.sum(dh_v * nrm, axis=0, keepdims=True)

        @pl.when(i == 0)
        def _():
            dg_ref[...] = part

        @pl.when(i > 0)
        def _():
            dg_ref[...] += part

    row = pl.BlockSpec((tl, D), lambda i: (i, 0))
    vec = pl.BlockSpec((1, D), lambda i: (0, 0))
    return _pcall(body, name="rms_bwd", grid=(L // tl,), in_specs=[row, row, vec, row],
                  out_specs=[row, vec], out_shape=[_sds((L, D)), _sds((1, D))], vmem_mb=40)(dh, x, g, dxn)


def _s5_param_math(lre, lim, ldt, bre, bim):
    lr = jnp.minimum(lre, -1e-4)
    dt = jnp.exp(ldt)
    mag = jnp.exp(lr * dt)
    ar = mag * jnp.cos(lim * dt)
    ai = mag * jnp.sin(lim * dt)
    den = lr * lr + lim * lim
    nr = ar - 1.0
    qr = (nr * lr + ai * lim) / den
    qi = (ai * lr - nr * lim) / den
    return ar, ai, qr * bre - qi * bim, qr * bim + qi * bre


def _s5_prep(lre, lim, ldt, bre, bim):
    n = lre.shape[0]

    def body(lre_ref, lim_ref, ldt_ref, bre_ref, bim_ref, ar_ref, ai_ref, bbr_ref, bbi_ref):
        ar, ai, bbr, bbi = _s5_param_math(lre_ref[...], lim_ref[...], ldt_ref[...], bre_ref[...], bim_ref[...])
        ar_ref[...] = ar
        ai_ref[...] = ai
        bbr_ref[...] = bbr
        bbi_ref[...] = bbi

    return _pcall(body, name="s5_prep",
                  out_shape=[_sds((n, 1, 64)), _sds((n, 1, 64)), _sds((n, 16, 64)), _sds((n, 16, 64))])(
                      lre, lim, ldt, bre, bim)


def _s5_prep_bwd(lre, lim, ldt, bre, bim, gar, gai, gbbr, gbbi):
    n = lre.shape[0]

    def body(lre_ref, lim_ref, ldt_ref, bre_ref, bim_ref, gar_ref, gai_ref, gbbr_ref, gbbi_ref,
             dlre_ref, dlim_ref, dldt_ref, dbre_ref, dbim_ref):
        _, vjp = jax.vjp(_s5_param_math, lre_ref[...], lim_ref[...], ldt_ref[...], bre_ref[...], bim_ref[...])
        dlre, dlim, dldt, dbre, dbim = vjp((gar_ref[...], gai_ref[...], gbbr_ref[...], gbbi_ref[...]))
        dlre_ref[...] = dlre
        dlim_ref[...] = dlim
        dldt_ref[...] = dldt
        dbre_ref[...] = dbre
        dbim_ref[...] = dbim

    return _pcall(body, name="s5_prep_bwd",
                  out_shape=[_sds((n, 1, 64)), _sds((n, 1, 64)), _sds((n, 1, 1)), _sds((n, 16, 64)),
                             _sds((n, 16, 64))])(lre, lim, ldt, bre, bim, gar, gai, gbbr, gbbi)


def _permute_rows(dst_ref, src_ref):
    for s in range(NSEG):
        dst_ref[pl.ds(s, SEG, stride=NSEG), :] = src_ref[pl.ds(s * SEG, SEG), :]


def _unpermute_rows(src_ref, s):
    return src_ref[pl.ds(s, SEG, stride=NSEG), :]


def _row_of(slab, rows, j):
    return jnp.sum(jnp.where(rows == j, slab, 0.0), axis=0, keepdims=True)


def _s5_scan_fwd(proj, wb4, wc4, ar4, ai4):
    def body(u_ref, wb_ref, wc_ref, ar_ref, ai_ref, x_ref, y_ref, up_ref, d_ref, yp_ref):
        _permute_rows(up_ref, u_ref)
        d_ref[...] = _dot(up_ref[...].astype(BF16), wb_ref[...])
        ar = jnp.broadcast_to(ar_ref[...], (NSEG, HW))
        ai = jnp.broadcast_to(ai_ref[...], (NSEG, HW))
        zero = jnp.zeros((NSEG, HW), F32)

        def drive(t):
            row = pl.multiple_of(t * NSEG, NSEG)
            dd = d_ref[pl.ds(row, NSEG), :]
            return row, dd[:, :HW], dd[:, HW:]

        def local_step(t, c):
            xr, xi = c
            _, dr, di = drive(t)
            return ar * xr - ai * xi + dr, ar * xi + ai * xr + di

        fr, fi = lax.fori_loop(0, SEG, local_step, (zero, zero), unroll=8)
        pr, pi_ = ar_ref[...], ai_ref[...]
        for _ in range(int(math.log2(SEG))):
            pr, pi_ = pr * pr - pi_ * pi_, 2.0 * pr * pi_
        rows = lax.broadcasted_iota(jnp.int32, (NSEG, HW), 0)
        cr, ci = zero, zero
        for j in range(NSEG - 1):
            sr, si = _row_of(cr, rows, j), _row_of(ci, rows, j)
            gr, gi = _row_of(fr, rows, j), _row_of(fi, rows, j)
            nr = pr * sr - pi_ * si + gr
            ni = pr * si + pi_ * sr + gi
            cr = jnp.where(rows == j + 1, nr, cr)
            ci = jnp.where(rows == j + 1, ni, ci)

        def true_step(t, c):
            xr, xi = c
            row, dr, di = drive(t)
            nr = ar * xr - ai * xi + dr
            ni = ar * xi + ai * xr + di
            x_ref[pl.ds(row, NSEG), :] = jnp.concatenate([nr, ni], axis=1)
            return nr, ni

        lax.fori_loop(0, SEG, true_step, (cr, ci), unroll=8)
        yp_ref[...] = _dot(x_ref[...].astype(BF16), wc_ref[...])
        for s in range(NSEG):
            y_ref[pl.ds(s * SEG, SEG), :] = _unpermute_rows(yp_ref, s)

    return _pcall(
        body, name="s5_scan_fwd", grid=(NCB,),
        in_specs=[pl.BlockSpec((L, 128), lambda b: (0, COL_U + b)),
                  pl.BlockSpec((None, 128, SW), lambda b: (b, 0, 0)),
                  pl.BlockSpec((None, SW, 128), lambda b: (b, 0, 0)),
                  pl.BlockSpec((None, 1, HW), lambda b: (b, 0, 0)),
                  pl.BlockSpec((None, 1, HW), lambda b: (b, 0, 0))],
        out_specs=[pl.BlockSpec((None, L, SW), lambda b: (b, 0, 0)),
                   pl.BlockSpec((L, 128), lambda b: (0, b))],
        out_shape=[_sds((NCB, L, SW)), _sds((L, SSM))],
        scratch=[pltpu.VMEM((L, 128), F32), pltpu.VMEM((L, SW), F32), pltpu.VMEM((L, 128), F32)],
        vmem_mb=56)(proj, wb4, wc4, ar4, ai4)


def _s5_scan_bwd(dy, xs, proj, du_skip, wb4, wc4, ar4, ai4):
    def body(dy_ref, x_ref, u_ref, dus_ref, wb_ref, wc_ref, ar_ref, ai_ref,
             du_ref, dwb_ref, dwc_ref, da_ref, dyp_ref, up_ref, g_ref, dup_ref):
        _permute_rows(dyp_ref, dy_ref)
        _permute_rows(up_ref, u_ref)
        dyp = dyp_ref[...].astype(BF16)
        g_ref[...] = _dot_nt(dyp, wc_ref[...])
        dwc_ref[...] = _dot_tn(x_ref[...].astype(BF16), dyp)
        ar = jnp.broadcast_to(ar_ref[...], (NSEG, HW))
        ai = jnp.broadcast_to(ai_ref[...], (NSEG, HW))
        zero = jnp.zeros((NSEG, HW), F32)

        def load(t):
            row = pl.multiple_of(t * NSEG, NSEG)
            gg = g_ref[pl.ds(row, NSEG), :]
            return row, gg[:, :HW], gg[:, HW:]

        def local_step(k, c):
            lr, li = c
            _, gr, gi = load(SEG - 1 - k)
            return ar * lr + ai * li + gr, ar * li - ai * lr + gi

        fr, fi = lax.fori_loop(0, SEG, local_step, (zero, zero), unroll=8)
        pr, pi_ = ar_ref[...], -ai_ref[...]
        for _ in range(int(math.log2(SEG))):
            pr, pi_ = pr * pr - pi_ * pi_, 2.0 * pr * pi_
        rows = lax.broadcasted_iota(jnp.int32, (NSEG, HW), 0)
        cr, ci = zero, zero
        for j in range(NSEG - 1, 0, -1):
            sr, si = _row_of(cr, rows, j), _row_of(ci, rows, j)
            gr, gi = _row_of(fr, rows, j), _row_of(fi, rows, j)
            nr = pr * sr - pi_ * si + gr
            ni = pr * si + pi_ * sr + gi
            cr = jnp.where(rows == j - 1, nr, cr)
            ci = jnp.where(rows == j - 1, ni, ci)

        def true_step(k, c):
            lr, li, gar, gai = c
            t = SEG - 1 - k
            row, gr, gi = load(t)
            nr = ar * lr + ai * li + gr
            ni = ar * li - ai * lr + gi
            g_ref[pl.ds(row, NSEG), :] = jnp.concatenate([nr, ni], axis=1)
            prow = pl.multiple_of(jnp.maximum(t - 1, 0) * NSEG, NSEG)
            xp = x_ref[pl.ds(prow, NSEG), :]
            live = (t > 0).astype(F32)
            xr, xi = xp[:, :HW] * live, xp[:, HW:] * live
            return nr, ni, gar + (nr * xr + ni * xi), gai + (ni * xr - nr * xi)

        _, _, gar, gai = lax.fori_loop(0, SEG, true_step, (cr, ci, zero, zero), unroll=4)
        l0 = g_ref[1:NSEG, :]
        xl = x_ref[(SEG - 1) * NSEG:(SEG - 1) * NSEG + NSEG - 1, :]
        l0r, l0i, xlr, xli = l0[:, :HW], l0[:, HW:], xl[:, :HW], xl[:, HW:]
        gar_t = jnp.sum(gar, axis=0, keepdims=True) + jnp.sum(l0r * xlr + l0i * xli, axis=0, keepdims=True)
        gai_t = jnp.sum(gai, axis=0, keepdims=True) + jnp.sum(l0i * xlr - l0r * xli, axis=0, keepdims=True)
        da_ref[...] = jnp.concatenate([gar_t, gai_t], axis=1)
        lam = g_ref[...].astype(BF16)
        dwb_ref[...] = _dot_tn(up_ref[...].astype(BF16), lam)
        dup_ref[...] = _dot_nt(lam, wb_ref[...])
        for s in range(NSEG):
            sl = pl.ds(s * SEG, SEG)
            du_ref[sl, :] = _unpermute_rows(dup_ref, s) + dus_ref[sl, :]

    col = lambda off: pl.BlockSpec((L, 128), lambda b: (0, off + b))
    return _pcall(
        body, name="s5_scan_bwd", grid=(NCB,),
        in_specs=[col(0), pl.BlockSpec((None, L, SW), lambda b: (b, 0, 0)), col(COL_U), col(0),
                  pl.BlockSpec((None, 128, SW), lambda b: (b, 0, 0)),
                  pl.BlockSpec((None, SW, 128), lambda b: (b, 0, 0)),
                  pl.BlockSpec((None, 1, HW), lambda b: (b, 0, 0)),
                  pl.BlockSpec((None, 1, HW), lambda b: (b, 0, 0))],
        out_specs=[col(0), pl.BlockSpec((None, 128, SW), lambda b: (b, 0, 0)),
                   pl.BlockSpec((None, SW, 128), lambda b: (b, 0, 0)),
                   pl.BlockSpec((None, 1, SW), lambda b: (b, 0, 0))],
        out_shape=[_sds((L, SSM)), _sds((NCB, 128, SW)), _sds((NCB, SW, 128)), _sds((NCB, 1, SW))],
        scratch=[pltpu.VMEM((L, 128), F32), pltpu.VMEM((L, 128), F32), pltpu.VMEM((L, SW), F32),
                 pltpu.VMEM((L, 128), F32)],
        vmem_mb=56)(dy, xs, proj, du_skip, wb4, wc4, ar4, ai4)


def _s5_tail_fwd(yraw, proj, dsk, wglu, bglu):
    tl = 512

    def body(y_ref, u_ref, z_ref, dsk_ref, w_ref, b_ref, o_ref):
        y1 = y_ref[...] + dsk_ref[...] * u_ref[...]
        y2 = _gelu(y1)
        gl = _dot(y2.astype(BF16), w_ref[...]) + b_ref[...]
        z = z_ref[...]
        o_ref[...] = (y2 * _sigmoid(gl)) * (z * _sigmoid(z))

    blk = lambda c: pl.BlockSpec((tl, SSM), lambda i: (i, c))
    vec = pl.BlockSpec((1, SSM), lambda i: (0, 0))
    return _pcall(body, name="s5_tail_fwd", grid=(L // tl,),
                  in_specs=[blk(0), blk(0), blk(1), vec, pl.BlockSpec((SSM, SSM), lambda i: (0, 0)), vec],
                  out_specs=blk(0), out_shape=_sds((L, SSM)), vmem_mb=40)(yraw, proj, proj, dsk, wglu, bglu)


def _s5_tail_bwd(dys, yraw, proj, dsk, wglu, bglu):
    tl = 512
    ni = L // tl

    def body(dys_ref, y_ref, u_ref, z_ref, dsk_ref, w_ref, b_ref,
             dy_ref, dus_ref, dz_ref, dw_ref, db_ref, dd_ref, acc_ref):
        i = pl.program_id(0)
        u = u_ref[...]
        y1 = y_ref[...] + dsk_ref[...] * u
        y2 = _gelu(y1)
        y2b = y2.astype(BF16)
        sg = _sigmoid(_dot(y2b, w_ref[...]) + b_ref[...])
        y3 = y2 * sg
        z = z_ref[...]
        sz = _sigmoid(z)
        dys = dys_ref[...]
        dy3 = dys * (z * sz)
        dz_ref[...] = dys * y3 * (sz * (1.0 + z * (1.0 - sz)))
        dgl = (dy3 * y2) * (sg * (1.0 - sg))
        dglb = dgl.astype(BF16)
        dy2 = dy3 * sg + _dot_nt(dglb, w_ref[...])
        dy1 = dy2 * _gelu_grad(y1)
        dy_ref[...] = dy1
        dus_ref[...] = dsk_ref[...] * dy1
        dw = _dot_tn(y2b, dglb)
        db = jnp.sum(dgl, axis=0, keepdims=True)
        dd = jnp.sum(dy1 * u, axis=0, keepdims=True)

        @pl.when(i == 0)
        def _():
            acc_ref[...] = dw
            db_ref[...] = db
            dd_ref[...] = dd

        @pl.when(i > 0)
        def _():
            acc_ref[...] += dw
            db_ref[...] += db
            dd_ref[...] += dd

        @pl.when(i == ni - 1)
        def _():
            dw_ref[...] = acc_ref[...].astype(BF16)

    blk = lambda c: pl.BlockSpec((tl, SSM), lambda i: (i, c))
    vec = pl.BlockSpec((1, SSM), lambda i: (0, 0))
    mat = pl.BlockSpec((SSM, SSM), lambda i: (0, 0))
    return _pcall(body, name="s5_tail_bwd", grid=(ni,),
                  in_specs=[blk(0), blk(0), blk(0), blk(1), vec, mat, vec],
                  out_specs=[blk(0), blk(0), blk(0), mat, vec, vec],
                  out_shape=[_sds((L, SSM)), _sds((L, SSM)), _sds((L, SSM)), _sds((SSM, SSM), BF16),
                             _sds((1, SSM)), _sds((1, SSM))],
                  scratch=[pltpu.VMEM((SSM, SSM), F32)], vmem_mb=40)(dys, yraw, proj, proj, dsk, wglu, bglu)


def _attn_blocks(dil):
    nb = L // dil // HEAD
    return [(i * HEAD * dil + r, (i - 1) * HEAD * dil + r if i > 0 else None)
            for r in range(dil) for i in range(nb)]


def _rows(start, dil):
    return pl.ds(start, HEAD) if dil == 1 else pl.ds(start, HEAD, stride=dil)


def _band_masks():
    ri = lax.broadcasted_iota(jnp.int32, (HEAD, HEAD), 0)
    ci = lax.broadcasted_iota(jnp.int32, (HEAD, HEAD), 1)
    return ci <= ri, ci >= ri


def _qkv_specs(index):
    def spec(off):
        return pl.BlockSpec((L, HEAD), lambda gi, h: (0, off + index(gi, h)))
    return [spec(off) for off in (COL_Q, COL_K, COL_V)]


def _attn_core_fwd(proj):
    def body(q_ref, k_ref, v_ref, o_ref, lse_ref):
        gi = pl.program_id(0)
        m_cur, m_prev = _band_masks()
        for g, dil in enumerate(DILATIONS):
            @pl.when(gi == g)
            def _(dil=dil):
                for st, pst in _attn_blocks(dil):
                    r = _rows(st, dil)
                    q = q_ref[r, :].astype(BF16)
                    s_c = jnp.where(m_cur, _dot_nt(q, k_ref[r, :].astype(BF16)) * SCALE, NEG)
                    mx = jnp.max(s_c, axis=-1, keepdims=True)
                    if pst is not None:
                        rp = _rows(pst, dil)
                        s_p = jnp.where(m_prev, _dot_nt(q, k_ref[rp, :].astype(BF16)) * SCALE, NEG)
                        mx = jnp.maximum(mx, jnp.max(s_p, axis=-1, keepdims=True))
                    p_c = jnp.exp(s_c - mx)
                    den = jnp.sum(p_c, axis=-1, keepdims=True)
                    acc = _dot(p_c.astype(BF16), v_ref[r, :].astype(BF16))
                    if pst is not None:
                        p_p = jnp.exp(s_p - mx)
                        den = den + jnp.sum(p_p, axis=-1, keepdims=True)
                        acc = acc + _dot(p_p.astype(BF16), v_ref[rp, :].astype(BF16))
                    o_ref[r, :] = acc / den
                    lse_ref[r, :] = jnp.broadcast_to(mx + jnp.log(den), (HEAD, HEAD))

    idx = lambda gi, h: gi * 4 + h
    out = pl.BlockSpec((L, HEAD), lambda gi, h: (0, gi * 4 + h))
    return _pcall(body, name="attn_core_fwd", grid=(3, 4), in_specs=_qkv_specs(idx), out_specs=[out, out],
                  out_shape=[_sds((L, QKV)), _sds((L, QKV))], vmem_mb=40)(proj, proj, proj)


def _attn_mix_fwd(o, lse, proj):
    tl = 512

    def body(o_ref, l_ref, z_ref, y_ref):
        for h in range(4):
            c = [pl.ds((g * 4 + h) * HEAD, HEAD) for g in range(3)]
            ls = [l_ref[:, c[g]] for g in range(3)]
            m = jnp.maximum(jnp.maximum(ls[0], ls[1]), ls[2])
            e = [jnp.exp(ls[g] - m) for g in range(3)]
            y = (e[0] * o_ref[:, c[0]] + e[1] * o_ref[:, c[1]] + e[2] * o_ref[:, c[2]]) / (e[0] + e[1] + e[2])
            z = z_ref[:, pl.ds(h * HEAD, HEAD)]
            y_ref[:, pl.ds(h * HEAD, HEAD)] = y * (z * _sigmoid(z))

    wide = pl.BlockSpec((tl, QKV), lambda i: (i, 0))
    return _pcall(body, name="attn_mix_fwd", grid=(L // tl,),
                  in_specs=[wide, wide, pl.BlockSpec((tl, AW), lambda i: (i, COL_ZA // 4))],
                  out_specs=pl.BlockSpec((tl, AW), lambda i: (i, 0)), out_shape=_sds((L, AW)),
                  vmem_mb=40)(o, lse, proj)


def _attn_mix_bwd(dya, o, lse, proj):
    tl = 512

    def body(dya_ref, o_ref, l_ref, z_ref, do_ref, c_ref, dz_ref):
        for h in range(4):
            c = [pl.ds((g * 4 + h) * HEAD, HEAD) for g in range(3)]
            hs = pl.ds(h * HEAD, HEAD)
            ls = [l_ref[:, c[g]] for g in range(3)]
            m = jnp.maximum(jnp.maximum(ls[0], ls[1]), ls[2])
            e = [jnp.exp(ls[g] - m) for g in range(3)]
            den = e[0] + e[1] + e[2]
            al = [e[g] / den for g in range(3)]
            y = al[0] * o_ref[:, c[0]] + al[1] * o_ref[:, c[1]] + al[2] * o_ref[:, c[2]]
            z = z_ref[:, hs]
            sz = _sigmoid(z)
            dya = dya_ref[:, hs]
            dz_ref[:, hs] = dya * y * (sz * (1.0 + z * (1.0 - sz)))
            dy = dya * (z * sz)
            tot = jnp.sum(dy * y, axis=-1, keepdims=True)
            for g in range(3):
                do_ref[:, c[g]] = al[g] * dy
                c_ref[:, c[g]] = -(al[g] * tot)

    wide = pl.BlockSpec((tl, QKV), lambda i: (i, 0))
    nar = pl.BlockSpec((tl, AW), lambda i: (i, 0))
    return _pcall(body, name="attn_mix_bwd", grid=(L // tl,),
                  in_specs=[nar, wide, wide, pl.BlockSpec((tl, AW), lambda i: (i, COL_ZA // 4))],
                  out_specs=[wide, wide, nar], out_shape=[_sds((L, QKV)), _sds((L, QKV)), _sds((L, AW))],
                  vmem_mb=48)(dya, o, lse, proj)


def _attn_core_bwd(proj, do, lse, cc):
    def body(q_ref, k_ref, v_ref, do_ref, lse_ref, c_ref, dq_ref, dk_ref, dv_ref):
        gi = pl.program_id(0)
        m_cur, m_prev = _band_masks()
        dk_ref[...] = jnp.zeros((L, HEAD), F32)
        dv_ref[...] = jnp.zeros((L, HEAD), F32)
        for g, dil in enumerate(DILATIONS):
            @pl.when(gi == g)
            def _(dil=dil):
                for st, pst in _attn_blocks(dil):
                    r = _rows(st, dil)
                    q = q_ref[r, :].astype(BF16)
                    do_b = do_ref[r, :].astype(BF16)
                    lse_b = lse_ref[r, :]
                    c_b = c_ref[r, :]
                    dq = None
                    for rr, mask in ((r, m_cur),) + (((_rows(pst, dil), m_prev),) if pst is not None else ()):
                        kb = k_ref[rr, :].astype(BF16)
                        s = _dot_nt(q, kb) * SCALE
                        p = jnp.where(mask, jnp.exp(s - lse_b), 0.0)
                        dp = _dot_nt(do_b, v_ref[rr, :].astype(BF16))
                        ds = (p * (dp + c_b) * SCALE).astype(BF16)
                        part = _dot(ds, kb)
                        dq = part if dq is None else dq + part
                        dk_ref[rr, :] += _dot_tn(ds, q)
                        dv_ref[rr, :] += _dot_tn(p.astype(BF16), do_b)
                    dq_ref[r, :] = dq

    idx = lambda gi, h: gi * 4 + h
    blk = pl.BlockSpec((L, HEAD), lambda gi, h: (0, gi * 4 + h))
    return _pcall(body, name="attn_core_bwd", grid=(3, 4), in_specs=_qkv_specs(idx) + [blk, blk, blk],
                  out_specs=[blk, blk, blk], out_shape=[_sds((L, QKV))] * 3, vmem_mb=48)(
                      proj, proj, proj, do, lse, cc)


def _merge_fwd_math(ys_b, ya_b, gs, ga, wbs_ref, wba_ref, wout_ref):
    bs = jnp.concatenate([_dot(ys_b, wbs_ref[s]) for s in range(NCHIP)], axis=1)
    ba = jnp.concatenate([_dot(ya_b, wba_ref[s]) for s in range(NCHIP)], axis=1)
    sgs, sga = _sigmoid(gs), _sigmoid(ga)
    merged = sgs * bs + sga * ba
    out = _dot(merged.astype(BF16), wout_ref[...])
    inv = lax.rsqrt(jnp.mean(out * out, axis=-1, keepdims=True) + RMS_EPS)
    return bs, ba, sgs, sga, merged, out, inv


def _merge_specs(tl):
    row = lambda w, c: pl.BlockSpec((tl, w), lambda i: (i, c))
    return [row(SSM, 0), row(AW, 0), row(D, COL_GS // 8), row(D, COL_GA // 8),
            pl.BlockSpec((NCHIP, SSM, D // NCHIP), lambda i: (0, 0, 0)),
            pl.BlockSpec((NCHIP, AW, D // NCHIP), lambda i: (0, 0, 0)),
            pl.BlockSpec((D, D), lambda i: (0, 0)),
            pl.BlockSpec((1, D), lambda i: (0, 0))]


def _merge_out_fwd(ys, ya, proj, wbs4, wba4, wout, g2, x):
    tl = 256

    def body(ys_ref, ya_ref, gs_ref, ga_ref, wbs_ref, wba_ref, wout_ref, g2_ref, x_ref, o_ref):
        *_, out, inv = _merge_fwd_math(ys_ref[...].astype(BF16), ya_ref[...].astype(BF16), gs_ref[...],
                                       ga_ref[...], wbs_ref, wba_ref, wout_ref)
        o_ref[...] = x_ref[...] + out * inv * g2_ref[...]

    row = pl.BlockSpec((tl, D), lambda i: (i, 0))
    return _pcall(body, name="merge_out_fwd", grid=(L // tl,), in_specs=_merge_specs(tl) + [row],
                  out_specs=row, out_shape=_sds((L, D)), vmem_mb=48)(ys, ya, proj, proj, wbs4, wba4, wout, g2, x)


def _merge_out_bwd(dxn, ys, ya, proj, wbs4, wba4, wout, g2):
    tl = 256
    ni = L // tl
    cw = D // NCHIP

    def body(dxn_ref, ys_ref, ya_ref, gs_ref, ga_ref, wbs_ref, wba_ref, wout_ref, g2_ref,
             dys_ref, dya_ref, dgs_ref, dga_ref, dwbs_ref, dwba_ref, dwout_ref, dg2_ref,
             abs_ref, aba_ref, aout_ref):
        i = pl.program_id(0)
        ys_b, ya_b = ys_ref[...].astype(BF16), ya_ref[...].astype(BF16)
        bs, ba, sgs, sga, merged, out, inv = _merge_fwd_math(ys_b, ya_b, gs_ref[...], ga_ref[...],
                                                             wbs_ref, wba_ref, wout_ref)
        nrm = out * inv
        dxn = dxn_ref[...]
        dg2 = jnp.sum(dxn * nrm, axis=0, keepdims=True)
        dn = dxn * g2_ref[...]
        dout = (inv * (dn - nrm * jnp.mean(dn * nrm, axis=-1, keepdims=True))).astype(BF16)
        dwout = _dot_tn(merged.astype(BF16), dout)
        dm = _dot_nt(dout, wout_ref[...])
        dbs, dba = dm * sgs, dm * sga
        dgs_ref[...] = dm * bs * (sgs * (1.0 - sgs))
        dga_ref[...] = dm * ba * (sga * (1.0 - sga))
        dbs_b, dba_b = dbs.astype(BF16), dba.astype(BF16)
        dys = None
        dya = None
        first = i == 0
        for s in range(NCHIP):
            cs = slice(s * cw, (s + 1) * cw)
            p_s = _dot_nt(dbs_b[:, cs], wbs_ref[s])
            p_a = _dot_nt(dba_b[:, cs], wba_ref[s])
            dys = p_s if dys is None else dys + p_s
            dya = p_a if dya is None else dya + p_a
            w_s = _dot_tn(ys_b, dbs_b[:, cs])
            w_a = _dot_tn(ya_b, dba_b[:, cs])

            @pl.when(first)
            def _(s=s, w_s=w_s, w_a=w_a):
                abs_ref[s] = w_s
                aba_ref[s] = w_a

            @pl.when(jnp.logical_not(first))
            def _(s=s, w_s=w_s, w_a=w_a):
                abs_ref[s] += w_s
                aba_ref[s] += w_a

        dys_ref[...] = dys
        dya_ref[...] = dya

        @pl.when(first)
        def _():
            aout_ref[...] = dwout
            dg2_ref[...] = dg2

        @pl.when(jnp.logical_not(first))
        def _():
            aout_ref[...] += dwout
            dg2_ref[...] += dg2

        @pl.when(i == ni - 1)
        def _():
            dwbs_ref[...] = abs_ref[...].astype(BF16)
            dwba_ref[...] = aba_ref[...].astype(BF16)
            dwout_ref[...] = aout_ref[...].astype(BF16)

    row = lambda w: pl.BlockSpec((tl, w), lambda i: (i, 0))
    w4 = pl.BlockSpec((NCHIP, SSM, cw), lambda i: (0, 0, 0))
    sq = pl.BlockSpec((D, D), lambda i: (0, 0))
    vec = pl.BlockSpec((1, D), lambda i: (0, 0))
    return _pcall(body, name="merge_out_bwd", grid=(ni,), in_specs=[row(D)] + _merge_specs(tl),
                  out_specs=[row(SSM), row(AW), row(D), row(D), w4, w4, sq, vec],
                  out_shape=[_sds((L, SSM)), _sds((L, AW)), _sds((L, D)), _sds((L, D)),
                             _sds((NCHIP, SSM, cw), BF16), _sds((NCHIP, AW, cw), BF16), _sds((D, D), BF16),
                             _sds((1, D))],
                  scratch=[pltpu.VMEM((NCHIP, SSM, cw), F32), pltpu.VMEM((NCHIP, AW, cw), F32),
                           pltpu.VMEM((D, D), F32)],
                  vmem_mb=56)(dxn, ys, ya, proj, proj, wbs4, wba4, wout, g2)


def _loss_head(y, target):
    tl = 512

    def body(y_ref, t_ref, loss_ref, dy_ref):
        i = pl.program_id(0)
        err = y_ref[...] - t_ref[...]
        dy_ref[...] = err / D
        part = 0.5 * jnp.sum(jnp.mean(err * err, axis=-1, keepdims=True), axis=0, keepdims=True)
        part = jnp.broadcast_to(part, (8, 128))

        @pl.when(i == 0)
        def _():
            loss_ref[...] = part

        @pl.when(i > 0)
        def _():
            loss_ref[...] += part

    row = pl.BlockSpec((tl, D), lambda i: (i, 0))
    return _pcall(body, name="loss_head", grid=(L // tl,), in_specs=[row, row],
                  out_specs=[pl.BlockSpec((8, 128), lambda i: (0, 0)), row],
                  out_shape=[_sds((8, 128)), _sds((L, D))], vmem_mb=40)(y, target)


def _adamw_math(w, g, m, v):
    m = ADAM_B1 * m + (1.0 - ADAM_B1) * g
    v = ADAM_B2 * v + (1.0 - ADAM_B2) * (g * g)
    m_hat = m / (1.0 - ADAM_B1 ** ADAM_STEP)
    v_hat = v / (1.0 - ADAM_B2 ** ADAM_STEP)
    delta = -ADAM_LR * (m_hat / (jnp.sqrt(v_hat) + ADAM_EPS) + ADAM_WD * w)
    return delta, m, v


def _adamw_big(layer, w, m, v, own, sib, prev):
    _, r, c = w.shape
    tr = min(r, 128)

    def body(w_ref, m_ref, v_ref, own_ref, sib_ref, *rest):
        g_ref, d_ref, nm_ref, nv_ref = rest[-4:]
        a = own_ref[0].astype(F32)
        b = sib_ref[0].astype(F32)
        for s in range(1, NCHIP):
            a = a + own_ref[s].astype(F32)
            b = b + sib_ref[s].astype(F32)
        g = a + b
        delta, nm, nv = _adamw_math(w_ref[...], g, m_ref[...], v_ref[...])
        g_ref[...] = g
        d_ref[...] = delta
        nm_ref[...] = nm
        nv_ref[...] = nv

    lay = pl.BlockSpec((None, tr, c), lambda i: (layer, i, 0))
    slots = pl.BlockSpec((NCHIP, tr, c), lambda i: (0, i, 0))
    ins = [w, m, v, own, sib]
    in_specs = [lay, lay, lay, slots, slots]
    aliases = {}
    if prev is not None:
        ins += list(prev)
        in_specs += [pl.BlockSpec(memory_space=pl.ANY)] * 4
        aliases = {5 + k: k for k in range(4)}
    return _pcall(body, name="adamw_big", grid=(r // tr,), in_specs=in_specs, out_specs=[lay] * 4,
                  out_shape=[_sds(w.shape)] * 4, aliases=aliases, vmem_mb=48)(*ins)


def _adamw_small(w, g, m, v):
    r = w.shape[0]

    def body(w_ref, g_ref, m_ref, v_ref, d_ref, nm_ref, nv_ref):
        delta, nm, nv = _adamw_math(w_ref[...], g_ref[...], m_ref[...], v_ref[...])
        d_ref[...] = delta
        nm_ref[...] = nm
        nv_ref[...] = nv

    return _pcall(body, name="adamw_small", out_shape=[_sds((r, 128))] * 3, vmem_mb=40)(w, g, m, v)


def _place():
    x, y, c = lax.axis_index("x"), lax.axis_index("y"), lax.axis_index("c")
    return x, y, c, 2 * x + y


def _chip_peer(x, y, j):
    return (1 - x if j & 2 else x), (1 - y if j & 1 else y)


def _gather_weights(shards):
    n = len(shards)

    def body(*refs):
        ins, outs = refs[:n], refs[n:2 * n]
        send_sems, recv_sems, local_sems = refs[2 * n:]
        x, y, c, s = _place()
        copies = []
        for k in range(n):
            mine = pltpu.make_async_copy(ins[k], outs[k].at[s], local_sems.at[k])
            mine.start()
            copies.append(mine)
        sends = []
        for k in range(n):
            for j in range(1, NCHIP):
                px, py = _chip_peer(x, y, j)
                cp = pltpu.make_async_remote_copy(
                    src_ref=ins[k], dst_ref=outs[k].at[s], send_sem=send_sems.at[k, j - 1],
                    recv_sem=recv_sems.at[k, j - 1], device_id=(px, py, c), device_id_type=MESH)
                cp.start()
                sends.append(cp)
        for k in range(n):
            for j in range(1, NCHIP):
                pltpu.make_async_remote_copy(
                    src_ref=ins[k], dst_ref=outs[k].at[s ^ j], send_sem=send_sems.at[k, j - 1],
                    recv_sem=recv_sems.at[k, j - 1], device_id=(x, y, c), device_id_type=MESH).wait_recv()
        for cp in sends:
            cp.wait_send()
        for cp in copies:
            cp.wait()

    hbm = pl.BlockSpec(memory_space=pl.ANY)
    return _pcall(body, name="gather_weights", in_specs=[hbm] * n, out_specs=[hbm] * n,
                  out_shape=[_sds((NCHIP,) + a.shape, a.dtype) for a in shards],
                  scratch=[pltpu.SemaphoreType.DMA((n, NCHIP - 1)), pltpu.SemaphoreType.DMA((n, NCHIP - 1)),
                           pltpu.SemaphoreType.DMA((n,))])(*shards)


def _exchange_grads(parts):
    n = len(parts)

    def body(*refs):
        ins, own, sib = refs[:n], refs[n:2 * n], refs[2 * n:3 * n]
        ici_send, ici_recv, d2d_send, d2d_recv, local_sems = refs[3 * n:]
        x, y, c, s = _place()
        sibling = (x, y, 1 - c)
        started = []
        for k in range(n):
            mine = pltpu.make_async_copy(ins[k].at[s], own[k].at[s], local_sems.at[k])
            mine.start()
            started.append(mine)
        sends = []
        for k in range(n):
            for j in range(1, NCHIP):
                px, py = _chip_peer(x, y, j)
                cp = pltpu.make_async_remote_copy(
                    src_ref=ins[k].at[s ^ j], dst_ref=own[k].at[s], send_sem=ici_send.at[k, j - 1],
                    recv_sem=ici_recv.at[k, j - 1], device_id=(px, py, c), device_id_type=MESH)
                cp.start()
                sends.append(cp)
            cp = pltpu.make_async_remote_copy(
                src_ref=ins[k].at[s], dst_ref=sib[k].at[s], send_sem=d2d_send.at[k, 0],
                recv_sem=d2d_recv.at[k, 0], device_id=sibling, device_id_type=MESH)
            cp.start()
            sends.append(cp)
        for k in range(n):
            for j in range(1, NCHIP):
                pltpu.make_async_remote_copy(
                    src_ref=ins[k].at[s ^ j], dst_ref=own[k].at[s ^ j], send_sem=ici_send.at[k, j - 1],
                    recv_sem=ici_recv.at[k, j - 1], device_id=sibling, device_id_type=MESH).wait_recv()
                cp = pltpu.make_async_remote_copy(
                    src_ref=own[k].at[s ^ j], dst_ref=sib[k].at[s ^ j], send_sem=d2d_send.at[k, j],
                    recv_sem=d2d_recv.at[k, j], device_id=sibling, device_id_type=MESH)
                cp.start()
                sends.append(cp)
        for k in range(n):
            for j in range(NCHIP):
                pltpu.make_async_remote_copy(
                    src_ref=own[k].at[s ^ j], dst_ref=sib[k].at[s ^ j], send_sem=d2d_send.at[k, j],
                    recv_sem=d2d_recv.at[k, j], device_id=sibling, device_id_type=MESH).wait_recv()
        for cp in sends:
            cp.wait_send()
        for cp in started:
            cp.wait()

    hbm = pl.BlockSpec(memory_space=pl.ANY)
    shapes = [_sds(a.shape, a.dtype) for a in parts]
    return _pcall(body, name="exchange_grads", in_specs=[hbm] * n, out_specs=[hbm] * (2 * n),
                  out_shape=shapes + shapes,
                  scratch=[pltpu.SemaphoreType.DMA((n, NCHIP - 1)), pltpu.SemaphoreType.DMA((n, NCHIP - 1)),
                           pltpu.SemaphoreType.DMA((n, NCHIP)), pltpu.SemaphoreType.DMA((n, NCHIP)),
                           pltpu.SemaphoreType.DMA((n,))])(*parts)


def _allreduce_small(part):
    r = part.shape[0]
    ndev = 8

    def body(in_ref, sum_ref, all_ref, send_sems, recv_sems):
        x, y, c, _ = _place()
        me = 4 * x + 2 * y + c
        all_ref[me] = in_ref[...]
        sends = []
        for j in range(1, ndev):
            px, py = _chip_peer(x, y, j >> 1)
            pc = 1 - c if j & 1 else c
            cp = pltpu.make_async_remote_copy(
                src_ref=in_ref, dst_ref=all_ref.at[me], send_sem=send_sems.at[j - 1],
                recv_sem=recv_sems.at[j - 1], device_id=(px, py, pc), device_id_type=MESH)
            cp.start()
            sends.append(cp)
        for j in range(1, ndev):
            pltpu.make_async_remote_copy(
                src_ref=in_ref, dst_ref=all_ref.at[me ^ j], send_sem=send_sems.at[j - 1],
                recv_sem=recv_sems.at[j - 1], device_id=(x, y, c), device_id_type=MESH).wait_recv()
        for cp in sends:
            cp.wait_send()
        acc = all_ref[0]
        for d in range(1, ndev):
            acc = acc + all_ref[d]
        sum_ref[...] = acc

    vmem = pl.BlockSpec(memory_space=pltpu.VMEM)
    return _pcall(body, name="allreduce_small", in_specs=[vmem], out_specs=vmem, out_shape=_sds((r, 128)),
                  scratch=[pltpu.VMEM((ndev, r, 128), F32), pltpu.SemaphoreType.DMA((ndev - 1,)),
                           pltpu.SemaphoreType.DMA((ndev - 1,))], vmem_mb=48)(part)


def _blockdiag_in(bb):
    eye = jnp.eye(8, dtype=F32)[None, :, None, None, :, None]
    t = bb.reshape(2, NCB, 8, 16, 64).transpose(1, 2, 3, 0, 4)
    return (t[:, :, :, :, None, :] * eye).reshape(NCB, 128, SW)


def _blockdiag_in_t(dw):
    eye = jnp.eye(8, dtype=F32)[None, :, None, None, :, None]
    t = jnp.sum(dw.reshape(NCB, 8, 16, 2, 8, 64) * eye, axis=4)
    return t.transpose(3, 0, 1, 2, 4).reshape(2, 32, 16, 64)


def _blockdiag_out(cc):
    eye = jnp.eye(8, dtype=F32)[None, None, :, None, :, None]
    t = cc.reshape(2, NCB, 8, 16, 64).transpose(1, 0, 4, 2, 3)
    return (t[:, :, None, :, :, :] * eye).reshape(NCB, SW, 128)


def _blockdiag_out_t(dw):
    eye = jnp.eye(8, dtype=F32)[None, None, :, None, :, None]
    t = jnp.sum(dw.reshape(NCB, 2, 8, 64, 8, 16) * eye, axis=2)
    return t.transpose(1, 0, 3, 4, 2).reshape(2, 32, 16, 64)


def _flatten_small(parts):
    flat = jnp.concatenate([p.reshape(-1) for p in parts])
    n = flat.shape[0]
    rows = -(-n // (128 * 8)) * 8
    return jnp.pad(flat, (0, rows * 128 - n)).reshape(rows, 128)


def _unflatten_small(buf, like):
    flat = buf.reshape(-1)
    out, at = [], 0
    for p in like:
        out.append(flat[at:at + p.size].reshape(p.shape))
        at += p.size
    return out


def kernel(x, pre_norm_g, w_in, lambda_re, lambda_im, log_dt, b_re, b_im, c_re, c_im, d_skip, w_glu, b_glu, w_branch_s, w_branch_a, w_out, post_norm_g, loss_target, m_pre_norm_g, m_w_in, m_lambda_re, m_lambda_im, m_log_dt, m_b_re, m_b_im, m_c_re, m_c_im, m_d_skip, m_w_glu, m_b_glu, m_w_branch_s, m_w_branch_a, m_w_out, m_post_norm_g, v_pre_norm_g, v_w_in, v_lambda_re, v_lambda_im, v_log_dt, v_b_re, v_b_im, v_c_re, v_c_im, v_d_skip, v_w_glu, v_b_glu, v_w_branch_s, v_w_branch_a, v_w_out, v_post_norm_g):
    ng = DEPTH * 32
    lre3 = lambda_re.reshape(ng, 1, 64)
    lim3 = lambda_im.reshape(ng, 1, 64)
    ldt3 = log_dt.reshape(ng, 1, 1)
    bre_t = b_re.transpose(0, 1, 3, 2).reshape(ng, 16, 64)
    bim_t = b_im.transpose(0, 1, 3, 2).reshape(ng, 16, 64)
    ar, ai, bbr, bbi = _s5_prep(lre3, lim3, ldt3, bre_t, bim_t)
    ar4 = ar.reshape(DEPTH, NCB, 1, HW)
    ai4 = ai.reshape(DEPTH, NCB, 1, HW)
    bb = jnp.stack([bbr.reshape(DEPTH, 32, 16, 64), bbi.reshape(DEPTH, 32, 16, 64)], axis=1)
    cc = jnp.stack([c_re, -c_im], axis=1)

    xs = [x[0]]
    saved = []
    for l in range(DEPTH):
        w4, wglu4, wbs4, wba4, wout4 = _gather_weights([
            w_in[l].astype(BF16), w_glu[l].astype(BF16), w_branch_s[l].astype(BF16),
            w_branch_a[l].astype(BF16), w_out[l].astype(BF16)])
        wglu = wglu4.reshape(SSM, SSM)
        wout = wout4.reshape(D, D)
        wb4 = _blockdiag_in(bb[l]).astype(BF16)
        wc4 = _blockdiag_out(cc[l]).astype(BF16)
        g1 = pre_norm_g[l].reshape(1, D)
        g2 = post_norm_g[l].reshape(1, D)
        dsk = d_skip[l].reshape(1, SSM)
        bgl = b_glu[l].reshape(1, SSM)
        proj = _rms_proj_fwd(xs[l], g1, w4)
        states, yraw = _s5_scan_fwd(proj, wb4, wc4, ar4[l], ai4[l])
        ys = _s5_tail_fwd(yraw, proj, dsk, wglu, bgl)
        o, lse = _attn_core_fwd(proj)
        ya = _attn_mix_fwd(o, lse, proj)
        xs.append(_merge_out_fwd(ys, ya, proj, wbs4, wba4, wout, g2, xs[l]))
        saved.append((w4, wglu, wbs4, wba4, wout, wb4, wc4, g1, g2, dsk, bgl, proj, states, yraw, ys, ya, o, lse))

    loss_part, dx = _loss_head(xs[DEPTH], loss_target[0])
    loss = lax.psum(loss_part[0, 0], ("x", "y", "c"))

    big_w = (w_in, w_glu.reshape(DEPTH, 128, SSM), w_branch_s, w_branch_a, w_out)
    big_m = (m_w_in, m_w_glu, m_w_branch_s, m_w_branch_a, m_w_out)
    big_v = (v_w_in, v_w_glu, v_w_branch_s, v_w_branch_a, v_w_out)
    big_out = [None] * 5
    small = {k: [None] * DEPTH for k in ("g1", "da", "dbb", "dcc", "dsk", "bgl", "g2")}
    for l in reversed(range(DEPTH)):
        w4, wglu, wbs4, wba4, wout, wb4, wc4, g1, g2, dsk, bgl, proj, states, yraw, ys, ya, o, lse = saved[l]
        dys, dya, dgs, dga, dwbs, dwba, dwout, dg2 = _merge_out_bwd(dx, ys, ya, proj, wbs4, wba4, wout, g2)
        dyraw, du_skip, dzs, dwglu, dbgl, ddsk = _s5_tail_bwd(dys, yraw, proj, dsk, wglu, bgl)
        du, dwb, dwc, da = _s5_scan_bwd(dyraw, states, proj, du_skip, wb4, wc4, ar4[l], ai4[l])
        do, cterm, dza = _attn_mix_bwd(dya, o, lse, proj)
        dq, dk, dv = _attn_core_bwd(proj, do, lse, cterm)
        dproj = jnp.concatenate([du, dzs, dq, dk, dv, dza, dgs, dga], axis=1)
        dh, dwin = _proj_bwd(dproj, xs[l], g1, w4)
        dx, dg1 = _rms_bwd(dh, xs[l], g1, dx)
        parts = [dwin, dwglu.reshape(NCHIP, 128, SSM), dwbs, dwba, dwout.reshape(NCHIP, D // NCHIP, D)]
        landed = _exchange_grads(parts)
        for k in range(5):
            big_out[k] = _adamw_big(l, big_w[k], big_m[k], big_v[k], landed[k], landed[5 + k], big_out[k])
        small["g1"][l], small["g2"][l], small["dsk"][l], small["bgl"][l] = dg1, dg2, ddsk, dbgl
        small["da"][l], small["dbb"][l], small["dcc"][l] = da, _blockdiag_in_t(dwb), _blockdiag_out_t(dwc)

    da = jnp.stack(small["da"]).reshape(DEPTH, NCB, 2, HW)
    gar = da[:, :, 0].reshape(ng, 1, 64)
    gai = da[:, :, 1].reshape(ng, 1, 64)
    dbb = jnp.stack(small["dbb"])
    dcc = jnp.stack(small["dcc"])
    dlre, dlim, dldt, dbre_t, dbim_t = _s5_prep_bwd(
        lre3, lim3, ldt3, bre_t, bim_t, gar, gai, dbb[:, 0].reshape(ng, 16, 64), dbb[:, 1].reshape(ng, 16, 64))
    unt = lambda t: t.reshape(DEPTH, 32, 16, 64).transpose(0, 1, 3, 2)
    small_w = [pre_norm_g, lambda_re, lambda_im, log_dt, b_re, b_im, c_re, c_im, d_skip, b_glu, post_norm_g]
    small_m = [m_pre_norm_g, m_lambda_re, m_lambda_im, m_log_dt, m_b_re, m_b_im, m_c_re, m_c_im, m_d_skip,
               m_b_glu, m_post_norm_g]
    small_v = [v_pre_norm_g, v_lambda_re, v_lambda_im, v_log_dt, v_b_re, v_b_im, v_c_re, v_c_im, v_d_skip,
               v_b_glu, v_post_norm_g]
    small_g = [jnp.stack(small["g1"]).reshape(DEPTH, D), dlre.reshape(DEPTH, 32, 64), dlim.reshape(DEPTH, 32, 64),
               dldt.reshape(DEPTH, 32), unt(dbre_t), unt(dbim_t), dcc[:, 0], -dcc[:, 1],
               jnp.stack(small["dsk"]).reshape(DEPTH, SSM), jnp.stack(small["bgl"]).reshape(DEPTH, SSM),
               jnp.stack(small["g2"]).reshape(DEPTH, D)]
    g_flat = _allreduce_small(_flatten_small(small_g))
    d_flat, nm_flat, nv_flat = _adamw_small(_flatten_small(small_w), g_flat, _flatten_small(small_m),
                                            _flatten_small(small_v))
    sg = _unflatten_small(g_flat, small_w)
    sd = _unflatten_small(d_flat, small_w)
    snm = _unflatten_small(nm_flat, small_w)
    snv = _unflatten_small(nv_flat, small_w)

    def ordered(sm, which):
        bg = [big_out[k][which] for k in range(5)]
        bg[1] = bg[1].reshape(DEPTH, 128, SSM)
        return [sm[0], bg[0], sm[1], sm[2], sm[3], sm[4], sm[5], sm[6], sm[7], sm[8], bg[1], sm[9], bg[2], bg[3],
                bg[4], sm[10]]

    return (loss, dx[None], *ordered(sg, 0), *ordered(sd, 1), *ordered(snm, 2), *ordered(snv, 3))
```

```python
import math

import jax
import jax.numpy as jnp
from jax import lax
from jax.experimental import pallas as pl
from jax.experimental.pallas import tpu as pltpu

F32 = jnp.float32
BF16 = jnp.bfloat16
MESH = pl.DeviceIdType.MESH

DEPTH = 4
L = 2048
D = 1024
NCOL = 8192
SSM = 512
AW = 512
QKV = 1536
RMS_EPS = 1e-6
NCHIP = 4

COL_U, COL_ZS, COL_Q, COL_K, COL_V, COL_ZA, COL_GS, COL_GA = 0, 4, 8, 20, 32, 44, 48, 56

NSEG = 8
SEG = L // NSEG
NCB = 4
HW = 512
SW = 2 * HW

HEAD = 128
DILATIONS = (1, 4, 16)
SCALE = HEAD ** -0.5
NEG = -1e30

ADAM_LR, ADAM_B1, ADAM_B2, ADAM_EPS, ADAM_WD, ADAM_STEP = 0.001, 0.9, 0.999, 1e-08, 0.01, 10


def _sds(shape, dtype=F32):
    return jax.ShapeDtypeStruct(shape, dtype)


def _pcall(body, *, name, out_shape, grid=None, in_specs=None, out_specs=None, scratch=(), vmem_mb=None,
           aliases=None):
    params = {}
    if vmem_mb is not None:
        params["vmem_limit_bytes"] = vmem_mb << 20
    kw = {}
    if grid is not None:
        kw["grid"] = grid
    if in_specs is not None:
        kw["in_specs"] = in_specs
    if out_specs is not None:
        kw["out_specs"] = out_specs
    return pl.pallas_call(body, name=name, out_shape=out_shape, scratch_shapes=list(scratch),
                          compiler_params=pltpu.CompilerParams(**params),
                          input_output_aliases=aliases or {}, **kw)


def _dot(a, b):
    return jnp.dot(a, b, preferred_element_type=F32)


def _dot_nt(a, b):
    return lax.dot_general(a, b, (((1,), (1,)), ((), ())), preferred_element_type=F32)


def _dot_tn(a, b):
    return lax.dot_general(a, b, (((0,), (0,)), ((), ())), preferred_element_type=F32)


def _sigmoid(x):
    return jax.nn.sigmoid(x)


_GELU_K = math.sqrt(2.0 / math.pi)


def _gelu(x):
    return 0.5 * x * (1.0 + jnp.tanh(_GELU_K * (x + 0.044715 * (x * x * x))))


def _gelu_grad(x):
    t = jnp.tanh(_GELU_K * (x + 0.044715 * (x * x * x)))
    return 0.5 * (1.0 + t) + 0.5 * x * (1.0 - t * t) * (_GELU_K * (1.0 + 3.0 * 0.044715 * (x * x)))


def _rms_proj_fwd(x, g, w4):
    tl, tn = 512, 1024

    def body(x_ref, g_ref, w_ref, o_ref):
        xx = x_ref[...]
        inv = lax.rsqrt(jnp.mean(xx * xx, axis=-1, keepdims=True) + RMS_EPS)
        h = (xx * inv * g_ref[...]).astype(BF16)
        o_ref[...] = _dot(h, w_ref[...])

    return _pcall(
        body, name="rms_proj_fwd", grid=(NCOL // tn, L // tl),
        in_specs=[pl.BlockSpec((tl, D), lambda j, i: (i, 0)),
                  pl.BlockSpec((1, D), lambda j, i: (0, 0)),
                  pl.BlockSpec((None, D, tn), lambda j, i: (lax.div(j, 2), 0, lax.rem(j, 2)))],
        out_specs=pl.BlockSpec((tl, tn), lambda j, i: (i, j)),
        out_shape=_sds((L, NCOL)), vmem_mb=40)(x, g, w4)


def _proj_bwd(dproj, x, g, w4):
    tl, tn = 512, 1024
    ni = L // tl

    def body(dp_ref, x_ref, g_ref, w_ref, dh_ref, dw_ref, acc_ref):
        j, i = pl.program_id(0), pl.program_id(1)
        xx = x_ref[...]
        inv = lax.rsqrt(jnp.mean(xx * xx, axis=-1, keepdims=True) + RMS_EPS)
        h = (xx * inv * g_ref[...]).astype(BF16)
        dp = dp_ref[...].astype(BF16)
        part = _dot_tn(h, dp)

        @pl.when(i == 0)
        def _():
            acc_ref[...] = part

        @pl.when(i > 0)
        def _():
            acc_ref[...] += part

        @pl.when(i == ni - 1)
        def _():
            dw_ref[...] = acc_ref[...].astype(BF16)

        rows = pl.ds(pl.multiple_of(i * tl, tl), tl)
        dh = _dot_nt(dp, w_ref[...])

        @pl.when(j == 0)
        def _():
            dh_ref[rows, :] = dh

        @pl.when(j > 0)
        def _():
            dh_ref[rows, :] += dh

    return _pcall(
        body, name="proj_bwd", grid=(NCOL // tn, ni),
        in_specs=[pl.BlockSpec((tl, tn), lambda j, i: (i, j)),
                  pl.BlockSpec((tl, D), lambda j, i: (i, 0)),
                  pl.BlockSpec((1, D), lambda j, i: (0, 0)),
                  pl.BlockSpec((None, D, tn), lambda j, i: (lax.div(j, 2), 0, lax.rem(j, 2)))],
        out_specs=[pl.BlockSpec((L, D), lambda j, i: (0, 0)),
                   pl.BlockSpec((None, D, tn), lambda j, i: (lax.div(j, 2), 0, lax.rem(j, 2)))],
        out_shape=[_sds((L, D)), _sds((NCHIP, D, NCOL // NCHIP), BF16)],
        scratch=[pltpu.VMEM((D, tn), F32)], vmem_mb=56)(dproj, x, g, w4)


def _rms_bwd(dh, x, g, dxn):
    tl = 512

    def body(dh_ref, x_ref, g_ref, dxn_ref, dx_ref, dg_ref):
        i = pl.program_id(0)
        xx = x_ref[...]
        inv = lax.rsqrt(jnp.mean(xx * xx, axis=-1, keepdims=True) + RMS_EPS)
        nrm = xx * inv
        dh_v = dh_ref[...]
        dn = dh_v * g_ref[...]
        dx_ref[...] = dxn_ref[...] + inv * (dn - nrm * jnp.mean(dn * nrm, axis=-1, keepdims=True))
        part = jnp.sum(dh_v * nrm, axis=0, keepdims=True)

        @pl.when(i == 0)
        def _():
            dg_ref[...] = part

        @pl.when(i > 0)
        def _():
            dg_ref[...] += part

    row = pl.BlockSpec((tl, D), lambda i: (i, 0))
    vec = pl.BlockSpec((1, D), lambda i: (0, 0))
    return _pcall(body, name="rms_bwd", grid=(L // tl,), in_specs=[row, row, vec, row],
                  out_specs=[row, vec], out_shape=[_sds((L, D)), _sds((1, D))], vmem_mb=40)(dh, x, g, dxn)


def _s5_param_math(lre, lim, ldt, bre, bim):
    lr = jnp.minimum(lre, -1e-4)
    dt = jnp.exp(ldt)
    mag = jnp.exp(lr * dt)
    ar = mag * jnp.cos(lim * dt)
    ai = mag * jnp.sin(lim * dt)
    den = lr * lr + lim * lim
    nr = ar - 1.0
    qr = (nr * lr + ai * lim) / den
    qi = (ai * lr - nr * lim) / den
    return ar, ai, qr * bre - qi * bim, qr * bim + qi * bre


def _s5_prep(lre, lim, ldt, bre, bim):
    n = lre.shape[0]

    def body(lre_ref, lim_ref, ldt_ref, bre_ref, bim_ref, ar_ref, ai_ref, bbr_ref, bbi_ref):
        ar, ai, bbr, bbi = _s5_param_math(lre_ref[...], lim_ref[...], ldt_ref[...], bre_ref[...], bim_ref[...])
        ar_ref[...] = ar
        ai_ref[...] = ai
        bbr_ref[...] = bbr
        bbi_ref[...] = bbi

    return _pcall(body, name="s5_prep",
                  out_shape=[_sds((n, 1, 64)), _sds((n, 1, 64)), _sds((n, 16, 64)), _sds((n, 16, 64))])(
                      lre, lim, ldt, bre, bim)


def _s5_prep_bwd(lre, lim, ldt, bre, bim, gar, gai, gbbr, gbbi):
    n = lre.shape[0]

    def body(lre_ref, lim_ref, ldt_ref, bre_ref, bim_ref, gar_ref, gai_ref, gbbr_ref, gbbi_ref,
             dlre_ref, dlim_ref, dldt_ref, dbre_ref, dbim_ref):
        _, vjp = jax.vjp(_s5_param_math, lre_ref[...], lim_ref[...], ldt_ref[...], bre_ref[...], bim_ref[...])
        dlre, dlim, dldt, dbre, dbim = vjp((gar_ref[...], gai_ref[...], gbbr_ref[...], gbbi_ref[...]))
        dlre_ref[...] = dlre
        dlim_ref[...] = dlim
        dldt_ref[...] = dldt
        dbre_ref[...] = dbre
        dbim_ref[...] = dbim

    return _pcall(body, name="s5_prep_bwd",
                  out_shape=[_sds((n, 1, 64)), _sds((n, 1, 64)), _sds((n, 1, 1)), _sds((n, 16, 64)),
                             _sds((n, 16, 64))])(lre, lim, ldt, bre, bim, gar, gai, gbbr, gbbi)


def _permute_rows(dst_ref, src_ref):
    for s in range(NSEG):
        dst_ref[pl.ds(s, SEG, stride=NSEG), :] = src_ref[pl.ds(s * SEG, SEG), :]


def _unpermute_rows(src_ref, s):
    return src_ref[pl.ds(s, SEG, stride=NSEG), :]


def _row_of(slab, rows, j):
    return jnp.sum(jnp.where(rows == j, slab, 0.0), axis=0, keepdims=True)


def _s5_scan_fwd(proj, wb4, wc4, ar4, ai4):
    def body(u_ref, wb_ref, wc_ref, ar_ref, ai_ref, x_ref, y_ref, up_ref, d_ref, yp_ref):
        _permute_rows(up_ref, u_ref)
        d_ref[...] = _dot(up_ref[...].astype(BF16), wb_ref[...])
        ar = jnp.broadcast_to(ar_ref[...], (NSEG, HW))
        ai = jnp.broadcast_to(ai_ref[...], (NSEG, HW))
        zero = jnp.zeros((NSEG, HW), F32)

        def drive(t):
            row = pl.multiple_of(t * NSEG, NSEG)
            dd = d_ref[pl.ds(row, NSEG), :]
            return row, dd[:, :HW], dd[:, HW:]

        def local_step(t, c):
            xr, xi = c
            _, dr, di = drive(t)
            return ar * xr - ai * xi + dr, ar * xi + ai * xr + di

        fr, fi = lax.fori_loop(0, SEG, local_step, (zero, zero), unroll=8)
        pr, pi_ = ar_ref[...], ai_ref[...]
        for _ in range(int(math.log2(SEG))):
            pr, pi_ = pr * pr - pi_ * pi_, 2.0 * pr * pi_
        rows = lax.broadcasted_iota(jnp.int32, (NSEG, HW), 0)
        cr, ci = zero, zero
        for j in range(NSEG - 1):
            sr, si = _row_of(cr, rows, j), _row_of(ci, rows, j)
            gr, gi = _row_of(fr, rows, j), _row_of(fi, rows, j)
            nr = pr * sr - pi_ * si + gr
            ni = pr * si + pi_ * sr + gi
            cr = jnp.where(rows == j + 1, nr, cr)
            ci = jnp.where(rows == j + 1, ni, ci)

        def true_step(t, c):
            xr, xi = c
            row, dr, di = drive(t)
            nr = ar * xr - ai * xi + dr
            ni = ar * xi + ai * xr + di
            x_ref[pl.ds(row, NSEG), :] = jnp.concatenate([nr, ni], axis=1)
            return nr, ni

        lax.fori_loop(0, SEG, true_step, (cr, ci), unroll=8)
        yp_ref[...] = _dot(x_ref[...].astype(BF16), wc_ref[...])
        for s in range(NSEG):
            y_ref[pl.ds(s * SEG, SEG), :] = _unpermute_rows(yp_ref, s)

    return _pcall(
        body, name="s5_scan_fwd", grid=(NCB,),
        in_specs=[pl.BlockSpec((L, 128), lambda b: (0, COL_U + b)),
                  pl.BlockSpec((None, 128, SW), lambda b: (b, 0, 0)),
                  pl.BlockSpec((None, SW, 128), lambda b: (b, 0, 0)),
                  pl.BlockSpec((None, 1, HW), lambda b: (b, 0, 0)),
                  pl.BlockSpec((None, 1, HW), lambda b: (b, 0, 0))],
        out_specs=[pl.BlockSpec((None, L, SW), lambda b: (b, 0, 0)),
                   pl.BlockSpec((L, 128), lambda b: (0, b))],
        out_shape=[_sds((NCB, L, SW)), _sds((L, SSM))],
        scratch=[pltpu.VMEM((L, 128), F32), pltpu.VMEM((L, SW), F32), pltpu.VMEM((L, 128), F32)],
        vmem_mb=56)(proj, wb4, wc4, ar4, ai4)


def _s5_scan_bwd(dy, xs, proj, du_skip, wb4, wc4, ar4, ai4):
    def body(dy_ref, x_ref, u_ref, dus_ref, wb_ref, wc_ref, ar_ref, ai_ref,
             du_ref, dwb_ref, dwc_ref, da_ref, dyp_ref, up_ref, g_ref, dup_ref):
        _permute_rows(dyp_ref, dy_ref)
        _permute_rows(up_ref, u_ref)
        dyp = dyp_ref[...].astype(BF16)
        g_ref[...] = _dot_nt(dyp, wc_ref[...])
        dwc_ref[...] = _dot_tn(x_ref[...].astype(BF16), dyp)
        ar = jnp.broadcast_to(ar_ref[...], (NSEG, HW))
        ai = jnp.broadcast_to(ai_ref[...], (NSEG, HW))
        zero = jnp.zeros((NSEG, HW), F32)

        def load(t):
            row = pl.multiple_of(t * NSEG, NSEG)
            gg = g_ref[pl.ds(row, NSEG), :]
            return row, gg[:, :HW], gg[:, HW:]

        def local_step(k, c):
            lr, li = c
            _, gr, gi = load(SEG - 1 - k)
            return ar * lr + ai * li + gr, ar * li - ai * lr + gi

        fr, fi = lax.fori_loop(0, SEG, local_step, (zero, zero), unroll=8)
        pr, pi_ = ar_ref[...], -ai_ref[...]
        for _ in range(int(math.log2(SEG))):
            pr, pi_ = pr * pr - pi_ * pi_, 2.0 * pr * pi_
        rows = lax.broadcasted_iota(jnp.int32, (NSEG, HW), 0)
        cr, ci = zero, zero
        for j in range(NSEG - 1, 0, -1):
            sr, si = _row_of(cr, rows, j), _row_of(ci, rows, j)
            gr, gi = _row_of(fr, rows, j), _row_of(fi, rows, j)
            nr = pr * sr - pi_ * si + gr
            ni = pr * si + pi_ * sr + gi
            cr = jnp.where(rows == j - 1, nr, cr)
            ci = jnp.where(rows == j - 1, ni, ci)

        def true_step(k, c):
            lr, li, gar, gai = c
            t = SEG - 1 - k
            row, gr, gi = load(t)
            nr = ar * lr + ai * li + gr
            ni = ar * li - ai * lr + gi
            g_ref[pl.ds(row, NSEG), :] = jnp.concatenate([nr, ni], axis=1)
            prow = pl.multiple_of(jnp.maximum(t - 1, 0) * NSEG, NSEG)
            xp = x_ref[pl.ds(prow, NSEG), :]
            live = (t > 0).astype(F32)
            xr, xi = xp[:, :HW] * live, xp[:, HW:] * live
            return nr, ni, gar + (nr * xr + ni * xi), gai + (ni * xr - nr * xi)

        _, _, gar, gai = lax.fori_loop(0, SEG, true_step, (cr, ci, zero, zero), unroll=4)
        l0 = g_ref[1:NSEG, :]
        xl = x_ref[(SEG - 1) * NSEG:(SEG - 1) * NSEG + NSEG - 1, :]
        l0r, l0i, xlr, xli = l0[:, :HW], l0[:, HW:], xl[:, :HW], xl[:, HW:]
        gar_t = jnp.sum(gar, axis=0, keepdims=True) + jnp.sum(l0r * xlr + l0i * xli, axis=0, keepdims=True)
        gai_t = jnp.sum(gai, axis=0, keepdims=True) + jnp.sum(l0i * xlr - l0r * xli, axis=0, keepdims=True)
        da_ref[...] = jnp.concatenate([gar_t, gai_t], axis=1)
        lam = g_ref[...].astype(BF16)
        dwb_ref[...] = _dot_tn(up_ref[...].astype(BF16), lam)
        dup_ref[...] = _dot_nt(lam, wb_ref[...])
        for s in range(NSEG):
            sl = pl.ds(s * SEG, SEG)
            du_ref[sl, :] = _unpermute_rows(dup_ref, s) + dus_ref[sl, :]

    col = lambda off: pl.BlockSpec((L, 128), lambda b: (0, off + b))
    return _pcall(
        body, name="s5_scan_bwd", grid=(NCB,),
        in_specs=[col(0), pl.BlockSpec((None, L, SW), lambda b: (b, 0, 0)), col(COL_U), col(0),
                  pl.BlockSpec((None, 128, SW), lambda b: (b, 0, 0)),
                  pl.BlockSpec((None, SW, 128), lambda b: (b, 0, 0)),
                  pl.BlockSpec((None, 1, HW), lambda b: (b, 0, 0)),
                  pl.BlockSpec((None, 1, HW), lambda b: (b, 0, 0))],
        out_specs=[col(0), pl.BlockSpec((None, 128, SW), lambda b: (b, 0, 0)),
                   pl.BlockSpec((None, SW, 128), lambda b: (b, 0, 0)),
                   pl.BlockSpec((None, 1, SW), lambda b: (b, 0, 0))],
        out_shape=[_sds((L, SSM)), _sds((NCB, 128, SW)), _sds((NCB, SW, 128)), _sds((NCB, 1, SW))],
        scratch=[pltpu.VMEM((L, 128), F32), pltpu.VMEM((L, 128), F32), pltpu.VMEM((L, SW), F32),
                 pltpu.VMEM((L, 128), F32)],
        vmem_mb=56)(dy, xs, proj, du_skip, wb4, wc4, ar4, ai4)


def _s5_tail_fwd(yraw, proj, dsk, wglu, bglu):
    tl = 512

    def body(y_ref, u_ref, z_ref, dsk_ref, w_ref, b_ref, o_ref):
        y1 = y_ref[...] + dsk_ref[...] * u_ref[...]
        y2 = _gelu(y1)
        gl = _dot(y2.astype(BF16), w_ref[...]) + b_ref[...]
        z = z_ref[...]
        o_ref[...] = (y2 * _sigmoid(gl)) * (z * _sigmoid(z))

    blk = lambda c: pl.BlockSpec((tl, SSM), lambda i: (i, c))
    vec = pl.BlockSpec((1, SSM), lambda i: (0, 0))
    return _pcall(body, name="s5_tail_fwd", grid=(L // tl,),
                  in_specs=[blk(0), blk(0), blk(1), vec, pl.BlockSpec((SSM, SSM), lambda i: (0, 0)), vec],
                  out_specs=blk(0), out_shape=_sds((L, SSM)), vmem_mb=40)(yraw, proj, proj, dsk, wglu, bglu)


def _s5_tail_bwd(dys, yraw, proj, dsk, wglu, bglu):
    tl = 512
    ni = L // tl

    def body(dys_ref, y_ref, u_ref, z_ref, dsk_ref, w_ref, b_ref,
             dy_ref, dus_ref, dz_ref, dw_ref, db_ref, dd_ref, acc_ref):
        i = pl.program_id(0)
        u = u_ref[...]
        y1 = y_ref[...] + dsk_ref[...] * u
        y2 = _gelu(y1)
        y2b = y2.astype(BF16)
        sg = _sigmoid(_dot(y2b, w_ref[...]) + b_ref[...])
        y3 = y2 * sg
        z = z_ref[...]
        sz = _sigmoid(z)
        dys = dys_ref[...]
        dy3 = dys * (z * sz)
        dz_ref[...] = dys * y3 * (sz * (1.0 + z * (1.0 - sz)))
        dgl = (dy3 * y2) * (sg * (1.0 - sg))
        dglb = dgl.astype(BF16)
        dy2 = dy3 * sg + _dot_nt(dglb, w_ref[...])
        dy1 = dy2 * _gelu_grad(y1)
        dy_ref[...] = dy1
        dus_ref[...] = dsk_ref[...] * dy1
        dw = _dot_tn(y2b, dglb)
        db = jnp.sum(dgl, axis=0, keepdims=True)
        dd = jnp.sum(dy1 * u, axis=0, keepdims=True)

        @pl.when(i == 0)
        def _():
            acc_ref[...] = dw
            db_ref[...] = db
            dd_ref[...] = dd

        @pl.when(i > 0)
        def _():
            acc_ref[...] += dw
            db_ref[...] += db
            dd_ref[...] += dd

        @pl.when(i == ni - 1)
        def _():
            dw_ref[...] = acc_ref[...].astype(BF16)

    blk = lambda c: pl.BlockSpec((tl, SSM), lambda i: (i, c))
    vec = pl.BlockSpec((1, SSM), lambda i: (0, 0))
    mat = pl.BlockSpec((SSM, SSM), lambda i: (0, 0))
    return _pcall(body, name="s5_tail_bwd", grid=(ni,),
                  in_specs=[blk(0), blk(0), blk(0), blk(1), vec, mat, vec],
                  out_specs=[blk(0), blk(0), blk(0), mat, vec, vec],
                  out_shape=[_sds((L, SSM)), _sds((L, SSM)), _sds((L, SSM)), _sds((SSM, SSM), BF16),
                             _sds((1, SSM)), _sds((1, SSM))],
                  scratch=[pltpu.VMEM((SSM, SSM), F32)], vmem_mb=40)(dys, yraw, proj, proj, dsk, wglu, bglu)


def _attn_blocks(dil):
    nb = L // dil // HEAD
    return [(i * HEAD * dil + r, (i - 1) * HEAD * dil + r if i > 0 else None)
            for r in range(dil) for i in range(nb)]


def _rows(start, dil):
    return pl.ds(start, HEAD) if dil == 1 else pl.ds(start, HEAD, stride=dil)


def _band_masks():
    ri = lax.broadcasted_iota(jnp.int32, (HEAD, HEAD), 0)
    ci = lax.broadcasted_iota(jnp.int32, (HEAD, HEAD), 1)
    return ci <= ri, ci >= ri


def _qkv_specs(index):
    def spec(off):
        return pl.BlockSpec((L, HEAD), lambda gi, h: (0, off + index(gi, h)))
    return [spec(off) for off in (COL_Q, COL_K, COL_V)]


def _attn_core_fwd(proj):
    def body(q_ref, k_ref, v_ref, o_ref, lse_ref):
        gi = pl.program_id(0)
        m_cur, m_prev = _band_masks()
        for g, dil in enumerate(DILATIONS):
            @pl.when(gi == g)
            def _(dil=dil):
                for st, pst in _attn_blocks(dil):
                    r = _rows(st, dil)
                    q = q_ref[r, :].astype(BF16)
                    s_c = jnp.where(m_cur, _dot_nt(q, k_ref[r, :].astype(BF16)) * SCALE, NEG)
                    mx = jnp.max(s_c, axis=-1, keepdims=True)
                    if pst is not None:
                        rp = _rows(pst, dil)
                        s_p = jnp.where(m_prev, _dot_nt(q, k_ref[rp, :].astype(BF16)) * SCALE, NEG)
                        mx = jnp.maximum(mx, jnp.max(s_p, axis=-1, keepdims=True))
                    p_c = jnp.exp(s_c - mx)
                    den = jnp.sum(p_c, axis=-1, keepdims=True)
                    acc = _dot(p_c.astype(BF16), v_ref[r, :].astype(BF16))
                    if pst is not None:
                        p_p = jnp.exp(s_p - mx)
                        den = den + jnp.sum(p_p, axis=-1, keepdims=True)
                        acc = acc + _dot(p_p.astype(BF16), v_ref[rp, :].astype(BF16))
                    o_ref[r, :] = acc / den
                    lse_ref[r, :] = jnp.broadcast_to(mx + jnp.log(den), (HEAD, HEAD))

    idx = lambda gi, h: gi * 4 + h
    out = pl.BlockSpec((L, HEAD), lambda gi, h: (0, gi * 4 + h))
    return _pcall(body, name="attn_core_fwd", grid=(3, 4), in_specs=_qkv_specs(idx), out_specs=[out, out],
                  out_shape=[_sds((L, QKV)), _sds((L, QKV))], vmem_mb=40)(proj, proj, proj)


def _attn_mix_fwd(o, lse, proj):
    tl = 512

    def body(o_ref, l_ref, z_ref, y_ref):
        for h in range(4):
            c = [pl.ds((g * 4 + h) * HEAD, HEAD) for g in range(3)]
            ls = [l_ref[:, c[g]] for g in range(3)]
            m = jnp.maximum(jnp.maximum(ls[0], ls[1]), ls[2])
            e = [jnp.exp(ls[g] - m) for g in range(3)]
            y = (e[0] * o_ref[:, c[0]] + e[1] * o_ref[:, c[1]] + e[2] * o_ref[:, c[2]]) / (e[0] + e[1] + e[2])
            z = z_ref[:, pl.ds(h * HEAD, HEAD)]
            y_ref[:, pl.ds(h * HEAD, HEAD)] = y * (z * _sigmoid(z))

    wide = pl.BlockSpec((tl, QKV), lambda i: (i, 0))
    return _pcall(body, name="attn_mix_fwd", grid=(L // tl,),
                  in_specs=[wide, wide, pl.BlockSpec((tl, AW), lambda i: (i, COL_ZA // 4))],
                  out_specs=pl.BlockSpec((tl, AW), lambda i: (i, 0)), out_shape=_sds((L, AW)),
                  vmem_mb=40)(o, lse, proj)


def _attn_mix_bwd(dya, o, lse, proj):
    tl = 512

    def body(dya_ref, o_ref, l_ref, z_ref, do_ref, c_ref, dz_ref):
        for h in range(4):
            c = [pl.ds((g * 4 + h) * HEAD, HEAD) for g in range(3)]
            hs = pl.ds(h * HEAD, HEAD)
            ls = [l_ref[:, c[g]] for g in range(3)]
            m = jnp.maximum(jnp.maximum(ls[0], ls[1]), ls[2])
            e = [jnp.exp(ls[g] - m) for g in range(3)]
            den = e[0] + e[1] + e[2]
            al = [e[g] / den for g in range(3)]
            y = al[0] * o_ref[:, c[0]] + al[1] * o_ref[:, c[1]] + al[2] * o_ref[:, c[2]]
            z = z_ref[:, hs]
            sz = _sigmoid(z)
            dya = dya_ref[:, hs]
            dz_ref[:, hs] = dya * y * (sz * (1.0 + z * (1.0 - sz)))
            dy = dya * (z * sz)
            tot = jnp.sum(dy * y, axis=-1, keepdims=True)
            for g in range(3):
                do_ref[:, c[g]] = al[g] * dy
                c_ref[:, c[g]] = -(al[g] * tot)

    wide = pl.BlockSpec((tl, QKV), lambda i: (i, 0))
    nar = pl.BlockSpec((tl, AW), lambda i: (i, 0))
    return _pcall(body, name="attn_mix_bwd", grid=(L // tl,),
                  in_specs=[nar, wide, wide, pl.BlockSpec((tl, AW), lambda i: (i, COL_ZA // 4))],
                  out_specs=[wide, wide, nar], out_shape=[_sds((L, QKV)), _sds((L, QKV)), _sds((L, AW))],
                  vmem_mb=48)(dya, o, lse, proj)


def _attn_core_bwd(proj, do, lse, cc):
    def body(q_ref, k_ref, v_ref, do_ref, lse_ref, c_ref, dq_ref, dk_ref, dv_ref):
        gi = pl.program_id(0)
        m_cur, m_prev = _band_masks()
        dk_ref[...] = jnp.zeros((L, HEAD), F32)
        dv_ref[...] = jnp.zeros((L, HEAD), F32)
        for g, dil in enumerate(DILATIONS):
            @pl.when(gi == g)
            def _(dil=dil):
                for st, pst in _attn_blocks(dil):
                    r = _rows(st, dil)
                    q = q_ref[r, :].astype(BF16)
                    do_b = do_ref[r, :].astype(BF16)
                    lse_b = lse_ref[r, :]
                    c_b = c_ref[r, :]
                    dq = None
                    for rr, mask in ((r, m_cur),) + (((_rows(pst, dil), m_prev),) if pst is not None else ()):
                        kb = k_ref[rr, :].astype(BF16)
                        s = _dot_nt(q, kb) * SCALE
                        p = jnp.where(mask, jnp.exp(s - lse_b), 0.0)
                        dp = _dot_nt(do_b, v_ref[rr, :].astype(BF16))
                        ds = (p * (dp + c_b) * SCALE).astype(BF16)
                        part = _dot(ds, kb)
                        dq = part if dq is None else dq + part
                        dk_ref[rr, :] += _dot_tn(ds, q)
                        dv_ref[rr, :] += _dot_tn(p.astype(BF16), do_b)
                    dq_ref[r, :] = dq

    idx = lambda gi, h: gi * 4 + h
    blk = pl.BlockSpec((L, HEAD), lambda gi, h: (0, gi * 4 + h))
    return _pcall(body, name="attn_core_bwd", grid=(3, 4), in_specs=_qkv_specs(idx) + [blk, blk, blk],
                  out_specs=[blk, blk, blk], out_shape=[_sds((L, QKV))] * 3, vmem_mb=48)(
                      proj, proj, proj, do, lse, cc)


def _merge_fwd_math(ys_b, ya_b, gs, ga, wbs_ref, wba_ref, wout_ref):
    bs = jnp.concatenate([_dot(ys_b, wbs_ref[s]) for s in range(NCHIP)], axis=1)
    ba = jnp.concatenate([_dot(ya_b, wba_ref[s]) for s in range(NCHIP)], axis=1)
    sgs, sga = _sigmoid(gs), _sigmoid(ga)
    merged = sgs * bs + sga * ba
    out = _dot(merged.astype(BF16), wout_ref[...])
    inv = lax.rsqrt(jnp.mean(out * out, axis=-1, keepdims=True) + RMS_EPS)
    return bs, ba, sgs, sga, merged, out, inv


def _merge_specs(tl):
    row = lambda w, c: pl.BlockSpec((tl, w), lambda i: (i, c))
    return [row(SSM, 0), row(AW, 0), row(D, COL_GS // 8), row(D, COL_GA // 8),
            pl.BlockSpec((NCHIP, SSM, D // NCHIP), lambda i: (0, 0, 0)),
            pl.BlockSpec((NCHIP, AW, D // NCHIP), lambda i: (0, 0, 0)),
            pl.BlockSpec((D, D), lambda i: (0, 0)),
            pl.BlockSpec((1, D), lambda i: (0, 0))]


def _merge_out_fwd(ys, ya, proj, wbs4, wba4, wout, g2, x):
    tl = 256

    def body(ys_ref, ya_ref, gs_ref, ga_ref, wbs_ref, wba_ref, wout_ref, g2_ref, x_ref, o_ref):
        *_, out, inv = _merge_fwd_math(ys_ref[...].astype(BF16), ya_ref[...].astype(BF16), gs_ref[...],
                                       ga_ref[...], wbs_ref, wba_ref, wout_ref)
        o_ref[...] = x_ref[...] + out * inv * g2_ref[...]

    row = pl.BlockSpec((tl, D), lambda i: (i, 0))
    return _pcall(body, name="merge_out_fwd", grid=(L // tl,), in_specs=_merge_specs(tl) + [row],
                  out_specs=row, out_shape=_sds((L, D)), vmem_mb=48)(ys, ya, proj, proj, wbs4, wba4, wout, g2, x)


def _merge_out_bwd(dxn, ys, ya, proj, wbs4, wba4, wout, g2):
    tl = 256
    ni = L // tl
    cw = D // NCHIP

    def body(dxn_ref, ys_ref, ya_ref, gs_ref, ga_ref, wbs_ref, wba_ref, wout_ref, g2_ref,
             dys_ref, dya_ref, dgs_ref, dga_ref, dwbs_ref, dwba_ref, dwout_ref, dg2_ref,
             abs_ref, aba_ref, aout_ref):
        i = pl.program_id(0)
        ys_b, ya_b = ys_ref[...].astype(BF16), ya_ref[...].astype(BF16)
        bs, ba, sgs, sga, merged, out, inv = _merge_fwd_math(ys_b, ya_b, gs_ref[...], ga_ref[...],
                                                             wbs_ref, wba_ref, wout_ref)
        nrm = out * inv
        dxn = dxn_ref[...]
        dg2 = jnp.sum(dxn * nrm, axis=0, keepdims=True)
        dn = dxn * g2_ref[...]
        dout = (inv * (dn - nrm * jnp.mean(dn * nrm, axis=-1, keepdims=True))).astype(BF16)
        dwout = _dot_tn(merged.astype(BF16), dout)
        dm = _dot_nt(dout, wout_ref[...])
        dbs, dba = dm * sgs, dm * sga
        dgs_ref[...] = dm * bs * (sgs * (1.0 - sgs))
        dga_ref[...] = dm * ba * (sga * (1.0 - sga))
        dbs_b, dba_b = dbs.astype(BF16), dba.astype(BF16)
        dys = None
        dya = None
        first = i == 0
        for s in range(NCHIP):
            cs = slice(s * cw, (s + 1) * cw)
            p_s = _dot_nt(dbs_b[:, cs], wbs_ref[s])
            p_a = _dot_nt(dba_b[:, cs], wba_ref[s])
            dys = p_s if dys is None else dys + p_s
            dya = p_a if dya is None else dya + p_a
            w_s = _dot_tn(ys_b, dbs_b[:, cs])
            w_a = _dot_tn(ya_b, dba_b[:, cs])

            @pl.when(first)
            def _(s=s, w_s=w_s, w_a=w_a):
                abs_ref[s] = w_s
                aba_ref[s] = w_a

            @pl.when(jnp.logical_not(first))
            def _(s=s, w_s=w_s, w_a=w_a):
                abs_ref[s] += w_s
                aba_ref[s] += w_a

        dys_ref[...] = dys
        dya_ref[...] = dya

        @pl.when(first)
        def _():
            aout_ref[...] = dwout
            dg2_ref[...] = dg2

        @pl.when(jnp.logical_not(first))
        def _():
            aout_ref[...] += dwout
            dg2_ref[...] += dg2

        @pl.when(i == ni - 1)
        def _():
            dwbs_ref[...] = abs_ref[...].astype(BF16)
            dwba_ref[...] = aba_ref[...].astype(BF16)
            dwout_ref[...] = aout_ref[...].astype(BF16)

    row = lambda w: pl.BlockSpec((tl, w), lambda i: (i, 0))
    w4 = pl.BlockSpec((NCHIP, SSM, cw), lambda i: (0, 0, 0))
    sq = pl.BlockSpec((D, D), lambda i: (0, 0))
    vec = pl.BlockSpec((1, D), lambda i: (0, 0))
    return _pcall(body, name="merge_out_bwd", grid=(ni,), in_specs=[row(D)] + _merge_specs(tl),
                  out_specs=[row(SSM), row(AW), row(D), row(D), w4, w4, sq, vec],
                  out_shape=[_sds((L, SSM)), _sds((L, AW)), _sds((L, D)), _sds((L, D)),
                             _sds((NCHIP, SSM, cw), BF16), _sds((NCHIP, AW, cw), BF16), _sds((D, D), BF16),
                             _sds((1, D))],
                  scratch=[pltpu.VMEM((NCHIP, SSM, cw), F32), pltpu.VMEM((NCHIP, AW, cw), F32),
                           pltpu.VMEM((D, D), F32)],
                  vmem_mb=56)(dxn, ys, ya, proj, proj, wbs4, wba4, wout, g2)


def _loss_head(y, target):
    tl = 512

    def body(y_ref, t_ref, loss_ref, dy_ref):
        i = pl.program_id(0)
        err = y_ref[...] - t_ref[...]
        dy_ref[...] = err / D
        part = 0.5 * jnp.sum(jnp.mean(err * err, axis=-1, keepdims=True), axis=0, keepdims=True)
        part = jnp.broadcast_to(part, (8, 128))

        @pl.when(i == 0)
        def _():
            loss_ref[...] = part

        @pl.when(i > 0)
        def _():
            loss_ref[...] += part

    row = pl.BlockSpec((tl, D), lambda i: (i, 0))
    return _pcall(body, name="loss_head", grid=(L // tl,), in_specs=[row, row],
                  out_specs=[pl.BlockSpec((8, 128), lambda i: (0, 0)), row],
                  out_shape=[_sds((8, 128)), _sds((L, D))], vmem_mb=40)(y, target)


def _adamw_math(w, g, m, v):
    m = ADAM_B1 * m + (1.0 - ADAM_B1) * g
    v = ADAM_B2 * v + (1.0 - ADAM_B2) * (g * g)
    m_hat = m / (1.0 - ADAM_B1 ** ADAM_STEP)
    v_hat = v / (1.0 - ADAM_B2 ** ADAM_STEP)
    delta = -ADAM_LR * (m_hat / (jnp.sqrt(v_hat) + ADAM_EPS) + ADAM_WD * w)
    return delta, m, v


def _adamw_big(layer, w, m, v, own, sib, prev):
    _, r, c = w.shape
    tr = min(r, 128)

    def body(w_ref, m_ref, v_ref, own_ref, sib_ref, *rest):
        g_ref, d_ref, nm_ref, nv_ref = rest[-4:]
        a = own_ref[0].astype(F32)
        b = sib_ref[0].astype(F32)
        for s in range(1, NCHIP):
            a = a + own_ref[s].astype(F32)
            b = b + sib_ref[s].astype(F32)
        g = a + b
        delta, nm, nv = _adamw_math(w_ref[...], g, m_ref[...], v_ref[...])
        g_ref[...] = g
        d_ref[...] = delta
        nm_ref[...] = nm
        nv_ref[...] = nv

    lay = pl.BlockSpec((None, tr, c), lambda i: (layer, i, 0))
    slots = pl.BlockSpec((NCHIP, tr, c), lambda i: (0, i, 0))
    ins = [w, m, v, own, sib]
    in_specs = [lay, lay, lay, slots, slots]
    aliases = {}
    if prev is not None:
        ins += list(prev)
        in_specs += [pl.BlockSpec(memory_space=pl.ANY)] * 4
        aliases = {5 + k: k for k in range(4)}
    return _pcall(body, name="adamw_big", grid=(r // tr,), in_specs=in_specs, out_specs=[lay] * 4,
                  out_shape=[_sds(w.shape)] * 4, aliases=aliases, vmem_mb=48)(*ins)


SMALL_TILE = 512


def _adamw_small(w, parts, m, v):
    r = w.shape[0]

    def body(w_ref, p_ref, m_ref, v_ref, g_ref, d_ref, nm_ref, nv_ref):
        g = p_ref[0]
        for dev in range(1, 8):
            g = g + p_ref[dev]
        delta, nm, nv = _adamw_math(w_ref[...], g, m_ref[...], v_ref[...])
        g_ref[...] = g
        d_ref[...] = delta
        nm_ref[...] = nm
        nv_ref[...] = nv

    row = pl.BlockSpec((SMALL_TILE, 128), lambda i: (i, 0))
    return _pcall(body, name="adamw_small", grid=(r // SMALL_TILE,),
                  in_specs=[row, pl.BlockSpec((8, SMALL_TILE, 128), lambda i: (0, i, 0)), row, row],
                  out_specs=[row] * 4, out_shape=[_sds((r, 128))] * 4, vmem_mb=40)(w, parts, m, v)


def _place():
    x, y, c = lax.axis_index("x"), lax.axis_index("y"), lax.axis_index("c")
    return x, y, c, 2 * x + y


def _chip_peer(x, y, j):
    return (1 - x if j & 2 else x), (1 - y if j & 1 else y)


_HBM = pl.BlockSpec(memory_space=pltpu.HBM)
_SEM = pl.BlockSpec(memory_space=pltpu.SEMAPHORE)
_EFFECT = pltpu.SideEffectType.DATAFLOW_SIDE_EFFECTING


def _in_hbm(a):
    return pltpu.with_memory_space_constraint(a, pltpu.HBM)


def _copies_start(name, plan, ncopy, srcs, lands):
    n = len(srcs) + len(lands)

    def body(*refs):
        send_sems, recv_sems = refs[n], refs[n + 1]
        token = refs[-1]
        for k, (src, dst, dev) in enumerate(plan(refs[:len(srcs)], refs[len(srcs):n])):
            pltpu.make_async_remote_copy(src_ref=src, dst_ref=dst, send_sem=send_sems.at[k],
                                         recv_sem=recv_sems.at[k], device_id=dev, device_id_type=MESH).start()
        token[...] = jnp.zeros_like(token)

    bufs = list(srcs) + list(lands)
    outs = pl.pallas_call(
        body, name=name,
        out_shape=(pltpu.SemaphoreType.DMA((ncopy,)), pltpu.SemaphoreType.DMA((ncopy,)),
                   *[pltpu.HBM(a.shape, a.dtype) for a in bufs], _sds((8, 128))),
        in_specs=[_HBM] * n,
        out_specs=(_SEM, _SEM, *[_HBM] * n, pl.BlockSpec(memory_space=pltpu.VMEM)),
        input_output_aliases={i: 2 + i for i in range(n)},
        compiler_params=pltpu.CompilerParams(has_side_effects=_EFFECT),
    )(*[_in_hbm(a) for a in bufs])
    return outs[0], outs[1], list(outs[2:2 + len(srcs)]), list(outs[2 + len(srcs):2 + n]), outs[-1]


def _copies_wait(name, plan, send_sems, recv_sems, srcs, lands, after):
    n = len(srcs) + len(lands)

    def body(*refs):
        s_sems, r_sems = refs[n], refs[n + 1]
        for k, (src, dst, dev) in enumerate(plan(refs[:len(srcs)], refs[len(srcs):n])):
            cp = pltpu.make_async_remote_copy(src_ref=src, dst_ref=dst, send_sem=s_sems.at[k],
                                              recv_sem=r_sems.at[k], device_id=dev, device_id_type=MESH)
            cp.wait_send()
            cp.wait_recv()

    bufs = list(srcs) + list(lands)
    outs = pl.pallas_call(
        body, name=name,
        out_shape=tuple(pltpu.HBM(a.shape, a.dtype) for a in bufs),
        in_specs=[_HBM] * n + [_SEM, _SEM, pl.BlockSpec(memory_space=pl.ANY)],
        out_specs=tuple([_HBM] * n),
        input_output_aliases={i: i for i in range(n)},
        compiler_params=pltpu.CompilerParams(has_side_effects=_EFFECT),
    )(*bufs, send_sems, recv_sems, after)
    return list(outs[:len(srcs)]), list(outs[len(srcs):])


def _with_own_slot(block, slot, nslots=NCHIP):
    land = lax.empty((nslots,) + block.shape, block.dtype)
    return lax.dynamic_update_slice(land, block[None], (slot,) + (0,) * block.ndim)


def _after(x, *tokens):
    return lax.optimization_barrier((x, *tokens))[0]


def _gather_plan(srcs, lands):
    x, y, c, s = _place()
    return [(src, land.at[s], (*_chip_peer(x, y, j), c))
            for src, land in zip(srcs, lands) for j in range(1, NCHIP)]


def _exchange_plan(srcs, lands):
    x, y, c, s = _place()
    return [(src.at[s ^ j], land.at[s], (*_chip_peer(x, y, j), c))
            for src, land in zip(srcs, lands) for j in range(1, NCHIP)]


def _sibling_plan(srcs, lands):
    x, y, c, _ = _place()
    return [(src, land, (x, y, 1 - c)) for src, land in zip(srcs, lands)]


def _everyone_plan(srcs, lands):
    x, y, c, _ = _place()
    me = 4 * x + 2 * y + c
    return [(srcs[0], lands[0].at[me], (*_chip_peer(x, y, j >> 1), 1 - c if j & 1 else c)) for j in range(1, 8)]


def _blockdiag_in(bb):
    eye = jnp.eye(8, dtype=F32)[None, :, None, None, :, None]
    t = bb.reshape(2, NCB, 8, 16, 64).transpose(1, 2, 3, 0, 4)
    return (t[:, :, :, :, None, :] * eye).reshape(NCB, 128, SW)


def _blockdiag_in_t(dw):
    eye = jnp.eye(8, dtype=F32)[None, :, None, None, :, None]
    t = jnp.sum(dw.reshape(NCB, 8, 16, 2, 8, 64) * eye, axis=4)
    return t.transpose(3, 0, 1, 2, 4).reshape(2, 32, 16, 64)


def _blockdiag_out(cc):
    eye = jnp.eye(8, dtype=F32)[None, None, :, None, :, None]
    t = cc.reshape(2, NCB, 8, 16, 64).transpose(1, 0, 4, 2, 3)
    return (t[:, :, None, :, :, :] * eye).reshape(NCB, SW, 128)


def _blockdiag_out_t(dw):
    eye = jnp.eye(8, dtype=F32)[None, None, :, None, :, None]
    t = jnp.sum(dw.reshape(NCB, 2, 8, 64, 8, 16) * eye, axis=2)
    return t.transpose(1, 0, 3, 4, 2).reshape(2, 32, 16, 64)


def _flatten_small(parts):
    flat = jnp.concatenate([p.reshape(-1) for p in parts])
    n = flat.shape[0]
    rows = -(-n // (128 * SMALL_TILE)) * SMALL_TILE
    return jnp.pad(flat, (0, rows * 128 - n)).reshape(rows, 128)


def _unflatten_small(buf, like):
    flat = buf.reshape(-1)
    out, at = [], 0
    for p in like:
        out.append(flat[at:at + p.size].reshape(p.shape))
        at += p.size
    return out


def kernel(x, pre_norm_g, w_in, lambda_re, lambda_im, log_dt, b_re, b_im, c_re, c_im, d_skip, w_glu, b_glu, w_branch_s, w_branch_a, w_out, post_norm_g, loss_target, m_pre_norm_g, m_w_in, m_lambda_re, m_lambda_im, m_log_dt, m_b_re, m_b_im, m_c_re, m_c_im, m_d_skip, m_w_glu, m_b_glu, m_w_branch_s, m_w_branch_a, m_w_out, m_post_norm_g, v_pre_norm_g, v_w_in, v_lambda_re, v_lambda_im, v_log_dt, v_b_re, v_b_im, v_c_re, v_c_im, v_d_skip, v_w_glu, v_b_glu, v_w_branch_s, v_w_branch_a, v_w_out, v_post_norm_g):
    ng = DEPTH * 32
    lre3 = lambda_re.reshape(ng, 1, 64)
    lim3 = lambda_im.reshape(ng, 1, 64)
    ldt3 = log_dt.reshape(ng, 1, 1)
    bre_t = b_re.transpose(0, 1, 3, 2).reshape(ng, 16, 64)
    bim_t = b_im.transpose(0, 1, 3, 2).reshape(ng, 16, 64)
    ar, ai, bbr, bbi = _s5_prep(lre3, lim3, ldt3, bre_t, bim_t)
    ar4 = ar.reshape(DEPTH, NCB, 1, HW)
    ai4 = ai.reshape(DEPTH, NCB, 1, HW)
    bb = jnp.stack([bbr.reshape(DEPTH, 32, 16, 64), bbi.reshape(DEPTH, 32, 16, 64)], axis=1)
    cc = jnp.stack([c_re, -c_im], axis=1)

    chip = 2 * lax.axis_index("x") + lax.axis_index("y")
    device = 2 * chip + lax.axis_index("c")

    def gather_start(l, dep):
        srcs = [w_in[l].astype(BF16), w_glu[l].astype(BF16), w_branch_s[l].astype(BF16),
                w_branch_a[l].astype(BF16), w_out[l].astype(BF16)]
        if dep is not None:
            srcs[0] = _after(srcs[0], dep)
        return _copies_start(f"gather_start_{l}", _gather_plan, 15, srcs, [_with_own_slot(a, chip) for a in srcs])

    xs = [x[0]]
    saved = []
    flight = gather_start(0, None)
    for l in range(DEPTH):
        send, recv, srcs, lands, _ = flight
        _, (w4, wglu4, wbs4, wba4, wout4) = _copies_wait(f"gather_wait_{l}", _gather_plan, send, recv, srcs, lands,
                                                         xs[l])
        if l + 1 < DEPTH:
            flight = gather_start(l + 1, w4)
            xs[l] = _after(xs[l], flight[4])
        wglu = wglu4.reshape(SSM, SSM)
        wout = wout4.reshape(D, D)
        wb4 = _blockdiag_in(bb[l]).astype(BF16)
        wc4 = _blockdiag_out(cc[l]).astype(BF16)
        g1 = pre_norm_g[l].reshape(1, D)
        g2 = post_norm_g[l].reshape(1, D)
        dsk = d_skip[l].reshape(1, SSM)
        bgl = b_glu[l].reshape(1, SSM)
        proj = _rms_proj_fwd(xs[l], g1, w4)
        states, yraw = _s5_scan_fwd(proj, wb4, wc4, ar4[l], ai4[l])
        ys = _s5_tail_fwd(yraw, proj, dsk, wglu, bgl)
        o, lse = _attn_core_fwd(proj)
        ya = _attn_mix_fwd(o, lse, proj)
        xs.append(_merge_out_fwd(ys, ya, proj, wbs4, wba4, wout, g2, xs[l]))
        saved.append((w4, wglu, wbs4, wba4, wout, wb4, wc4, g1, g2, dsk, bgl, proj, states, yraw, ys, ya, o, lse))

    loss_part, dx = _loss_head(xs[DEPTH], loss_target[0])
    loss = lax.psum(loss_part[0, 0], ("x", "y", "c"))

    big_w = (w_in, w_glu.reshape(DEPTH, 128, SSM), w_branch_s, w_branch_a, w_out)
    big_m = (m_w_in, m_w_glu, m_w_branch_s, m_w_branch_a, m_w_out)
    big_v = (v_w_in, v_w_glu, v_w_branch_s, v_w_branch_a, v_w_out)
    big_out = [None] * 5
    small = {k: [None] * DEPTH for k in ("g1", "da", "dbb", "dcc", "dsk", "bgl", "g2")}
    ici = [None] * DEPTH
    d2d = [None] * DEPTH

    def exchange_start(l, parts):
        lands = [_with_own_slot(lax.dynamic_index_in_dim(p, chip, 0, keepdims=False), chip) for p in parts]
        ici[l] = _copies_start(f"exchange_start_{l}", _exchange_plan, 15, parts, lands)

    def handoff_start(l, after):
        send, recv, srcs, lands, _ = ici[l]
        _, own = _copies_wait(f"exchange_wait_{l}", _exchange_plan, send, recv, srcs, lands, after)
        d2d[l] = _copies_start(f"handoff_start_{l}", _sibling_plan, 5, own,
                               [lax.empty(a.shape, a.dtype) for a in own])

    def update(l, after):
        send, recv, own, lands, _ = d2d[l]
        own, sib = _copies_wait(f"handoff_wait_{l}", _sibling_plan, send, recv, own, lands, after)
        for k in range(5):
            big_out[k] = _adamw_big(l, big_w[k], big_m[k], big_v[k], own[k], sib[k], big_out[k])

    for l in reversed(range(DEPTH)):
        w4, wglu, wbs4, wba4, wout, wb4, wc4, g1, g2, dsk, bgl, proj, states, yraw, ys, ya, o, lse = saved[l]
        if l + 1 < DEPTH:
            dx = _after(dx, ici[l + 1][4], *([d2d[l + 2][4]] if l + 2 < DEPTH else []))
        dys, dya, dgs, dga, dwbs, dwba, dwout, dg2 = _merge_out_bwd(dx, ys, ya, proj, wbs4, wba4, wout, g2)
        dyraw, du_skip, dzs, dwglu, dbgl, ddsk = _s5_tail_bwd(dys, yraw, proj, dsk, wglu, bgl)
        du, dwb, dwc, da = _s5_scan_bwd(dyraw, states, proj, du_skip, wb4, wc4, ar4[l], ai4[l])
        do, cterm, dza = _attn_mix_bwd(dya, o, lse, proj)
        dq, dk, dv = _attn_core_bwd(proj, do, lse, cterm)
        dproj = jnp.concatenate([du, dzs, dq, dk, dv, dza, dgs, dga], axis=1)
        dh, dwin = _proj_bwd(dproj, xs[l], g1, w4)
        dx, dg1 = _rms_bwd(dh, xs[l], g1, dx)
        parts = [dwin, dwglu.reshape(NCHIP, 128, SSM), dwbs, dwba, dwout.reshape(NCHIP, D // NCHIP, D)]
        exchange_start(l, parts)
        if l + 2 < DEPTH:
            update(l + 2, dx)
        if l + 1 < DEPTH:
            handoff_start(l + 1, dx)
        small["g1"][l], small["g2"][l], small["dsk"][l], small["bgl"][l] = dg1, dg2, ddsk, dbgl
        small["da"][l], small["dbb"][l], small["dcc"][l] = da, _blockdiag_in_t(dwb), _blockdiag_out_t(dwc)

    da = jnp.stack(small["da"]).reshape(DEPTH, NCB, 2, HW)
    gar = da[:, :, 0].reshape(ng, 1, 64)
    gai = da[:, :, 1].reshape(ng, 1, 64)
    dbb = jnp.stack(small["dbb"])
    dcc = jnp.stack(small["dcc"])
    dlre, dlim, dldt, dbre_t, dbim_t = _s5_prep_bwd(
        lre3, lim3, ldt3, bre_t, bim_t, gar, gai, dbb[:, 0].reshape(ng, 16, 64), dbb[:, 1].reshape(ng, 16, 64))
    unt = lambda t: t.reshape(DEPTH, 32, 16, 64).transpose(0, 1, 3, 2)
    small_w = [pre_norm_g, lambda_re, lambda_im, log_dt, b_re, b_im, c_re, c_im, d_skip, b_glu, post_norm_g]
    small_m = [m_pre_norm_g, m_lambda_re, m_lambda_im, m_log_dt, m_b_re, m_b_im, m_c_re, m_c_im, m_d_skip,
               m_b_glu, m_post_norm_g]
    small_v = [v_pre_norm_g, v_lambda_re, v_lambda_im, v_log_dt, v_b_re, v_b_im, v_c_re, v_c_im, v_d_skip,
               v_b_glu, v_post_norm_g]
    small_g = [jnp.stack(small["g1"]).reshape(DEPTH, D), dlre.reshape(DEPTH, 32, 64), dlim.reshape(DEPTH, 32, 64),
               dldt.reshape(DEPTH, 32), unt(dbre_t), unt(dbim_t), dcc[:, 0], -dcc[:, 1],
               jnp.stack(small["dsk"]).reshape(DEPTH, SSM), jnp.stack(small["bgl"]).reshape(DEPTH, SSM),
               jnp.stack(small["g2"]).reshape(DEPTH, D)]
    part = _flatten_small(small_g)
    send, recv, srcs, lands, token = _copies_start("small_start", _everyone_plan, 7, [part],
                                                   [_with_own_slot(part, device, 8)])
    update(1, _after(dx, token))
    handoff_start(0, big_out[4][0])
    update(0, dx)
    _, (parts8,) = _copies_wait("small_wait", _everyone_plan, send, recv, srcs, lands, big_out[4][0])
    g_flat, d_flat, nm_flat, nv_flat = _adamw_small(_flatten_small(small_w), parts8, _flatten_small(small_m),
                                                    _flatten_small(small_v))
    sg = _unflatten_small(g_flat, small_w)
    sd = _unflatten_small(d_flat, small_w)
    snm = _unflatten_small(nm_flat, small_w)
    snv = _unflatten_small(nv_flat, small_w)

    def ordered(sm, which):
        bg = [big_out[k][which] for k in range(5)]
        bg[1] = bg[1].reshape(DEPTH, 128, SSM)
        return [sm[0], bg[0], sm[1], sm[2], sm[3], sm[4], sm[5], sm[6], sm[7], sm[8], bg[1], sm[9], bg[2], bg[3],
                bg[4], sm[10]]

    return (loss, dx[None], *ordered(sg, 0), *ordered(sd, 1), *ordered(snm, 2), *ordered(snv, 3))
```

```python
import math

import jax
import jax.numpy as jnp
from jax import lax
from jax.experimental import pallas as pl
from jax.experimental.pallas import tpu as pltpu

F32 = jnp.float32
BF16 = jnp.bfloat16
MESH = pl.DeviceIdType.MESH

DEPTH = 4
L = 2048
D = 1024
NCOL = 8192
SSM = 512
AW = 512
QKV = 1536
RMS_EPS = 1e-6
NCHIP = 4

COL_U, COL_ZS, COL_Q, COL_K, COL_V, COL_ZA, COL_GS, COL_GA = 0, 4, 8, 20, 32, 44, 48, 56

NSEG = 8
SEG = L // NSEG
NCB = 4
HW = 512
SW = 2 * HW

HEAD = 128
DILATIONS = (1, 4, 16)
SCALE = HEAD ** -0.5
NEG = -1e30

ADAM_LR, ADAM_B1, ADAM_B2, ADAM_EPS, ADAM_WD, ADAM_STEP = 0.001, 0.9, 0.999, 1e-08, 0.01, 10


def _sds(shape, dtype=F32):
    return jax.ShapeDtypeStruct(shape, dtype)


def _pcall(body, *, name, out_shape, grid=None, in_specs=None, out_specs=None, scratch=(), vmem_mb=None,
           aliases=None):
    params = {}
    if vmem_mb is not None:
        params["vmem_limit_bytes"] = vmem_mb << 20
    kw = {}
    if grid is not None:
        kw["grid"] = grid
    if in_specs is not None:
        kw["in_specs"] = in_specs
    if out_specs is not None:
        kw["out_specs"] = out_specs
    return pl.pallas_call(body, name=name, out_shape=out_shape, scratch_shapes=list(scratch),
                          compiler_params=pltpu.CompilerParams(**params),
                          input_output_aliases=aliases or {}, **kw)


def _dot(a, b):
    return jnp.dot(a, b, preferred_element_type=F32)


def _dot_nt(a, b):
    return lax.dot_general(a, b, (((1,), (1,)), ((), ())), preferred_element_type=F32)


def _dot_tn(a, b):
    return lax.dot_general(a, b, (((0,), (0,)), ((), ())), preferred_element_type=F32)


def _sigmoid(x):
    return jax.nn.sigmoid(x)


_GELU_K = math.sqrt(2.0 / math.pi)


def _gelu(x):
    return 0.5 * x * (1.0 + jnp.tanh(_GELU_K * (x + 0.044715 * (x * x * x))))


def _gelu_grad(x):
    t = jnp.tanh(_GELU_K * (x + 0.044715 * (x * x * x)))
    return 0.5 * (1.0 + t) + 0.5 * x * (1.0 - t * t) * (_GELU_K * (1.0 + 3.0 * 0.044715 * (x * x)))


def _token_operand(dep):
    if dep is None:
        return [], []
    return [dep], [pl.BlockSpec(memory_space=pl.ANY)]


def _rms_proj_fwd(x, g, w4, dep=None):
    tl, tn = 512, 1024
    deps, dep_specs = _token_operand(dep)

    def body(x_ref, g_ref, w_ref, *rest):
        o_ref = rest[-1]
        xx = x_ref[...]
        inv = lax.rsqrt(jnp.mean(xx * xx, axis=-1, keepdims=True) + RMS_EPS)
        h = (xx * inv * g_ref[...]).astype(BF16)
        o_ref[...] = _dot(h, w_ref[...])

    return _pcall(
        body, name="rms_proj_fwd", grid=(NCOL // tn, L // tl),
        in_specs=[pl.BlockSpec((tl, D), lambda j, i: (i, 0)),
                  pl.BlockSpec((1, D), lambda j, i: (0, 0)),
                  pl.BlockSpec((None, D, tn), lambda j, i: (lax.div(j, 2), 0, lax.rem(j, 2)))] + dep_specs,
        out_specs=pl.BlockSpec((tl, tn), lambda j, i: (i, j)),
        out_shape=_sds((L, NCOL)), vmem_mb=40)(x, g, w4, *deps)


def _proj_bwd(dproj, x, g, w4):
    tl, tn = 512, 1024
    ni = L // tl

    def body(dp_ref, x_ref, g_ref, w_ref, dh_ref, dw_ref, acc_ref):
        j, i = pl.program_id(0), pl.program_id(1)
        xx = x_ref[...]
        inv = lax.rsqrt(jnp.mean(xx * xx, axis=-1, keepdims=True) + RMS_EPS)
        h = (xx * inv * g_ref[...]).astype(BF16)
        dp = dp_ref[...].astype(BF16)
        part = _dot_tn(h, dp)

        @pl.when(i == 0)
        def _():
            acc_ref[...] = part

        @pl.when(i > 0)
        def _():
            acc_ref[...] += part

        @pl.when(i == ni - 1)
        def _():
            dw_ref[...] = acc_ref[...].astype(BF16)

        rows = pl.ds(pl.multiple_of(i * tl, tl), tl)
        dh = _dot_nt(dp, w_ref[...])

        @pl.when(j == 0)
        def _():
            dh_ref[rows, :] = dh

        @pl.when(j > 0)
        def _():
            dh_ref[rows, :] += dh

    return _pcall(
        body, name="proj_bwd", grid=(NCOL // tn, ni),
        in_specs=[pl.BlockSpec((tl, tn), lambda j, i: (i, j)),
                  pl.BlockSpec((tl, D), lambda j, i: (i, 0)),
                  pl.BlockSpec((1, D), lambda j, i: (0, 0)),
                  pl.BlockSpec((None, D, tn), lambda j, i: (lax.div(j, 2), 0, lax.rem(j, 2)))],
        out_specs=[pl.BlockSpec((L, D), lambda j, i: (0, 0)),
                   pl.BlockSpec((None, D, tn), lambda j, i: (lax.div(j, 2), 0, lax.rem(j, 2)))],
        out_shape=[_sds((L, D)), _sds((NCHIP, D, NCOL // NCHIP), BF16)],
        scratch=[pltpu.VMEM((D, tn), F32)], vmem_mb=56)(dproj, x, g, w4)


def _rms_bwd(dh, x, g, dxn):
    tl = 512

    def body(dh_ref, x_ref, g_ref, dxn_ref, dx_ref, dg_ref):
        i = pl.program_id(0)
        xx = x_ref[...]
        inv = lax.rsqrt(jnp.mean(xx * xx, axis=-1, keepdims=True) + RMS_EPS)
        nrm = xx * inv
        dh_v = dh_ref[...]
        dn = dh_v * g_ref[...]
        dx_ref[...] = dxn_ref[...] + inv * (dn - nrm * jnp.mean(dn * nrm, axis=-1, keepdims=True))
        part = jnp.sum(dh_v * nrm, axis=0, keepdims=True)

        @pl.when(i == 0)
        def _():
            dg_ref[...] = part

        @pl.when(i > 0)
        def _():
            dg_ref[...] += part

    row = pl.BlockSpec((tl, D), lambda i: (i, 0))
    vec = pl.BlockSpec((1, D), lambda i: (0, 0))
    return _pcall(body, name="rms_bwd", grid=(L // tl,), in_specs=[row, row, vec, row],
                  out_specs=[row, vec], out_shape=[_sds((L, D)), _sds((1, D))], vmem_mb=40)(dh, x, g, dxn)


def _s5_param_math(lre, lim, ldt, bre, bim):
    lr = jnp.minimum(lre, -1e-4)
    dt = jnp.exp(ldt)
    mag = jnp.exp(lr * dt)
    ar = mag * jnp.cos(lim * dt)
    ai = mag * jnp.sin(lim * dt)
    den = lr * lr + lim * lim
    nr = ar - 1.0
    qr = (nr * lr + ai * lim) / den
    qi = (ai * lr - nr * lim) / den
    return ar, ai, qr * bre - qi * bim, qr * bim + qi * bre


def _s5_prep(lre, lim, ldt, bre, bim):
    n = lre.shape[0]

    def body(lre_ref, lim_ref, ldt_ref, bre_ref, bim_ref, ar_ref, ai_ref, bbr_ref, bbi_ref):
        ar, ai, bbr, bbi = _s5_param_math(lre_ref[...], lim_ref[...], ldt_ref[...], bre_ref[...], bim_ref[...])
        ar_ref[...] = ar
        ai_ref[...] = ai
        bbr_ref[...] = bbr
        bbi_ref[...] = bbi

    return _pcall(body, name="s5_prep",
                  out_shape=[_sds((n, 1, 64)), _sds((n, 1, 64)), _sds((n, 16, 64)), _sds((n, 16, 64))])(
                      lre, lim, ldt, bre, bim)


def _s5_prep_bwd(lre, lim, ldt, bre, bim, gar, gai, gbbr, gbbi):
    n = lre.shape[0]

    def body(lre_ref, lim_ref, ldt_ref, bre_ref, bim_ref, gar_ref, gai_ref, gbbr_ref, gbbi_ref,
             dlre_ref, dlim_ref, dldt_ref, dbre_ref, dbim_ref):
        _, vjp = jax.vjp(_s5_param_math, lre_ref[...], lim_ref[...], ldt_ref[...], bre_ref[...], bim_ref[...])
        dlre, dlim, dldt, dbre, dbim = vjp((gar_ref[...], gai_ref[...], gbbr_ref[...], gbbi_ref[...]))
        dlre_ref[...] = dlre
        dlim_ref[...] = dlim
        dldt_ref[...] = dldt
        dbre_ref[...] = dbre
        dbim_ref[...] = dbim

    return _pcall(body, name="s5_prep_bwd",
                  out_shape=[_sds((n, 1, 64)), _sds((n, 1, 64)), _sds((n, 1, 1)), _sds((n, 16, 64)),
                             _sds((n, 16, 64))])(lre, lim, ldt, bre, bim, gar, gai, gbbr, gbbi)


def _permute_rows(dst_ref, src_ref):
    for s in range(NSEG):
        dst_ref[pl.ds(s, SEG, stride=NSEG), :] = src_ref[pl.ds(s * SEG, SEG), :]


def _unpermute_rows(src_ref, s):
    return src_ref[pl.ds(s, SEG, stride=NSEG), :]


def _row_of(slab, rows, j):
    return jnp.sum(jnp.where(rows == j, slab, 0.0), axis=0, keepdims=True)


def _s5_scan_fwd(proj, wb4, wc4, ar4, ai4):
    def body(u_ref, wb_ref, wc_ref, ar_ref, ai_ref, x_ref, y_ref, up_ref, d_ref, yp_ref):
        _permute_rows(up_ref, u_ref)
        d_ref[...] = _dot(up_ref[...].astype(BF16), wb_ref[...])
        ar = jnp.broadcast_to(ar_ref[...], (NSEG, HW))
        ai = jnp.broadcast_to(ai_ref[...], (NSEG, HW))
        zero = jnp.zeros((NSEG, HW), F32)

        def drive(t):
            row = pl.multiple_of(t * NSEG, NSEG)
            dd = d_ref[pl.ds(row, NSEG), :]
            return row, dd[:, :HW], dd[:, HW:]

        def local_step(t, c):
            xr, xi = c
            _, dr, di = drive(t)
            return ar * xr - ai * xi + dr, ar * xi + ai * xr + di

        fr, fi = lax.fori_loop(0, SEG, local_step, (zero, zero), unroll=8)
        pr, pi_ = ar_ref[...], ai_ref[...]
        for _ in range(int(math.log2(SEG))):
            pr, pi_ = pr * pr - pi_ * pi_, 2.0 * pr * pi_
        rows = lax.broadcasted_iota(jnp.int32, (NSEG, HW), 0)
        cr, ci = zero, zero
        for j in range(NSEG - 1):
            sr, si = _row_of(cr, rows, j), _row_of(ci, rows, j)
            gr, gi = _row_of(fr, rows, j), _row_of(fi, rows, j)
            nr = pr * sr - pi_ * si + gr
            ni = pr * si + pi_ * sr + gi
            cr = jnp.where(rows == j + 1, nr, cr)
            ci = jnp.where(rows == j + 1, ni, ci)

        def true_step(t, c):
            xr, xi = c
            row, dr, di = drive(t)
            nr = ar * xr - ai * xi + dr
            ni = ar * xi + ai * xr + di
            x_ref[pl.ds(row, NSEG), :] = jnp.concatenate([nr, ni], axis=1)
            return nr, ni

        lax.fori_loop(0, SEG, true_step, (cr, ci), unroll=8)
        yp_ref[...] = _dot(x_ref[...].astype(BF16), wc_ref[...])
        for s in range(NSEG):
            y_ref[pl.ds(s * SEG, SEG), :] = _unpermute_rows(yp_ref, s)

    return _pcall(
        body, name="s5_scan_fwd", grid=(NCB,),
        in_specs=[pl.BlockSpec((L, 128), lambda b: (0, COL_U + b)),
                  pl.BlockSpec((None, 128, SW), lambda b: (b, 0, 0)),
                  pl.BlockSpec((None, SW, 128), lambda b: (b, 0, 0)),
                  pl.BlockSpec((None, 1, HW), lambda b: (b, 0, 0)),
                  pl.BlockSpec((None, 1, HW), lambda b: (b, 0, 0))],
        out_specs=[pl.BlockSpec((None, L, SW), lambda b: (b, 0, 0)),
                   pl.BlockSpec((L, 128), lambda b: (0, b))],
        out_shape=[_sds((NCB, L, SW)), _sds((L, SSM))],
        scratch=[pltpu.VMEM((L, 128), F32), pltpu.VMEM((L, SW), F32), pltpu.VMEM((L, 128), F32)],
        vmem_mb=56)(proj, wb4, wc4, ar4, ai4)


def _s5_scan_bwd(dy, xs, proj, du_skip, wb4, wc4, ar4, ai4, dproj):
    def body(dy_ref, x_ref, u_ref, dus_ref, wb_ref, wc_ref, ar_ref, ai_ref, _,
             du_ref, dwb_ref, dwc_ref, da_ref, dyp_ref, up_ref, g_ref, dup_ref):
        _permute_rows(dyp_ref, dy_ref)
        _permute_rows(up_ref, u_ref)
        dyp = dyp_ref[...].astype(BF16)
        g_ref[...] = _dot_nt(dyp, wc_ref[...])
        dwc_ref[...] = _dot_tn(x_ref[...].astype(BF16), dyp)
        ar = jnp.broadcast_to(ar_ref[...], (NSEG, HW))
        ai = jnp.broadcast_to(ai_ref[...], (NSEG, HW))
        zero = jnp.zeros((NSEG, HW), F32)

        def load(t):
            row = pl.multiple_of(t * NSEG, NSEG)
            gg = g_ref[pl.ds(row, NSEG), :]
            return row, gg[:, :HW], gg[:, HW:]

        def local_step(k, c):
            lr, li = c
            _, gr, gi = load(SEG - 1 - k)
            return ar * lr + ai * li + gr, ar * li - ai * lr + gi

        fr, fi = lax.fori_loop(0, SEG, local_step, (zero, zero), unroll=8)
        pr, pi_ = ar_ref[...], -ai_ref[...]
        for _ in range(int(math.log2(SEG))):
            pr, pi_ = pr * pr - pi_ * pi_, 2.0 * pr * pi_
        rows = lax.broadcasted_iota(jnp.int32, (NSEG, HW), 0)
        cr, ci = zero, zero
        for j in range(NSEG - 1, 0, -1):
            sr, si = _row_of(cr, rows, j), _row_of(ci, rows, j)
            gr, gi = _row_of(fr, rows, j), _row_of(fi, rows, j)
            nr = pr * sr - pi_ * si + gr
            ni = pr * si + pi_ * sr + gi
            cr = jnp.where(rows == j - 1, nr, cr)
            ci = jnp.where(rows == j - 1, ni, ci)

        def true_step(k, c):
            lr, li, gar, gai = c
            t = SEG - 1 - k
            row, gr, gi = load(t)
            nr = ar * lr + ai * li + gr
            ni = ar * li - ai * lr + gi
            g_ref[pl.ds(row, NSEG), :] = jnp.concatenate([nr, ni], axis=1)
            prow = pl.multiple_of(jnp.maximum(t - 1, 0) * NSEG, NSEG)
            xp = x_ref[pl.ds(prow, NSEG), :]
            live = (t > 0).astype(F32)
            xr, xi = xp[:, :HW] * live, xp[:, HW:] * live
            return nr, ni, gar + (nr * xr + ni * xi), gai + (ni * xr - nr * xi)

        _, _, gar, gai = lax.fori_loop(0, SEG, true_step, (cr, ci, zero, zero), unroll=4)
        l0 = g_ref[1:NSEG, :]
        xl = x_ref[(SEG - 1) * NSEG:(SEG - 1) * NSEG + NSEG - 1, :]
        l0r, l0i, xlr, xli = l0[:, :HW], l0[:, HW:], xl[:, :HW], xl[:, HW:]
        gar_t = jnp.sum(gar, axis=0, keepdims=True) + jnp.sum(l0r * xlr + l0i * xli, axis=0, keepdims=True)
        gai_t = jnp.sum(gai, axis=0, keepdims=True) + jnp.sum(l0i * xlr - l0r * xli, axis=0, keepdims=True)
        da_ref[...] = jnp.concatenate([gar_t, gai_t], axis=1)
        lam = g_ref[...].astype(BF16)
        dwb_ref[...] = _dot_tn(up_ref[...].astype(BF16), lam)
        dup_ref[...] = _dot_nt(lam, wb_ref[...])
        for s in range(NSEG):
            sl = pl.ds(s * SEG, SEG)
            du_ref[sl, :] = _unpermute_rows(dup_ref, s) + dus_ref[sl, :]

    col = lambda off: pl.BlockSpec((L, 128), lambda b: (0, off + b))
    return _pcall(
        body, name="s5_scan_bwd", grid=(NCB,),
        in_specs=[col(0), pl.BlockSpec((None, L, SW), lambda b: (b, 0, 0)), col(COL_U), col(0),
                  pl.BlockSpec((None, 128, SW), lambda b: (b, 0, 0)),
                  pl.BlockSpec((None, SW, 128), lambda b: (b, 0, 0)),
                  pl.BlockSpec((None, 1, HW), lambda b: (b, 0, 0)),
                  pl.BlockSpec((None, 1, HW), lambda b: (b, 0, 0)),
                  pl.BlockSpec(memory_space=pl.ANY)],
        out_specs=[col(COL_U), pl.BlockSpec((None, 128, SW), lambda b: (b, 0, 0)),
                   pl.BlockSpec((None, SW, 128), lambda b: (b, 0, 0)),
                   pl.BlockSpec((None, 1, SW), lambda b: (b, 0, 0))],
        out_shape=[_sds((L, NCOL)), _sds((NCB, 128, SW)), _sds((NCB, SW, 128)), _sds((NCB, 1, SW))],
        scratch=[pltpu.VMEM((L, 128), F32), pltpu.VMEM((L, 128), F32), pltpu.VMEM((L, SW), F32),
                 pltpu.VMEM((L, 128), F32)],
        aliases={8: 0}, vmem_mb=56)(dy, xs, proj, du_skip, wb4, wc4, ar4, ai4, dproj)


def _s5_tail_fwd(yraw, proj, dsk, wglu, bglu):
    tl = 512

    def body(y_ref, u_ref, z_ref, dsk_ref, w_ref, b_ref, o_ref):
        y1 = y_ref[...] + dsk_ref[...] * u_ref[...]
        y2 = _gelu(y1)
        gl = _dot(y2.astype(BF16), w_ref[...]) + b_ref[...]
        z = z_ref[...]
        o_ref[...] = (y2 * _sigmoid(gl)) * (z * _sigmoid(z))

    blk = lambda c: pl.BlockSpec((tl, SSM), lambda i: (i, c))
    vec = pl.BlockSpec((1, SSM), lambda i: (0, 0))
    return _pcall(body, name="s5_tail_fwd", grid=(L // tl,),
                  in_specs=[blk(0), blk(0), blk(1), vec, pl.BlockSpec((SSM, SSM), lambda i: (0, 0)), vec],
                  out_specs=blk(0), out_shape=_sds((L, SSM)), vmem_mb=40)(yraw, proj, proj, dsk, wglu, bglu)


def _s5_tail_bwd(dys, yraw, proj, dsk, wglu, bglu, dproj):
    tl = 512
    ni = L // tl

    def body(dys_ref, y_ref, u_ref, z_ref, dsk_ref, w_ref, b_ref, _,
             dy_ref, dus_ref, dz_ref, dw_ref, db_ref, dd_ref, acc_ref):
        i = pl.program_id(0)
        u = u_ref[...]
        y1 = y_ref[...] + dsk_ref[...] * u
        y2 = _gelu(y1)
        y2b = y2.astype(BF16)
        sg = _sigmoid(_dot(y2b, w_ref[...]) + b_ref[...])
        y3 = y2 * sg
        z = z_ref[...]
        sz = _sigmoid(z)
        dys = dys_ref[...]
        dy3 = dys * (z * sz)
        dz_ref[...] = dys * y3 * (sz * (1.0 + z * (1.0 - sz)))
        dgl = (dy3 * y2) * (sg * (1.0 - sg))
        dglb = dgl.astype(BF16)
        dy2 = dy3 * sg + _dot_nt(dglb, w_ref[...])
        dy1 = dy2 * _gelu_grad(y1)
        dy_ref[...] = dy1
        dus_ref[...] = dsk_ref[...] * dy1
        dw = _dot_tn(y2b, dglb)
        db = jnp.sum(dgl, axis=0, keepdims=True)
        dd = jnp.sum(dy1 * u, axis=0, keepdims=True)

        @pl.when(i == 0)
        def _():
            acc_ref[...] = dw
            db_ref[...] = db
            dd_ref[...] = dd

        @pl.when(i > 0)
        def _():
            acc_ref[...] += dw
            db_ref[...] += db
            dd_ref[...] += dd

        @pl.when(i == ni - 1)
        def _():
            dw_ref[...] = acc_ref[...].astype(BF16)

    blk = lambda c: pl.BlockSpec((tl, SSM), lambda i: (i, c))
    vec = pl.BlockSpec((1, SSM), lambda i: (0, 0))
    mat = pl.BlockSpec((SSM, SSM), lambda i: (0, 0))
    return _pcall(body, name="s5_tail_bwd", grid=(ni,),
                  in_specs=[blk(0), blk(0), blk(0), blk(1), vec, mat, vec, pl.BlockSpec(memory_space=pl.ANY)],
                  out_specs=[blk(0), blk(0), blk(COL_ZS // 4), mat, vec, vec],
                  out_shape=[_sds((L, SSM)), _sds((L, SSM)), _sds((L, NCOL)), _sds((SSM, SSM), BF16),
                             _sds((1, SSM)), _sds((1, SSM))],
                  scratch=[pltpu.VMEM((SSM, SSM), F32)], aliases={7: 2},
                  vmem_mb=40)(dys, yraw, proj, proj, dsk, wglu, bglu, dproj)


def _attn_blocks(dil):
    nb = L // dil // HEAD
    return [(i * HEAD * dil + r, (i - 1) * HEAD * dil + r if i > 0 else None)
            for r in range(dil) for i in range(nb)]


def _rows(start, dil):
    return pl.ds(start, HEAD) if dil == 1 else pl.ds(start, HEAD, stride=dil)


def _band_masks():
    ri = lax.broadcasted_iota(jnp.int32, (HEAD, HEAD), 0)
    ci = lax.broadcasted_iota(jnp.int32, (HEAD, HEAD), 1)
    return ci <= ri, ci >= ri


def _qkv_specs(index):
    def spec(off):
        return pl.BlockSpec((L, HEAD), lambda gi, h: (0, off + index(gi, h)))
    return [spec(off) for off in (COL_Q, COL_K, COL_V)]


def _attn_core_fwd(proj):
    def body(q_ref, k_ref, v_ref, o_ref, lse_ref):
        gi = pl.program_id(0)
        m_cur, m_prev = _band_masks()
        for g, dil in enumerate(DILATIONS):
            @pl.when(gi == g)
            def _(dil=dil):
                for st, pst in _attn_blocks(dil):
                    r = _rows(st, dil)
                    q = q_ref[r, :].astype(BF16)
                    s_c = jnp.where(m_cur, _dot_nt(q, k_ref[r, :].astype(BF16)) * SCALE, NEG)
                    mx = jnp.max(s_c, axis=-1, keepdims=True)
                    if pst is not None:
                        rp = _rows(pst, dil)
                        s_p = jnp.where(m_prev, _dot_nt(q, k_ref[rp, :].astype(BF16)) * SCALE, NEG)
                        mx = jnp.maximum(mx, jnp.max(s_p, axis=-1, keepdims=True))
                    p_c = jnp.exp(s_c - mx)
                    den = jnp.sum(p_c, axis=-1, keepdims=True)
                    acc = _dot(p_c.astype(BF16), v_ref[r, :].astype(BF16))
                    if pst is not None:
                        p_p = jnp.exp(s_p - mx)
                        den = den + jnp.sum(p_p, axis=-1, keepdims=True)
                        acc = acc + _dot(p_p.astype(BF16), v_ref[rp, :].astype(BF16))
                    o_ref[r, :] = acc / den
                    lse_ref[r, :] = jnp.broadcast_to(mx + jnp.log(den), (HEAD, HEAD))

    idx = lambda gi, h: gi * 4 + h
    out = pl.BlockSpec((L, HEAD), lambda gi, h: (0, gi * 4 + h))
    return _pcall(body, name="attn_core_fwd", grid=(3, 4), in_specs=_qkv_specs(idx), out_specs=[out, out],
                  out_shape=[_sds((L, QKV)), _sds((L, QKV))], vmem_mb=40)(proj, proj, proj)


def _attn_mix_fwd(o, lse, proj):
    tl = 512

    def body(o_ref, l_ref, z_ref, y_ref):
        for h in range(4):
            c = [pl.ds((g * 4 + h) * HEAD, HEAD) for g in range(3)]
            ls = [l_ref[:, c[g]] for g in range(3)]
            m = jnp.maximum(jnp.maximum(ls[0], ls[1]), ls[2])
            e = [jnp.exp(ls[g] - m) for g in range(3)]
            y = (e[0] * o_ref[:, c[0]] + e[1] * o_ref[:, c[1]] + e[2] * o_ref[:, c[2]]) / (e[0] + e[1] + e[2])
            z = z_ref[:, pl.ds(h * HEAD, HEAD)]
            y_ref[:, pl.ds(h * HEAD, HEAD)] = y * (z * _sigmoid(z))

    wide = pl.BlockSpec((tl, QKV), lambda i: (i, 0))
    return _pcall(body, name="attn_mix_fwd", grid=(L // tl,),
                  in_specs=[wide, wide, pl.BlockSpec((tl, AW), lambda i: (i, COL_ZA // 4))],
                  out_specs=pl.BlockSpec((tl, AW), lambda i: (i, 0)), out_shape=_sds((L, AW)),
                  vmem_mb=40)(o, lse, proj)


def _attn_mix_bwd(dya, o, lse, proj, dproj):
    tl = 512

    def body(dya_ref, o_ref, l_ref, z_ref, _, do_ref, c_ref, dz_ref):
        for h in range(4):
            c = [pl.ds((g * 4 + h) * HEAD, HEAD) for g in range(3)]
            hs = pl.ds(h * HEAD, HEAD)
            ls = [l_ref[:, c[g]] for g in range(3)]
            m = jnp.maximum(jnp.maximum(ls[0], ls[1]), ls[2])
            e = [jnp.exp(ls[g] - m) for g in range(3)]
            den = e[0] + e[1] + e[2]
            al = [e[g] / den for g in range(3)]
            y = al[0] * o_ref[:, c[0]] + al[1] * o_ref[:, c[1]] + al[2] * o_ref[:, c[2]]
            z = z_ref[:, hs]
            sz = _sigmoid(z)
            dya = dya_ref[:, hs]
            dz_ref[:, hs] = dya * y * (sz * (1.0 + z * (1.0 - sz)))
            dy = dya * (z * sz)
            tot = jnp.sum(dy * y, axis=-1, keepdims=True)
            for g in range(3):
                do_ref[:, c[g]] = al[g] * dy
                c_ref[:, c[g]] = -(al[g] * tot)

    wide = pl.BlockSpec((tl, QKV), lambda i: (i, 0))
    nar = pl.BlockSpec((tl, AW), lambda i: (i, 0))
    za = pl.BlockSpec((tl, AW), lambda i: (i, COL_ZA // 4))
    return _pcall(body, name="attn_mix_bwd", grid=(L // tl,),
                  in_specs=[nar, wide, wide, za, pl.BlockSpec(memory_space=pl.ANY)],
                  out_specs=[wide, wide, za], out_shape=[_sds((L, QKV)), _sds((L, QKV)), _sds((L, NCOL))],
                  aliases={4: 2}, vmem_mb=48)(dya, o, lse, proj, dproj)


def _attn_core_bwd(proj, do, lse, cc, dproj):
    def body(q_ref, k_ref, v_ref, do_ref, lse_ref, c_ref, _, dp_ref, dq_ref, dk_ref, dv_ref, sems):
        gi = pl.program_id(0)
        head = gi * 4 + pl.program_id(1)
        m_cur, m_prev = _band_masks()
        dk_ref[...] = jnp.zeros((L, HEAD), F32)
        dv_ref[...] = jnp.zeros((L, HEAD), F32)
        for g, dil in enumerate(DILATIONS):
            @pl.when(gi == g)
            def _(dil=dil):
                for st, pst in _attn_blocks(dil):
                    r = _rows(st, dil)
                    q = q_ref[r, :].astype(BF16)
                    do_b = do_ref[r, :].astype(BF16)
                    lse_b = lse_ref[r, :]
                    c_b = c_ref[r, :]
                    dq = None
                    for rr, mask in ((r, m_cur),) + (((_rows(pst, dil), m_prev),) if pst is not None else ()):
                        kb = k_ref[rr, :].astype(BF16)
                        s = _dot_nt(q, kb) * SCALE
                        p = jnp.where(mask, jnp.exp(s - lse_b), 0.0)
                        dp = _dot_nt(do_b, v_ref[rr, :].astype(BF16))
                        ds = (p * (dp + c_b) * SCALE).astype(BF16)
                        part = _dot(ds, kb)
                        dq = part if dq is None else dq + part
                        dk_ref[rr, :] += _dot_tn(ds, q)
                        dv_ref[rr, :] += _dot_tn(p.astype(BF16), do_b)
                    dq_ref[r, :] = dq

        copies = [pltpu.make_async_copy(src, dp_ref.at[:, pl.ds(pl.multiple_of((off + head) * HEAD, HEAD), HEAD)],
                                        sems.at[n])
                  for n, (src, off) in enumerate(((dq_ref, COL_Q), (dk_ref, COL_K), (dv_ref, COL_V)))]
        for cp in copies:
            cp.start()
        for cp in copies:
            cp.wait()

    idx = lambda gi, h: gi * 4 + h
    blk = pl.BlockSpec((L, HEAD), lambda gi, h: (0, gi * 4 + h))
    hbm = pl.BlockSpec(memory_space=pl.ANY)
    return _pcall(body, name="attn_core_bwd", grid=(3, 4), in_specs=_qkv_specs(idx) + [blk, blk, blk, hbm],
                  out_specs=hbm, out_shape=_sds((L, NCOL)),
                  scratch=[pltpu.VMEM((L, HEAD), F32)] * 3 + [pltpu.SemaphoreType.DMA((3,))],
                  aliases={6: 0}, vmem_mb=48)(proj, proj, proj, do, lse, cc, dproj)


def _merge_fwd_math(ys_b, ya_b, gs, ga, wbs_ref, wba_ref, wout_ref):
    bs = jnp.concatenate([_dot(ys_b, wbs_ref[s]) for s in range(NCHIP)], axis=1)
    ba = jnp.concatenate([_dot(ya_b, wba_ref[s]) for s in range(NCHIP)], axis=1)
    sgs, sga = _sigmoid(gs), _sigmoid(ga)
    merged = sgs * bs + sga * ba
    out = _dot(merged.astype(BF16), wout_ref[...])
    inv = lax.rsqrt(jnp.mean(out * out, axis=-1, keepdims=True) + RMS_EPS)
    return bs, ba, sgs, sga, merged, out, inv


def _merge_specs(tl):
    row = lambda w, c: pl.BlockSpec((tl, w), lambda i: (i, c))
    return [row(SSM, 0), row(AW, 0), row(D, COL_GS // 8), row(D, COL_GA // 8),
            pl.BlockSpec((NCHIP, SSM, D // NCHIP), lambda i: (0, 0, 0)),
            pl.BlockSpec((NCHIP, AW, D // NCHIP), lambda i: (0, 0, 0)),
            pl.BlockSpec((D, D), lambda i: (0, 0)),
            pl.BlockSpec((1, D), lambda i: (0, 0))]


def _merge_out_fwd(ys, ya, proj, wbs4, wba4, wout, g2, x):
    tl = 256

    def body(ys_ref, ya_ref, gs_ref, ga_ref, wbs_ref, wba_ref, wout_ref, g2_ref, x_ref, o_ref):
        *_, out, inv = _merge_fwd_math(ys_ref[...].astype(BF16), ya_ref[...].astype(BF16), gs_ref[...],
                                       ga_ref[...], wbs_ref, wba_ref, wout_ref)
        o_ref[...] = x_ref[...] + out * inv * g2_ref[...]

    row = pl.BlockSpec((tl, D), lambda i: (i, 0))
    return _pcall(body, name="merge_out_fwd", grid=(L // tl,), in_specs=_merge_specs(tl) + [row],
                  out_specs=row, out_shape=_sds((L, D)), vmem_mb=48)(ys, ya, proj, proj, wbs4, wba4, wout, g2, x)


def _merge_out_bwd(dxn, ys, ya, proj, wbs4, wba4, wout, g2, dep=None):
    tl = 256
    ni = L // tl
    cw = D // NCHIP
    deps, dep_specs = _token_operand(dep)

    def body(dxn_ref, ys_ref, ya_ref, gs_ref, ga_ref, wbs_ref, wba_ref, wout_ref, g2_ref, *rest):
        (dys_ref, dya_ref, dgate_ref, dwbs_ref, dwba_ref, dwout_ref, dg2_ref,
         abs_ref, aba_ref, aout_ref) = rest[len(deps):]
        i = pl.program_id(0)
        ys_b, ya_b = ys_ref[...].astype(BF16), ya_ref[...].astype(BF16)
        bs, ba, sgs, sga, merged, out, inv = _merge_fwd_math(ys_b, ya_b, gs_ref[...], ga_ref[...],
                                                             wbs_ref, wba_ref, wout_ref)
        nrm = out * inv
        dxn = dxn_ref[...]
        dg2 = jnp.sum(dxn * nrm, axis=0, keepdims=True)
        dn = dxn * g2_ref[...]
        dout = (inv * (dn - nrm * jnp.mean(dn * nrm, axis=-1, keepdims=True))).astype(BF16)
        dwout = _dot_tn(merged.astype(BF16), dout)
        dm = _dot_nt(dout, wout_ref[...])
        dbs, dba = dm * sgs, dm * sga
        dgate_ref[:, :D] = dm * bs * (sgs * (1.0 - sgs))
        dgate_ref[:, D:] = dm * ba * (sga * (1.0 - sga))
        dbs_b, dba_b = dbs.astype(BF16), dba.astype(BF16)
        dys = None
        dya = None
        first = i == 0
        for s in range(NCHIP):
            cs = slice(s * cw, (s + 1) * cw)
            p_s = _dot_nt(dbs_b[:, cs], wbs_ref[s])
            p_a = _dot_nt(dba_b[:, cs], wba_ref[s])
            dys = p_s if dys is None else dys + p_s
            dya = p_a if dya is None else dya + p_a
            w_s = _dot_tn(ys_b, dbs_b[:, cs])
            w_a = _dot_tn(ya_b, dba_b[:, cs])

            @pl.when(first)
            def _(s=s, w_s=w_s, w_a=w_a):
                abs_ref[s] = w_s
                aba_ref[s] = w_a

            @pl.when(jnp.logical_not(first))
            def _(s=s, w_s=w_s, w_a=w_a):
                abs_ref[s] += w_s
                aba_ref[s] += w_a

        dys_ref[...] = dys
        dya_ref[...] = dya

        @pl.when(first)
        def _():
            aout_ref[...] = dwout
            dg2_ref[...] = dg2

        @pl.when(jnp.logical_not(first))
        def _():
            aout_ref[...] += dwout
            dg2_ref[...] += dg2

        @pl.when(i == ni - 1)
        def _():
            dwbs_ref[...] = abs_ref[...].astype(BF16)
            dwba_ref[...] = aba_ref[...].astype(BF16)
            dwout_ref[...] = aout_ref[...].astype(BF16)

    row = lambda w: pl.BlockSpec((tl, w), lambda i: (i, 0))
    w4 = pl.BlockSpec((NCHIP, SSM, cw), lambda i: (0, 0, 0))
    sq = pl.BlockSpec((D, D), lambda i: (0, 0))
    vec = pl.BlockSpec((1, D), lambda i: (0, 0))
    return _pcall(body, name="merge_out_bwd", grid=(ni,), in_specs=[row(D)] + _merge_specs(tl) + dep_specs,
                  out_specs=[row(SSM), row(AW), pl.BlockSpec((tl, 2 * D), lambda i: (i, COL_GS // 16)), w4, w4, sq,
                             vec],
                  out_shape=[_sds((L, SSM)), _sds((L, AW)), _sds((L, NCOL)),
                             _sds((NCHIP, SSM, cw), BF16), _sds((NCHIP, AW, cw), BF16), _sds((D, D), BF16),
                             _sds((1, D))],
                  scratch=[pltpu.VMEM((NCHIP, SSM, cw), F32), pltpu.VMEM((NCHIP, AW, cw), F32),
                           pltpu.VMEM((D, D), F32)],
                  vmem_mb=56)(dxn, ys, ya, proj, proj, wbs4, wba4, wout, g2, *deps)


def _loss_head(y, target):
    tl = 512

    def body(y_ref, t_ref, loss_ref, dy_ref):
        i = pl.program_id(0)
        err = y_ref[...] - t_ref[...]
        dy_ref[...] = err / D
        part = 0.5 * jnp.sum(jnp.mean(err * err, axis=-1, keepdims=True), axis=0, keepdims=True)
        part = jnp.broadcast_to(part, (8, 128))

        @pl.when(i == 0)
        def _():
            loss_ref[...] = part

        @pl.when(i > 0)
        def _():
            loss_ref[...] += part

    row = pl.BlockSpec((tl, D), lambda i: (i, 0))
    return _pcall(body, name="loss_head", grid=(L // tl,), in_specs=[row, row],
                  out_specs=[pl.BlockSpec((8, 128), lambda i: (0, 0)), row],
                  out_shape=[_sds((8, 128)), _sds((L, D))], vmem_mb=40)(y, target)


def _adamw_math(w, g, m, v):
    m = ADAM_B1 * m + (1.0 - ADAM_B1) * g
    v = ADAM_B2 * v + (1.0 - ADAM_B2) * (g * g)
    m_hat = m / (1.0 - ADAM_B1 ** ADAM_STEP)
    v_hat = v / (1.0 - ADAM_B2 ** ADAM_STEP)
    delta = -ADAM_LR * (m_hat / (jnp.sqrt(v_hat) + ADAM_EPS) + ADAM_WD * w)
    return delta, m, v


def _adamw_big(layer, w, m, v, own, sib, prev):
    _, r, c = w.shape
    tr = min(r, 128)

    def body(w_ref, m_ref, v_ref, own_ref, sib_ref, *rest):
        g_ref, d_ref, nm_ref, nv_ref = rest[-4:]
        a = own_ref[0].astype(F32)
        b = sib_ref[0].astype(F32)
        for s in range(1, NCHIP):
            a = a + own_ref[s].astype(F32)
            b = b + sib_ref[s].astype(F32)
        g = a + b
        delta, nm, nv = _adamw_math(w_ref[...], g, m_ref[...], v_ref[...])
        g_ref[...] = g
        d_ref[...] = delta
        nm_ref[...] = nm
        nv_ref[...] = nv

    lay = pl.BlockSpec((None, tr, c), lambda i: (layer, i, 0))
    slots = pl.BlockSpec((NCHIP, tr, c), lambda i: (0, i, 0))
    ins = [w, m, v, own, sib]
    in_specs = [lay, lay, lay, slots, slots]
    aliases = {}
    if prev is not None:
        ins += list(prev)
        in_specs += [pl.BlockSpec(memory_space=pl.ANY)] * 4
        aliases = {5 + k: k for k in range(4)}
    return _pcall(body, name="adamw_big", grid=(r // tr,), in_specs=in_specs, out_specs=[lay] * 4,
                  out_shape=[_sds(w.shape)] * 4, aliases=aliases, vmem_mb=48)(*ins)


SMALL_TILE = 512


def _adamw_small(w, parts, m, v):
    r = w.shape[0]

    def body(w_ref, p_ref, m_ref, v_ref, g_ref, d_ref, nm_ref, nv_ref):
        g = p_ref[0]
        for dev in range(1, 8):
            g = g + p_ref[dev]
        delta, nm, nv = _adamw_math(w_ref[...], g, m_ref[...], v_ref[...])
        g_ref[...] = g
        d_ref[...] = delta
        nm_ref[...] = nm
        nv_ref[...] = nv

    row = pl.BlockSpec((SMALL_TILE, 128), lambda i: (i, 0))
    return _pcall(body, name="adamw_small", grid=(r // SMALL_TILE,),
                  in_specs=[row, pl.BlockSpec((8, SMALL_TILE, 128), lambda i: (0, i, 0)), row, row],
                  out_specs=[row] * 4, out_shape=[_sds((r, 128))] * 4, vmem_mb=40)(w, parts, m, v)


def _place():
    x, y, c = lax.axis_index("x"), lax.axis_index("y"), lax.axis_index("c")
    return x, y, c, 2 * x + y


def _chip_peer(x, y, j):
    return (1 - x if j & 2 else x), (1 - y if j & 1 else y)


_HBM = pl.BlockSpec(memory_space=pltpu.HBM)
_SEM = pl.BlockSpec(memory_space=pltpu.SEMAPHORE)
_EFFECT = pltpu.SideEffectType.DATAFLOW_SIDE_EFFECTING


def _in_hbm(a):
    return pltpu.with_memory_space_constraint(a, pltpu.HBM)


def _copies_start(name, plan, ncopy, srcs, lands, dep=None):
    n = len(srcs) + len(lands)
    deps, dep_specs = _token_operand(dep)

    def body(*refs):
        send_sems, recv_sems = refs[n + len(deps)], refs[n + len(deps) + 1]
        token = refs[-1]
        for k, (src, dst, dev) in enumerate(plan(refs[:len(srcs)], refs[len(srcs):n])):
            pltpu.make_async_remote_copy(src_ref=src, dst_ref=dst, send_sem=send_sems.at[k],
                                         recv_sem=recv_sems.at[k], device_id=dev, device_id_type=MESH).start()
        token[...] = jnp.zeros_like(token)

    bufs = list(srcs) + list(lands)
    outs = pl.pallas_call(
        body, name=name,
        out_shape=(pltpu.SemaphoreType.DMA((ncopy,)), pltpu.SemaphoreType.DMA((ncopy,)),
                   *[pltpu.HBM(a.shape, a.dtype) for a in bufs], _sds((8, 128))),
        in_specs=[_HBM] * n + dep_specs,
        out_specs=(_SEM, _SEM, *[_HBM] * n, pl.BlockSpec(memory_space=pltpu.VMEM)),
        input_output_aliases={i: 2 + i for i in range(n)},
        compiler_params=pltpu.CompilerParams(has_side_effects=_EFFECT),
    )(*[_in_hbm(a) for a in bufs], *deps)
    return outs[0], outs[1], list(outs[2:2 + len(srcs)]), list(outs[2 + len(srcs):2 + n]), outs[-1]


def _copies_wait(name, plan, send_sems, recv_sems, srcs, lands, after):
    n = len(srcs) + len(lands)

    def body(*refs):
        s_sems, r_sems = refs[n], refs[n + 1]
        for k, (src, dst, dev) in enumerate(plan(refs[:len(srcs)], refs[len(srcs):n])):
            cp = pltpu.make_async_remote_copy(src_ref=src, dst_ref=dst, send_sem=s_sems.at[k],
                                              recv_sem=r_sems.at[k], device_id=dev, device_id_type=MESH)
            cp.wait_send()
            cp.wait_recv()

    bufs = list(srcs) + list(lands)
    outs = pl.pallas_call(
        body, name=name,
        out_shape=tuple(pltpu.HBM(a.shape, a.dtype) for a in bufs),
        in_specs=[_HBM] * n + [_SEM, _SEM, pl.BlockSpec(memory_space=pl.ANY)],
        out_specs=tuple([_HBM] * n),
        input_output_aliases={i: i for i in range(n)},
        compiler_params=pltpu.CompilerParams(has_side_effects=_EFFECT),
    )(*bufs, send_sems, recv_sems, after)
    return list(outs[:len(srcs)]), list(outs[len(srcs):])


def _with_own_slot(block, slot, nslots=NCHIP):
    land = lax.empty((nslots,) + block.shape, block.dtype)
    return lax.dynamic_update_slice(land, block[None], (slot,) + (0,) * block.ndim)


def _gather_plan(srcs, lands):
    x, y, c, s = _place()
    return [(src, land.at[s], (*_chip_peer(x, y, j), c))
            for src, land in zip(srcs, lands) for j in range(1, NCHIP)]


def _exchange_plan(srcs, lands):
    x, y, c, s = _place()
    return [(src.at[s ^ j], land.at[s], (*_chip_peer(x, y, j), c))
            for src, land in zip(srcs, lands) for j in range(1, NCHIP)]


def _sibling_plan(srcs, lands):
    x, y, c, _ = _place()
    return [(src, land, (x, y, 1 - c)) for src, land in zip(srcs, lands)]


def _everyone_plan(srcs, lands):
    x, y, c, _ = _place()
    me = 4 * x + 2 * y + c
    return [(srcs[0], lands[0].at[me], (*_chip_peer(x, y, j >> 1), 1 - c if j & 1 else c)) for j in range(1, 8)]


def _blockdiag_in(bb):
    eye = jnp.eye(8, dtype=F32)[None, :, None, None, :, None]
    t = bb.reshape(2, NCB, 8, 16, 64).transpose(1, 2, 3, 0, 4)
    return (t[:, :, :, :, None, :] * eye).reshape(NCB, 128, SW)


def _blockdiag_in_t(dw):
    eye = jnp.eye(8, dtype=F32)[None, :, None, None, :, None]
    t = jnp.sum(dw.reshape(NCB, 8, 16, 2, 8, 64) * eye, axis=4)
    return t.transpose(3, 0, 1, 2, 4).reshape(2, 32, 16, 64)


def _blockdiag_out(cc):
    eye = jnp.eye(8, dtype=F32)[None, None, :, None, :, None]
    t = cc.reshape(2, NCB, 8, 16, 64).transpose(1, 0, 4, 2, 3)
    return (t[:, :, None, :, :, :] * eye).reshape(NCB, SW, 128)


def _blockdiag_out_t(dw):
    eye = jnp.eye(8, dtype=F32)[None, None, :, None, :, None]
    t = jnp.sum(dw.reshape(NCB, 2, 8, 64, 8, 16) * eye, axis=2)
    return t.transpose(1, 0, 3, 4, 2).reshape(2, 32, 16, 64)


def _flatten_small(parts):
    flat = jnp.concatenate([p.reshape(-1) for p in parts])
    n = flat.shape[0]
    rows = -(-n // (128 * SMALL_TILE)) * SMALL_TILE
    return jnp.pad(flat, (0, rows * 128 - n)).reshape(rows, 128)


def _unflatten_small(buf, like):
    flat = buf.reshape(-1)
    out, at = [], 0
    for p in like:
        out.append(flat[at:at + p.size].reshape(p.shape))
        at += p.size
    return out


def kernel(x, pre_norm_g, w_in, lambda_re, lambda_im, log_dt, b_re, b_im, c_re, c_im, d_skip, w_glu, b_glu, w_branch_s, w_branch_a, w_out, post_norm_g, loss_target, m_pre_norm_g, m_w_in, m_lambda_re, m_lambda_im, m_log_dt, m_b_re, m_b_im, m_c_re, m_c_im, m_d_skip, m_w_glu, m_b_glu, m_w_branch_s, m_w_branch_a, m_w_out, m_post_norm_g, v_pre_norm_g, v_w_in, v_lambda_re, v_lambda_im, v_log_dt, v_b_re, v_b_im, v_c_re, v_c_im, v_d_skip, v_w_glu, v_b_glu, v_w_branch_s, v_w_branch_a, v_w_out, v_post_norm_g):
    ng = DEPTH * 32
    lre3 = lambda_re.reshape(ng, 1, 64)
    lim3 = lambda_im.reshape(ng, 1, 64)
    ldt3 = log_dt.reshape(ng, 1, 1)
    bre_t = b_re.transpose(0, 1, 3, 2).reshape(ng, 16, 64)
    bim_t = b_im.transpose(0, 1, 3, 2).reshape(ng, 16, 64)
    ar, ai, bbr, bbi = _s5_prep(lre3, lim3, ldt3, bre_t, bim_t)
    ar4 = ar.reshape(DEPTH, NCB, 1, HW)
    ai4 = ai.reshape(DEPTH, NCB, 1, HW)
    bb = jnp.stack([bbr.reshape(DEPTH, 32, 16, 64), bbi.reshape(DEPTH, 32, 16, 64)], axis=1)
    cc = jnp.stack([c_re, -c_im], axis=1)

    chip = 2 * lax.axis_index("x") + lax.axis_index("y")
    device = 2 * chip + lax.axis_index("c")

    def gather_start(l, dep):
        srcs = [w_in[l].astype(BF16), w_glu[l].astype(BF16), w_branch_s[l].astype(BF16),
                w_branch_a[l].astype(BF16), w_out[l].astype(BF16)]
        return _copies_start(f"gather_start_{l}", _gather_plan, 15, srcs, [_with_own_slot(a, chip) for a in srcs],
                             dep)

    xs = [x[0]]
    saved = []
    flight = gather_start(0, None)
    for l in range(DEPTH):
        send, recv, srcs, lands, _ = flight
        _, (w4, wglu4, wbs4, wba4, wout4) = _copies_wait(f"gather_wait_{l}", _gather_plan, send, recv, srcs, lands,
                                                         xs[l])
        started = None
        if l + 1 < DEPTH:
            flight = gather_start(l + 1, wglu4)
            started = flight[4]
        wglu = wglu4.reshape(SSM, SSM)
        wout = wout4.reshape(D, D)
        wb4 = _blockdiag_in(bb[l]).astype(BF16)
        wc4 = _blockdiag_out(cc[l]).astype(BF16)
        g1 = pre_norm_g[l].reshape(1, D)
        g2 = post_norm_g[l].reshape(1, D)
        dsk = d_skip[l].reshape(1, SSM)
        bgl = b_glu[l].reshape(1, SSM)
        proj = _rms_proj_fwd(xs[l], g1, w4, started)
        states, yraw = _s5_scan_fwd(proj, wb4, wc4, ar4[l], ai4[l])
        ys = _s5_tail_fwd(yraw, proj, dsk, wglu, bgl)
        o, lse = _attn_core_fwd(proj)
        ya = _attn_mix_fwd(o, lse, proj)
        xs.append(_merge_out_fwd(ys, ya, proj, wbs4, wba4, wout, g2, xs[l]))
        saved.append((w4, wglu, wbs4, wba4, wout, wb4, wc4, g1, g2, dsk, bgl, proj, states, yraw, ys, ya, o, lse))

    loss_part, dx = _loss_head(xs[DEPTH], loss_target[0])
    loss = lax.psum(loss_part[0, 0], ("x", "y", "c"))

    big_w = (w_in, w_glu.reshape(DEPTH, 128, SSM), w_branch_s, w_branch_a, w_out)
    big_m = (m_w_in, m_w_glu, m_w_branch_s, m_w_branch_a, m_w_out)
    big_v = (v_w_in, v_w_glu, v_w_branch_s, v_w_branch_a, v_w_out)
    big_out = [None] * 5
    small = {k: [None] * DEPTH for k in ("g1", "da", "dbb", "dcc", "dsk", "bgl", "g2")}
    ici = [None] * DEPTH
    d2d = [None] * DEPTH

    def exchange_start(l, parts):
        lands = [_with_own_slot(lax.dynamic_index_in_dim(p, chip, 0, keepdims=False), chip) for p in parts]
        ici[l] = _copies_start(f"exchange_start_{l}", _exchange_plan, 15, parts, lands)

    def handoff_start(l, after):
        send, recv, srcs, lands, _ = ici[l]
        _, own = _copies_wait(f"exchange_wait_{l}", _exchange_plan, send, recv, srcs, lands, after)
        d2d[l] = _copies_start(f"handoff_start_{l}", _sibling_plan, 5, own,
                               [lax.empty(a.shape, a.dtype) for a in own])

    def update(l, after):
        send, recv, own, lands, _ = d2d[l]
        own, sib = _copies_wait(f"handoff_wait_{l}", _sibling_plan, send, recv, own, lands, after)
        for k in range(5):
            big_out[k] = _adamw_big(l, big_w[k], big_m[k], big_v[k], own[k], sib[k], big_out[k])

    for l in reversed(range(DEPTH)):
        w4, wglu, wbs4, wba4, wout, wb4, wc4, g1, g2, dsk, bgl, proj, states, yraw, ys, ya, o, lse = saved[l]
        started = None
        if l + 1 < DEPTH:
            started = ici[l + 1][4] + d2d[l + 2][4] if l + 2 < DEPTH else ici[l + 1][4]
        dys, dya, dproj, dwbs, dwba, dwout, dg2 = _merge_out_bwd(dx, ys, ya, proj, wbs4, wba4, wout, g2, started)
        dyraw, du_skip, dproj, dwglu, dbgl, ddsk = _s5_tail_bwd(dys, yraw, proj, dsk, wglu, bgl, dproj)
        dproj, dwb, dwc, da = _s5_scan_bwd(dyraw, states, proj, du_skip, wb4, wc4, ar4[l], ai4[l], dproj)
        do, cterm, dproj = _attn_mix_bwd(dya, o, lse, proj, dproj)
        dproj = _attn_core_bwd(proj, do, lse, cterm, dproj)
        dh, dwin = _proj_bwd(dproj, xs[l], g1, w4)
        dx, dg1 = _rms_bwd(dh, xs[l], g1, dx)
        parts = [dwin, dwglu.reshape(NCHIP, 128, SSM), dwbs, dwba, dwout.reshape(NCHIP, D // NCHIP, D)]
        exchange_start(l, parts)
        if l + 1 < DEPTH:
            handoff_start(l + 1, ici[l][4])
        small["g1"][l], small["g2"][l], small["dsk"][l], small["bgl"][l] = dg1, dg2, ddsk, dbgl
        small["da"][l], small["dbb"][l], small["dcc"][l] = da, _blockdiag_in_t(dwb), _blockdiag_out_t(dwc)

    da = jnp.stack(small["da"]).reshape(DEPTH, NCB, 2, HW)
    gar = da[:, :, 0].reshape(ng, 1, 64)
    gai = da[:, :, 1].reshape(ng, 1, 64)
    dbb = jnp.stack(small["dbb"])
    dcc = jnp.stack(small["dcc"])
    dlre, dlim, dldt, dbre_t, dbim_t = _s5_prep_bwd(
        lre3, lim3, ldt3, bre_t, bim_t, gar, gai, dbb[:, 0].reshape(ng, 16, 64), dbb[:, 1].reshape(ng, 16, 64))
    unt = lambda t: t.reshape(DEPTH, 32, 16, 64).transpose(0, 1, 3, 2)
    small_w = [pre_norm_g, lambda_re, lambda_im, log_dt, b_re, b_im, c_re, c_im, d_skip, b_glu, post_norm_g]
    small_m = [m_pre_norm_g, m_lambda_re, m_lambda_im, m_log_dt, m_b_re, m_b_im, m_c_re, m_c_im, m_d_skip,
               m_b_glu, m_post_norm_g]
    small_v = [v_pre_norm_g, v_lambda_re, v_lambda_im, v_log_dt, v_b_re, v_b_im, v_c_re, v_c_im, v_d_skip,
               v_b_glu, v_post_norm_g]
    small_g = [jnp.stack(small["g1"]).reshape(DEPTH, D), dlre.reshape(DEPTH, 32, 64), dlim.reshape(DEPTH, 32, 64),
               dldt.reshape(DEPTH, 32), unt(dbre_t), unt(dbim_t), dcc[:, 0], -dcc[:, 1],
               jnp.stack(small["dsk"]).reshape(DEPTH, SSM), jnp.stack(small["bgl"]).reshape(DEPTH, SSM),
               jnp.stack(small["g2"]).reshape(DEPTH, D)]
    part = _flatten_small(small_g)
    send, recv, srcs, lands, token = _copies_start("small_start", _everyone_plan, 7, [part],
                                                   [_with_own_slot(part, device, 8)])
    update(DEPTH - 1, token + ici[0][4])
    for l in range(DEPTH - 2, 0, -1):
        update(l, big_out[4][0])
    handoff_start(0, big_out[4][0])
    update(0, d2d[0][4])
    _, (parts8,) = _copies_wait("small_wait", _everyone_plan, send, recv, srcs, lands, big_out[4][0])
    g_flat, d_flat, nm_flat, nv_flat = _adamw_small(_flatten_small(small_w), parts8, _flatten_small(small_m),
                                                    _flatten_small(small_v))
    sg = _unflatten_small(g_flat, small_w)
    sd = _unflatten_small(d_flat, small_w)
    snm = _unflatten_small(nm_flat, small_w)
    snv = _unflatten_small(nv_flat, small_w)

    def ordered(sm, which):
        bg = [big_out[k][which] for k in range(5)]
        bg[1] = bg[1].reshape(DEPTH, 128, SSM)
        return [sm[0], bg[0], sm[1], sm[2], sm[3], sm[4], sm[5], sm[6], sm[7], sm[8], bg[1], sm[9], bg[2], bg[3],
                bg[4], sm[10]]

    return (loss, dx[None], *ordered(sg, 0), *ordered(sd, 1), *ordered(snm, 2), *ordered(snv, 3))
```

```python
import math

import jax
import jax.numpy as jnp
from jax import lax
from jax.experimental import pallas as pl
from jax.experimental.pallas import tpu as pltpu

F32 = jnp.float32
BF16 = jnp.bfloat16
MESH = pl.DeviceIdType.MESH

DEPTH = 4
L = 2048
D = 1024
NCOL = 8192
SSM = 512
AW = 512
QKV = 1536
RMS_EPS = 1e-6
NCHIP = 4

COL_U, COL_ZS, COL_Q, COL_K, COL_V, COL_ZA, COL_GS, COL_GA = 0, 4, 8, 20, 32, 44, 48, 56

NSEG = 8
SEG = L // NSEG
NCB = 4
HW = 512
SW = 2 * HW

HEAD = 128
DILATIONS = (1, 4, 16)
SCALE = HEAD ** -0.5
NEG = -1e30

ADAM_LR, ADAM_B1, ADAM_B2, ADAM_EPS, ADAM_WD, ADAM_STEP = 0.001, 0.9, 0.999, 1e-08, 0.01, 10


def _sds(shape, dtype=F32):
    return jax.ShapeDtypeStruct(shape, dtype)


def _pcall(body, *, name, out_shape, grid=None, in_specs=None, out_specs=None, scratch=(), vmem_mb=None,
           aliases=None):
    params = {}
    if vmem_mb is not None:
        params["vmem_limit_bytes"] = vmem_mb << 20
    kw = {}
    if grid is not None:
        kw["grid"] = grid
    if in_specs is not None:
        kw["in_specs"] = in_specs
    if out_specs is not None:
        kw["out_specs"] = out_specs
    return pl.pallas_call(body, name=name, out_shape=out_shape, scratch_shapes=list(scratch),
                          compiler_params=pltpu.CompilerParams(**params),
                          input_output_aliases=aliases or {}, **kw)


def _dot(a, b):
    return jnp.dot(a, b, preferred_element_type=F32)


def _dot_nt(a, b):
    return lax.dot_general(a, b, (((1,), (1,)), ((), ())), preferred_element_type=F32)


def _dot_tn(a, b):
    return lax.dot_general(a, b, (((0,), (0,)), ((), ())), preferred_element_type=F32)


def _sigmoid(x):
    return jax.nn.sigmoid(x)


_GELU_K = math.sqrt(2.0 / math.pi)


def _gelu(x):
    return 0.5 * x * (1.0 + jnp.tanh(_GELU_K * (x + 0.044715 * (x * x * x))))


def _gelu_grad(x):
    t = jnp.tanh(_GELU_K * (x + 0.044715 * (x * x * x)))
    return 0.5 * (1.0 + t) + 0.5 * x * (1.0 - t * t) * (_GELU_K * (1.0 + 3.0 * 0.044715 * (x * x)))


def _token_operand(dep):
    if dep is None:
        return [], []
    return [dep], [pl.BlockSpec(memory_space=pl.ANY)]


def _rms_proj_fwd(x, g, w4, dep=None):
    tl, tn = 512, 1024
    deps, dep_specs = _token_operand(dep)

    def body(x_ref, g_ref, w_ref, *rest):
        o_ref = rest[-1]
        xx = x_ref[...]
        inv = lax.rsqrt(jnp.mean(xx * xx, axis=-1, keepdims=True) + RMS_EPS)
        h = (xx * inv * g_ref[...]).astype(BF16)
        o_ref[...] = _dot(h, w_ref[...])

    return _pcall(
        body, name="rms_proj_fwd", grid=(NCOL // tn, L // tl),
        in_specs=[pl.BlockSpec((tl, D), lambda j, i: (i, 0)),
                  pl.BlockSpec((1, D), lambda j, i: (0, 0)),
                  pl.BlockSpec((None, D, tn), lambda j, i: (lax.div(j, 2), 0, lax.rem(j, 2)))] + dep_specs,
        out_specs=pl.BlockSpec((tl, tn), lambda j, i: (i, j)),
        out_shape=_sds((L, NCOL)), vmem_mb=40)(x, g, w4, *deps)


def _proj_bwd(dproj, x, g, w4):
    tl, tn = 512, 1024
    ni = L // tl

    def body(dp_ref, x_ref, g_ref, w_ref, dh_ref, dw_ref, acc_ref):
        j, i = pl.program_id(0), pl.program_id(1)
        xx = x_ref[...]
        inv = lax.rsqrt(jnp.mean(xx * xx, axis=-1, keepdims=True) + RMS_EPS)
        h = (xx * inv * g_ref[...]).astype(BF16)
        dp = dp_ref[...].astype(BF16)
        part = _dot_tn(h, dp)

        @pl.when(i == 0)
        def _():
            acc_ref[...] = part

        @pl.when(i > 0)
        def _():
            acc_ref[...] += part

        @pl.when(i == ni - 1)
        def _():
            dw_ref[...] = acc_ref[...].astype(BF16)

        rows = pl.ds(pl.multiple_of(i * tl, tl), tl)
        dh = _dot_nt(dp, w_ref[...])

        @pl.when(j == 0)
        def _():
            dh_ref[rows, :] = dh

        @pl.when(j > 0)
        def _():
            dh_ref[rows, :] += dh

    return _pcall(
        body, name="proj_bwd", grid=(NCOL // tn, ni),
        in_specs=[pl.BlockSpec((tl, tn), lambda j, i: (i, j)),
                  pl.BlockSpec((tl, D), lambda j, i: (i, 0)),
                  pl.BlockSpec((1, D), lambda j, i: (0, 0)),
                  pl.BlockSpec((None, D, tn), lambda j, i: (lax.div(j, 2), 0, lax.rem(j, 2)))],
        out_specs=[pl.BlockSpec((L, D), lambda j, i: (0, 0)),
                   pl.BlockSpec((None, D, tn), lambda j, i: (lax.div(j, 2), 0, lax.rem(j, 2)))],
        out_shape=[_sds((L, D)), _sds((NCHIP, D, NCOL // NCHIP), BF16)],
        scratch=[pltpu.VMEM((D, tn), F32)], vmem_mb=56)(dproj, x, g, w4)


def _rms_bwd(dh, x, g, dxn, dep=None):
    tl = 512
    deps, dep_specs = _token_operand(dep)

    def body(dh_ref, x_ref, g_ref, dxn_ref, *rest):
        dx_ref, dg_ref = rest[len(deps):]
        i = pl.program_id(0)
        xx = x_ref[...]
        inv = lax.rsqrt(jnp.mean(xx * xx, axis=-1, keepdims=True) + RMS_EPS)
        nrm = xx * inv
        dh_v = dh_ref[...]
        dn = dh_v * g_ref[...]
        dx_ref[...] = dxn_ref[...] + inv * (dn - nrm * jnp.mean(dn * nrm, axis=-1, keepdims=True))
        part = jnp.sum(dh_v * nrm, axis=0, keepdims=True)

        @pl.when(i == 0)
        def _():
            dg_ref[...] = part

        @pl.when(i > 0)
        def _():
            dg_ref[...] += part

    row = pl.BlockSpec((tl, D), lambda i: (i, 0))
    vec = pl.BlockSpec((1, D), lambda i: (0, 0))
    return _pcall(body, name="rms_bwd", grid=(L // tl,), in_specs=[row, row, vec, row] + dep_specs,
                  out_specs=[row, vec], out_shape=[_sds((L, D)), _sds((1, D))], vmem_mb=40)(dh, x, g, dxn, *deps)


def _s5_param_math(lre, lim, ldt, bre, bim):
    lr = jnp.minimum(lre, -1e-4)
    dt = jnp.exp(ldt)
    mag = jnp.exp(lr * dt)
    ar = mag * jnp.cos(lim * dt)
    ai = mag * jnp.sin(lim * dt)
    den = lr * lr + lim * lim
    nr = ar - 1.0
    qr = (nr * lr + ai * lim) / den
    qi = (ai * lr - nr * lim) / den
    return ar, ai, qr * bre - qi * bim, qr * bim + qi * bre


def _s5_prep(lre, lim, ldt, bre, bim):
    n = lre.shape[0]

    def body(lre_ref, lim_ref, ldt_ref, bre_ref, bim_ref, ar_ref, ai_ref, bbr_ref, bbi_ref):
        ar, ai, bbr, bbi = _s5_param_math(lre_ref[...], lim_ref[...], ldt_ref[...], bre_ref[...], bim_ref[...])
        ar_ref[...] = ar
        ai_ref[...] = ai
        bbr_ref[...] = bbr
        bbi_ref[...] = bbi

    return _pcall(body, name="s5_prep",
                  out_shape=[_sds((n, 1, 64)), _sds((n, 1, 64)), _sds((n, 16, 64)), _sds((n, 16, 64))])(
                      lre, lim, ldt, bre, bim)


def _s5_prep_bwd(lre, lim, ldt, bre, bim, gar, gai, gbbr, gbbi):
    n = lre.shape[0]

    def body(lre_ref, lim_ref, ldt_ref, bre_ref, bim_ref, gar_ref, gai_ref, gbbr_ref, gbbi_ref,
             dlre_ref, dlim_ref, dldt_ref, dbre_ref, dbim_ref):
        _, vjp = jax.vjp(_s5_param_math, lre_ref[...], lim_ref[...], ldt_ref[...], bre_ref[...], bim_ref[...])
        dlre, dlim, dldt, dbre, dbim = vjp((gar_ref[...], gai_ref[...], gbbr_ref[...], gbbi_ref[...]))
        dlre_ref[...] = dlre
        dlim_ref[...] = dlim
        dldt_ref[...] = dldt
        dbre_ref[...] = dbre
        dbim_ref[...] = dbim

    return _pcall(body, name="s5_prep_bwd",
                  out_shape=[_sds((n, 1, 64)), _sds((n, 1, 64)), _sds((n, 1, 1)), _sds((n, 16, 64)),
                             _sds((n, 16, 64))])(lre, lim, ldt, bre, bim, gar, gai, gbbr, gbbi)


def _permute_rows(dst_ref, src_ref):
    for s in range(NSEG):
        dst_ref[pl.ds(s, SEG, stride=NSEG), :] = src_ref[pl.ds(s * SEG, SEG), :]


def _unpermute_rows(src_ref, s):
    return src_ref[pl.ds(s, SEG, stride=NSEG), :]


def _row_of(slab, rows, j):
    return jnp.sum(jnp.where(rows == j, slab, 0.0), axis=0, keepdims=True)


def _s5_scan_fwd(proj, wb4, wc4, ar4, ai4):
    def body(u_ref, wb_ref, wc_ref, ar_ref, ai_ref, x_ref, y_ref, up_ref, d_ref, yp_ref):
        _permute_rows(up_ref, u_ref)
        d_ref[...] = _dot(up_ref[...].astype(BF16), wb_ref[...])
        ar = jnp.broadcast_to(ar_ref[...], (NSEG, HW))
        ai = jnp.broadcast_to(ai_ref[...], (NSEG, HW))
        zero = jnp.zeros((NSEG, HW), F32)

        def drive(t):
            row = pl.multiple_of(t * NSEG, NSEG)
            dd = d_ref[pl.ds(row, NSEG), :]
            return row, dd[:, :HW], dd[:, HW:]

        def local_step(t, c):
            xr, xi = c
            _, dr, di = drive(t)
            return ar * xr - ai * xi + dr, ar * xi + ai * xr + di

        fr, fi = lax.fori_loop(0, SEG, local_step, (zero, zero), unroll=8)
        pr, pi_ = ar_ref[...], ai_ref[...]
        for _ in range(int(math.log2(SEG))):
            pr, pi_ = pr * pr - pi_ * pi_, 2.0 * pr * pi_
        rows = lax.broadcasted_iota(jnp.int32, (NSEG, HW), 0)
        cr, ci = zero, zero
        for j in range(NSEG - 1):
            sr, si = _row_of(cr, rows, j), _row_of(ci, rows, j)
            gr, gi = _row_of(fr, rows, j), _row_of(fi, rows, j)
            nr = pr * sr - pi_ * si + gr
            ni = pr * si + pi_ * sr + gi
            cr = jnp.where(rows == j + 1, nr, cr)
            ci = jnp.where(rows == j + 1, ni, ci)

        def true_step(t, c):
            xr, xi = c
            row, dr, di = drive(t)
            nr = ar * xr - ai * xi + dr
            ni = ar * xi + ai * xr + di
            x_ref[pl.ds(row, NSEG), :] = jnp.concatenate([nr, ni], axis=1)
            return nr, ni

        lax.fori_loop(0, SEG, true_step, (cr, ci), unroll=8)
        yp_ref[...] = _dot(x_ref[...].astype(BF16), wc_ref[...])
        for s in range(NSEG):
            y_ref[pl.ds(s * SEG, SEG), :] = _unpermute_rows(yp_ref, s)

    return _pcall(
        body, name="s5_scan_fwd", grid=(NCB,),
        in_specs=[pl.BlockSpec((L, 128), lambda b: (0, COL_U + b)),
                  pl.BlockSpec((None, 128, SW), lambda b: (b, 0, 0)),
                  pl.BlockSpec((None, SW, 128), lambda b: (b, 0, 0)),
                  pl.BlockSpec((None, 1, HW), lambda b: (b, 0, 0)),
                  pl.BlockSpec((None, 1, HW), lambda b: (b, 0, 0))],
        out_specs=[pl.BlockSpec((None, L, SW), lambda b: (b, 0, 0)),
                   pl.BlockSpec((L, 128), lambda b: (0, b))],
        out_shape=[_sds((NCB, L, SW)), _sds((L, SSM))],
        scratch=[pltpu.VMEM((L, 128), F32), pltpu.VMEM((L, SW), F32), pltpu.VMEM((L, 128), F32)],
        vmem_mb=56)(proj, wb4, wc4, ar4, ai4)


def _s5_scan_bwd(dy, xs, proj, du_skip, wb4, wc4, ar4, ai4, dproj):
    def body(dy_ref, x_ref, u_ref, dus_ref, wb_ref, wc_ref, ar_ref, ai_ref, _,
             du_ref, dwb_ref, dwc_ref, da_ref, dyp_ref, up_ref, g_ref, dup_ref):
        _permute_rows(dyp_ref, dy_ref)
        _permute_rows(up_ref, u_ref)
        dyp = dyp_ref[...].astype(BF16)
        g_ref[...] = _dot_nt(dyp, wc_ref[...])
        dwc_ref[...] = _dot_tn(x_ref[...].astype(BF16), dyp)
        ar = jnp.broadcast_to(ar_ref[...], (NSEG, HW))
        ai = jnp.broadcast_to(ai_ref[...], (NSEG, HW))
        zero = jnp.zeros((NSEG, HW), F32)

        def load(t):
            row = pl.multiple_of(t * NSEG, NSEG)
            gg = g_ref[pl.ds(row, NSEG), :]
            return row, gg[:, :HW], gg[:, HW:]

        def local_step(k, c):
            lr, li = c
            _, gr, gi = load(SEG - 1 - k)
            return ar * lr + ai * li + gr, ar * li - ai * lr + gi

        fr, fi = lax.fori_loop(0, SEG, local_step, (zero, zero), unroll=8)
        pr, pi_ = ar_ref[...], -ai_ref[...]
        for _ in range(int(math.log2(SEG))):
            pr, pi_ = pr * pr - pi_ * pi_, 2.0 * pr * pi_
        rows = lax.broadcasted_iota(jnp.int32, (NSEG, HW), 0)
        cr, ci = zero, zero
        for j in range(NSEG - 1, 0, -1):
            sr, si = _row_of(cr, rows, j), _row_of(ci, rows, j)
            gr, gi = _row_of(fr, rows, j), _row_of(fi, rows, j)
            nr = pr * sr - pi_ * si + gr
            ni = pr * si + pi_ * sr + gi
            cr = jnp.where(rows == j - 1, nr, cr)
            ci = jnp.where(rows == j - 1, ni, ci)

        def true_step(k, c):
            lr, li, gar, gai = c
            t = SEG - 1 - k
            row, gr, gi = load(t)
            nr = ar * lr + ai * li + gr
            ni = ar * li - ai * lr + gi
            g_ref[pl.ds(row, NSEG), :] = jnp.concatenate([nr, ni], axis=1)
            prow = pl.multiple_of(jnp.maximum(t - 1, 0) * NSEG, NSEG)
            xp = x_ref[pl.ds(prow, NSEG), :]
            live = (t > 0).astype(F32)
            xr, xi = xp[:, :HW] * live, xp[:, HW:] * live
            return nr, ni, gar + (nr * xr + ni * xi), gai + (ni * xr - nr * xi)

        _, _, gar, gai = lax.fori_loop(0, SEG, true_step, (cr, ci, zero, zero), unroll=4)
        l0 = g_ref[1:NSEG, :]
        xl = x_ref[(SEG - 1) * NSEG:(SEG - 1) * NSEG + NSEG - 1, :]
        l0r, l0i, xlr, xli = l0[:, :HW], l0[:, HW:], xl[:, :HW], xl[:, HW:]
        gar_t = jnp.sum(gar, axis=0, keepdims=True) + jnp.sum(l0r * xlr + l0i * xli, axis=0, keepdims=True)
        gai_t = jnp.sum(gai, axis=0, keepdims=True) + jnp.sum(l0i * xlr - l0r * xli, axis=0, keepdims=True)
        da_ref[...] = jnp.concatenate([gar_t, gai_t], axis=1)
        lam = g_ref[...].astype(BF16)
        dwb_ref[...] = _dot_tn(up_ref[...].astype(BF16), lam)
        dup_ref[...] = _dot_nt(lam, wb_ref[...])
        for s in range(NSEG):
            sl = pl.ds(s * SEG, SEG)
            du_ref[sl, :] = _unpermute_rows(dup_ref, s) + dus_ref[sl, :]

    col = lambda off: pl.BlockSpec((L, 128), lambda b: (0, off + b))
    return _pcall(
        body, name="s5_scan_bwd", grid=(NCB,),
        in_specs=[col(0), pl.BlockSpec((None, L, SW), lambda b: (b, 0, 0)), col(COL_U), col(0),
                  pl.BlockSpec((None, 128, SW), lambda b: (b, 0, 0)),
                  pl.BlockSpec((None, SW, 128), lambda b: (b, 0, 0)),
                  pl.BlockSpec((None, 1, HW), lambda b: (b, 0, 0)),
                  pl.BlockSpec((None, 1, HW), lambda b: (b, 0, 0)),
                  pl.BlockSpec(memory_space=pl.ANY)],
        out_specs=[col(COL_U), pl.BlockSpec((None, 128, SW), lambda b: (b, 0, 0)),
                   pl.BlockSpec((None, SW, 128), lambda b: (b, 0, 0)),
                   pl.BlockSpec((None, 1, SW), lambda b: (b, 0, 0))],
        out_shape=[_sds((L, NCOL)), _sds((NCB, 128, SW)), _sds((NCB, SW, 128)), _sds((NCB, 1, SW))],
        scratch=[pltpu.VMEM((L, 128), F32), pltpu.VMEM((L, 128), F32), pltpu.VMEM((L, SW), F32),
                 pltpu.VMEM((L, 128), F32)],
        aliases={8: 0}, vmem_mb=56)(dy, xs, proj, du_skip, wb4, wc4, ar4, ai4, dproj)


def _s5_tail_fwd(yraw, proj, dsk, wglu, bglu):
    tl = 512

    def body(y_ref, u_ref, z_ref, dsk_ref, w_ref, b_ref, o_ref):
        y1 = y_ref[...] + dsk_ref[...] * u_ref[...]
        y2 = _gelu(y1)
        gl = _dot(y2.astype(BF16), w_ref[...]) + b_ref[...]
        z = z_ref[...]
        o_ref[...] = (y2 * _sigmoid(gl)) * (z * _sigmoid(z))

    blk = lambda c: pl.BlockSpec((tl, SSM), lambda i: (i, c))
    vec = pl.BlockSpec((1, SSM), lambda i: (0, 0))
    return _pcall(body, name="s5_tail_fwd", grid=(L // tl,),
                  in_specs=[blk(0), blk(0), blk(1), vec, pl.BlockSpec((SSM, SSM), lambda i: (0, 0)), vec],
                  out_specs=blk(0), out_shape=_sds((L, SSM)), vmem_mb=40)(yraw, proj, proj, dsk, wglu, bglu)


def _s5_tail_bwd(dys, yraw, proj, dsk, wglu, bglu, dproj):
    tl = 512
    ni = L // tl

    def body(dys_ref, y_ref, u_ref, z_ref, dsk_ref, w_ref, b_ref, _,
             dy_ref, dus_ref, dz_ref, dw_ref, db_ref, dd_ref, acc_ref):
        i = pl.program_id(0)
        u = u_ref[...]
        y1 = y_ref[...] + dsk_ref[...] * u
        y2 = _gelu(y1)
        y2b = y2.astype(BF16)
        sg = _sigmoid(_dot(y2b, w_ref[...]) + b_ref[...])
        y3 = y2 * sg
        z = z_ref[...]
        sz = _sigmoid(z)
        dys = dys_ref[...]
        dy3 = dys * (z * sz)
        dz_ref[...] = dys * y3 * (sz * (1.0 + z * (1.0 - sz)))
        dgl = (dy3 * y2) * (sg * (1.0 - sg))
        dglb = dgl.astype(BF16)
        dy2 = dy3 * sg + _dot_nt(dglb, w_ref[...])
        dy1 = dy2 * _gelu_grad(y1)
        dy_ref[...] = dy1
        dus_ref[...] = dsk_ref[...] * dy1
        dw = _dot_tn(y2b, dglb)
        db = jnp.sum(dgl, axis=0, keepdims=True)
        dd = jnp.sum(dy1 * u, axis=0, keepdims=True)

        @pl.when(i == 0)
        def _():
            acc_ref[...] = dw
            db_ref[...] = db
            dd_ref[...] = dd

        @pl.when(i > 0)
        def _():
            acc_ref[...] += dw
            db_ref[...] += db
            dd_ref[...] += dd

        @pl.when(i == ni - 1)
        def _():
            dw_ref[...] = acc_ref[...].astype(BF16)

    blk = lambda c: pl.BlockSpec((tl, SSM), lambda i: (i, c))
    vec = pl.BlockSpec((1, SSM), lambda i: (0, 0))
    mat = pl.BlockSpec((SSM, SSM), lambda i: (0, 0))
    return _pcall(body, name="s5_tail_bwd", grid=(ni,),
                  in_specs=[blk(0), blk(0), blk(0), blk(1), vec, mat, vec, pl.BlockSpec(memory_space=pl.ANY)],
                  out_specs=[blk(0), blk(0), blk(COL_ZS // 4), mat, vec, vec],
                  out_shape=[_sds((L, SSM)), _sds((L, SSM)), _sds((L, NCOL)), _sds((SSM, SSM), BF16),
                             _sds((1, SSM)), _sds((1, SSM))],
                  scratch=[pltpu.VMEM((SSM, SSM), F32)], aliases={7: 2},
                  vmem_mb=40)(dys, yraw, proj, proj, dsk, wglu, bglu, dproj)


def _attn_blocks(dil):
    nb = L // dil // HEAD
    return [(i * HEAD * dil + r, (i - 1) * HEAD * dil + r if i > 0 else None)
            for r in range(dil) for i in range(nb)]


def _rows(start, dil):
    return pl.ds(start, HEAD) if dil == 1 else pl.ds(start, HEAD, stride=dil)


def _band_masks():
    ri = lax.broadcasted_iota(jnp.int32, (HEAD, HEAD), 0)
    ci = lax.broadcasted_iota(jnp.int32, (HEAD, HEAD), 1)
    return ci <= ri, ci >= ri


def _qkv_specs(index):
    def spec(off):
        return pl.BlockSpec((L, HEAD), lambda gi, h: (0, off + index(gi, h)))
    return [spec(off) for off in (COL_Q, COL_K, COL_V)]


def _attn_core_fwd(proj):
    def body(q_ref, k_ref, v_ref, o_ref, lse_ref):
        gi = pl.program_id(0)
        m_cur, m_prev = _band_masks()
        for g, dil in enumerate(DILATIONS):
            @pl.when(gi == g)
            def _(dil=dil):
                for st, pst in _attn_blocks(dil):
                    r = _rows(st, dil)
                    q = q_ref[r, :].astype(BF16)
                    s_c = jnp.where(m_cur, _dot_nt(q, k_ref[r, :].astype(BF16)) * SCALE, NEG)
                    mx = jnp.max(s_c, axis=-1, keepdims=True)
                    if pst is not None:
                        rp = _rows(pst, dil)
                        s_p = jnp.where(m_prev, _dot_nt(q, k_ref[rp, :].astype(BF16)) * SCALE, NEG)
                        mx = jnp.maximum(mx, jnp.max(s_p, axis=-1, keepdims=True))
                    p_c = jnp.exp(s_c - mx)
                    den = jnp.sum(p_c, axis=-1, keepdims=True)
                    acc = _dot(p_c.astype(BF16), v_ref[r, :].astype(BF16))
                    if pst is not None:
                        p_p = jnp.exp(s_p - mx)
                        den = den + jnp.sum(p_p, axis=-1, keepdims=True)
                        acc = acc + _dot(p_p.astype(BF16), v_ref[rp, :].astype(BF16))
                    o_ref[r, :] = acc / den
                    lse_ref[r, :] = jnp.broadcast_to(mx + jnp.log(den), (HEAD, HEAD))

    idx = lambda gi, h: gi * 4 + h
    out = pl.BlockSpec((L, HEAD), lambda gi, h: (0, gi * 4 + h))
    return _pcall(body, name="attn_core_fwd", grid=(3, 4), in_specs=_qkv_specs(idx), out_specs=[out, out],
                  out_shape=[_sds((L, QKV)), _sds((L, QKV))], vmem_mb=40)(proj, proj, proj)


def _attn_mix_fwd(o, lse, proj):
    tl = 512

    def body(o_ref, l_ref, z_ref, y_ref):
        for h in range(4):
            c = [pl.ds((g * 4 + h) * HEAD, HEAD) for g in range(3)]
            ls = [l_ref[:, c[g]] for g in range(3)]
            m = jnp.maximum(jnp.maximum(ls[0], ls[1]), ls[2])
            e = [jnp.exp(ls[g] - m) for g in range(3)]
            y = (e[0] * o_ref[:, c[0]] + e[1] * o_ref[:, c[1]] + e[2] * o_ref[:, c[2]]) / (e[0] + e[1] + e[2])
            z = z_ref[:, pl.ds(h * HEAD, HEAD)]
            y_ref[:, pl.ds(h * HEAD, HEAD)] = y * (z * _sigmoid(z))

    wide = pl.BlockSpec((tl, QKV), lambda i: (i, 0))
    return _pcall(body, name="attn_mix_fwd", grid=(L // tl,),
                  in_specs=[wide, wide, pl.BlockSpec((tl, AW), lambda i: (i, COL_ZA // 4))],
                  out_specs=pl.BlockSpec((tl, AW), lambda i: (i, 0)), out_shape=_sds((L, AW)),
                  vmem_mb=40)(o, lse, proj)


def _attn_mix_bwd(dya, o, lse, proj, dproj):
    tl = 512

    def body(dya_ref, o_ref, l_ref, z_ref, _, do_ref, c_ref, dz_ref):
        for h in range(4):
            c = [pl.ds((g * 4 + h) * HEAD, HEAD) for g in range(3)]
            hs = pl.ds(h * HEAD, HEAD)
            ls = [l_ref[:, c[g]] for g in range(3)]
            m = jnp.maximum(jnp.maximum(ls[0], ls[1]), ls[2])
            e = [jnp.exp(ls[g] - m) for g in range(3)]
            den = e[0] + e[1] + e[2]
            al = [e[g] / den for g in range(3)]
            y = al[0] * o_ref[:, c[0]] + al[1] * o_ref[:, c[1]] + al[2] * o_ref[:, c[2]]
            z = z_ref[:, hs]
            sz = _sigmoid(z)
            dya = dya_ref[:, hs]
            dz_ref[:, hs] = dya * y * (sz * (1.0 + z * (1.0 - sz)))
            dy = dya * (z * sz)
            tot = jnp.sum(dy * y, axis=-1, keepdims=True)
            for g in range(3):
                do_ref[:, c[g]] = al[g] * dy
                c_ref[:, c[g]] = -(al[g] * tot)

    wide = pl.BlockSpec((tl, QKV), lambda i: (i, 0))
    nar = pl.BlockSpec((tl, AW), lambda i: (i, 0))
    za = pl.BlockSpec((tl, AW), lambda i: (i, COL_ZA // 4))
    return _pcall(body, name="attn_mix_bwd", grid=(L // tl,),
                  in_specs=[nar, wide, wide, za, pl.BlockSpec(memory_space=pl.ANY)],
                  out_specs=[wide, wide, za], out_shape=[_sds((L, QKV)), _sds((L, QKV)), _sds((L, NCOL))],
                  aliases={4: 2}, vmem_mb=48)(dya, o, lse, proj, dproj)


def _attn_core_bwd(proj, do, lse, cc, dproj):
    def body(q_ref, k_ref, v_ref, do_ref, lse_ref, c_ref, _, dp_ref, dq_ref, dk_ref, dv_ref, sems):
        gi = pl.program_id(0)
        head = gi * 4 + pl.program_id(1)
        m_cur, m_prev = _band_masks()
        dk_ref[...] = jnp.zeros((L, HEAD), F32)
        dv_ref[...] = jnp.zeros((L, HEAD), F32)
        for g, dil in enumerate(DILATIONS):
            @pl.when(gi == g)
            def _(dil=dil):
                for st, pst in _attn_blocks(dil):
                    r = _rows(st, dil)
                    q = q_ref[r, :].astype(BF16)
                    do_b = do_ref[r, :].astype(BF16)
                    lse_b = lse_ref[r, :]
                    c_b = c_ref[r, :]
                    dq = None
                    for rr, mask in ((r, m_cur),) + (((_rows(pst, dil), m_prev),) if pst is not None else ()):
                        kb = k_ref[rr, :].astype(BF16)
                        s = _dot_nt(q, kb) * SCALE
                        p = jnp.where(mask, jnp.exp(s - lse_b), 0.0)
                        dp = _dot_nt(do_b, v_ref[rr, :].astype(BF16))
                        ds = (p * (dp + c_b) * SCALE).astype(BF16)
                        part = _dot(ds, kb)
                        dq = part if dq is None else dq + part
                        dk_ref[rr, :] += _dot_tn(ds, q)
                        dv_ref[rr, :] += _dot_tn(p.astype(BF16), do_b)
                    dq_ref[r, :] = dq

        copies = [pltpu.make_async_copy(src, dp_ref.at[:, pl.ds(pl.multiple_of((off + head) * HEAD, HEAD), HEAD)],
                                        sems.at[n])
                  for n, (src, off) in enumerate(((dq_ref, COL_Q), (dk_ref, COL_K), (dv_ref, COL_V)))]
        for cp in copies:
            cp.start()
        for cp in copies:
            cp.wait()

    idx = lambda gi, h: gi * 4 + h
    blk = pl.BlockSpec((L, HEAD), lambda gi, h: (0, gi * 4 + h))
    hbm = pl.BlockSpec(memory_space=pl.ANY)
    return _pcall(body, name="attn_core_bwd", grid=(3, 4), in_specs=_qkv_specs(idx) + [blk, blk, blk, hbm],
                  out_specs=hbm, out_shape=_sds((L, NCOL)),
                  scratch=[pltpu.VMEM((L, HEAD), F32)] * 3 + [pltpu.SemaphoreType.DMA((3,))],
                  aliases={6: 0}, vmem_mb=48)(proj, proj, proj, do, lse, cc, dproj)


def _merge_fwd_math(ys_b, ya_b, gs, ga, wbs_ref, wba_ref, wout_ref):
    bs = jnp.concatenate([_dot(ys_b, wbs_ref[s]) for s in range(NCHIP)], axis=1)
    ba = jnp.concatenate([_dot(ya_b, wba_ref[s]) for s in range(NCHIP)], axis=1)
    sgs, sga = _sigmoid(gs), _sigmoid(ga)
    merged = sgs * bs + sga * ba
    out = _dot(merged.astype(BF16), wout_ref[...])
    inv = lax.rsqrt(jnp.mean(out * out, axis=-1, keepdims=True) + RMS_EPS)
    return bs, ba, sgs, sga, merged, out, inv


def _merge_specs(tl):
    row = lambda w, c: pl.BlockSpec((tl, w), lambda i: (i, c))
    return [row(SSM, 0), row(AW, 0), row(D, COL_GS // 8), row(D, COL_GA // 8),
            pl.BlockSpec((NCHIP, SSM, D // NCHIP), lambda i: (0, 0, 0)),
            pl.BlockSpec((NCHIP, AW, D // NCHIP), lambda i: (0, 0, 0)),
            pl.BlockSpec((D, D), lambda i: (0, 0)),
            pl.BlockSpec((1, D), lambda i: (0, 0))]


def _merge_out_fwd(ys, ya, proj, wbs4, wba4, wout, g2, x):
    tl = 256

    def body(ys_ref, ya_ref, gs_ref, ga_ref, wbs_ref, wba_ref, wout_ref, g2_ref, x_ref, o_ref):
        *_, out, inv = _merge_fwd_math(ys_ref[...].astype(BF16), ya_ref[...].astype(BF16), gs_ref[...],
                                       ga_ref[...], wbs_ref, wba_ref, wout_ref)
        o_ref[...] = x_ref[...] + out * inv * g2_ref[...]

    row = pl.BlockSpec((tl, D), lambda i: (i, 0))
    return _pcall(body, name="merge_out_fwd", grid=(L // tl,), in_specs=_merge_specs(tl) + [row],
                  out_specs=row, out_shape=_sds((L, D)), vmem_mb=48)(ys, ya, proj, proj, wbs4, wba4, wout, g2, x)


def _merge_out_bwd(dxn, ys, ya, proj, wbs4, wba4, wout, g2, dep=None):
    tl = 256
    ni = L // tl
    cw = D // NCHIP
    deps, dep_specs = _token_operand(dep)

    def body(dxn_ref, ys_ref, ya_ref, gs_ref, ga_ref, wbs_ref, wba_ref, wout_ref, g2_ref, *rest):
        (dys_ref, dya_ref, dgate_ref, dwbs_ref, dwba_ref, dwout_ref, dg2_ref,
         abs_ref, aba_ref, aout_ref) = rest[len(deps):]
        i = pl.program_id(0)
        ys_b, ya_b = ys_ref[...].astype(BF16), ya_ref[...].astype(BF16)
        bs, ba, sgs, sga, merged, out, inv = _merge_fwd_math(ys_b, ya_b, gs_ref[...], ga_ref[...],
                                                             wbs_ref, wba_ref, wout_ref)
        nrm = out * inv
        dxn = dxn_ref[...]
        dg2 = jnp.sum(dxn * nrm, axis=0, keepdims=True)
        dn = dxn * g2_ref[...]
        dout = (inv * (dn - nrm * jnp.mean(dn * nrm, axis=-1, keepdims=True))).astype(BF16)
        dwout = _dot_tn(merged.astype(BF16), dout)
        dm = _dot_nt(dout, wout_ref[...])
        dbs, dba = dm * sgs, dm * sga
        dgate_ref[:, :D] = dm * bs * (sgs * (1.0 - sgs))
        dgate_ref[:, D:] = dm * ba * (sga * (1.0 - sga))
        dbs_b, dba_b = dbs.astype(BF16), dba.astype(BF16)
        dys = None
        dya = None
        first = i == 0
        for s in range(NCHIP):
            cs = slice(s * cw, (s + 1) * cw)
            p_s = _dot_nt(dbs_b[:, cs], wbs_ref[s])
            p_a = _dot_nt(dba_b[:, cs], wba_ref[s])
            dys = p_s if dys is None else dys + p_s
            dya = p_a if dya is None else dya + p_a
            w_s = _dot_tn(ys_b, dbs_b[:, cs])
            w_a = _dot_tn(ya_b, dba_b[:, cs])

            @pl.when(first)
            def _(s=s, w_s=w_s, w_a=w_a):
                abs_ref[s] = w_s
                aba_ref[s] = w_a

            @pl.when(jnp.logical_not(first))
            def _(s=s, w_s=w_s, w_a=w_a):
                abs_ref[s] += w_s
                aba_ref[s] += w_a

        dys_ref[...] = dys
        dya_ref[...] = dya

        @pl.when(first)
        def _():
            aout_ref[...] = dwout
            dg2_ref[...] = dg2

        @pl.when(jnp.logical_not(first))
        def _():
            aout_ref[...] += dwout
            dg2_ref[...] += dg2

        @pl.when(i == ni - 1)
        def _():
            dwbs_ref[...] = abs_ref[...].astype(BF16)
            dwba_ref[...] = aba_ref[...].astype(BF16)
            dwout_ref[...] = aout_ref[...].astype(BF16)

    row = lambda w: pl.BlockSpec((tl, w), lambda i: (i, 0))
    w4 = pl.BlockSpec((NCHIP, SSM, cw), lambda i: (0, 0, 0))
    sq = pl.BlockSpec((D, D), lambda i: (0, 0))
    vec = pl.BlockSpec((1, D), lambda i: (0, 0))
    return _pcall(body, name="merge_out_bwd", grid=(ni,), in_specs=[row(D)] + _merge_specs(tl) + dep_specs,
                  out_specs=[row(SSM), row(AW), pl.BlockSpec((tl, 2 * D), lambda i: (i, COL_GS // 16)), w4, w4, sq,
                             vec],
                  out_shape=[_sds((L, SSM)), _sds((L, AW)), _sds((L, NCOL)),
                             _sds((NCHIP, SSM, cw), BF16), _sds((NCHIP, AW, cw), BF16), _sds((D, D), BF16),
                             _sds((1, D))],
                  scratch=[pltpu.VMEM((NCHIP, SSM, cw), F32), pltpu.VMEM((NCHIP, AW, cw), F32),
                           pltpu.VMEM((D, D), F32)],
                  vmem_mb=56)(dxn, ys, ya, proj, proj, wbs4, wba4, wout, g2, *deps)


def _loss_head(y, target):
    tl = 512

    def body(y_ref, t_ref, loss_ref, dy_ref):
        i = pl.program_id(0)
        err = y_ref[...] - t_ref[...]
        dy_ref[...] = err / D
        part = 0.5 * jnp.sum(jnp.mean(err * err, axis=-1, keepdims=True), axis=0, keepdims=True)
        part = jnp.broadcast_to(part, (8, 128))

        @pl.when(i == 0)
        def _():
            loss_ref[...] = part

        @pl.when(i > 0)
        def _():
            loss_ref[...] += part

    row = pl.BlockSpec((tl, D), lambda i: (i, 0))
    return _pcall(body, name="loss_head", grid=(L // tl,), in_specs=[row, row],
                  out_specs=[pl.BlockSpec((8, 128), lambda i: (0, 0)), row],
                  out_shape=[_sds((8, 128)), _sds((L, D))], vmem_mb=40)(y, target)


def _adamw_math(w, g, m, v):
    m = ADAM_B1 * m + (1.0 - ADAM_B1) * g
    v = ADAM_B2 * v + (1.0 - ADAM_B2) * (g * g)
    m_hat = m / (1.0 - ADAM_B1 ** ADAM_STEP)
    v_hat = v / (1.0 - ADAM_B2 ** ADAM_STEP)
    delta = -ADAM_LR * (m_hat / (jnp.sqrt(v_hat) + ADAM_EPS) + ADAM_WD * w)
    return delta, m, v


def _adamw_big(layer, w, m, v, own, sib, prev):
    _, r, c = w.shape
    tr = min(r, 128)

    def body(w_ref, m_ref, v_ref, own_ref, sib_ref, *rest):
        g_ref, d_ref, nm_ref, nv_ref = rest[-4:]
        a = own_ref[0].astype(F32)
        b = sib_ref[0].astype(F32)
        for s in range(1, NCHIP):
            a = a + own_ref[s].astype(F32)
            b = b + sib_ref[s].astype(F32)
        g = a + b
        delta, nm, nv = _adamw_math(w_ref[...], g, m_ref[...], v_ref[...])
        g_ref[...] = g
        d_ref[...] = delta
        nm_ref[...] = nm
        nv_ref[...] = nv

    lay = pl.BlockSpec((None, tr, c), lambda i: (layer, i, 0))
    slots = pl.BlockSpec((NCHIP, tr, c), lambda i: (0, i, 0))
    ins = [w, m, v, own, sib]
    in_specs = [lay, lay, lay, slots, slots]
    aliases = {}
    if prev is not None:
        ins += list(prev)
        in_specs += [pl.BlockSpec(memory_space=pl.ANY)] * 4
        aliases = {5 + k: k for k in range(4)}
    return _pcall(body, name="adamw_big", grid=(r // tr,), in_specs=in_specs, out_specs=[lay] * 4,
                  out_shape=[_sds(w.shape)] * 4, aliases=aliases, vmem_mb=48)(*ins)


SMALL_TILE = 512


def _adamw_small(w, parts, m, v):
    r = w.shape[0]

    def body(w_ref, p_ref, m_ref, v_ref, g_ref, d_ref, nm_ref, nv_ref):
        g = p_ref[0]
        for dev in range(1, 8):
            g = g + p_ref[dev]
        delta, nm, nv = _adamw_math(w_ref[...], g, m_ref[...], v_ref[...])
        g_ref[...] = g
        d_ref[...] = delta
        nm_ref[...] = nm
        nv_ref[...] = nv

    row = pl.BlockSpec((SMALL_TILE, 128), lambda i: (i, 0))
    return _pcall(body, name="adamw_small", grid=(r // SMALL_TILE,),
                  in_specs=[row, pl.BlockSpec((8, SMALL_TILE, 128), lambda i: (0, i, 0)), row, row],
                  out_specs=[row] * 4, out_shape=[_sds((r, 128))] * 4, vmem_mb=40)(w, parts, m, v)


def _place():
    x, y, c = lax.axis_index("x"), lax.axis_index("y"), lax.axis_index("c")
    return x, y, c, 2 * x + y


def _chip_peer(x, y, j):
    return (1 - x if j & 2 else x), (1 - y if j & 1 else y)


_HBM = pl.BlockSpec(memory_space=pltpu.HBM)
_SEM = pl.BlockSpec(memory_space=pltpu.SEMAPHORE)
_EFFECT = pltpu.SideEffectType.DATAFLOW_SIDE_EFFECTING


def _in_hbm(a):
    return pltpu.with_memory_space_constraint(a, pltpu.HBM)


def _copies_start(name, plan, ncopy, srcs, lands, dep=None):
    n = len(srcs) + len(lands)
    deps, dep_specs = _token_operand(dep)

    def body(*refs):
        send_sems, recv_sems = refs[n + len(deps)], refs[n + len(deps) + 1]
        token = refs[-1]
        for k, (src, dst, dev) in enumerate(plan(refs[:len(srcs)], refs[len(srcs):n])):
            pltpu.make_async_remote_copy(src_ref=src, dst_ref=dst, send_sem=send_sems.at[k],
                                         recv_sem=recv_sems.at[k], device_id=dev, device_id_type=MESH).start()
        token[...] = jnp.zeros_like(token)

    bufs = list(srcs) + list(lands)
    outs = pl.pallas_call(
        body, name=name,
        out_shape=(pltpu.SemaphoreType.DMA((ncopy,)), pltpu.SemaphoreType.DMA((ncopy,)),
                   *[pltpu.HBM(a.shape, a.dtype) for a in bufs], _sds((8, 128))),
        in_specs=[_HBM] * n + dep_specs,
        out_specs=(_SEM, _SEM, *[_HBM] * n, pl.BlockSpec(memory_space=pltpu.VMEM)),
        input_output_aliases={i: 2 + i for i in range(n)},
        compiler_params=pltpu.CompilerParams(has_side_effects=_EFFECT),
    )(*[_in_hbm(a) for a in bufs], *deps)
    return outs[0], outs[1], list(outs[2:2 + len(srcs)]), list(outs[2 + len(srcs):2 + n]), outs[-1]


def _copies_wait(name, plan, send_sems, recv_sems, srcs, lands, after):
    n = len(srcs) + len(lands)
    after = list(after)

    def body(*refs):
        s_sems, r_sems = refs[n], refs[n + 1]
        for k, (src, dst, dev) in enumerate(plan(refs[:len(srcs)], refs[len(srcs):n])):
            cp = pltpu.make_async_remote_copy(src_ref=src, dst_ref=dst, send_sem=s_sems.at[k],
                                              recv_sem=r_sems.at[k], device_id=dev, device_id_type=MESH)
            cp.wait_send()
            cp.wait_recv()

    bufs = list(srcs) + list(lands)
    outs = pl.pallas_call(
        body, name=name,
        out_shape=tuple(pltpu.HBM(a.shape, a.dtype) for a in bufs),
        in_specs=[_HBM] * n + [_SEM, _SEM] + [pl.BlockSpec(memory_space=pl.ANY)] * len(after),
        out_specs=tuple([_HBM] * n),
        input_output_aliases={i: i for i in range(n)},
        compiler_params=pltpu.CompilerParams(has_side_effects=_EFFECT),
    )(*bufs, send_sems, recv_sems, *after)
    return list(outs[:len(srcs)]), list(outs[len(srcs):])


def _with_own_slot(block, slot, nslots=NCHIP):
    land = lax.empty((nslots,) + block.shape, block.dtype)
    return lax.dynamic_update_slice(land, block[None], (slot,) + (0,) * block.ndim)


def _gather_plan(srcs, lands):
    x, y, c, s = _place()
    return [(src, land.at[s], (*_chip_peer(x, y, j), c))
            for src, land in zip(srcs, lands) for j in range(1, NCHIP)]


def _exchange_plan(srcs, lands):
    x, y, c, s = _place()
    return [(src.at[s ^ j], land.at[s], (*_chip_peer(x, y, j), c))
            for src, land in zip(srcs, lands) for j in range(1, NCHIP)]


def _sibling_plan(srcs, lands):
    x, y, c, _ = _place()
    return [(src, land, (x, y, 1 - c)) for src, land in zip(srcs, lands)]


def _everyone_plan(srcs, lands):
    x, y, c, _ = _place()
    me = 4 * x + 2 * y + c
    return [(srcs[0], lands[0].at[me], (*_chip_peer(x, y, j >> 1), 1 - c if j & 1 else c)) for j in range(1, 8)]


def _blockdiag_in(bb):
    eye = jnp.eye(8, dtype=F32)[None, :, None, None, :, None]
    t = bb.reshape(2, NCB, 8, 16, 64).transpose(1, 2, 3, 0, 4)
    return (t[:, :, :, :, None, :] * eye).reshape(NCB, 128, SW)


def _blockdiag_in_t(dw):
    eye = jnp.eye(8, dtype=F32)[None, :, None, None, :, None]
    t = jnp.sum(dw.reshape(NCB, 8, 16, 2, 8, 64) * eye, axis=4)
    return t.transpose(3, 0, 1, 2, 4).reshape(2, 32, 16, 64)


def _blockdiag_out(cc):
    eye = jnp.eye(8, dtype=F32)[None, None, :, None, :, None]
    t = cc.reshape(2, NCB, 8, 16, 64).transpose(1, 0, 4, 2, 3)
    return (t[:, :, None, :, :, :] * eye).reshape(NCB, SW, 128)


def _blockdiag_out_t(dw):
    eye = jnp.eye(8, dtype=F32)[None, None, :, None, :, None]
    t = jnp.sum(dw.reshape(NCB, 2, 8, 64, 8, 16) * eye, axis=2)
    return t.transpose(1, 0, 3, 4, 2).reshape(2, 32, 16, 64)


def _flatten_small(parts):
    flat = jnp.concatenate([p.reshape(-1) for p in parts])
    n = flat.shape[0]
    rows = -(-n // (128 * SMALL_TILE)) * SMALL_TILE
    return jnp.pad(flat, (0, rows * 128 - n)).reshape(rows, 128)


def _unflatten_small(buf, like):
    flat = buf.reshape(-1)
    out, at = [], 0
    for p in like:
        out.append(flat[at:at + p.size].reshape(p.shape))
        at += p.size
    return out


def kernel(x, pre_norm_g, w_in, lambda_re, lambda_im, log_dt, b_re, b_im, c_re, c_im, d_skip, w_glu, b_glu, w_branch_s, w_branch_a, w_out, post_norm_g, loss_target, m_pre_norm_g, m_w_in, m_lambda_re, m_lambda_im, m_log_dt, m_b_re, m_b_im, m_c_re, m_c_im, m_d_skip, m_w_glu, m_b_glu, m_w_branch_s, m_w_branch_a, m_w_out, m_post_norm_g, v_pre_norm_g, v_w_in, v_lambda_re, v_lambda_im, v_log_dt, v_b_re, v_b_im, v_c_re, v_c_im, v_d_skip, v_w_glu, v_b_glu, v_w_branch_s, v_w_branch_a, v_w_out, v_post_norm_g):
    ng = DEPTH * 32
    lre3 = lambda_re.reshape(ng, 1, 64)
    lim3 = lambda_im.reshape(ng, 1, 64)
    ldt3 = log_dt.reshape(ng, 1, 1)
    bre_t = b_re.transpose(0, 1, 3, 2).reshape(ng, 16, 64)
    bim_t = b_im.transpose(0, 1, 3, 2).reshape(ng, 16, 64)
    ar, ai, bbr, bbi = _s5_prep(lre3, lim3, ldt3, bre_t, bim_t)
    ar4 = ar.reshape(DEPTH, NCB, 1, HW)
    ai4 = ai.reshape(DEPTH, NCB, 1, HW)
    bb = jnp.stack([bbr.reshape(DEPTH, 32, 16, 64), bbi.reshape(DEPTH, 32, 16, 64)], axis=1)
    cc = jnp.stack([c_re, -c_im], axis=1)

    chip = 2 * lax.axis_index("x") + lax.axis_index("y")
    device = 2 * chip + lax.axis_index("c")

    def gather_start(l, dep):
        srcs = [w_in[l].astype(BF16), w_glu[l].astype(BF16), w_branch_s[l].astype(BF16),
                w_branch_a[l].astype(BF16), w_out[l].astype(BF16)]
        return _copies_start(f"gather_start_{l}", _gather_plan, 15, srcs, [_with_own_slot(a, chip) for a in srcs],
                             dep)

    xs = [x[0]]
    saved = []
    flight = gather_start(0, None)
    for l in range(DEPTH):
        send, recv, srcs, lands, _ = flight
        _, (w4, wglu4, wbs4, wba4, wout4) = _copies_wait(f"gather_wait_{l}", _gather_plan, send, recv, srcs, lands,
                                                         [xs[l]])
        started = None
        if l + 1 < DEPTH:
            flight = gather_start(l + 1, wglu4)
            started = flight[4]
        wglu = wglu4.reshape(SSM, SSM)
        wout = wout4.reshape(D, D)
        wb4 = _blockdiag_in(bb[l]).astype(BF16)
        wc4 = _blockdiag_out(cc[l]).astype(BF16)
        g1 = pre_norm_g[l].reshape(1, D)
        g2 = post_norm_g[l].reshape(1, D)
        dsk = d_skip[l].reshape(1, SSM)
        bgl = b_glu[l].reshape(1, SSM)
        proj = _rms_proj_fwd(xs[l], g1, w4, started)
        states, yraw = _s5_scan_fwd(proj, wb4, wc4, ar4[l], ai4[l])
        ys = _s5_tail_fwd(yraw, proj, dsk, wglu, bgl)
        o, lse = _attn_core_fwd(proj)
        ya = _attn_mix_fwd(o, lse, proj)
        xs.append(_merge_out_fwd(ys, ya, proj, wbs4, wba4, wout, g2, xs[l]))
        saved.append((w4, wglu, wbs4, wba4, wout, wb4, wc4, g1, g2, dsk, bgl, proj, states, yraw, ys, ya, o, lse))

    loss_part, dx = _loss_head(xs[DEPTH], loss_target[0])
    loss = lax.psum(loss_part[0, 0], ("x", "y", "c"))

    big_w = (w_in, w_glu.reshape(DEPTH, 128, SSM), w_branch_s, w_branch_a, w_out)
    big_m = (m_w_in, m_w_glu, m_w_branch_s, m_w_branch_a, m_w_out)
    big_v = (v_w_in, v_w_glu, v_w_branch_s, v_w_branch_a, v_w_out)
    big_out = [None] * 5
    small = {k: [None] * DEPTH for k in ("g1", "da", "dbb", "dcc", "dsk", "bgl", "g2")}
    ici = [None] * DEPTH
    d2d = [None] * DEPTH

    def exchange_start(l, parts):
        lands = [_with_own_slot(lax.dynamic_index_in_dim(p, chip, 0, keepdims=False), chip) for p in parts]
        ici[l] = _copies_start(f"exchange_start_{l}", _exchange_plan, 15, parts, lands)

    def handoff_start(l, after):
        send, recv, srcs, lands, _ = ici[l]
        _, own = _copies_wait(f"exchange_wait_{l}", _exchange_plan, send, recv, srcs, lands, after)
        d2d[l] = _copies_start(f"handoff_start_{l}", _sibling_plan, 5, own,
                               [lax.empty(a.shape, a.dtype) for a in own])

    def update(l, after):
        send, recv, own, lands, _ = d2d[l]
        own, sib = _copies_wait(f"handoff_wait_{l}", _sibling_plan, send, recv, own, lands, after)
        for k in range(5):
            big_out[k] = _adamw_big(l, big_w[k], big_m[k], big_v[k], own[k], sib[k], big_out[k])

    for l in reversed(range(DEPTH)):
        w4, wglu, wbs4, wba4, wout, wb4, wc4, g1, g2, dsk, bgl, proj, states, yraw, ys, ya, o, lse = saved[l]
        started = None
        if l + 1 < DEPTH:
            started = ici[l + 1][4] + d2d[l + 2][4] if l + 2 < DEPTH else ici[l + 1][4]
        dys, dya, dproj, dwbs, dwba, dwout, dg2 = _merge_out_bwd(dx, ys, ya, proj, wbs4, wba4, wout, g2, started)
        dyraw, du_skip, dproj, dwglu, dbgl, ddsk = _s5_tail_bwd(dys, yraw, proj, dsk, wglu, bgl, dproj)
        dproj, dwb, dwc, da = _s5_scan_bwd(dyraw, states, proj, du_skip, wb4, wc4, ar4[l], ai4[l], dproj)
        do, cterm, dproj = _attn_mix_bwd(dya, o, lse, proj, dproj)
        dproj = _attn_core_bwd(proj, do, lse, cterm, dproj)
        dh, dwin = _proj_bwd(dproj, xs[l], g1, w4)
        parts = [dwin, dwglu.reshape(NCHIP, 128, SSM), dwbs, dwba, dwout.reshape(NCHIP, D // NCHIP, D)]
        exchange_start(l, parts)
        dx, dg1 = _rms_bwd(dh, xs[l], g1, dx, ici[l][4])
        if l + 1 < DEPTH:
            handoff_start(l + 1, [ici[l][4]])
        small["g1"][l], small["g2"][l], small["dsk"][l], small["bgl"][l] = dg1, dg2, ddsk, dbgl
        small["da"][l], small["dbb"][l], small["dcc"][l] = da, _blockdiag_in_t(dwb), _blockdiag_out_t(dwc)

    da = jnp.stack(small["da"]).reshape(DEPTH, NCB, 2, HW)
    gar = da[:, :, 0].reshape(ng, 1, 64)
    gai = da[:, :, 1].reshape(ng, 1, 64)
    dbb = jnp.stack(small["dbb"])
    dcc = jnp.stack(small["dcc"])
    dlre, dlim, dldt, dbre_t, dbim_t = _s5_prep_bwd(
        lre3, lim3, ldt3, bre_t, bim_t, gar, gai, dbb[:, 0].reshape(ng, 16, 64), dbb[:, 1].reshape(ng, 16, 64))
    unt = lambda t: t.reshape(DEPTH, 32, 16, 64).transpose(0, 1, 3, 2)
    small_w = [pre_norm_g, lambda_re, lambda_im, log_dt, b_re, b_im, c_re, c_im, d_skip, b_glu, post_norm_g]
    small_m = [m_pre_norm_g, m_lambda_re, m_lambda_im, m_log_dt, m_b_re, m_b_im, m_c_re, m_c_im, m_d_skip,
               m_b_glu, m_post_norm_g]
    small_v = [v_pre_norm_g, v_lambda_re, v_lambda_im, v_log_dt, v_b_re, v_b_im, v_c_re, v_c_im, v_d_skip,
               v_b_glu, v_post_norm_g]
    small_g = [jnp.stack(small["g1"]).reshape(DEPTH, D), dlre.reshape(DEPTH, 32, 64), dlim.reshape(DEPTH, 32, 64),
               dldt.reshape(DEPTH, 32), unt(dbre_t), unt(dbim_t), dcc[:, 0], -dcc[:, 1],
               jnp.stack(small["dsk"]).reshape(DEPTH, SSM), jnp.stack(small["bgl"]).reshape(DEPTH, SSM),
               jnp.stack(small["g2"]).reshape(DEPTH, D)]
    part = _flatten_small(small_g)
    send, recv, srcs, lands, token = _copies_start("small_start", _everyone_plan, 7, [part],
                                                   [_with_own_slot(part, device, 8)])
    update(DEPTH - 1, [token, ici[0][4]])
    for l in range(DEPTH - 2, 0, -1):
        update(l, [out[0] for out in big_out])
    handoff_start(0, [out[0] for out in big_out])
    update(0, [d2d[0][4]])
    _, (parts8,) = _copies_wait("small_wait", _everyone_plan, send, recv, srcs, lands, [out[0] for out in big_out])
    g_flat, d_flat, nm_flat, nv_flat = _adamw_small(_flatten_small(small_w), parts8, _flatten_small(small_m),
                                                    _flatten_small(small_v))
    sg = _unflatten_small(g_flat, small_w)
    sd = _unflatten_small(d_flat, small_w)
    snm = _unflatten_small(nm_flat, small_w)
    snv = _unflatten_small(nv_flat, small_w)

    def ordered(sm, which):
        bg = [big_out[k][which] for k in range(5)]
        bg[1] = bg[1].reshape(DEPTH, 128, SSM)
        return [sm[0], bg[0], sm[1], sm[2], sm[3], sm[4], sm[5], sm[6], sm[7], sm[8], bg[1], sm[9], bg[2], bg[3],
                bg[4], sm[10]]

    return (loss, dx[None], *ordered(sg, 0), *ordered(sd, 1), *ordered(snm, 2), *ordered(snv, 3))
```

```python
import math

import jax
import jax.numpy as jnp
from jax import lax
from jax.experimental import pallas as pl
from jax.experimental.pallas import tpu as pltpu

F32 = jnp.float32
BF16 = jnp.bfloat16
MESH = pl.DeviceIdType.MESH

DEPTH = 4
L = 2048
D = 1024
NCOL = 8192
SSM = 512
AW = 512
QKV = 1536
RMS_EPS = 1e-6
NCHIP = 4

COL_U, COL_ZS, COL_Q, COL_K, COL_V, COL_ZA, COL_GS, COL_GA = 0, 4, 8, 20, 32, 44, 48, 56

NSEG = 8
SEG = L // NSEG
NCB = 4
HW = 512
SW = 2 * HW

HEAD = 128
DILATIONS = (1, 4, 16)
SCALE = HEAD ** -0.5
NEG = -1e30

ADAM_LR, ADAM_B1, ADAM_B2, ADAM_EPS, ADAM_WD, ADAM_STEP = 0.001, 0.9, 0.999, 1e-08, 0.01, 10


def _sds(shape, dtype=F32):
    return jax.ShapeDtypeStruct(shape, dtype)


def _pcall(body, *, name, out_shape, grid=None, in_specs=None, out_specs=None, scratch=(), vmem_mb=None,
           aliases=None):
    params = {}
    if vmem_mb is not None:
        params["vmem_limit_bytes"] = vmem_mb << 20
    kw = {}
    if grid is not None:
        kw["grid"] = grid
    if in_specs is not None:
        kw["in_specs"] = in_specs
    if out_specs is not None:
        kw["out_specs"] = out_specs
    return pl.pallas_call(body, name=name, out_shape=out_shape, scratch_shapes=list(scratch),
                          compiler_params=pltpu.CompilerParams(**params),
                          input_output_aliases=aliases or {}, **kw)


def _dot(a, b):
    return jnp.dot(a, b, preferred_element_type=F32)


def _dot_nt(a, b):
    return lax.dot_general(a, b, (((1,), (1,)), ((), ())), preferred_element_type=F32)


def _dot_tn(a, b):
    return lax.dot_general(a, b, (((0,), (0,)), ((), ())), preferred_element_type=F32)


def _sigmoid(x):
    return jax.nn.sigmoid(x)


_GELU_K = math.sqrt(2.0 / math.pi)


def _gelu(x):
    return 0.5 * x * (1.0 + jnp.tanh(_GELU_K * (x + 0.044715 * (x * x * x))))


def _gelu_grad(x):
    t = jnp.tanh(_GELU_K * (x + 0.044715 * (x * x * x)))
    return 0.5 * (1.0 + t) + 0.5 * x * (1.0 - t * t) * (_GELU_K * (1.0 + 3.0 * 0.044715 * (x * x)))


def _token_operand(dep):
    if dep is None:
        return [], []
    return [dep], [pl.BlockSpec(memory_space=pl.ANY)]


def _rms_proj_fwd(x, g, w4, dep=None):
    tl, tn = 512, 1024
    deps, dep_specs = _token_operand(dep)

    def body(x_ref, g_ref, w_ref, *rest):
        o_ref = rest[-1]
        xx = x_ref[...]
        inv = lax.rsqrt(jnp.mean(xx * xx, axis=-1, keepdims=True) + RMS_EPS)
        h = (xx * inv * g_ref[...]).astype(BF16)
        o_ref[...] = _dot(h, w_ref[...])

    return _pcall(
        body, name="rms_proj_fwd", grid=(NCOL // tn, L // tl),
        in_specs=[pl.BlockSpec((tl, D), lambda j, i: (i, 0)),
                  pl.BlockSpec((1, D), lambda j, i: (0, 0)),
                  pl.BlockSpec((None, D, tn), lambda j, i: (lax.div(j, 2), 0, lax.rem(j, 2)))] + dep_specs,
        out_specs=pl.BlockSpec((tl, tn), lambda j, i: (i, j)),
        out_shape=_sds((L, NCOL)), vmem_mb=40)(x, g, w4, *deps)


def _proj_bwd(dproj, x, g, w4):
    tl, tn = 512, 1024
    ni = L // tl

    def body(dp_ref, x_ref, g_ref, w_ref, dh_ref, dw_ref, acc_ref):
        j, i = pl.program_id(0), pl.program_id(1)
        xx = x_ref[...]
        inv = lax.rsqrt(jnp.mean(xx * xx, axis=-1, keepdims=True) + RMS_EPS)
        h = (xx * inv * g_ref[...]).astype(BF16)
        dp = dp_ref[...].astype(BF16)
        part = _dot_tn(h, dp)

        @pl.when(i == 0)
        def _():
            acc_ref[...] = part

        @pl.when(i > 0)
        def _():
            acc_ref[...] += part

        @pl.when(i == ni - 1)
        def _():
            dw_ref[...] = acc_ref[...].astype(BF16)

        rows = pl.ds(pl.multiple_of(i * tl, tl), tl)
        dh = _dot_nt(dp, w_ref[...])

        @pl.when(j == 0)
        def _():
            dh_ref[rows, :] = dh

        @pl.when(j > 0)
        def _():
            dh_ref[rows, :] += dh

    return _pcall(
        body, name="proj_bwd", grid=(NCOL // tn, ni),
        in_specs=[pl.BlockSpec((tl, tn), lambda j, i: (i, j)),
                  pl.BlockSpec((tl, D), lambda j, i: (i, 0)),
                  pl.BlockSpec((1, D), lambda j, i: (0, 0)),
                  pl.BlockSpec((None, D, tn), lambda j, i: (lax.div(j, 2), 0, lax.rem(j, 2)))],
        out_specs=[pl.BlockSpec((L, D), lambda j, i: (0, 0)),
                   pl.BlockSpec((None, D, tn), lambda j, i: (lax.div(j, 2), 0, lax.rem(j, 2)))],
        out_shape=[_sds((L, D)), _sds((NCHIP, D, NCOL // NCHIP), BF16)],
        scratch=[pltpu.VMEM((D, tn), F32)], vmem_mb=56)(dproj, x, g, w4)


def _rms_bwd(dh, x, g, dxn, dep=None):
    tl = 512
    deps, dep_specs = _token_operand(dep)

    def body(dh_ref, x_ref, g_ref, dxn_ref, *rest):
        dx_ref, dg_ref = rest[len(deps):]
        i = pl.program_id(0)
        xx = x_ref[...]
        inv = lax.rsqrt(jnp.mean(xx * xx, axis=-1, keepdims=True) + RMS_EPS)
        nrm = xx * inv
        dh_v = dh_ref[...]
        dn = dh_v * g_ref[...]
        dx_ref[...] = dxn_ref[...] + inv * (dn - nrm * jnp.mean(dn * nrm, axis=-1, keepdims=True))
        part = jnp.sum(dh_v * nrm, axis=0, keepdims=True)

        @pl.when(i == 0)
        def _():
            dg_ref[...] = part

        @pl.when(i > 0)
        def _():
            dg_ref[...] += part

    row = pl.BlockSpec((tl, D), lambda i: (i, 0))
    vec = pl.BlockSpec((1, D), lambda i: (0, 0))
    return _pcall(body, name="rms_bwd", grid=(L // tl,), in_specs=[row, row, vec, row] + dep_specs,
                  out_specs=[row, vec], out_shape=[_sds((L, D)), _sds((1, D))], vmem_mb=40)(dh, x, g, dxn, *deps)


def _s5_param_math(lre, lim, ldt, bre, bim):
    lr = jnp.minimum(lre, -1e-4)
    dt = jnp.exp(ldt)
    mag = jnp.exp(lr * dt)
    ar = mag * jnp.cos(lim * dt)
    ai = mag * jnp.sin(lim * dt)
    den = lr * lr + lim * lim
    nr = ar - 1.0
    qr = (nr * lr + ai * lim) / den
    qi = (ai * lr - nr * lim) / den
    return ar, ai, qr * bre - qi * bim, qr * bim + qi * bre


def _group_of_lane(width):
    lane = lax.broadcasted_iota(jnp.int32, (16, width), 1)
    return lax.shift_right_logical(jnp.bitwise_and(lane, HW - 1), 6)


def _vec_spec():
    return pl.BlockSpec((None, 1, HW), lambda n: (n, 0, 0))


def _mat_spec():
    return pl.BlockSpec((None, 16, HW), lambda n: (n, 0, 0))


def _s5_prep(lre, lim, ldt, bre, bim, cre, cim):
    n = lre.shape[0]

    def body(lre_ref, lim_ref, ldt_ref, bre_ref, bim_ref, cre_ref, cim_ref, ar_ref, ai_ref, wb_ref, wct_ref):
        ar, ai, bbr, bbi = _s5_param_math(lre_ref[...], lim_ref[...], ldt_ref[...], bre_ref[...], bim_ref[...])
        ar_ref[...] = ar
        ai_ref[...] = ai
        group = _group_of_lane(SW)
        bb = jnp.concatenate([bbr, bbi], axis=1)
        cc = jnp.concatenate([cre_ref[...], -cim_ref[...]], axis=1)
        for g in range(8):
            rows = pl.ds(16 * g, 16)
            wb_ref[rows, :] = jnp.where(group == g, bb, 0.0).astype(BF16)
            wct_ref[rows, :] = jnp.where(group == g, cc, 0.0).astype(BF16)

    wide = pl.BlockSpec((None, 128, SW), lambda n: (n, 0, 0))
    return _pcall(body, name="s5_prep", grid=(n,),
                  in_specs=[_vec_spec()] * 3 + [_mat_spec()] * 4,
                  out_specs=[_vec_spec(), _vec_spec(), wide, wide],
                  out_shape=[_sds((n, 1, HW)), _sds((n, 1, HW)), _sds((n, 128, SW), BF16), _sds((n, 128, SW), BF16)])(
                      lre, lim, ldt, bre, bim, cre, cim)


def _s5_prep_bwd(lre, lim, ldt, bre, bim, ga, gbb):
    n = lre.shape[0]

    def body(lre_ref, lim_ref, ldt_ref, bre_ref, bim_ref, ga_ref, gbb_ref,
             dlre_ref, dlim_ref, dldt_ref, dbre_ref, dbim_ref):
        _, vjp = jax.vjp(_s5_param_math, lre_ref[...], lim_ref[...], ldt_ref[...], bre_ref[...], bim_ref[...])
        ga, gbb = ga_ref[...], gbb_ref[...]
        dlre, dlim, dldt, dbre, dbim = vjp((ga[:, :HW], ga[:, HW:], gbb[:, :HW], gbb[:, HW:]))
        dlre_ref[...] = dlre
        dlim_ref[...] = dlim
        dldt_ref[...] = dldt
        dbre_ref[...] = dbre
        dbim_ref[...] = dbim

    return _pcall(body, name="s5_prep_bwd", grid=(n,),
                  in_specs=[_vec_spec()] * 3 + [_mat_spec()] * 2 +
                  [pl.BlockSpec((None, 1, SW), lambda n: (n, 0, 0)), pl.BlockSpec((None, 16, SW), lambda n: (n, 0, 0))],
                  out_specs=[_vec_spec()] * 3 + [_mat_spec()] * 2,
                  out_shape=[_sds((n, 1, HW))] * 3 + [_sds((n, 16, HW))] * 2)(lre, lim, ldt, bre, bim, ga, gbb)


def _diag_blocks(full):
    group = _group_of_lane(SW)
    out = jnp.where(group == 0, full[0:16, :], 0.0)
    for g in range(1, 8):
        out = out + jnp.where(group == g, full[16 * g:16 * g + 16, :], 0.0)
    return out


def _permute_rows(dst_ref, src_ref):
    for s in range(NSEG):
        dst_ref[pl.ds(s, SEG, stride=NSEG), :] = src_ref[pl.ds(s * SEG, SEG), :]


def _unpermute_rows(src_ref, s):
    return src_ref[pl.ds(s, SEG, stride=NSEG), :]


def _row_of(slab, rows, j):
    return jnp.sum(jnp.where(rows == j, slab, 0.0), axis=0, keepdims=True)


def _s5_scan_fwd(proj, wb4, wc4, ar4, ai4):
    def body(u_ref, wb_ref, wc_ref, ar_ref, ai_ref, x_ref, y_ref, up_ref, d_ref, yp_ref):
        _permute_rows(up_ref, u_ref)
        d_ref[...] = _dot(up_ref[...].astype(BF16), wb_ref[...])
        ar = jnp.broadcast_to(ar_ref[...], (NSEG, HW))
        ai = jnp.broadcast_to(ai_ref[...], (NSEG, HW))
        zero = jnp.zeros((NSEG, HW), F32)

        def drive(t):
            row = pl.multiple_of(t * NSEG, NSEG)
            dd = d_ref[pl.ds(row, NSEG), :]
            return row, dd[:, :HW], dd[:, HW:]

        def local_step(t, c):
            xr, xi = c
            _, dr, di = drive(t)
            return ar * xr - ai * xi + dr, ar * xi + ai * xr + di

        fr, fi = lax.fori_loop(0, SEG, local_step, (zero, zero), unroll=8)
        pr, pi_ = ar_ref[...], ai_ref[...]
        for _ in range(int(math.log2(SEG))):
            pr, pi_ = pr * pr - pi_ * pi_, 2.0 * pr * pi_
        rows = lax.broadcasted_iota(jnp.int32, (NSEG, HW), 0)
        cr, ci = zero, zero
        for j in range(NSEG - 1):
            sr, si = _row_of(cr, rows, j), _row_of(ci, rows, j)
            gr, gi = _row_of(fr, rows, j), _row_of(fi, rows, j)
            nr = pr * sr - pi_ * si + gr
            ni = pr * si + pi_ * sr + gi
            cr = jnp.where(rows == j + 1, nr, cr)
            ci = jnp.where(rows == j + 1, ni, ci)

        def true_step(t, c):
            xr, xi = c
            row, dr, di = drive(t)
            nr = ar * xr - ai * xi + dr
            ni = ar * xi + ai * xr + di
            x_ref[pl.ds(row, NSEG), :] = jnp.concatenate([nr, ni], axis=1)
            return nr, ni

        lax.fori_loop(0, SEG, true_step, (cr, ci), unroll=8)
        yp_ref[...] = _dot_nt(x_ref[...].astype(BF16), wc_ref[...])
        for s in range(NSEG):
            y_ref[pl.ds(s * SEG, SEG), :] = _unpermute_rows(yp_ref, s)

    return _pcall(
        body, name="s5_scan_fwd", grid=(NCB,),
        in_specs=[pl.BlockSpec((L, 128), lambda b: (0, COL_U + b)),
                  pl.BlockSpec((None, 128, SW), lambda b: (b, 0, 0)),
                  pl.BlockSpec((None, 128, SW), lambda b: (b, 0, 0)),
                  pl.BlockSpec((None, 1, HW), lambda b: (b, 0, 0)),
                  pl.BlockSpec((None, 1, HW), lambda b: (b, 0, 0))],
        out_specs=[pl.BlockSpec((None, L, SW), lambda b: (b, 0, 0)),
                   pl.BlockSpec((L, 128), lambda b: (0, b))],
        out_shape=[_sds((NCB, L, SW)), _sds((L, SSM))],
        scratch=[pltpu.VMEM((L, 128), F32), pltpu.VMEM((L, SW), F32), pltpu.VMEM((L, 128), F32)],
        vmem_mb=56)(proj, wb4, wc4, ar4, ai4)


def _s5_scan_bwd(dy, xs, proj, du_skip, wb4, wc4, ar4, ai4, dproj):
    def body(dy_ref, x_ref, u_ref, dus_ref, wb_ref, wc_ref, ar_ref, ai_ref, _,
             du_ref, dbb_ref, dcc_ref, da_ref, dyp_ref, up_ref, g_ref, dup_ref):
        _permute_rows(dyp_ref, dy_ref)
        _permute_rows(up_ref, u_ref)
        dyp = dyp_ref[...].astype(BF16)
        g_ref[...] = _dot(dyp, wc_ref[...])
        dcc = _diag_blocks(_dot_tn(dyp, x_ref[...].astype(BF16)))
        dcc_ref[...] = jnp.concatenate([dcc[:, :HW], -dcc[:, HW:]], axis=1)
        ar = jnp.broadcast_to(ar_ref[...], (NSEG, HW))
        ai = jnp.broadcast_to(ai_ref[...], (NSEG, HW))
        zero = jnp.zeros((NSEG, HW), F32)

        def load(t):
            row = pl.multiple_of(t * NSEG, NSEG)
            gg = g_ref[pl.ds(row, NSEG), :]
            return row, gg[:, :HW], gg[:, HW:]

        def local_step(k, c):
            lr, li = c
            _, gr, gi = load(SEG - 1 - k)
            return ar * lr + ai * li + gr, ar * li - ai * lr + gi

        fr, fi = lax.fori_loop(0, SEG, local_step, (zero, zero), unroll=8)
        pr, pi_ = ar_ref[...], -ai_ref[...]
        for _ in range(int(math.log2(SEG))):
            pr, pi_ = pr * pr - pi_ * pi_, 2.0 * pr * pi_
        rows = lax.broadcasted_iota(jnp.int32, (NSEG, HW), 0)
        cr, ci = zero, zero
        for j in range(NSEG - 1, 0, -1):
            sr, si = _row_of(cr, rows, j), _row_of(ci, rows, j)
            gr, gi = _row_of(fr, rows, j), _row_of(fi, rows, j)
            nr = pr * sr - pi_ * si + gr
            ni = pr * si + pi_ * sr + gi
            cr = jnp.where(rows == j - 1, nr, cr)
            ci = jnp.where(rows == j - 1, ni, ci)

        def true_step(k, c):
            lr, li, gar, gai = c
            t = SEG - 1 - k
            row, gr, gi = load(t)
            nr = ar * lr + ai * li + gr
            ni = ar * li - ai * lr + gi
            g_ref[pl.ds(row, NSEG), :] = jnp.concatenate([nr, ni], axis=1)
            prow = pl.multiple_of(jnp.maximum(t - 1, 0) * NSEG, NSEG)
            xp = x_ref[pl.ds(prow, NSEG), :]
            live = (t > 0).astype(F32)
            xr, xi = xp[:, :HW] * live, xp[:, HW:] * live
            return nr, ni, gar + (nr * xr + ni * xi), gai + (ni * xr - nr * xi)

        _, _, gar, gai = lax.fori_loop(0, SEG, true_step, (cr, ci, zero, zero), unroll=4)
        l0 = g_ref[1:NSEG, :]
        xl = x_ref[(SEG - 1) * NSEG:(SEG - 1) * NSEG + NSEG - 1, :]
        l0r, l0i, xlr, xli = l0[:, :HW], l0[:, HW:], xl[:, :HW], xl[:, HW:]
        gar_t = jnp.sum(gar, axis=0, keepdims=True) + jnp.sum(l0r * xlr + l0i * xli, axis=0, keepdims=True)
        gai_t = jnp.sum(gai, axis=0, keepdims=True) + jnp.sum(l0i * xlr - l0r * xli, axis=0, keepdims=True)
        da_ref[...] = jnp.concatenate([gar_t, gai_t], axis=1)
        lam = g_ref[...].astype(BF16)
        dbb_ref[...] = _diag_blocks(_dot_tn(up_ref[...].astype(BF16), lam))
        dup_ref[...] = _dot_nt(lam, wb_ref[...])
        for s in range(NSEG):
            sl = pl.ds(s * SEG, SEG)
            du_ref[sl, :] = _unpermute_rows(dup_ref, s) + dus_ref[sl, :]

    col = lambda off: pl.BlockSpec((L, 128), lambda b: (0, off + b))
    return _pcall(
        body, name="s5_scan_bwd", grid=(NCB,),
        in_specs=[col(0), pl.BlockSpec((None, L, SW), lambda b: (b, 0, 0)), col(COL_U), col(0),
                  pl.BlockSpec((None, 128, SW), lambda b: (b, 0, 0)),
                  pl.BlockSpec((None, 128, SW), lambda b: (b, 0, 0)),
                  pl.BlockSpec((None, 1, HW), lambda b: (b, 0, 0)),
                  pl.BlockSpec((None, 1, HW), lambda b: (b, 0, 0)),
                  pl.BlockSpec(memory_space=pl.ANY)],
        out_specs=[col(COL_U), pl.BlockSpec((None, 16, SW), lambda b: (b, 0, 0)),
                   pl.BlockSpec((None, 16, SW), lambda b: (b, 0, 0)),
                   pl.BlockSpec((None, 1, SW), lambda b: (b, 0, 0))],
        out_shape=[_sds((L, NCOL)), _sds((NCB, 16, SW)), _sds((NCB, 16, SW)), _sds((NCB, 1, SW))],
        scratch=[pltpu.VMEM((L, 128), F32), pltpu.VMEM((L, 128), F32), pltpu.VMEM((L, SW), F32),
                 pltpu.VMEM((L, 128), F32)],
        aliases={8: 0}, vmem_mb=56)(dy, xs, proj, du_skip, wb4, wc4, ar4, ai4, dproj)


def _s5_tail_fwd(yraw, proj, dsk, wglu, bglu):
    tl = 512

    def body(y_ref, u_ref, z_ref, dsk_ref, w_ref, b_ref, o_ref):
        y1 = y_ref[...] + dsk_ref[...] * u_ref[...]
        y2 = _gelu(y1)
        gl = _dot(y2.astype(BF16), w_ref[...]) + b_ref[...]
        z = z_ref[...]
        o_ref[...] = (y2 * _sigmoid(gl)) * (z * _sigmoid(z))

    blk = lambda c: pl.BlockSpec((tl, SSM), lambda i: (i, c))
    vec = pl.BlockSpec((1, SSM), lambda i: (0, 0))
    return _pcall(body, name="s5_tail_fwd", grid=(L // tl,),
                  in_specs=[blk(0), blk(0), blk(1), vec, pl.BlockSpec((SSM, SSM), lambda i: (0, 0)), vec],
                  out_specs=blk(0), out_shape=_sds((L, SSM)), vmem_mb=40)(yraw, proj, proj, dsk, wglu, bglu)


def _s5_tail_bwd(dys, yraw, proj, dsk, wglu, bglu, dproj):
    tl = 512
    ni = L // tl

    def body(dys_ref, y_ref, u_ref, z_ref, dsk_ref, w_ref, b_ref, _,
             dy_ref, dus_ref, dz_ref, dw_ref, db_ref, dd_ref, acc_ref):
        i = pl.program_id(0)
        u = u_ref[...]
        y1 = y_ref[...] + dsk_ref[...] * u
        y2 = _gelu(y1)
        y2b = y2.astype(BF16)
        sg = _sigmoid(_dot(y2b, w_ref[...]) + b_ref[...])
        y3 = y2 * sg
        z = z_ref[...]
        sz = _sigmoid(z)
        dys = dys_ref[...]
        dy3 = dys * (z * sz)
        dz_ref[...] = dys * y3 * (sz * (1.0 + z * (1.0 - sz)))
        dgl = (dy3 * y2) * (sg * (1.0 - sg))
        dglb = dgl.astype(BF16)
        dy2 = dy3 * sg + _dot_nt(dglb, w_ref[...])
        dy1 = dy2 * _gelu_grad(y1)
        dy_ref[...] = dy1
        dus_ref[...] = dsk_ref[...] * dy1
        dw = _dot_tn(y2b, dglb)
        db = jnp.sum(dgl, axis=0, keepdims=True)
        dd = jnp.sum(dy1 * u, axis=0, keepdims=True)

        @pl.when(i == 0)
        def _():
            acc_ref[...] = dw
            db_ref[...] = db
            dd_ref[...] = dd

        @pl.when(i > 0)
        def _():
            acc_ref[...] += dw
            db_ref[...] += db
            dd_ref[...] += dd

        @pl.when(i == ni - 1)
        def _():
            dw_ref[...] = acc_ref[...].astype(BF16)

    blk = lambda c: pl.BlockSpec((tl, SSM), lambda i: (i, c))
    vec = pl.BlockSpec((1, SSM), lambda i: (0, 0))
    mat = pl.BlockSpec((SSM, SSM), lambda i: (0, 0))
    return _pcall(body, name="s5_tail_bwd", grid=(ni,),
                  in_specs=[blk(0), blk(0), blk(0), blk(1), vec, mat, vec, pl.BlockSpec(memory_space=pl.ANY)],
                  out_specs=[blk(0), blk(0), blk(COL_ZS // 4), mat, vec, vec],
                  out_shape=[_sds((L, SSM)), _sds((L, SSM)), _sds((L, NCOL)), _sds((SSM, SSM), BF16),
                             _sds((1, SSM)), _sds((1, SSM))],
                  scratch=[pltpu.VMEM((SSM, SSM), F32)], aliases={7: 2},
                  vmem_mb=40)(dys, yraw, proj, proj, dsk, wglu, bglu, dproj)


def _attn_blocks(dil):
    nb = L // dil // HEAD
    return [(i * HEAD * dil + r, (i - 1) * HEAD * dil + r if i > 0 else None)
            for r in range(dil) for i in range(nb)]


def _rows(start, dil):
    return pl.ds(start, HEAD) if dil == 1 else pl.ds(start, HEAD, stride=dil)


def _band_masks():
    ri = lax.broadcasted_iota(jnp.int32, (HEAD, HEAD), 0)
    ci = lax.broadcasted_iota(jnp.int32, (HEAD, HEAD), 1)
    return ci <= ri, ci >= ri


def _qkv_specs(index):
    def spec(off):
        return pl.BlockSpec((L, HEAD), lambda gi, h: (0, off + index(gi, h)))
    return [spec(off) for off in (COL_Q, COL_K, COL_V)]


def _attn_core_fwd(proj):
    def body(q_ref, k_ref, v_ref, o_ref, lse_ref):
        gi = pl.program_id(0)
        m_cur, m_prev = _band_masks()
        for g, dil in enumerate(DILATIONS):
            @pl.when(gi == g)
            def _(dil=dil):
                for st, pst in _attn_blocks(dil):
                    r = _rows(st, dil)
                    q = q_ref[r, :].astype(BF16)
                    s_c = jnp.where(m_cur, _dot_nt(q, k_ref[r, :].astype(BF16)) * SCALE, NEG)
                    mx = jnp.max(s_c, axis=-1, keepdims=True)
                    if pst is not None:
                        rp = _rows(pst, dil)
                        s_p = jnp.where(m_prev, _dot_nt(q, k_ref[rp, :].astype(BF16)) * SCALE, NEG)
                        mx = jnp.maximum(mx, jnp.max(s_p, axis=-1, keepdims=True))
                    p_c = jnp.exp(s_c - mx)
                    den = jnp.sum(p_c, axis=-1, keepdims=True)
                    acc = _dot(p_c.astype(BF16), v_ref[r, :].astype(BF16))
                    if pst is not None:
                        p_p = jnp.exp(s_p - mx)
                        den = den + jnp.sum(p_p, axis=-1, keepdims=True)
                        acc = acc + _dot(p_p.astype(BF16), v_ref[rp, :].astype(BF16))
                    o_ref[r, :] = acc / den
                    lse_ref[r, :] = jnp.broadcast_to(mx + jnp.log(den), (HEAD, HEAD))

    idx = lambda gi, h: gi * 4 + h
    out = pl.BlockSpec((L, HEAD), lambda gi, h: (0, gi * 4 + h))
    return _pcall(body, name="attn_core_fwd", grid=(3, 4), in_specs=_qkv_specs(idx), out_specs=[out, out],
                  out_shape=[_sds((L, QKV)), _sds((L, QKV))], vmem_mb=40)(proj, proj, proj)


def _attn_mix_fwd(o, lse, proj):
    tl = 512

    def body(o_ref, l_ref, z_ref, y_ref):
        for h in range(4):
            c = [pl.ds((g * 4 + h) * HEAD, HEAD) for g in range(3)]
            ls = [l_ref[:, c[g]] for g in range(3)]
            m = jnp.maximum(jnp.maximum(ls[0], ls[1]), ls[2])
            e = [jnp.exp(ls[g] - m) for g in range(3)]
            y = (e[0] * o_ref[:, c[0]] + e[1] * o_ref[:, c[1]] + e[2] * o_ref[:, c[2]]) / (e[0] + e[1] + e[2])
            z = z_ref[:, pl.ds(h * HEAD, HEAD)]
            y_ref[:, pl.ds(h * HEAD, HEAD)] = y * (z * _sigmoid(z))

    wide = pl.BlockSpec((tl, QKV), lambda i: (i, 0))
    return _pcall(body, name="attn_mix_fwd", grid=(L // tl,),
                  in_specs=[wide, wide, pl.BlockSpec((tl, AW), lambda i: (i, COL_ZA // 4))],
                  out_specs=pl.BlockSpec((tl, AW), lambda i: (i, 0)), out_shape=_sds((L, AW)),
                  vmem_mb=40)(o, lse, proj)


def _attn_mix_bwd(dya, o, lse, proj, dproj):
    tl = 512

    def body(dya_ref, o_ref, l_ref, z_ref, _, do_ref, c_ref, dz_ref):
        for h in range(4):
            c = [pl.ds((g * 4 + h) * HEAD, HEAD) for g in range(3)]
            hs = pl.ds(h * HEAD, HEAD)
            ls = [l_ref[:, c[g]] for g in range(3)]
            m = jnp.maximum(jnp.maximum(ls[0], ls[1]), ls[2])
            e = [jnp.exp(ls[g] - m) for g in range(3)]
            den = e[0] + e[1] + e[2]
            al = [e[g] / den for g in range(3)]
            y = al[0] * o_ref[:, c[0]] + al[1] * o_ref[:, c[1]] + al[2] * o_ref[:, c[2]]
            z = z_ref[:, hs]
            sz = _sigmoid(z)
            dya = dya_ref[:, hs]
            dz_ref[:, hs] = dya * y * (sz * (1.0 + z * (1.0 - sz)))
            dy = dya * (z * sz)
            tot = jnp.sum(dy * y, axis=-1, keepdims=True)
            for g in range(3):
                do_ref[:, c[g]] = al[g] * dy
                c_ref[:, c[g]] = -(al[g] * tot)

    wide = pl.BlockSpec((tl, QKV), lambda i: (i, 0))
    nar = pl.BlockSpec((tl, AW), lambda i: (i, 0))
    za = pl.BlockSpec((tl, AW), lambda i: (i, COL_ZA // 4))
    return _pcall(body, name="attn_mix_bwd", grid=(L // tl,),
                  in_specs=[nar, wide, wide, za, pl.BlockSpec(memory_space=pl.ANY)],
                  out_specs=[wide, wide, za], out_shape=[_sds((L, QKV)), _sds((L, QKV)), _sds((L, NCOL))],
                  aliases={4: 2}, vmem_mb=48)(dya, o, lse, proj, dproj)


def _attn_core_bwd(proj, do, lse, cc, dproj):
    def body(q_ref, k_ref, v_ref, do_ref, lse_ref, c_ref, _, dp_ref, dq_ref, dk_ref, dv_ref, sems):
        gi = pl.program_id(0)
        head = gi * 4 + pl.program_id(1)
        m_cur, m_prev = _band_masks()
        dk_ref[...] = jnp.zeros((L, HEAD), F32)
        dv_ref[...] = jnp.zeros((L, HEAD), F32)
        for g, dil in enumerate(DILATIONS):
            @pl.when(gi == g)
            def _(dil=dil):
                for st, pst in _attn_blocks(dil):
                    r = _rows(st, dil)
                    q = q_ref[r, :].astype(BF16)
                    do_b = do_ref[r, :].astype(BF16)
                    lse_b = lse_ref[r, :]
                    c_b = c_ref[r, :]
                    dq = None
                    for rr, mask in ((r, m_cur),) + (((_rows(pst, dil), m_prev),) if pst is not None else ()):
                        kb = k_ref[rr, :].astype(BF16)
                        s = _dot_nt(q, kb) * SCALE
                        p = jnp.where(mask, jnp.exp(s - lse_b), 0.0)
                        dp = _dot_nt(do_b, v_ref[rr, :].astype(BF16))
                        ds = (p * (dp + c_b) * SCALE).astype(BF16)
                        part = _dot(ds, kb)
                        dq = part if dq is None else dq + part
                        dk_ref[rr, :] += _dot_tn(ds, q)
                        dv_ref[rr, :] += _dot_tn(p.astype(BF16), do_b)
                    dq_ref[r, :] = dq

        copies = [pltpu.make_async_copy(src, dp_ref.at[:, pl.ds(pl.multiple_of((off + head) * HEAD, HEAD), HEAD)],
                                        sems.at[n])
                  for n, (src, off) in enumerate(((dq_ref, COL_Q), (dk_ref, COL_K), (dv_ref, COL_V)))]
        for cp in copies:
            cp.start()
        for cp in copies:
            cp.wait()

    idx = lambda gi, h: gi * 4 + h
    blk = pl.BlockSpec((L, HEAD), lambda gi, h: (0, gi * 4 + h))
    hbm = pl.BlockSpec(memory_space=pl.ANY)
    return _pcall(body, name="attn_core_bwd", grid=(3, 4), in_specs=_qkv_specs(idx) + [blk, blk, blk, hbm],
                  out_specs=hbm, out_shape=_sds((L, NCOL)),
                  scratch=[pltpu.VMEM((L, HEAD), F32)] * 3 + [pltpu.SemaphoreType.DMA((3,))],
                  aliases={6: 0}, vmem_mb=48)(proj, proj, proj, do, lse, cc, dproj)


def _merge_fwd_math(ys_b, ya_b, gs, ga, wbs_ref, wba_ref, wout_ref):
    bs = jnp.concatenate([_dot(ys_b, wbs_ref[s]) for s in range(NCHIP)], axis=1)
    ba = jnp.concatenate([_dot(ya_b, wba_ref[s]) for s in range(NCHIP)], axis=1)
    sgs, sga = _sigmoid(gs), _sigmoid(ga)
    merged = sgs * bs + sga * ba
    out = _dot(merged.astype(BF16), wout_ref[...])
    inv = lax.rsqrt(jnp.mean(out * out, axis=-1, keepdims=True) + RMS_EPS)
    return bs, ba, sgs, sga, merged, out, inv


def _merge_specs(tl):
    row = lambda w, c: pl.BlockSpec((tl, w), lambda i: (i, c))
    return [row(SSM, 0), row(AW, 0), row(D, COL_GS // 8), row(D, COL_GA // 8),
            pl.BlockSpec((NCHIP, SSM, D // NCHIP), lambda i: (0, 0, 0)),
            pl.BlockSpec((NCHIP, AW, D // NCHIP), lambda i: (0, 0, 0)),
            pl.BlockSpec((D, D), lambda i: (0, 0)),
            pl.BlockSpec((1, D), lambda i: (0, 0))]


def _merge_out_fwd(ys, ya, proj, wbs4, wba4, wout, g2, x):
    tl = 256

    def body(ys_ref, ya_ref, gs_ref, ga_ref, wbs_ref, wba_ref, wout_ref, g2_ref, x_ref, o_ref):
        *_, out, inv = _merge_fwd_math(ys_ref[...].astype(BF16), ya_ref[...].astype(BF16), gs_ref[...],
                                       ga_ref[...], wbs_ref, wba_ref, wout_ref)
        o_ref[...] = x_ref[...] + out * inv * g2_ref[...]

    row = pl.BlockSpec((tl, D), lambda i: (i, 0))
    return _pcall(body, name="merge_out_fwd", grid=(L // tl,), in_specs=_merge_specs(tl) + [row],
                  out_specs=row, out_shape=_sds((L, D)), vmem_mb=48)(ys, ya, proj, proj, wbs4, wba4, wout, g2, x)


def _merge_out_bwd(dxn, ys, ya, proj, wbs4, wba4, wout, g2, dep=None):
    tl = 256
    ni = L // tl
    cw = D // NCHIP
    deps, dep_specs = _token_operand(dep)

    def body(dxn_ref, ys_ref, ya_ref, gs_ref, ga_ref, wbs_ref, wba_ref, wout_ref, g2_ref, *rest):
        (dys_ref, dya_ref, dgate_ref, dwbs_ref, dwba_ref, dwout_ref, dg2_ref,
         abs_ref, aba_ref, aout_ref) = rest[len(deps):]
        i = pl.program_id(0)
        ys_b, ya_b = ys_ref[...].astype(BF16), ya_ref[...].astype(BF16)
        bs, ba, sgs, sga, merged, out, inv = _merge_fwd_math(ys_b, ya_b, gs_ref[...], ga_ref[...],
                                                             wbs_ref, wba_ref, wout_ref)
        nrm = out * inv
        dxn = dxn_ref[...]
        dg2 = jnp.sum(dxn * nrm, axis=0, keepdims=True)
        dn = dxn * g2_ref[...]
        dout = (inv * (dn - nrm * jnp.mean(dn * nrm, axis=-1, keepdims=True))).astype(BF16)
        dwout = _dot_tn(merged.astype(BF16), dout)
        dm = _dot_nt(dout, wout_ref[...])
        dbs, dba = dm * sgs, dm * sga
        dgate_ref[:, :D] = dm * bs * (sgs * (1.0 - sgs))
        dgate_ref[:, D:] = dm * ba * (sga * (1.0 - sga))
        dbs_b, dba_b = dbs.astype(BF16), dba.astype(BF16)
        dys = None
        dya = None
        first = i == 0
        for s in range(NCHIP):
            cs = slice(s * cw, (s + 1) * cw)
            p_s = _dot_nt(dbs_b[:, cs], wbs_ref[s])
            p_a = _dot_nt(dba_b[:, cs], wba_ref[s])
            dys = p_s if dys is None else dys + p_s
            dya = p_a if dya is None else dya + p_a
            w_s = _dot_tn(ys_b, dbs_b[:, cs])
            w_a = _dot_tn(ya_b, dba_b[:, cs])

            @pl.when(first)
            def _(s=s, w_s=w_s, w_a=w_a):
                abs_ref[s] = w_s
                aba_ref[s] = w_a

            @pl.when(jnp.logical_not(first))
            def _(s=s, w_s=w_s, w_a=w_a):
                abs_ref[s] += w_s
                aba_ref[s] += w_a

        dys_ref[...] = dys
        dya_ref[...] = dya

        @pl.when(first)
        def _():
            aout_ref[...] = dwout
            dg2_ref[...] = dg2

        @pl.when(jnp.logical_not(first))
        def _():
            aout_ref[...] += dwout
            dg2_ref[...] += dg2

        @pl.when(i == ni - 1)
        def _():
            dwbs_ref[...] = abs_ref[...].astype(BF16)
            dwba_ref[...] = aba_ref[...].astype(BF16)
            dwout_ref[...] = aout_ref[...].astype(BF16)

    row = lambda w: pl.BlockSpec((tl, w), lambda i: (i, 0))
    w4 = pl.BlockSpec((NCHIP, SSM, cw), lambda i: (0, 0, 0))
    sq = pl.BlockSpec((D, D), lambda i: (0, 0))
    vec = pl.BlockSpec((1, D), lambda i: (0, 0))
    return _pcall(body, name="merge_out_bwd", grid=(ni,), in_specs=[row(D)] + _merge_specs(tl) + dep_specs,
                  out_specs=[row(SSM), row(AW), pl.BlockSpec((tl, 2 * D), lambda i: (i, COL_GS // 16)), w4, w4, sq,
                             vec],
                  out_shape=[_sds((L, SSM)), _sds((L, AW)), _sds((L, NCOL)),
                             _sds((NCHIP, SSM, cw), BF16), _sds((NCHIP, AW, cw), BF16), _sds((D, D), BF16),
                             _sds((1, D))],
                  scratch=[pltpu.VMEM((NCHIP, SSM, cw), F32), pltpu.VMEM((NCHIP, AW, cw), F32),
                           pltpu.VMEM((D, D), F32)],
                  vmem_mb=56)(dxn, ys, ya, proj, proj, wbs4, wba4, wout, g2, *deps)


def _loss_head(y, target):
    tl = 512

    def body(y_ref, t_ref, loss_ref, dy_ref):
        i = pl.program_id(0)
        err = y_ref[...] - t_ref[...]
        dy_ref[...] = err / D
        part = 0.5 * jnp.sum(jnp.mean(err * err, axis=-1, keepdims=True), axis=0, keepdims=True)
        part = jnp.broadcast_to(part, (8, 128))

        @pl.when(i == 0)
        def _():
            loss_ref[...] = part

        @pl.when(i > 0)
        def _():
            loss_ref[...] += part

    row = pl.BlockSpec((tl, D), lambda i: (i, 0))
    return _pcall(body, name="loss_head", grid=(L // tl,), in_specs=[row, row],
                  out_specs=[pl.BlockSpec((8, 128), lambda i: (0, 0)), row],
                  out_shape=[_sds((8, 128)), _sds((L, D))], vmem_mb=40)(y, target)


def _adamw_math(w, g, m, v):
    m = ADAM_B1 * m + (1.0 - ADAM_B1) * g
    v = ADAM_B2 * v + (1.0 - ADAM_B2) * (g * g)
    m_hat = m / (1.0 - ADAM_B1 ** ADAM_STEP)
    v_hat = v / (1.0 - ADAM_B2 ** ADAM_STEP)
    delta = -ADAM_LR * (m_hat / (jnp.sqrt(v_hat) + ADAM_EPS) + ADAM_WD * w)
    return delta, m, v


def _adamw_big(layer, w, m, v, own, sib, prev):
    _, r, c = w.shape
    tr = min(r, 128)

    def body(w_ref, m_ref, v_ref, own_ref, sib_ref, *rest):
        g_ref, d_ref, nm_ref, nv_ref = rest[-4:]
        a = own_ref[0].astype(F32)
        b = sib_ref[0].astype(F32)
        for s in range(1, NCHIP):
            a = a + own_ref[s].astype(F32)
            b = b + sib_ref[s].astype(F32)
        g = a + b
        delta, nm, nv = _adamw_math(w_ref[...], g, m_ref[...], v_ref[...])
        g_ref[...] = g
        d_ref[...] = delta
        nm_ref[...] = nm
        nv_ref[...] = nv

    lay = pl.BlockSpec((None, tr, c), lambda i: (layer, i, 0))
    slots = pl.BlockSpec((NCHIP, tr, c), lambda i: (0, i, 0))
    ins = [w, m, v, own, sib]
    in_specs = [lay, lay, lay, slots, slots]
    aliases = {}
    if prev is not None:
        ins += list(prev)
        in_specs += [pl.BlockSpec(memory_space=pl.ANY)] * 4
        aliases = {5 + k: k for k in range(4)}
    return _pcall(body, name="adamw_big", grid=(r // tr,), in_specs=in_specs, out_specs=[lay] * 4,
                  out_shape=[_sds(w.shape)] * 4, aliases=aliases, vmem_mb=48)(*ins)


SMALL_TILE = 512


def _adamw_small(w, parts, m, v):
    r = w.shape[0]

    def body(w_ref, p_ref, m_ref, v_ref, g_ref, d_ref, nm_ref, nv_ref):
        g = p_ref[0]
        for dev in range(1, 8):
            g = g + p_ref[dev]
        delta, nm, nv = _adamw_math(w_ref[...], g, m_ref[...], v_ref[...])
        g_ref[...] = g
        d_ref[...] = delta
        nm_ref[...] = nm
        nv_ref[...] = nv

    row = pl.BlockSpec((SMALL_TILE, 128), lambda i: (i, 0))
    return _pcall(body, name="adamw_small", grid=(r // SMALL_TILE,),
                  in_specs=[row, pl.BlockSpec((8, SMALL_TILE, 128), lambda i: (0, i, 0)), row, row],
                  out_specs=[row] * 4, out_shape=[_sds((r, 128))] * 4, vmem_mb=40)(w, parts, m, v)


def _place():
    x, y, c = lax.axis_index("x"), lax.axis_index("y"), lax.axis_index("c")
    return x, y, c, 2 * x + y


def _chip_peer(x, y, j):
    return (1 - x if j & 2 else x), (1 - y if j & 1 else y)


_HBM = pl.BlockSpec(memory_space=pltpu.HBM)
_SEM = pl.BlockSpec(memory_space=pltpu.SEMAPHORE)
_EFFECT = pltpu.SideEffectType.DATAFLOW_SIDE_EFFECTING


def _in_hbm(a):
    return pltpu.with_memory_space_constraint(a, pltpu.HBM)


def _copies_start(name, plan, ncopy, srcs, lands, dep=None):
    n = len(srcs) + len(lands)
    deps, dep_specs = _token_operand(dep)

    def body(*refs):
        send_sems, recv_sems = refs[n + len(deps)], refs[n + len(deps) + 1]
        token = refs[-1]
        for k, (src, dst, dev) in enumerate(plan(refs[:len(srcs)], refs[len(srcs):n])):
            pltpu.make_async_remote_copy(src_ref=src, dst_ref=dst, send_sem=send_sems.at[k],
                                         recv_sem=recv_sems.at[k], device_id=dev, device_id_type=MESH).start()
        token[...] = jnp.zeros_like(token)

    bufs = list(srcs) + list(lands)
    outs = pl.pallas_call(
        body, name=name,
        out_shape=(pltpu.SemaphoreType.DMA((ncopy,)), pltpu.SemaphoreType.DMA((ncopy,)),
                   *[pltpu.HBM(a.shape, a.dtype) for a in bufs], _sds((8, 128))),
        in_specs=[_HBM] * n + dep_specs,
        out_specs=(_SEM, _SEM, *[_HBM] * n, pl.BlockSpec(memory_space=pltpu.VMEM)),
        input_output_aliases={i: 2 + i for i in range(n)},
        compiler_params=pltpu.CompilerParams(has_side_effects=_EFFECT),
    )(*[_in_hbm(a) for a in bufs], *deps)
    return outs[0], outs[1], list(outs[2:2 + len(srcs)]), list(outs[2 + len(srcs):2 + n]), outs[-1]


def _copies_wait(name, plan, send_sems, recv_sems, srcs, lands, after):
    n = len(srcs) + len(lands)
    after = list(after)

    def body(*refs):
        s_sems, r_sems = refs[n], refs[n + 1]
        for k, (src, dst, dev) in enumerate(plan(refs[:len(srcs)], refs[len(srcs):n])):
            cp = pltpu.make_async_remote_copy(src_ref=src, dst_ref=dst, send_sem=s_sems.at[k],
                                              recv_sem=r_sems.at[k], device_id=dev, device_id_type=MESH)
            cp.wait_send()
            cp.wait_recv()

    bufs = list(srcs) + list(lands)
    outs = pl.pallas_call(
        body, name=name,
        out_shape=tuple(pltpu.HBM(a.shape, a.dtype) for a in bufs),
        in_specs=[_HBM] * n + [_SEM, _SEM] + [pl.BlockSpec(memory_space=pl.ANY)] * len(after),
        out_specs=tuple([_HBM] * n),
        input_output_aliases={i: i for i in range(n)},
        compiler_params=pltpu.CompilerParams(has_side_effects=_EFFECT),
    )(*bufs, send_sems, recv_sems, *after)
    return list(outs[:len(srcs)]), list(outs[len(srcs):])


def _with_own_slot(block, slot, nslots=NCHIP):
    land = lax.empty((nslots,) + block.shape, block.dtype)
    return lax.dynamic_update_slice(land, block[None], (slot,) + (0,) * block.ndim)


def _gather_plan(srcs, lands):
    x, y, c, s = _place()
    return [(src, land.at[s], (*_chip_peer(x, y, j), c))
            for src, land in zip(srcs, lands) for j in range(1, NCHIP)]


def _exchange_plan(srcs, lands):
    x, y, c, s = _place()
    return [(src.at[s ^ j], land.at[s], (*_chip_peer(x, y, j), c))
            for src, land in zip(srcs, lands) for j in range(1, NCHIP)]


def _sibling_plan(srcs, lands):
    x, y, c, _ = _place()
    return [(src, land, (x, y, 1 - c)) for src, land in zip(srcs, lands)]


def _everyone_plan(srcs, lands):
    x, y, c, _ = _place()
    me = 4 * x + 2 * y + c
    return [(srcs[0], lands[0].at[me], (*_chip_peer(x, y, j >> 1), 1 - c if j & 1 else c)) for j in range(1, 8)]


def _flatten_small(parts):
    flat = jnp.concatenate([p.reshape(-1) for p in parts])
    n = flat.shape[0]
    rows = -(-n // (128 * SMALL_TILE)) * SMALL_TILE
    return jnp.pad(flat, (0, rows * 128 - n)).reshape(rows, 128)


def _unflatten_small(buf, like):
    flat = buf.reshape(-1)
    out, at = [], 0
    for p in like:
        out.append(flat[at:at + p.size].reshape(p.shape))
        at += p.size
    return out


def kernel(x, pre_norm_g, w_in, lambda_re, lambda_im, log_dt, b_re, b_im, c_re, c_im, d_skip, w_glu, b_glu, w_branch_s, w_branch_a, w_out, post_norm_g, loss_target, m_pre_norm_g, m_w_in, m_lambda_re, m_lambda_im, m_log_dt, m_b_re, m_b_im, m_c_re, m_c_im, m_d_skip, m_w_glu, m_b_glu, m_w_branch_s, m_w_branch_a, m_w_out, m_post_norm_g, v_pre_norm_g, v_w_in, v_lambda_re, v_lambda_im, v_log_dt, v_b_re, v_b_im, v_c_re, v_c_im, v_d_skip, v_w_glu, v_b_glu, v_w_branch_s, v_w_branch_a, v_w_out, v_post_norm_g):
    nb = DEPTH * NCB
    lre_c = lambda_re.reshape(nb, 1, HW)
    lim_c = lambda_im.reshape(nb, 1, HW)
    ldt_c = jnp.broadcast_to(log_dt[:, :, None], (DEPTH, 32, 64)).reshape(nb, 1, HW)
    b_rows = lambda t: t.reshape(DEPTH, NCB, 8, 64, 16).transpose(0, 1, 4, 2, 3).reshape(nb, 16, HW)
    c_rows = lambda t: t.reshape(DEPTH, NCB, 8, 16, 64).transpose(0, 1, 3, 2, 4).reshape(nb, 16, HW)
    bre_c, bim_c = b_rows(b_re), b_rows(b_im)
    ar, ai, wb, wct = _s5_prep(lre_c, lim_c, ldt_c, bre_c, bim_c, c_rows(c_re), c_rows(c_im))
    ar4 = ar.reshape(DEPTH, NCB, 1, HW)
    ai4 = ai.reshape(DEPTH, NCB, 1, HW)
    wb = wb.reshape(DEPTH, NCB, 128, SW)
    wct = wct.reshape(DEPTH, NCB, 128, SW)

    chip = 2 * lax.axis_index("x") + lax.axis_index("y")
    device = 2 * chip + lax.axis_index("c")

    def gather_start(l, dep):
        srcs = [w_in[l].astype(BF16), w_glu[l].astype(BF16), w_branch_s[l].astype(BF16),
                w_branch_a[l].astype(BF16), w_out[l].astype(BF16)]
        return _copies_start(f"gather_start_{l}", _gather_plan, 15, srcs, [_with_own_slot(a, chip) for a in srcs],
                             dep)

    xs = [x[0]]
    saved = []
    flight = gather_start(0, None)
    for l in range(DEPTH):
        send, recv, srcs, lands, _ = flight
        _, (w4, wglu4, wbs4, wba4, wout4) = _copies_wait(f"gather_wait_{l}", _gather_plan, send, recv, srcs, lands,
                                                         [xs[l]])
        started = None
        if l + 1 < DEPTH:
            flight = gather_start(l + 1, wglu4)
            started = flight[4]
        wglu = wglu4.reshape(SSM, SSM)
        wout = wout4.reshape(D, D)
        wb4, wc4 = wb[l], wct[l]
        g1 = pre_norm_g[l].reshape(1, D)
        g2 = post_norm_g[l].reshape(1, D)
        dsk = d_skip[l].reshape(1, SSM)
        bgl = b_glu[l].reshape(1, SSM)
        proj = _rms_proj_fwd(xs[l], g1, w4, started)
        states, yraw = _s5_scan_fwd(proj, wb4, wc4, ar4[l], ai4[l])
        ys = _s5_tail_fwd(yraw, proj, dsk, wglu, bgl)
        o, lse = _attn_core_fwd(proj)
        ya = _attn_mix_fwd(o, lse, proj)
        xs.append(_merge_out_fwd(ys, ya, proj, wbs4, wba4, wout, g2, xs[l]))
        saved.append((w4, wglu, wbs4, wba4, wout, wb4, wc4, g1, g2, dsk, bgl, proj, states, yraw, ys, ya, o, lse))

    loss_part, dx = _loss_head(xs[DEPTH], loss_target[0])
    loss = lax.psum(loss_part[0, 0], ("x", "y", "c"))

    big_w = (w_in, w_glu.reshape(DEPTH, 128, SSM), w_branch_s, w_branch_a, w_out)
    big_m = (m_w_in, m_w_glu, m_w_branch_s, m_w_branch_a, m_w_out)
    big_v = (v_w_in, v_w_glu, v_w_branch_s, v_w_branch_a, v_w_out)
    big_out = [None] * 5
    small = {k: [None] * DEPTH for k in ("g1", "da", "dbb", "dcc", "dsk", "bgl", "g2")}
    ici = [None] * DEPTH
    d2d = [None] * DEPTH

    def exchange_start(l, parts):
        lands = [_with_own_slot(lax.dynamic_index_in_dim(p, chip, 0, keepdims=False), chip) for p in parts]
        ici[l] = _copies_start(f"exchange_start_{l}", _exchange_plan, 15, parts, lands)

    def handoff_start(l, after):
        send, recv, srcs, lands, _ = ici[l]
        _, own = _copies_wait(f"exchange_wait_{l}", _exchange_plan, send, recv, srcs, lands, after)
        d2d[l] = _copies_start(f"handoff_start_{l}", _sibling_plan, 5, own,
                               [lax.empty(a.shape, a.dtype) for a in own])

    def update(l, after):
        send, recv, own, lands, _ = d2d[l]
        own, sib = _copies_wait(f"handoff_wait_{l}", _sibling_plan, send, recv, own, lands, after)
        for k in range(5):
            big_out[k] = _adamw_big(l, big_w[k], big_m[k], big_v[k], own[k], sib[k], big_out[k])

    for l in reversed(range(DEPTH)):
        w4, wglu, wbs4, wba4, wout, wb4, wc4, g1, g2, dsk, bgl, proj, states, yraw, ys, ya, o, lse = saved[l]
        started = None
        if l + 1 < DEPTH:
            started = ici[l + 1][4] + d2d[l + 2][4] if l + 2 < DEPTH else ici[l + 1][4]
        dys, dya, dproj, dwbs, dwba, dwout, dg2 = _merge_out_bwd(dx, ys, ya, proj, wbs4, wba4, wout, g2, started)
        dyraw, du_skip, dproj, dwglu, dbgl, ddsk = _s5_tail_bwd(dys, yraw, proj, dsk, wglu, bgl, dproj)
        dproj, dbb, dcc, da = _s5_scan_bwd(dyraw, states, proj, du_skip, wb4, wc4, ar4[l], ai4[l], dproj)
        do, cterm, dproj = _attn_mix_bwd(dya, o, lse, proj, dproj)
        dproj = _attn_core_bwd(proj, do, lse, cterm, dproj)
        dh, dwin = _proj_bwd(dproj, xs[l], g1, w4)
        parts = [dwin, dwglu.reshape(NCHIP, 128, SSM), dwbs, dwba, dwout.reshape(NCHIP, D // NCHIP, D)]
        exchange_start(l, parts)
        dx, dg1 = _rms_bwd(dh, xs[l], g1, dx, ici[l][4])
        if l + 1 < DEPTH:
            handoff_start(l + 1, [ici[l][4]])
        small["g1"][l], small["g2"][l], small["dsk"][l], small["bgl"][l] = dg1, dg2, ddsk, dbgl
        small["da"][l], small["dbb"][l], small["dcc"][l] = da, dbb, dcc

    dlre, dlim, dldt, dbre_c, dbim_c = _s5_prep_bwd(
        lre_c, lim_c, ldt_c, bre_c, bim_c, jnp.stack(small["da"]).reshape(nb, 1, SW),
        jnp.stack(small["dbb"]).reshape(nb, 16, SW))
    dcc = jnp.stack(small["dcc"]).reshape(nb, 16, SW)
    b_back = lambda t: t.reshape(DEPTH, NCB, 16, 8, 64).transpose(0, 1, 3, 4, 2).reshape(DEPTH, 32, 64, 16)
    c_back = lambda t: t.reshape(DEPTH, NCB, 16, 8, 64).transpose(0, 1, 3, 2, 4).reshape(DEPTH, 32, 16, 64)
    small_w = [pre_norm_g, lambda_re, lambda_im, log_dt, b_re, b_im, c_re, c_im, d_skip, b_glu, post_norm_g]
    small_m = [m_pre_norm_g, m_lambda_re, m_lambda_im, m_log_dt, m_b_re, m_b_im, m_c_re, m_c_im, m_d_skip,
               m_b_glu, m_post_norm_g]
    small_v = [v_pre_norm_g, v_lambda_re, v_lambda_im, v_log_dt, v_b_re, v_b_im, v_c_re, v_c_im, v_d_skip,
               v_b_glu, v_post_norm_g]
    small_g = [jnp.stack(small["g1"]).reshape(DEPTH, D), dlre.reshape(DEPTH, 32, 64), dlim.reshape(DEPTH, 32, 64),
               dldt.reshape(DEPTH, 32, 64).sum(-1), b_back(dbre_c), b_back(dbim_c), c_back(dcc[:, :, :HW]),
               c_back(dcc[:, :, HW:]),
               jnp.stack(small["dsk"]).reshape(DEPTH, SSM), jnp.stack(small["bgl"]).reshape(DEPTH, SSM),
               jnp.stack(small["g2"]).reshape(DEPTH, D)]
    part = _flatten_small(small_g)
    send, recv, srcs, lands, token = _copies_start("small_start", _everyone_plan, 7, [part],
                                                   [_with_own_slot(part, device, 8)])
    update(DEPTH - 1, [token, ici[0][4]])
    for l in range(DEPTH - 2, 0, -1):
        update(l, [out[0] for out in big_out])
    handoff_start(0, [out[0] for out in big_out])
    update(0, [d2d[0][4]])
    _, (parts8,) = _copies_wait("small_wait", _everyone_plan, send, recv, srcs, lands, [out[0] for out in big_out])
    g_flat, d_flat, nm_flat, nv_flat = _adamw_small(_flatten_small(small_w), parts8, _flatten_small(small_m),
                                                    _flatten_small(small_v))
    sg = _unflatten_small(g_flat, small_w)
    sd = _unflatten_small(d_flat, small_w)
    snm = _unflatten_small(nm_flat, small_w)
    snv = _unflatten_small(nv_flat, small_w)

    def ordered(sm, which):
        bg = [big_out[k][which] for k in range(5)]
        bg[1] = bg[1].reshape(DEPTH, 128, SSM)
        return [sm[0], bg[0], sm[1], sm[2], sm[3], sm[4], sm[5], sm[6], sm[7], sm[8], bg[1], sm[9], bg[2], bg[3],
                bg[4], sm[10]]

    return (loss, dx[None], *ordered(sg, 0), *ordered(sd, 1), *ordered(snm, 2), *ordered(snv, 3))
```

```python
import math

import jax
import jax.numpy as jnp
from jax import lax
from jax.experimental import pallas as pl
from jax.experimental.pallas import tpu as pltpu

F32 = jnp.float32
BF16 = jnp.bfloat16
MESH = pl.DeviceIdType.MESH

DEPTH = 4
L = 2048
D = 1024
NCOL = 8192
SSM = 512
AW = 512
QKV = 1536
RMS_EPS = 1e-6
NCHIP = 4

COL_U, COL_ZS, COL_Q, COL_K, COL_V, COL_ZA, COL_GS, COL_GA = 0, 4, 8, 20, 32, 44, 48, 56

NSEG = 8
SEG = L // NSEG
NCB = 4
HW = 512
SW = 2 * HW

HEAD = 128
DILATIONS = (1, 4, 16)
SCALE = HEAD ** -0.5
NEG = -1e30

ADAM_LR, ADAM_B1, ADAM_B2, ADAM_EPS, ADAM_WD, ADAM_STEP = 0.001, 0.9, 0.999, 1e-08, 0.01, 10


def _sds(shape, dtype=F32):
    return jax.ShapeDtypeStruct(shape, dtype)


def _pcall(body, *, name, out_shape, grid=None, in_specs=None, out_specs=None, scratch=(), vmem_mb=None,
           aliases=None):
    params = {}
    if vmem_mb is not None:
        params["vmem_limit_bytes"] = vmem_mb << 20
    kw = {}
    if grid is not None:
        kw["grid"] = grid
    if in_specs is not None:
        kw["in_specs"] = in_specs
    if out_specs is not None:
        kw["out_specs"] = out_specs
    return pl.pallas_call(body, name=name, out_shape=out_shape, scratch_shapes=list(scratch),
                          compiler_params=pltpu.CompilerParams(**params),
                          input_output_aliases=aliases or {}, **kw)


def _dot(a, b):
    return jnp.dot(a, b, preferred_element_type=F32)


def _dot_nt(a, b):
    return lax.dot_general(a, b, (((1,), (1,)), ((), ())), preferred_element_type=F32)


def _dot_tn(a, b):
    return lax.dot_general(a, b, (((0,), (0,)), ((), ())), preferred_element_type=F32)


def _sigmoid(x):
    return jax.nn.sigmoid(x)


_GELU_K = math.sqrt(2.0 / math.pi)


def _gelu(x):
    return 0.5 * x * (1.0 + jnp.tanh(_GELU_K * (x + 0.044715 * (x * x * x))))


def _gelu_grad(x):
    t = jnp.tanh(_GELU_K * (x + 0.044715 * (x * x * x)))
    return 0.5 * (1.0 + t) + 0.5 * x * (1.0 - t * t) * (_GELU_K * (1.0 + 3.0 * 0.044715 * (x * x)))


def _token_operand(dep):
    if dep is None:
        return [], []
    return [dep], [pl.BlockSpec(memory_space=pl.ANY)]


def _rms_proj_fwd(x, g, w4, dep=None):
    tl, tn = 512, 1024
    deps, dep_specs = _token_operand(dep)

    def body(x_ref, g_ref, w_ref, *rest):
        o_ref = rest[-1]
        xx = x_ref[...]
        inv = lax.rsqrt(jnp.mean(xx * xx, axis=-1, keepdims=True) + RMS_EPS)
        h = (xx * inv * g_ref[...]).astype(BF16)
        o_ref[...] = _dot(h, w_ref[...])

    return _pcall(
        body, name="rms_proj_fwd", grid=(NCOL // tn, L // tl),
        in_specs=[pl.BlockSpec((tl, D), lambda j, i: (i, 0)),
                  pl.BlockSpec((1, D), lambda j, i: (0, 0)),
                  pl.BlockSpec((None, D, tn), lambda j, i: (lax.div(j, 2), 0, lax.rem(j, 2)))] + dep_specs,
        out_specs=pl.BlockSpec((tl, tn), lambda j, i: (i, j)),
        out_shape=_sds((L, NCOL)), vmem_mb=40)(x, g, w4, *deps)


def _proj_bwd(dproj, x, g, w4):
    tl, tn = 512, 1024
    ni = L // tl

    def body(dp_ref, x_ref, g_ref, w_ref, dh_ref, dw_ref, acc_ref):
        j, i = pl.program_id(0), pl.program_id(1)
        xx = x_ref[...]
        inv = lax.rsqrt(jnp.mean(xx * xx, axis=-1, keepdims=True) + RMS_EPS)
        h = (xx * inv * g_ref[...]).astype(BF16)
        dp = dp_ref[...].astype(BF16)
        part = _dot_tn(h, dp)

        @pl.when(i == 0)
        def _():
            acc_ref[...] = part

        @pl.when(i > 0)
        def _():
            acc_ref[...] += part

        @pl.when(i == ni - 1)
        def _():
            dw_ref[...] = acc_ref[...].astype(BF16)

        rows = pl.ds(pl.multiple_of(i * tl, tl), tl)
        dh = _dot_nt(dp, w_ref[...])

        @pl.when(j == 0)
        def _():
            dh_ref[rows, :] = dh

        @pl.when(j > 0)
        def _():
            dh_ref[rows, :] += dh

    return _pcall(
        body, name="proj_bwd", grid=(NCOL // tn, ni),
        in_specs=[pl.BlockSpec((tl, tn), lambda j, i: (i, j)),
                  pl.BlockSpec((tl, D), lambda j, i: (i, 0)),
                  pl.BlockSpec((1, D), lambda j, i: (0, 0)),
                  pl.BlockSpec((None, D, tn), lambda j, i: (lax.div(j, 2), 0, lax.rem(j, 2)))],
        out_specs=[pl.BlockSpec((L, D), lambda j, i: (0, 0)),
                   pl.BlockSpec((None, D, tn), lambda j, i: (lax.div(j, 2), 0, lax.rem(j, 2)))],
        out_shape=[_sds((L, D)), _sds((NCHIP, D, NCOL // NCHIP), BF16)],
        scratch=[pltpu.VMEM((D, tn), F32)], vmem_mb=56)(dproj, x, g, w4)


def _rms_bwd(dh, x, g, dxn, dep=None):
    tl = 512
    deps, dep_specs = _token_operand(dep)

    def body(dh_ref, x_ref, g_ref, dxn_ref, *rest):
        dx_ref, dg_ref = rest[len(deps):]
        i = pl.program_id(0)
        xx = x_ref[...]
        inv = lax.rsqrt(jnp.mean(xx * xx, axis=-1, keepdims=True) + RMS_EPS)
        nrm = xx * inv
        dh_v = dh_ref[...]
        dn = dh_v * g_ref[...]
        dx_ref[...] = dxn_ref[...] + inv * (dn - nrm * jnp.mean(dn * nrm, axis=-1, keepdims=True))
        part = jnp.sum(dh_v * nrm, axis=0, keepdims=True)

        @pl.when(i == 0)
        def _():
            dg_ref[...] = part

        @pl.when(i > 0)
        def _():
            dg_ref[...] += part

    row = pl.BlockSpec((tl, D), lambda i: (i, 0))
    vec = pl.BlockSpec((1, D), lambda i: (0, 0))
    return _pcall(body, name="rms_bwd", grid=(L // tl,), in_specs=[row, row, vec, row] + dep_specs,
                  out_specs=[row, vec], out_shape=[_sds((L, D)), _sds((1, D))], vmem_mb=40)(dh, x, g, dxn, *deps)


def _s5_param_math(lre, lim, ldt, bre, bim):
    lr = jnp.minimum(lre, -1e-4)
    dt = jnp.exp(ldt)
    mag = jnp.exp(lr * dt)
    ar = mag * jnp.cos(lim * dt)
    ai = mag * jnp.sin(lim * dt)
    den = lr * lr + lim * lim
    nr = ar - 1.0
    qr = (nr * lr + ai * lim) / den
    qi = (ai * lr - nr * lim) / den
    return ar, ai, qr * bre - qi * bim, qr * bim + qi * bre


def _group_of_lane(width):
    lane = lax.broadcasted_iota(jnp.int32, (16, width), 1)
    return lax.shift_right_logical(jnp.bitwise_and(lane, HW - 1), 6)


def _vec_spec():
    return pl.BlockSpec((None, 1, HW), lambda n: (n, 0, 0))


def _mat_spec():
    return pl.BlockSpec((None, 16, HW), lambda n: (n, 0, 0))


def _s5_prep(lre, lim, ldt, bre, bim, cre, cim):
    n = lre.shape[0]

    def body(lre_ref, lim_ref, ldt_ref, bre_ref, bim_ref, cre_ref, cim_ref, ar_ref, ai_ref, wb_ref, wct_ref):
        ar, ai, bbr, bbi = _s5_param_math(lre_ref[...], lim_ref[...], ldt_ref[...], bre_ref[...], bim_ref[...])
        ar_ref[...] = ar
        ai_ref[...] = ai
        group = _group_of_lane(SW)
        bb = jnp.concatenate([bbr, bbi], axis=1)
        cc = jnp.concatenate([cre_ref[...], -cim_ref[...]], axis=1)
        for g in range(8):
            rows = pl.ds(16 * g, 16)
            wb_ref[rows, :] = jnp.where(group == g, bb, 0.0).astype(BF16)
            wct_ref[rows, :] = jnp.where(group == g, cc, 0.0).astype(BF16)

    wide = pl.BlockSpec((None, 128, SW), lambda n: (n, 0, 0))
    return _pcall(body, name="s5_prep", grid=(n,),
                  in_specs=[_vec_spec()] * 3 + [_mat_spec()] * 4,
                  out_specs=[_vec_spec(), _vec_spec(), wide, wide],
                  out_shape=[_sds((n, 1, HW)), _sds((n, 1, HW)), _sds((n, 128, SW), BF16), _sds((n, 128, SW), BF16)])(
                      lre, lim, ldt, bre, bim, cre, cim)


def _s5_prep_bwd(lre, lim, ldt, bre, bim, ga, gbb):
    n = lre.shape[0]

    def body(lre_ref, lim_ref, ldt_ref, bre_ref, bim_ref, ga_ref, gbb_ref,
             dlre_ref, dlim_ref, dldt_ref, dbre_ref, dbim_ref):
        _, vjp = jax.vjp(_s5_param_math, lre_ref[...], lim_ref[...], ldt_ref[...], bre_ref[...], bim_ref[...])
        ga, gbb = ga_ref[...], gbb_ref[...]
        dlre, dlim, dldt, dbre, dbim = vjp((ga[:, :HW], ga[:, HW:], gbb[:, :HW], gbb[:, HW:]))
        dlre_ref[...] = dlre
        dlim_ref[...] = dlim
        dldt_ref[...] = dldt
        dbre_ref[...] = dbre
        dbim_ref[...] = dbim

    return _pcall(body, name="s5_prep_bwd", grid=(n,),
                  in_specs=[_vec_spec()] * 3 + [_mat_spec()] * 2 +
                  [pl.BlockSpec((None, 1, SW), lambda n: (n, 0, 0)), pl.BlockSpec((None, 16, SW), lambda n: (n, 0, 0))],
                  out_specs=[_vec_spec()] * 3 + [_mat_spec()] * 2,
                  out_shape=[_sds((n, 1, HW))] * 3 + [_sds((n, 16, HW))] * 2)(lre, lim, ldt, bre, bim, ga, gbb)


def _diag_blocks(full):
    group = _group_of_lane(SW)
    out = jnp.where(group == 0, full[0:16, :], 0.0)
    for g in range(1, 8):
        out = out + jnp.where(group == g, full[16 * g:16 * g + 16, :], 0.0)
    return out


def _permute_rows(dst_ref, src_ref):
    for s in range(NSEG):
        dst_ref[pl.ds(s, SEG, stride=NSEG), :] = src_ref[pl.ds(s * SEG, SEG), :]


def _unpermute_rows(src_ref, s):
    return src_ref[pl.ds(s, SEG, stride=NSEG), :]


def _row_of(slab, rows, j):
    return jnp.sum(jnp.where(rows == j, slab, 0.0), axis=0, keepdims=True)


def _s5_scan_fwd(proj, wb4, wc4, ar4, ai4):
    def body(u_ref, wb_ref, wc_ref, ar_ref, ai_ref, x_ref, y_ref, up_ref, d_ref, yp_ref):
        _permute_rows(up_ref, u_ref)
        d_ref[...] = _dot(up_ref[...].astype(BF16), wb_ref[...])
        ar = jnp.broadcast_to(ar_ref[...], (NSEG, HW))
        ai = jnp.broadcast_to(ai_ref[...], (NSEG, HW))
        zero = jnp.zeros((NSEG, HW), F32)

        def drive(t):
            row = pl.multiple_of(t * NSEG, NSEG)
            dd = d_ref[pl.ds(row, NSEG), :]
            return row, dd[:, :HW], dd[:, HW:]

        def local_step(t, c):
            xr, xi = c
            _, dr, di = drive(t)
            return ar * xr - ai * xi + dr, ar * xi + ai * xr + di

        fr, fi = lax.fori_loop(0, SEG, local_step, (zero, zero), unroll=8)
        pr, pi_ = ar_ref[...], ai_ref[...]
        for _ in range(int(math.log2(SEG))):
            pr, pi_ = pr * pr - pi_ * pi_, 2.0 * pr * pi_
        rows = lax.broadcasted_iota(jnp.int32, (NSEG, HW), 0)
        cr, ci = zero, zero
        for j in range(NSEG - 1):
            sr, si = _row_of(cr, rows, j), _row_of(ci, rows, j)
            gr, gi = _row_of(fr, rows, j), _row_of(fi, rows, j)
            nr = pr * sr - pi_ * si + gr
            ni = pr * si + pi_ * sr + gi
            cr = jnp.where(rows == j + 1, nr, cr)
            ci = jnp.where(rows == j + 1, ni, ci)

        def true_step(t, c):
            xr, xi = c
            row, dr, di = drive(t)
            nr = ar * xr - ai * xi + dr
            ni = ar * xi + ai * xr + di
            x_ref[pl.ds(row, NSEG), :] = jnp.concatenate([nr, ni], axis=1)
            return nr, ni

        lax.fori_loop(0, SEG, true_step, (cr, ci), unroll=8)
        yp_ref[...] = _dot_nt(x_ref[...].astype(BF16), wc_ref[...])
        for s in range(NSEG):
            y_ref[pl.ds(s * SEG, SEG), :] = _unpermute_rows(yp_ref, s)

    return _pcall(
        body, name="s5_scan_fwd", grid=(NCB,),
        in_specs=[pl.BlockSpec((L, 128), lambda b: (0, COL_U + b)),
                  pl.BlockSpec((None, 128, SW), lambda b: (b, 0, 0)),
                  pl.BlockSpec((None, 128, SW), lambda b: (b, 0, 0)),
                  pl.BlockSpec((None, 1, HW), lambda b: (b, 0, 0)),
                  pl.BlockSpec((None, 1, HW), lambda b: (b, 0, 0))],
        out_specs=[pl.BlockSpec((None, L, SW), lambda b: (b, 0, 0)),
                   pl.BlockSpec((L, 128), lambda b: (0, b))],
        out_shape=[_sds((NCB, L, SW)), _sds((L, SSM))],
        scratch=[pltpu.VMEM((L, 128), F32), pltpu.VMEM((L, SW), F32), pltpu.VMEM((L, 128), F32)],
        vmem_mb=56)(proj, wb4, wc4, ar4, ai4)


def _s5_scan_bwd(dy, xs, proj, du_skip, wb4, wc4, ar4, ai4, dproj):
    def body(dy_ref, x_ref, u_ref, dus_ref, wb_ref, wc_ref, ar_ref, ai_ref, _,
             du_ref, dbb_ref, dcc_ref, da_ref, dyp_ref, up_ref, g_ref, dup_ref):
        _permute_rows(dyp_ref, dy_ref)
        _permute_rows(up_ref, u_ref)
        dyp = dyp_ref[...].astype(BF16)
        g_ref[...] = _dot(dyp, wc_ref[...])
        dcc = _diag_blocks(_dot_tn(dyp, x_ref[...].astype(BF16)))
        dcc_ref[...] = jnp.concatenate([dcc[:, :HW], -dcc[:, HW:]], axis=1)
        ar = jnp.broadcast_to(ar_ref[...], (NSEG, HW))
        ai = jnp.broadcast_to(ai_ref[...], (NSEG, HW))
        zero = jnp.zeros((NSEG, HW), F32)

        def load(t):
            row = pl.multiple_of(t * NSEG, NSEG)
            gg = g_ref[pl.ds(row, NSEG), :]
            return row, gg[:, :HW], gg[:, HW:]

        def local_step(k, c):
            lr, li = c
            _, gr, gi = load(SEG - 1 - k)
            return ar * lr + ai * li + gr, ar * li - ai * lr + gi

        fr, fi = lax.fori_loop(0, SEG, local_step, (zero, zero), unroll=8)
        pr, pi_ = ar_ref[...], -ai_ref[...]
        for _ in range(int(math.log2(SEG))):
            pr, pi_ = pr * pr - pi_ * pi_, 2.0 * pr * pi_
        rows = lax.broadcasted_iota(jnp.int32, (NSEG, HW), 0)
        cr, ci = zero, zero
        for j in range(NSEG - 1, 0, -1):
            sr, si = _row_of(cr, rows, j), _row_of(ci, rows, j)
            gr, gi = _row_of(fr, rows, j), _row_of(fi, rows, j)
            nr = pr * sr - pi_ * si + gr
            ni = pr * si + pi_ * sr + gi
            cr = jnp.where(rows == j - 1, nr, cr)
            ci = jnp.where(rows == j - 1, ni, ci)

        def true_step(k, c):
            lr, li, gar, gai = c
            t = SEG - 1 - k
            row, gr, gi = load(t)
            nr = ar * lr + ai * li + gr
            ni = ar * li - ai * lr + gi
            g_ref[pl.ds(row, NSEG), :] = jnp.concatenate([nr, ni], axis=1)
            prow = pl.multiple_of(jnp.maximum(t - 1, 0) * NSEG, NSEG)
            xp = x_ref[pl.ds(prow, NSEG), :]
            live = (t > 0).astype(F32)
            xr, xi = xp[:, :HW] * live, xp[:, HW:] * live
            return nr, ni, gar + (nr * xr + ni * xi), gai + (ni * xr - nr * xi)

        _, _, gar, gai = lax.fori_loop(0, SEG, true_step, (cr, ci, zero, zero), unroll=4)
        l0 = g_ref[1:NSEG, :]
        xl = x_ref[(SEG - 1) * NSEG:(SEG - 1) * NSEG + NSEG - 1, :]
        l0r, l0i, xlr, xli = l0[:, :HW], l0[:, HW:], xl[:, :HW], xl[:, HW:]
        gar_t = jnp.sum(gar, axis=0, keepdims=True) + jnp.sum(l0r * xlr + l0i * xli, axis=0, keepdims=True)
        gai_t = jnp.sum(gai, axis=0, keepdims=True) + jnp.sum(l0i * xlr - l0r * xli, axis=0, keepdims=True)
        da_ref[...] = jnp.concatenate([gar_t, gai_t], axis=1)
        lam = g_ref[...].astype(BF16)
        dbb_ref[...] = _diag_blocks(_dot_tn(up_ref[...].astype(BF16), lam))
        dup_ref[...] = _dot_nt(lam, wb_ref[...])
        for s in range(NSEG):
            sl = pl.ds(s * SEG, SEG)
            du_ref[sl, :] = _unpermute_rows(dup_ref, s) + dus_ref[sl, :]

    col = lambda off: pl.BlockSpec((L, 128), lambda b: (0, off + b))
    return _pcall(
        body, name="s5_scan_bwd", grid=(NCB,),
        in_specs=[col(0), pl.BlockSpec((None, L, SW), lambda b: (b, 0, 0)), col(COL_U), col(0),
                  pl.BlockSpec((None, 128, SW), lambda b: (b, 0, 0)),
                  pl.BlockSpec((None, 128, SW), lambda b: (b, 0, 0)),
                  pl.BlockSpec((None, 1, HW), lambda b: (b, 0, 0)),
                  pl.BlockSpec((None, 1, HW), lambda b: (b, 0, 0)),
                  pl.BlockSpec(memory_space=pl.ANY)],
        out_specs=[col(COL_U), pl.BlockSpec((None, 16, SW), lambda b: (b, 0, 0)),
                   pl.BlockSpec((None, 16, SW), lambda b: (b, 0, 0)),
                   pl.BlockSpec((None, 1, SW), lambda b: (b, 0, 0))],
        out_shape=[_sds((L, NCOL)), _sds((NCB, 16, SW)), _sds((NCB, 16, SW)), _sds((NCB, 1, SW))],
        scratch=[pltpu.VMEM((L, 128), F32), pltpu.VMEM((L, 128), F32), pltpu.VMEM((L, SW), F32),
                 pltpu.VMEM((L, 128), F32)],
        aliases={8: 0}, vmem_mb=56)(dy, xs, proj, du_skip, wb4, wc4, ar4, ai4, dproj)


def _s5_tail_fwd(yraw, proj, dsk, wglu, bglu):
    tl = 512

    def body(y_ref, u_ref, z_ref, dsk_ref, w_ref, b_ref, o_ref):
        y1 = y_ref[...] + dsk_ref[...] * u_ref[...]
        y2 = _gelu(y1)
        gl = _dot(y2.astype(BF16), w_ref[...]) + b_ref[...]
        z = z_ref[...]
        o_ref[...] = (y2 * _sigmoid(gl)) * (z * _sigmoid(z))

    blk = lambda c: pl.BlockSpec((tl, SSM), lambda i: (i, c))
    vec = pl.BlockSpec((1, SSM), lambda i: (0, 0))
    return _pcall(body, name="s5_tail_fwd", grid=(L // tl,),
                  in_specs=[blk(0), blk(0), blk(1), vec, pl.BlockSpec((SSM, SSM), lambda i: (0, 0)), vec],
                  out_specs=blk(0), out_shape=_sds((L, SSM)), vmem_mb=40)(yraw, proj, proj, dsk, wglu, bglu)


def _s5_tail_bwd(dys, yraw, proj, dsk, wglu, bglu, dproj):
    tl = 512
    ni = L // tl

    def body(dys_ref, y_ref, u_ref, z_ref, dsk_ref, w_ref, b_ref, _,
             dy_ref, dus_ref, dz_ref, dw_ref, db_ref, dd_ref, acc_ref):
        i = pl.program_id(0)
        u = u_ref[...]
        y1 = y_ref[...] + dsk_ref[...] * u
        y2 = _gelu(y1)
        y2b = y2.astype(BF16)
        sg = _sigmoid(_dot(y2b, w_ref[...]) + b_ref[...])
        y3 = y2 * sg
        z = z_ref[...]
        sz = _sigmoid(z)
        dys = dys_ref[...]
        dy3 = dys * (z * sz)
        dz_ref[...] = dys * y3 * (sz * (1.0 + z * (1.0 - sz)))
        dgl = (dy3 * y2) * (sg * (1.0 - sg))
        dglb = dgl.astype(BF16)
        dy2 = dy3 * sg + _dot_nt(dglb, w_ref[...])
        dy1 = dy2 * _gelu_grad(y1)
        dy_ref[...] = dy1
        dus_ref[...] = dsk_ref[...] * dy1
        dw = _dot_tn(y2b, dglb)
        db = jnp.sum(dgl, axis=0, keepdims=True)
        dd = jnp.sum(dy1 * u, axis=0, keepdims=True)

        @pl.when(i == 0)
        def _():
            acc_ref[...] = dw
            db_ref[...] = db
            dd_ref[...] = dd

        @pl.when(i > 0)
        def _():
            acc_ref[...] += dw
            db_ref[...] += db
            dd_ref[...] += dd

        @pl.when(i == ni - 1)
        def _():
            dw_ref[...] = acc_ref[...].astype(BF16)

    blk = lambda c: pl.BlockSpec((tl, SSM), lambda i: (i, c))
    vec = pl.BlockSpec((1, SSM), lambda i: (0, 0))
    mat = pl.BlockSpec((SSM, SSM), lambda i: (0, 0))
    return _pcall(body, name="s5_tail_bwd", grid=(ni,),
                  in_specs=[blk(0), blk(0), blk(0), blk(1), vec, mat, vec, pl.BlockSpec(memory_space=pl.ANY)],
                  out_specs=[blk(0), blk(0), blk(COL_ZS // 4), mat, vec, vec],
                  out_shape=[_sds((L, SSM)), _sds((L, SSM)), _sds((L, NCOL)), _sds((SSM, SSM), BF16),
                             _sds((1, SSM)), _sds((1, SSM))],
                  scratch=[pltpu.VMEM((SSM, SSM), F32)], aliases={7: 2},
                  vmem_mb=40)(dys, yraw, proj, proj, dsk, wglu, bglu, dproj)


def _attn_blocks(dil):
    nb = L // dil // HEAD
    return [(i * HEAD * dil + r, (i - 1) * HEAD * dil + r if i > 0 else None)
            for r in range(dil) for i in range(nb)]


def _rows(start, dil):
    return pl.ds(start, HEAD) if dil == 1 else pl.ds(start, HEAD, stride=dil)


def _band_masks():
    ri = lax.broadcasted_iota(jnp.int32, (HEAD, HEAD), 0)
    ci = lax.broadcasted_iota(jnp.int32, (HEAD, HEAD), 1)
    return ci <= ri, ci >= ri


def _qkv_specs(index):
    def spec(off):
        return pl.BlockSpec((L, HEAD), lambda gi, h: (0, off + index(gi, h)))
    return [spec(off) for off in (COL_Q, COL_K, COL_V)]


def _attn_core_fwd(proj):
    def body(q_ref, k_ref, v_ref, o_ref, lse_ref):
        gi = pl.program_id(0)
        m_cur, m_prev = _band_masks()
        for g, dil in enumerate(DILATIONS):
            @pl.when(gi == g)
            def _(dil=dil):
                for st, pst in _attn_blocks(dil):
                    r = _rows(st, dil)
                    q = q_ref[r, :].astype(BF16)
                    s_c = jnp.where(m_cur, _dot_nt(q, k_ref[r, :].astype(BF16)) * SCALE, NEG)
                    mx = jnp.max(s_c, axis=-1, keepdims=True)
                    if pst is not None:
                        rp = _rows(pst, dil)
                        s_p = jnp.where(m_prev, _dot_nt(q, k_ref[rp, :].astype(BF16)) * SCALE, NEG)
                        mx = jnp.maximum(mx, jnp.max(s_p, axis=-1, keepdims=True))
                    p_c = jnp.exp(s_c - mx)
                    den = jnp.sum(p_c, axis=-1, keepdims=True)
                    acc = _dot(p_c.astype(BF16), v_ref[r, :].astype(BF16))
                    if pst is not None:
                        p_p = jnp.exp(s_p - mx)
                        den = den + jnp.sum(p_p, axis=-1, keepdims=True)
                        acc = acc + _dot(p_p.astype(BF16), v_ref[rp, :].astype(BF16))
                    o_ref[r, :] = acc / den
                    lse_ref[r, :] = jnp.broadcast_to(mx + jnp.log(den), (HEAD, HEAD))

    idx = lambda gi, h: gi * 4 + h
    out = pl.BlockSpec((L, HEAD), lambda gi, h: (0, gi * 4 + h))
    return _pcall(body, name="attn_core_fwd", grid=(3, 4), in_specs=_qkv_specs(idx), out_specs=[out, out],
                  out_shape=[_sds((L, QKV)), _sds((L, QKV))], vmem_mb=40)(proj, proj, proj)


def _attn_mix_fwd(o, lse, proj):
    tl = 512

    def body(o_ref, l_ref, z_ref, y_ref):
        for h in range(4):
            c = [pl.ds((g * 4 + h) * HEAD, HEAD) for g in range(3)]
            ls = [l_ref[:, c[g]] for g in range(3)]
            m = jnp.maximum(jnp.maximum(ls[0], ls[1]), ls[2])
            e = [jnp.exp(ls[g] - m) for g in range(3)]
            y = (e[0] * o_ref[:, c[0]] + e[1] * o_ref[:, c[1]] + e[2] * o_ref[:, c[2]]) / (e[0] + e[1] + e[2])
            z = z_ref[:, pl.ds(h * HEAD, HEAD)]
            y_ref[:, pl.ds(h * HEAD, HEAD)] = y * (z * _sigmoid(z))

    wide = pl.BlockSpec((tl, QKV), lambda i: (i, 0))
    return _pcall(body, name="attn_mix_fwd", grid=(L // tl,),
                  in_specs=[wide, wide, pl.BlockSpec((tl, AW), lambda i: (i, COL_ZA // 4))],
                  out_specs=pl.BlockSpec((tl, AW), lambda i: (i, 0)), out_shape=_sds((L, AW)),
                  vmem_mb=40)(o, lse, proj)


def _attn_mix_bwd(dya, o, lse, proj, dproj):
    tl = 512

    def body(dya_ref, o_ref, l_ref, z_ref, _, do_ref, c_ref, dz_ref):
        for h in range(4):
            c = [pl.ds((g * 4 + h) * HEAD, HEAD) for g in range(3)]
            hs = pl.ds(h * HEAD, HEAD)
            ls = [l_ref[:, c[g]] for g in range(3)]
            m = jnp.maximum(jnp.maximum(ls[0], ls[1]), ls[2])
            e = [jnp.exp(ls[g] - m) for g in range(3)]
            den = e[0] + e[1] + e[2]
            al = [e[g] / den for g in range(3)]
            y = al[0] * o_ref[:, c[0]] + al[1] * o_ref[:, c[1]] + al[2] * o_ref[:, c[2]]
            z = z_ref[:, hs]
            sz = _sigmoid(z)
            dya = dya_ref[:, hs]
            dz_ref[:, hs] = dya * y * (sz * (1.0 + z * (1.0 - sz)))
            dy = dya * (z * sz)
            tot = jnp.sum(dy * y, axis=-1, keepdims=True)
            for g in range(3):
                do_ref[:, c[g]] = al[g] * dy
                c_ref[:, c[g]] = -(al[g] * tot)

    wide = pl.BlockSpec((tl, QKV), lambda i: (i, 0))
    nar = pl.BlockSpec((tl, AW), lambda i: (i, 0))
    za = pl.BlockSpec((tl, AW), lambda i: (i, COL_ZA // 4))
    return _pcall(body, name="attn_mix_bwd", grid=(L // tl,),
                  in_specs=[nar, wide, wide, za, pl.BlockSpec(memory_space=pl.ANY)],
                  out_specs=[wide, wide, za], out_shape=[_sds((L, QKV)), _sds((L, QKV)), _sds((L, NCOL))],
                  aliases={4: 2}, vmem_mb=48)(dya, o, lse, proj, dproj)


def _attn_core_bwd(proj, do, lse, cc, dproj):
    def body(q_ref, k_ref, v_ref, do_ref, lse_ref, c_ref, _, dp_ref, dq_ref, dk_ref, dv_ref, sems):
        gi = pl.program_id(0)
        head = gi * 4 + pl.program_id(1)
        m_cur, m_prev = _band_masks()
        dk_ref[...] = jnp.zeros((L, HEAD), F32)
        dv_ref[...] = jnp.zeros((L, HEAD), F32)
        for g, dil in enumerate(DILATIONS):
            @pl.when(gi == g)
            def _(dil=dil):
                for st, pst in _attn_blocks(dil):
                    r = _rows(st, dil)
                    q = q_ref[r, :].astype(BF16)
                    do_b = do_ref[r, :].astype(BF16)
                    lse_b = lse_ref[r, :]
                    c_b = c_ref[r, :]
                    dq = None
                    for rr, mask in ((r, m_cur),) + (((_rows(pst, dil), m_prev),) if pst is not None else ()):
                        kb = k_ref[rr, :].astype(BF16)
                        s = _dot_nt(q, kb) * SCALE
                        p = jnp.where(mask, jnp.exp(s - lse_b), 0.0)
                        dp = _dot_nt(do_b, v_ref[rr, :].astype(BF16))
                        ds = (p * (dp + c_b) * SCALE).astype(BF16)
                        part = _dot(ds, kb)
                        dq = part if dq is None else dq + part
                        dk_ref[rr, :] += _dot_tn(ds, q)
                        dv_ref[rr, :] += _dot_tn(p.astype(BF16), do_b)
                    dq_ref[r, :] = dq

        copies = [pltpu.make_async_copy(src, dp_ref.at[:, pl.ds(pl.multiple_of((off + head) * HEAD, HEAD), HEAD)],
                                        sems.at[n])
                  for n, (src, off) in enumerate(((dq_ref, COL_Q), (dk_ref, COL_K), (dv_ref, COL_V)))]
        for cp in copies:
            cp.start()
        for cp in copies:
            cp.wait()

    idx = lambda gi, h: gi * 4 + h
    blk = pl.BlockSpec((L, HEAD), lambda gi, h: (0, gi * 4 + h))
    hbm = pl.BlockSpec(memory_space=pl.ANY)
    return _pcall(body, name="attn_core_bwd", grid=(3, 4), in_specs=_qkv_specs(idx) + [blk, blk, blk, hbm],
                  out_specs=hbm, out_shape=_sds((L, NCOL)),
                  scratch=[pltpu.VMEM((L, HEAD), F32)] * 3 + [pltpu.SemaphoreType.DMA((3,))],
                  aliases={6: 0}, vmem_mb=48)(proj, proj, proj, do, lse, cc, dproj)


def _merge_fwd_math(ys_b, ya_b, gs, ga, wbs_ref, wba_ref, wout_ref):
    bs = jnp.concatenate([_dot(ys_b, wbs_ref[s]) for s in range(NCHIP)], axis=1)
    ba = jnp.concatenate([_dot(ya_b, wba_ref[s]) for s in range(NCHIP)], axis=1)
    sgs, sga = _sigmoid(gs), _sigmoid(ga)
    merged = sgs * bs + sga * ba
    out = _dot(merged.astype(BF16), wout_ref[...])
    inv = lax.rsqrt(jnp.mean(out * out, axis=-1, keepdims=True) + RMS_EPS)
    return bs, ba, sgs, sga, merged, out, inv


def _merge_specs(tl):
    row = lambda w, c: pl.BlockSpec((tl, w), lambda i: (i, c))
    return [row(SSM, 0), row(AW, 0), row(D, COL_GS // 8), row(D, COL_GA // 8),
            pl.BlockSpec((NCHIP, SSM, D // NCHIP), lambda i: (0, 0, 0)),
            pl.BlockSpec((NCHIP, AW, D // NCHIP), lambda i: (0, 0, 0)),
            pl.BlockSpec((D, D), lambda i: (0, 0)),
            pl.BlockSpec((1, D), lambda i: (0, 0))]


def _merge_out_fwd(ys, ya, proj, wbs4, wba4, wout, g2, x, dep=None):
    tl = 256
    deps, dep_specs = _token_operand(dep)

    def body(ys_ref, ya_ref, gs_ref, ga_ref, wbs_ref, wba_ref, wout_ref, g2_ref, x_ref, *rest):
        o_ref = rest[-1]
        *_, out, inv = _merge_fwd_math(ys_ref[...].astype(BF16), ya_ref[...].astype(BF16), gs_ref[...],
                                       ga_ref[...], wbs_ref, wba_ref, wout_ref)
        o_ref[...] = x_ref[...] + out * inv * g2_ref[...]

    row = pl.BlockSpec((tl, D), lambda i: (i, 0))
    return _pcall(body, name="merge_out_fwd", grid=(L // tl,), in_specs=_merge_specs(tl) + [row] + dep_specs,
                  out_specs=row, out_shape=_sds((L, D)),
                  vmem_mb=48)(ys, ya, proj, proj, wbs4, wba4, wout, g2, x, *deps)


def _merge_out_bwd(dxn, ys, ya, proj, wbs4, wba4, wout, g2, dep=None):
    tl = 256
    ni = L // tl
    cw = D // NCHIP
    deps, dep_specs = _token_operand(dep)

    def body(dxn_ref, ys_ref, ya_ref, gs_ref, ga_ref, wbs_ref, wba_ref, wout_ref, g2_ref, *rest):
        (dys_ref, dya_ref, dgate_ref, dwbs_ref, dwba_ref, dwout_ref, dg2_ref,
         abs_ref, aba_ref, aout_ref) = rest[len(deps):]
        i = pl.program_id(0)
        ys_b, ya_b = ys_ref[...].astype(BF16), ya_ref[...].astype(BF16)
        bs, ba, sgs, sga, merged, out, inv = _merge_fwd_math(ys_b, ya_b, gs_ref[...], ga_ref[...],
                                                             wbs_ref, wba_ref, wout_ref)
        nrm = out * inv
        dxn = dxn_ref[...]
        dg2 = jnp.sum(dxn * nrm, axis=0, keepdims=True)
        dn = dxn * g2_ref[...]
        dout = (inv * (dn - nrm * jnp.mean(dn * nrm, axis=-1, keepdims=True))).astype(BF16)
        dwout = _dot_tn(merged.astype(BF16), dout)
        dm = _dot_nt(dout, wout_ref[...])
        dbs, dba = dm * sgs, dm * sga
        dgate_ref[:, :D] = dm * bs * (sgs * (1.0 - sgs))
        dgate_ref[:, D:] = dm * ba * (sga * (1.0 - sga))
        dbs_b, dba_b = dbs.astype(BF16), dba.astype(BF16)
        dys = None
        dya = None
        first = i == 0
        for s in range(NCHIP):
            cs = slice(s * cw, (s + 1) * cw)
            p_s = _dot_nt(dbs_b[:, cs], wbs_ref[s])
            p_a = _dot_nt(dba_b[:, cs], wba_ref[s])
            dys = p_s if dys is None else dys + p_s
            dya = p_a if dya is None else dya + p_a
            w_s = _dot_tn(ys_b, dbs_b[:, cs])
            w_a = _dot_tn(ya_b, dba_b[:, cs])

            @pl.when(first)
            def _(s=s, w_s=w_s, w_a=w_a):
                abs_ref[s] = w_s
                aba_ref[s] = w_a

            @pl.when(jnp.logical_not(first))
            def _(s=s, w_s=w_s, w_a=w_a):
                abs_ref[s] += w_s
                aba_ref[s] += w_a

        dys_ref[...] = dys
        dya_ref[...] = dya

        @pl.when(first)
        def _():
            aout_ref[...] = dwout
            dg2_ref[...] = dg2

        @pl.when(jnp.logical_not(first))
        def _():
            aout_ref[...] += dwout
            dg2_ref[...] += dg2

        @pl.when(i == ni - 1)
        def _():
            dwbs_ref[...] = abs_ref[...].astype(BF16)
            dwba_ref[...] = aba_ref[...].astype(BF16)
            dwout_ref[...] = aout_ref[...].astype(BF16)

    row = lambda w: pl.BlockSpec((tl, w), lambda i: (i, 0))
    w4 = pl.BlockSpec((NCHIP, SSM, cw), lambda i: (0, 0, 0))
    sq = pl.BlockSpec((D, D), lambda i: (0, 0))
    vec = pl.BlockSpec((1, D), lambda i: (0, 0))
    return _pcall(body, name="merge_out_bwd", grid=(ni,), in_specs=[row(D)] + _merge_specs(tl) + dep_specs,
                  out_specs=[row(SSM), row(AW), pl.BlockSpec((tl, 2 * D), lambda i: (i, COL_GS // 16)), w4, w4, sq,
                             vec],
                  out_shape=[_sds((L, SSM)), _sds((L, AW)), _sds((L, NCOL)),
                             _sds((NCHIP, SSM, cw), BF16), _sds((NCHIP, AW, cw), BF16), _sds((D, D), BF16),
                             _sds((1, D))],
                  scratch=[pltpu.VMEM((NCHIP, SSM, cw), F32), pltpu.VMEM((NCHIP, AW, cw), F32),
                           pltpu.VMEM((D, D), F32)],
                  vmem_mb=56)(dxn, ys, ya, proj, proj, wbs4, wba4, wout, g2, *deps)


def _loss_head(y, target):
    tl = 512

    def body(y_ref, t_ref, loss_ref, dy_ref):
        i = pl.program_id(0)
        err = y_ref[...] - t_ref[...]
        dy_ref[...] = err / D
        part = 0.5 * jnp.sum(jnp.mean(err * err, axis=-1, keepdims=True), axis=0, keepdims=True)
        part = jnp.broadcast_to(part, (8, 128))

        @pl.when(i == 0)
        def _():
            loss_ref[...] = part

        @pl.when(i > 0)
        def _():
            loss_ref[...] += part

    row = pl.BlockSpec((tl, D), lambda i: (i, 0))
    return _pcall(body, name="loss_head", grid=(L // tl,), in_specs=[row, row],
                  out_specs=[pl.BlockSpec((8, 128), lambda i: (0, 0)), row],
                  out_shape=[_sds((8, 128)), _sds((L, D))], vmem_mb=40)(y, target)


def _adamw_math(w, g, m, v):
    m = ADAM_B1 * m + (1.0 - ADAM_B1) * g
    v = ADAM_B2 * v + (1.0 - ADAM_B2) * (g * g)
    m_hat = m / (1.0 - ADAM_B1 ** ADAM_STEP)
    v_hat = v / (1.0 - ADAM_B2 ** ADAM_STEP)
    delta = -ADAM_LR * (m_hat / (jnp.sqrt(v_hat) + ADAM_EPS) + ADAM_WD * w)
    return delta, m, v


def _adamw_big(layer, w, m, v, own, sib, prev):
    _, r, c = w.shape
    tr = min(r, 128)

    def body(w_ref, m_ref, v_ref, own_ref, sib_ref, *rest):
        g_ref, d_ref, nm_ref, nv_ref = rest[-4:]
        a = own_ref[0].astype(F32)
        b = sib_ref[0].astype(F32)
        for s in range(1, NCHIP):
            a = a + own_ref[s].astype(F32)
            b = b + sib_ref[s].astype(F32)
        g = a + b
        delta, nm, nv = _adamw_math(w_ref[...], g, m_ref[...], v_ref[...])
        g_ref[...] = g
        d_ref[...] = delta
        nm_ref[...] = nm
        nv_ref[...] = nv

    lay = pl.BlockSpec((None, tr, c), lambda i: (layer, i, 0))
    slots = pl.BlockSpec((NCHIP, tr, c), lambda i: (0, i, 0))
    ins = [w, m, v, own, sib]
    in_specs = [lay, lay, lay, slots, slots]
    aliases = {}
    if prev is not None:
        ins += list(prev)
        in_specs += [pl.BlockSpec(memory_space=pl.ANY)] * 4
        aliases = {5 + k: k for k in range(4)}
    return _pcall(body, name="adamw_big", grid=(r // tr,), in_specs=in_specs, out_specs=[lay] * 4,
                  out_shape=[_sds(w.shape)] * 4, aliases=aliases, vmem_mb=48)(*ins)


SMALL_TILE = 512


def _adamw_small(w, parts, m, v):
    r = w.shape[0]

    def body(w_ref, p_ref, m_ref, v_ref, g_ref, d_ref, nm_ref, nv_ref):
        g = p_ref[0].astype(F32)
        for dev in range(1, 8):
            g = g + p_ref[dev].astype(F32)
        delta, nm, nv = _adamw_math(w_ref[...], g, m_ref[...], v_ref[...])
        g_ref[...] = g
        d_ref[...] = delta
        nm_ref[...] = nm
        nv_ref[...] = nv

    row = pl.BlockSpec((SMALL_TILE, 128), lambda i: (i, 0))
    return _pcall(body, name="adamw_small", grid=(r // SMALL_TILE,),
                  in_specs=[row, pl.BlockSpec((8, SMALL_TILE, 128), lambda i: (0, i, 0)), row, row],
                  out_specs=[row] * 4, out_shape=[_sds((r, 128))] * 4, vmem_mb=40)(w, parts, m, v)


def _place():
    x, y, c = lax.axis_index("x"), lax.axis_index("y"), lax.axis_index("c")
    return x, y, c, 2 * x + y


def _chip_peer(x, y, j):
    return (1 - x if j & 2 else x), (1 - y if j & 1 else y)


_HBM = pl.BlockSpec(memory_space=pltpu.HBM)
_SEM = pl.BlockSpec(memory_space=pltpu.SEMAPHORE)
_EFFECT = pltpu.SideEffectType.DATAFLOW_SIDE_EFFECTING


def _in_hbm(a):
    return pltpu.with_memory_space_constraint(a, pltpu.HBM)


def _copies_start(name, plan, ncopy, srcs, lands, dep=None):
    n = len(srcs) + len(lands)
    deps, dep_specs = _token_operand(dep)

    def body(*refs):
        send_sems, recv_sems = refs[n + len(deps)], refs[n + len(deps) + 1]
        token = refs[-1]
        for k, (src, dst, dev) in enumerate(plan(refs[:len(srcs)], refs[len(srcs):n])):
            pltpu.make_async_remote_copy(src_ref=src, dst_ref=dst, send_sem=send_sems.at[k],
                                         recv_sem=recv_sems.at[k], device_id=dev, device_id_type=MESH).start()
        token[...] = jnp.zeros_like(token)

    bufs = list(srcs) + list(lands)
    outs = pl.pallas_call(
        body, name=name,
        out_shape=(pltpu.SemaphoreType.DMA((ncopy,)), pltpu.SemaphoreType.DMA((ncopy,)),
                   *[pltpu.HBM(a.shape, a.dtype) for a in bufs], _sds((8, 128))),
        in_specs=[_HBM] * n + dep_specs,
        out_specs=(_SEM, _SEM, *[_HBM] * n, pl.BlockSpec(memory_space=pltpu.VMEM)),
        input_output_aliases={i: 2 + i for i in range(n)},
        compiler_params=pltpu.CompilerParams(has_side_effects=_EFFECT),
    )(*[_in_hbm(a) for a in bufs], *deps)
    return outs[0], outs[1], list(outs[2:2 + len(srcs)]), list(outs[2 + len(srcs):2 + n]), outs[-1]


def _copies_wait(name, plan, send_sems, recv_sems, srcs, lands, after):
    n = len(srcs) + len(lands)
    after = list(after)

    def body(*refs):
        s_sems, r_sems = refs[n], refs[n + 1]
        for k, (src, dst, dev) in enumerate(plan(refs[:len(srcs)], refs[len(srcs):n])):
            cp = pltpu.make_async_remote_copy(src_ref=src, dst_ref=dst, send_sem=s_sems.at[k],
                                              recv_sem=r_sems.at[k], device_id=dev, device_id_type=MESH)
            cp.wait_send()
            cp.wait_recv()

    bufs = list(srcs) + list(lands)
    outs = pl.pallas_call(
        body, name=name,
        out_shape=tuple(pltpu.HBM(a.shape, a.dtype) for a in bufs),
        in_specs=[_HBM] * n + [_SEM, _SEM] + [pl.BlockSpec(memory_space=pl.ANY)] * len(after),
        out_specs=tuple([_HBM] * n),
        input_output_aliases={i: i for i in range(n)},
        compiler_params=pltpu.CompilerParams(has_side_effects=_EFFECT),
    )(*bufs, send_sems, recv_sems, *after)
    return list(outs[:len(srcs)]), list(outs[len(srcs):])


def _with_own_slot(block, slot, nslots=NCHIP):
    land = lax.empty((nslots,) + block.shape, block.dtype)
    return lax.dynamic_update_slice(land, block[None], (slot,) + (0,) * block.ndim)


def _my_half(land, slot, c):
    half = land.shape[1] // 2
    return land.at[slot, pl.ds(pl.multiple_of(c * half, 16), half)]


def _gather_ici_plan(srcs, lands):
    x, y, c, s = _place()
    return [(_my_half(land, s, c), _my_half(land, s, c), (*_chip_peer(x, y, j), c))
            for land in lands for j in range(1, NCHIP)]


def _gather_d2d_plan(srcs, lands):
    x, y, c, s = _place()
    return [(_my_half(land, s ^ j, c), _my_half(land, s ^ j, c), (x, y, 1 - c))
            for land in lands for j in range(1, NCHIP)]


def _exchange_plan(srcs, lands):
    x, y, c, s = _place()
    return [(src.at[s ^ j], land.at[s], (*_chip_peer(x, y, j), c))
            for src, land in zip(srcs, lands) for j in range(1, NCHIP)]


def _sibling_plan(srcs, lands):
    x, y, c, _ = _place()
    return [(src, land, (x, y, 1 - c)) for src, land in zip(srcs, lands)]


def _everyone_plan(srcs, lands):
    x, y, c, _ = _place()
    me = 4 * x + 2 * y + c
    return [(srcs[0], lands[0].at[me], (*_chip_peer(x, y, j >> 1), 1 - c if j & 1 else c)) for j in range(1, 8)]


def _flatten_small(parts):
    flat = jnp.concatenate([p.reshape(-1) for p in parts])
    n = flat.shape[0]
    rows = -(-n // (128 * SMALL_TILE)) * SMALL_TILE
    return jnp.pad(flat, (0, rows * 128 - n)).reshape(rows, 128)


def _unflatten_small(buf, like):
    flat = buf.reshape(-1)
    out, at = [], 0
    for p in like:
        out.append(flat[at:at + p.size].reshape(p.shape))
        at += p.size
    return out


def kernel(x, pre_norm_g, w_in, lambda_re, lambda_im, log_dt, b_re, b_im, c_re, c_im, d_skip, w_glu, b_glu, w_branch_s, w_branch_a, w_out, post_norm_g, loss_target, m_pre_norm_g, m_w_in, m_lambda_re, m_lambda_im, m_log_dt, m_b_re, m_b_im, m_c_re, m_c_im, m_d_skip, m_w_glu, m_b_glu, m_w_branch_s, m_w_branch_a, m_w_out, m_post_norm_g, v_pre_norm_g, v_w_in, v_lambda_re, v_lambda_im, v_log_dt, v_b_re, v_b_im, v_c_re, v_c_im, v_d_skip, v_w_glu, v_b_glu, v_w_branch_s, v_w_branch_a, v_w_out, v_post_norm_g):
    nb = DEPTH * NCB
    lre_c = lambda_re.reshape(nb, 1, HW)
    lim_c = lambda_im.reshape(nb, 1, HW)
    ldt_c = jnp.broadcast_to(log_dt[:, :, None], (DEPTH, 32, 64)).reshape(nb, 1, HW)
    b_rows = lambda t: t.reshape(DEPTH, NCB, 8, 64, 16).transpose(0, 1, 4, 2, 3).reshape(nb, 16, HW)
    c_rows = lambda t: t.reshape(DEPTH, NCB, 8, 16, 64).transpose(0, 1, 3, 2, 4).reshape(nb, 16, HW)
    bre_c, bim_c = b_rows(b_re), b_rows(b_im)
    ar, ai, wb, wct = _s5_prep(lre_c, lim_c, ldt_c, bre_c, bim_c, c_rows(c_re), c_rows(c_im))
    ar4 = ar.reshape(DEPTH, NCB, 1, HW)
    ai4 = ai.reshape(DEPTH, NCB, 1, HW)
    wb = wb.reshape(DEPTH, NCB, 128, SW)
    wct = wct.reshape(DEPTH, NCB, 128, SW)

    chip = 2 * lax.axis_index("x") + lax.axis_index("y")
    device = 2 * chip + lax.axis_index("c")

    def gather_start(l, dep):
        blocks = [w_in[l], w_glu[l], w_branch_s[l], w_branch_a[l], w_out[l]]
        lands = [_with_own_slot(a.astype(BF16), chip) for a in blocks]
        return _copies_start(f"gather_start_{l}", _gather_ici_plan, 15, [], lands, dep)

    def gather_forward(l, flight, after):
        send, recv, _, lands, _ = flight
        _, lands = _copies_wait(f"gather_wait_{l}", _gather_ici_plan, send, recv, [], lands, after)
        return _copies_start(f"forward_start_{l}", _gather_d2d_plan, 15, [], lands)

    def gather_done(l, forward, after):
        send, recv, _, lands, _ = forward
        return _copies_wait(f"forward_wait_{l}", _gather_d2d_plan, send, recv, [], lands, after)[1]

    xs = [x[0]]
    saved = []
    forward = gather_forward(0, gather_start(0, None), [xs[0]])
    for l in range(DEPTH):
        w4, wglu4, wbs4, wba4, wout4 = gather_done(l, forward, [xs[l]])
        started = None
        if l + 1 < DEPTH:
            flight = gather_start(l + 1, wglu4)
            started = flight[4]
        wglu = wglu4.reshape(SSM, SSM)
        wout = wout4.reshape(D, D)
        wb4, wc4 = wb[l], wct[l]
        g1 = pre_norm_g[l].reshape(1, D)
        g2 = post_norm_g[l].reshape(1, D)
        dsk = d_skip[l].reshape(1, SSM)
        bgl = b_glu[l].reshape(1, SSM)
        proj = _rms_proj_fwd(xs[l], g1, w4, started)
        states, yraw = _s5_scan_fwd(proj, wb4, wc4, ar4[l], ai4[l])
        o, lse = _attn_core_fwd(proj)
        forwarded = None
        if l + 1 < DEPTH:
            forward = gather_forward(l + 1, flight, [yraw, o])
            forwarded = forward[4]
        ys = _s5_tail_fwd(yraw, proj, dsk, wglu, bgl)
        ya = _attn_mix_fwd(o, lse, proj)
        xs.append(_merge_out_fwd(ys, ya, proj, wbs4, wba4, wout, g2, xs[l], forwarded))
        saved.append((w4, wglu, wbs4, wba4, wout, wb4, wc4, g1, g2, dsk, bgl, proj, states, yraw, ys, ya, o, lse))

    loss_part, dx = _loss_head(xs[DEPTH], loss_target[0])
    loss = lax.psum(loss_part[0, 0], ("x", "y", "c"))

    big_w = (w_in, w_glu.reshape(DEPTH, 128, SSM), w_branch_s, w_branch_a, w_out)
    big_m = (m_w_in, m_w_glu, m_w_branch_s, m_w_branch_a, m_w_out)
    big_v = (v_w_in, v_w_glu, v_w_branch_s, v_w_branch_a, v_w_out)
    big_out = [None] * 5
    small = {k: [None] * DEPTH for k in ("g1", "da", "dbb", "dcc", "dsk", "bgl", "g2")}
    ici = [None] * DEPTH
    d2d = [None] * DEPTH

    def exchange_start(l, parts):
        lands = [_with_own_slot(lax.dynamic_index_in_dim(p, chip, 0, keepdims=False), chip) for p in parts]
        ici[l] = _copies_start(f"exchange_start_{l}", _exchange_plan, 15, parts, lands)

    def handoff_start(l, after):
        send, recv, srcs, lands, _ = ici[l]
        _, own = _copies_wait(f"exchange_wait_{l}", _exchange_plan, send, recv, srcs, lands, after)
        d2d[l] = _copies_start(f"handoff_start_{l}", _sibling_plan, 5, own,
                               [lax.empty(a.shape, a.dtype) for a in own])

    def update(l, after):
        send, recv, own, lands, _ = d2d[l]
        own, sib = _copies_wait(f"handoff_wait_{l}", _sibling_plan, send, recv, own, lands, after)
        for k in range(5):
            big_out[k] = _adamw_big(l, big_w[k], big_m[k], big_v[k], own[k], sib[k], big_out[k])

    for l in reversed(range(DEPTH)):
        w4, wglu, wbs4, wba4, wout, wb4, wc4, g1, g2, dsk, bgl, proj, states, yraw, ys, ya, o, lse = saved[l]
        started = None
        if l + 1 < DEPTH:
            started = ici[l + 1][4] + d2d[l + 2][4] if l + 2 < DEPTH else ici[l + 1][4]
        dys, dya, dproj, dwbs, dwba, dwout, dg2 = _merge_out_bwd(dx, ys, ya, proj, wbs4, wba4, wout, g2, started)
        dyraw, du_skip, dproj, dwglu, dbgl, ddsk = _s5_tail_bwd(dys, yraw, proj, dsk, wglu, bgl, dproj)
        dproj, dbb, dcc, da = _s5_scan_bwd(dyraw, states, proj, du_skip, wb4, wc4, ar4[l], ai4[l], dproj)
        do, cterm, dproj = _attn_mix_bwd(dya, o, lse, proj, dproj)
        dproj = _attn_core_bwd(proj, do, lse, cterm, dproj)
        dh, dwin = _proj_bwd(dproj, xs[l], g1, w4)
        parts = [dwin, dwglu.reshape(NCHIP, 128, SSM), dwbs, dwba, dwout.reshape(NCHIP, D // NCHIP, D)]
        exchange_start(l, parts)
        dx, dg1 = _rms_bwd(dh, xs[l], g1, dx, ici[l][4])
        if l + 1 < DEPTH:
            handoff_start(l + 1, [ici[l][4]])
        small["g1"][l], small["g2"][l], small["dsk"][l], small["bgl"][l] = dg1, dg2, ddsk, dbgl
        small["da"][l], small["dbb"][l], small["dcc"][l] = da, dbb, dcc

    dlre, dlim, dldt, dbre_c, dbim_c = _s5_prep_bwd(
        lre_c, lim_c, ldt_c, bre_c, bim_c, jnp.stack(small["da"]).reshape(nb, 1, SW),
        jnp.stack(small["dbb"]).reshape(nb, 16, SW))
    dcc = jnp.stack(small["dcc"]).reshape(nb, 16, SW)
    b_back = lambda t: t.reshape(DEPTH, NCB, 16, 8, 64).transpose(0, 1, 3, 4, 2).reshape(DEPTH, 32, 64, 16)
    c_back = lambda t: t.reshape(DEPTH, NCB, 16, 8, 64).transpose(0, 1, 3, 2, 4).reshape(DEPTH, 32, 16, 64)
    small_w = [pre_norm_g, lambda_re, lambda_im, log_dt, b_re, b_im, c_re, c_im, d_skip, b_glu, post_norm_g]
    small_m = [m_pre_norm_g, m_lambda_re, m_lambda_im, m_log_dt, m_b_re, m_b_im, m_c_re, m_c_im, m_d_skip,
               m_b_glu, m_post_norm_g]
    small_v = [v_pre_norm_g, v_lambda_re, v_lambda_im, v_log_dt, v_b_re, v_b_im, v_c_re, v_c_im, v_d_skip,
               v_b_glu, v_post_norm_g]
    small_g = [jnp.stack(small["g1"]).reshape(DEPTH, D), dlre.reshape(DEPTH, 32, 64), dlim.reshape(DEPTH, 32, 64),
               dldt.reshape(DEPTH, 32, 64).sum(-1), b_back(dbre_c), b_back(dbim_c), c_back(dcc[:, :, :HW]),
               c_back(dcc[:, :, HW:]),
               jnp.stack(small["dsk"]).reshape(DEPTH, SSM), jnp.stack(small["bgl"]).reshape(DEPTH, SSM),
               jnp.stack(small["g2"]).reshape(DEPTH, D)]
    part = _flatten_small(small_g).astype(BF16)
    send, recv, srcs, lands, token = _copies_start("small_start", _everyone_plan, 7, [part],
                                                   [_with_own_slot(part, device, 8)])
    update(DEPTH - 1, [token, ici[0][4]])
    for l in range(DEPTH - 2, 0, -1):
        update(l, [out[0] for out in big_out])
    handoff_start(0, [out[0] for out in big_out])
    update(0, [d2d[0][4]])
    _, (parts8,) = _copies_wait("small_wait", _everyone_plan, send, recv, srcs, lands, [out[0] for out in big_out])
    g_flat, d_flat, nm_flat, nv_flat = _adamw_small(_flatten_small(small_w), parts8, _flatten_small(small_m),
                                                    _flatten_small(small_v))
    sg = _unflatten_small(g_flat, small_w)
    sd = _unflatten_small(d_flat, small_w)
    snm = _unflatten_small(nm_flat, small_w)
    snv = _unflatten_small(nv_flat, small_w)

    def ordered(sm, which):
        bg = [big_out[k][which] for k in range(5)]
        bg[1] = bg[1].reshape(DEPTH, 128, SSM)
        return [sm[0], bg[0], sm[1], sm[2], sm[3], sm[4], sm[5], sm[6], sm[7], sm[8], bg[1], sm[9], bg[2], bg[3],
                bg[4], sm[10]]

    return (loss, dx[None], *ordered(sg, 0), *ordered(sd, 1), *ordered(snm, 2), *ordered(snv, 3))
```

```python
import math

import jax
import jax.numpy as jnp
from jax import lax
from jax.experimental import pallas as pl
from jax.experimental.pallas import tpu as pltpu

F32 = jnp.float32
BF16 = jnp.bfloat16
MESH = pl.DeviceIdType.MESH

DEPTH = 4
L = 2048
D = 1024
NCOL = 8192
SSM = 512
AW = 512
QKV = 1536
RMS_EPS = 1e-6
NCHIP = 4

COL_U, COL_ZS, COL_Q, COL_K, COL_V, COL_ZA, COL_GS, COL_GA = 0, 4, 8, 20, 32, 44, 48, 56

NSEG = 8
SEG = L // NSEG
NCB = 4
HW = 512
SW = 2 * HW

HEAD = 128
DILATIONS = (1, 4, 16)
SCALE = HEAD ** -0.5
NEG = -1e30

ADAM_LR, ADAM_B1, ADAM_B2, ADAM_EPS, ADAM_WD, ADAM_STEP = 0.001, 0.9, 0.999, 1e-08, 0.01, 10


def _sds(shape, dtype=F32):
    return jax.ShapeDtypeStruct(shape, dtype)


def _pcall(body, *, name, out_shape, grid=None, in_specs=None, out_specs=None, scratch=(), vmem_mb=None,
           aliases=None):
    params = {}
    if vmem_mb is not None:
        params["vmem_limit_bytes"] = vmem_mb << 20
    kw = {}
    if grid is not None:
        kw["grid"] = grid
    if in_specs is not None:
        kw["in_specs"] = in_specs
    if out_specs is not None:
        kw["out_specs"] = out_specs
    return pl.pallas_call(body, name=name, out_shape=out_shape, scratch_shapes=list(scratch),
                          compiler_params=pltpu.CompilerParams(**params),
                          input_output_aliases=aliases or {}, **kw)


def _dot(a, b):
    return jnp.dot(a, b, preferred_element_type=F32)


def _dot_nt(a, b):
    return lax.dot_general(a, b, (((1,), (1,)), ((), ())), preferred_element_type=F32)


def _dot_tn(a, b):
    return lax.dot_general(a, b, (((0,), (0,)), ((), ())), preferred_element_type=F32)


def _sigmoid(x):
    return jax.nn.sigmoid(x)


_GELU_K = math.sqrt(2.0 / math.pi)


def _gelu(x):
    return 0.5 * x * (1.0 + jnp.tanh(_GELU_K * (x + 0.044715 * (x * x * x))))


def _gelu_grad(x):
    t = jnp.tanh(_GELU_K * (x + 0.044715 * (x * x * x)))
    return 0.5 * (1.0 + t) + 0.5 * x * (1.0 - t * t) * (_GELU_K * (1.0 + 3.0 * 0.044715 * (x * x)))


def _token_operand(dep):
    if dep is None:
        return [], []
    return [dep], [pl.BlockSpec(memory_space=pl.ANY)]


def _rms_proj_fwd(x, g, w4, dep=None):
    tl, tn = 512, 1024
    deps, dep_specs = _token_operand(dep)

    def body(x_ref, g_ref, w_ref, *rest):
        o_ref, ht_ref, h_ref = rest[len(deps):]

        @pl.when(pl.program_id(1) == 0)
        def _():
            xx = x_ref[...]
            inv = lax.rsqrt(jnp.mean(xx * xx, axis=-1, keepdims=True) + RMS_EPS)
            h = xx * inv * g_ref[...]
            h_ref[...] = h.astype(BF16)
            ht_ref[...] = h.T.astype(BF16)

        o_ref[...] = _dot(h_ref[...], w_ref[...])

    return _pcall(
        body, name="rms_proj_fwd", grid=(L // tl, NCOL // tn),
        in_specs=[pl.BlockSpec((tl, D), lambda i, j: (i, 0)),
                  pl.BlockSpec((1, D), lambda i, j: (0, 0)),
                  pl.BlockSpec((None, D, tn), lambda i, j: (lax.div(j, 2), 0, lax.rem(j, 2)))] + dep_specs,
        out_specs=[pl.BlockSpec((tl, tn), lambda i, j: (i, j)), pl.BlockSpec((D, tl), lambda i, j: (0, i))],
        out_shape=[_sds((L, NCOL)), _sds((D, L), BF16)],
        scratch=[pltpu.VMEM((tl, D), BF16)], vmem_mb=48)(x, g, w4, *deps)


def _proj_bwd(dproj, ht, w4):
    tn = 1024

    def body(dp_ref, ht_ref, w_ref, dh_ref, dw_ref):
        @pl.when(pl.program_id(0) == 0)
        def _():
            dh_ref[...] = jnp.zeros((L, D), F32)

        dw_ref[...] = _dot(ht_ref[...], dp_ref[...]).astype(BF16)
        dh_ref[...] += _dot_nt(dp_ref[...], w_ref[...])

    return _pcall(
        body, name="proj_bwd", grid=(NCOL // tn,),
        in_specs=[pl.BlockSpec((L, tn), lambda j: (0, j)),
                  pl.BlockSpec((D, L), lambda j: (0, 0)),
                  pl.BlockSpec((None, D, tn), lambda j: (lax.div(j, 2), 0, lax.rem(j, 2)))],
        out_specs=[pl.BlockSpec((L, D), lambda j: (0, 0)),
                   pl.BlockSpec((None, D, tn), lambda j: (lax.div(j, 2), 0, lax.rem(j, 2)))],
        out_shape=[_sds((L, D)), _sds((NCHIP, D, NCOL // NCHIP), BF16)], vmem_mb=56)(dproj, ht, w4)


def _rms_bwd(dh, x, g, dxn, dep=None):
    tl = 512
    deps, dep_specs = _token_operand(dep)

    def body(dh_ref, x_ref, g_ref, dxn_ref, *rest):
        dx_ref, dg_ref = rest[len(deps):]
        i = pl.program_id(0)
        xx = x_ref[...]
        inv = lax.rsqrt(jnp.mean(xx * xx, axis=-1, keepdims=True) + RMS_EPS)
        nrm = xx * inv
        dh_v = dh_ref[...]
        dn = dh_v * g_ref[...]
        dx_ref[...] = dxn_ref[...] + inv * (dn - nrm * jnp.mean(dn * nrm, axis=-1, keepdims=True))
        part = jnp.sum(dh_v * nrm, axis=0, keepdims=True)

        @pl.when(i == 0)
        def _():
            dg_ref[...] = part

        @pl.when(i > 0)
        def _():
            dg_ref[...] += part

    row = pl.BlockSpec((tl, D), lambda i: (i, 0))
    vec = pl.BlockSpec((1, D), lambda i: (0, 0))
    return _pcall(body, name="rms_bwd", grid=(L // tl,), in_specs=[row, row, vec, row] + dep_specs,
                  out_specs=[row, vec], out_shape=[_sds((L, D)), _sds((1, D))], vmem_mb=40)(dh, x, g, dxn, *deps)


def _s5_param_math(lre, lim, ldt, bre, bim):
    lr = jnp.minimum(lre, -1e-4)
    dt = jnp.exp(ldt)
    mag = jnp.exp(lr * dt)
    ar = mag * jnp.cos(lim * dt)
    ai = mag * jnp.sin(lim * dt)
    den = lr * lr + lim * lim
    nr = ar - 1.0
    qr = (nr * lr + ai * lim) / den
    qi = (ai * lr - nr * lim) / den
    return ar, ai, qr * bre - qi * bim, qr * bim + qi * bre


def _group_of_lane(width):
    lane = lax.broadcasted_iota(jnp.int32, (16, width), 1)
    return lax.shift_right_logical(jnp.bitwise_and(lane, HW - 1), 6)


def _vec_spec():
    return pl.BlockSpec((None, 1, HW), lambda n: (n, 0, 0))


def _mat_spec():
    return pl.BlockSpec((None, 16, HW), lambda n: (n, 0, 0))


def _s5_prep(lre, lim, ldt, bre, bim, cre, cim):
    n = lre.shape[0]

    def body(lre_ref, lim_ref, ldt_ref, bre_ref, bim_ref, cre_ref, cim_ref, ar_ref, ai_ref, wb_ref, wct_ref):
        ar, ai, bbr, bbi = _s5_param_math(lre_ref[...], lim_ref[...], ldt_ref[...], bre_ref[...], bim_ref[...])
        ar_ref[...] = ar
        ai_ref[...] = ai
        group = _group_of_lane(SW)
        bb = jnp.concatenate([bbr, bbi], axis=1)
        cc = jnp.concatenate([cre_ref[...], -cim_ref[...]], axis=1)
        for g in range(8):
            rows = pl.ds(16 * g, 16)
            wb_ref[rows, :] = jnp.where(group == g, bb, 0.0).astype(BF16)
            wct_ref[rows, :] = jnp.where(group == g, cc, 0.0).astype(BF16)

    wide = pl.BlockSpec((None, 128, SW), lambda n: (n, 0, 0))
    return _pcall(body, name="s5_prep", grid=(n,),
                  in_specs=[_vec_spec()] * 3 + [_mat_spec()] * 4,
                  out_specs=[_vec_spec(), _vec_spec(), wide, wide],
                  out_shape=[_sds((n, 1, HW)), _sds((n, 1, HW)), _sds((n, 128, SW), BF16), _sds((n, 128, SW), BF16)])(
                      lre, lim, ldt, bre, bim, cre, cim)


def _s5_prep_bwd(lre, lim, ldt, bre, bim, ga, gbb):
    n = lre.shape[0]

    def body(lre_ref, lim_ref, ldt_ref, bre_ref, bim_ref, ga_ref, gbb_ref,
             dlre_ref, dlim_ref, dldt_ref, dbre_ref, dbim_ref):
        _, vjp = jax.vjp(_s5_param_math, lre_ref[...], lim_ref[...], ldt_ref[...], bre_ref[...], bim_ref[...])
        ga, gbb = ga_ref[...], gbb_ref[...]
        dlre, dlim, dldt, dbre, dbim = vjp((ga[:, :HW], ga[:, HW:], gbb[:, :HW], gbb[:, HW:]))
        dlre_ref[...] = dlre
        dlim_ref[...] = dlim
        dldt_ref[...] = dldt
        dbre_ref[...] = dbre
        dbim_ref[...] = dbim

    return _pcall(body, name="s5_prep_bwd", grid=(n,),
                  in_specs=[_vec_spec()] * 3 + [_mat_spec()] * 2 +
                  [pl.BlockSpec((None, 1, SW), lambda n: (n, 0, 0)), pl.BlockSpec((None, 16, SW), lambda n: (n, 0, 0))],
                  out_specs=[_vec_spec()] * 3 + [_mat_spec()] * 2,
                  out_shape=[_sds((n, 1, HW))] * 3 + [_sds((n, 16, HW))] * 2)(lre, lim, ldt, bre, bim, ga, gbb)


def _diag_blocks(full):
    group = _group_of_lane(SW)
    out = jnp.where(group == 0, full[0:16, :], 0.0)
    for g in range(1, 8):
        out = out + jnp.where(group == g, full[16 * g:16 * g + 16, :], 0.0)
    return out


def _permute_rows(dst_ref, src_ref):
    for s in range(NSEG):
        dst_ref[pl.ds(s, SEG, stride=NSEG), :] = src_ref[pl.ds(s * SEG, SEG), :]


def _unpermute_rows(src_ref, s):
    return src_ref[pl.ds(s, SEG, stride=NSEG), :]


def _row_of(slab, rows, j):
    return jnp.sum(jnp.where(rows == j, slab, 0.0), axis=0, keepdims=True)


def _s5_scan_fwd(proj, wb4, wc4, ar4, ai4):
    def body(u_ref, wb_ref, wc_ref, ar_ref, ai_ref, x_ref, y_ref, up_ref, d_ref, yp_ref):
        _permute_rows(up_ref, u_ref)
        d_ref[...] = _dot(up_ref[...].astype(BF16), wb_ref[...])
        ar = jnp.broadcast_to(ar_ref[...], (NSEG, HW))
        ai = jnp.broadcast_to(ai_ref[...], (NSEG, HW))
        zero = jnp.zeros((NSEG, HW), F32)

        def drive(t):
            row = pl.multiple_of(t * NSEG, NSEG)
            dd = d_ref[pl.ds(row, NSEG), :]
            return row, dd[:, :HW], dd[:, HW:]

        def local_step(t, c):
            xr, xi = c
            _, dr, di = drive(t)
            return ar * xr - ai * xi + dr, ar * xi + ai * xr + di

        fr, fi = lax.fori_loop(0, SEG, local_step, (zero, zero), unroll=8)
        pr, pi_ = ar_ref[...], ai_ref[...]
        for _ in range(int(math.log2(SEG))):
            pr, pi_ = pr * pr - pi_ * pi_, 2.0 * pr * pi_
        rows = lax.broadcasted_iota(jnp.int32, (NSEG, HW), 0)
        cr, ci = zero, zero
        for j in range(NSEG - 1):
            sr, si = _row_of(cr, rows, j), _row_of(ci, rows, j)
            gr, gi = _row_of(fr, rows, j), _row_of(fi, rows, j)
            nr = pr * sr - pi_ * si + gr
            ni = pr * si + pi_ * sr + gi
            cr = jnp.where(rows == j + 1, nr, cr)
            ci = jnp.where(rows == j + 1, ni, ci)

        def true_step(t, c):
            xr, xi = c
            row, dr, di = drive(t)
            nr = ar * xr - ai * xi + dr
            ni = ar * xi + ai * xr + di
            x_ref[pl.ds(row, NSEG), :] = jnp.concatenate([nr, ni], axis=1)
            return nr, ni

        lax.fori_loop(0, SEG, true_step, (cr, ci), unroll=8)
        yp_ref[...] = _dot_nt(x_ref[...].astype(BF16), wc_ref[...])
        for s in range(NSEG):
            y_ref[pl.ds(s * SEG, SEG), :] = _unpermute_rows(yp_ref, s)

    return _pcall(
        body, name="s5_scan_fwd", grid=(NCB,),
        in_specs=[pl.BlockSpec((L, 128), lambda b: (0, COL_U + b)),
                  pl.BlockSpec((None, 128, SW), lambda b: (b, 0, 0)),
                  pl.BlockSpec((None, 128, SW), lambda b: (b, 0, 0)),
                  pl.BlockSpec((None, 1, HW), lambda b: (b, 0, 0)),
                  pl.BlockSpec((None, 1, HW), lambda b: (b, 0, 0))],
        out_specs=[pl.BlockSpec((None, L, SW), lambda b: (b, 0, 0)),
                   pl.BlockSpec((L, 128), lambda b: (0, b))],
        out_shape=[_sds((NCB, L, SW)), _sds((L, SSM))],
        scratch=[pltpu.VMEM((L, 128), F32), pltpu.VMEM((L, SW), F32), pltpu.VMEM((L, 128), F32)],
        vmem_mb=56)(proj, wb4, wc4, ar4, ai4)


def _s5_scan_bwd(dy, xs, proj, du_skip, wb4, wc4, ar4, ai4, dproj):
    def body(dy_ref, x_ref, u_ref, dus_ref, wb_ref, wc_ref, ar_ref, ai_ref, _,
             du_ref, dbb_ref, dcc_ref, da_ref, dyp_ref, up_ref, g_ref, dup_ref):
        _permute_rows(dyp_ref, dy_ref)
        _permute_rows(up_ref, u_ref)
        dyp = dyp_ref[...].astype(BF16)
        g_ref[...] = _dot(dyp, wc_ref[...])
        dcc = _diag_blocks(_dot_tn(dyp, x_ref[...].astype(BF16)))
        dcc_ref[...] = jnp.concatenate([dcc[:, :HW], -dcc[:, HW:]], axis=1)
        ar = jnp.broadcast_to(ar_ref[...], (NSEG, HW))
        ai = jnp.broadcast_to(ai_ref[...], (NSEG, HW))
        zero = jnp.zeros((NSEG, HW), F32)

        def load(t):
            row = pl.multiple_of(t * NSEG, NSEG)
            gg = g_ref[pl.ds(row, NSEG), :]
            return row, gg[:, :HW], gg[:, HW:]

        def local_step(k, c):
            lr, li = c
            _, gr, gi = load(SEG - 1 - k)
            return ar * lr + ai * li + gr, ar * li - ai * lr + gi

        fr, fi = lax.fori_loop(0, SEG, local_step, (zero, zero), unroll=8)
        pr, pi_ = ar_ref[...], -ai_ref[...]
        for _ in range(int(math.log2(SEG))):
            pr, pi_ = pr * pr - pi_ * pi_, 2.0 * pr * pi_
        rows = lax.broadcasted_iota(jnp.int32, (NSEG, HW), 0)
        cr, ci = zero, zero
        for j in range(NSEG - 1, 0, -1):
            sr, si = _row_of(cr, rows, j), _row_of(ci, rows, j)
            gr, gi = _row_of(fr, rows, j), _row_of(fi, rows, j)
            nr = pr * sr - pi_ * si + gr
            ni = pr * si + pi_ * sr + gi
            cr = jnp.where(rows == j - 1, nr, cr)
            ci = jnp.where(rows == j - 1, ni, ci)

        def true_step(k, c):
            lr, li, gar, gai = c
            t = SEG - 1 - k
            row, gr, gi = load(t)
            nr = ar * lr + ai * li + gr
            ni = ar * li - ai * lr + gi
            g_ref[pl.ds(row, NSEG), :] = jnp.concatenate([nr, ni], axis=1)
            prow = pl.multiple_of(jnp.maximum(t - 1, 0) * NSEG, NSEG)
            xp = x_ref[pl.ds(prow, NSEG), :]
            live = (t > 0).astype(F32)
            xr, xi = xp[:, :HW] * live, xp[:, HW:] * live
            return nr, ni, gar + (nr * xr + ni * xi), gai + (ni * xr - nr * xi)

        _, _, gar, gai = lax.fori_loop(0, SEG, true_step, (cr, ci, zero, zero), unroll=4)
        l0 = g_ref[1:NSEG, :]
        xl = x_ref[(SEG - 1) * NSEG:(SEG - 1) * NSEG + NSEG - 1, :]
        l0r, l0i, xlr, xli = l0[:, :HW], l0[:, HW:], xl[:, :HW], xl[:, HW:]
        gar_t = jnp.sum(gar, axis=0, keepdims=True) + jnp.sum(l0r * xlr + l0i * xli, axis=0, keepdims=True)
        gai_t = jnp.sum(gai, axis=0, keepdims=True) + jnp.sum(l0i * xlr - l0r * xli, axis=0, keepdims=True)
        da_ref[...] = jnp.concatenate([gar_t, gai_t], axis=1)
        lam = g_ref[...].astype(BF16)
        dbb_ref[...] = _diag_blocks(_dot_tn(up_ref[...].astype(BF16), lam))
        dup_ref[...] = _dot_nt(lam, wb_ref[...])
        for s in range(NSEG):
            sl = pl.ds(s * SEG, SEG)
            du_ref[sl, :] = (_unpermute_rows(dup_ref, s) + dus_ref[sl, :]).astype(BF16)

    col = lambda off: pl.BlockSpec((L, 128), lambda b: (0, off + b))
    return _pcall(
        body, name="s5_scan_bwd", grid=(NCB,),
        in_specs=[col(0), pl.BlockSpec((None, L, SW), lambda b: (b, 0, 0)), col(COL_U), col(0),
                  pl.BlockSpec((None, 128, SW), lambda b: (b, 0, 0)),
                  pl.BlockSpec((None, 128, SW), lambda b: (b, 0, 0)),
                  pl.BlockSpec((None, 1, HW), lambda b: (b, 0, 0)),
                  pl.BlockSpec((None, 1, HW), lambda b: (b, 0, 0)),
                  pl.BlockSpec(memory_space=pl.ANY)],
        out_specs=[col(COL_U), pl.BlockSpec((None, 16, SW), lambda b: (b, 0, 0)),
                   pl.BlockSpec((None, 16, SW), lambda b: (b, 0, 0)),
                   pl.BlockSpec((None, 1, SW), lambda b: (b, 0, 0))],
        out_shape=[_sds((L, NCOL), BF16), _sds((NCB, 16, SW)), _sds((NCB, 16, SW)), _sds((NCB, 1, SW))],
        scratch=[pltpu.VMEM((L, 128), F32), pltpu.VMEM((L, 128), F32), pltpu.VMEM((L, SW), F32),
                 pltpu.VMEM((L, 128), F32)],
        aliases={8: 0}, vmem_mb=56)(dy, xs, proj, du_skip, wb4, wc4, ar4, ai4, dproj)


def _s5_tail_fwd(yraw, proj, dsk, wglu, bglu):
    tl = 512

    def body(y_ref, u_ref, z_ref, dsk_ref, w_ref, b_ref, o_ref):
        y1 = y_ref[...] + dsk_ref[...] * u_ref[...]
        y2 = _gelu(y1)
        gl = _dot(y2.astype(BF16), w_ref[...]) + b_ref[...]
        z = z_ref[...]
        o_ref[...] = (y2 * _sigmoid(gl)) * (z * _sigmoid(z))

    blk = lambda c: pl.BlockSpec((tl, SSM), lambda i: (i, c))
    vec = pl.BlockSpec((1, SSM), lambda i: (0, 0))
    return _pcall(body, name="s5_tail_fwd", grid=(L // tl,),
                  in_specs=[blk(0), blk(0), blk(1), vec, pl.BlockSpec((SSM, SSM), lambda i: (0, 0)), vec],
                  out_specs=blk(0), out_shape=_sds((L, SSM)), vmem_mb=40)(yraw, proj, proj, dsk, wglu, bglu)


def _s5_tail_bwd(dys, yraw, proj, dsk, wglu, bglu, dproj):
    tl = 512
    ni = L // tl

    def body(dys_ref, y_ref, u_ref, z_ref, dsk_ref, w_ref, b_ref, _,
             dy_ref, dus_ref, dz_ref, dw_ref, db_ref, dd_ref, acc_ref):
        i = pl.program_id(0)
        u = u_ref[...]
        y1 = y_ref[...] + dsk_ref[...] * u
        y2 = _gelu(y1)
        y2b = y2.astype(BF16)
        sg = _sigmoid(_dot(y2b, w_ref[...]) + b_ref[...])
        y3 = y2 * sg
        z = z_ref[...]
        sz = _sigmoid(z)
        dys = dys_ref[...]
        dy3 = dys * (z * sz)
        dz_ref[...] = (dys * y3 * (sz * (1.0 + z * (1.0 - sz)))).astype(BF16)
        dgl = (dy3 * y2) * (sg * (1.0 - sg))
        dglb = dgl.astype(BF16)
        dy2 = dy3 * sg + _dot_nt(dglb, w_ref[...])
        dy1 = dy2 * _gelu_grad(y1)
        dy_ref[...] = dy1
        dus_ref[...] = dsk_ref[...] * dy1
        dw = _dot_tn(y2b, dglb)
        db = jnp.sum(dgl, axis=0, keepdims=True)
        dd = jnp.sum(dy1 * u, axis=0, keepdims=True)

        @pl.when(i == 0)
        def _():
            acc_ref[...] = dw
            db_ref[...] = db
            dd_ref[...] = dd

        @pl.when(i > 0)
        def _():
            acc_ref[...] += dw
            db_ref[...] += db
            dd_ref[...] += dd

        @pl.when(i == ni - 1)
        def _():
            dw_ref[...] = acc_ref[...].astype(BF16)

    blk = lambda c: pl.BlockSpec((tl, SSM), lambda i: (i, c))
    vec = pl.BlockSpec((1, SSM), lambda i: (0, 0))
    mat = pl.BlockSpec((SSM, SSM), lambda i: (0, 0))
    return _pcall(body, name="s5_tail_bwd", grid=(ni,),
                  in_specs=[blk(0), blk(0), blk(0), blk(1), vec, mat, vec, pl.BlockSpec(memory_space=pl.ANY)],
                  out_specs=[blk(0), blk(0), blk(COL_ZS // 4), mat, vec, vec],
                  out_shape=[_sds((L, SSM)), _sds((L, SSM)), _sds((L, NCOL), BF16), _sds((SSM, SSM), BF16),
                             _sds((1, SSM)), _sds((1, SSM))],
                  scratch=[pltpu.VMEM((SSM, SSM), F32)], aliases={7: 2},
                  vmem_mb=40)(dys, yraw, proj, proj, dsk, wglu, bglu, dproj)


def _attn_blocks(dil):
    nb = L // dil // HEAD
    return [(i * HEAD * dil + r, (i - 1) * HEAD * dil + r if i > 0 else None)
            for r in range(dil) for i in range(nb)]


def _rows(start, dil):
    return pl.ds(start, HEAD) if dil == 1 else pl.ds(start, HEAD, stride=dil)


def _band_masks():
    ri = lax.broadcasted_iota(jnp.int32, (HEAD, HEAD), 0)
    ci = lax.broadcasted_iota(jnp.int32, (HEAD, HEAD), 1)
    return ci <= ri, ci >= ri


def _qkv_specs(index):
    def spec(off):
        return pl.BlockSpec((L, HEAD), lambda gi, h: (0, off + index(gi, h)))
    return [spec(off) for off in (COL_Q, COL_K, COL_V)]


def _attn_core_fwd(proj):
    def body(q_ref, k_ref, v_ref, o_ref, lse_ref):
        gi = pl.program_id(0)
        m_cur, m_prev = _band_masks()
        for g, dil in enumerate(DILATIONS):
            @pl.when(gi == g)
            def _(dil=dil):
                for st, pst in _attn_blocks(dil):
                    r = _rows(st, dil)
                    q = q_ref[r, :].astype(BF16)
                    s_c = jnp.where(m_cur, _dot_nt(q, k_ref[r, :].astype(BF16)) * SCALE, NEG)
                    mx = jnp.max(s_c, axis=-1, keepdims=True)
                    if pst is not None:
                        rp = _rows(pst, dil)
                        s_p = jnp.where(m_prev, _dot_nt(q, k_ref[rp, :].astype(BF16)) * SCALE, NEG)
                        mx = jnp.maximum(mx, jnp.max(s_p, axis=-1, keepdims=True))
                    p_c = jnp.exp(s_c - mx)
                    den = jnp.sum(p_c, axis=-1, keepdims=True)
                    acc = _dot(p_c.astype(BF16), v_ref[r, :].astype(BF16))
                    if pst is not None:
                        p_p = jnp.exp(s_p - mx)
                        den = den + jnp.sum(p_p, axis=-1, keepdims=True)
                        acc = acc + _dot(p_p.astype(BF16), v_ref[rp, :].astype(BF16))
                    o_ref[r, :] = acc / den
                    lse_ref[r, :] = jnp.broadcast_to(mx + jnp.log(den), (HEAD, HEAD))

    idx = lambda gi, h: gi * 4 + h
    out = pl.BlockSpec((L, HEAD), lambda gi, h: (0, gi * 4 + h))
    return _pcall(body, name="attn_core_fwd", grid=(3, 4), in_specs=_qkv_specs(idx), out_specs=[out, out],
                  out_shape=[_sds((L, QKV)), _sds((L, QKV))], vmem_mb=40)(proj, proj, proj)


def _attn_mix_fwd(o, lse, proj):
    tl = 512

    def body(o_ref, l_ref, z_ref, y_ref):
        for h in range(4):
            c = [pl.ds((g * 4 + h) * HEAD, HEAD) for g in range(3)]
            ls = [l_ref[:, c[g]] for g in range(3)]
            m = jnp.maximum(jnp.maximum(ls[0], ls[1]), ls[2])
            e = [jnp.exp(ls[g] - m) for g in range(3)]
            y = (e[0] * o_ref[:, c[0]] + e[1] * o_ref[:, c[1]] + e[2] * o_ref[:, c[2]]) / (e[0] + e[1] + e[2])
            z = z_ref[:, pl.ds(h * HEAD, HEAD)]
            y_ref[:, pl.ds(h * HEAD, HEAD)] = y * (z * _sigmoid(z))

    wide = pl.BlockSpec((tl, QKV), lambda i: (i, 0))
    return _pcall(body, name="attn_mix_fwd", grid=(L // tl,),
                  in_specs=[wide, wide, pl.BlockSpec((tl, AW), lambda i: (i, COL_ZA // 4))],
                  out_specs=pl.BlockSpec((tl, AW), lambda i: (i, 0)), out_shape=_sds((L, AW)),
                  vmem_mb=40)(o, lse, proj)


def _attn_mix_bwd(dya, o, lse, proj, dproj):
    tl = 512

    def body(dya_ref, o_ref, l_ref, z_ref, _, do_ref, c_ref, dz_ref):
        for h in range(4):
            c = [pl.ds((g * 4 + h) * HEAD, HEAD) for g in range(3)]
            hs = pl.ds(h * HEAD, HEAD)
            ls = [l_ref[:, c[g]] for g in range(3)]
            m = jnp.maximum(jnp.maximum(ls[0], ls[1]), ls[2])
            e = [jnp.exp(ls[g] - m) for g in range(3)]
            den = e[0] + e[1] + e[2]
            al = [e[g] / den for g in range(3)]
            y = al[0] * o_ref[:, c[0]] + al[1] * o_ref[:, c[1]] + al[2] * o_ref[:, c[2]]
            z = z_ref[:, hs]
            sz = _sigmoid(z)
            dya = dya_ref[:, hs]
            dz_ref[:, hs] = (dya * y * (sz * (1.0 + z * (1.0 - sz)))).astype(BF16)
            dy = dya * (z * sz)
            tot = jnp.sum(dy * y, axis=-1, keepdims=True)
            for g in range(3):
                do_ref[:, c[g]] = al[g] * dy
                c_ref[:, c[g]] = -(al[g] * tot)

    wide = pl.BlockSpec((tl, QKV), lambda i: (i, 0))
    nar = pl.BlockSpec((tl, AW), lambda i: (i, 0))
    za = pl.BlockSpec((tl, AW), lambda i: (i, COL_ZA // 4))
    return _pcall(body, name="attn_mix_bwd", grid=(L // tl,),
                  in_specs=[nar, wide, wide, za, pl.BlockSpec(memory_space=pl.ANY)],
                  out_specs=[wide, wide, za], out_shape=[_sds((L, QKV)), _sds((L, QKV)), _sds((L, NCOL), BF16)],
                  aliases={4: 2}, vmem_mb=48)(dya, o, lse, proj, dproj)


def _attn_core_bwd(proj, do, lse, cc, dproj):
    def body(q_ref, k_ref, v_ref, do_ref, lse_ref, c_ref, _, dp_ref, dq_ref, dk_ref, dv_ref, out_ref, sems):
        gi = pl.program_id(0)
        head = gi * 4 + pl.program_id(1)
        m_cur, m_prev = _band_masks()
        dk_ref[...] = jnp.zeros((L, HEAD), F32)
        dv_ref[...] = jnp.zeros((L, HEAD), F32)
        for g, dil in enumerate(DILATIONS):
            @pl.when(gi == g)
            def _(dil=dil):
                for st, pst in _attn_blocks(dil):
                    r = _rows(st, dil)
                    q = q_ref[r, :].astype(BF16)
                    do_b = do_ref[r, :].astype(BF16)
                    lse_b = lse_ref[r, :]
                    c_b = c_ref[r, :]
                    dq = None
                    for rr, mask in ((r, m_cur),) + (((_rows(pst, dil), m_prev),) if pst is not None else ()):
                        kb = k_ref[rr, :].astype(BF16)
                        s = _dot_nt(q, kb) * SCALE
                        p = jnp.where(mask, jnp.exp(s - lse_b), 0.0)
                        dp = _dot_nt(do_b, v_ref[rr, :].astype(BF16))
                        ds = (p * (dp + c_b) * SCALE).astype(BF16)
                        part = _dot(ds, kb)
                        dq = part if dq is None else dq + part
                        dk_ref[rr, :] += _dot_tn(ds, q)
                        dv_ref[rr, :] += _dot_tn(p.astype(BF16), do_b)
                    dq_ref[r, :] = dq

        copies = []
        for n, (src, off) in enumerate(((dq_ref, COL_Q), (dk_ref, COL_K), (dv_ref, COL_V))):
            out_ref[n] = src[...].astype(BF16)
            cols = pl.ds(pl.multiple_of((off + head) * HEAD, HEAD), HEAD)
            copies.append(pltpu.make_async_copy(out_ref.at[n], dp_ref.at[:, cols], sems.at[n]))
            copies[-1].start()
        for cp in copies:
            cp.wait()

    idx = lambda gi, h: gi * 4 + h
    blk = pl.BlockSpec((L, HEAD), lambda gi, h: (0, gi * 4 + h))
    hbm = pl.BlockSpec(memory_space=pl.ANY)
    return _pcall(body, name="attn_core_bwd", grid=(3, 4), in_specs=_qkv_specs(idx) + [blk, blk, blk, hbm],
                  out_specs=hbm, out_shape=_sds((L, NCOL), BF16),
                  scratch=[pltpu.VMEM((L, HEAD), F32)] * 3 + [pltpu.VMEM((3, L, HEAD), BF16),
                                                              pltpu.SemaphoreType.DMA((3,))],
                  aliases={6: 0}, vmem_mb=48)(proj, proj, proj, do, lse, cc, dproj)


def _merge_fwd_math(ys_b, ya_b, gs, ga, wbs_ref, wba_ref, wout_ref):
    bs = jnp.concatenate([_dot(ys_b, wbs_ref[s]) for s in range(NCHIP)], axis=1)
    ba = jnp.concatenate([_dot(ya_b, wba_ref[s]) for s in range(NCHIP)], axis=1)
    sgs, sga = _sigmoid(gs), _sigmoid(ga)
    merged = sgs * bs + sga * ba
    out = _dot(merged.astype(BF16), wout_ref[...])
    inv = lax.rsqrt(jnp.mean(out * out, axis=-1, keepdims=True) + RMS_EPS)
    return bs, ba, sgs, sga, merged, out, inv


def _merge_specs(tl):
    row = lambda w, c: pl.BlockSpec((tl, w), lambda i: (i, c))
    return [row(SSM, 0), row(AW, 0), row(D, COL_GS // 8), row(D, COL_GA // 8),
            pl.BlockSpec((NCHIP, SSM, D // NCHIP), lambda i: (0, 0, 0)),
            pl.BlockSpec((NCHIP, AW, D // NCHIP), lambda i: (0, 0, 0)),
            pl.BlockSpec((D, D), lambda i: (0, 0)),
            pl.BlockSpec((1, D), lambda i: (0, 0))]


def _merge_out_fwd(ys, ya, proj, wbs4, wba4, wout, g2, x, dep=None):
    tl = 256
    deps, dep_specs = _token_operand(dep)

    def body(ys_ref, ya_ref, gs_ref, ga_ref, wbs_ref, wba_ref, wout_ref, g2_ref, x_ref, *rest):
        o_ref = rest[-1]
        *_, out, inv = _merge_fwd_math(ys_ref[...].astype(BF16), ya_ref[...].astype(BF16), gs_ref[...],
                                       ga_ref[...], wbs_ref, wba_ref, wout_ref)
        o_ref[...] = x_ref[...] + out * inv * g2_ref[...]

    row = pl.BlockSpec((tl, D), lambda i: (i, 0))
    return _pcall(body, name="merge_out_fwd", grid=(L // tl,), in_specs=_merge_specs(tl) + [row] + dep_specs,
                  out_specs=row, out_shape=_sds((L, D)),
                  vmem_mb=48)(ys, ya, proj, proj, wbs4, wba4, wout, g2, x, *deps)


def _merge_out_bwd(dxn, ys, ya, proj, wbs4, wba4, wout, g2, dep=None):
    tl = 256
    ni = L // tl
    cw = D // NCHIP
    deps, dep_specs = _token_operand(dep)

    def body(dxn_ref, ys_ref, ya_ref, gs_ref, ga_ref, wbs_ref, wba_ref, wout_ref, g2_ref, *rest):
        (dys_ref, dya_ref, dgate_ref, dwbs_ref, dwba_ref, dwout_ref, dg2_ref,
         abs_ref, aba_ref, aout_ref) = rest[len(deps):]
        i = pl.program_id(0)

        @pl.when(i == 0)
        def _():
            abs_ref[...] = jnp.zeros(abs_ref.shape, F32)
            aba_ref[...] = jnp.zeros(aba_ref.shape, F32)
            aout_ref[...] = jnp.zeros(aout_ref.shape, F32)
            dg2_ref[...] = jnp.zeros(dg2_ref.shape, F32)

        ys_b, ya_b = ys_ref[...].astype(BF16), ya_ref[...].astype(BF16)
        bs, ba, sgs, sga, merged, out, inv = _merge_fwd_math(ys_b, ya_b, gs_ref[...], ga_ref[...],
                                                             wbs_ref, wba_ref, wout_ref)
        nrm = out * inv
        dxn = dxn_ref[...]
        dg2_ref[...] += jnp.sum(dxn * nrm, axis=0, keepdims=True)
        dn = dxn * g2_ref[...]
        dout = (inv * (dn - nrm * jnp.mean(dn * nrm, axis=-1, keepdims=True))).astype(BF16)
        aout_ref[...] += _dot_tn(merged.astype(BF16), dout)
        dm = _dot_nt(dout, wout_ref[...])
        dbs, dba = dm * sgs, dm * sga
        dgate_ref[:, :D] = (dm * bs * (sgs * (1.0 - sgs))).astype(BF16)
        dgate_ref[:, D:] = (dm * ba * (sga * (1.0 - sga))).astype(BF16)
        dbs_b, dba_b = dbs.astype(BF16), dba.astype(BF16)
        dys = None
        dya = None
        for s in range(NCHIP):
            cs = slice(s * cw, (s + 1) * cw)
            p_s = _dot_nt(dbs_b[:, cs], wbs_ref[s])
            p_a = _dot_nt(dba_b[:, cs], wba_ref[s])
            dys = p_s if dys is None else dys + p_s
            dya = p_a if dya is None else dya + p_a
            abs_ref[s] += _dot_tn(ys_b, dbs_b[:, cs])
            aba_ref[s] += _dot_tn(ya_b, dba_b[:, cs])

        dys_ref[...] = dys
        dya_ref[...] = dya

        @pl.when(i == ni - 1)
        def _():
            dwbs_ref[...] = abs_ref[...].astype(BF16)
            dwba_ref[...] = aba_ref[...].astype(BF16)
            dwout_ref[...] = aout_ref[...].astype(BF16)

    row = lambda w: pl.BlockSpec((tl, w), lambda i: (i, 0))
    w4 = pl.BlockSpec((NCHIP, SSM, cw), lambda i: (0, 0, 0))
    sq = pl.BlockSpec((D, D), lambda i: (0, 0))
    vec = pl.BlockSpec((1, D), lambda i: (0, 0))
    return _pcall(body, name="merge_out_bwd", grid=(ni,), in_specs=[row(D)] + _merge_specs(tl) + dep_specs,
                  out_specs=[row(SSM), row(AW), pl.BlockSpec((tl, 2 * D), lambda i: (i, COL_GS // 16)), w4, w4, sq,
                             vec],
                  out_shape=[_sds((L, SSM)), _sds((L, AW)), _sds((L, NCOL), BF16),
                             _sds((NCHIP, SSM, cw), BF16), _sds((NCHIP, AW, cw), BF16), _sds((D, D), BF16),
                             _sds((1, D))],
                  scratch=[pltpu.VMEM((NCHIP, SSM, cw), F32), pltpu.VMEM((NCHIP, AW, cw), F32),
                           pltpu.VMEM((D, D), F32)],
                  vmem_mb=56)(dxn, ys, ya, proj, proj, wbs4, wba4, wout, g2, *deps)


def _loss_head(y, target):
    tl = 512

    def body(y_ref, t_ref, loss_ref, dy_ref):
        i = pl.program_id(0)
        err = y_ref[...] - t_ref[...]
        dy_ref[...] = err / D
        part = 0.5 * jnp.sum(jnp.mean(err * err, axis=-1, keepdims=True), axis=0, keepdims=True)
        part = jnp.broadcast_to(part, (8, 128))

        @pl.when(i == 0)
        def _():
            loss_ref[...] = part

        @pl.when(i > 0)
        def _():
            loss_ref[...] += part

    row = pl.BlockSpec((tl, D), lambda i: (i, 0))
    return _pcall(body, name="loss_head", grid=(L // tl,), in_specs=[row, row],
                  out_specs=[pl.BlockSpec((8, 128), lambda i: (0, 0)), row],
                  out_shape=[_sds((8, 128)), _sds((L, D))], vmem_mb=40)(y, target)


def _adamw_math(w, g, m, v):
    m = ADAM_B1 * m + (1.0 - ADAM_B1) * g
    v = ADAM_B2 * v + (1.0 - ADAM_B2) * (g * g)
    m_hat = m / (1.0 - ADAM_B1 ** ADAM_STEP)
    v_hat = v / (1.0 - ADAM_B2 ** ADAM_STEP)
    delta = -ADAM_LR * (m_hat / (jnp.sqrt(v_hat) + ADAM_EPS) + ADAM_WD * w)
    return delta, m, v


def _adamw_big(layer, w, m, v, own, sib, prev):
    _, r, c = w.shape
    tr = min(r, 128)

    def body(w_ref, m_ref, v_ref, own_ref, sib_ref, *rest):
        g_ref, d_ref, nm_ref, nv_ref = rest[-4:]
        a = own_ref[0].astype(F32)
        b = sib_ref[0].astype(F32)
        for s in range(1, NCHIP):
            a = a + own_ref[s].astype(F32)
            b = b + sib_ref[s].astype(F32)
        g = a + b
        delta, nm, nv = _adamw_math(w_ref[...], g, m_ref[...], v_ref[...])
        g_ref[...] = g
        d_ref[...] = delta
        nm_ref[...] = nm
        nv_ref[...] = nv

    lay = pl.BlockSpec((None, tr, c), lambda i: (layer, i, 0))
    slots = pl.BlockSpec((NCHIP, tr, c), lambda i: (0, i, 0))
    ins = [w, m, v, own, sib]
    in_specs = [lay, lay, lay, slots, slots]
    aliases = {}
    if prev is not None:
        ins += list(prev)
        in_specs += [pl.BlockSpec(memory_space=pl.ANY)] * 4
        aliases = {5 + k: k for k in range(4)}
    return _pcall(body, name="adamw_big", grid=(r // tr,), in_specs=in_specs, out_specs=[lay] * 4,
                  out_shape=[_sds(w.shape)] * 4, aliases=aliases, vmem_mb=48)(*ins)


SMALL_TILE = 512


def _adamw_small(w, parts, m, v):
    r = w.shape[0]

    def body(w_ref, p_ref, m_ref, v_ref, g_ref, d_ref, nm_ref, nv_ref):
        g = p_ref[0].astype(F32)
        for dev in range(1, 8):
            g = g + p_ref[dev].astype(F32)
        delta, nm, nv = _adamw_math(w_ref[...], g, m_ref[...], v_ref[...])
        g_ref[...] = g
        d_ref[...] = delta
        nm_ref[...] = nm
        nv_ref[...] = nv

    row = pl.BlockSpec((SMALL_TILE, 128), lambda i: (i, 0))
    return _pcall(body, name="adamw_small", grid=(r // SMALL_TILE,),
                  in_specs=[row, pl.BlockSpec((8, SMALL_TILE, 128), lambda i: (0, i, 0)), row, row],
                  out_specs=[row] * 4, out_shape=[_sds((r, 128))] * 4, vmem_mb=40)(w, parts, m, v)


def _place():
    x, y, c = lax.axis_index("x"), lax.axis_index("y"), lax.axis_index("c")
    return x, y, c, 2 * x + y


def _chip_peer(x, y, j):
    return (1 - x if j & 2 else x), (1 - y if j & 1 else y)


_HBM = pl.BlockSpec(memory_space=pltpu.HBM)
_SEM = pl.BlockSpec(memory_space=pltpu.SEMAPHORE)
_EFFECT = pltpu.SideEffectType.DATAFLOW_SIDE_EFFECTING


def _in_hbm(a):
    return pltpu.with_memory_space_constraint(a, pltpu.HBM)


def _copies_start(name, plan, ncopy, srcs, lands, dep=None):
    n = len(srcs) + len(lands)
    deps, dep_specs = _token_operand(dep)

    def body(*refs):
        send_sems, recv_sems = refs[n + len(deps)], refs[n + len(deps) + 1]
        token = refs[-1]
        for k, (src, dst, dev) in enumerate(plan(refs[:len(srcs)], refs[len(srcs):n])):
            pltpu.make_async_remote_copy(src_ref=src, dst_ref=dst, send_sem=send_sems.at[k],
                                         recv_sem=recv_sems.at[k], device_id=dev, device_id_type=MESH).start()
        token[...] = jnp.zeros_like(token)

    bufs = list(srcs) + list(lands)
    outs = pl.pallas_call(
        body, name=name,
        out_shape=(pltpu.SemaphoreType.DMA((ncopy,)), pltpu.SemaphoreType.DMA((ncopy,)),
                   *[pltpu.HBM(a.shape, a.dtype) for a in bufs], _sds((8, 128))),
        in_specs=[_HBM] * n + dep_specs,
        out_specs=(_SEM, _SEM, *[_HBM] * n, pl.BlockSpec(memory_space=pltpu.VMEM)),
        input_output_aliases={i: 2 + i for i in range(n)},
        compiler_params=pltpu.CompilerParams(has_side_effects=_EFFECT),
    )(*[_in_hbm(a) for a in bufs], *deps)
    return outs[0], outs[1], list(outs[2:2 + len(srcs)]), list(outs[2 + len(srcs):2 + n]), outs[-1]


def _copies_wait(name, plan, send_sems, recv_sems, srcs, lands, after):
    n = len(srcs) + len(lands)
    after = list(after)

    def body(*refs):
        s_sems, r_sems = refs[n], refs[n + 1]
        for k, (src, dst, dev) in enumerate(plan(refs[:len(srcs)], refs[len(srcs):n])):
            cp = pltpu.make_async_remote_copy(src_ref=src, dst_ref=dst, send_sem=s_sems.at[k],
                                              recv_sem=r_sems.at[k], device_id=dev, device_id_type=MESH)
            cp.wait_send()
            cp.wait_recv()

    bufs = list(srcs) + list(lands)
    outs = pl.pallas_call(
        body, name=name,
        out_shape=tuple(pltpu.HBM(a.shape, a.dtype) for a in bufs),
        in_specs=[_HBM] * n + [_SEM, _SEM] + [pl.BlockSpec(memory_space=pl.ANY)] * len(after),
        out_specs=tuple([_HBM] * n),
        input_output_aliases={i: i for i in range(n)},
        compiler_params=pltpu.CompilerParams(has_side_effects=_EFFECT),
    )(*bufs, send_sems, recv_sems, *after)
    return list(outs[:len(srcs)]), list(outs[len(srcs):])


def _with_own_slot(block, slot, nslots=NCHIP):
    land = lax.empty((nslots,) + block.shape, block.dtype)
    return lax.dynamic_update_slice(land, block[None], (slot,) + (0,) * block.ndim)


def _my_half(land, slot, c):
    half = land.shape[1] // 2
    return land.at[slot, pl.ds(pl.multiple_of(c * half, 16), half)]


def _gather_ici_plan(srcs, lands):
    x, y, c, s = _place()
    return [(_my_half(land, s, c), _my_half(land, s, c), (*_chip_peer(x, y, j), c))
            for land in lands for j in range(1, NCHIP)]


def _gather_d2d_plan(srcs, lands):
    x, y, c, s = _place()
    return [(_my_half(land, s ^ j, c), _my_half(land, s ^ j, c), (x, y, 1 - c))
            for land in lands for j in range(1, NCHIP)]


def _exchange_plan(srcs, lands):
    x, y, c, s = _place()
    return [(src.at[s ^ j], land.at[s], (*_chip_peer(x, y, j), c))
            for src, land in zip(srcs, lands) for j in range(1, NCHIP)]


def _sibling_plan(srcs, lands):
    x, y, c, _ = _place()
    return [(src, land, (x, y, 1 - c)) for src, land in zip(srcs, lands)]


def _everyone_plan(srcs, lands):
    x, y, c, _ = _place()
    me = 4 * x + 2 * y + c
    return [(srcs[0], lands[0].at[me], (*_chip_peer(x, y, j >> 1), 1 - c if j & 1 else c)) for j in range(1, 8)]


def _flatten_small(parts):
    flat = jnp.concatenate([p.reshape(-1) for p in parts])
    n = flat.shape[0]
    rows = -(-n // (128 * SMALL_TILE)) * SMALL_TILE
    return jnp.pad(flat, (0, rows * 128 - n)).reshape(rows, 128)


def _unflatten_small(buf, like):
    flat = buf.reshape(-1)
    out, at = [], 0
    for p in like:
        out.append(flat[at:at + p.size].reshape(p.shape))
        at += p.size
    return out


def kernel(x, pre_norm_g, w_in, lambda_re, lambda_im, log_dt, b_re, b_im, c_re, c_im, d_skip, w_glu, b_glu, w_branch_s, w_branch_a, w_out, post_norm_g, loss_target, m_pre_norm_g, m_w_in, m_lambda_re, m_lambda_im, m_log_dt, m_b_re, m_b_im, m_c_re, m_c_im, m_d_skip, m_w_glu, m_b_glu, m_w_branch_s, m_w_branch_a, m_w_out, m_post_norm_g, v_pre_norm_g, v_w_in, v_lambda_re, v_lambda_im, v_log_dt, v_b_re, v_b_im, v_c_re, v_c_im, v_d_skip, v_w_glu, v_b_glu, v_w_branch_s, v_w_branch_a, v_w_out, v_post_norm_g):
    nb = DEPTH * NCB
    lre_c = lambda_re.reshape(nb, 1, HW)
    lim_c = lambda_im.reshape(nb, 1, HW)
    ldt_c = jnp.broadcast_to(log_dt[:, :, None], (DEPTH, 32, 64)).reshape(nb, 1, HW)
    b_rows = lambda t: t.reshape(DEPTH, NCB, 8, 64, 16).transpose(0, 1, 4, 2, 3).reshape(nb, 16, HW)
    c_rows = lambda t: t.reshape(DEPTH, NCB, 8, 16, 64).transpose(0, 1, 3, 2, 4).reshape(nb, 16, HW)
    bre_c, bim_c = b_rows(b_re), b_rows(b_im)
    ar, ai, wb, wct = _s5_prep(lre_c, lim_c, ldt_c, bre_c, bim_c, c_rows(c_re), c_rows(c_im))
    ar4 = ar.reshape(DEPTH, NCB, 1, HW)
    ai4 = ai.reshape(DEPTH, NCB, 1, HW)
    wb = wb.reshape(DEPTH, NCB, 128, SW)
    wct = wct.reshape(DEPTH, NCB, 128, SW)

    chip = 2 * lax.axis_index("x") + lax.axis_index("y")
    device = 2 * chip + lax.axis_index("c")

    def gather_start(l, dep):
        blocks = [w_in[l], w_glu[l], w_branch_s[l], w_branch_a[l], w_out[l]]
        lands = [_with_own_slot(a.astype(BF16), chip) for a in blocks]
        return _copies_start(f"gather_start_{l}", _gather_ici_plan, 15, [], lands, dep)

    def gather_forward(l, flight, after):
        send, recv, _, lands, _ = flight
        _, lands = _copies_wait(f"gather_wait_{l}", _gather_ici_plan, send, recv, [], lands, after)
        return _copies_start(f"forward_start_{l}", _gather_d2d_plan, 15, [], lands)

    def gather_done(l, forward, after):
        send, recv, _, lands, _ = forward
        return _copies_wait(f"forward_wait_{l}", _gather_d2d_plan, send, recv, [], lands, after)[1]

    xs = [x[0]]
    saved = []
    forward = gather_forward(0, gather_start(0, None), [xs[0]])
    for l in range(DEPTH):
        w4, wglu4, wbs4, wba4, wout4 = gather_done(l, forward, [xs[l]])
        started = None
        if l + 1 < DEPTH:
            flight = gather_start(l + 1, wglu4)
            started = flight[4]
        wglu = wglu4.reshape(SSM, SSM)
        wout = wout4.reshape(D, D)
        wb4, wc4 = wb[l], wct[l]
        g1 = pre_norm_g[l].reshape(1, D)
        g2 = post_norm_g[l].reshape(1, D)
        dsk = d_skip[l].reshape(1, SSM)
        bgl = b_glu[l].reshape(1, SSM)
        proj, ht = _rms_proj_fwd(xs[l], g1, w4, started)
        states, yraw = _s5_scan_fwd(proj, wb4, wc4, ar4[l], ai4[l])
        o, lse = _attn_core_fwd(proj)
        forwarded = None
        if l + 1 < DEPTH:
            forward = gather_forward(l + 1, flight, [yraw, o])
            forwarded = forward[4]
        ys = _s5_tail_fwd(yraw, proj, dsk, wglu, bgl)
        ya = _attn_mix_fwd(o, lse, proj)
        xs.append(_merge_out_fwd(ys, ya, proj, wbs4, wba4, wout, g2, xs[l], forwarded))
        saved.append((w4, wglu, wbs4, wba4, wout, wb4, wc4, g1, g2, dsk, bgl, proj, ht, states, yraw, ys, ya, o, lse))

    loss_part, dx = _loss_head(xs[DEPTH], loss_target[0])
    loss = lax.psum(loss_part[0, 0], ("x", "y", "c"))

    big_w = (w_in, w_glu.reshape(DEPTH, 128, SSM), w_branch_s, w_branch_a, w_out)
    big_m = (m_w_in, m_w_glu, m_w_branch_s, m_w_branch_a, m_w_out)
    big_v = (v_w_in, v_w_glu, v_w_branch_s, v_w_branch_a, v_w_out)
    big_out = [None] * 5
    small = {k: [None] * DEPTH for k in ("g1", "da", "dbb", "dcc", "dsk", "bgl", "g2")}
    ici = [None] * DEPTH
    d2d = [None] * DEPTH

    def exchange_start(l, parts):
        lands = [_with_own_slot(lax.dynamic_index_in_dim(p, chip, 0, keepdims=False), chip) for p in parts]
        ici[l] = _copies_start(f"exchange_start_{l}", _exchange_plan, 15, parts, lands)

    def handoff_start(l, after):
        send, recv, srcs, lands, _ = ici[l]
        _, own = _copies_wait(f"exchange_wait_{l}", _exchange_plan, send, recv, srcs, lands, after)
        d2d[l] = _copies_start(f"handoff_start_{l}", _sibling_plan, 5, own,
                               [lax.empty(a.shape, a.dtype) for a in own])

    def update(l, after):
        send, recv, own, lands, _ = d2d[l]
        own, sib = _copies_wait(f"handoff_wait_{l}", _sibling_plan, send, recv, own, lands, after)
        for k in range(5):
            big_out[k] = _adamw_big(l, big_w[k], big_m[k], big_v[k], own[k], sib[k], big_out[k])

    for l in reversed(range(DEPTH)):
        w4, wglu, wbs4, wba4, wout, wb4, wc4, g1, g2, dsk, bgl, proj, ht, states, yraw, ys, ya, o, lse = saved[l]
        started = None
        if l + 1 < DEPTH:
            started = ici[l + 1][4] + d2d[l + 2][4] if l + 2 < DEPTH else ici[l + 1][4]
        dys, dya, dproj, dwbs, dwba, dwout, dg2 = _merge_out_bwd(dx, ys, ya, proj, wbs4, wba4, wout, g2, started)
        dyraw, du_skip, dproj, dwglu, dbgl, ddsk = _s5_tail_bwd(dys, yraw, proj, dsk, wglu, bgl, dproj)
        dproj, dbb, dcc, da = _s5_scan_bwd(dyraw, states, proj, du_skip, wb4, wc4, ar4[l], ai4[l], dproj)
        do, cterm, dproj = _attn_mix_bwd(dya, o, lse, proj, dproj)
        dproj = _attn_core_bwd(proj, do, lse, cterm, dproj)
        dh, dwin = _proj_bwd(dproj, ht, w4)
        parts = [dwin, dwglu.reshape(NCHIP, 128, SSM), dwbs, dwba, dwout.reshape(NCHIP, D // NCHIP, D)]
        exchange_start(l, parts)
        dx, dg1 = _rms_bwd(dh, xs[l], g1, dx, ici[l][4])
        if l + 1 < DEPTH:
            handoff_start(l + 1, [ici[l][4]])
        small["g1"][l], small["g2"][l], small["dsk"][l], small["bgl"][l] = dg1, dg2, ddsk, dbgl
        small["da"][l], small["dbb"][l], small["dcc"][l] = da, dbb, dcc

    dlre, dlim, dldt, dbre_c, dbim_c = _s5_prep_bwd(
        lre_c, lim_c, ldt_c, bre_c, bim_c, jnp.stack(small["da"]).reshape(nb, 1, SW),
        jnp.stack(small["dbb"]).reshape(nb, 16, SW))
    dcc = jnp.stack(small["dcc"]).reshape(nb, 16, SW)
    b_back = lambda t: t.reshape(DEPTH, NCB, 16, 8, 64).transpose(0, 1, 3, 4, 2).reshape(DEPTH, 32, 64, 16)
    c_back = lambda t: t.reshape(DEPTH, NCB, 16, 8, 64).transpose(0, 1, 3, 2, 4).reshape(DEPTH, 32, 16, 64)
    small_w = [pre_norm_g, lambda_re, lambda_im, log_dt, b_re, b_im, c_re, c_im, d_skip, b_glu, post_norm_g]
    small_m = [m_pre_norm_g, m_lambda_re, m_lambda_im, m_log_dt, m_b_re, m_b_im, m_c_re, m_c_im, m_d_skip,
               m_b_glu, m_post_norm_g]
    small_v = [v_pre_norm_g, v_lambda_re, v_lambda_im, v_log_dt, v_b_re, v_b_im, v_c_re, v_c_im, v_d_skip,
               v_b_glu, v_post_norm_g]
    small_g = [jnp.stack(small["g1"]).reshape(DEPTH, D), dlre.reshape(DEPTH, 32, 64), dlim.reshape(DEPTH, 32, 64),
               dldt.reshape(DEPTH, 32, 64).sum(-1), b_back(dbre_c), b_back(dbim_c), c_back(dcc[:, :, :HW]),
               c_back(dcc[:, :, HW:]),
               jnp.stack(small["dsk"]).reshape(DEPTH, SSM), jnp.stack(small["bgl"]).reshape(DEPTH, SSM),
               jnp.stack(small["g2"]).reshape(DEPTH, D)]
    part = _flatten_small(small_g).astype(BF16)
    send, recv, srcs, lands, token = _copies_start("small_start", _everyone_plan, 7, [part],
                                                   [_with_own_slot(part, device, 8)])
    update(DEPTH - 1, [token, ici[0][4]])
    for l in range(DEPTH - 2, 0, -1):
        update(l, [out[0] for out in big_out])
    handoff_start(0, [out[0] for out in big_out])
    update(0, [d2d[0][4]])
    _, (parts8,) = _copies_wait("small_wait", _everyone_plan, send, recv, srcs, lands, [out[0] for out in big_out])
    g_flat, d_flat, nm_flat, nv_flat = _adamw_small(_flatten_small(small_w), parts8, _flatten_small(small_m),
                                                    _flatten_small(small_v))
    sg = _unflatten_small(g_flat, small_w)
    sd = _unflatten_small(d_flat, small_w)
    snm = _unflatten_small(nm_flat, small_w)
    snv = _unflatten_small(nv_flat, small_w)

    def ordered(sm, which):
        bg = [big_out[k][which] for k in range(5)]
        bg[1] = bg[1].reshape(DEPTH, 128, SSM)
        return [sm[0], bg[0], sm[1], sm[2], sm[3], sm[4], sm[5], sm[6], sm[7], sm[8], bg[1], sm[9], bg[2], bg[3],
                bg[4], sm[10]]

    return (loss, dx[None], *ordered(sg, 0), *ordered(sd, 1), *ordered(snm, 2), *ordered(snv, 3))
```

```python
import math

import jax
import jax.numpy as jnp
from jax import lax
from jax.experimental import pallas as pl
from jax.experimental.pallas import tpu as pltpu

F32 = jnp.float32
BF16 = jnp.bfloat16
MESH = pl.DeviceIdType.MESH

DEPTH = 4
L = 2048
D = 1024
NCOL = 8192
SSM = 512
AW = 512
QKV = 1536
RMS_EPS = 1e-6
NCHIP = 4

COL_U, COL_ZS, COL_Q, COL_K, COL_V, COL_ZA, COL_GS, COL_GA = 0, 4, 8, 20, 32, 44, 48, 56

NSEG = 8
SEG = L // NSEG
NCB = 4
HW = 512
SW = 2 * HW

HEAD = 128
DILATIONS = (1, 4, 16)
SCALE = HEAD ** -0.5
NEG = -1e30

ADAM_LR, ADAM_B1, ADAM_B2, ADAM_EPS, ADAM_WD, ADAM_STEP = 0.001, 0.9, 0.999, 1e-08, 0.01, 10


def _sds(shape, dtype=F32):
    return jax.ShapeDtypeStruct(shape, dtype)


def _pcall(body, *, name, out_shape, grid=None, in_specs=None, out_specs=None, scratch=(), vmem_mb=None,
           aliases=None):
    params = {}
    if vmem_mb is not None:
        params["vmem_limit_bytes"] = vmem_mb << 20
    kw = {}
    if grid is not None:
        kw["grid"] = grid
    if in_specs is not None:
        kw["in_specs"] = in_specs
    if out_specs is not None:
        kw["out_specs"] = out_specs
    return pl.pallas_call(body, name=name, out_shape=out_shape, scratch_shapes=list(scratch),
                          compiler_params=pltpu.CompilerParams(**params),
                          input_output_aliases=aliases or {}, **kw)


def _dot(a, b):
    return jnp.dot(a, b, preferred_element_type=F32)


def _dot_nt(a, b):
    return lax.dot_general(a, b, (((1,), (1,)), ((), ())), preferred_element_type=F32)


def _dot_tn(a, b):
    return lax.dot_general(a, b, (((0,), (0,)), ((), ())), preferred_element_type=F32)


def _sigmoid(x):
    return jax.nn.sigmoid(x)


_GELU_K = math.sqrt(2.0 / math.pi)


def _gelu(x):
    return 0.5 * x * (1.0 + jnp.tanh(_GELU_K * (x + 0.044715 * (x * x * x))))


def _gelu_grad(x):
    t = jnp.tanh(_GELU_K * (x + 0.044715 * (x * x * x)))
    return 0.5 * (1.0 + t) + 0.5 * x * (1.0 - t * t) * (_GELU_K * (1.0 + 3.0 * 0.044715 * (x * x)))


def _token_operand(dep):
    if dep is None:
        return [], []
    return [dep], [pl.BlockSpec(memory_space=pl.ANY)]


def _rms_proj_fwd(x, g, w4, dep=None):
    tl, tn = 512, 1024
    ni = L // tl
    deps, dep_specs = _token_operand(dep)

    def body(x_ref, g_ref, w_ref, *rest):
        o_ref, h_ref = rest[len(deps):]
        rows = pl.ds(pl.multiple_of(pl.program_id(1) * tl, tl), tl)

        @pl.when(pl.program_id(0) == 0)
        def _():
            xx = x_ref[...]
            inv = lax.rsqrt(jnp.mean(xx * xx, axis=-1, keepdims=True) + RMS_EPS)
            h_ref[rows, :] = (xx * inv * g_ref[...]).astype(BF16)

        o_ref[...] = _dot(h_ref[rows, :], w_ref[...])

    proj, h = _pcall(
        body, name="rms_proj_fwd", grid=(NCOL // tn, ni),
        in_specs=[pl.BlockSpec((tl, D), lambda j, i: (jnp.where(j == 0, i, ni - 1), 0)),
                  pl.BlockSpec((1, D), lambda j, i: (0, 0)),
                  pl.BlockSpec((None, D, tn), lambda j, i: (lax.div(j, 2), 0, lax.rem(j, 2)))] + dep_specs,
        out_specs=[pl.BlockSpec((tl, tn), lambda j, i: (i, j)), pl.BlockSpec((L, D), lambda j, i: (0, 0))],
        out_shape=[_sds((L, NCOL)), _sds((L, D), BF16)], vmem_mb=48)(x, g, w4, *deps)

    def transpose(h_ref, ht_ref):
        ht_ref[...] = h_ref[...].astype(F32).T.astype(BF16)

    ht = _pcall(transpose, name="transpose_h", grid=(ni,), in_specs=[pl.BlockSpec((tl, D), lambda i: (i, 0))],
                out_specs=pl.BlockSpec((D, tl), lambda i: (0, i)), out_shape=_sds((D, L), BF16), vmem_mb=40)(h)
    return proj, ht


def _proj_bwd(dproj, ht, w4):
    tn = 1024

    def body(dp_ref, ht_ref, w_ref, dh_ref, dw_ref):
        @pl.when(pl.program_id(0) == 0)
        def _():
            dh_ref[...] = jnp.zeros((L, D), F32)

        dw_ref[...] = _dot(ht_ref[...], dp_ref[...]).astype(BF16)
        dh_ref[...] += _dot_nt(dp_ref[...], w_ref[...])

    return _pcall(
        body, name="proj_bwd", grid=(NCOL // tn,),
        in_specs=[pl.BlockSpec((L, tn), lambda j: (0, j)),
                  pl.BlockSpec((D, L), lambda j: (0, 0)),
                  pl.BlockSpec((None, D, tn), lambda j: (lax.div(j, 2), 0, lax.rem(j, 2)))],
        out_specs=[pl.BlockSpec((L, D), lambda j: (0, 0)),
                   pl.BlockSpec((None, D, tn), lambda j: (lax.div(j, 2), 0, lax.rem(j, 2)))],
        out_shape=[_sds((L, D)), _sds((NCHIP, D, NCOL // NCHIP), BF16)], vmem_mb=56)(dproj, ht, w4)


def _rms_bwd(dh, x, g, dxn, dep=None):
    tl = 512
    deps, dep_specs = _token_operand(dep)

    def body(dh_ref, x_ref, g_ref, dxn_ref, *rest):
        dx_ref, dg_ref = rest[len(deps):]
        i = pl.program_id(0)
        xx = x_ref[...]
        inv = lax.rsqrt(jnp.mean(xx * xx, axis=-1, keepdims=True) + RMS_EPS)
        nrm = xx * inv
        dh_v = dh_ref[...]
        dn = dh_v * g_ref[...]
        dx_ref[...] = dxn_ref[...] + inv * (dn - nrm * jnp.mean(dn * nrm, axis=-1, keepdims=True))
        part = jnp.sum(dh_v * nrm, axis=0, keepdims=True)

        @pl.when(i == 0)
        def _():
            dg_ref[...] = part

        @pl.when(i > 0)
        def _():
            dg_ref[...] += part

    row = pl.BlockSpec((tl, D), lambda i: (i, 0))
    vec = pl.BlockSpec((1, D), lambda i: (0, 0))
    return _pcall(body, name="rms_bwd", grid=(L // tl,), in_specs=[row, row, vec, row] + dep_specs,
                  out_specs=[row, vec], out_shape=[_sds((L, D)), _sds((1, D))], vmem_mb=40)(dh, x, g, dxn, *deps)


def _s5_param_math(lre, lim, ldt, bre, bim):
    lr = jnp.minimum(lre, -1e-4)
    dt = jnp.exp(ldt)
    mag = jnp.exp(lr * dt)
    ar = mag * jnp.cos(lim * dt)
    ai = mag * jnp.sin(lim * dt)
    den = lr * lr + lim * lim
    nr = ar - 1.0
    qr = (nr * lr + ai * lim) / den
    qi = (ai * lr - nr * lim) / den
    return ar, ai, qr * bre - qi * bim, qr * bim + qi * bre


def _group_of_lane(width):
    lane = lax.broadcasted_iota(jnp.int32, (16, width), 1)
    return lax.shift_right_logical(jnp.bitwise_and(lane, HW - 1), 6)


def _vec_spec():
    return pl.BlockSpec((None, 1, HW), lambda n: (n, 0, 0))


def _mat_spec():
    return pl.BlockSpec((None, 16, HW), lambda n: (n, 0, 0))


def _s5_prep(lre, lim, ldt, bre, bim, cre, cim):
    n = lre.shape[0]

    def body(lre_ref, lim_ref, ldt_ref, bre_ref, bim_ref, cre_ref, cim_ref, ar_ref, ai_ref, wb_ref, wct_ref):
        ar, ai, bbr, bbi = _s5_param_math(lre_ref[...], lim_ref[...], ldt_ref[...], bre_ref[...], bim_ref[...])
        ar_ref[...] = ar
        ai_ref[...] = ai
        group = _group_of_lane(SW)
        bb = jnp.concatenate([bbr, bbi], axis=1)
        cc = jnp.concatenate([cre_ref[...], -cim_ref[...]], axis=1)
        for g in range(8):
            rows = pl.ds(16 * g, 16)
            wb_ref[rows, :] = jnp.where(group == g, bb, 0.0).astype(BF16)
            wct_ref[rows, :] = jnp.where(group == g, cc, 0.0).astype(BF16)

    wide = pl.BlockSpec((None, 128, SW), lambda n: (n, 0, 0))
    return _pcall(body, name="s5_prep", grid=(n,),
                  in_specs=[_vec_spec()] * 3 + [_mat_spec()] * 4,
                  out_specs=[_vec_spec(), _vec_spec(), wide, wide],
                  out_shape=[_sds((n, 1, HW)), _sds((n, 1, HW)), _sds((n, 128, SW), BF16), _sds((n, 128, SW), BF16)])(
                      lre, lim, ldt, bre, bim, cre, cim)


def _s5_prep_bwd(lre, lim, ldt, bre, bim, ga, gbb):
    n = lre.shape[0]

    def body(lre_ref, lim_ref, ldt_ref, bre_ref, bim_ref, ga_ref, gbb_ref,
             dlre_ref, dlim_ref, dldt_ref, dbre_ref, dbim_ref):
        _, vjp = jax.vjp(_s5_param_math, lre_ref[...], lim_ref[...], ldt_ref[...], bre_ref[...], bim_ref[...])
        ga, gbb = ga_ref[...], gbb_ref[...]
        dlre, dlim, dldt, dbre, dbim = vjp((ga[:, :HW], ga[:, HW:], gbb[:, :HW], gbb[:, HW:]))
        dlre_ref[...] = dlre
        dlim_ref[...] = dlim
        dldt_ref[...] = dldt
        dbre_ref[...] = dbre
        dbim_ref[...] = dbim

    return _pcall(body, name="s5_prep_bwd", grid=(n,),
                  in_specs=[_vec_spec()] * 3 + [_mat_spec()] * 2 +
                  [pl.BlockSpec((None, 1, SW), lambda n: (n, 0, 0)), pl.BlockSpec((None, 16, SW), lambda n: (n, 0, 0))],
                  out_specs=[_vec_spec()] * 3 + [_mat_spec()] * 2,
                  out_shape=[_sds((n, 1, HW))] * 3 + [_sds((n, 16, HW))] * 2)(lre, lim, ldt, bre, bim, ga, gbb)


def _diag_blocks(full):
    group = _group_of_lane(SW)
    out = jnp.where(group == 0, full[0:16, :], 0.0)
    for g in range(1, 8):
        out = out + jnp.where(group == g, full[16 * g:16 * g + 16, :], 0.0)
    return out


def _permute_rows(dst_ref, src_ref):
    for s in range(NSEG):
        dst_ref[pl.ds(s, SEG, stride=NSEG), :] = src_ref[pl.ds(s * SEG, SEG), :]


def _unpermute_rows(src_ref, s):
    return src_ref[pl.ds(s, SEG, stride=NSEG), :]


def _row_of(slab, rows, j):
    return jnp.sum(jnp.where(rows == j, slab, 0.0), axis=0, keepdims=True)


def _s5_scan_fwd(proj, wb4, wc4, ar4, ai4):
    def body(u_ref, wb_ref, wc_ref, ar_ref, ai_ref, x_ref, y_ref, up_ref, d_ref, yp_ref):
        _permute_rows(up_ref, u_ref)
        d_ref[...] = _dot(up_ref[...].astype(BF16), wb_ref[...])
        ar = jnp.broadcast_to(ar_ref[...], (NSEG, HW))
        ai = jnp.broadcast_to(ai_ref[...], (NSEG, HW))
        zero = jnp.zeros((NSEG, HW), F32)

        def drive(t):
            row = pl.multiple_of(t * NSEG, NSEG)
            dd = d_ref[pl.ds(row, NSEG), :]
            return row, dd[:, :HW], dd[:, HW:]

        def local_step(t, c):
            xr, xi = c
            _, dr, di = drive(t)
            return ar * xr - ai * xi + dr, ar * xi + ai * xr + di

        fr, fi = lax.fori_loop(0, SEG, local_step, (zero, zero), unroll=8)
        pr, pi_ = ar_ref[...], ai_ref[...]
        for _ in range(int(math.log2(SEG))):
            pr, pi_ = pr * pr - pi_ * pi_, 2.0 * pr * pi_
        rows = lax.broadcasted_iota(jnp.int32, (NSEG, HW), 0)
        cr, ci = zero, zero
        for j in range(NSEG - 1):
            sr, si = _row_of(cr, rows, j), _row_of(ci, rows, j)
            gr, gi = _row_of(fr, rows, j), _row_of(fi, rows, j)
            nr = pr * sr - pi_ * si + gr
            ni = pr * si + pi_ * sr + gi
            cr = jnp.where(rows == j + 1, nr, cr)
            ci = jnp.where(rows == j + 1, ni, ci)

        def true_step(t, c):
            xr, xi = c
            row, dr, di = drive(t)
            nr = ar * xr - ai * xi + dr
            ni = ar * xi + ai * xr + di
            x_ref[pl.ds(row, NSEG), :] = jnp.concatenate([nr, ni], axis=1)
            return nr, ni

        lax.fori_loop(0, SEG, true_step, (cr, ci), unroll=8)
        yp_ref[...] = _dot_nt(x_ref[...].astype(BF16), wc_ref[...])
        for s in range(NSEG):
            y_ref[pl.ds(s * SEG, SEG), :] = _unpermute_rows(yp_ref, s)

    return _pcall(
        body, name="s5_scan_fwd", grid=(NCB,),
        in_specs=[pl.BlockSpec((L, 128), lambda b: (0, COL_U + b)),
                  pl.BlockSpec((None, 128, SW), lambda b: (b, 0, 0)),
                  pl.BlockSpec((None, 128, SW), lambda b: (b, 0, 0)),
                  pl.BlockSpec((None, 1, HW), lambda b: (b, 0, 0)),
                  pl.BlockSpec((None, 1, HW), lambda b: (b, 0, 0))],
        out_specs=[pl.BlockSpec((None, L, SW), lambda b: (b, 0, 0)),
                   pl.BlockSpec((L, 128), lambda b: (0, b))],
        out_shape=[_sds((NCB, L, SW)), _sds((L, SSM))],
        scratch=[pltpu.VMEM((L, 128), F32), pltpu.VMEM((L, SW), F32), pltpu.VMEM((L, 128), F32)],
        vmem_mb=56)(proj, wb4, wc4, ar4, ai4)


def _s5_scan_bwd(dy, xs, proj, du_skip, wb4, wc4, ar4, ai4, dproj):
    def body(dy_ref, x_ref, u_ref, dus_ref, wb_ref, wc_ref, ar_ref, ai_ref, _,
             du_ref, dbb_ref, dcc_ref, da_ref, dyp_ref, up_ref, g_ref, dup_ref):
        _permute_rows(dyp_ref, dy_ref)
        _permute_rows(up_ref, u_ref)
        dyp = dyp_ref[...].astype(BF16)
        g_ref[...] = _dot(dyp, wc_ref[...])
        dcc = _diag_blocks(_dot_tn(dyp, x_ref[...].astype(BF16)))
        dcc_ref[...] = jnp.concatenate([dcc[:, :HW], -dcc[:, HW:]], axis=1)
        ar = jnp.broadcast_to(ar_ref[...], (NSEG, HW))
        ai = jnp.broadcast_to(ai_ref[...], (NSEG, HW))
        zero = jnp.zeros((NSEG, HW), F32)

        def load(t):
            row = pl.multiple_of(t * NSEG, NSEG)
            gg = g_ref[pl.ds(row, NSEG), :]
            return row, gg[:, :HW], gg[:, HW:]

        def local_step(k, c):
            lr, li = c
            _, gr, gi = load(SEG - 1 - k)
            return ar * lr + ai * li + gr, ar * li - ai * lr + gi

        fr, fi = lax.fori_loop(0, SEG, local_step, (zero, zero), unroll=8)
        pr, pi_ = ar_ref[...], -ai_ref[...]
        for _ in range(int(math.log2(SEG))):
            pr, pi_ = pr * pr - pi_ * pi_, 2.0 * pr * pi_
        rows = lax.broadcasted_iota(jnp.int32, (NSEG, HW), 0)
        cr, ci = zero, zero
        for j in range(NSEG - 1, 0, -1):
            sr, si = _row_of(cr, rows, j), _row_of(ci, rows, j)
            gr, gi = _row_of(fr, rows, j), _row_of(fi, rows, j)
            nr = pr * sr - pi_ * si + gr
            ni = pr * si + pi_ * sr + gi
            cr = jnp.where(rows == j - 1, nr, cr)
            ci = jnp.where(rows == j - 1, ni, ci)

        def true_step(k, c):
            lr, li, gar, gai = c
            t = SEG - 1 - k
            row, gr, gi = load(t)
            nr = ar * lr + ai * li + gr
            ni = ar * li - ai * lr + gi
            g_ref[pl.ds(row, NSEG), :] = jnp.concatenate([nr, ni], axis=1)
            prow = pl.multiple_of(jnp.maximum(t - 1, 0) * NSEG, NSEG)
            xp = x_ref[pl.ds(prow, NSEG), :]
            live = (t > 0).astype(F32)
            xr, xi = xp[:, :HW] * live, xp[:, HW:] * live
            return nr, ni, gar + (nr * xr + ni * xi), gai + (ni * xr - nr * xi)

        _, _, gar, gai = lax.fori_loop(0, SEG, true_step, (cr, ci, zero, zero), unroll=4)
        l0 = g_ref[1:NSEG, :]
        xl = x_ref[(SEG - 1) * NSEG:(SEG - 1) * NSEG + NSEG - 1, :]
        l0r, l0i, xlr, xli = l0[:, :HW], l0[:, HW:], xl[:, :HW], xl[:, HW:]
        gar_t = jnp.sum(gar, axis=0, keepdims=True) + jnp.sum(l0r * xlr + l0i * xli, axis=0, keepdims=True)
        gai_t = jnp.sum(gai, axis=0, keepdims=True) + jnp.sum(l0i * xlr - l0r * xli, axis=0, keepdims=True)
        da_ref[...] = jnp.concatenate([gar_t, gai_t], axis=1)
        lam = g_ref[...].astype(BF16)
        dbb_ref[...] = _diag_blocks(_dot_tn(up_ref[...].astype(BF16), lam))
        dup_ref[...] = _dot_nt(lam, wb_ref[...])
        for s in range(NSEG):
            sl = pl.ds(s * SEG, SEG)
            du_ref[sl, :] = (_unpermute_rows(dup_ref, s) + dus_ref[sl, :]).astype(BF16)

    col = lambda off: pl.BlockSpec((L, 128), lambda b: (0, off + b))
    return _pcall(
        body, name="s5_scan_bwd", grid=(NCB,),
        in_specs=[col(0), pl.BlockSpec((None, L, SW), lambda b: (b, 0, 0)), col(COL_U), col(0),
                  pl.BlockSpec((None, 128, SW), lambda b: (b, 0, 0)),
                  pl.BlockSpec((None, 128, SW), lambda b: (b, 0, 0)),
                  pl.BlockSpec((None, 1, HW), lambda b: (b, 0, 0)),
                  pl.BlockSpec((None, 1, HW), lambda b: (b, 0, 0)),
                  pl.BlockSpec(memory_space=pl.ANY)],
        out_specs=[col(COL_U), pl.BlockSpec((None, 16, SW), lambda b: (b, 0, 0)),
                   pl.BlockSpec((None, 16, SW), lambda b: (b, 0, 0)),
                   pl.BlockSpec((None, 1, SW), lambda b: (b, 0, 0))],
        out_shape=[_sds((L, NCOL), BF16), _sds((NCB, 16, SW)), _sds((NCB, 16, SW)), _sds((NCB, 1, SW))],
        scratch=[pltpu.VMEM((L, 128), F32), pltpu.VMEM((L, 128), F32), pltpu.VMEM((L, SW), F32),
                 pltpu.VMEM((L, 128), F32)],
        aliases={8: 0}, vmem_mb=56)(dy, xs, proj, du_skip, wb4, wc4, ar4, ai4, dproj)


def _s5_tail_fwd(yraw, proj, dsk, wglu, bglu):
    tl = 512

    def body(y_ref, u_ref, z_ref, dsk_ref, w_ref, b_ref, o_ref):
        y1 = y_ref[...] + dsk_ref[...] * u_ref[...]
        y2 = _gelu(y1)
        gl = _dot(y2.astype(BF16), w_ref[...]) + b_ref[...]
        z = z_ref[...]
        o_ref[...] = (y2 * _sigmoid(gl)) * (z * _sigmoid(z))

    blk = lambda c: pl.BlockSpec((tl, SSM), lambda i: (i, c))
    vec = pl.BlockSpec((1, SSM), lambda i: (0, 0))
    return _pcall(body, name="s5_tail_fwd", grid=(L // tl,),
                  in_specs=[blk(0), blk(0), blk(1), vec, pl.BlockSpec((SSM, SSM), lambda i: (0, 0)), vec],
                  out_specs=blk(0), out_shape=_sds((L, SSM)), vmem_mb=40)(yraw, proj, proj, dsk, wglu, bglu)


def _s5_tail_bwd(dys, yraw, proj, dsk, wglu, bglu, dproj):
    tl = 512
    ni = L // tl

    def body(dys_ref, y_ref, u_ref, z_ref, dsk_ref, w_ref, b_ref, _,
             dy_ref, dus_ref, dz_ref, dw_ref, db_ref, dd_ref, acc_ref):
        i = pl.program_id(0)
        u = u_ref[...]
        y1 = y_ref[...] + dsk_ref[...] * u
        y2 = _gelu(y1)
        y2b = y2.astype(BF16)
        sg = _sigmoid(_dot(y2b, w_ref[...]) + b_ref[...])
        y3 = y2 * sg
        z = z_ref[...]
        sz = _sigmoid(z)
        dys = dys_ref[...]
        dy3 = dys * (z * sz)
        dz_ref[...] = (dys * y3 * (sz * (1.0 + z * (1.0 - sz)))).astype(BF16)
        dgl = (dy3 * y2) * (sg * (1.0 - sg))
        dglb = dgl.astype(BF16)
        dy2 = dy3 * sg + _dot_nt(dglb, w_ref[...])
        dy1 = dy2 * _gelu_grad(y1)
        dy_ref[...] = dy1
        dus_ref[...] = dsk_ref[...] * dy1
        dw = _dot_tn(y2b, dglb)
        db = jnp.sum(dgl, axis=0, keepdims=True)
        dd = jnp.sum(dy1 * u, axis=0, keepdims=True)

        @pl.when(i == 0)
        def _():
            acc_ref[...] = dw
            db_ref[...] = db
            dd_ref[...] = dd

        @pl.when(i > 0)
        def _():
            acc_ref[...] += dw
            db_ref[...] += db
            dd_ref[...] += dd

        @pl.when(i == ni - 1)
        def _():
            dw_ref[...] = acc_ref[...].astype(BF16)

    blk = lambda c: pl.BlockSpec((tl, SSM), lambda i: (i, c))
    vec = pl.BlockSpec((1, SSM), lambda i: (0, 0))
    mat = pl.BlockSpec((SSM, SSM), lambda i: (0, 0))
    return _pcall(body, name="s5_tail_bwd", grid=(ni,),
                  in_specs=[blk(0), blk(0), blk(0), blk(1), vec, mat, vec, pl.BlockSpec(memory_space=pl.ANY)],
                  out_specs=[blk(0), blk(0), blk(COL_ZS // 4), mat, vec, vec],
                  out_shape=[_sds((L, SSM)), _sds((L, SSM)), _sds((L, NCOL), BF16), _sds((SSM, SSM), BF16),
                             _sds((1, SSM)), _sds((1, SSM))],
                  scratch=[pltpu.VMEM((SSM, SSM), F32)], aliases={7: 2},
                  vmem_mb=40)(dys, yraw, proj, proj, dsk, wglu, bglu, dproj)


def _attn_blocks(dil):
    n = L // dil
    return [(i * HEAD * dil + r, r * n + i * HEAD, i == 0) for r in range(dil) for i in range(n // HEAD)]


def _rows(start, dil):
    return pl.ds(start, HEAD) if dil == 1 else pl.ds(start, HEAD, stride=dil)


def _residue_rows(r, dil):
    n = L // dil
    return pl.ds(0, L) if dil == 1 else pl.ds(r, n, stride=dil)


def _gather_residues(pairs, dil, dtype=BF16):
    n = L // dil
    for src, dst in pairs:
        for r in range(dil):
            dst[pl.ds(r * n, n), :] = src[_residue_rows(r, dil), :].astype(dtype)


def _band_masks():
    ri = lax.broadcasted_iota(jnp.int32, (HEAD, HEAD), 0)
    ci = lax.broadcasted_iota(jnp.int32, (HEAD, HEAD), 1)
    ri2 = lax.broadcasted_iota(jnp.int32, (HEAD, 2 * HEAD), 0)
    ci2 = lax.broadcasted_iota(jnp.int32, (HEAD, 2 * HEAD), 1)
    return ci <= ri, jnp.logical_and(ci2 >= ri2, ci2 - HEAD <= ri2)


def _qkv_specs(index):
    def spec(off):
        return pl.BlockSpec((L, HEAD), lambda gi, h: (0, off + index(gi, h)))
    return [spec(off) for off in (COL_Q, COL_K, COL_V)]


def _attn_core_fwd(proj):
    def body(q_ref, k_ref, v_ref, o_ref, lse_ref, qp_ref, kp_ref, vp_ref):
        gi = pl.program_id(0)
        m_cur, m_both = _band_masks()
        for g, dil in enumerate(DILATIONS):
            @pl.when(gi == g)
            def _(dil=dil):
                _gather_residues(((q_ref, qp_ref), (k_ref, kp_ref), (v_ref, vp_ref)), dil)
                for st, base, first in _attn_blocks(dil):
                    q = qp_ref[pl.ds(base, HEAD), :]
                    keys = pl.ds(base, HEAD) if first else pl.ds(base - HEAD, 2 * HEAD)
                    s = jnp.where(m_cur if first else m_both, _dot_nt(q, kp_ref[keys, :]) * SCALE, NEG)
                    mx = jnp.max(s, axis=-1, keepdims=True)
                    p = jnp.exp(s - mx)
                    den = jnp.sum(p, axis=-1, keepdims=True)
                    r = _rows(st, dil)
                    o_ref[r, :] = _dot(p.astype(BF16), vp_ref[keys, :]) / den
                    lse_ref[r, :] = jnp.broadcast_to(mx + jnp.log(den), (HEAD, HEAD))

    idx = lambda gi, h: gi * 4 + h
    out = pl.BlockSpec((L, HEAD), lambda gi, h: (0, gi * 4 + h))
    return _pcall(body, name="attn_core_fwd", grid=(3, 4), in_specs=_qkv_specs(idx), out_specs=[out, out],
                  out_shape=[_sds((L, QKV)), _sds((L, QKV))], scratch=[pltpu.VMEM((L, HEAD), BF16)] * 3,
                  vmem_mb=40)(proj, proj, proj)


def _attn_mix_fwd(o, lse, proj):
    tl = 512

    def body(o_ref, l_ref, z_ref, y_ref):
        for h in range(4):
            c = [pl.ds((g * 4 + h) * HEAD, HEAD) for g in range(3)]
            ls = [l_ref[:, c[g]] for g in range(3)]
            m = jnp.maximum(jnp.maximum(ls[0], ls[1]), ls[2])
            e = [jnp.exp(ls[g] - m) for g in range(3)]
            y = (e[0] * o_ref[:, c[0]] + e[1] * o_ref[:, c[1]] + e[2] * o_ref[:, c[2]]) / (e[0] + e[1] + e[2])
            z = z_ref[:, pl.ds(h * HEAD, HEAD)]
            y_ref[:, pl.ds(h * HEAD, HEAD)] = y * (z * _sigmoid(z))

    wide = pl.BlockSpec((tl, QKV), lambda i: (i, 0))
    return _pcall(body, name="attn_mix_fwd", grid=(L // tl,),
                  in_specs=[wide, wide, pl.BlockSpec((tl, AW), lambda i: (i, COL_ZA // 4))],
                  out_specs=pl.BlockSpec((tl, AW), lambda i: (i, 0)), out_shape=_sds((L, AW)),
                  vmem_mb=40)(o, lse, proj)


def _attn_mix_bwd(dya, o, lse, proj, dproj):
    tl = 512

    def body(dya_ref, o_ref, l_ref, z_ref, _, do_ref, c_ref, dz_ref):
        for h in range(4):
            c = [pl.ds((g * 4 + h) * HEAD, HEAD) for g in range(3)]
            hs = pl.ds(h * HEAD, HEAD)
            ls = [l_ref[:, c[g]] for g in range(3)]
            m = jnp.maximum(jnp.maximum(ls[0], ls[1]), ls[2])
            e = [jnp.exp(ls[g] - m) for g in range(3)]
            den = e[0] + e[1] + e[2]
            al = [e[g] / den for g in range(3)]
            y = al[0] * o_ref[:, c[0]] + al[1] * o_ref[:, c[1]] + al[2] * o_ref[:, c[2]]
            z = z_ref[:, hs]
            sz = _sigmoid(z)
            dya = dya_ref[:, hs]
            dz_ref[:, hs] = (dya * y * (sz * (1.0 + z * (1.0 - sz)))).astype(BF16)
            dy = dya * (z * sz)
            tot = jnp.sum(dy * y, axis=-1, keepdims=True)
            for g in range(3):
                do_ref[:, c[g]] = al[g] * dy
                c_ref[:, c[g]] = -(al[g] * tot)

    wide = pl.BlockSpec((tl, QKV), lambda i: (i, 0))
    nar = pl.BlockSpec((tl, AW), lambda i: (i, 0))
    za = pl.BlockSpec((tl, AW), lambda i: (i, COL_ZA // 4))
    return _pcall(body, name="attn_mix_bwd", grid=(L // tl,),
                  in_specs=[nar, wide, wide, za, pl.BlockSpec(memory_space=pl.ANY)],
                  out_specs=[wide, wide, za], out_shape=[_sds((L, QKV)), _sds((L, QKV)), _sds((L, NCOL), BF16)],
                  aliases={4: 2}, vmem_mb=48)(dya, o, lse, proj, dproj)


def _attn_core_bwd(proj, do, lse, cc, dproj):
    def body(q_ref, k_ref, v_ref, do_ref, lse_ref, c_ref, _, dp_ref,
             qp_ref, kp_ref, vp_ref, dop_ref, lsep_ref, cp_ref, dqp_ref, dkp_ref, dvp_ref, tok_ref, out_ref, sems):
        gi = pl.program_id(0)
        head = gi * 4 + pl.program_id(1)
        m_cur, m_both = _band_masks()
        dkp_ref[...] = jnp.zeros((L, HEAD), F32)
        dvp_ref[...] = jnp.zeros((L, HEAD), F32)
        copies = []
        for g, dil in enumerate(DILATIONS):
            @pl.when(gi == g)
            def _(dil=dil):
                _gather_residues(((q_ref, qp_ref), (k_ref, kp_ref), (v_ref, vp_ref), (do_ref, dop_ref)), dil)
                _gather_residues(((lse_ref, lsep_ref), (c_ref, cp_ref)), dil, F32)
                for _, base, first in _attn_blocks(dil):
                    mine = pl.ds(base, HEAD)
                    keys = mine if first else pl.ds(base - HEAD, 2 * HEAD)
                    q, do_b, kb = qp_ref[mine, :], dop_ref[mine, :], kp_ref[keys, :]
                    lse_b, c_b = lsep_ref[mine, :], cp_ref[mine, :]
                    if not first:
                        lse_b = jnp.concatenate([lse_b, lse_b], axis=1)
                        c_b = jnp.concatenate([c_b, c_b], axis=1)
                    s = _dot_nt(q, kb) * SCALE
                    p = jnp.where(m_cur if first else m_both, jnp.exp(s - lse_b), 0.0)
                    ds = (p * (_dot_nt(do_b, vp_ref[keys, :]) + c_b) * SCALE).astype(BF16)
                    dqp_ref[mine, :] = _dot(ds, kb)
                    dkp_ref[keys, :] += _dot_tn(ds, q)
                    dvp_ref[keys, :] += _dot_tn(p.astype(BF16), do_b)
                n = L // dil
                for slot, src in enumerate((dqp_ref, dkp_ref, dvp_ref)):
                    for r in range(dil):
                        tok_ref[_residue_rows(r, dil), :] = src[pl.ds(r * n, n), :]
                    out_ref[slot] = tok_ref[...].astype(BF16)

        for slot, off in enumerate((COL_Q, COL_K, COL_V)):
            cols = pl.ds(pl.multiple_of((off + head) * HEAD, HEAD), HEAD)
            copies.append(pltpu.make_async_copy(out_ref.at[slot], dp_ref.at[:, cols], sems.at[slot]))
            copies[-1].start()
        for cp in copies:
            cp.wait()

    idx = lambda gi, h: gi * 4 + h
    blk = pl.BlockSpec((L, HEAD), lambda gi, h: (0, gi * 4 + h))
    hbm = pl.BlockSpec(memory_space=pl.ANY)
    return _pcall(body, name="attn_core_bwd", grid=(3, 4), in_specs=_qkv_specs(idx) + [blk, blk, blk, hbm],
                  out_specs=hbm, out_shape=_sds((L, NCOL), BF16),
                  scratch=[pltpu.VMEM((L, HEAD), BF16)] * 4 + [pltpu.VMEM((L, HEAD), F32)] * 6 +
                  [pltpu.VMEM((3, L, HEAD), BF16), pltpu.SemaphoreType.DMA((3,))],
                  aliases={6: 0}, vmem_mb=48)(proj, proj, proj, do, lse, cc, dproj)


def _merge_fwd_math(ys_b, ya_b, gs, ga, wbs_ref, wba_ref, wout_ref):
    bs = jnp.concatenate([_dot(ys_b, wbs_ref[s]) for s in range(NCHIP)], axis=1)
    ba = jnp.concatenate([_dot(ya_b, wba_ref[s]) for s in range(NCHIP)], axis=1)
    sgs, sga = _sigmoid(gs), _sigmoid(ga)
    merged = sgs * bs + sga * ba
    out = _dot(merged.astype(BF16), wout_ref[...])
    inv = lax.rsqrt(jnp.mean(out * out, axis=-1, keepdims=True) + RMS_EPS)
    return bs, ba, sgs, sga, merged, out, inv


def _merge_specs(tl):
    row = lambda w, c: pl.BlockSpec((tl, w), lambda i: (i, c))
    return [row(SSM, 0), row(AW, 0), row(D, COL_GS // 8), row(D, COL_GA // 8),
            pl.BlockSpec((NCHIP, SSM, D // NCHIP), lambda i: (0, 0, 0)),
            pl.BlockSpec((NCHIP, AW, D // NCHIP), lambda i: (0, 0, 0)),
            pl.BlockSpec((D, D), lambda i: (0, 0)),
            pl.BlockSpec((1, D), lambda i: (0, 0))]


def _merge_out_fwd(ys, ya, proj, wbs4, wba4, wout, g2, x, dep=None):
    tl = 256
    deps, dep_specs = _token_operand(dep)

    def body(ys_ref, ya_ref, gs_ref, ga_ref, wbs_ref, wba_ref, wout_ref, g2_ref, x_ref, *rest):
        o_ref = rest[-1]
        *_, out, inv = _merge_fwd_math(ys_ref[...].astype(BF16), ya_ref[...].astype(BF16), gs_ref[...],
                                       ga_ref[...], wbs_ref, wba_ref, wout_ref)
        o_ref[...] = x_ref[...] + out * inv * g2_ref[...]

    row = pl.BlockSpec((tl, D), lambda i: (i, 0))
    return _pcall(body, name="merge_out_fwd", grid=(L // tl,), in_specs=_merge_specs(tl) + [row] + dep_specs,
                  out_specs=row, out_shape=_sds((L, D)),
                  vmem_mb=48)(ys, ya, proj, proj, wbs4, wba4, wout, g2, x, *deps)


def _merge_out_bwd(dxn, ys, ya, proj, wbs4, wba4, wout, g2, dep=None):
    tl = 256
    ni = L // tl
    cw = D // NCHIP
    deps, dep_specs = _token_operand(dep)

    def body(dxn_ref, ys_ref, ya_ref, gs_ref, ga_ref, wbs_ref, wba_ref, wout_ref, g2_ref, *rest):
        (dys_ref, dya_ref, dgate_ref, dwbs_ref, dwba_ref, dwout_ref, dg2_ref,
         abs_ref, aba_ref, aout_ref) = rest[len(deps):]
        i = pl.program_id(0)

        @pl.when(i == 0)
        def _():
            abs_ref[...] = jnp.zeros(abs_ref.shape, F32)
            aba_ref[...] = jnp.zeros(aba_ref.shape, F32)
            aout_ref[...] = jnp.zeros(aout_ref.shape, F32)
            dg2_ref[...] = jnp.zeros(dg2_ref.shape, F32)

        ys_b, ya_b = ys_ref[...].astype(BF16), ya_ref[...].astype(BF16)
        bs, ba, sgs, sga, merged, out, inv = _merge_fwd_math(ys_b, ya_b, gs_ref[...], ga_ref[...],
                                                             wbs_ref, wba_ref, wout_ref)
        nrm = out * inv
        dxn = dxn_ref[...]
        dg2_ref[...] += jnp.sum(dxn * nrm, axis=0, keepdims=True)
        dn = dxn * g2_ref[...]
        dout = (inv * (dn - nrm * jnp.mean(dn * nrm, axis=-1, keepdims=True))).astype(BF16)
        aout_ref[...] += _dot_tn(merged.astype(BF16), dout)
        dm = _dot_nt(dout, wout_ref[...])
        dbs, dba = dm * sgs, dm * sga
        dgate_ref[:, :D] = (dm * bs * (sgs * (1.0 - sgs))).astype(BF16)
        dgate_ref[:, D:] = (dm * ba * (sga * (1.0 - sga))).astype(BF16)
        dbs_b, dba_b = dbs.astype(BF16), dba.astype(BF16)
        dys = None
        dya = None
        for s in range(NCHIP):
            cs = slice(s * cw, (s + 1) * cw)
            p_s = _dot_nt(dbs_b[:, cs], wbs_ref[s])
            p_a = _dot_nt(dba_b[:, cs], wba_ref[s])
            dys = p_s if dys is None else dys + p_s
            dya = p_a if dya is None else dya + p_a
            abs_ref[s] += _dot_tn(ys_b, dbs_b[:, cs])
            aba_ref[s] += _dot_tn(ya_b, dba_b[:, cs])

        dys_ref[...] = dys
        dya_ref[...] = dya

        @pl.when(i == ni - 1)
        def _():
            dwbs_ref[...] = abs_ref[...].astype(BF16)
            dwba_ref[...] = aba_ref[...].astype(BF16)
            dwout_ref[...] = aout_ref[...].astype(BF16)

    row = lambda w: pl.BlockSpec((tl, w), lambda i: (i, 0))
    w4 = pl.BlockSpec((NCHIP, SSM, cw), lambda i: (0, 0, 0))
    sq = pl.BlockSpec((D, D), lambda i: (0, 0))
    vec = pl.BlockSpec((1, D), lambda i: (0, 0))
    return _pcall(body, name="merge_out_bwd", grid=(ni,), in_specs=[row(D)] + _merge_specs(tl) + dep_specs,
                  out_specs=[row(SSM), row(AW), pl.BlockSpec((tl, 2 * D), lambda i: (i, COL_GS // 16)), w4, w4, sq,
                             vec],
                  out_shape=[_sds((L, SSM)), _sds((L, AW)), _sds((L, NCOL), BF16),
                             _sds((NCHIP, SSM, cw), BF16), _sds((NCHIP, AW, cw), BF16), _sds((D, D), BF16),
                             _sds((1, D))],
                  scratch=[pltpu.VMEM((NCHIP, SSM, cw), F32), pltpu.VMEM((NCHIP, AW, cw), F32),
                           pltpu.VMEM((D, D), F32)],
                  vmem_mb=56)(dxn, ys, ya, proj, proj, wbs4, wba4, wout, g2, *deps)


def _loss_head(y, target):
    tl = 512

    def body(y_ref, t_ref, loss_ref, dy_ref):
        i = pl.program_id(0)
        err = y_ref[...] - t_ref[...]
        dy_ref[...] = err / D
        part = 0.5 * jnp.sum(jnp.mean(err * err, axis=-1, keepdims=True), axis=0, keepdims=True)
        part = jnp.broadcast_to(part, (8, 128))

        @pl.when(i == 0)
        def _():
            loss_ref[...] = part

        @pl.when(i > 0)
        def _():
            loss_ref[...] += part

    row = pl.BlockSpec((tl, D), lambda i: (i, 0))
    return _pcall(body, name="loss_head", grid=(L // tl,), in_specs=[row, row],
                  out_specs=[pl.BlockSpec((8, 128), lambda i: (0, 0)), row],
                  out_shape=[_sds((8, 128)), _sds((L, D))], vmem_mb=40)(y, target)


def _adamw_math(w, g, m, v):
    m = ADAM_B1 * m + (1.0 - ADAM_B1) * g
    v = ADAM_B2 * v + (1.0 - ADAM_B2) * (g * g)
    m_hat = m / (1.0 - ADAM_B1 ** ADAM_STEP)
    v_hat = v / (1.0 - ADAM_B2 ** ADAM_STEP)
    delta = -ADAM_LR * (m_hat / (jnp.sqrt(v_hat) + ADAM_EPS) + ADAM_WD * w)
    return delta, m, v


def _adamw_big(layer, w, m, v, own, sib, prev):
    _, r, c = w.shape
    tr = min(r, 128)

    def body(w_ref, m_ref, v_ref, own_ref, sib_ref, *rest):
        g_ref, d_ref, nm_ref, nv_ref = rest[-4:]
        a = own_ref[0].astype(F32)
        b = sib_ref[0].astype(F32)
        for s in range(1, NCHIP):
            a = a + own_ref[s].astype(F32)
            b = b + sib_ref[s].astype(F32)
        g = a + b
        delta, nm, nv = _adamw_math(w_ref[...], g, m_ref[...], v_ref[...])
        g_ref[...] = g
        d_ref[...] = delta
        nm_ref[...] = nm
        nv_ref[...] = nv

    lay = pl.BlockSpec((None, tr, c), lambda i: (layer, i, 0))
    slots = pl.BlockSpec((NCHIP, tr, c), lambda i: (0, i, 0))
    ins = [w, m, v, own, sib]
    in_specs = [lay, lay, lay, slots, slots]
    aliases = {}
    if prev is not None:
        ins += list(prev)
        in_specs += [pl.BlockSpec(memory_space=pl.ANY)] * 4
        aliases = {5 + k: k for k in range(4)}
    return _pcall(body, name="adamw_big", grid=(r // tr,), in_specs=in_specs, out_specs=[lay] * 4,
                  out_shape=[_sds(w.shape)] * 4, aliases=aliases, vmem_mb=48)(*ins)


SMALL_TILE = 512


def _adamw_small(w, parts, m, v):
    r = w.shape[0]

    def body(w_ref, p_ref, m_ref, v_ref, g_ref, d_ref, nm_ref, nv_ref):
        g = p_ref[0].astype(F32)
        for dev in range(1, 8):
            g = g + p_ref[dev].astype(F32)
        delta, nm, nv = _adamw_math(w_ref[...], g, m_ref[...], v_ref[...])
        g_ref[...] = g
        d_ref[...] = delta
        nm_ref[...] = nm
        nv_ref[...] = nv

    row = pl.BlockSpec((SMALL_TILE, 128), lambda i: (i, 0))
    return _pcall(body, name="adamw_small", grid=(r // SMALL_TILE,),
                  in_specs=[row, pl.BlockSpec((8, SMALL_TILE, 128), lambda i: (0, i, 0)), row, row],
                  out_specs=[row] * 4, out_shape=[_sds((r, 128))] * 4, vmem_mb=40)(w, parts, m, v)


def _place():
    x, y, c = lax.axis_index("x"), lax.axis_index("y"), lax.axis_index("c")
    return x, y, c, 2 * x + y


def _chip_peer(x, y, j):
    return (1 - x if j & 2 else x), (1 - y if j & 1 else y)


_HBM = pl.BlockSpec(memory_space=pltpu.HBM)
_SEM = pl.BlockSpec(memory_space=pltpu.SEMAPHORE)
_EFFECT = pltpu.SideEffectType.DATAFLOW_SIDE_EFFECTING


def _in_hbm(a):
    return pltpu.with_memory_space_constraint(a, pltpu.HBM)


def _copies_start(name, plan, ncopy, srcs, lands, dep=None):
    n = len(srcs) + len(lands)
    deps, dep_specs = _token_operand(dep)

    def body(*refs):
        send_sems, recv_sems = refs[n + len(deps)], refs[n + len(deps) + 1]
        token = refs[-1]
        for k, (src, dst, dev) in enumerate(plan(refs[:len(srcs)], refs[len(srcs):n])):
            pltpu.make_async_remote_copy(src_ref=src, dst_ref=dst, send_sem=send_sems.at[k],
                                         recv_sem=recv_sems.at[k], device_id=dev, device_id_type=MESH).start()
        token[...] = jnp.zeros_like(token)

    bufs = list(srcs) + list(lands)
    outs = pl.pallas_call(
        body, name=name,
        out_shape=(pltpu.SemaphoreType.DMA((ncopy,)), pltpu.SemaphoreType.DMA((ncopy,)),
                   *[pltpu.HBM(a.shape, a.dtype) for a in bufs], _sds((8, 128))),
        in_specs=[_HBM] * n + dep_specs,
        out_specs=(_SEM, _SEM, *[_HBM] * n, pl.BlockSpec(memory_space=pltpu.VMEM)),
        input_output_aliases={i: 2 + i for i in range(n)},
        compiler_params=pltpu.CompilerParams(has_side_effects=_EFFECT),
    )(*[_in_hbm(a) for a in bufs], *deps)
    return outs[0], outs[1], list(outs[2:2 + len(srcs)]), list(outs[2 + len(srcs):2 + n]), outs[-1]


def _copies_wait(name, plan, send_sems, recv_sems, srcs, lands, after):
    n = len(srcs) + len(lands)
    after = list(after)

    def body(*refs):
        s_sems, r_sems = refs[n], refs[n + 1]
        for k, (src, dst, dev) in enumerate(plan(refs[:len(srcs)], refs[len(srcs):n])):
            cp = pltpu.make_async_remote_copy(src_ref=src, dst_ref=dst, send_sem=s_sems.at[k],
                                              recv_sem=r_sems.at[k], device_id=dev, device_id_type=MESH)
            cp.wait_send()
            cp.wait_recv()

    bufs = list(srcs) + list(lands)
    outs = pl.pallas_call(
        body, name=name,
        out_shape=tuple(pltpu.HBM(a.shape, a.dtype) for a in bufs),
        in_specs=[_HBM] * n + [_SEM, _SEM] + [pl.BlockSpec(memory_space=pl.ANY)] * len(after),
        out_specs=tuple([_HBM] * n),
        input_output_aliases={i: i for i in range(n)},
        compiler_params=pltpu.CompilerParams(has_side_effects=_EFFECT),
    )(*bufs, send_sems, recv_sems, *after)
    return list(outs[:len(srcs)]), list(outs[len(srcs):])


def _with_own_slot(block, slot, nslots=NCHIP):
    land = lax.empty((nslots,) + block.shape, block.dtype)
    return lax.dynamic_update_slice(land, block[None], (slot,) + (0,) * block.ndim)


def _my_half(land, slot, c):
    half = land.shape[1] // 2
    return land.at[slot, pl.ds(pl.multiple_of(c * half, 16), half)]


def _gather_ici_plan(srcs, lands):
    x, y, c, s = _place()
    return [(_my_half(land, s, c), _my_half(land, s, c), (*_chip_peer(x, y, j), c))
            for land in lands for j in range(1, NCHIP)]


def _gather_d2d_plan(srcs, lands):
    x, y, c, s = _place()
    return [(_my_half(land, s ^ j, c), _my_half(land, s ^ j, c), (x, y, 1 - c))
            for land in lands for j in range(1, NCHIP)]


def _exchange_plan(srcs, lands):
    x, y, c, s = _place()
    return [(src.at[s ^ j], land.at[s], (*_chip_peer(x, y, j), c))
            for src, land in zip(srcs, lands) for j in range(1, NCHIP)]


def _sibling_plan(srcs, lands):
    x, y, c, _ = _place()
    return [(src, land, (x, y, 1 - c)) for src, land in zip(srcs, lands)]


def _everyone_plan(srcs, lands):
    x, y, c, _ = _place()
    me = 4 * x + 2 * y + c
    return [(srcs[0], lands[0].at[me], (*_chip_peer(x, y, j >> 1), 1 - c if j & 1 else c)) for j in range(1, 8)]


def _flatten_small(parts):
    flat = jnp.concatenate([p.reshape(-1) for p in parts])
    n = flat.shape[0]
    rows = -(-n // (128 * SMALL_TILE)) * SMALL_TILE
    return jnp.pad(flat, (0, rows * 128 - n)).reshape(rows, 128)


def _unflatten_small(buf, like):
    flat = buf.reshape(-1)
    out, at = [], 0
    for p in like:
        out.append(flat[at:at + p.size].reshape(p.shape))
        at += p.size
    return out


def kernel(x, pre_norm_g, w_in, lambda_re, lambda_im, log_dt, b_re, b_im, c_re, c_im, d_skip, w_glu, b_glu, w_branch_s, w_branch_a, w_out, post_norm_g, loss_target, m_pre_norm_g, m_w_in, m_lambda_re, m_lambda_im, m_log_dt, m_b_re, m_b_im, m_c_re, m_c_im, m_d_skip, m_w_glu, m_b_glu, m_w_branch_s, m_w_branch_a, m_w_out, m_post_norm_g, v_pre_norm_g, v_w_in, v_lambda_re, v_lambda_im, v_log_dt, v_b_re, v_b_im, v_c_re, v_c_im, v_d_skip, v_w_glu, v_b_glu, v_w_branch_s, v_w_branch_a, v_w_out, v_post_norm_g):
    nb = DEPTH * NCB
    lre_c = lambda_re.reshape(nb, 1, HW)
    lim_c = lambda_im.reshape(nb, 1, HW)
    ldt_c = jnp.broadcast_to(log_dt[:, :, None], (DEPTH, 32, 64)).reshape(nb, 1, HW)
    b_rows = lambda t: t.reshape(DEPTH, NCB, 8, 64, 16).transpose(0, 1, 4, 2, 3).reshape(nb, 16, HW)
    c_rows = lambda t: t.reshape(DEPTH, NCB, 8, 16, 64).transpose(0, 1, 3, 2, 4).reshape(nb, 16, HW)
    bre_c, bim_c = b_rows(b_re), b_rows(b_im)
    ar, ai, wb, wct = _s5_prep(lre_c, lim_c, ldt_c, bre_c, bim_c, c_rows(c_re), c_rows(c_im))
    ar4 = ar.reshape(DEPTH, NCB, 1, HW)
    ai4 = ai.reshape(DEPTH, NCB, 1, HW)
    wb = wb.reshape(DEPTH, NCB, 128, SW)
    wct = wct.reshape(DEPTH, NCB, 128, SW)

    chip = 2 * lax.axis_index("x") + lax.axis_index("y")
    device = 2 * chip + lax.axis_index("c")

    def gather_start(l, dep):
        blocks = [w_in[l], w_glu[l], w_branch_s[l], w_branch_a[l], w_out[l]]
        lands = [_with_own_slot(a.astype(BF16), chip) for a in blocks]
        return _copies_start(f"gather_start_{l}", _gather_ici_plan, 15, [], lands, dep)

    def gather_forward(l, flight, after):
        send, recv, _, lands, _ = flight
        _, lands = _copies_wait(f"gather_wait_{l}", _gather_ici_plan, send, recv, [], lands, after)
        return _copies_start(f"forward_start_{l}", _gather_d2d_plan, 15, [], lands)

    def gather_done(l, forward, after):
        send, recv, _, lands, _ = forward
        return _copies_wait(f"forward_wait_{l}", _gather_d2d_plan, send, recv, [], lands, after)[1]

    xs = [x[0]]
    saved = []
    forward = gather_forward(0, gather_start(0, None), [xs[0]])
    for l in range(DEPTH):
        w4, wglu4, wbs4, wba4, wout4 = gather_done(l, forward, [xs[l]])
        started = None
        if l + 1 < DEPTH:
            flight = gather_start(l + 1, wglu4)
            started = flight[4]
        wglu = wglu4.reshape(SSM, SSM)
        wout = wout4.reshape(D, D)
        wb4, wc4 = wb[l], wct[l]
        g1 = pre_norm_g[l].reshape(1, D)
        g2 = post_norm_g[l].reshape(1, D)
        dsk = d_skip[l].reshape(1, SSM)
        bgl = b_glu[l].reshape(1, SSM)
        proj, ht = _rms_proj_fwd(xs[l], g1, w4, started)
        states, yraw = _s5_scan_fwd(proj, wb4, wc4, ar4[l], ai4[l])
        o, lse = _attn_core_fwd(proj)
        forwarded = None
        if l + 1 < DEPTH:
            forward = gather_forward(l + 1, flight, [yraw, o])
            forwarded = forward[4]
        ys = _s5_tail_fwd(yraw, proj, dsk, wglu, bgl)
        ya = _attn_mix_fwd(o, lse, proj)
        xs.append(_merge_out_fwd(ys, ya, proj, wbs4, wba4, wout, g2, xs[l], forwarded))
        saved.append((w4, wglu, wbs4, wba4, wout, wb4, wc4, g1, g2, dsk, bgl, proj, ht, states, yraw, ys, ya, o, lse))

    loss_part, dx = _loss_head(xs[DEPTH], loss_target[0])
    loss = lax.psum(loss_part[0, 0], ("x", "y", "c"))

    big_w = (w_in, w_glu.reshape(DEPTH, 128, SSM), w_branch_s, w_branch_a, w_out)
    big_m = (m_w_in, m_w_glu, m_w_branch_s, m_w_branch_a, m_w_out)
    big_v = (v_w_in, v_w_glu, v_w_branch_s, v_w_branch_a, v_w_out)
    big_out = [None] * 5
    small = {k: [None] * DEPTH for k in ("g1", "da", "dbb", "dcc", "dsk", "bgl", "g2")}
    ici = [None] * DEPTH
    d2d = [None] * DEPTH

    def exchange_start(l, parts):
        lands = [_with_own_slot(lax.dynamic_index_in_dim(p, chip, 0, keepdims=False), chip) for p in parts]
        ici[l] = _copies_start(f"exchange_start_{l}", _exchange_plan, 15, parts, lands)

    def handoff_start(l, after):
        send, recv, srcs, lands, _ = ici[l]
        _, own = _copies_wait(f"exchange_wait_{l}", _exchange_plan, send, recv, srcs, lands, after)
        d2d[l] = _copies_start(f"handoff_start_{l}", _sibling_plan, 5, own,
                               [lax.empty(a.shape, a.dtype) for a in own])

    def update(l, after):
        send, recv, own, lands, _ = d2d[l]
        own, sib = _copies_wait(f"handoff_wait_{l}", _sibling_plan, send, recv, own, lands, after)
        for k in range(5):
            big_out[k] = _adamw_big(l, big_w[k], big_m[k], big_v[k], own[k], sib[k], big_out[k])

    for l in reversed(range(DEPTH)):
        w4, wglu, wbs4, wba4, wout, wb4, wc4, g1, g2, dsk, bgl, proj, ht, states, yraw, ys, ya, o, lse = saved[l]
        started = None
        if l + 1 < DEPTH:
            started = ici[l + 1][4] + d2d[l + 2][4] if l + 2 < DEPTH else ici[l + 1][4]
        dys, dya, dproj, dwbs, dwba, dwout, dg2 = _merge_out_bwd(dx, ys, ya, proj, wbs4, wba4, wout, g2, started)
        dyraw, du_skip, dproj, dwglu, dbgl, ddsk = _s5_tail_bwd(dys, yraw, proj, dsk, wglu, bgl, dproj)
        dproj, dbb, dcc, da = _s5_scan_bwd(dyraw, states, proj, du_skip, wb4, wc4, ar4[l], ai4[l], dproj)
        do, cterm, dproj = _attn_mix_bwd(dya, o, lse, proj, dproj)
        dproj = _attn_core_bwd(proj, do, lse, cterm, dproj)
        dh, dwin = _proj_bwd(dproj, ht, w4)
        parts = [dwin, dwglu.reshape(NCHIP, 128, SSM), dwbs, dwba, dwout.reshape(NCHIP, D // NCHIP, D)]
        exchange_start(l, parts)
        dx, dg1 = _rms_bwd(dh, xs[l], g1, dx, ici[l][4])
        if l + 1 < DEPTH:
            handoff_start(l + 1, [ici[l][4]])
        small["g1"][l], small["g2"][l], small["dsk"][l], small["bgl"][l] = dg1, dg2, ddsk, dbgl
        small["da"][l], small["dbb"][l], small["dcc"][l] = da, dbb, dcc

    dlre, dlim, dldt, dbre_c, dbim_c = _s5_prep_bwd(
        lre_c, lim_c, ldt_c, bre_c, bim_c, jnp.stack(small["da"]).reshape(nb, 1, SW),
        jnp.stack(small["dbb"]).reshape(nb, 16, SW))
    dcc = jnp.stack(small["dcc"]).reshape(nb, 16, SW)
    b_back = lambda t: t.reshape(DEPTH, NCB, 16, 8, 64).transpose(0, 1, 3, 4, 2).reshape(DEPTH, 32, 64, 16)
    c_back = lambda t: t.reshape(DEPTH, NCB, 16, 8, 64).transpose(0, 1, 3, 2, 4).reshape(DEPTH, 32, 16, 64)
    small_w = [pre_norm_g, lambda_re, lambda_im, log_dt, b_re, b_im, c_re, c_im, d_skip, b_glu, post_norm_g]
    small_m = [m_pre_norm_g, m_lambda_re, m_lambda_im, m_log_dt, m_b_re, m_b_im, m_c_re, m_c_im, m_d_skip,
               m_b_glu, m_post_norm_g]
    small_v = [v_pre_norm_g, v_lambda_re, v_lambda_im, v_log_dt, v_b_re, v_b_im, v_c_re, v_c_im, v_d_skip,
               v_b_glu, v_post_norm_g]
    small_g = [jnp.stack(small["g1"]).reshape(DEPTH, D), dlre.reshape(DEPTH, 32, 64), dlim.reshape(DEPTH, 32, 64),
               dldt.reshape(DEPTH, 32, 64).sum(-1), b_back(dbre_c), b_back(dbim_c), c_back(dcc[:, :, :HW]),
               c_back(dcc[:, :, HW:]),
               jnp.stack(small["dsk"]).reshape(DEPTH, SSM), jnp.stack(small["bgl"]).reshape(DEPTH, SSM),
               jnp.stack(small["g2"]).reshape(DEPTH, D)]
    part = _flatten_small(small_g).astype(BF16)
    send, recv, srcs, lands, token = _copies_start("small_start", _everyone_plan, 7, [part],
                                                   [_with_own_slot(part, device, 8)])
    update(DEPTH - 1, [token, ici[0][4]])
    for l in range(DEPTH - 2, 0, -1):
        update(l, [out[0] for out in big_out])
    handoff_start(0, [out[0] for out in big_out])
    update(0, [d2d[0][4]])
    _, (parts8,) = _copies_wait("small_wait", _everyone_plan, send, recv, srcs, lands, [out[0] for out in big_out])
    g_flat, d_flat, nm_flat, nv_flat = _adamw_small(_flatten_small(small_w), parts8, _flatten_small(small_m),
                                                    _flatten_small(small_v))
    sg = _unflatten_small(g_flat, small_w)
    sd = _unflatten_small(d_flat, small_w)
    snm = _unflatten_small(nm_flat, small_w)
    snv = _unflatten_small(nv_flat, small_w)

    def ordered(sm, which):
        bg = [big_out[k][which] for k in range(5)]
        bg[1] = bg[1].reshape(DEPTH, 128, SSM)
        return [sm[0], bg[0], sm[1], sm[2], sm[3], sm[4], sm[5], sm[6], sm[7], sm[8], bg[1], sm[9], bg[2], bg[3],
                bg[4], sm[10]]

    return (loss, dx[None], *ordered(sg, 0), *ordered(sd, 1), *ordered(snm, 2), *ordered(snv, 3))
```

```python
import math

import jax
import jax.numpy as jnp
from jax import lax
from jax.experimental import pallas as pl
from jax.experimental.pallas import tpu as pltpu

F32 = jnp.float32
BF16 = jnp.bfloat16
MESH = pl.DeviceIdType.MESH

DEPTH = 4
L = 2048
D = 1024
NCOL = 8192
SSM = 512
AW = 512
QKV = 1536
RMS_EPS = 1e-6
NCHIP = 4

COL_U, COL_ZS, COL_Q, COL_K, COL_V, COL_ZA, COL_GS, COL_GA = 0, 4, 8, 20, 32, 44, 48, 56

NSEG = 8
SEG = L // NSEG
NCB = 4
HW = 512
SW = 2 * HW

HEAD = 128
DILATIONS = (1, 4, 16)
SCALE = HEAD ** -0.5
NEG = -1e30

ADAM_LR, ADAM_B1, ADAM_B2, ADAM_EPS, ADAM_WD, ADAM_STEP = 0.001, 0.9, 0.999, 1e-08, 0.01, 10


def _sds(shape, dtype=F32):
    return jax.ShapeDtypeStruct(shape, dtype)


def _pcall(body, *, name, out_shape, grid=None, in_specs=None, out_specs=None, scratch=(), vmem_mb=None,
           aliases=None):
    params = {}
    if vmem_mb is not None:
        params["vmem_limit_bytes"] = vmem_mb << 20
    kw = {}
    if grid is not None:
        kw["grid"] = grid
    if in_specs is not None:
        kw["in_specs"] = in_specs
    if out_specs is not None:
        kw["out_specs"] = out_specs
    call = pl.pallas_call(body, name=name, out_shape=out_shape, scratch_shapes=list(scratch),
                          compiler_params=pltpu.CompilerParams(**params),
                          input_output_aliases=aliases or {}, **kw)

    def run(*operands):
        return call(*[pltpu.with_memory_space_constraint(a, pltpu.HBM) for a in operands])

    return run


def _dot(a, b):
    return jnp.dot(a, b, preferred_element_type=F32)


def _dot_nt(a, b):
    return lax.dot_general(a, b, (((1,), (1,)), ((), ())), preferred_element_type=F32)


def _dot_tn(a, b):
    return lax.dot_general(a, b, (((0,), (0,)), ((), ())), preferred_element_type=F32)


def _sigmoid(x):
    return jax.nn.sigmoid(x)


_GELU_K = math.sqrt(2.0 / math.pi)


def _gelu(x):
    return 0.5 * x * (1.0 + jnp.tanh(_GELU_K * (x + 0.044715 * (x * x * x))))


def _gelu_grad(x):
    t = jnp.tanh(_GELU_K * (x + 0.044715 * (x * x * x)))
    return 0.5 * (1.0 + t) + 0.5 * x * (1.0 - t * t) * (_GELU_K * (1.0 + 3.0 * 0.044715 * (x * x)))


def _token_operand(dep):
    if dep is None:
        return [], []
    return [dep], [pl.BlockSpec(memory_space=pl.ANY)]


def _rms_proj_fwd(x, g, w4, dep=None):
    tl, tn = 512, 1024
    ni = L // tl
    deps, dep_specs = _token_operand(dep)

    def body(x_ref, g_ref, w_ref, *rest):
        o_ref, h_ref = rest[len(deps):]
        rows = pl.ds(pl.multiple_of(pl.program_id(1) * tl, tl), tl)

        @pl.when(pl.program_id(0) == 0)
        def _():
            xx = x_ref[...]
            inv = lax.rsqrt(jnp.mean(xx * xx, axis=-1, keepdims=True) + RMS_EPS)
            h_ref[rows, :] = (xx * inv * g_ref[...]).astype(BF16)

        o_ref[...] = _dot(h_ref[rows, :], w_ref[...])

    proj, h = _pcall(
        body, name="rms_proj_fwd", grid=(NCOL // tn, ni),
        in_specs=[pl.BlockSpec((tl, D), lambda j, i: (jnp.where(j == 0, i, ni - 1), 0)),
                  pl.BlockSpec((1, D), lambda j, i: (0, 0)),
                  pl.BlockSpec((None, D, tn), lambda j, i: (lax.div(j, 2), 0, lax.rem(j, 2)))] + dep_specs,
        out_specs=[pl.BlockSpec((tl, tn), lambda j, i: (i, j)), pl.BlockSpec((L, D), lambda j, i: (0, 0))],
        out_shape=[_sds((L, NCOL)), _sds((L, D), BF16)], vmem_mb=48)(x, g, w4, *deps)

    def transpose(h_ref, ht_ref):
        ht_ref[...] = h_ref[...].astype(F32).T.astype(BF16)

    ht = _pcall(transpose, name="transpose_h", grid=(ni,), in_specs=[pl.BlockSpec((tl, D), lambda i: (i, 0))],
                out_specs=pl.BlockSpec((D, tl), lambda i: (0, i)), out_shape=_sds((D, L), BF16), vmem_mb=40)(h)
    return proj, ht


def _proj_bwd(dproj, ht, w4):
    tn = 1024

    def body(dp_ref, ht_ref, w_ref, dh_ref, dw_ref):
        @pl.when(pl.program_id(0) == 0)
        def _():
            dh_ref[...] = jnp.zeros((L, D), F32)

        dw_ref[...] = _dot(ht_ref[...], dp_ref[...]).astype(BF16)
        dh_ref[...] += _dot_nt(dp_ref[...], w_ref[...])

    return _pcall(
        body, name="proj_bwd", grid=(NCOL // tn,),
        in_specs=[pl.BlockSpec((L, tn), lambda j: (0, j)),
                  pl.BlockSpec((D, L), lambda j: (0, 0)),
                  pl.BlockSpec((None, D, tn), lambda j: (lax.div(j, 2), 0, lax.rem(j, 2)))],
        out_specs=[pl.BlockSpec((L, D), lambda j: (0, 0)),
                   pl.BlockSpec((None, D, tn), lambda j: (lax.div(j, 2), 0, lax.rem(j, 2)))],
        out_shape=[_sds((L, D)), _sds((NCHIP, D, NCOL // NCHIP), BF16)], vmem_mb=56)(dproj, ht, w4)


def _rms_bwd(dh, x, g, dxn, dep=None):
    tl = 512
    deps, dep_specs = _token_operand(dep)

    def body(dh_ref, x_ref, g_ref, dxn_ref, *rest):
        dx_ref, dg_ref = rest[len(deps):]
        i = pl.program_id(0)
        xx = x_ref[...]
        inv = lax.rsqrt(jnp.mean(xx * xx, axis=-1, keepdims=True) + RMS_EPS)
        nrm = xx * inv
        dh_v = dh_ref[...]
        dn = dh_v * g_ref[...]
        dx_ref[...] = dxn_ref[...] + inv * (dn - nrm * jnp.mean(dn * nrm, axis=-1, keepdims=True))
        part = jnp.sum(dh_v * nrm, axis=0, keepdims=True)

        @pl.when(i == 0)
        def _():
            dg_ref[...] = part

        @pl.when(i > 0)
        def _():
            dg_ref[...] += part

    row = pl.BlockSpec((tl, D), lambda i: (i, 0))
    vec = pl.BlockSpec((1, D), lambda i: (0, 0))
    return _pcall(body, name="rms_bwd", grid=(L // tl,), in_specs=[row, row, vec, row] + dep_specs,
                  out_specs=[row, vec], out_shape=[_sds((L, D)), _sds((1, D))], vmem_mb=40)(dh, x, g, dxn, *deps)


def _s5_param_math(lre, lim, ldt, bre, bim):
    lr = jnp.minimum(lre, -1e-4)
    dt = jnp.exp(ldt)
    mag = jnp.exp(lr * dt)
    ar = mag * jnp.cos(lim * dt)
    ai = mag * jnp.sin(lim * dt)
    den = lr * lr + lim * lim
    nr = ar - 1.0
    qr = (nr * lr + ai * lim) / den
    qi = (ai * lr - nr * lim) / den
    return ar, ai, qr * bre - qi * bim, qr * bim + qi * bre


def _group_of_lane(width):
    lane = lax.broadcasted_iota(jnp.int32, (16, width), 1)
    return lax.shift_right_logical(jnp.bitwise_and(lane, HW - 1), 6)


def _vec_spec():
    return pl.BlockSpec((None, 1, HW), lambda n: (n, 0, 0))


def _mat_spec():
    return pl.BlockSpec((None, 16, HW), lambda n: (n, 0, 0))


def _s5_prep(lre, lim, ldt, bre, bim, cre, cim):
    n = lre.shape[0]

    def body(lre_ref, lim_ref, ldt_ref, bre_ref, bim_ref, cre_ref, cim_ref, ar_ref, ai_ref, wb_ref, wct_ref):
        ar, ai, bbr, bbi = _s5_param_math(lre_ref[...], lim_ref[...], ldt_ref[...], bre_ref[...], bim_ref[...])
        ar_ref[...] = ar
        ai_ref[...] = ai
        group = _group_of_lane(SW)
        bb = jnp.concatenate([bbr, bbi], axis=1)
        cc = jnp.concatenate([cre_ref[...], -cim_ref[...]], axis=1)
        for g in range(8):
            rows = pl.ds(16 * g, 16)
            wb_ref[rows, :] = jnp.where(group == g, bb, 0.0).astype(BF16)
            wct_ref[rows, :] = jnp.where(group == g, cc, 0.0).astype(BF16)

    wide = pl.BlockSpec((None, 128, SW), lambda n: (n, 0, 0))
    return _pcall(body, name="s5_prep", grid=(n,),
                  in_specs=[_vec_spec()] * 3 + [_mat_spec()] * 4,
                  out_specs=[_vec_spec(), _vec_spec(), wide, wide],
                  out_shape=[_sds((n, 1, HW)), _sds((n, 1, HW)), _sds((n, 128, SW), BF16), _sds((n, 128, SW), BF16)])(
                      lre, lim, ldt, bre, bim, cre, cim)


def _s5_prep_bwd(lre, lim, ldt, bre, bim, ga, gbb):
    n = lre.shape[0]

    def body(lre_ref, lim_ref, ldt_ref, bre_ref, bim_ref, ga_ref, gbb_ref,
             dlre_ref, dlim_ref, dldt_ref, dbre_ref, dbim_ref):
        _, vjp = jax.vjp(_s5_param_math, lre_ref[...], lim_ref[...], ldt_ref[...], bre_ref[...], bim_ref[...])
        ga, gbb = ga_ref[...], gbb_ref[...]
        dlre, dlim, dldt, dbre, dbim = vjp((ga[:, :HW], ga[:, HW:], gbb[:, :HW], gbb[:, HW:]))
        dlre_ref[...] = dlre
        dlim_ref[...] = dlim
        dldt_ref[...] = dldt
        dbre_ref[...] = dbre
        dbim_ref[...] = dbim

    return _pcall(body, name="s5_prep_bwd", grid=(n,),
                  in_specs=[_vec_spec()] * 3 + [_mat_spec()] * 2 +
                  [pl.BlockSpec((None, 1, SW), lambda n: (n, 0, 0)), pl.BlockSpec((None, 16, SW), lambda n: (n, 0, 0))],
                  out_specs=[_vec_spec()] * 3 + [_mat_spec()] * 2,
                  out_shape=[_sds((n, 1, HW))] * 3 + [_sds((n, 16, HW))] * 2)(lre, lim, ldt, bre, bim, ga, gbb)


def _diag_blocks(full):
    group = _group_of_lane(SW)
    out = jnp.where(group == 0, full[0:16, :], 0.0)
    for g in range(1, 8):
        out = out + jnp.where(group == g, full[16 * g:16 * g + 16, :], 0.0)
    return out


def _permute_rows(dst_ref, src_ref):
    for s in range(NSEG):
        dst_ref[pl.ds(s, SEG, stride=NSEG), :] = src_ref[pl.ds(s * SEG, SEG), :]


def _unpermute_rows(src_ref, s):
    return src_ref[pl.ds(s, SEG, stride=NSEG), :]


def _row_of(slab, rows, j):
    return jnp.sum(jnp.where(rows == j, slab, 0.0), axis=0, keepdims=True)


def _s5_scan_fwd(proj, wb4, wc4, ar4, ai4):
    def body(u_ref, wb_ref, wc_ref, ar_ref, ai_ref, x_ref, y_ref, up_ref, d_ref, yp_ref):
        _permute_rows(up_ref, u_ref)
        d_ref[...] = _dot(up_ref[...].astype(BF16), wb_ref[...])
        ar = jnp.broadcast_to(ar_ref[...], (NSEG, HW))
        ai = jnp.broadcast_to(ai_ref[...], (NSEG, HW))
        zero = jnp.zeros((NSEG, HW), F32)

        def drive(t):
            row = pl.multiple_of(t * NSEG, NSEG)
            dd = d_ref[pl.ds(row, NSEG), :]
            return row, dd[:, :HW], dd[:, HW:]

        def local_step(t, c):
            xr, xi = c
            _, dr, di = drive(t)
            return ar * xr - ai * xi + dr, ar * xi + ai * xr + di

        fr, fi = lax.fori_loop(0, SEG, local_step, (zero, zero), unroll=8)
        pr, pi_ = ar_ref[...], ai_ref[...]
        for _ in range(int(math.log2(SEG))):
            pr, pi_ = pr * pr - pi_ * pi_, 2.0 * pr * pi_
        rows = lax.broadcasted_iota(jnp.int32, (NSEG, HW), 0)
        cr, ci = zero, zero
        for j in range(NSEG - 1):
            sr, si = _row_of(cr, rows, j), _row_of(ci, rows, j)
            gr, gi = _row_of(fr, rows, j), _row_of(fi, rows, j)
            nr = pr * sr - pi_ * si + gr
            ni = pr * si + pi_ * sr + gi
            cr = jnp.where(rows == j + 1, nr, cr)
            ci = jnp.where(rows == j + 1, ni, ci)

        def true_step(t, c):
            xr, xi = c
            row, dr, di = drive(t)
            nr = ar * xr - ai * xi + dr
            ni = ar * xi + ai * xr + di
            x_ref[pl.ds(row, NSEG), :] = jnp.concatenate([nr, ni], axis=1)
            return nr, ni

        lax.fori_loop(0, SEG, true_step, (cr, ci), unroll=8)
        yp_ref[...] = _dot_nt(x_ref[...].astype(BF16), wc_ref[...])
        for s in range(NSEG):
            y_ref[pl.ds(s * SEG, SEG), :] = _unpermute_rows(yp_ref, s)

    return _pcall(
        body, name="s5_scan_fwd", grid=(NCB,),
        in_specs=[pl.BlockSpec((L, 128), lambda b: (0, COL_U + b)),
                  pl.BlockSpec((None, 128, SW), lambda b: (b, 0, 0)),
                  pl.BlockSpec((None, 128, SW), lambda b: (b, 0, 0)),
                  pl.BlockSpec((None, 1, HW), lambda b: (b, 0, 0)),
                  pl.BlockSpec((None, 1, HW), lambda b: (b, 0, 0))],
        out_specs=[pl.BlockSpec((None, L, SW), lambda b: (b, 0, 0)),
                   pl.BlockSpec((L, 128), lambda b: (0, b))],
        out_shape=[_sds((NCB, L, SW)), _sds((L, SSM))],
        scratch=[pltpu.VMEM((L, 128), F32), pltpu.VMEM((L, SW), F32), pltpu.VMEM((L, 128), F32)],
        vmem_mb=56)(proj, wb4, wc4, ar4, ai4)


def _s5_scan_bwd(dy, xs, proj, du_skip, wb4, wc4, ar4, ai4, dproj):
    def body(dy_ref, x_ref, u_ref, dus_ref, wb_ref, wc_ref, ar_ref, ai_ref, _,
             du_ref, dbb_ref, dcc_ref, da_ref, dyp_ref, up_ref, g_ref, dup_ref):
        _permute_rows(dyp_ref, dy_ref)
        _permute_rows(up_ref, u_ref)
        dyp = dyp_ref[...].astype(BF16)
        g_ref[...] = _dot(dyp, wc_ref[...])
        dcc = _diag_blocks(_dot_tn(dyp, x_ref[...].astype(BF16)))
        dcc_ref[...] = jnp.concatenate([dcc[:, :HW], -dcc[:, HW:]], axis=1)
        ar = jnp.broadcast_to(ar_ref[...], (NSEG, HW))
        ai = jnp.broadcast_to(ai_ref[...], (NSEG, HW))
        zero = jnp.zeros((NSEG, HW), F32)

        def load(t):
            row = pl.multiple_of(t * NSEG, NSEG)
            gg = g_ref[pl.ds(row, NSEG), :]
            return row, gg[:, :HW], gg[:, HW:]

        def local_step(k, c):
            lr, li = c
            _, gr, gi = load(SEG - 1 - k)
            return ar * lr + ai * li + gr, ar * li - ai * lr + gi

        fr, fi = lax.fori_loop(0, SEG, local_step, (zero, zero), unroll=8)
        pr, pi_ = ar_ref[...], -ai_ref[...]
        for _ in range(int(math.log2(SEG))):
            pr, pi_ = pr * pr - pi_ * pi_, 2.0 * pr * pi_
        rows = lax.broadcasted_iota(jnp.int32, (NSEG, HW), 0)
        cr, ci = zero, zero
        for j in range(NSEG - 1, 0, -1):
            sr, si = _row_of(cr, rows, j), _row_of(ci, rows, j)
            gr, gi = _row_of(fr, rows, j), _row_of(fi, rows, j)
            nr = pr * sr - pi_ * si + gr
            ni = pr * si + pi_ * sr + gi
            cr = jnp.where(rows == j - 1, nr, cr)
            ci = jnp.where(rows == j - 1, ni, ci)

        def true_step(k, c):
            lr, li, gar, gai = c
            t = SEG - 1 - k
            row, gr, gi = load(t)
            nr = ar * lr + ai * li + gr
            ni = ar * li - ai * lr + gi
            g_ref[pl.ds(row, NSEG), :] = jnp.concatenate([nr, ni], axis=1)
            prow = pl.multiple_of(jnp.maximum(t - 1, 0) * NSEG, NSEG)
            xp = x_ref[pl.ds(prow, NSEG), :]
            live = (t > 0).astype(F32)
            xr, xi = xp[:, :HW] * live, xp[:, HW:] * live
            return nr, ni, gar + (nr * xr + ni * xi), gai + (ni * xr - nr * xi)

        _, _, gar, gai = lax.fori_loop(0, SEG, true_step, (cr, ci, zero, zero), unroll=4)
        l0 = g_ref[1:NSEG, :]
        xl = x_ref[(SEG - 1) * NSEG:(SEG - 1) * NSEG + NSEG - 1, :]
        l0r, l0i, xlr, xli = l0[:, :HW], l0[:, HW:], xl[:, :HW], xl[:, HW:]
        gar_t = jnp.sum(gar, axis=0, keepdims=True) + jnp.sum(l0r * xlr + l0i * xli, axis=0, keepdims=True)
        gai_t = jnp.sum(gai, axis=0, keepdims=True) + jnp.sum(l0i * xlr - l0r * xli, axis=0, keepdims=True)
        da_ref[...] = jnp.concatenate([gar_t, gai_t], axis=1)
        lam = g_ref[...].astype(BF16)
        dbb_ref[...] = _diag_blocks(_dot_tn(up_ref[...].astype(BF16), lam))
        dup_ref[...] = _dot_nt(lam, wb_ref[...])
        for s in range(NSEG):
            sl = pl.ds(s * SEG, SEG)
            du_ref[sl, :] = (_unpermute_rows(dup_ref, s) + dus_ref[sl, :]).astype(BF16)

    col = lambda off: pl.BlockSpec((L, 128), lambda b: (0, off + b))
    return _pcall(
        body, name="s5_scan_bwd", grid=(NCB,),
        in_specs=[col(0), pl.BlockSpec((None, L, SW), lambda b: (b, 0, 0)), col(COL_U), col(0),
                  pl.BlockSpec((None, 128, SW), lambda b: (b, 0, 0)),
                  pl.BlockSpec((None, 128, SW), lambda b: (b, 0, 0)),
                  pl.BlockSpec((None, 1, HW), lambda b: (b, 0, 0)),
                  pl.BlockSpec((None, 1, HW), lambda b: (b, 0, 0)),
                  pl.BlockSpec(memory_space=pl.ANY)],
        out_specs=[col(COL_U), pl.BlockSpec((None, 16, SW), lambda b: (b, 0, 0)),
                   pl.BlockSpec((None, 16, SW), lambda b: (b, 0, 0)),
                   pl.BlockSpec((None, 1, SW), lambda b: (b, 0, 0))],
        out_shape=[_sds((L, NCOL), BF16), _sds((NCB, 16, SW)), _sds((NCB, 16, SW)), _sds((NCB, 1, SW))],
        scratch=[pltpu.VMEM((L, 128), F32), pltpu.VMEM((L, 128), F32), pltpu.VMEM((L, SW), F32),
                 pltpu.VMEM((L, 128), F32)],
        aliases={8: 0}, vmem_mb=56)(dy, xs, proj, du_skip, wb4, wc4, ar4, ai4, dproj)


def _s5_tail_fwd(yraw, proj, dsk, wglu, bglu):
    tl = 512

    def body(y_ref, u_ref, z_ref, dsk_ref, w_ref, b_ref, o_ref):
        y1 = y_ref[...] + dsk_ref[...] * u_ref[...]
        y2 = _gelu(y1)
        gl = _dot(y2.astype(BF16), w_ref[...]) + b_ref[...]
        z = z_ref[...]
        o_ref[...] = (y2 * _sigmoid(gl)) * (z * _sigmoid(z))

    blk = lambda c: pl.BlockSpec((tl, SSM), lambda i: (i, c))
    vec = pl.BlockSpec((1, SSM), lambda i: (0, 0))
    return _pcall(body, name="s5_tail_fwd", grid=(L // tl,),
                  in_specs=[blk(0), blk(0), blk(1), vec, pl.BlockSpec((SSM, SSM), lambda i: (0, 0)), vec],
                  out_specs=blk(0), out_shape=_sds((L, SSM)), vmem_mb=40)(yraw, proj, proj, dsk, wglu, bglu)


def _s5_tail_bwd(dys, yraw, proj, dsk, wglu, bglu, dproj):
    tl = 512
    ni = L // tl

    def body(dys_ref, y_ref, u_ref, z_ref, dsk_ref, w_ref, b_ref, _,
             dy_ref, dus_ref, dz_ref, dw_ref, db_ref, dd_ref, acc_ref):
        i = pl.program_id(0)
        u = u_ref[...]
        y1 = y_ref[...] + dsk_ref[...] * u
        y2 = _gelu(y1)
        y2b = y2.astype(BF16)
        sg = _sigmoid(_dot(y2b, w_ref[...]) + b_ref[...])
        y3 = y2 * sg
        z = z_ref[...]
        sz = _sigmoid(z)
        dys = dys_ref[...]
        dy3 = dys * (z * sz)
        dz_ref[...] = (dys * y3 * (sz * (1.0 + z * (1.0 - sz)))).astype(BF16)
        dgl = (dy3 * y2) * (sg * (1.0 - sg))
        dglb = dgl.astype(BF16)
        dy2 = dy3 * sg + _dot_nt(dglb, w_ref[...])
        dy1 = dy2 * _gelu_grad(y1)
        dy_ref[...] = dy1
        dus_ref[...] = dsk_ref[...] * dy1
        dw = _dot_tn(y2b, dglb)
        db = jnp.sum(dgl, axis=0, keepdims=True)
        dd = jnp.sum(dy1 * u, axis=0, keepdims=True)

        @pl.when(i == 0)
        def _():
            acc_ref[...] = dw
            db_ref[...] = db
            dd_ref[...] = dd

        @pl.when(i > 0)
        def _():
            acc_ref[...] += dw
            db_ref[...] += db
            dd_ref[...] += dd

        @pl.when(i == ni - 1)
        def _():
            dw_ref[...] = acc_ref[...].astype(BF16)

    blk = lambda c: pl.BlockSpec((tl, SSM), lambda i: (i, c))
    vec = pl.BlockSpec((1, SSM), lambda i: (0, 0))
    mat = pl.BlockSpec((SSM, SSM), lambda i: (0, 0))
    return _pcall(body, name="s5_tail_bwd", grid=(ni,),
                  in_specs=[blk(0), blk(0), blk(0), blk(1), vec, mat, vec, pl.BlockSpec(memory_space=pl.ANY)],
                  out_specs=[blk(0), blk(0), blk(COL_ZS // 4), mat, vec, vec],
                  out_shape=[_sds((L, SSM)), _sds((L, SSM)), _sds((L, NCOL), BF16), _sds((SSM, SSM), BF16),
                             _sds((1, SSM)), _sds((1, SSM))],
                  scratch=[pltpu.VMEM((SSM, SSM), F32)], aliases={7: 2},
                  vmem_mb=40)(dys, yraw, proj, proj, dsk, wglu, bglu, dproj)


def _attn_blocks(dil):
    n = L // dil
    return [(i * HEAD * dil + r, r * n + i * HEAD, i == 0) for r in range(dil) for i in range(n // HEAD)]


def _rows(start, dil):
    return pl.ds(start, HEAD) if dil == 1 else pl.ds(start, HEAD, stride=dil)


def _residue_rows(r, dil):
    n = L // dil
    return pl.ds(0, L) if dil == 1 else pl.ds(r, n, stride=dil)


def _gather_residues(pairs, dil, dtype=BF16):
    n = L // dil
    for src, dst in pairs:
        for r in range(dil):
            dst[pl.ds(r * n, n), :] = src[_residue_rows(r, dil), :].astype(dtype)


def _band_masks():
    ri = lax.broadcasted_iota(jnp.int32, (HEAD, HEAD), 0)
    ci = lax.broadcasted_iota(jnp.int32, (HEAD, HEAD), 1)
    ri2 = lax.broadcasted_iota(jnp.int32, (HEAD, 2 * HEAD), 0)
    ci2 = lax.broadcasted_iota(jnp.int32, (HEAD, 2 * HEAD), 1)
    return ci <= ri, jnp.logical_and(ci2 >= ri2, ci2 - HEAD <= ri2)


def _qkv_specs(index):
    def spec(off):
        return pl.BlockSpec((L, HEAD), lambda gi, h: (0, off + index(gi, h)))
    return [spec(off) for off in (COL_Q, COL_K, COL_V)]


def _attn_core_fwd(proj):
    def body(q_ref, k_ref, v_ref, o_ref, lse_ref, qp_ref, kp_ref, vp_ref):
        gi = pl.program_id(0)
        m_cur, m_both = _band_masks()
        for g, dil in enumerate(DILATIONS):
            @pl.when(gi == g)
            def _(dil=dil):
                _gather_residues(((q_ref, qp_ref), (k_ref, kp_ref), (v_ref, vp_ref)), dil)
                for st, base, first in _attn_blocks(dil):
                    q = qp_ref[pl.ds(base, HEAD), :]
                    keys = pl.ds(base, HEAD) if first else pl.ds(base - HEAD, 2 * HEAD)
                    s = jnp.where(m_cur if first else m_both, _dot_nt(q, kp_ref[keys, :]) * SCALE, NEG)
                    mx = jnp.max(s, axis=-1, keepdims=True)
                    p = jnp.exp(s - mx)
                    den = jnp.sum(p, axis=-1, keepdims=True)
                    r = _rows(st, dil)
                    o_ref[r, :] = _dot(p.astype(BF16), vp_ref[keys, :]) / den
                    lse_ref[r, :] = jnp.broadcast_to(mx + jnp.log(den), (HEAD, HEAD))

    idx = lambda gi, h: gi * 4 + h
    out = pl.BlockSpec((L, HEAD), lambda gi, h: (0, gi * 4 + h))
    return _pcall(body, name="attn_core_fwd", grid=(3, 4), in_specs=_qkv_specs(idx), out_specs=[out, out],
                  out_shape=[_sds((L, QKV)), _sds((L, QKV))], scratch=[pltpu.VMEM((L, HEAD), BF16)] * 3,
                  vmem_mb=40)(proj, proj, proj)


def _attn_mix_fwd(o, lse, proj):
    tl = 512

    def body(o_ref, l_ref, z_ref, y_ref):
        for h in range(4):
            c = [pl.ds((g * 4 + h) * HEAD, HEAD) for g in range(3)]
            ls = [l_ref[:, c[g]] for g in range(3)]
            m = jnp.maximum(jnp.maximum(ls[0], ls[1]), ls[2])
            e = [jnp.exp(ls[g] - m) for g in range(3)]
            y = (e[0] * o_ref[:, c[0]] + e[1] * o_ref[:, c[1]] + e[2] * o_ref[:, c[2]]) / (e[0] + e[1] + e[2])
            z = z_ref[:, pl.ds(h * HEAD, HEAD)]
            y_ref[:, pl.ds(h * HEAD, HEAD)] = y * (z * _sigmoid(z))

    wide = pl.BlockSpec((tl, QKV), lambda i: (i, 0))
    return _pcall(body, name="attn_mix_fwd", grid=(L // tl,),
                  in_specs=[wide, wide, pl.BlockSpec((tl, AW), lambda i: (i, COL_ZA // 4))],
                  out_specs=pl.BlockSpec((tl, AW), lambda i: (i, 0)), out_shape=_sds((L, AW)),
                  vmem_mb=40)(o, lse, proj)


def _attn_mix_bwd(dya, o, lse, proj, dproj):
    tl = 512

    def body(dya_ref, o_ref, l_ref, z_ref, _, do_ref, c_ref, dz_ref):
        for h in range(4):
            c = [pl.ds((g * 4 + h) * HEAD, HEAD) for g in range(3)]
            hs = pl.ds(h * HEAD, HEAD)
            ls = [l_ref[:, c[g]] for g in range(3)]
            m = jnp.maximum(jnp.maximum(ls[0], ls[1]), ls[2])
            e = [jnp.exp(ls[g] - m) for g in range(3)]
            den = e[0] + e[1] + e[2]
            al = [e[g] / den for g in range(3)]
            y = al[0] * o_ref[:, c[0]] + al[1] * o_ref[:, c[1]] + al[2] * o_ref[:, c[2]]
            z = z_ref[:, hs]
            sz = _sigmoid(z)
            dya = dya_ref[:, hs]
            dz_ref[:, hs] = (dya * y * (sz * (1.0 + z * (1.0 - sz)))).astype(BF16)
            dy = dya * (z * sz)
            tot = jnp.sum(dy * y, axis=-1, keepdims=True)
            for g in range(3):
                do_ref[:, c[g]] = al[g] * dy
                c_ref[:, c[g]] = -(al[g] * tot)

    wide = pl.BlockSpec((tl, QKV), lambda i: (i, 0))
    nar = pl.BlockSpec((tl, AW), lambda i: (i, 0))
    za = pl.BlockSpec((tl, AW), lambda i: (i, COL_ZA // 4))
    return _pcall(body, name="attn_mix_bwd", grid=(L // tl,),
                  in_specs=[nar, wide, wide, za, pl.BlockSpec(memory_space=pl.ANY)],
                  out_specs=[wide, wide, za], out_shape=[_sds((L, QKV)), _sds((L, QKV)), _sds((L, NCOL), BF16)],
                  aliases={4: 2}, vmem_mb=48)(dya, o, lse, proj, dproj)


def _attn_core_bwd(proj, do, lse, cc, dproj):
    def body(q_ref, k_ref, v_ref, do_ref, lse_ref, c_ref, _, dp_ref,
             qp_ref, kp_ref, vp_ref, dop_ref, lsep_ref, cp_ref, dqp_ref, dkp_ref, dvp_ref, tok_ref, out_ref, sems):
        gi = pl.program_id(0)
        head = gi * 4 + pl.program_id(1)
        m_cur, m_both = _band_masks()
        dkp_ref[...] = jnp.zeros((L, HEAD), F32)
        dvp_ref[...] = jnp.zeros((L, HEAD), F32)
        copies = []
        for g, dil in enumerate(DILATIONS):
            @pl.when(gi == g)
            def _(dil=dil):
                _gather_residues(((q_ref, qp_ref), (k_ref, kp_ref), (v_ref, vp_ref), (do_ref, dop_ref)), dil)
                _gather_residues(((lse_ref, lsep_ref), (c_ref, cp_ref)), dil, F32)
                for _, base, first in _attn_blocks(dil):
                    mine = pl.ds(base, HEAD)
                    keys = mine if first else pl.ds(base - HEAD, 2 * HEAD)
                    q, do_b, kb = qp_ref[mine, :], dop_ref[mine, :], kp_ref[keys, :]
                    lse_b, c_b = lsep_ref[mine, :], cp_ref[mine, :]
                    if not first:
                        lse_b = jnp.concatenate([lse_b, lse_b], axis=1)
                        c_b = jnp.concatenate([c_b, c_b], axis=1)
                    s = _dot_nt(q, kb) * SCALE
                    p = jnp.where(m_cur if first else m_both, jnp.exp(s - lse_b), 0.0)
                    ds = (p * (_dot_nt(do_b, vp_ref[keys, :]) + c_b) * SCALE).astype(BF16)
                    dqp_ref[mine, :] = _dot(ds, kb)
                    dkp_ref[keys, :] += _dot_tn(ds, q)
                    dvp_ref[keys, :] += _dot_tn(p.astype(BF16), do_b)
                n = L // dil
                for slot, src in enumerate((dqp_ref, dkp_ref, dvp_ref)):
                    for r in range(dil):
                        tok_ref[_residue_rows(r, dil), :] = src[pl.ds(r * n, n), :]
                    out_ref[slot] = tok_ref[...].astype(BF16)

        for slot, off in enumerate((COL_Q, COL_K, COL_V)):
            cols = pl.ds(pl.multiple_of((off + head) * HEAD, HEAD), HEAD)
            copies.append(pltpu.make_async_copy(out_ref.at[slot], dp_ref.at[:, cols], sems.at[slot]))
            copies[-1].start()
        for cp in copies:
            cp.wait()

    idx = lambda gi, h: gi * 4 + h
    blk = pl.BlockSpec((L, HEAD), lambda gi, h: (0, gi * 4 + h))
    hbm = pl.BlockSpec(memory_space=pl.ANY)
    return _pcall(body, name="attn_core_bwd", grid=(3, 4), in_specs=_qkv_specs(idx) + [blk, blk, blk, hbm],
                  out_specs=hbm, out_shape=_sds((L, NCOL), BF16),
                  scratch=[pltpu.VMEM((L, HEAD), BF16)] * 4 + [pltpu.VMEM((L, HEAD), F32)] * 6 +
                  [pltpu.VMEM((3, L, HEAD), BF16), pltpu.SemaphoreType.DMA((3,))],
                  aliases={6: 0}, vmem_mb=48)(proj, proj, proj, do, lse, cc, dproj)


def _merge_fwd_math(ys_b, ya_b, gs, ga, wbs_ref, wba_ref, wout_ref):
    bs = jnp.concatenate([_dot(ys_b, wbs_ref[s]) for s in range(NCHIP)], axis=1)
    ba = jnp.concatenate([_dot(ya_b, wba_ref[s]) for s in range(NCHIP)], axis=1)
    sgs, sga = _sigmoid(gs), _sigmoid(ga)
    merged = sgs * bs + sga * ba
    out = _dot(merged.astype(BF16), wout_ref[...])
    inv = lax.rsqrt(jnp.mean(out * out, axis=-1, keepdims=True) + RMS_EPS)
    return bs, ba, sgs, sga, merged, out, inv


def _merge_specs(tl):
    row = lambda w, c: pl.BlockSpec((tl, w), lambda i: (i, c))
    return [row(SSM, 0), row(AW, 0), row(D, COL_GS // 8), row(D, COL_GA // 8),
            pl.BlockSpec((NCHIP, SSM, D // NCHIP), lambda i: (0, 0, 0)),
            pl.BlockSpec((NCHIP, AW, D // NCHIP), lambda i: (0, 0, 0)),
            pl.BlockSpec((D, D), lambda i: (0, 0)),
            pl.BlockSpec((1, D), lambda i: (0, 0))]


def _merge_out_fwd(ys, ya, proj, wbs4, wba4, wout, g2, x, dep=None):
    tl = 256
    deps, dep_specs = _token_operand(dep)

    def body(ys_ref, ya_ref, gs_ref, ga_ref, wbs_ref, wba_ref, wout_ref, g2_ref, x_ref, *rest):
        o_ref = rest[-1]
        *_, out, inv = _merge_fwd_math(ys_ref[...].astype(BF16), ya_ref[...].astype(BF16), gs_ref[...],
                                       ga_ref[...], wbs_ref, wba_ref, wout_ref)
        o_ref[...] = x_ref[...] + out * inv * g2_ref[...]

    row = pl.BlockSpec((tl, D), lambda i: (i, 0))
    return _pcall(body, name="merge_out_fwd", grid=(L // tl,), in_specs=_merge_specs(tl) + [row] + dep_specs,
                  out_specs=row, out_shape=_sds((L, D)),
                  vmem_mb=48)(ys, ya, proj, proj, wbs4, wba4, wout, g2, x, *deps)


def _merge_out_bwd(dxn, ys, ya, proj, wbs4, wba4, wout, g2, dep=None):
    tl = 256
    ni = L // tl
    cw = D // NCHIP
    deps, dep_specs = _token_operand(dep)

    def body(dxn_ref, ys_ref, ya_ref, gs_ref, ga_ref, wbs_ref, wba_ref, wout_ref, g2_ref, *rest):
        (dys_ref, dya_ref, dgate_ref, dwbs_ref, dwba_ref, dwout_ref, dg2_ref,
         abs_ref, aba_ref, aout_ref) = rest[len(deps):]
        i = pl.program_id(0)

        @pl.when(i == 0)
        def _():
            abs_ref[...] = jnp.zeros(abs_ref.shape, F32)
            aba_ref[...] = jnp.zeros(aba_ref.shape, F32)
            aout_ref[...] = jnp.zeros(aout_ref.shape, F32)
            dg2_ref[...] = jnp.zeros(dg2_ref.shape, F32)

        ys_b, ya_b = ys_ref[...].astype(BF16), ya_ref[...].astype(BF16)
        bs, ba, sgs, sga, merged, out, inv = _merge_fwd_math(ys_b, ya_b, gs_ref[...], ga_ref[...],
                                                             wbs_ref, wba_ref, wout_ref)
        nrm = out * inv
        dxn = dxn_ref[...]
        dg2_ref[...] += jnp.sum(dxn * nrm, axis=0, keepdims=True)
        dn = dxn * g2_ref[...]
        dout = (inv * (dn - nrm * jnp.mean(dn * nrm, axis=-1, keepdims=True))).astype(BF16)
        aout_ref[...] += _dot_tn(merged.astype(BF16), dout)
        dm = _dot_nt(dout, wout_ref[...])
        dbs, dba = dm * sgs, dm * sga
        dgate_ref[:, :D] = (dm * bs * (sgs * (1.0 - sgs))).astype(BF16)
        dgate_ref[:, D:] = (dm * ba * (sga * (1.0 - sga))).astype(BF16)
        dbs_b, dba_b = dbs.astype(BF16), dba.astype(BF16)
        dys = None
        dya = None
        for s in range(NCHIP):
            cs = slice(s * cw, (s + 1) * cw)
            p_s = _dot_nt(dbs_b[:, cs], wbs_ref[s])
            p_a = _dot_nt(dba_b[:, cs], wba_ref[s])
            dys = p_s if dys is None else dys + p_s
            dya = p_a if dya is None else dya + p_a
            abs_ref[s] += _dot_tn(ys_b, dbs_b[:, cs])
            aba_ref[s] += _dot_tn(ya_b, dba_b[:, cs])

        dys_ref[...] = dys
        dya_ref[...] = dya

        @pl.when(i == ni - 1)
        def _():
            dwbs_ref[...] = abs_ref[...].astype(BF16)
            dwba_ref[...] = aba_ref[...].astype(BF16)
            dwout_ref[...] = aout_ref[...].astype(BF16)

    row = lambda w: pl.BlockSpec((tl, w), lambda i: (i, 0))
    w4 = pl.BlockSpec((NCHIP, SSM, cw), lambda i: (0, 0, 0))
    sq = pl.BlockSpec((D, D), lambda i: (0, 0))
    vec = pl.BlockSpec((1, D), lambda i: (0, 0))
    return _pcall(body, name="merge_out_bwd", grid=(ni,), in_specs=[row(D)] + _merge_specs(tl) + dep_specs,
                  out_specs=[row(SSM), row(AW), pl.BlockSpec((tl, 2 * D), lambda i: (i, COL_GS // 16)), w4, w4, sq,
                             vec],
                  out_shape=[_sds((L, SSM)), _sds((L, AW)), _sds((L, NCOL), BF16),
                             _sds((NCHIP, SSM, cw), BF16), _sds((NCHIP, AW, cw), BF16), _sds((D, D), BF16),
                             _sds((1, D))],
                  scratch=[pltpu.VMEM((NCHIP, SSM, cw), F32), pltpu.VMEM((NCHIP, AW, cw), F32),
                           pltpu.VMEM((D, D), F32)],
                  vmem_mb=56)(dxn, ys, ya, proj, proj, wbs4, wba4, wout, g2, *deps)


def _loss_head(y, target):
    tl = 512

    def body(y_ref, t_ref, loss_ref, dy_ref):
        i = pl.program_id(0)
        err = y_ref[...] - t_ref[...]
        dy_ref[...] = err / D
        part = 0.5 * jnp.sum(jnp.mean(err * err, axis=-1, keepdims=True), axis=0, keepdims=True)
        part = jnp.broadcast_to(part, (8, 128))

        @pl.when(i == 0)
        def _():
            loss_ref[...] = part

        @pl.when(i > 0)
        def _():
            loss_ref[...] += part

    row = pl.BlockSpec((tl, D), lambda i: (i, 0))
    return _pcall(body, name="loss_head", grid=(L // tl,), in_specs=[row, row],
                  out_specs=[pl.BlockSpec((8, 128), lambda i: (0, 0)), row],
                  out_shape=[_sds((8, 128)), _sds((L, D))], vmem_mb=40)(y, target)


def _adamw_math(w, g, m, v):
    m = ADAM_B1 * m + (1.0 - ADAM_B1) * g
    v = ADAM_B2 * v + (1.0 - ADAM_B2) * (g * g)
    m_hat = m / (1.0 - ADAM_B1 ** ADAM_STEP)
    v_hat = v / (1.0 - ADAM_B2 ** ADAM_STEP)
    delta = -ADAM_LR * (m_hat / (jnp.sqrt(v_hat) + ADAM_EPS) + ADAM_WD * w)
    return delta, m, v


def _adamw_big(layer, w, m, v, own, sib, prev):
    _, r, c = w.shape
    tr = min(r, 128)

    def body(w_ref, m_ref, v_ref, own_ref, sib_ref, *rest):
        g_ref, d_ref, nm_ref, nv_ref = rest[-4:]
        a = own_ref[0].astype(F32)
        b = sib_ref[0].astype(F32)
        for s in range(1, NCHIP):
            a = a + own_ref[s].astype(F32)
            b = b + sib_ref[s].astype(F32)
        g = a + b
        delta, nm, nv = _adamw_math(w_ref[...], g, m_ref[...], v_ref[...])
        g_ref[...] = g
        d_ref[...] = delta
        nm_ref[...] = nm
        nv_ref[...] = nv

    lay = pl.BlockSpec((None, tr, c), lambda i: (layer, i, 0))
    slots = pl.BlockSpec((NCHIP, tr, c), lambda i: (0, i, 0))
    ins = [w, m, v, own, sib]
    in_specs = [lay, lay, lay, slots, slots]
    aliases = {}
    if prev is not None:
        ins += list(prev)
        in_specs += [pl.BlockSpec(memory_space=pl.ANY)] * 4
        aliases = {5 + k: k for k in range(4)}
    return _pcall(body, name="adamw_big", grid=(r // tr,), in_specs=in_specs, out_specs=[lay] * 4,
                  out_shape=[_sds(w.shape)] * 4, aliases=aliases, vmem_mb=48)(*ins)


SMALL_TILE = 512


def _adamw_small(w, parts, m, v):
    r = w.shape[0]

    def body(w_ref, p_ref, m_ref, v_ref, g_ref, d_ref, nm_ref, nv_ref):
        g = p_ref[0].astype(F32)
        for dev in range(1, 8):
            g = g + p_ref[dev].astype(F32)
        delta, nm, nv = _adamw_math(w_ref[...], g, m_ref[...], v_ref[...])
        g_ref[...] = g
        d_ref[...] = delta
        nm_ref[...] = nm
        nv_ref[...] = nv

    row = pl.BlockSpec((SMALL_TILE, 128), lambda i: (i, 0))
    return _pcall(body, name="adamw_small", grid=(r // SMALL_TILE,),
                  in_specs=[row, pl.BlockSpec((8, SMALL_TILE, 128), lambda i: (0, i, 0)), row, row],
                  out_specs=[row] * 4, out_shape=[_sds((r, 128))] * 4, vmem_mb=40)(w, parts, m, v)


def _place():
    x, y, c = lax.axis_index("x"), lax.axis_index("y"), lax.axis_index("c")
    return x, y, c, 2 * x + y


def _chip_peer(x, y, j):
    return (1 - x if j & 2 else x), (1 - y if j & 1 else y)


_HBM = pl.BlockSpec(memory_space=pltpu.HBM)
_SEM = pl.BlockSpec(memory_space=pltpu.SEMAPHORE)
_EFFECT = pltpu.SideEffectType.DATAFLOW_SIDE_EFFECTING


def _in_hbm(a):
    return pltpu.with_memory_space_constraint(a, pltpu.HBM)


def _copies_start(name, plan, ncopy, srcs, lands, dep=None):
    n = len(srcs) + len(lands)
    deps, dep_specs = _token_operand(dep)

    def body(*refs):
        send_sems, recv_sems = refs[n + len(deps)], refs[n + len(deps) + 1]
        token = refs[-1]
        for k, (src, dst, dev) in enumerate(plan(refs[:len(srcs)], refs[len(srcs):n])):
            pltpu.make_async_remote_copy(src_ref=src, dst_ref=dst, send_sem=send_sems.at[k],
                                         recv_sem=recv_sems.at[k], device_id=dev, device_id_type=MESH).start()
        token[...] = jnp.zeros_like(token)

    bufs = list(srcs) + list(lands)
    outs = pl.pallas_call(
        body, name=name,
        out_shape=(pltpu.SemaphoreType.DMA((ncopy,)), pltpu.SemaphoreType.DMA((ncopy,)),
                   *[pltpu.HBM(a.shape, a.dtype) for a in bufs], _sds((8, 128))),
        in_specs=[_HBM] * n + dep_specs,
        out_specs=(_SEM, _SEM, *[_HBM] * n, pl.BlockSpec(memory_space=pltpu.VMEM)),
        input_output_aliases={i: 2 + i for i in range(n)},
        compiler_params=pltpu.CompilerParams(has_side_effects=_EFFECT),
    )(*[_in_hbm(a) for a in bufs], *deps)
    return outs[0], outs[1], list(outs[2:2 + len(srcs)]), list(outs[2 + len(srcs):2 + n]), outs[-1]


def _copies_wait(name, plan, send_sems, recv_sems, srcs, lands, after):
    n = len(srcs) + len(lands)
    after = list(after)

    def body(*refs):
        s_sems, r_sems = refs[n], refs[n + 1]
        for k, (src, dst, dev) in enumerate(plan(refs[:len(srcs)], refs[len(srcs):n])):
            cp = pltpu.make_async_remote_copy(src_ref=src, dst_ref=dst, send_sem=s_sems.at[k],
                                              recv_sem=r_sems.at[k], device_id=dev, device_id_type=MESH)
            cp.wait_send()
            cp.wait_recv()

    bufs = list(srcs) + list(lands)
    outs = pl.pallas_call(
        body, name=name,
        out_shape=tuple(pltpu.HBM(a.shape, a.dtype) for a in bufs),
        in_specs=[_HBM] * n + [_SEM, _SEM] + [pl.BlockSpec(memory_space=pl.ANY)] * len(after),
        out_specs=tuple([_HBM] * n),
        input_output_aliases={i: i for i in range(n)},
        compiler_params=pltpu.CompilerParams(has_side_effects=_EFFECT),
    )(*bufs, send_sems, recv_sems, *after)
    return list(outs[:len(srcs)]), list(outs[len(srcs):])


def _place_blocks(layer, weights, chip):
    n = len(weights)

    def body(chip_ref, *refs):
        for src, dst in zip(refs[:n], refs[n:]):
            dst[...] = src[...].astype(BF16)

    def src_spec(a):
        return pl.BlockSpec((None,) + a.shape[1:], lambda i, c: (layer, 0, 0))

    def dst_spec(a):
        return pl.BlockSpec((None,) + a.shape[1:], lambda i, c: (c[0], 0, 0))

    grid_spec = pltpu.PrefetchScalarGridSpec(
        num_scalar_prefetch=1, grid=(1,), in_specs=[src_spec(a) for a in weights],
        out_specs=[dst_spec(a) for a in weights])
    return pl.pallas_call(body, name="place_blocks", grid_spec=grid_spec,
                          out_shape=[_sds((NCHIP,) + a.shape[1:], BF16) for a in weights],
                          compiler_params=pltpu.CompilerParams(vmem_limit_bytes=48 << 20))(
                              jnp.reshape(chip, (1,)).astype(jnp.int32), *weights)


def _with_own_slot(block, slot, nslots=NCHIP):
    land = lax.empty((nslots,) + block.shape, block.dtype)
    return lax.dynamic_update_slice(land, block[None], (slot,) + (0,) * block.ndim)


def _my_half(land, slot, c):
    half = land.shape[1] // 2
    return land.at[slot, pl.ds(pl.multiple_of(c * half, 16), half)]


def _gather_ici_plan(srcs, lands):
    x, y, c, s = _place()
    return [(_my_half(land, s, c), _my_half(land, s, c), (*_chip_peer(x, y, j), c))
            for land in lands for j in range(1, NCHIP)]


def _gather_d2d_plan(srcs, lands):
    x, y, c, s = _place()
    return [(_my_half(land, s ^ j, c), _my_half(land, s ^ j, c), (x, y, 1 - c))
            for land in lands for j in range(1, NCHIP)]


def _exchange_plan(srcs, lands):
    x, y, c, s = _place()
    return [(src.at[s ^ j], land.at[s], (*_chip_peer(x, y, j), c))
            for src, land in zip(srcs, lands) for j in range(1, NCHIP)]


def _sibling_plan(srcs, lands):
    x, y, c, _ = _place()
    return [(src, land, (x, y, 1 - c)) for src, land in zip(srcs, lands)]


def _everyone_plan(srcs, lands):
    x, y, c, _ = _place()
    me = 4 * x + 2 * y + c
    return [(srcs[0], lands[0].at[me], (*_chip_peer(x, y, j >> 1), 1 - c if j & 1 else c)) for j in range(1, 8)]


def _flatten_small(parts):
    flat = jnp.concatenate([p.reshape(-1) for p in parts])
    n = flat.shape[0]
    rows = -(-n // (128 * SMALL_TILE)) * SMALL_TILE
    return jnp.pad(flat, (0, rows * 128 - n)).reshape(rows, 128)


def _unflatten_small(buf, like):
    flat = buf.reshape(-1)
    out, at = [], 0
    for p in like:
        out.append(flat[at:at + p.size].reshape(p.shape))
        at += p.size
    return out


def kernel(x, pre_norm_g, w_in, lambda_re, lambda_im, log_dt, b_re, b_im, c_re, c_im, d_skip, w_glu, b_glu, w_branch_s, w_branch_a, w_out, post_norm_g, loss_target, m_pre_norm_g, m_w_in, m_lambda_re, m_lambda_im, m_log_dt, m_b_re, m_b_im, m_c_re, m_c_im, m_d_skip, m_w_glu, m_b_glu, m_w_branch_s, m_w_branch_a, m_w_out, m_post_norm_g, v_pre_norm_g, v_w_in, v_lambda_re, v_lambda_im, v_log_dt, v_b_re, v_b_im, v_c_re, v_c_im, v_d_skip, v_w_glu, v_b_glu, v_w_branch_s, v_w_branch_a, v_w_out, v_post_norm_g):
    nb = DEPTH * NCB
    lre_c = lambda_re.reshape(nb, 1, HW)
    lim_c = lambda_im.reshape(nb, 1, HW)
    ldt_c = jnp.broadcast_to(log_dt[:, :, None], (DEPTH, 32, 64)).reshape(nb, 1, HW)
    b_rows = lambda t: t.reshape(DEPTH, NCB, 8, 64, 16).transpose(0, 1, 4, 2, 3).reshape(nb, 16, HW)
    c_rows = lambda t: t.reshape(DEPTH, NCB, 8, 16, 64).transpose(0, 1, 3, 2, 4).reshape(nb, 16, HW)
    bre_c, bim_c = b_rows(b_re), b_rows(b_im)
    ar, ai, wb, wct = _s5_prep(lre_c, lim_c, ldt_c, bre_c, bim_c, c_rows(c_re), c_rows(c_im))
    ar4 = ar.reshape(DEPTH, NCB, 1, HW)
    ai4 = ai.reshape(DEPTH, NCB, 1, HW)
    wb = wb.reshape(DEPTH, NCB, 128, SW)
    wct = wct.reshape(DEPTH, NCB, 128, SW)

    chip = 2 * lax.axis_index("x") + lax.axis_index("y")
    device = 2 * chip + lax.axis_index("c")

    def gather_start(l, dep):
        lands = _place_blocks(l, [w_in, w_glu, w_branch_s, w_branch_a, w_out], chip)
        return _copies_start(f"gather_start_{l}", _gather_ici_plan, 15, [], lands, dep)

    def gather_forward(l, flight, after):
        send, recv, _, lands, _ = flight
        _, lands = _copies_wait(f"gather_wait_{l}", _gather_ici_plan, send, recv, [], lands, after)
        return _copies_start(f"forward_start_{l}", _gather_d2d_plan, 15, [], lands)

    def gather_done(l, forward, after):
        send, recv, _, lands, _ = forward
        return _copies_wait(f"forward_wait_{l}", _gather_d2d_plan, send, recv, [], lands, after)[1]

    xs = [x[0]]
    saved = []
    forward = gather_forward(0, gather_start(0, None), [xs[0]])
    for l in range(DEPTH):
        w4, wglu4, wbs4, wba4, wout4 = gather_done(l, forward, [xs[l]])
        started = None
        if l + 1 < DEPTH:
            flight = gather_start(l + 1, wglu4)
            started = flight[4]
        wglu = wglu4.reshape(SSM, SSM)
        wout = wout4.reshape(D, D)
        wb4, wc4 = wb[l], wct[l]
        g1 = pre_norm_g[l].reshape(1, D)
        g2 = post_norm_g[l].reshape(1, D)
        dsk = d_skip[l].reshape(1, SSM)
        bgl = b_glu[l].reshape(1, SSM)
        proj, ht = _rms_proj_fwd(xs[l], g1, w4, started)
        states, yraw = _s5_scan_fwd(proj, wb4, wc4, ar4[l], ai4[l])
        o, lse = _attn_core_fwd(proj)
        forwarded = None
        if l + 1 < DEPTH:
            forward = gather_forward(l + 1, flight, [yraw, o])
            forwarded = forward[4]
        ys = _s5_tail_fwd(yraw, proj, dsk, wglu, bgl)
        ya = _attn_mix_fwd(o, lse, proj)
        xs.append(_merge_out_fwd(ys, ya, proj, wbs4, wba4, wout, g2, xs[l], forwarded))
        saved.append((w4, wglu, wbs4, wba4, wout, wb4, wc4, g1, g2, dsk, bgl, proj, ht, states, yraw, ys, ya, o, lse))

    loss_part, dx = _loss_head(xs[DEPTH], loss_target[0])
    loss = lax.psum(loss_part[0, 0], ("x", "y", "c"))

    big_w = (w_in, w_glu.reshape(DEPTH, 128, SSM), w_branch_s, w_branch_a, w_out)
    big_m = (m_w_in, m_w_glu, m_w_branch_s, m_w_branch_a, m_w_out)
    big_v = (v_w_in, v_w_glu, v_w_branch_s, v_w_branch_a, v_w_out)
    big_out = [None] * 5
    small = {k: [None] * DEPTH for k in ("g1", "da", "dbb", "dcc", "dsk", "bgl", "g2")}
    ici = [None] * DEPTH
    d2d = [None] * DEPTH

    def exchange_start(l, parts):
        lands = [_with_own_slot(lax.dynamic_index_in_dim(p, chip, 0, keepdims=False), chip) for p in parts]
        ici[l] = _copies_start(f"exchange_start_{l}", _exchange_plan, 15, parts, lands)

    def handoff_start(l, after):
        send, recv, srcs, lands, _ = ici[l]
        _, own = _copies_wait(f"exchange_wait_{l}", _exchange_plan, send, recv, srcs, lands, after)
        d2d[l] = _copies_start(f"handoff_start_{l}", _sibling_plan, 5, own,
                               [lax.empty(a.shape, a.dtype) for a in own])

    def update(l, after):
        send, recv, own, lands, _ = d2d[l]
        own, sib = _copies_wait(f"handoff_wait_{l}", _sibling_plan, send, recv, own, lands, after)
        for k in range(5):
            big_out[k] = _adamw_big(l, big_w[k], big_m[k], big_v[k], own[k], sib[k], big_out[k])

    for l in reversed(range(DEPTH)):
        w4, wglu, wbs4, wba4, wout, wb4, wc4, g1, g2, dsk, bgl, proj, ht, states, yraw, ys, ya, o, lse = saved[l]
        started = None
        if l + 1 < DEPTH:
            started = ici[l + 1][4] + d2d[l + 2][4] if l + 2 < DEPTH else ici[l + 1][4]
        dys, dya, dproj, dwbs, dwba, dwout, dg2 = _merge_out_bwd(dx, ys, ya, proj, wbs4, wba4, wout, g2, started)
        dyraw, du_skip, dproj, dwglu, dbgl, ddsk = _s5_tail_bwd(dys, yraw, proj, dsk, wglu, bgl, dproj)
        dproj, dbb, dcc, da = _s5_scan_bwd(dyraw, states, proj, du_skip, wb4, wc4, ar4[l], ai4[l], dproj)
        do, cterm, dproj = _attn_mix_bwd(dya, o, lse, proj, dproj)
        dproj = _attn_core_bwd(proj, do, lse, cterm, dproj)
        dh, dwin = _proj_bwd(dproj, ht, w4)
        parts = [dwin, dwglu.reshape(NCHIP, 128, SSM), dwbs, dwba, dwout.reshape(NCHIP, D // NCHIP, D)]
        exchange_start(l, parts)
        dx, dg1 = _rms_bwd(dh, xs[l], g1, dx, ici[l][4])
        if l + 1 < DEPTH:
            handoff_start(l + 1, [ici[l][4]])
        small["g1"][l], small["g2"][l], small["dsk"][l], small["bgl"][l] = dg1, dg2, ddsk, dbgl
        small["da"][l], small["dbb"][l], small["dcc"][l] = da, dbb, dcc

    dlre, dlim, dldt, dbre_c, dbim_c = _s5_prep_bwd(
        lre_c, lim_c, ldt_c, bre_c, bim_c, jnp.stack(small["da"]).reshape(nb, 1, SW),
        jnp.stack(small["dbb"]).reshape(nb, 16, SW))
    dcc = jnp.stack(small["dcc"]).reshape(nb, 16, SW)
    b_back = lambda t: t.reshape(DEPTH, NCB, 16, 8, 64).transpose(0, 1, 3, 4, 2).reshape(DEPTH, 32, 64, 16)
    c_back = lambda t: t.reshape(DEPTH, NCB, 16, 8, 64).transpose(0, 1, 3, 2, 4).reshape(DEPTH, 32, 16, 64)
    small_w = [pre_norm_g, lambda_re, lambda_im, log_dt, b_re, b_im, c_re, c_im, d_skip, b_glu, post_norm_g]
    small_m = [m_pre_norm_g, m_lambda_re, m_lambda_im, m_log_dt, m_b_re, m_b_im, m_c_re, m_c_im, m_d_skip,
               m_b_glu, m_post_norm_g]
    small_v = [v_pre_norm_g, v_lambda_re, v_lambda_im, v_log_dt, v_b_re, v_b_im, v_c_re, v_c_im, v_d_skip,
               v_b_glu, v_post_norm_g]
    small_g = [jnp.stack(small["g1"]).reshape(DEPTH, D), dlre.reshape(DEPTH, 32, 64), dlim.reshape(DEPTH, 32, 64),
               dldt.reshape(DEPTH, 32, 64).sum(-1), b_back(dbre_c), b_back(dbim_c), c_back(dcc[:, :, :HW]),
               c_back(dcc[:, :, HW:]),
               jnp.stack(small["dsk"]).reshape(DEPTH, SSM), jnp.stack(small["bgl"]).reshape(DEPTH, SSM),
               jnp.stack(small["g2"]).reshape(DEPTH, D)]
    part = _flatten_small(small_g).astype(BF16)
    send, recv, srcs, lands, token = _copies_start("small_start", _everyone_plan, 7, [part],
                                                   [_with_own_slot(part, device, 8)])
    update(DEPTH - 1, [token, ici[0][4]])
    for l in range(DEPTH - 2, 0, -1):
        update(l, [out[0] for out in big_out])
    handoff_start(0, [out[0] for out in big_out])
    update(0, [d2d[0][4]])
    _, (parts8,) = _copies_wait("small_wait", _everyone_plan, send, recv, srcs, lands, [out[0] for out in big_out])
    g_flat, d_flat, nm_flat, nv_flat = _adamw_small(_flatten_small(small_w), parts8, _flatten_small(small_m),
                                                    _flatten_small(small_v))
    sg = _unflatten_small(g_flat, small_w)
    sd = _unflatten_small(d_flat, small_w)
    snm = _unflatten_small(nm_flat, small_w)
    snv = _unflatten_small(nv_flat, small_w)

    def ordered(sm, which):
        bg = [big_out[k][which] for k in range(5)]
        bg[1] = bg[1].reshape(DEPTH, 128, SSM)
        return [sm[0], bg[0], sm[1], sm[2], sm[3], sm[4], sm[5], sm[6], sm[7], sm[8], bg[1], sm[9], bg[2], bg[3],
                bg[4], sm[10]]

    return (loss, dx[None], *ordered(sg, 0), *ordered(sd, 1), *ordered(snm, 2), *ordered(snv, 3))
```

```python
import math

import jax
import jax.numpy as jnp
from jax import lax
from jax.experimental import pallas as pl
from jax.experimental.pallas import tpu as pltpu

F32 = jnp.float32
BF16 = jnp.bfloat16
MESH = pl.DeviceIdType.MESH

DEPTH = 4
L = 2048
D = 1024
NCOL = 8192
SSM = 512
AW = 512
QKV = 1536
RMS_EPS = 1e-6
NCHIP = 4

COL_U, COL_ZS, COL_Q, COL_K, COL_V, COL_ZA, COL_GS, COL_GA = 0, 4, 8, 20, 32, 44, 48, 56

NSEG = 8
SEG = L // NSEG
NCB = 4
HW = 512
SW = 2 * HW

HEAD = 128
DILATIONS = (1, 4, 16)
SCALE = HEAD ** -0.5
NEG = -1e30

ADAM_LR, ADAM_B1, ADAM_B2, ADAM_EPS, ADAM_WD, ADAM_STEP = 0.001, 0.9, 0.999, 1e-08, 0.01, 10


def _sds(shape, dtype=F32):
    return jax.ShapeDtypeStruct(shape, dtype)


def _pcall(body, *, name, out_shape, grid=None, in_specs=None, out_specs=None, scratch=(), vmem_mb=None,
           aliases=None):
    params = {}
    if vmem_mb is not None:
        params["vmem_limit_bytes"] = vmem_mb << 20
    kw = {}
    if grid is not None:
        kw["grid"] = grid
    if in_specs is not None:
        kw["in_specs"] = in_specs
    if out_specs is not None:
        kw["out_specs"] = out_specs
    return pl.pallas_call(body, name=name, out_shape=out_shape, scratch_shapes=list(scratch),
                          compiler_params=pltpu.CompilerParams(**params),
                          input_output_aliases=aliases or {}, **kw)


def _dot(a, b):
    return jnp.dot(a, b, preferred_element_type=F32)


def _dot_nt(a, b):
    return lax.dot_general(a, b, (((1,), (1,)), ((), ())), preferred_element_type=F32)


def _dot_tn(a, b):
    return lax.dot_general(a, b, (((0,), (0,)), ((), ())), preferred_element_type=F32)


def _sigmoid(x):
    return jax.nn.sigmoid(x)


_GELU_K = math.sqrt(2.0 / math.pi)


def _gelu(x):
    return 0.5 * x * (1.0 + jnp.tanh(_GELU_K * (x + 0.044715 * (x * x * x))))


def _gelu_grad(x):
    t = jnp.tanh(_GELU_K * (x + 0.044715 * (x * x * x)))
    return 0.5 * (1.0 + t) + 0.5 * x * (1.0 - t * t) * (_GELU_K * (1.0 + 3.0 * 0.044715 * (x * x)))


def _token_operand(dep):
    if dep is None:
        return [], []
    return [dep], [pl.BlockSpec(memory_space=pl.ANY)]


def _rms_proj_fwd(x, g, w4, dep=None):
    tl, tn = 512, 1024
    ni = L // tl
    deps, dep_specs = _token_operand(dep)

    def body(x_ref, g_ref, w_ref, *rest):
        o_ref, h_ref = rest[len(deps):]
        rows = pl.ds(pl.multiple_of(pl.program_id(1) * tl, tl), tl)

        @pl.when(pl.program_id(0) == 0)
        def _():
            xx = x_ref[...]
            inv = lax.rsqrt(jnp.mean(xx * xx, axis=-1, keepdims=True) + RMS_EPS)
            h_ref[rows, :] = (xx * inv * g_ref[...]).astype(BF16)

        o_ref[...] = _dot(h_ref[rows, :], w_ref[...])

    proj, h = _pcall(
        body, name="rms_proj_fwd", grid=(NCOL // tn, ni),
        in_specs=[pl.BlockSpec((tl, D), lambda j, i: (jnp.where(j == 0, i, ni - 1), 0)),
                  pl.BlockSpec((1, D), lambda j, i: (0, 0)),
                  pl.BlockSpec((None, D, tn), lambda j, i: (lax.div(j, 2), 0, lax.rem(j, 2)))] + dep_specs,
        out_specs=[pl.BlockSpec((tl, tn), lambda j, i: (i, j)), pl.BlockSpec((L, D), lambda j, i: (0, 0))],
        out_shape=[_sds((L, NCOL)), _sds((L, D), BF16)], vmem_mb=48)(x, g, w4, *deps)

    def transpose(h_ref, ht_ref):
        ht_ref[...] = h_ref[...].astype(F32).T.astype(BF16)

    ht = _pcall(transpose, name="transpose_h", grid=(ni,), in_specs=[pl.BlockSpec((tl, D), lambda i: (i, 0))],
                out_specs=pl.BlockSpec((D, tl), lambda i: (0, i)), out_shape=_sds((D, L), BF16), vmem_mb=40)(h)
    return proj, ht


def _proj_bwd(dproj, ht, w4):
    tn = 1024

    def body(dp_ref, ht_ref, w_ref, dh_ref, dw_ref):
        @pl.when(pl.program_id(0) == 0)
        def _():
            dh_ref[...] = jnp.zeros((L, D), F32)

        dw_ref[...] = _dot(ht_ref[...], dp_ref[...]).astype(BF16)
        dh_ref[...] += _dot_nt(dp_ref[...], w_ref[...])

    return _pcall(
        body, name="proj_bwd", grid=(NCOL // tn,),
        in_specs=[pl.BlockSpec((L, tn), lambda j: (0, j)),
                  pl.BlockSpec((D, L), lambda j: (0, 0)),
                  pl.BlockSpec((None, D, tn), lambda j: (lax.div(j, 2), 0, lax.rem(j, 2)))],
        out_specs=[pl.BlockSpec((L, D), lambda j: (0, 0)),
                   pl.BlockSpec((None, D, tn), lambda j: (lax.div(j, 2), 0, lax.rem(j, 2)))],
        out_shape=[_sds((L, D)), _sds((NCHIP, D, NCOL // NCHIP), BF16)], vmem_mb=56)(dproj, ht, w4)


def _rms_bwd(dh, x, g, dxn, dep=None):
    tl = 512
    deps, dep_specs = _token_operand(dep)

    def body(dh_ref, x_ref, g_ref, dxn_ref, *rest):
        dx_ref, dg_ref = rest[len(deps):]
        i = pl.program_id(0)
        xx = x_ref[...]
        inv = lax.rsqrt(jnp.mean(xx * xx, axis=-1, keepdims=True) + RMS_EPS)
        nrm = xx * inv
        dh_v = dh_ref[...]
        dn = dh_v * g_ref[...]
        dx_ref[...] = dxn_ref[...] + inv * (dn - nrm * jnp.mean(dn * nrm, axis=-1, keepdims=True))
        part = jnp.sum(dh_v * nrm, axis=0, keepdims=True)

        @pl.when(i == 0)
        def _():
            dg_ref[...] = part

        @pl.when(i > 0)
        def _():
            dg_ref[...] += part

    row = pl.BlockSpec((tl, D), lambda i: (i, 0))
    vec = pl.BlockSpec((1, D), lambda i: (0, 0))
    return _pcall(body, name="rms_bwd", grid=(L // tl,), in_specs=[row, row, vec, row] + dep_specs,
                  out_specs=[row, vec], out_shape=[_sds((L, D)), _sds((1, D))], vmem_mb=40)(dh, x, g, dxn, *deps)


def _s5_param_math(lre, lim, ldt, bre, bim):
    lr = jnp.minimum(lre, -1e-4)
    dt = jnp.exp(ldt)
    mag = jnp.exp(lr * dt)
    ar = mag * jnp.cos(lim * dt)
    ai = mag * jnp.sin(lim * dt)
    den = lr * lr + lim * lim
    nr = ar - 1.0
    qr = (nr * lr + ai * lim) / den
    qi = (ai * lr - nr * lim) / den
    return ar, ai, qr * bre - qi * bim, qr * bim + qi * bre


def _group_of_lane(width):
    lane = lax.broadcasted_iota(jnp.int32, (16, width), 1)
    return lax.shift_right_logical(jnp.bitwise_and(lane, HW - 1), 6)


def _vec_spec():
    return pl.BlockSpec((None, 1, HW), lambda n: (n, 0, 0))


def _mat_spec():
    return pl.BlockSpec((None, 16, HW), lambda n: (n, 0, 0))


def _s5_prep(lre, lim, ldt, bre, bim, cre, cim):
    n = lre.shape[0]

    def body(lre_ref, lim_ref, ldt_ref, bre_ref, bim_ref, cre_ref, cim_ref, ar_ref, ai_ref, wb_ref, wct_ref):
        ar, ai, bbr, bbi = _s5_param_math(lre_ref[...], lim_ref[...], ldt_ref[...], bre_ref[...], bim_ref[...])
        ar_ref[...] = ar
        ai_ref[...] = ai
        group = _group_of_lane(SW)
        bb = jnp.concatenate([bbr, bbi], axis=1)
        cc = jnp.concatenate([cre_ref[...], -cim_ref[...]], axis=1)
        for g in range(8):
            rows = pl.ds(16 * g, 16)
            wb_ref[rows, :] = jnp.where(group == g, bb, 0.0).astype(BF16)
            wct_ref[rows, :] = jnp.where(group == g, cc, 0.0).astype(BF16)

    wide = pl.BlockSpec((None, 128, SW), lambda n: (n, 0, 0))
    return _pcall(body, name="s5_prep", grid=(n,),
                  in_specs=[_vec_spec()] * 3 + [_mat_spec()] * 4,
                  out_specs=[_vec_spec(), _vec_spec(), wide, wide],
                  out_shape=[_sds((n, 1, HW)), _sds((n, 1, HW)), _sds((n, 128, SW), BF16), _sds((n, 128, SW), BF16)])(
                      lre, lim, ldt, bre, bim, cre, cim)


def _s5_prep_bwd(lre, lim, ldt, bre, bim, ga, gbb):
    n = lre.shape[0]

    def body(lre_ref, lim_ref, ldt_ref, bre_ref, bim_ref, ga_ref, gbb_ref,
             dlre_ref, dlim_ref, dldt_ref, dbre_ref, dbim_ref):
        _, vjp = jax.vjp(_s5_param_math, lre_ref[...], lim_ref[...], ldt_ref[...], bre_ref[...], bim_ref[...])
        ga, gbb = ga_ref[...], gbb_ref[...]
        dlre, dlim, dldt, dbre, dbim = vjp((ga[:, :HW], ga[:, HW:], gbb[:, :HW], gbb[:, HW:]))
        dlre_ref[...] = dlre
        dlim_ref[...] = dlim
        dldt_ref[...] = dldt
        dbre_ref[...] = dbre
        dbim_ref[...] = dbim

    return _pcall(body, name="s5_prep_bwd", grid=(n,),
                  in_specs=[_vec_spec()] * 3 + [_mat_spec()] * 2 +
                  [pl.BlockSpec((None, 1, SW), lambda n: (n, 0, 0)), pl.BlockSpec((None, 16, SW), lambda n: (n, 0, 0))],
                  out_specs=[_vec_spec()] * 3 + [_mat_spec()] * 2,
                  out_shape=[_sds((n, 1, HW))] * 3 + [_sds((n, 16, HW))] * 2)(lre, lim, ldt, bre, bim, ga, gbb)


def _diag_blocks(full):
    group = _group_of_lane(SW)
    out = jnp.where(group == 0, full[0:16, :], 0.0)
    for g in range(1, 8):
        out = out + jnp.where(group == g, full[16 * g:16 * g + 16, :], 0.0)
    return out


def _permute_rows(dst_ref, src_ref):
    for s in range(NSEG):
        dst_ref[pl.ds(s, SEG, stride=NSEG), :] = src_ref[pl.ds(s * SEG, SEG), :]


def _unpermute_rows(src_ref, s):
    return src_ref[pl.ds(s, SEG, stride=NSEG), :]


def _row_of(slab, rows, j):
    return jnp.sum(jnp.where(rows == j, slab, 0.0), axis=0, keepdims=True)


def _s5_scan_fwd(proj, wb4, wc4, ar4, ai4):
    def body(u_ref, wb_ref, wc_ref, ar_ref, ai_ref, x_ref, y_ref, up_ref, d_ref, yp_ref):
        _permute_rows(up_ref, u_ref)
        d_ref[...] = _dot(up_ref[...].astype(BF16), wb_ref[...])
        ar = jnp.broadcast_to(ar_ref[...], (NSEG, HW))
        ai = jnp.broadcast_to(ai_ref[...], (NSEG, HW))
        zero = jnp.zeros((NSEG, HW), F32)

        def drive(t):
            row = pl.multiple_of(t * NSEG, NSEG)
            dd = d_ref[pl.ds(row, NSEG), :]
            return row, dd[:, :HW], dd[:, HW:]

        def local_step(t, c):
            xr, xi = c
            _, dr, di = drive(t)
            return ar * xr - ai * xi + dr, ar * xi + ai * xr + di

        fr, fi = lax.fori_loop(0, SEG, local_step, (zero, zero), unroll=8)
        pr, pi_ = ar_ref[...], ai_ref[...]
        for _ in range(int(math.log2(SEG))):
            pr, pi_ = pr * pr - pi_ * pi_, 2.0 * pr * pi_
        rows = lax.broadcasted_iota(jnp.int32, (NSEG, HW), 0)
        cr, ci = zero, zero
        for j in range(NSEG - 1):
            sr, si = _row_of(cr, rows, j), _row_of(ci, rows, j)
            gr, gi = _row_of(fr, rows, j), _row_of(fi, rows, j)
            nr = pr * sr - pi_ * si + gr
            ni = pr * si + pi_ * sr + gi
            cr = jnp.where(rows == j + 1, nr, cr)
            ci = jnp.where(rows == j + 1, ni, ci)

        def true_step(t, c):
            xr, xi = c
            row, dr, di = drive(t)
            nr = ar * xr - ai * xi + dr
            ni = ar * xi + ai * xr + di
            x_ref[pl.ds(row, NSEG), :] = jnp.concatenate([nr, ni], axis=1)
            return nr, ni

        lax.fori_loop(0, SEG, true_step, (cr, ci), unroll=8)
        yp_ref[...] = _dot_nt(x_ref[...].astype(BF16), wc_ref[...])
        for s in range(NSEG):
            y_ref[pl.ds(s * SEG, SEG), :] = _unpermute_rows(yp_ref, s)

    return _pcall(
        body, name="s5_scan_fwd", grid=(NCB,),
        in_specs=[pl.BlockSpec((L, 128), lambda b: (0, COL_U + b)),
                  pl.BlockSpec((None, 128, SW), lambda b: (b, 0, 0)),
                  pl.BlockSpec((None, 128, SW), lambda b: (b, 0, 0)),
                  pl.BlockSpec((None, 1, HW), lambda b: (b, 0, 0)),
                  pl.BlockSpec((None, 1, HW), lambda b: (b, 0, 0))],
        out_specs=[pl.BlockSpec((None, L, SW), lambda b: (b, 0, 0)),
                   pl.BlockSpec((L, 128), lambda b: (0, b))],
        out_shape=[_sds((NCB, L, SW)), _sds((L, SSM))],
        scratch=[pltpu.VMEM((L, 128), F32), pltpu.VMEM((L, SW), F32), pltpu.VMEM((L, 128), F32)],
        vmem_mb=56)(proj, wb4, wc4, ar4, ai4)


def _s5_scan_bwd(dy, xs, proj, du_skip, wb4, wc4, ar4, ai4, dproj):
    def body(dy_ref, x_ref, u_ref, dus_ref, wb_ref, wc_ref, ar_ref, ai_ref, _,
             du_ref, dbb_ref, dcc_ref, da_ref, dyp_ref, up_ref, g_ref, dup_ref):
        _permute_rows(dyp_ref, dy_ref)
        _permute_rows(up_ref, u_ref)
        dyp = dyp_ref[...].astype(BF16)
        g_ref[...] = _dot(dyp, wc_ref[...])
        dcc = _diag_blocks(_dot_tn(dyp, x_ref[...].astype(BF16)))
        dcc_ref[...] = jnp.concatenate([dcc[:, :HW], -dcc[:, HW:]], axis=1)
        ar = jnp.broadcast_to(ar_ref[...], (NSEG, HW))
        ai = jnp.broadcast_to(ai_ref[...], (NSEG, HW))
        zero = jnp.zeros((NSEG, HW), F32)

        def load(t):
            row = pl.multiple_of(t * NSEG, NSEG)
            gg = g_ref[pl.ds(row, NSEG), :]
            return row, gg[:, :HW], gg[:, HW:]

        def local_step(k, c):
            lr, li = c
            _, gr, gi = load(SEG - 1 - k)
            return ar * lr + ai * li + gr, ar * li - ai * lr + gi

        fr, fi = lax.fori_loop(0, SEG, local_step, (zero, zero), unroll=8)
        pr, pi_ = ar_ref[...], -ai_ref[...]
        for _ in range(int(math.log2(SEG))):
            pr, pi_ = pr * pr - pi_ * pi_, 2.0 * pr * pi_
        rows = lax.broadcasted_iota(jnp.int32, (NSEG, HW), 0)
        cr, ci = zero, zero
        for j in range(NSEG - 1, 0, -1):
            sr, si = _row_of(cr, rows, j), _row_of(ci, rows, j)
            gr, gi = _row_of(fr, rows, j), _row_of(fi, rows, j)
            nr = pr * sr - pi_ * si + gr
            ni = pr * si + pi_ * sr + gi
            cr = jnp.where(rows == j - 1, nr, cr)
            ci = jnp.where(rows == j - 1, ni, ci)

        def true_step(k, c):
            lr, li, gar, gai = c
            t = SEG - 1 - k
            row, gr, gi = load(t)
            nr = ar * lr + ai * li + gr
            ni = ar * li - ai * lr + gi
            g_ref[pl.ds(row, NSEG), :] = jnp.concatenate([nr, ni], axis=1)
            prow = pl.multiple_of(jnp.maximum(t - 1, 0) * NSEG, NSEG)
            xp = x_ref[pl.ds(prow, NSEG), :]
            live = (t > 0).astype(F32)
            xr, xi = xp[:, :HW] * live, xp[:, HW:] * live
            return nr, ni, gar + (nr * xr + ni * xi), gai + (ni * xr - nr * xi)

        _, _, gar, gai = lax.fori_loop(0, SEG, true_step, (cr, ci, zero, zero), unroll=4)
        l0 = g_ref[1:NSEG, :]
        xl = x_ref[(SEG - 1) * NSEG:(SEG - 1) * NSEG + NSEG - 1, :]
        l0r, l0i, xlr, xli = l0[:, :HW], l0[:, HW:], xl[:, :HW], xl[:, HW:]
        gar_t = jnp.sum(gar, axis=0, keepdims=True) + jnp.sum(l0r * xlr + l0i * xli, axis=0, keepdims=True)
        gai_t = jnp.sum(gai, axis=0, keepdims=True) + jnp.sum(l0i * xlr - l0r * xli, axis=0, keepdims=True)
        da_ref[...] = jnp.concatenate([gar_t, gai_t], axis=1)
        lam = g_ref[...].astype(BF16)
        dbb_ref[...] = _diag_blocks(_dot_tn(up_ref[...].astype(BF16), lam))
        dup_ref[...] = _dot_nt(lam, wb_ref[...])
        for s in range(NSEG):
            sl = pl.ds(s * SEG, SEG)
            du_ref[sl, :] = (_unpermute_rows(dup_ref, s) + dus_ref[sl, :]).astype(BF16)

    col = lambda off: pl.BlockSpec((L, 128), lambda b: (0, off + b))
    return _pcall(
        body, name="s5_scan_bwd", grid=(NCB,),
        in_specs=[col(0), pl.BlockSpec((None, L, SW), lambda b: (b, 0, 0)), col(COL_U), col(0),
                  pl.BlockSpec((None, 128, SW), lambda b: (b, 0, 0)),
                  pl.BlockSpec((None, 128, SW), lambda b: (b, 0, 0)),
                  pl.BlockSpec((None, 1, HW), lambda b: (b, 0, 0)),
                  pl.BlockSpec((None, 1, HW), lambda b: (b, 0, 0)),
                  pl.BlockSpec(memory_space=pl.ANY)],
        out_specs=[col(COL_U), pl.BlockSpec((None, 16, SW), lambda b: (b, 0, 0)),
                   pl.BlockSpec((None, 16, SW), lambda b: (b, 0, 0)),
                   pl.BlockSpec((None, 1, SW), lambda b: (b, 0, 0))],
        out_shape=[_sds((L, NCOL), BF16), _sds((NCB, 16, SW)), _sds((NCB, 16, SW)), _sds((NCB, 1, SW))],
        scratch=[pltpu.VMEM((L, 128), F32), pltpu.VMEM((L, 128), F32), pltpu.VMEM((L, SW), F32),
                 pltpu.VMEM((L, 128), F32)],
        aliases={8: 0}, vmem_mb=56)(dy, xs, proj, du_skip, wb4, wc4, ar4, ai4, dproj)


def _s5_tail_fwd(yraw, proj, dsk, wglu, bglu):
    tl = 512

    def body(y_ref, u_ref, z_ref, dsk_ref, w_ref, b_ref, o_ref):
        y1 = y_ref[...] + dsk_ref[...] * u_ref[...]
        y2 = _gelu(y1)
        gl = _dot(y2.astype(BF16), w_ref[...]) + b_ref[...]
        z = z_ref[...]
        o_ref[...] = (y2 * _sigmoid(gl)) * (z * _sigmoid(z))

    blk = lambda c: pl.BlockSpec((tl, SSM), lambda i: (i, c))
    vec = pl.BlockSpec((1, SSM), lambda i: (0, 0))
    return _pcall(body, name="s5_tail_fwd", grid=(L // tl,),
                  in_specs=[blk(0), blk(0), blk(1), vec, pl.BlockSpec((SSM, SSM), lambda i: (0, 0)), vec],
                  out_specs=blk(0), out_shape=_sds((L, SSM)), vmem_mb=40)(yraw, proj, proj, dsk, wglu, bglu)


def _s5_tail_bwd(dys, yraw, proj, dsk, wglu, bglu, dproj):
    tl = 512
    ni = L // tl

    def body(dys_ref, y_ref, u_ref, z_ref, dsk_ref, w_ref, b_ref, _,
             dy_ref, dus_ref, dz_ref, dw_ref, db_ref, dd_ref, acc_ref):
        i = pl.program_id(0)
        u = u_ref[...]
        y1 = y_ref[...] + dsk_ref[...] * u
        y2 = _gelu(y1)
        y2b = y2.astype(BF16)
        sg = _sigmoid(_dot(y2b, w_ref[...]) + b_ref[...])
        y3 = y2 * sg
        z = z_ref[...]
        sz = _sigmoid(z)
        dys = dys_ref[...]
        dy3 = dys * (z * sz)
        dz_ref[...] = (dys * y3 * (sz * (1.0 + z * (1.0 - sz)))).astype(BF16)
        dgl = (dy3 * y2) * (sg * (1.0 - sg))
        dglb = dgl.astype(BF16)
        dy2 = dy3 * sg + _dot_nt(dglb, w_ref[...])
        dy1 = dy2 * _gelu_grad(y1)
        dy_ref[...] = dy1
        dus_ref[...] = dsk_ref[...] * dy1
        dw = _dot_tn(y2b, dglb)
        db = jnp.sum(dgl, axis=0, keepdims=True)
        dd = jnp.sum(dy1 * u, axis=0, keepdims=True)

        @pl.when(i == 0)
        def _():
            acc_ref[...] = dw
            db_ref[...] = db
            dd_ref[...] = dd

        @pl.when(i > 0)
        def _():
            acc_ref[...] += dw
            db_ref[...] += db
            dd_ref[...] += dd

        @pl.when(i == ni - 1)
        def _():
            dw_ref[...] = acc_ref[...].astype(BF16)

    blk = lambda c: pl.BlockSpec((tl, SSM), lambda i: (i, c))
    vec = pl.BlockSpec((1, SSM), lambda i: (0, 0))
    mat = pl.BlockSpec((SSM, SSM), lambda i: (0, 0))
    return _pcall(body, name="s5_tail_bwd", grid=(ni,),
                  in_specs=[blk(0), blk(0), blk(0), blk(1), vec, mat, vec, pl.BlockSpec(memory_space=pl.ANY)],
                  out_specs=[blk(0), blk(0), blk(COL_ZS // 4), mat, vec, vec],
                  out_shape=[_sds((L, SSM)), _sds((L, SSM)), _sds((L, NCOL), BF16), _sds((SSM, SSM), BF16),
                             _sds((1, SSM)), _sds((1, SSM))],
                  scratch=[pltpu.VMEM((SSM, SSM), F32)], aliases={7: 2},
                  vmem_mb=40)(dys, yraw, proj, proj, dsk, wglu, bglu, dproj)


def _attn_blocks(dil):
    n = L // dil
    return [(i * HEAD * dil + r, r * n + i * HEAD, i == 0) for r in range(dil) for i in range(n // HEAD)]


def _rows(start, dil):
    return pl.ds(start, HEAD) if dil == 1 else pl.ds(start, HEAD, stride=dil)


def _residue_rows(r, dil):
    n = L // dil
    return pl.ds(0, L) if dil == 1 else pl.ds(r, n, stride=dil)


def _gather_residues(pairs, dil, dtype=BF16):
    n = L // dil
    for src, dst in pairs:
        for r in range(dil):
            dst[pl.ds(r * n, n), :] = src[_residue_rows(r, dil), :].astype(dtype)


def _band_masks():
    ri = lax.broadcasted_iota(jnp.int32, (HEAD, HEAD), 0)
    ci = lax.broadcasted_iota(jnp.int32, (HEAD, HEAD), 1)
    ri2 = lax.broadcasted_iota(jnp.int32, (HEAD, 2 * HEAD), 0)
    ci2 = lax.broadcasted_iota(jnp.int32, (HEAD, 2 * HEAD), 1)
    return ci <= ri, jnp.logical_and(ci2 >= ri2, ci2 - HEAD <= ri2)


def _qkv_specs(index):
    def spec(off):
        return pl.BlockSpec((L, HEAD), lambda gi, h: (0, off + index(gi, h)))
    return [spec(off) for off in (COL_Q, COL_K, COL_V)]


def _attn_core_fwd(proj):
    def body(q_ref, k_ref, v_ref, o_ref, lse_ref, qp_ref, kp_ref, vp_ref):
        gi = pl.program_id(0)
        head = gi * 4 + pl.program_id(1)
        m_cur, m_both = _band_masks()
        mine = lax.broadcasted_iota(jnp.int32, (HEAD, HEAD), 1) == head

        @pl.when(head == 0)
        def _():
            lse_ref[...] = jnp.zeros((L, HEAD), F32)

        for g, dil in enumerate(DILATIONS):
            @pl.when(gi == g)
            def _(dil=dil):
                _gather_residues(((q_ref, qp_ref), (k_ref, kp_ref), (v_ref, vp_ref)), dil)
                for st, base, first in _attn_blocks(dil):
                    q = qp_ref[pl.ds(base, HEAD), :]
                    keys = pl.ds(base, HEAD) if first else pl.ds(base - HEAD, 2 * HEAD)
                    s = jnp.where(m_cur if first else m_both, _dot_nt(q, kp_ref[keys, :]) * SCALE, NEG)
                    mx = jnp.max(s, axis=-1, keepdims=True)
                    p = jnp.exp(s - mx)
                    den = jnp.sum(p, axis=-1, keepdims=True)
                    r = _rows(st, dil)
                    o_ref[r, :] = _dot(p.astype(BF16), vp_ref[keys, :]) / den
                    lse_ref[r, :] = jnp.where(mine, mx + jnp.log(den), lse_ref[r, :])

    idx = lambda gi, h: gi * 4 + h
    out = pl.BlockSpec((L, HEAD), lambda gi, h: (0, gi * 4 + h))
    shared = pl.BlockSpec((L, HEAD), lambda gi, h: (0, 0))
    return _pcall(body, name="attn_core_fwd", grid=(3, 4), in_specs=_qkv_specs(idx), out_specs=[out, shared],
                  out_shape=[_sds((L, QKV)), _sds((L, HEAD))], scratch=[pltpu.VMEM((L, HEAD), BF16)] * 3,
                  vmem_mb=40)(proj, proj, proj)


def _attn_mix_fwd(o, lse, proj):
    tl = 512

    def body(o_ref, l_ref, z_ref, y_ref):
        for h in range(4):
            c = [pl.ds((g * 4 + h) * HEAD, HEAD) for g in range(3)]
            ls = [l_ref[:, g * 4 + h:g * 4 + h + 1] for g in range(3)]
            m = jnp.maximum(jnp.maximum(ls[0], ls[1]), ls[2])
            e = [jnp.exp(ls[g] - m) for g in range(3)]
            y = (e[0] * o_ref[:, c[0]] + e[1] * o_ref[:, c[1]] + e[2] * o_ref[:, c[2]]) / (e[0] + e[1] + e[2])
            z = z_ref[:, pl.ds(h * HEAD, HEAD)]
            y_ref[:, pl.ds(h * HEAD, HEAD)] = y * (z * _sigmoid(z))

    wide = pl.BlockSpec((tl, QKV), lambda i: (i, 0))
    return _pcall(body, name="attn_mix_fwd", grid=(L // tl,),
                  in_specs=[wide, pl.BlockSpec((tl, HEAD), lambda i: (i, 0)),
                            pl.BlockSpec((tl, AW), lambda i: (i, COL_ZA // 4))],
                  out_specs=pl.BlockSpec((tl, AW), lambda i: (i, 0)), out_shape=_sds((L, AW)),
                  vmem_mb=40)(o, lse, proj)


def _attn_mix_bwd(dya, o, lse, proj, dproj):
    tl = 512

    def body(dya_ref, o_ref, l_ref, z_ref, _, do_ref, c_ref, dz_ref):
        c_ref[...] = jnp.zeros(c_ref.shape, F32)
        for h in range(4):
            c = [pl.ds((g * 4 + h) * HEAD, HEAD) for g in range(3)]
            hs = pl.ds(h * HEAD, HEAD)
            ls = [l_ref[:, g * 4 + h:g * 4 + h + 1] for g in range(3)]
            m = jnp.maximum(jnp.maximum(ls[0], ls[1]), ls[2])
            e = [jnp.exp(ls[g] - m) for g in range(3)]
            den = e[0] + e[1] + e[2]
            al = [e[g] / den for g in range(3)]
            y = al[0] * o_ref[:, c[0]] + al[1] * o_ref[:, c[1]] + al[2] * o_ref[:, c[2]]
            z = z_ref[:, hs]
            sz = _sigmoid(z)
            dya = dya_ref[:, hs]
            dz_ref[:, hs] = (dya * y * (sz * (1.0 + z * (1.0 - sz)))).astype(BF16)
            dy = dya * (z * sz)
            tot = jnp.sum(dy * y, axis=-1, keepdims=True)
            for g in range(3):
                do_ref[:, c[g]] = al[g] * dy
                c_ref[:, g * 4 + h:g * 4 + h + 1] = -(al[g] * tot)

    wide = pl.BlockSpec((tl, QKV), lambda i: (i, 0))
    nar = pl.BlockSpec((tl, AW), lambda i: (i, 0))
    one = pl.BlockSpec((tl, HEAD), lambda i: (i, 0))
    za = pl.BlockSpec((tl, AW), lambda i: (i, COL_ZA // 4))
    return _pcall(body, name="attn_mix_bwd", grid=(L // tl,),
                  in_specs=[nar, wide, one, za, pl.BlockSpec(memory_space=pl.ANY)],
                  out_specs=[wide, one, za], out_shape=[_sds((L, QKV)), _sds((L, HEAD)), _sds((L, NCOL), BF16)],
                  aliases={4: 2}, vmem_mb=48)(dya, o, lse, proj, dproj)


def _attn_core_bwd(proj, do, lse, cc, dproj):
    def body(q_ref, k_ref, v_ref, do_ref, lse_ref, c_ref, _, dp_ref,
             qp_ref, kp_ref, vp_ref, dop_ref, lsep_ref, cp_ref, dqp_ref, dkp_ref, dvp_ref, tok_ref, out_ref, sems):
        gi = pl.program_id(0)
        head = gi * 4 + pl.program_id(1)
        m_cur, m_both = _band_masks()
        my_lane = lax.broadcasted_iota(jnp.int32, (HEAD, HEAD), 1) == head
        dkp_ref[...] = jnp.zeros((L, HEAD), F32)
        dvp_ref[...] = jnp.zeros((L, HEAD), F32)
        copies = []
        for g, dil in enumerate(DILATIONS):
            @pl.when(gi == g)
            def _(dil=dil):
                _gather_residues(((q_ref, qp_ref), (k_ref, kp_ref), (v_ref, vp_ref), (do_ref, dop_ref)), dil)
                _gather_residues(((lse_ref, lsep_ref), (c_ref, cp_ref)), dil, F32)
                for _, base, first in _attn_blocks(dil):
                    mine = pl.ds(base, HEAD)
                    keys = mine if first else pl.ds(base - HEAD, 2 * HEAD)
                    q, do_b, kb = qp_ref[mine, :], dop_ref[mine, :], kp_ref[keys, :]
                    lse_b = jnp.sum(jnp.where(my_lane, lsep_ref[mine, :], 0.0), axis=-1, keepdims=True)
                    c_b = jnp.sum(jnp.where(my_lane, cp_ref[mine, :], 0.0), axis=-1, keepdims=True)
                    s = _dot_nt(q, kb) * SCALE
                    p = jnp.where(m_cur if first else m_both, jnp.exp(s - lse_b), 0.0)
                    ds = (p * (_dot_nt(do_b, vp_ref[keys, :]) + c_b) * SCALE).astype(BF16)
                    dqp_ref[mine, :] = _dot(ds, kb)
                    dkp_ref[keys, :] += _dot_tn(ds, q)
                    dvp_ref[keys, :] += _dot_tn(p.astype(BF16), do_b)
                n = L // dil
                for slot, src in enumerate((dqp_ref, dkp_ref, dvp_ref)):
                    for r in range(dil):
                        tok_ref[_residue_rows(r, dil), :] = src[pl.ds(r * n, n), :]
                    out_ref[slot] = tok_ref[...].astype(BF16)

        for slot, off in enumerate((COL_Q, COL_K, COL_V)):
            cols = pl.ds(pl.multiple_of((off + head) * HEAD, HEAD), HEAD)
            copies.append(pltpu.make_async_copy(out_ref.at[slot], dp_ref.at[:, cols], sems.at[slot]))
            copies[-1].start()
        for cp in copies:
            cp.wait()

    idx = lambda gi, h: gi * 4 + h
    blk = pl.BlockSpec((L, HEAD), lambda gi, h: (0, gi * 4 + h))
    shared = pl.BlockSpec((L, HEAD), lambda gi, h: (0, 0))
    hbm = pl.BlockSpec(memory_space=pl.ANY)
    return _pcall(body, name="attn_core_bwd", grid=(3, 4), in_specs=_qkv_specs(idx) + [blk, shared, shared, hbm],
                  out_specs=hbm, out_shape=_sds((L, NCOL), BF16),
                  scratch=[pltpu.VMEM((L, HEAD), BF16)] * 4 + [pltpu.VMEM((L, HEAD), F32)] * 6 +
                  [pltpu.VMEM((3, L, HEAD), BF16), pltpu.SemaphoreType.DMA((3,))],
                  aliases={6: 0}, vmem_mb=48)(proj, proj, proj, do, lse, cc, dproj)


def _merge_fwd_math(ys_b, ya_b, gs, ga, wbs_ref, wba_ref, wout_ref):
    bs = jnp.concatenate([_dot(ys_b, wbs_ref[s]) for s in range(NCHIP)], axis=1)
    ba = jnp.concatenate([_dot(ya_b, wba_ref[s]) for s in range(NCHIP)], axis=1)
    sgs, sga = _sigmoid(gs), _sigmoid(ga)
    merged = sgs * bs + sga * ba
    out = _dot(merged.astype(BF16), wout_ref[...])
    inv = lax.rsqrt(jnp.mean(out * out, axis=-1, keepdims=True) + RMS_EPS)
    return bs, ba, sgs, sga, merged, out, inv


def _merge_specs(tl):
    row = lambda w, c: pl.BlockSpec((tl, w), lambda i: (i, c))
    return [row(SSM, 0), row(AW, 0), row(D, COL_GS // 8), row(D, COL_GA // 8),
            pl.BlockSpec((NCHIP, SSM, D // NCHIP), lambda i: (0, 0, 0)),
            pl.BlockSpec((NCHIP, AW, D // NCHIP), lambda i: (0, 0, 0)),
            pl.BlockSpec((D, D), lambda i: (0, 0)),
            pl.BlockSpec((1, D), lambda i: (0, 0))]


def _merge_out_fwd(ys, ya, proj, wbs4, wba4, wout, g2, x, dep=None):
    tl = 256
    deps, dep_specs = _token_operand(dep)

    def body(ys_ref, ya_ref, gs_ref, ga_ref, wbs_ref, wba_ref, wout_ref, g2_ref, x_ref, *rest):
        o_ref = rest[-1]
        *_, out, inv = _merge_fwd_math(ys_ref[...].astype(BF16), ya_ref[...].astype(BF16), gs_ref[...],
                                       ga_ref[...], wbs_ref, wba_ref, wout_ref)
        o_ref[...] = x_ref[...] + out * inv * g2_ref[...]

    row = pl.BlockSpec((tl, D), lambda i: (i, 0))
    return _pcall(body, name="merge_out_fwd", grid=(L // tl,), in_specs=_merge_specs(tl) + [row] + dep_specs,
                  out_specs=row, out_shape=_sds((L, D)),
                  vmem_mb=48)(ys, ya, proj, proj, wbs4, wba4, wout, g2, x, *deps)


def _merge_out_bwd(dxn, ys, ya, proj, wbs4, wba4, wout, g2, dep=None):
    tl = 256
    ni = L // tl
    cw = D // NCHIP
    deps, dep_specs = _token_operand(dep)

    def body(dxn_ref, ys_ref, ya_ref, gs_ref, ga_ref, wbs_ref, wba_ref, wout_ref, g2_ref, *rest):
        (dys_ref, dya_ref, dgate_ref, dwbs_ref, dwba_ref, dwout_ref, dg2_ref,
         abs_ref, aba_ref, aout_ref) = rest[len(deps):]
        i = pl.program_id(0)

        @pl.when(i == 0)
        def _():
            abs_ref[...] = jnp.zeros(abs_ref.shape, F32)
            aba_ref[...] = jnp.zeros(aba_ref.shape, F32)
            aout_ref[...] = jnp.zeros(aout_ref.shape, F32)
            dg2_ref[...] = jnp.zeros(dg2_ref.shape, F32)

        ys_b, ya_b = ys_ref[...].astype(BF16), ya_ref[...].astype(BF16)
        bs, ba, sgs, sga, merged, out, inv = _merge_fwd_math(ys_b, ya_b, gs_ref[...], ga_ref[...],
                                                             wbs_ref, wba_ref, wout_ref)
        nrm = out * inv
        dxn = dxn_ref[...]
        dg2_ref[...] += jnp.sum(dxn * nrm, axis=0, keepdims=True)
        dn = dxn * g2_ref[...]
        dout = (inv * (dn - nrm * jnp.mean(dn * nrm, axis=-1, keepdims=True))).astype(BF16)
        aout_ref[...] += _dot_tn(merged.astype(BF16), dout)
        dm = _dot_nt(dout, wout_ref[...])
        dbs, dba = dm * sgs, dm * sga
        dgate_ref[:, :D] = (dm * bs * (sgs * (1.0 - sgs))).astype(BF16)
        dgate_ref[:, D:] = (dm * ba * (sga * (1.0 - sga))).astype(BF16)
        dbs_b, dba_b = dbs.astype(BF16), dba.astype(BF16)
        dys = None
        dya = None
        for s in range(NCHIP):
            cs = slice(s * cw, (s + 1) * cw)
            p_s = _dot_nt(dbs_b[:, cs], wbs_ref[s])
            p_a = _dot_nt(dba_b[:, cs], wba_ref[s])
            dys = p_s if dys is None else dys + p_s
            dya = p_a if dya is None else dya + p_a
            abs_ref[s] += _dot_tn(ys_b, dbs_b[:, cs])
            aba_ref[s] += _dot_tn(ya_b, dba_b[:, cs])

        dys_ref[...] = dys
        dya_ref[...] = dya

        @pl.when(i == ni - 1)
        def _():
            dwbs_ref[...] = abs_ref[...].astype(BF16)
            dwba_ref[...] = aba_ref[...].astype(BF16)
            dwout_ref[...] = aout_ref[...].astype(BF16)

    row = lambda w: pl.BlockSpec((tl, w), lambda i: (i, 0))
    w4 = pl.BlockSpec((NCHIP, SSM, cw), lambda i: (0, 0, 0))
    sq = pl.BlockSpec((D, D), lambda i: (0, 0))
    vec = pl.BlockSpec((1, D), lambda i: (0, 0))
    return _pcall(body, name="merge_out_bwd", grid=(ni,), in_specs=[row(D)] + _merge_specs(tl) + dep_specs,
                  out_specs=[row(SSM), row(AW), pl.BlockSpec((tl, 2 * D), lambda i: (i, COL_GS // 16)), w4, w4, sq,
                             vec],
                  out_shape=[_sds((L, SSM)), _sds((L, AW)), _sds((L, NCOL), BF16),
                             _sds((NCHIP, SSM, cw), BF16), _sds((NCHIP, AW, cw), BF16), _sds((D, D), BF16),
                             _sds((1, D))],
                  scratch=[pltpu.VMEM((NCHIP, SSM, cw), F32), pltpu.VMEM((NCHIP, AW, cw), F32),
                           pltpu.VMEM((D, D), F32)],
                  vmem_mb=56)(dxn, ys, ya, proj, proj, wbs4, wba4, wout, g2, *deps)


def _loss_head(y, target):
    tl = 512

    def body(y_ref, t_ref, loss_ref, dy_ref):
        i = pl.program_id(0)
        err = y_ref[...] - t_ref[...]
        dy_ref[...] = err / D
        part = 0.5 * jnp.sum(jnp.mean(err * err, axis=-1, keepdims=True), axis=0, keepdims=True)
        part = jnp.broadcast_to(part, (8, 128))

        @pl.when(i == 0)
        def _():
            loss_ref[...] = part

        @pl.when(i > 0)
        def _():
            loss_ref[...] += part

    row = pl.BlockSpec((tl, D), lambda i: (i, 0))
    return _pcall(body, name="loss_head", grid=(L // tl,), in_specs=[row, row],
                  out_specs=[pl.BlockSpec((8, 128), lambda i: (0, 0)), row],
                  out_shape=[_sds((8, 128)), _sds((L, D))], vmem_mb=40)(y, target)


def _adamw_math(w, g, m, v):
    m = ADAM_B1 * m + (1.0 - ADAM_B1) * g
    v = ADAM_B2 * v + (1.0 - ADAM_B2) * (g * g)
    m_hat = m / (1.0 - ADAM_B1 ** ADAM_STEP)
    v_hat = v / (1.0 - ADAM_B2 ** ADAM_STEP)
    delta = -ADAM_LR * (m_hat / (jnp.sqrt(v_hat) + ADAM_EPS) + ADAM_WD * w)
    return delta, m, v


def _adamw_big(layer, w, m, v, own, sib, prev):
    _, r, c = w.shape
    tr = min(r, 128)

    def body(w_ref, m_ref, v_ref, own_ref, sib_ref, *rest):
        g_ref, d_ref, nm_ref, nv_ref = rest[-4:]
        a = own_ref[0].astype(F32)
        b = sib_ref[0].astype(F32)
        for s in range(1, NCHIP):
            a = a + own_ref[s].astype(F32)
            b = b + sib_ref[s].astype(F32)
        g = a + b
        delta, nm, nv = _adamw_math(w_ref[...], g, m_ref[...], v_ref[...])
        g_ref[...] = g
        d_ref[...] = delta
        nm_ref[...] = nm
        nv_ref[...] = nv

    lay = pl.BlockSpec((None, tr, c), lambda i: (layer, i, 0))
    slots = pl.BlockSpec((NCHIP, tr, c), lambda i: (0, i, 0))
    ins = [w, m, v, own, sib]
    in_specs = [lay, lay, lay, slots, slots]
    aliases = {}
    if prev is not None:
        ins += list(prev)
        in_specs += [pl.BlockSpec(memory_space=pl.ANY)] * 4
        aliases = {5 + k: k for k in range(4)}
    return _pcall(body, name="adamw_big", grid=(r // tr,), in_specs=in_specs, out_specs=[lay] * 4,
                  out_shape=[_sds(w.shape)] * 4, aliases=aliases, vmem_mb=48)(*ins)


SMALL_TILE = 512


def _adamw_small(w, parts, m, v):
    r = w.shape[0]

    def body(w_ref, p_ref, m_ref, v_ref, g_ref, d_ref, nm_ref, nv_ref):
        g = p_ref[0].astype(F32)
        for dev in range(1, 8):
            g = g + p_ref[dev].astype(F32)
        delta, nm, nv = _adamw_math(w_ref[...], g, m_ref[...], v_ref[...])
        g_ref[...] = g
        d_ref[...] = delta
        nm_ref[...] = nm
        nv_ref[...] = nv

    row = pl.BlockSpec((SMALL_TILE, 128), lambda i: (i, 0))
    return _pcall(body, name="adamw_small", grid=(r // SMALL_TILE,),
                  in_specs=[row, pl.BlockSpec((8, SMALL_TILE, 128), lambda i: (0, i, 0)), row, row],
                  out_specs=[row] * 4, out_shape=[_sds((r, 128))] * 4, vmem_mb=40)(w, parts, m, v)


def _place():
    x, y, c = lax.axis_index("x"), lax.axis_index("y"), lax.axis_index("c")
    return x, y, c, 2 * x + y


def _chip_peer(x, y, j):
    return (1 - x if j & 2 else x), (1 - y if j & 1 else y)


_HBM = pl.BlockSpec(memory_space=pltpu.HBM)
_SEM = pl.BlockSpec(memory_space=pltpu.SEMAPHORE)
_EFFECT = pltpu.SideEffectType.DATAFLOW_SIDE_EFFECTING


def _in_hbm(a):
    return pltpu.with_memory_space_constraint(a, pltpu.HBM)


def _copies_start(name, plan, ncopy, srcs, lands, dep=None):
    n = len(srcs) + len(lands)
    deps, dep_specs = _token_operand(dep)

    def body(*refs):
        send_sems, recv_sems = refs[n + len(deps)], refs[n + len(deps) + 1]
        token = refs[-1]
        for k, (src, dst, dev) in enumerate(plan(refs[:len(srcs)], refs[len(srcs):n])):
            pltpu.make_async_remote_copy(src_ref=src, dst_ref=dst, send_sem=send_sems.at[k],
                                         recv_sem=recv_sems.at[k], device_id=dev, device_id_type=MESH).start()
        token[...] = jnp.zeros_like(token)

    bufs = list(srcs) + list(lands)
    outs = pl.pallas_call(
        body, name=name,
        out_shape=(pltpu.SemaphoreType.DMA((ncopy,)), pltpu.SemaphoreType.DMA((ncopy,)),
                   *[pltpu.HBM(a.shape, a.dtype) for a in bufs], _sds((8, 128))),
        in_specs=[_HBM] * n + dep_specs,
        out_specs=(_SEM, _SEM, *[_HBM] * n, pl.BlockSpec(memory_space=pltpu.VMEM)),
        input_output_aliases={i: 2 + i for i in range(n)},
        compiler_params=pltpu.CompilerParams(has_side_effects=_EFFECT),
    )(*[_in_hbm(a) for a in bufs], *deps)
    return outs[0], outs[1], list(outs[2:2 + len(srcs)]), list(outs[2 + len(srcs):2 + n]), outs[-1]


def _copies_wait(name, plan, send_sems, recv_sems, srcs, lands, after):
    n = len(srcs) + len(lands)
    after = list(after)

    def body(*refs):
        s_sems, r_sems = refs[n], refs[n + 1]
        for k, (src, dst, dev) in enumerate(plan(refs[:len(srcs)], refs[len(srcs):n])):
            cp = pltpu.make_async_remote_copy(src_ref=src, dst_ref=dst, send_sem=s_sems.at[k],
                                              recv_sem=r_sems.at[k], device_id=dev, device_id_type=MESH)
            cp.wait_send()
            cp.wait_recv()

    bufs = list(srcs) + list(lands)
    outs = pl.pallas_call(
        body, name=name,
        out_shape=tuple(pltpu.HBM(a.shape, a.dtype) for a in bufs),
        in_specs=[_HBM] * n + [_SEM, _SEM] + [pl.BlockSpec(memory_space=pl.ANY)] * len(after),
        out_specs=tuple([_HBM] * n),
        input_output_aliases={i: i for i in range(n)},
        compiler_params=pltpu.CompilerParams(has_side_effects=_EFFECT),
    )(*bufs, send_sems, recv_sems, *after)
    return list(outs[:len(srcs)]), list(outs[len(srcs):])


def _with_own_slot(block, slot, nslots=NCHIP):
    land = lax.empty((nslots,) + block.shape, block.dtype)
    return lax.dynamic_update_slice(land, block[None], (slot,) + (0,) * block.ndim)


def _my_half(land, slot, c):
    half = land.shape[1] // 2
    return land.at[slot, pl.ds(pl.multiple_of(c * half, 16), half)]


def _gather_ici_plan(srcs, lands):
    x, y, c, s = _place()
    return [(_my_half(land, s, c), _my_half(land, s, c), (*_chip_peer(x, y, j), c))
            for land in lands for j in range(1, NCHIP)]


def _gather_d2d_plan(srcs, lands):
    x, y, c, s = _place()
    return [(_my_half(land, s ^ j, c), _my_half(land, s ^ j, c), (x, y, 1 - c))
            for land in lands for j in range(1, NCHIP)]


def _exchange_plan(srcs, lands):
    x, y, c, s = _place()
    return [(src.at[s ^ j], land.at[s], (*_chip_peer(x, y, j), c))
            for src, land in zip(srcs, lands) for j in range(1, NCHIP)]


def _sibling_plan(srcs, lands):
    x, y, c, _ = _place()
    return [(src, land, (x, y, 1 - c)) for src, land in zip(srcs, lands)]


def _everyone_plan(srcs, lands):
    x, y, c, _ = _place()
    me = 4 * x + 2 * y + c
    return [(srcs[0], lands[0].at[me], (*_chip_peer(x, y, j >> 1), 1 - c if j & 1 else c)) for j in range(1, 8)]


def _flatten_small(parts):
    flat = jnp.concatenate([p.reshape(-1) for p in parts])
    n = flat.shape[0]
    rows = -(-n // (128 * SMALL_TILE)) * SMALL_TILE
    return jnp.pad(flat, (0, rows * 128 - n)).reshape(rows, 128)


def _unflatten_small(buf, like):
    flat = buf.reshape(-1)
    out, at = [], 0
    for p in like:
        out.append(flat[at:at + p.size].reshape(p.shape))
        at += p.size
    return out


def kernel(x, pre_norm_g, w_in, lambda_re, lambda_im, log_dt, b_re, b_im, c_re, c_im, d_skip, w_glu, b_glu, w_branch_s, w_branch_a, w_out, post_norm_g, loss_target, m_pre_norm_g, m_w_in, m_lambda_re, m_lambda_im, m_log_dt, m_b_re, m_b_im, m_c_re, m_c_im, m_d_skip, m_w_glu, m_b_glu, m_w_branch_s, m_w_branch_a, m_w_out, m_post_norm_g, v_pre_norm_g, v_w_in, v_lambda_re, v_lambda_im, v_log_dt, v_b_re, v_b_im, v_c_re, v_c_im, v_d_skip, v_w_glu, v_b_glu, v_w_branch_s, v_w_branch_a, v_w_out, v_post_norm_g):
    nb = DEPTH * NCB
    lre_c = lambda_re.reshape(nb, 1, HW)
    lim_c = lambda_im.reshape(nb, 1, HW)
    ldt_c = jnp.broadcast_to(log_dt[:, :, None], (DEPTH, 32, 64)).reshape(nb, 1, HW)
    b_rows = lambda t: t.reshape(DEPTH, NCB, 8, 64, 16).transpose(0, 1, 4, 2, 3).reshape(nb, 16, HW)
    c_rows = lambda t: t.reshape(DEPTH, NCB, 8, 16, 64).transpose(0, 1, 3, 2, 4).reshape(nb, 16, HW)
    bre_c, bim_c = b_rows(b_re), b_rows(b_im)
    ar, ai, wb, wct = _s5_prep(lre_c, lim_c, ldt_c, bre_c, bim_c, c_rows(c_re), c_rows(c_im))
    ar4 = ar.reshape(DEPTH, NCB, 1, HW)
    ai4 = ai.reshape(DEPTH, NCB, 1, HW)
    wb = wb.reshape(DEPTH, NCB, 128, SW)
    wct = wct.reshape(DEPTH, NCB, 128, SW)

    chip = 2 * lax.axis_index("x") + lax.axis_index("y")
    device = 2 * chip + lax.axis_index("c")

    def gather_start(l, dep):
        blocks = [w_in[l], w_glu[l], w_branch_s[l], w_branch_a[l], w_out[l]]
        lands = [_with_own_slot(a.astype(BF16), chip) for a in blocks]
        return _copies_start(f"gather_start_{l}", _gather_ici_plan, 15, [], lands, dep)

    def gather_forward(l, flight, after):
        send, recv, _, lands, _ = flight
        _, lands = _copies_wait(f"gather_wait_{l}", _gather_ici_plan, send, recv, [], lands, after)
        return _copies_start(f"forward_start_{l}", _gather_d2d_plan, 15, [], lands)

    def gather_done(l, forward, after):
        send, recv, _, lands, _ = forward
        return _copies_wait(f"forward_wait_{l}", _gather_d2d_plan, send, recv, [], lands, after)[1]

    xs = [x[0]]
    saved = []
    forward = gather_forward(0, gather_start(0, None), [xs[0]])
    for l in range(DEPTH):
        w4, wglu4, wbs4, wba4, wout4 = gather_done(l, forward, [xs[l]])
        started = None
        if l + 1 < DEPTH:
            flight = gather_start(l + 1, wglu4)
            started = flight[4]
        wglu = wglu4.reshape(SSM, SSM)
        wout = wout4.reshape(D, D)
        wb4, wc4 = wb[l], wct[l]
        g1 = pre_norm_g[l].reshape(1, D)
        g2 = post_norm_g[l].reshape(1, D)
        dsk = d_skip[l].reshape(1, SSM)
        bgl = b_glu[l].reshape(1, SSM)
        proj, ht = _rms_proj_fwd(xs[l], g1, w4, started)
        states, yraw = _s5_scan_fwd(proj, wb4, wc4, ar4[l], ai4[l])
        o, lse = _attn_core_fwd(proj)
        forwarded = None
        if l + 1 < DEPTH:
            forward = gather_forward(l + 1, flight, [yraw, o])
            forwarded = forward[4]
        ys = _s5_tail_fwd(yraw, proj, dsk, wglu, bgl)
        ya = _attn_mix_fwd(o, lse, proj)
        xs.append(_merge_out_fwd(ys, ya, proj, wbs4, wba4, wout, g2, xs[l], forwarded))
        saved.append((w4, wglu, wbs4, wba4, wout, wb4, wc4, g1, g2, dsk, bgl, proj, ht, states, yraw, ys, ya, o, lse))

    loss_part, dx = _loss_head(xs[DEPTH], loss_target[0])
    loss = lax.psum(loss_part[0, 0], ("x", "y", "c"))

    big_w = (w_in, w_glu.reshape(DEPTH, 128, SSM), w_branch_s, w_branch_a, w_out)
    big_m = (m_w_in, m_w_glu, m_w_branch_s, m_w_branch_a, m_w_out)
    big_v = (v_w_in, v_w_glu, v_w_branch_s, v_w_branch_a, v_w_out)
    big_out = [None] * 5
    small = {k: [None] * DEPTH for k in ("g1", "da", "dbb", "dcc", "dsk", "bgl", "g2")}
    ici = [None] * DEPTH
    d2d = [None] * DEPTH

    def exchange_start(l, parts):
        lands = [_with_own_slot(lax.dynamic_index_in_dim(p, chip, 0, keepdims=False), chip) for p in parts]
        ici[l] = _copies_start(f"exchange_start_{l}", _exchange_plan, 15, parts, lands)

    def handoff_start(l, after):
        send, recv, srcs, lands, _ = ici[l]
        _, own = _copies_wait(f"exchange_wait_{l}", _exchange_plan, send, recv, srcs, lands, after)
        d2d[l] = _copies_start(f"handoff_start_{l}", _sibling_plan, 5, own,
                               [lax.empty(a.shape, a.dtype) for a in own])

    def update(l, after):
        send, recv, own, lands, _ = d2d[l]
        own, sib = _copies_wait(f"handoff_wait_{l}", _sibling_plan, send, recv, own, lands, after)
        for k in range(5):
            big_out[k] = _adamw_big(l, big_w[k], big_m[k], big_v[k], own[k], sib[k], big_out[k])

    for l in reversed(range(DEPTH)):
        w4, wglu, wbs4, wba4, wout, wb4, wc4, g1, g2, dsk, bgl, proj, ht, states, yraw, ys, ya, o, lse = saved[l]
        started = None
        if l + 1 < DEPTH:
            started = ici[l + 1][4] + d2d[l + 2][4] if l + 2 < DEPTH else ici[l + 1][4]
        dys, dya, dproj, dwbs, dwba, dwout, dg2 = _merge_out_bwd(dx, ys, ya, proj, wbs4, wba4, wout, g2, started)
        dyraw, du_skip, dproj, dwglu, dbgl, ddsk = _s5_tail_bwd(dys, yraw, proj, dsk, wglu, bgl, dproj)
        dproj, dbb, dcc, da = _s5_scan_bwd(dyraw, states, proj, du_skip, wb4, wc4, ar4[l], ai4[l], dproj)
        do, cterm, dproj = _attn_mix_bwd(dya, o, lse, proj, dproj)
        dproj = _attn_core_bwd(proj, do, lse, cterm, dproj)
        dh, dwin = _proj_bwd(dproj, ht, w4)
        parts = [dwin, dwglu.reshape(NCHIP, 128, SSM), dwbs, dwba, dwout.reshape(NCHIP, D // NCHIP, D)]
        exchange_start(l, parts)
        dx, dg1 = _rms_bwd(dh, xs[l], g1, dx, ici[l][4])
        if l + 1 < DEPTH:
            handoff_start(l + 1, [ici[l][4]])
        small["g1"][l], small["g2"][l], small["dsk"][l], small["bgl"][l] = dg1, dg2, ddsk, dbgl
        small["da"][l], small["dbb"][l], small["dcc"][l] = da, dbb, dcc

    dlre, dlim, dldt, dbre_c, dbim_c = _s5_prep_bwd(
        lre_c, lim_c, ldt_c, bre_c, bim_c, jnp.stack(small["da"]).reshape(nb, 1, SW),
        jnp.stack(small["dbb"]).reshape(nb, 16, SW))
    dcc = jnp.stack(small["dcc"]).reshape(nb, 16, SW)
    b_back = lambda t: t.reshape(DEPTH, NCB, 16, 8, 64).transpose(0, 1, 3, 4, 2).reshape(DEPTH, 32, 64, 16)
    c_back = lambda t: t.reshape(DEPTH, NCB, 16, 8, 64).transpose(0, 1, 3, 2, 4).reshape(DEPTH, 32, 16, 64)
    small_w = [pre_norm_g, lambda_re, lambda_im, log_dt, b_re, b_im, c_re, c_im, d_skip, b_glu, post_norm_g]
    small_m = [m_pre_norm_g, m_lambda_re, m_lambda_im, m_log_dt, m_b_re, m_b_im, m_c_re, m_c_im, m_d_skip,
               m_b_glu, m_post_norm_g]
    small_v = [v_pre_norm_g, v_lambda_re, v_lambda_im, v_log_dt, v_b_re, v_b_im, v_c_re, v_c_im, v_d_skip,
               v_b_glu, v_post_norm_g]
    small_g = [jnp.stack(small["g1"]).reshape(DEPTH, D), dlre.reshape(DEPTH, 32, 64), dlim.reshape(DEPTH, 32, 64),
               dldt.reshape(DEPTH, 32, 64).sum(-1), b_back(dbre_c), b_back(dbim_c), c_back(dcc[:, :, :HW]),
               c_back(dcc[:, :, HW:]),
               jnp.stack(small["dsk"]).reshape(DEPTH, SSM), jnp.stack(small["bgl"]).reshape(DEPTH, SSM),
               jnp.stack(small["g2"]).reshape(DEPTH, D)]
    part = _flatten_small(small_g).astype(BF16)
    send, recv, srcs, lands, token = _copies_start("small_start", _everyone_plan, 7, [part],
                                                   [_with_own_slot(part, device, 8)])
    update(DEPTH - 1, [token, ici[0][4]])
    for l in range(DEPTH - 2, 0, -1):
        update(l, [out[0] for out in big_out])
    handoff_start(0, [out[0] for out in big_out])
    update(0, [d2d[0][4]])
    _, (parts8,) = _copies_wait("small_wait", _everyone_plan, send, recv, srcs, lands, [out[0] for out in big_out])
    g_flat, d_flat, nm_flat, nv_flat = _adamw_small(_flatten_small(small_w), parts8, _flatten_small(small_m),
                                                    _flatten_small(small_v))
    sg = _unflatten_small(g_flat, small_w)
    sd = _unflatten_small(d_flat, small_w)
    snm = _unflatten_small(nm_flat, small_w)
    snv = _unflatten_small(nv_flat, small_w)

    def ordered(sm, which):
        bg = [big_out[k][which] for k in range(5)]
        bg[1] = bg[1].reshape(DEPTH, 128, SSM)
        return [sm[0], bg[0], sm[1], sm[2], sm[3], sm[4], sm[5], sm[6], sm[7], sm[8], bg[1], sm[9], bg[2], bg[3],
                bg[4], sm[10]]

    return (loss, dx[None], *ordered(sg, 0), *ordered(sd, 1), *ordered(snm, 2), *ordered(snv, 3))
```

```python
import math

import jax
import jax.numpy as jnp
from jax import lax
from jax.experimental import pallas as pl
from jax.experimental.pallas import tpu as pltpu

F32 = jnp.float32
BF16 = jnp.bfloat16
MESH = pl.DeviceIdType.MESH

DEPTH = 4
L = 2048
D = 1024
NCOL = 8192
SSM = 512
AW = 512
QKV = 1536
RMS_EPS = 1e-6
NCHIP = 4

COL_U, COL_ZS, COL_Q, COL_K, COL_V, COL_ZA, COL_GS, COL_GA = 0, 4, 8, 20, 32, 44, 48, 56

NSEG = 8
SEG = L // NSEG
NCB = 4
HW = 512
SW = 2 * HW

HEAD = 128
DILATIONS = (1, 4, 16)
SCALE = HEAD ** -0.5
NEG = -1e30

ADAM_LR, ADAM_B1, ADAM_B2, ADAM_EPS, ADAM_WD, ADAM_STEP = 0.001, 0.9, 0.999, 1e-08, 0.01, 10


def _sds(shape, dtype=F32):
    return jax.ShapeDtypeStruct(shape, dtype)


def _pcall(body, *, name, out_shape, grid=None, in_specs=None, out_specs=None, scratch=(), vmem_mb=None,
           aliases=None):
    params = {}
    if vmem_mb is not None:
        params["vmem_limit_bytes"] = vmem_mb << 20
    kw = {}
    if grid is not None:
        kw["grid"] = grid
    if in_specs is not None:
        kw["in_specs"] = in_specs
    if out_specs is not None:
        kw["out_specs"] = out_specs
    return pl.pallas_call(body, name=name, out_shape=out_shape, scratch_shapes=list(scratch),
                          compiler_params=pltpu.CompilerParams(**params),
                          input_output_aliases=aliases or {}, **kw)


def _dot(a, b):
    return jnp.dot(a, b, preferred_element_type=F32)


def _dot_nt(a, b):
    return lax.dot_general(a, b, (((1,), (1,)), ((), ())), preferred_element_type=F32)


def _dot_tn(a, b):
    return lax.dot_general(a, b, (((0,), (0,)), ((), ())), preferred_element_type=F32)


def _sigmoid(x):
    return jax.nn.sigmoid(x)


_GELU_K = math.sqrt(2.0 / math.pi)


def _gelu(x):
    return 0.5 * x * (1.0 + jnp.tanh(_GELU_K * (x + 0.044715 * (x * x * x))))


def _gelu_grad(x):
    t = jnp.tanh(_GELU_K * (x + 0.044715 * (x * x * x)))
    return 0.5 * (1.0 + t) + 0.5 * x * (1.0 - t * t) * (_GELU_K * (1.0 + 3.0 * 0.044715 * (x * x)))


def _token_operand(dep):
    if dep is None:
        return [], []
    return [dep], [pl.BlockSpec(memory_space=pl.ANY)]


def _rms_proj_fwd(x, g, w4, dep=None):
    tl, tn = 512, 1024
    ni = L // tl
    deps, dep_specs = _token_operand(dep)

    def body(x_ref, g_ref, w_ref, *rest):
        o_ref, h_ref = rest[len(deps):]
        rows = pl.ds(pl.multiple_of(pl.program_id(1) * tl, tl), tl)

        @pl.when(pl.program_id(0) == 0)
        def _():
            xx = x_ref[...]
            inv = lax.rsqrt(jnp.mean(xx * xx, axis=-1, keepdims=True) + RMS_EPS)
            h_ref[rows, :] = (xx * inv * g_ref[...]).astype(BF16)

        o_ref[...] = _dot(h_ref[rows, :], w_ref[...])

    proj, h = _pcall(
        body, name="rms_proj_fwd", grid=(NCOL // tn, ni),
        in_specs=[pl.BlockSpec((tl, D), lambda j, i: (jnp.where(j == 0, i, ni - 1), 0)),
                  pl.BlockSpec((1, D), lambda j, i: (0, 0)),
                  pl.BlockSpec((None, D, tn), lambda j, i: (lax.div(j, 2), 0, lax.rem(j, 2)))] + dep_specs,
        out_specs=[pl.BlockSpec((tl, tn), lambda j, i: (i, j)), pl.BlockSpec((L, D), lambda j, i: (0, 0))],
        out_shape=[_sds((L, NCOL)), _sds((L, D), BF16)], vmem_mb=48)(x, g, w4, *deps)

    def transpose(h_ref, ht_ref):
        ht_ref[...] = h_ref[...].astype(F32).T.astype(BF16)

    ht = _pcall(transpose, name="transpose_h", grid=(ni,), in_specs=[pl.BlockSpec((tl, D), lambda i: (i, 0))],
                out_specs=pl.BlockSpec((D, tl), lambda i: (0, i)), out_shape=_sds((D, L), BF16), vmem_mb=40)(h)
    return proj, ht


def _proj_bwd(dproj, ht, w4):
    tn = 1024

    def body(dp_ref, ht_ref, w_ref, dh_ref, dw_ref):
        @pl.when(pl.program_id(0) == 0)
        def _():
            dh_ref[...] = jnp.zeros((L, D), F32)

        dw_ref[...] = _dot(ht_ref[...], dp_ref[...]).astype(BF16)
        dh_ref[...] += _dot_nt(dp_ref[...], w_ref[...])

    return _pcall(
        body, name="proj_bwd", grid=(NCOL // tn,),
        in_specs=[pl.BlockSpec((L, tn), lambda j: (0, j)),
                  pl.BlockSpec((D, L), lambda j: (0, 0)),
                  pl.BlockSpec((None, D, tn), lambda j: (lax.div(j, 2), 0, lax.rem(j, 2)))],
        out_specs=[pl.BlockSpec((L, D), lambda j: (0, 0)),
                   pl.BlockSpec((None, D, tn), lambda j: (lax.div(j, 2), 0, lax.rem(j, 2)))],
        out_shape=[_sds((L, D)), _sds((NCHIP, D, NCOL // NCHIP), BF16)], vmem_mb=56)(dproj, ht, w4)


def _rms_bwd(dh, x, g, dxn, dep=None):
    tl = 512
    deps, dep_specs = _token_operand(dep)

    def body(dh_ref, x_ref, g_ref, dxn_ref, *rest):
        dx_ref, dg_ref = rest[len(deps):]
        i = pl.program_id(0)
        xx = x_ref[...]
        inv = lax.rsqrt(jnp.mean(xx * xx, axis=-1, keepdims=True) + RMS_EPS)
        nrm = xx * inv
        dh_v = dh_ref[...]
        dn = dh_v * g_ref[...]
        dx_ref[...] = dxn_ref[...] + inv * (dn - nrm * jnp.mean(dn * nrm, axis=-1, keepdims=True))
        part = jnp.sum(dh_v * nrm, axis=0, keepdims=True)

        @pl.when(i == 0)
        def _():
            dg_ref[...] = part

        @pl.when(i > 0)
        def _():
            dg_ref[...] += part

    row = pl.BlockSpec((tl, D), lambda i: (i, 0))
    vec = pl.BlockSpec((1, D), lambda i: (0, 0))
    return _pcall(body, name="rms_bwd", grid=(L // tl,), in_specs=[row, row, vec, row] + dep_specs,
                  out_specs=[row, vec], out_shape=[_sds((L, D)), _sds((1, D))], vmem_mb=40)(dh, x, g, dxn, *deps)


def _s5_param_math(lre, lim, ldt, bre, bim):
    lr = jnp.minimum(lre, -1e-4)
    dt = jnp.exp(ldt)
    mag = jnp.exp(lr * dt)
    ar = mag * jnp.cos(lim * dt)
    ai = mag * jnp.sin(lim * dt)
    den = lr * lr + lim * lim
    nr = ar - 1.0
    qr = (nr * lr + ai * lim) / den
    qi = (ai * lr - nr * lim) / den
    return ar, ai, qr * bre - qi * bim, qr * bim + qi * bre


def _group_of_lane(width):
    lane = lax.broadcasted_iota(jnp.int32, (16, width), 1)
    return lax.shift_right_logical(jnp.bitwise_and(lane, HW - 1), 6)


def _vec_spec():
    return pl.BlockSpec((None, 1, HW), lambda n: (n, 0, 0))


def _mat_spec():
    return pl.BlockSpec((None, 16, HW), lambda n: (n, 0, 0))


def _s5_prep(lre, lim, ldt, bre, bim, cre, cim):
    n = lre.shape[0]

    def body(lre_ref, lim_ref, ldt_ref, bre_ref, bim_ref, cre_ref, cim_ref, ar_ref, ai_ref, wb_ref, wct_ref):
        ar, ai, bbr, bbi = _s5_param_math(lre_ref[...], lim_ref[...], ldt_ref[...], bre_ref[...], bim_ref[...])
        ar_ref[...] = ar
        ai_ref[...] = ai
        group = _group_of_lane(SW)
        bb = jnp.concatenate([bbr, bbi], axis=1)
        cc = jnp.concatenate([cre_ref[...], -cim_ref[...]], axis=1)
        for g in range(8):
            rows = pl.ds(16 * g, 16)
            wb_ref[rows, :] = jnp.where(group == g, bb, 0.0).astype(BF16)
            wct_ref[rows, :] = jnp.where(group == g, cc, 0.0).astype(BF16)

    wide = pl.BlockSpec((None, 128, SW), lambda n: (n, 0, 0))
    return _pcall(body, name="s5_prep", grid=(n,),
                  in_specs=[_vec_spec()] * 3 + [_mat_spec()] * 4,
                  out_specs=[_vec_spec(), _vec_spec(), wide, wide],
                  out_shape=[_sds((n, 1, HW)), _sds((n, 1, HW)), _sds((n, 128, SW), BF16), _sds((n, 128, SW), BF16)])(
                      lre, lim, ldt, bre, bim, cre, cim)


def _s5_prep_bwd(lre, lim, ldt, bre, bim, ga, gbb):
    n = lre.shape[0]

    def body(lre_ref, lim_ref, ldt_ref, bre_ref, bim_ref, ga_ref, gbb_ref,
             dlre_ref, dlim_ref, dldt_ref, dbre_ref, dbim_ref):
        _, vjp = jax.vjp(_s5_param_math, lre_ref[...], lim_ref[...], ldt_ref[...], bre_ref[...], bim_ref[...])
        ga, gbb = ga_ref[...], gbb_ref[...]
        dlre, dlim, dldt, dbre, dbim = vjp((ga[:, :HW], ga[:, HW:], gbb[:, :HW], gbb[:, HW:]))
        dlre_ref[...] = dlre
        dlim_ref[...] = dlim
        dldt_ref[...] = dldt
        dbre_ref[...] = dbre
        dbim_ref[...] = dbim

    return _pcall(body, name="s5_prep_bwd", grid=(n,),
                  in_specs=[_vec_spec()] * 3 + [_mat_spec()] * 2 +
                  [pl.BlockSpec((None, 1, SW), lambda n: (n, 0, 0)), pl.BlockSpec((None, 16, SW), lambda n: (n, 0, 0))],
                  out_specs=[_vec_spec()] * 3 + [_mat_spec()] * 2,
                  out_shape=[_sds((n, 1, HW))] * 3 + [_sds((n, 16, HW))] * 2)(lre, lim, ldt, bre, bim, ga, gbb)


def _diag_blocks(full):
    group = _group_of_lane(SW)
    out = jnp.where(group == 0, full[0:16, :], 0.0)
    for g in range(1, 8):
        out = out + jnp.where(group == g, full[16 * g:16 * g + 16, :], 0.0)
    return out


def _permute_rows(dst_ref, src_ref):
    for s in range(NSEG):
        dst_ref[pl.ds(s, SEG, stride=NSEG), :] = src_ref[pl.ds(s * SEG, SEG), :]


def _unpermute_rows(src_ref, s):
    return src_ref[pl.ds(s, SEG, stride=NSEG), :]


def _row_of(slab, rows, j):
    return jnp.sum(jnp.where(rows == j, slab, 0.0), axis=0, keepdims=True)


def _s5_scan_fwd(proj, wb4, wc4, ar4, ai4):
    def body(u_ref, wb_ref, wc_ref, ar_ref, ai_ref, x_ref, y_ref, up_ref, d_ref, yp_ref):
        _permute_rows(up_ref, u_ref)
        d_ref[...] = _dot(up_ref[...].astype(BF16), wb_ref[...])
        ar = jnp.broadcast_to(ar_ref[...], (NSEG, HW))
        ai = jnp.broadcast_to(ai_ref[...], (NSEG, HW))
        zero = jnp.zeros((NSEG, HW), F32)

        def drive(t):
            row = pl.multiple_of(t * NSEG, NSEG)
            dd = d_ref[pl.ds(row, NSEG), :]
            return row, dd[:, :HW], dd[:, HW:]

        def local_step(t, c):
            xr, xi = c
            _, dr, di = drive(t)
            return ar * xr - ai * xi + dr, ar * xi + ai * xr + di

        fr, fi = lax.fori_loop(0, SEG, local_step, (zero, zero), unroll=8)
        pr, pi_ = ar_ref[...], ai_ref[...]
        for _ in range(int(math.log2(SEG))):
            pr, pi_ = pr * pr - pi_ * pi_, 2.0 * pr * pi_
        rows = lax.broadcasted_iota(jnp.int32, (NSEG, HW), 0)
        cr, ci = zero, zero
        for j in range(NSEG - 1):
            sr, si = _row_of(cr, rows, j), _row_of(ci, rows, j)
            gr, gi = _row_of(fr, rows, j), _row_of(fi, rows, j)
            nr = pr * sr - pi_ * si + gr
            ni = pr * si + pi_ * sr + gi
            cr = jnp.where(rows == j + 1, nr, cr)
            ci = jnp.where(rows == j + 1, ni, ci)

        def true_step(t, c):
            xr, xi = c
            row, dr, di = drive(t)
            nr = ar * xr - ai * xi + dr
            ni = ar * xi + ai * xr + di
            x_ref[pl.ds(row, NSEG), :] = jnp.concatenate([nr, ni], axis=1)
            return nr, ni

        lax.fori_loop(0, SEG, true_step, (cr, ci), unroll=8)
        yp_ref[...] = _dot_nt(x_ref[...].astype(BF16), wc_ref[...])
        for s in range(NSEG):
            y_ref[pl.ds(s * SEG, SEG), :] = _unpermute_rows(yp_ref, s)

    return _pcall(
        body, name="s5_scan_fwd", grid=(NCB,),
        in_specs=[pl.BlockSpec((L, 128), lambda b: (0, COL_U + b)),
                  pl.BlockSpec((None, 128, SW), lambda b: (b, 0, 0)),
                  pl.BlockSpec((None, 128, SW), lambda b: (b, 0, 0)),
                  pl.BlockSpec((None, 1, HW), lambda b: (b, 0, 0)),
                  pl.BlockSpec((None, 1, HW), lambda b: (b, 0, 0))],
        out_specs=[pl.BlockSpec((None, L, SW), lambda b: (b, 0, 0)),
                   pl.BlockSpec((L, 128), lambda b: (0, b))],
        out_shape=[_sds((NCB, L, SW)), _sds((L, SSM))],
        scratch=[pltpu.VMEM((L, 128), F32), pltpu.VMEM((L, SW), F32), pltpu.VMEM((L, 128), F32)],
        vmem_mb=56)(proj, wb4, wc4, ar4, ai4)


def _s5_scan_bwd(dy, xs, proj, du_skip, wb4, wc4, ar4, ai4, dproj):
    def body(dy_ref, x_ref, u_ref, dus_ref, wb_ref, wc_ref, ar_ref, ai_ref, _,
             du_ref, dbb_ref, dcc_ref, da_ref, dyp_ref, up_ref, g_ref, dup_ref):
        _permute_rows(dyp_ref, dy_ref)
        _permute_rows(up_ref, u_ref)
        dyp = dyp_ref[...].astype(BF16)
        g_ref[...] = _dot(dyp, wc_ref[...])
        dcc = _diag_blocks(_dot_tn(dyp, x_ref[...].astype(BF16)))
        dcc_ref[...] = jnp.concatenate([dcc[:, :HW], -dcc[:, HW:]], axis=1)
        ar = jnp.broadcast_to(ar_ref[...], (NSEG, HW))
        ai = jnp.broadcast_to(ai_ref[...], (NSEG, HW))
        zero = jnp.zeros((NSEG, HW), F32)

        def load(t):
            row = pl.multiple_of(t * NSEG, NSEG)
            gg = g_ref[pl.ds(row, NSEG), :]
            return row, gg[:, :HW], gg[:, HW:]

        def local_step(k, c):
            lr, li = c
            _, gr, gi = load(SEG - 1 - k)
            return ar * lr + ai * li + gr, ar * li - ai * lr + gi

        fr, fi = lax.fori_loop(0, SEG, local_step, (zero, zero), unroll=8)
        pr, pi_ = ar_ref[...], -ai_ref[...]
        for _ in range(int(math.log2(SEG))):
            pr, pi_ = pr * pr - pi_ * pi_, 2.0 * pr * pi_
        rows = lax.broadcasted_iota(jnp.int32, (NSEG, HW), 0)
        cr, ci = zero, zero
        for j in range(NSEG - 1, 0, -1):
            sr, si = _row_of(cr, rows, j), _row_of(ci, rows, j)
            gr, gi = _row_of(fr, rows, j), _row_of(fi, rows, j)
            nr = pr * sr - pi_ * si + gr
            ni = pr * si + pi_ * sr + gi
            cr = jnp.where(rows == j - 1, nr, cr)
            ci = jnp.where(rows == j - 1, ni, ci)

        def true_step(k, c):
            lr, li, gar, gai = c
            t = SEG - 1 - k
            row, gr, gi = load(t)
            nr = ar * lr + ai * li + gr
            ni = ar * li - ai * lr + gi
            g_ref[pl.ds(row, NSEG), :] = jnp.concatenate([nr, ni], axis=1)
            prow = pl.multiple_of(jnp.maximum(t - 1, 0) * NSEG, NSEG)
            xp = x_ref[pl.ds(prow, NSEG), :]
            live = (t > 0).astype(F32)
            xr, xi = xp[:, :HW] * live, xp[:, HW:] * live
            return nr, ni, gar + (nr * xr + ni * xi), gai + (ni * xr - nr * xi)

        _, _, gar, gai = lax.fori_loop(0, SEG, true_step, (cr, ci, zero, zero), unroll=4)
        l0 = g_ref[1:NSEG, :]
        xl = x_ref[(SEG - 1) * NSEG:(SEG - 1) * NSEG + NSEG - 1, :]
        l0r, l0i, xlr, xli = l0[:, :HW], l0[:, HW:], xl[:, :HW], xl[:, HW:]
        gar_t = jnp.sum(gar, axis=0, keepdims=True) + jnp.sum(l0r * xlr + l0i * xli, axis=0, keepdims=True)
        gai_t = jnp.sum(gai, axis=0, keepdims=True) + jnp.sum(l0i * xlr - l0r * xli, axis=0, keepdims=True)
        da_ref[...] = jnp.concatenate([gar_t, gai_t], axis=1)
        lam = g_ref[...].astype(BF16)
        dbb_ref[...] = _diag_blocks(_dot_tn(up_ref[...].astype(BF16), lam))
        dup_ref[...] = _dot_nt(lam, wb_ref[...])
        for s in range(NSEG):
            sl = pl.ds(s * SEG, SEG)
            du_ref[sl, :] = (_unpermute_rows(dup_ref, s) + dus_ref[sl, :]).astype(BF16)

    col = lambda off: pl.BlockSpec((L, 128), lambda b: (0, off + b))
    return _pcall(
        body, name="s5_scan_bwd", grid=(NCB,),
        in_specs=[col(0), pl.BlockSpec((None, L, SW), lambda b: (b, 0, 0)), col(COL_U), col(0),
                  pl.BlockSpec((None, 128, SW), lambda b: (b, 0, 0)),
                  pl.BlockSpec((None, 128, SW), lambda b: (b, 0, 0)),
                  pl.BlockSpec((None, 1, HW), lambda b: (b, 0, 0)),
                  pl.BlockSpec((None, 1, HW), lambda b: (b, 0, 0)),
                  pl.BlockSpec(memory_space=pl.ANY)],
        out_specs=[col(COL_U), pl.BlockSpec((None, 16, SW), lambda b: (b, 0, 0)),
                   pl.BlockSpec((None, 16, SW), lambda b: (b, 0, 0)),
                   pl.BlockSpec((None, 1, SW), lambda b: (b, 0, 0))],
        out_shape=[_sds((L, NCOL), BF16), _sds((NCB, 16, SW)), _sds((NCB, 16, SW)), _sds((NCB, 1, SW))],
        scratch=[pltpu.VMEM((L, 128), F32), pltpu.VMEM((L, 128), F32), pltpu.VMEM((L, SW), F32),
                 pltpu.VMEM((L, 128), F32)],
        aliases={8: 0}, vmem_mb=56)(dy, xs, proj, du_skip, wb4, wc4, ar4, ai4, dproj)


def _s5_tail_fwd(yraw, proj, dsk, wglu, bglu):
    tl = 512

    def body(y_ref, u_ref, z_ref, dsk_ref, w_ref, b_ref, o_ref):
        y1 = y_ref[...] + dsk_ref[...] * u_ref[...]
        y2 = _gelu(y1)
        gl = _dot(y2.astype(BF16), w_ref[...]) + b_ref[...]
        z = z_ref[...]
        o_ref[...] = (y2 * _sigmoid(gl)) * (z * _sigmoid(z))

    blk = lambda c: pl.BlockSpec((tl, SSM), lambda i: (i, c))
    vec = pl.BlockSpec((1, SSM), lambda i: (0, 0))
    return _pcall(body, name="s5_tail_fwd", grid=(L // tl,),
                  in_specs=[blk(0), blk(0), blk(1), vec, pl.BlockSpec((SSM, SSM), lambda i: (0, 0)), vec],
                  out_specs=blk(0), out_shape=_sds((L, SSM)), vmem_mb=40)(yraw, proj, proj, dsk, wglu, bglu)


def _s5_tail_bwd(dys, yraw, proj, dsk, wglu, bglu, dproj):
    tl = 512
    ni = L // tl

    def body(dys_ref, y_ref, u_ref, z_ref, dsk_ref, w_ref, b_ref, _,
             dy_ref, dus_ref, dz_ref, dw_ref, db_ref, dd_ref, acc_ref):
        i = pl.program_id(0)
        u = u_ref[...]
        y1 = y_ref[...] + dsk_ref[...] * u
        y2 = _gelu(y1)
        y2b = y2.astype(BF16)
        sg = _sigmoid(_dot(y2b, w_ref[...]) + b_ref[...])
        y3 = y2 * sg
        z = z_ref[...]
        sz = _sigmoid(z)
        dys = dys_ref[...]
        dy3 = dys * (z * sz)
        dz_ref[...] = (dys * y3 * (sz * (1.0 + z * (1.0 - sz)))).astype(BF16)
        dgl = (dy3 * y2) * (sg * (1.0 - sg))
        dglb = dgl.astype(BF16)
        dy2 = dy3 * sg + _dot_nt(dglb, w_ref[...])
        dy1 = dy2 * _gelu_grad(y1)
        dy_ref[...] = dy1
        dus_ref[...] = dsk_ref[...] * dy1
        dw = _dot_tn(y2b, dglb)
        db = jnp.sum(dgl, axis=0, keepdims=True)
        dd = jnp.sum(dy1 * u, axis=0, keepdims=True)

        @pl.when(i == 0)
        def _():
            acc_ref[...] = dw
            db_ref[...] = db
            dd_ref[...] = dd

        @pl.when(i > 0)
        def _():
            acc_ref[...] += dw
            db_ref[...] += db
            dd_ref[...] += dd

        @pl.when(i == ni - 1)
        def _():
            dw_ref[...] = acc_ref[...].astype(BF16)

    blk = lambda c: pl.BlockSpec((tl, SSM), lambda i: (i, c))
    vec = pl.BlockSpec((1, SSM), lambda i: (0, 0))
    mat = pl.BlockSpec((SSM, SSM), lambda i: (0, 0))
    return _pcall(body, name="s5_tail_bwd", grid=(ni,),
                  in_specs=[blk(0), blk(0), blk(0), blk(1), vec, mat, vec, pl.BlockSpec(memory_space=pl.ANY)],
                  out_specs=[blk(0), blk(0), blk(COL_ZS // 4), mat, vec, vec],
                  out_shape=[_sds((L, SSM)), _sds((L, SSM)), _sds((L, NCOL), BF16), _sds((SSM, SSM), BF16),
                             _sds((1, SSM)), _sds((1, SSM))],
                  scratch=[pltpu.VMEM((SSM, SSM), F32)], aliases={7: 2},
                  vmem_mb=40)(dys, yraw, proj, proj, dsk, wglu, bglu, dproj)


def _attn_blocks(dil):
    n = L // dil
    return [(i * HEAD * dil + r, r * n + i * HEAD, i == 0) for r in range(dil) for i in range(n // HEAD)]


def _rows(start, dil):
    return pl.ds(start, HEAD) if dil == 1 else pl.ds(start, HEAD, stride=dil)


def _residue_rows(r, dil):
    n = L // dil
    return pl.ds(0, L) if dil == 1 else pl.ds(r, n, stride=dil)


def _gather_residues(pairs, dil, dtype=BF16):
    n = L // dil
    for src, dst in pairs:
        for r in range(dil):
            dst[pl.ds(r * n, n), :] = src[_residue_rows(r, dil), :].astype(dtype)


def _band_masks():
    ri = lax.broadcasted_iota(jnp.int32, (HEAD, HEAD), 0)
    ci = lax.broadcasted_iota(jnp.int32, (HEAD, HEAD), 1)
    ri2 = lax.broadcasted_iota(jnp.int32, (HEAD, 2 * HEAD), 0)
    ci2 = lax.broadcasted_iota(jnp.int32, (HEAD, 2 * HEAD), 1)
    return ci <= ri, jnp.logical_and(ci2 >= ri2, ci2 - HEAD <= ri2)


def _qkv_specs(index):
    def spec(off):
        return pl.BlockSpec((L, HEAD), lambda gi, h: (0, off + index(gi, h)))
    return [spec(off) for off in (COL_Q, COL_K, COL_V)]


def _attn_core_fwd(proj):
    def body(q_ref, k_ref, v_ref, o_ref, lse_ref, qp_ref, kp_ref, vp_ref):
        gi = pl.program_id(0)
        m_cur, m_both = _band_masks()
        for g, dil in enumerate(DILATIONS):
            @pl.when(gi == g)
            def _(dil=dil):
                _gather_residues(((q_ref, qp_ref), (k_ref, kp_ref), (v_ref, vp_ref)), dil)
                for st, base, first in _attn_blocks(dil):
                    q = qp_ref[pl.ds(base, HEAD), :]
                    keys = pl.ds(base, HEAD) if first else pl.ds(base - HEAD, 2 * HEAD)
                    s = jnp.where(m_cur if first else m_both, _dot_nt(q, kp_ref[keys, :]) * SCALE, NEG)
                    mx = jnp.max(s, axis=-1, keepdims=True)
                    p = jnp.exp(s - mx)
                    den = jnp.sum(p, axis=-1, keepdims=True)
                    r = _rows(st, dil)
                    o_ref[r, :] = _dot(p.astype(BF16), vp_ref[keys, :]) / den
                    lse_ref[r, :] = jnp.broadcast_to(mx + jnp.log(den), (HEAD, HEAD))

    idx = lambda gi, h: gi * 4 + h
    out = pl.BlockSpec((L, HEAD), lambda gi, h: (0, gi * 4 + h))
    return _pcall(body, name="attn_core_fwd", grid=(3, 4), in_specs=_qkv_specs(idx), out_specs=[out, out],
                  out_shape=[_sds((L, QKV)), _sds((L, QKV))], scratch=[pltpu.VMEM((L, HEAD), BF16)] * 3,
                  vmem_mb=40)(proj, proj, proj)


def _attn_mix_fwd(o, lse, proj):
    tl = 512

    def body(o_ref, l_ref, z_ref, y_ref):
        for h in range(4):
            c = [pl.ds((g * 4 + h) * HEAD, HEAD) for g in range(3)]
            ls = [l_ref[:, c[g]] for g in range(3)]
            m = jnp.maximum(jnp.maximum(ls[0], ls[1]), ls[2])
            e = [jnp.exp(ls[g] - m) for g in range(3)]
            y = (e[0] * o_ref[:, c[0]] + e[1] * o_ref[:, c[1]] + e[2] * o_ref[:, c[2]]) / (e[0] + e[1] + e[2])
            z = z_ref[:, pl.ds(h * HEAD, HEAD)]
            y_ref[:, pl.ds(h * HEAD, HEAD)] = y * (z * _sigmoid(z))

    wide = pl.BlockSpec((tl, QKV), lambda i: (i, 0))
    return _pcall(body, name="attn_mix_fwd", grid=(L // tl,),
                  in_specs=[wide, wide, pl.BlockSpec((tl, AW), lambda i: (i, COL_ZA // 4))],
                  out_specs=pl.BlockSpec((tl, AW), lambda i: (i, 0)), out_shape=_sds((L, AW)),
                  vmem_mb=40)(o, lse, proj)


def _attn_mix_bwd(dya, o, lse, proj, dproj):
    tl = 512

    def body(dya_ref, o_ref, l_ref, z_ref, _, do_ref, c_ref, dz_ref):
        for h in range(4):
            c = [pl.ds((g * 4 + h) * HEAD, HEAD) for g in range(3)]
            hs = pl.ds(h * HEAD, HEAD)
            ls = [l_ref[:, c[g]] for g in range(3)]
            m = jnp.maximum(jnp.maximum(ls[0], ls[1]), ls[2])
            e = [jnp.exp(ls[g] - m) for g in range(3)]
            den = e[0] + e[1] + e[2]
            al = [e[g] / den for g in range(3)]
            y = al[0] * o_ref[:, c[0]] + al[1] * o_ref[:, c[1]] + al[2] * o_ref[:, c[2]]
            z = z_ref[:, hs]
            sz = _sigmoid(z)
            dya = dya_ref[:, hs]
            dz_ref[:, hs] = (dya * y * (sz * (1.0 + z * (1.0 - sz)))).astype(BF16)
            dy = dya * (z * sz)
            tot = jnp.sum(dy * y, axis=-1, keepdims=True)
            for g in range(3):
                do_ref[:, c[g]] = al[g] * dy
                c_ref[:, c[g]] = -(al[g] * tot)

    wide = pl.BlockSpec((tl, QKV), lambda i: (i, 0))
    nar = pl.BlockSpec((tl, AW), lambda i: (i, 0))
    za = pl.BlockSpec((tl, AW), lambda i: (i, COL_ZA // 4))
    return _pcall(body, name="attn_mix_bwd", grid=(L // tl,),
                  in_specs=[nar, wide, wide, za, pl.BlockSpec(memory_space=pl.ANY)],
                  out_specs=[wide, wide, za], out_shape=[_sds((L, QKV)), _sds((L, QKV)), _sds((L, NCOL), BF16)],
                  aliases={4: 2}, vmem_mb=48)(dya, o, lse, proj, dproj)


def _attn_core_bwd(proj, do, lse, cc, dproj):
    def body(q_ref, k_ref, v_ref, do_ref, lse_ref, c_ref, _, dp_ref,
             qp_ref, kp_ref, vp_ref, dop_ref, lsep_ref, cp_ref, dqp_ref, dkp_ref, dvp_ref, tok_ref, out_ref, sems):
        gi = pl.program_id(0)
        head = gi * 4 + pl.program_id(1)
        m_cur, m_both = _band_masks()
        dkp_ref[...] = jnp.zeros((L, HEAD), F32)
        dvp_ref[...] = jnp.zeros((L, HEAD), F32)
        copies = []
        for g, dil in enumerate(DILATIONS):
            @pl.when(gi == g)
            def _(dil=dil):
                _gather_residues(((q_ref, qp_ref), (k_ref, kp_ref), (v_ref, vp_ref), (do_ref, dop_ref)), dil)
                _gather_residues(((lse_ref, lsep_ref), (c_ref, cp_ref)), dil, F32)
                for _, base, first in _attn_blocks(dil):
                    mine = pl.ds(base, HEAD)
                    keys = mine if first else pl.ds(base - HEAD, 2 * HEAD)
                    q, do_b, kb = qp_ref[mine, :], dop_ref[mine, :], kp_ref[keys, :]
                    lse_b, c_b = lsep_ref[mine, :], cp_ref[mine, :]
                    if not first:
                        lse_b = jnp.concatenate([lse_b, lse_b], axis=1)
                        c_b = jnp.concatenate([c_b, c_b], axis=1)
                    s = _dot_nt(q, kb) * SCALE
                    p = jnp.where(m_cur if first else m_both, jnp.exp(s - lse_b), 0.0)
                    ds = (p * (_dot_nt(do_b, vp_ref[keys, :]) + c_b) * SCALE).astype(BF16)
                    dqp_ref[mine, :] = _dot(ds, kb)
                    dkp_ref[keys, :] += _dot_tn(ds, q)
                    dvp_ref[keys, :] += _dot_tn(p.astype(BF16), do_b)
                n = L // dil
                for slot, src in enumerate((dqp_ref, dkp_ref, dvp_ref)):
                    for r in range(dil):
                        tok_ref[_residue_rows(r, dil), :] = src[pl.ds(r * n, n), :]
                    out_ref[slot] = tok_ref[...].astype(BF16)

        for slot, off in enumerate((COL_Q, COL_K, COL_V)):
            cols = pl.ds(pl.multiple_of((off + head) * HEAD, HEAD), HEAD)
            copies.append(pltpu.make_async_copy(out_ref.at[slot], dp_ref.at[:, cols], sems.at[slot]))
            copies[-1].start()
        for cp in copies:
            cp.wait()

    idx = lambda gi, h: gi * 4 + h
    blk = pl.BlockSpec((L, HEAD), lambda gi, h: (0, gi * 4 + h))
    hbm = pl.BlockSpec(memory_space=pl.ANY)
    return _pcall(body, name="attn_core_bwd", grid=(3, 4), in_specs=_qkv_specs(idx) + [blk, blk, blk, hbm],
                  out_specs=hbm, out_shape=_sds((L, NCOL), BF16),
                  scratch=[pltpu.VMEM((L, HEAD), BF16)] * 4 + [pltpu.VMEM((L, HEAD), F32)] * 6 +
                  [pltpu.VMEM((3, L, HEAD), BF16), pltpu.SemaphoreType.DMA((3,))],
                  aliases={6: 0}, vmem_mb=48)(proj, proj, proj, do, lse, cc, dproj)


def _merge_fwd_math(ys_b, ya_b, gs, ga, wbs_ref, wba_ref, wout_ref):
    bs = jnp.concatenate([_dot(ys_b, wbs_ref[s]) for s in range(NCHIP)], axis=1)
    ba = jnp.concatenate([_dot(ya_b, wba_ref[s]) for s in range(NCHIP)], axis=1)
    sgs, sga = _sigmoid(gs), _sigmoid(ga)
    merged = sgs * bs + sga * ba
    out = _dot(merged.astype(BF16), wout_ref[...])
    inv = lax.rsqrt(jnp.mean(out * out, axis=-1, keepdims=True) + RMS_EPS)
    return bs, ba, sgs, sga, merged, out, inv


def _merge_specs(tl):
    row = lambda w, c: pl.BlockSpec((tl, w), lambda i: (i, c))
    return [row(SSM, 0), row(AW, 0), row(D, COL_GS // 8), row(D, COL_GA // 8),
            pl.BlockSpec((NCHIP, SSM, D // NCHIP), lambda i: (0, 0, 0)),
            pl.BlockSpec((NCHIP, AW, D // NCHIP), lambda i: (0, 0, 0)),
            pl.BlockSpec((D, D), lambda i: (0, 0)),
            pl.BlockSpec((1, D), lambda i: (0, 0))]


def _merge_out_fwd(ys, ya, proj, wbs4, wba4, wout, g2, x, dep=None):
    tl = 512
    deps, dep_specs = _token_operand(dep)

    def body(ys_ref, ya_ref, gs_ref, ga_ref, wbs_ref, wba_ref, wout_ref, g2_ref, x_ref, *rest):
        o_ref = rest[-1]
        *_, out, inv = _merge_fwd_math(ys_ref[...].astype(BF16), ya_ref[...].astype(BF16), gs_ref[...],
                                       ga_ref[...], wbs_ref, wba_ref, wout_ref)
        o_ref[...] = x_ref[...] + out * inv * g2_ref[...]

    row = pl.BlockSpec((tl, D), lambda i: (i, 0))
    return _pcall(body, name="merge_out_fwd", grid=(L // tl,), in_specs=_merge_specs(tl) + [row] + dep_specs,
                  out_specs=row, out_shape=_sds((L, D)),
                  vmem_mb=48)(ys, ya, proj, proj, wbs4, wba4, wout, g2, x, *deps)


def _merge_out_bwd(dxn, ys, ya, proj, wbs4, wba4, wout, g2, dep=None):
    tl = 256
    ni = L // tl
    cw = D // NCHIP
    deps, dep_specs = _token_operand(dep)

    def body(dxn_ref, ys_ref, ya_ref, gs_ref, ga_ref, wbs_ref, wba_ref, wout_ref, g2_ref, *rest):
        (dys_ref, dya_ref, dgate_ref, dwbs_ref, dwba_ref, dwout_ref, dg2_ref,
         abs_ref, aba_ref, aout_ref) = rest[len(deps):]
        i = pl.program_id(0)

        @pl.when(i == 0)
        def _():
            abs_ref[...] = jnp.zeros(abs_ref.shape, F32)
            aba_ref[...] = jnp.zeros(aba_ref.shape, F32)
            aout_ref[...] = jnp.zeros(aout_ref.shape, F32)
            dg2_ref[...] = jnp.zeros(dg2_ref.shape, F32)

        ys_b, ya_b = ys_ref[...].astype(BF16), ya_ref[...].astype(BF16)
        bs, ba, sgs, sga, merged, out, inv = _merge_fwd_math(ys_b, ya_b, gs_ref[...], ga_ref[...],
                                                             wbs_ref, wba_ref, wout_ref)
        nrm = out * inv
        dxn = dxn_ref[...]
        dg2_ref[...] += jnp.sum(dxn * nrm, axis=0, keepdims=True)
        dn = dxn * g2_ref[...]
        dout = (inv * (dn - nrm * jnp.mean(dn * nrm, axis=-1, keepdims=True))).astype(BF16)
        aout_ref[...] += _dot_tn(merged.astype(BF16), dout)
        dm = _dot_nt(dout, wout_ref[...])
        dbs, dba = dm * sgs, dm * sga
        dgate_ref[:, :D] = (dm * bs * (sgs * (1.0 - sgs))).astype(BF16)
        dgate_ref[:, D:] = (dm * ba * (sga * (1.0 - sga))).astype(BF16)
        dbs_b, dba_b = dbs.astype(BF16), dba.astype(BF16)
        dys = None
        dya = None
        for s in range(NCHIP):
            cs = slice(s * cw, (s + 1) * cw)
            p_s = _dot_nt(dbs_b[:, cs], wbs_ref[s])
            p_a = _dot_nt(dba_b[:, cs], wba_ref[s])
            dys = p_s if dys is None else dys + p_s
            dya = p_a if dya is None else dya + p_a
            abs_ref[s] += _dot_tn(ys_b, dbs_b[:, cs])
            aba_ref[s] += _dot_tn(ya_b, dba_b[:, cs])

        dys_ref[...] = dys
        dya_ref[...] = dya

        @pl.when(i == ni - 1)
        def _():
            dwbs_ref[...] = abs_ref[...].astype(BF16)
            dwba_ref[...] = aba_ref[...].astype(BF16)
            dwout_ref[...] = aout_ref[...].astype(BF16)

    row = lambda w: pl.BlockSpec((tl, w), lambda i: (i, 0))
    w4 = pl.BlockSpec((NCHIP, SSM, cw), lambda i: (0, 0, 0))
    sq = pl.BlockSpec((D, D), lambda i: (0, 0))
    vec = pl.BlockSpec((1, D), lambda i: (0, 0))
    return _pcall(body, name="merge_out_bwd", grid=(ni,), in_specs=[row(D)] + _merge_specs(tl) + dep_specs,
                  out_specs=[row(SSM), row(AW), pl.BlockSpec((tl, 2 * D), lambda i: (i, COL_GS // 16)), w4, w4, sq,
                             vec],
                  out_shape=[_sds((L, SSM)), _sds((L, AW)), _sds((L, NCOL), BF16),
                             _sds((NCHIP, SSM, cw), BF16), _sds((NCHIP, AW, cw), BF16), _sds((D, D), BF16),
                             _sds((1, D))],
                  scratch=[pltpu.VMEM((NCHIP, SSM, cw), F32), pltpu.VMEM((NCHIP, AW, cw), F32),
                           pltpu.VMEM((D, D), F32)],
                  vmem_mb=56)(dxn, ys, ya, proj, proj, wbs4, wba4, wout, g2, *deps)


def _loss_head(y, target):
    tl = 512

    def body(y_ref, t_ref, loss_ref, dy_ref):
        i = pl.program_id(0)
        err = y_ref[...] - t_ref[...]
        dy_ref[...] = err / D
        part = 0.5 * jnp.sum(jnp.mean(err * err, axis=-1, keepdims=True), axis=0, keepdims=True)
        part = jnp.broadcast_to(part, (8, 128))

        @pl.when(i == 0)
        def _():
            loss_ref[...] = part

        @pl.when(i > 0)
        def _():
            loss_ref[...] += part

    row = pl.BlockSpec((tl, D), lambda i: (i, 0))
    return _pcall(body, name="loss_head", grid=(L // tl,), in_specs=[row, row],
                  out_specs=[pl.BlockSpec((8, 128), lambda i: (0, 0)), row],
                  out_shape=[_sds((8, 128)), _sds((L, D))], vmem_mb=40)(y, target)


def _adamw_math(w, g, m, v):
    m = ADAM_B1 * m + (1.0 - ADAM_B1) * g
    v = ADAM_B2 * v + (1.0 - ADAM_B2) * (g * g)
    m_hat = m / (1.0 - ADAM_B1 ** ADAM_STEP)
    v_hat = v / (1.0 - ADAM_B2 ** ADAM_STEP)
    delta = -ADAM_LR * (m_hat / (jnp.sqrt(v_hat) + ADAM_EPS) + ADAM_WD * w)
    return delta, m, v


def _adamw_big(layer, w, m, v, own, sib, prev):
    _, r, c = w.shape
    tr = min(r, 128)

    def body(w_ref, m_ref, v_ref, own_ref, sib_ref, *rest):
        g_ref, d_ref, nm_ref, nv_ref = rest[-4:]
        a = own_ref[0].astype(F32)
        b = sib_ref[0].astype(F32)
        for s in range(1, NCHIP):
            a = a + own_ref[s].astype(F32)
            b = b + sib_ref[s].astype(F32)
        g = a + b
        delta, nm, nv = _adamw_math(w_ref[...], g, m_ref[...], v_ref[...])
        g_ref[...] = g
        d_ref[...] = delta
        nm_ref[...] = nm
        nv_ref[...] = nv

    lay = pl.BlockSpec((None, tr, c), lambda i: (layer, i, 0))
    slots = pl.BlockSpec((NCHIP, tr, c), lambda i: (0, i, 0))
    ins = [w, m, v, own, sib]
    in_specs = [lay, lay, lay, slots, slots]
    aliases = {}
    if prev is not None:
        ins += list(prev)
        in_specs += [pl.BlockSpec(memory_space=pl.ANY)] * 4
        aliases = {5 + k: k for k in range(4)}
    return _pcall(body, name="adamw_big", grid=(r // tr,), in_specs=in_specs, out_specs=[lay] * 4,
                  out_shape=[_sds(w.shape)] * 4, aliases=aliases, vmem_mb=48)(*ins)


SMALL_TILE = 512


def _adamw_small(w, parts, m, v):
    r = w.shape[0]

    def body(w_ref, p_ref, m_ref, v_ref, g_ref, d_ref, nm_ref, nv_ref):
        g = p_ref[0].astype(F32)
        for dev in range(1, 8):
            g = g + p_ref[dev].astype(F32)
        delta, nm, nv = _adamw_math(w_ref[...], g, m_ref[...], v_ref[...])
        g_ref[...] = g
        d_ref[...] = delta
        nm_ref[...] = nm
        nv_ref[...] = nv

    row = pl.BlockSpec((SMALL_TILE, 128), lambda i: (i, 0))
    return _pcall(body, name="adamw_small", grid=(r // SMALL_TILE,),
                  in_specs=[row, pl.BlockSpec((8, SMALL_TILE, 128), lambda i: (0, i, 0)), row, row],
                  out_specs=[row] * 4, out_shape=[_sds((r, 128))] * 4, vmem_mb=40)(w, parts, m, v)


def _place():
    x, y, c = lax.axis_index("x"), lax.axis_index("y"), lax.axis_index("c")
    return x, y, c, 2 * x + y


def _chip_peer(x, y, j):
    return (1 - x if j & 2 else x), (1 - y if j & 1 else y)


_HBM = pl.BlockSpec(memory_space=pltpu.HBM)
_SEM = pl.BlockSpec(memory_space=pltpu.SEMAPHORE)
_EFFECT = pltpu.SideEffectType.DATAFLOW_SIDE_EFFECTING


def _in_hbm(a):
    return pltpu.with_memory_space_constraint(a, pltpu.HBM)


def _copies_start(name, plan, ncopy, srcs, lands, dep=None):
    n = len(srcs) + len(lands)
    deps, dep_specs = _token_operand(dep)

    def body(*refs):
        send_sems, recv_sems = refs[n + len(deps)], refs[n + len(deps) + 1]
        token = refs[-1]
        for k, (src, dst, dev) in enumerate(plan(refs[:len(srcs)], refs[len(srcs):n])):
            if dev is None:
                pltpu.make_async_copy(src, dst, recv_sems.at[k]).start()
            else:
                pltpu.make_async_remote_copy(src_ref=src, dst_ref=dst, send_sem=send_sems.at[k],
                                             recv_sem=recv_sems.at[k], device_id=dev, device_id_type=MESH).start()
        token[...] = jnp.zeros_like(token)

    bufs = list(srcs) + list(lands)
    outs = pl.pallas_call(
        body, name=name,
        out_shape=(pltpu.SemaphoreType.DMA((ncopy,)), pltpu.SemaphoreType.DMA((ncopy,)),
                   *[pltpu.HBM(a.shape, a.dtype) for a in bufs], _sds((8, 128))),
        in_specs=[_HBM] * n + dep_specs,
        out_specs=(_SEM, _SEM, *[_HBM] * n, pl.BlockSpec(memory_space=pltpu.VMEM)),
        input_output_aliases={i: 2 + i for i in range(n)},
        compiler_params=pltpu.CompilerParams(has_side_effects=_EFFECT),
    )(*[_in_hbm(a) for a in bufs], *deps)
    return outs[0], outs[1], list(outs[2:2 + len(srcs)]), list(outs[2 + len(srcs):2 + n]), outs[-1]


def _copies_wait(name, plan, send_sems, recv_sems, srcs, lands, after):
    n = len(srcs) + len(lands)
    after = list(after)

    def body(*refs):
        s_sems, r_sems = refs[n], refs[n + 1]
        for k, (src, dst, dev) in enumerate(plan(refs[:len(srcs)], refs[len(srcs):n])):
            if dev is None:
                pltpu.make_async_copy(src, dst, r_sems.at[k]).wait()
                continue
            cp = pltpu.make_async_remote_copy(src_ref=src, dst_ref=dst, send_sem=s_sems.at[k],
                                              recv_sem=r_sems.at[k], device_id=dev, device_id_type=MESH)
            cp.wait_send()
            cp.wait_recv()

    bufs = list(srcs) + list(lands)
    outs = pl.pallas_call(
        body, name=name,
        out_shape=tuple(pltpu.HBM(a.shape, a.dtype) for a in bufs),
        in_specs=[_HBM] * n + [_SEM, _SEM] + [pl.BlockSpec(memory_space=pl.ANY)] * len(after),
        out_specs=tuple([_HBM] * n),
        input_output_aliases={i: i for i in range(n)},
        compiler_params=pltpu.CompilerParams(has_side_effects=_EFFECT),
    )(*bufs, send_sems, recv_sems, *after)
    return list(outs[:len(srcs)]), list(outs[len(srcs):])


def _with_own_slot(block, slot, nslots=NCHIP):
    land = lax.empty((nslots,) + block.shape, block.dtype)
    return lax.dynamic_update_slice(land, block[None], (slot,) + (0,) * block.ndim)


def _my_half(land, slot, c):
    half = land.shape[1] // 2
    return land.at[slot, pl.ds(pl.multiple_of(c * half, 16), half)]


def _gather_ici_plan(srcs, lands):
    x, y, c, s = _place()
    return [(_my_half(land, s, c), _my_half(land, s, c), (*_chip_peer(x, y, j), c))
            for land in lands for j in range(1, NCHIP)]


def _gather_d2d_plan(srcs, lands):
    x, y, c, s = _place()
    return [(_my_half(land, s ^ j, c), _my_half(land, s ^ j, c), (x, y, 1 - c))
            for land in lands for j in range(1, NCHIP)]


def _exchange_plan(srcs, lands):
    x, y, c, s = _place()
    return [(src.at[s ^ j], land.at[s], (*_chip_peer(x, y, j), c) if j else None)
            for src, land in zip(srcs, lands) for j in range(NCHIP)]


def _sibling_plan(srcs, lands):
    x, y, c, _ = _place()
    return [(src, land, (x, y, 1 - c)) for src, land in zip(srcs, lands)]


def _everyone_plan(srcs, lands):
    x, y, c, _ = _place()
    me = 4 * x + 2 * y + c
    return [(srcs[0], lands[0].at[me], (*_chip_peer(x, y, j >> 1), 1 - c if j & 1 else c)) for j in range(1, 8)]


def _flatten_small(parts):
    flat = jnp.concatenate([p.reshape(-1) for p in parts])
    n = flat.shape[0]
    rows = -(-n // (128 * SMALL_TILE)) * SMALL_TILE
    return jnp.pad(flat, (0, rows * 128 - n)).reshape(rows, 128)


def _unflatten_small(buf, like):
    flat = buf.reshape(-1)
    out, at = [], 0
    for p in like:
        out.append(flat[at:at + p.size].reshape(p.shape))
        at += p.size
    return out


def kernel(x, pre_norm_g, w_in, lambda_re, lambda_im, log_dt, b_re, b_im, c_re, c_im, d_skip, w_glu, b_glu, w_branch_s, w_branch_a, w_out, post_norm_g, loss_target, m_pre_norm_g, m_w_in, m_lambda_re, m_lambda_im, m_log_dt, m_b_re, m_b_im, m_c_re, m_c_im, m_d_skip, m_w_glu, m_b_glu, m_w_branch_s, m_w_branch_a, m_w_out, m_post_norm_g, v_pre_norm_g, v_w_in, v_lambda_re, v_lambda_im, v_log_dt, v_b_re, v_b_im, v_c_re, v_c_im, v_d_skip, v_w_glu, v_b_glu, v_w_branch_s, v_w_branch_a, v_w_out, v_post_norm_g):
    nb = DEPTH * NCB
    lre_c = lambda_re.reshape(nb, 1, HW)
    lim_c = lambda_im.reshape(nb, 1, HW)
    ldt_c = jnp.broadcast_to(log_dt[:, :, None], (DEPTH, 32, 64)).reshape(nb, 1, HW)
    b_rows = lambda t: t.reshape(DEPTH, NCB, 8, 64, 16).transpose(0, 1, 4, 2, 3).reshape(nb, 16, HW)
    c_rows = lambda t: t.reshape(DEPTH, NCB, 8, 16, 64).transpose(0, 1, 3, 2, 4).reshape(nb, 16, HW)
    bre_c, bim_c = b_rows(b_re), b_rows(b_im)
    ar, ai, wb, wct = _s5_prep(lre_c, lim_c, ldt_c, bre_c, bim_c, c_rows(c_re), c_rows(c_im))
    ar4 = ar.reshape(DEPTH, NCB, 1, HW)
    ai4 = ai.reshape(DEPTH, NCB, 1, HW)
    wb = wb.reshape(DEPTH, NCB, 128, SW)
    wct = wct.reshape(DEPTH, NCB, 128, SW)

    chip = 2 * lax.axis_index("x") + lax.axis_index("y")
    device = 2 * chip + lax.axis_index("c")

    def gather_start(l, dep):
        blocks = [w_in[l], w_glu[l], w_branch_s[l], w_branch_a[l], w_out[l]]
        if l == 0:
            blocks = lax.optimization_barrier(blocks)
        lands = [_with_own_slot(a.astype(BF16), chip) for a in blocks]
        return _copies_start(f"gather_start_{l}", _gather_ici_plan, 15, [], lands, dep)

    def gather_forward(l, flight, after):
        send, recv, _, lands, _ = flight
        _, lands = _copies_wait(f"gather_wait_{l}", _gather_ici_plan, send, recv, [], lands, after)
        return _copies_start(f"forward_start_{l}", _gather_d2d_plan, 15, [], lands)

    def gather_done(l, forward, after):
        send, recv, _, lands, _ = forward
        return _copies_wait(f"forward_wait_{l}", _gather_d2d_plan, send, recv, [], lands, after)[1]

    xs = [x[0]]
    saved = []
    forward = gather_forward(0, gather_start(0, None), [xs[0]])
    for l in range(DEPTH):
        w4, wglu4, wbs4, wba4, wout4 = gather_done(l, forward, [xs[l]])
        started = None
        if l + 1 < DEPTH:
            flight = gather_start(l + 1, wglu4)
            started = flight[4]
        wglu = wglu4.reshape(SSM, SSM)
        wout = wout4.reshape(D, D)
        wb4, wc4 = wb[l], wct[l]
        g1 = pre_norm_g[l].reshape(1, D)
        g2 = post_norm_g[l].reshape(1, D)
        dsk = d_skip[l].reshape(1, SSM)
        bgl = b_glu[l].reshape(1, SSM)
        proj, ht = _rms_proj_fwd(xs[l], g1, w4, started)
        states, yraw = _s5_scan_fwd(proj, wb4, wc4, ar4[l], ai4[l])
        o, lse = _attn_core_fwd(proj)
        forwarded = None
        if l + 1 < DEPTH:
            forward = gather_forward(l + 1, flight, [yraw, o])
            forwarded = forward[4]
        ys = _s5_tail_fwd(yraw, proj, dsk, wglu, bgl)
        ya = _attn_mix_fwd(o, lse, proj)
        xs.append(_merge_out_fwd(ys, ya, proj, wbs4, wba4, wout, g2, xs[l], forwarded))
        saved.append((w4, wglu, wbs4, wba4, wout, wb4, wc4, g1, g2, dsk, bgl, proj, ht, states, yraw, ys, ya, o, lse))

    loss_part, dx = _loss_head(xs[DEPTH], loss_target[0])
    loss = lax.psum(loss_part[0, 0], ("x", "y", "c"))

    big_w = (w_in, w_glu.reshape(DEPTH, 128, SSM), w_branch_s, w_branch_a, w_out)
    big_m = (m_w_in, m_w_glu, m_w_branch_s, m_w_branch_a, m_w_out)
    big_v = (v_w_in, v_w_glu, v_w_branch_s, v_w_branch_a, v_w_out)
    big_out = [None] * 5
    small = {k: [None] * DEPTH for k in ("g1", "da", "dbb", "dcc", "dsk", "bgl", "g2")}
    ici = [None] * DEPTH
    d2d = [None] * DEPTH

    def exchange_start(l, parts):
        lands = [lax.empty(p.shape, p.dtype) for p in parts]
        ici[l] = _copies_start(f"exchange_start_{l}", _exchange_plan, 20, parts, lands)

    def handoff_start(l, after):
        send, recv, srcs, lands, _ = ici[l]
        _, own = _copies_wait(f"exchange_wait_{l}", _exchange_plan, send, recv, srcs, lands, after)
        d2d[l] = _copies_start(f"handoff_start_{l}", _sibling_plan, 5, own,
                               [lax.empty(a.shape, a.dtype) for a in own])

    def update(l, after):
        send, recv, own, lands, _ = d2d[l]
        own, sib = _copies_wait(f"handoff_wait_{l}", _sibling_plan, send, recv, own, lands, after)
        for k in range(5):
            big_out[k] = _adamw_big(l, big_w[k], big_m[k], big_v[k], own[k], sib[k], big_out[k])

    for l in reversed(range(DEPTH)):
        w4, wglu, wbs4, wba4, wout, wb4, wc4, g1, g2, dsk, bgl, proj, ht, states, yraw, ys, ya, o, lse = saved[l]
        started = None
        if l + 1 < DEPTH:
            started = ici[l + 1][4] + d2d[l + 2][4] if l + 2 < DEPTH else ici[l + 1][4]
        dys, dya, dproj, dwbs, dwba, dwout, dg2 = _merge_out_bwd(dx, ys, ya, proj, wbs4, wba4, wout, g2, started)
        dyraw, du_skip, dproj, dwglu, dbgl, ddsk = _s5_tail_bwd(dys, yraw, proj, dsk, wglu, bgl, dproj)
        dproj, dbb, dcc, da = _s5_scan_bwd(dyraw, states, proj, du_skip, wb4, wc4, ar4[l], ai4[l], dproj)
        do, cterm, dproj = _attn_mix_bwd(dya, o, lse, proj, dproj)
        dproj = _attn_core_bwd(proj, do, lse, cterm, dproj)
        dh, dwin = _proj_bwd(dproj, ht, w4)
        parts = [dwin, dwglu.reshape(NCHIP, 128, SSM), dwbs, dwba, dwout.reshape(NCHIP, D // NCHIP, D)]
        exchange_start(l, parts)
        dx, dg1 = _rms_bwd(dh, xs[l], g1, dx, ici[l][4])
        if l + 1 < DEPTH:
            handoff_start(l + 1, [ici[l][4]])
        small["g1"][l], small["g2"][l], small["dsk"][l], small["bgl"][l] = dg1, dg2, ddsk, dbgl
        small["da"][l], small["dbb"][l], small["dcc"][l] = da, dbb, dcc

    dlre, dlim, dldt, dbre_c, dbim_c = _s5_prep_bwd(
        lre_c, lim_c, ldt_c, bre_c, bim_c, jnp.stack(small["da"]).reshape(nb, 1, SW),
        jnp.stack(small["dbb"]).reshape(nb, 16, SW))
    dcc = jnp.stack(small["dcc"]).reshape(nb, 16, SW)
    b_back = lambda t: t.reshape(DEPTH, NCB, 16, 8, 64).transpose(0, 1, 3, 4, 2).reshape(DEPTH, 32, 64, 16)
    c_back = lambda t: t.reshape(DEPTH, NCB, 16, 8, 64).transpose(0, 1, 3, 2, 4).reshape(DEPTH, 32, 16, 64)
    small_w = [pre_norm_g, lambda_re, lambda_im, log_dt, b_re, b_im, c_re, c_im, d_skip, b_glu, post_norm_g]
    small_m = [m_pre_norm_g, m_lambda_re, m_lambda_im, m_log_dt, m_b_re, m_b_im, m_c_re, m_c_im, m_d_skip,
               m_b_glu, m_post_norm_g]
    small_v = [v_pre_norm_g, v_lambda_re, v_lambda_im, v_log_dt, v_b_re, v_b_im, v_c_re, v_c_im, v_d_skip,
               v_b_glu, v_post_norm_g]
    small_g = [jnp.stack(small["g1"]).reshape(DEPTH, D), dlre.reshape(DEPTH, 32, 64), dlim.reshape(DEPTH, 32, 64),
               dldt.reshape(DEPTH, 32, 64).sum(-1), b_back(dbre_c), b_back(dbim_c), c_back(dcc[:, :, :HW]),
               c_back(dcc[:, :, HW:]),
               jnp.stack(small["dsk"]).reshape(DEPTH, SSM), jnp.stack(small["bgl"]).reshape(DEPTH, SSM),
               jnp.stack(small["g2"]).reshape(DEPTH, D)]
    part = _flatten_small(small_g).astype(BF16)
    send, recv, srcs, lands, token = _copies_start("small_start", _everyone_plan, 7, [part],
                                                   [_with_own_slot(part, device, 8)])
    update(DEPTH - 1, [token, ici[0][4]])
    for l in range(DEPTH - 2, 0, -1):
        update(l, [out[0] for out in big_out])
    handoff_start(0, [out[0] for out in big_out])
    update(0, [d2d[0][4]])
    _, (parts8,) = _copies_wait("small_wait", _everyone_plan, send, recv, srcs, lands, [out[0] for out in big_out])
    g_flat, d_flat, nm_flat, nv_flat = _adamw_small(_flatten_small(small_w), parts8, _flatten_small(small_m),
                                                    _flatten_small(small_v))
    sg = _unflatten_small(g_flat, small_w)
    sd = _unflatten_small(d_flat, small_w)
    snm = _unflatten_small(nm_flat, small_w)
    snv = _unflatten_small(nv_flat, small_w)

    def ordered(sm, which):
        bg = [big_out[k][which] for k in range(5)]
        bg[1] = bg[1].reshape(DEPTH, 128, SSM)
        return [sm[0], bg[0], sm[1], sm[2], sm[3], sm[4], sm[5], sm[6], sm[7], sm[8], bg[1], sm[9], bg[2], bg[3],
                bg[4], sm[10]]

    return (loss, dx[None], *ordered(sg, 0), *ordered(sd, 1), *ordered(snm, 2), *ordered(snv, 3))
```

```python
import math

import jax
import jax.numpy as jnp
from jax import lax
from jax.experimental import pallas as pl
from jax.experimental.pallas import tpu as pltpu

F32 = jnp.float32
BF16 = jnp.bfloat16
MESH = pl.DeviceIdType.MESH

DEPTH = 4
L = 2048
D = 1024
NCOL = 8192
SSM = 512
AW = 512
QKV = 1536
RMS_EPS = 1e-6
NCHIP = 4

COL_U, COL_ZS, COL_Q, COL_K, COL_V, COL_ZA, COL_GS, COL_GA = 0, 4, 8, 20, 32, 44, 48, 56

NSEG = 8
SEG = L // NSEG
NCB = 4
HW = 512
SW = 2 * HW

HEAD = 128
DILATIONS = (1, 4, 16)
SCALE = HEAD ** -0.5
NEG = -1e30

ADAM_LR, ADAM_B1, ADAM_B2, ADAM_EPS, ADAM_WD, ADAM_STEP = 0.001, 0.9, 0.999, 1e-08, 0.01, 10


def _sds(shape, dtype=F32):
    return jax.ShapeDtypeStruct(shape, dtype)


def _pcall(body, *, name, out_shape, grid=None, in_specs=None, out_specs=None, scratch=(), vmem_mb=None,
           aliases=None):
    params = {}
    if vmem_mb is not None:
        params["vmem_limit_bytes"] = vmem_mb << 20
    kw = {}
    if grid is not None:
        kw["grid"] = grid
    if in_specs is not None:
        kw["in_specs"] = in_specs
    if out_specs is not None:
        kw["out_specs"] = out_specs
    return pl.pallas_call(body, name=name, out_shape=out_shape, scratch_shapes=list(scratch),
                          compiler_params=pltpu.CompilerParams(**params),
                          input_output_aliases=aliases or {}, **kw)


def _dot(a, b):
    return jnp.dot(a, b, preferred_element_type=F32)


def _dot_nt(a, b):
    return lax.dot_general(a, b, (((1,), (1,)), ((), ())), preferred_element_type=F32)


def _dot_tn(a, b):
    return lax.dot_general(a, b, (((0,), (0,)), ((), ())), preferred_element_type=F32)


def _sigmoid(x):
    return jax.nn.sigmoid(x)


_GELU_K = math.sqrt(2.0 / math.pi)


def _gelu(x):
    return 0.5 * x * (1.0 + jnp.tanh(_GELU_K * (x + 0.044715 * (x * x * x))))


def _gelu_grad(x):
    t = jnp.tanh(_GELU_K * (x + 0.044715 * (x * x * x)))
    return 0.5 * (1.0 + t) + 0.5 * x * (1.0 - t * t) * (_GELU_K * (1.0 + 3.0 * 0.044715 * (x * x)))


def _token_operand(dep):
    if dep is None:
        return [], []
    return [dep], [pl.BlockSpec(memory_space=pl.ANY)]


def _rms_proj_fwd(x, g, w4, dep=None):
    tl, tn = 512, 1024
    ni = L // tl
    deps, dep_specs = _token_operand(dep)

    def body(x_ref, g_ref, w_ref, *rest):
        o_ref, h_ref = rest[len(deps):]
        rows = pl.ds(pl.multiple_of(pl.program_id(1) * tl, tl), tl)

        @pl.when(pl.program_id(0) == 0)
        def _():
            xx = x_ref[...]
            inv = lax.rsqrt(jnp.mean(xx * xx, axis=-1, keepdims=True) + RMS_EPS)
            h_ref[rows, :] = (xx * inv * g_ref[...]).astype(BF16)

        o_ref[...] = _dot(h_ref[rows, :], w_ref[...])

    proj, h = _pcall(
        body, name="rms_proj_fwd", grid=(NCOL // tn, ni),
        in_specs=[pl.BlockSpec((tl, D), lambda j, i: (jnp.where(j == 0, i, ni - 1), 0)),
                  pl.BlockSpec((1, D), lambda j, i: (0, 0)),
                  pl.BlockSpec((None, D, tn), lambda j, i: (lax.div(j, 2), 0, lax.rem(j, 2)))] + dep_specs,
        out_specs=[pl.BlockSpec((tl, tn), lambda j, i: (i, j)), pl.BlockSpec((L, D), lambda j, i: (0, 0))],
        out_shape=[_sds((L, NCOL)), _sds((L, D), BF16)], vmem_mb=48)(x, g, w4, *deps)

    def transpose(h_ref, ht_ref):
        ht_ref[...] = h_ref[...].astype(F32).T.astype(BF16)

    ht = _pcall(transpose, name="transpose_h", grid=(ni,), in_specs=[pl.BlockSpec((tl, D), lambda i: (i, 0))],
                out_specs=pl.BlockSpec((D, tl), lambda i: (0, i)), out_shape=_sds((D, L), BF16), vmem_mb=40)(h)
    return proj, ht


def _proj_bwd(dproj, ht, w4):
    tn = 1024

    def body(dp_ref, ht_ref, w_ref, dh_ref, dw_ref):
        @pl.when(pl.program_id(0) == 0)
        def _():
            dh_ref[...] = jnp.zeros((L, D), F32)

        dw_ref[...] = _dot(ht_ref[...], dp_ref[...]).astype(BF16)
        dh_ref[...] += _dot_nt(dp_ref[...], w_ref[...])

    return _pcall(
        body, name="proj_bwd", grid=(NCOL // tn,),
        in_specs=[pl.BlockSpec((L, tn), lambda j: (0, j)),
                  pl.BlockSpec((D, L), lambda j: (0, 0)),
                  pl.BlockSpec((None, D, tn), lambda j: (lax.div(j, 2), 0, lax.rem(j, 2)))],
        out_specs=[pl.BlockSpec((L, D), lambda j: (0, 0)),
                   pl.BlockSpec((None, D, tn), lambda j: (lax.div(j, 2), 0, lax.rem(j, 2)))],
        out_shape=[_sds((L, D)), _sds((NCHIP, D, NCOL // NCHIP), BF16)], vmem_mb=56)(dproj, ht, w4)


def _rms_bwd(dh, x, g, dxn, dep=None):
    tl = 512
    deps, dep_specs = _token_operand(dep)

    def body(dh_ref, x_ref, g_ref, dxn_ref, *rest):
        dx_ref, dg_ref = rest[len(deps):]
        i = pl.program_id(0)
        xx = x_ref[...]
        inv = lax.rsqrt(jnp.mean(xx * xx, axis=-1, keepdims=True) + RMS_EPS)
        nrm = xx * inv
        dh_v = dh_ref[...]
        dn = dh_v * g_ref[...]
        dx_ref[...] = dxn_ref[...] + inv * (dn - nrm * jnp.mean(dn * nrm, axis=-1, keepdims=True))
        part = jnp.sum(dh_v * nrm, axis=0, keepdims=True)

        @pl.when(i == 0)
        def _():
            dg_ref[...] = part

        @pl.when(i > 0)
        def _():
            dg_ref[...] += part

    row = pl.BlockSpec((tl, D), lambda i: (i, 0))
    vec = pl.BlockSpec((1, D), lambda i: (0, 0))
    return _pcall(body, name="rms_bwd", grid=(L // tl,), in_specs=[row, row, vec, row] + dep_specs,
                  out_specs=[row, vec], out_shape=[_sds((L, D)), _sds((1, D))], vmem_mb=40)(dh, x, g, dxn, *deps)


def _s5_param_math(lre, lim, ldt, bre, bim):
    lr = jnp.minimum(lre, -1e-4)
    dt = jnp.exp(ldt)
    mag = jnp.exp(lr * dt)
    ar = mag * jnp.cos(lim * dt)
    ai = mag * jnp.sin(lim * dt)
    den = lr * lr + lim * lim
    nr = ar - 1.0
    qr = (nr * lr + ai * lim) / den
    qi = (ai * lr - nr * lim) / den
    return ar, ai, qr * bre - qi * bim, qr * bim + qi * bre


def _group_of_lane(width):
    lane = lax.broadcasted_iota(jnp.int32, (16, width), 1)
    return lax.shift_right_logical(jnp.bitwise_and(lane, HW - 1), 6)


def _vec_spec():
    return pl.BlockSpec((None, 1, HW), lambda n: (n, 0, 0))


def _mat_spec():
    return pl.BlockSpec((None, 16, HW), lambda n: (n, 0, 0))


def _s5_prep(lre, lim, ldt, bre, bim, cre, cim):
    n = lre.shape[0]

    def body(lre_ref, lim_ref, ldt_ref, bre_ref, bim_ref, cre_ref, cim_ref, ar_ref, ai_ref, wb_ref, wct_ref):
        ar, ai, bbr, bbi = _s5_param_math(lre_ref[...], lim_ref[...], ldt_ref[...], bre_ref[...], bim_ref[...])
        ar_ref[...] = ar
        ai_ref[...] = ai
        group = _group_of_lane(SW)
        bb = jnp.concatenate([bbr, bbi], axis=1)
        cc = jnp.concatenate([cre_ref[...], -cim_ref[...]], axis=1)
        for g in range(8):
            rows = pl.ds(16 * g, 16)
            wb_ref[rows, :] = jnp.where(group == g, bb, 0.0).astype(BF16)
            wct_ref[rows, :] = jnp.where(group == g, cc, 0.0).astype(BF16)

    wide = pl.BlockSpec((None, 128, SW), lambda n: (n, 0, 0))
    return _pcall(body, name="s5_prep", grid=(n,),
                  in_specs=[_vec_spec()] * 3 + [_mat_spec()] * 4,
                  out_specs=[_vec_spec(), _vec_spec(), wide, wide],
                  out_shape=[_sds((n, 1, HW)), _sds((n, 1, HW)), _sds((n, 128, SW), BF16), _sds((n, 128, SW), BF16)])(
                      lre, lim, ldt, bre, bim, cre, cim)


def _s5_prep_bwd(lre, lim, ldt, bre, bim, ga, gbb):
    n = lre.shape[0]

    def body(lre_ref, lim_ref, ldt_ref, bre_ref, bim_ref, ga_ref, gbb_ref,
             dlre_ref, dlim_ref, dldt_ref, dbre_ref, dbim_ref):
        _, vjp = jax.vjp(_s5_param_math, lre_ref[...], lim_ref[...], ldt_ref[...], bre_ref[...], bim_ref[...])
        ga, gbb = ga_ref[...], gbb_ref[...]
        dlre, dlim, dldt, dbre, dbim = vjp((ga[:, :HW], ga[:, HW:], gbb[:, :HW], gbb[:, HW:]))
        dlre_ref[...] = dlre
        dlim_ref[...] = dlim
        dldt_ref[...] = dldt
        dbre_ref[...] = dbre
        dbim_ref[...] = dbim

    return _pcall(body, name="s5_prep_bwd", grid=(n,),
                  in_specs=[_vec_spec()] * 3 + [_mat_spec()] * 2 +
                  [pl.BlockSpec((None, 1, SW), lambda n: (n, 0, 0)), pl.BlockSpec((None, 16, SW), lambda n: (n, 0, 0))],
                  out_specs=[_vec_spec()] * 3 + [_mat_spec()] * 2,
                  out_shape=[_sds((n, 1, HW))] * 3 + [_sds((n, 16, HW))] * 2)(lre, lim, ldt, bre, bim, ga, gbb)


def _diag_blocks(full):
    group = _group_of_lane(SW)
    out = jnp.where(group == 0, full[0:16, :], 0.0)
    for g in range(1, 8):
        out = out + jnp.where(group == g, full[16 * g:16 * g + 16, :], 0.0)
    return out


def _permute_rows(dst_ref, src_ref):
    for s in range(NSEG):
        dst_ref[pl.ds(s, SEG, stride=NSEG), :] = src_ref[pl.ds(s * SEG, SEG), :]


def _unpermute_rows(src_ref, s):
    return src_ref[pl.ds(s, SEG, stride=NSEG), :]


def _row_of(slab, rows, j):
    return jnp.sum(jnp.where(rows == j, slab, 0.0), axis=0, keepdims=True)


def _s5_scan_fwd(proj, wb4, wc4, ar4, ai4):
    def body(u_ref, wb_ref, wc_ref, ar_ref, ai_ref, x_ref, y_ref, up_ref, d_ref, yp_ref):
        _permute_rows(up_ref, u_ref)
        d_ref[...] = _dot(up_ref[...].astype(BF16), wb_ref[...])
        ar = jnp.broadcast_to(ar_ref[...], (NSEG, HW))
        ai = jnp.broadcast_to(ai_ref[...], (NSEG, HW))
        zero = jnp.zeros((NSEG, HW), F32)

        def drive(t):
            row = pl.multiple_of(t * NSEG, NSEG)
            dd = d_ref[pl.ds(row, NSEG), :]
            return row, dd[:, :HW], dd[:, HW:]

        def local_step(t, c):
            xr, xi = c
            _, dr, di = drive(t)
            return ar * xr - ai * xi + dr, ar * xi + ai * xr + di

        fr, fi = lax.fori_loop(0, SEG, local_step, (zero, zero), unroll=8)
        pr, pi_ = ar_ref[...], ai_ref[...]
        for _ in range(int(math.log2(SEG))):
            pr, pi_ = pr * pr - pi_ * pi_, 2.0 * pr * pi_
        rows = lax.broadcasted_iota(jnp.int32, (NSEG, HW), 0)
        cr, ci = zero, zero
        for j in range(NSEG - 1):
            sr, si = _row_of(cr, rows, j), _row_of(ci, rows, j)
            gr, gi = _row_of(fr, rows, j), _row_of(fi, rows, j)
            nr = pr * sr - pi_ * si + gr
            ni = pr * si + pi_ * sr + gi
            cr = jnp.where(rows == j + 1, nr, cr)
            ci = jnp.where(rows == j + 1, ni, ci)

        def true_step(t, c):
            xr, xi = c
            row, dr, di = drive(t)
            nr = ar * xr - ai * xi + dr
            ni = ar * xi + ai * xr + di
            x_ref[pl.ds(row, NSEG), :] = jnp.concatenate([nr, ni], axis=1)
            return nr, ni

        lax.fori_loop(0, SEG, true_step, (cr, ci), unroll=8)
        yp_ref[...] = _dot_nt(x_ref[...].astype(BF16), wc_ref[...])
        for s in range(NSEG):
            y_ref[pl.ds(s * SEG, SEG), :] = _unpermute_rows(yp_ref, s)

    return _pcall(
        body, name="s5_scan_fwd", grid=(NCB,),
        in_specs=[pl.BlockSpec((L, 128), lambda b: (0, COL_U + b)),
                  pl.BlockSpec((None, 128, SW), lambda b: (b, 0, 0)),
                  pl.BlockSpec((None, 128, SW), lambda b: (b, 0, 0)),
                  pl.BlockSpec((None, 1, HW), lambda b: (b, 0, 0)),
                  pl.BlockSpec((None, 1, HW), lambda b: (b, 0, 0))],
        out_specs=[pl.BlockSpec((None, L, SW), lambda b: (b, 0, 0)),
                   pl.BlockSpec((L, 128), lambda b: (0, b))],
        out_shape=[_sds((NCB, L, SW)), _sds((L, SSM))],
        scratch=[pltpu.VMEM((L, 128), F32), pltpu.VMEM((L, SW), F32), pltpu.VMEM((L, 128), F32)],
        vmem_mb=56)(proj, wb4, wc4, ar4, ai4)


def _s5_scan_bwd(dy, xs, proj, du_skip, wb4, wc4, ar4, ai4, dproj):
    def body(dy_ref, x_ref, u_ref, dus_ref, wb_ref, wc_ref, ar_ref, ai_ref, _,
             du_ref, dbb_ref, dcc_ref, da_ref, dyp_ref, up_ref, g_ref, dup_ref):
        _permute_rows(dyp_ref, dy_ref)
        _permute_rows(up_ref, u_ref)
        dyp = dyp_ref[...].astype(BF16)
        g_ref[...] = _dot(dyp, wc_ref[...])
        dcc = _diag_blocks(_dot_tn(dyp, x_ref[...].astype(BF16)))
        dcc_ref[...] = jnp.concatenate([dcc[:, :HW], -dcc[:, HW:]], axis=1)
        ar = jnp.broadcast_to(ar_ref[...], (NSEG, HW))
        ai = jnp.broadcast_to(ai_ref[...], (NSEG, HW))
        zero = jnp.zeros((NSEG, HW), F32)

        def load(t):
            row = pl.multiple_of(t * NSEG, NSEG)
            gg = g_ref[pl.ds(row, NSEG), :]
            return row, gg[:, :HW], gg[:, HW:]

        def local_step(k, c):
            lr, li = c
            _, gr, gi = load(SEG - 1 - k)
            return ar * lr + ai * li + gr, ar * li - ai * lr + gi

        fr, fi = lax.fori_loop(0, SEG, local_step, (zero, zero), unroll=8)
        pr, pi_ = ar_ref[...], -ai_ref[...]
        for _ in range(int(math.log2(SEG))):
            pr, pi_ = pr * pr - pi_ * pi_, 2.0 * pr * pi_
        rows = lax.broadcasted_iota(jnp.int32, (NSEG, HW), 0)
        cr, ci = zero, zero
        for j in range(NSEG - 1, 0, -1):
            sr, si = _row_of(cr, rows, j), _row_of(ci, rows, j)
            gr, gi = _row_of(fr, rows, j), _row_of(fi, rows, j)
            nr = pr * sr - pi_ * si + gr
            ni = pr * si + pi_ * sr + gi
            cr = jnp.where(rows == j - 1, nr, cr)
            ci = jnp.where(rows == j - 1, ni, ci)

        def true_step(k, c):
            lr, li, gar, gai = c
            t = SEG - 1 - k
            row, gr, gi = load(t)
            nr = ar * lr + ai * li + gr
            ni = ar * li - ai * lr + gi
            g_ref[pl.ds(row, NSEG), :] = jnp.concatenate([nr, ni], axis=1)
            prow = pl.multiple_of(jnp.maximum(t - 1, 0) * NSEG, NSEG)
            xp = x_ref[pl.ds(prow, NSEG), :]
            live = (t > 0).astype(F32)
            xr, xi = xp[:, :HW] * live, xp[:, HW:] * live
            return nr, ni, gar + (nr * xr + ni * xi), gai + (ni * xr - nr * xi)

        _, _, gar, gai = lax.fori_loop(0, SEG, true_step, (cr, ci, zero, zero), unroll=4)
        l0 = g_ref[1:NSEG, :]
        xl = x_ref[(SEG - 1) * NSEG:(SEG - 1) * NSEG + NSEG - 1, :]
        l0r, l0i, xlr, xli = l0[:, :HW], l0[:, HW:], xl[:, :HW], xl[:, HW:]
        gar_t = jnp.sum(gar, axis=0, keepdims=True) + jnp.sum(l0r * xlr + l0i * xli, axis=0, keepdims=True)
        gai_t = jnp.sum(gai, axis=0, keepdims=True) + jnp.sum(l0i * xlr - l0r * xli, axis=0, keepdims=True)
        da_ref[...] = jnp.concatenate([gar_t, gai_t], axis=1)
        lam = g_ref[...].astype(BF16)
        dbb_ref[...] = _diag_blocks(_dot_tn(up_ref[...].astype(BF16), lam))
        dup_ref[...] = _dot_nt(lam, wb_ref[...])
        for s in range(NSEG):
            sl = pl.ds(s * SEG, SEG)
            du_ref[sl, :] = (_unpermute_rows(dup_ref, s) + dus_ref[sl, :]).astype(BF16)

    col = lambda off: pl.BlockSpec((L, 128), lambda b: (0, off + b))
    return _pcall(
        body, name="s5_scan_bwd", grid=(NCB,),
        in_specs=[col(0), pl.BlockSpec((None, L, SW), lambda b: (b, 0, 0)), col(COL_U), col(0),
                  pl.BlockSpec((None, 128, SW), lambda b: (b, 0, 0)),
                  pl.BlockSpec((None, 128, SW), lambda b: (b, 0, 0)),
                  pl.BlockSpec((None, 1, HW), lambda b: (b, 0, 0)),
                  pl.BlockSpec((None, 1, HW), lambda b: (b, 0, 0)),
                  pl.BlockSpec(memory_space=pl.ANY)],
        out_specs=[col(COL_U), pl.BlockSpec((None, 16, SW), lambda b: (b, 0, 0)),
                   pl.BlockSpec((None, 16, SW), lambda b: (b, 0, 0)),
                   pl.BlockSpec((None, 1, SW), lambda b: (b, 0, 0))],
        out_shape=[_sds((L, NCOL), BF16), _sds((NCB, 16, SW)), _sds((NCB, 16, SW)), _sds((NCB, 1, SW))],
        scratch=[pltpu.VMEM((L, 128), F32), pltpu.VMEM((L, 128), F32), pltpu.VMEM((L, SW), F32),
                 pltpu.VMEM((L, 128), F32)],
        aliases={8: 0}, vmem_mb=56)(dy, xs, proj, du_skip, wb4, wc4, ar4, ai4, dproj)


def _s5_tail_fwd(yraw, proj, dsk, wglu, bglu):
    tl = 512

    def body(y_ref, u_ref, z_ref, dsk_ref, w_ref, b_ref, o_ref):
        y1 = y_ref[...] + dsk_ref[...] * u_ref[...]
        y2 = _gelu(y1)
        gl = _dot(y2.astype(BF16), w_ref[...]) + b_ref[...]
        z = z_ref[...]
        o_ref[...] = (y2 * _sigmoid(gl)) * (z * _sigmoid(z))

    blk = lambda c: pl.BlockSpec((tl, SSM), lambda i: (i, c))
    vec = pl.BlockSpec((1, SSM), lambda i: (0, 0))
    return _pcall(body, name="s5_tail_fwd", grid=(L // tl,),
                  in_specs=[blk(0), blk(0), blk(1), vec, pl.BlockSpec((SSM, SSM), lambda i: (0, 0)), vec],
                  out_specs=blk(0), out_shape=_sds((L, SSM)), vmem_mb=40)(yraw, proj, proj, dsk, wglu, bglu)


def _s5_tail_bwd(dys, yraw, proj, dsk, wglu, bglu, dproj):
    tl = 512
    ni = L // tl

    def body(dys_ref, y_ref, u_ref, z_ref, dsk_ref, w_ref, b_ref, _,
             dy_ref, dus_ref, dz_ref, dw_ref, db_ref, dd_ref, acc_ref):
        i = pl.program_id(0)
        u = u_ref[...]
        y1 = y_ref[...] + dsk_ref[...] * u
        y2 = _gelu(y1)
        y2b = y2.astype(BF16)
        sg = _sigmoid(_dot(y2b, w_ref[...]) + b_ref[...])
        y3 = y2 * sg
        z = z_ref[...]
        sz = _sigmoid(z)
        dys = dys_ref[...]
        dy3 = dys * (z * sz)
        dz_ref[...] = (dys * y3 * (sz * (1.0 + z * (1.0 - sz)))).astype(BF16)
        dgl = (dy3 * y2) * (sg * (1.0 - sg))
        dglb = dgl.astype(BF16)
        dy2 = dy3 * sg + _dot_nt(dglb, w_ref[...])
        dy1 = dy2 * _gelu_grad(y1)
        dy_ref[...] = dy1
        dus_ref[...] = dsk_ref[...] * dy1
        dw = _dot_tn(y2b, dglb)
        db = jnp.sum(dgl, axis=0, keepdims=True)
        dd = jnp.sum(dy1 * u, axis=0, keepdims=True)

        @pl.when(i == 0)
        def _():
            acc_ref[...] = dw
            db_ref[...] = db
            dd_ref[...] = dd

        @pl.when(i > 0)
        def _():
            acc_ref[...] += dw
            db_ref[...] += db
            dd_ref[...] += dd

        @pl.when(i == ni - 1)
        def _():
            dw_ref[...] = acc_ref[...].astype(BF16)

    blk = lambda c: pl.BlockSpec((tl, SSM), lambda i: (i, c))
    vec = pl.BlockSpec((1, SSM), lambda i: (0, 0))
    mat = pl.BlockSpec((SSM, SSM), lambda i: (0, 0))
    return _pcall(body, name="s5_tail_bwd", grid=(ni,),
                  in_specs=[blk(0), blk(0), blk(0), blk(1), vec, mat, vec, pl.BlockSpec(memory_space=pl.ANY)],
                  out_specs=[blk(0), blk(0), blk(COL_ZS // 4), mat, vec, vec],
                  out_shape=[_sds((L, SSM)), _sds((L, SSM)), _sds((L, NCOL), BF16), _sds((SSM, SSM), BF16),
                             _sds((1, SSM)), _sds((1, SSM))],
                  scratch=[pltpu.VMEM((SSM, SSM), F32)], aliases={7: 2},
                  vmem_mb=40)(dys, yraw, proj, proj, dsk, wglu, bglu, dproj)


def _attn_blocks(dil):
    n = L // dil
    return [(i * HEAD * dil + r, r * n + i * HEAD, i == 0) for r in range(dil) for i in range(n // HEAD)]


def _rows(start, dil):
    return pl.ds(start, HEAD) if dil == 1 else pl.ds(start, HEAD, stride=dil)


def _residue_rows(r, dil):
    n = L // dil
    return pl.ds(0, L) if dil == 1 else pl.ds(r, n, stride=dil)


def _gather_residues(pairs, dil, dtype=BF16):
    n = L // dil
    for src, dst in pairs:
        for r in range(dil):
            dst[pl.ds(r * n, n), :] = src[_residue_rows(r, dil), :].astype(dtype)


def _band_masks():
    ri = lax.broadcasted_iota(jnp.int32, (HEAD, HEAD), 0)
    ci = lax.broadcasted_iota(jnp.int32, (HEAD, HEAD), 1)
    ri2 = lax.broadcasted_iota(jnp.int32, (HEAD, 2 * HEAD), 0)
    ci2 = lax.broadcasted_iota(jnp.int32, (HEAD, 2 * HEAD), 1)
    return ci <= ri, jnp.logical_and(ci2 >= ri2, ci2 - HEAD <= ri2)


def _qkv_specs(index):
    def spec(off):
        return pl.BlockSpec((L, HEAD), lambda gi, h: (0, off + index(gi, h)))
    return [spec(off) for off in (COL_Q, COL_K, COL_V)]


def _attn_core_fwd(proj):
    def body(q_ref, k_ref, v_ref, o_ref, lse_ref, qp_ref, kp_ref, vp_ref):
        gi = pl.program_id(0)
        m_cur, m_both = _band_masks()
        for g, dil in enumerate(DILATIONS):
            @pl.when(gi == g)
            def _(dil=dil):
                _gather_residues(((q_ref, qp_ref), (k_ref, kp_ref), (v_ref, vp_ref)), dil)
                for st, base, first in _attn_blocks(dil):
                    q = qp_ref[pl.ds(base, HEAD), :]
                    keys = pl.ds(base, HEAD) if first else pl.ds(base - HEAD, 2 * HEAD)
                    s = jnp.where(m_cur if first else m_both, _dot_nt(q, kp_ref[keys, :]) * SCALE, NEG)
                    mx = jnp.max(s, axis=-1, keepdims=True)
                    p = jnp.exp(s - mx)
                    den = jnp.sum(p, axis=-1, keepdims=True)
                    r = _rows(st, dil)
                    o_ref[r, :] = _dot(p.astype(BF16), vp_ref[keys, :]) / den
                    lse_ref[r, :] = jnp.broadcast_to(mx + jnp.log(den), (HEAD, HEAD))

    idx = lambda gi, h: gi * 4 + h
    out = pl.BlockSpec((L, HEAD), lambda gi, h: (0, gi * 4 + h))
    return _pcall(body, name="attn_core_fwd", grid=(3, 4), in_specs=_qkv_specs(idx), out_specs=[out, out],
                  out_shape=[_sds((L, QKV)), _sds((L, QKV))], scratch=[pltpu.VMEM((L, HEAD), BF16)] * 3,
                  vmem_mb=40)(proj, proj, proj)


def _attn_mix_fwd(o, lse, proj):
    tl = 512

    def body(o_ref, l_ref, z_ref, y_ref):
        for h in range(4):
            c = [pl.ds((g * 4 + h) * HEAD, HEAD) for g in range(3)]
            ls = [l_ref[:, c[g]] for g in range(3)]
            m = jnp.maximum(jnp.maximum(ls[0], ls[1]), ls[2])
            e = [jnp.exp(ls[g] - m) for g in range(3)]
            y = (e[0] * o_ref[:, c[0]] + e[1] * o_ref[:, c[1]] + e[2] * o_ref[:, c[2]]) / (e[0] + e[1] + e[2])
            z = z_ref[:, pl.ds(h * HEAD, HEAD)]
            y_ref[:, pl.ds(h * HEAD, HEAD)] = y * (z * _sigmoid(z))

    wide = pl.BlockSpec((tl, QKV), lambda i: (i, 0))
    return _pcall(body, name="attn_mix_fwd", grid=(L // tl,),
                  in_specs=[wide, wide, pl.BlockSpec((tl, AW), lambda i: (i, COL_ZA // 4))],
                  out_specs=pl.BlockSpec((tl, AW), lambda i: (i, 0)), out_shape=_sds((L, AW)),
                  vmem_mb=40)(o, lse, proj)


def _attn_mix_bwd(dya, o, lse, proj, dproj):
    tl = 512

    def body(dya_ref, o_ref, l_ref, z_ref, _, do_ref, c_ref, dz_ref):
        for h in range(4):
            c = [pl.ds((g * 4 + h) * HEAD, HEAD) for g in range(3)]
            hs = pl.ds(h * HEAD, HEAD)
            ls = [l_ref[:, c[g]] for g in range(3)]
            m = jnp.maximum(jnp.maximum(ls[0], ls[1]), ls[2])
            e = [jnp.exp(ls[g] - m) for g in range(3)]
            den = e[0] + e[1] + e[2]
            al = [e[g] / den for g in range(3)]
            y = al[0] * o_ref[:, c[0]] + al[1] * o_ref[:, c[1]] + al[2] * o_ref[:, c[2]]
            z = z_ref[:, hs]
            sz = _sigmoid(z)
            dya = dya_ref[:, hs]
            dz_ref[:, hs] = (dya * y * (sz * (1.0 + z * (1.0 - sz)))).astype(BF16)
            dy = dya * (z * sz)
            tot = jnp.sum(dy * y, axis=-1, keepdims=True)
            for g in range(3):
                do_ref[:, c[g]] = al[g] * dy
                c_ref[:, c[g]] = -(al[g] * tot)

    wide = pl.BlockSpec((tl, QKV), lambda i: (i, 0))
    nar = pl.BlockSpec((tl, AW), lambda i: (i, 0))
    za = pl.BlockSpec((tl, AW), lambda i: (i, COL_ZA // 4))
    return _pcall(body, name="attn_mix_bwd", grid=(L // tl,),
                  in_specs=[nar, wide, wide, za, pl.BlockSpec(memory_space=pl.ANY)],
                  out_specs=[wide, wide, za], out_shape=[_sds((L, QKV)), _sds((L, QKV)), _sds((L, NCOL), BF16)],
                  aliases={4: 2}, vmem_mb=48)(dya, o, lse, proj, dproj)


def _attn_core_bwd(proj, do, lse, cc, dproj):
    def body(q_ref, k_ref, v_ref, do_ref, lse_ref, c_ref, _, dp_ref,
             qp_ref, kp_ref, vp_ref, dop_ref, lsep_ref, cp_ref, dqp_ref, dkp_ref, dvp_ref, tok_ref, out_ref, sems):
        gi = pl.program_id(0)
        head = gi * 4 + pl.program_id(1)
        m_cur, m_both = _band_masks()
        dkp_ref[...] = jnp.zeros((L, HEAD), F32)
        dvp_ref[...] = jnp.zeros((L, HEAD), F32)
        copies = []
        for g, dil in enumerate(DILATIONS):
            @pl.when(gi == g)
            def _(dil=dil):
                _gather_residues(((q_ref, qp_ref), (k_ref, kp_ref), (v_ref, vp_ref), (do_ref, dop_ref)), dil)
                _gather_residues(((lse_ref, lsep_ref), (c_ref, cp_ref)), dil, F32)
                for _, base, first in _attn_blocks(dil):
                    mine = pl.ds(base, HEAD)
                    keys = mine if first else pl.ds(base - HEAD, 2 * HEAD)
                    q, do_b, kb = qp_ref[mine, :], dop_ref[mine, :], kp_ref[keys, :]
                    lse_b, c_b = lsep_ref[mine, :], cp_ref[mine, :]
                    if not first:
                        lse_b = jnp.concatenate([lse_b, lse_b], axis=1)
                        c_b = jnp.concatenate([c_b, c_b], axis=1)
                    s = _dot_nt(q, kb) * SCALE
                    p = jnp.where(m_cur if first else m_both, jnp.exp(s - lse_b), 0.0)
                    ds = (p * (_dot_nt(do_b, vp_ref[keys, :]) + c_b) * SCALE).astype(BF16)
                    dqp_ref[mine, :] = _dot(ds, kb)
                    dkp_ref[keys, :] += _dot_tn(ds, q)
                    dvp_ref[keys, :] += _dot_tn(p.astype(BF16), do_b)
                n = L // dil
                for slot, src in enumerate((dqp_ref, dkp_ref, dvp_ref)):
                    for r in range(dil):
                        tok_ref[_residue_rows(r, dil), :] = src[pl.ds(r * n, n), :]
                    out_ref[slot] = tok_ref[...].astype(BF16)

        for slot, off in enumerate((COL_Q, COL_K, COL_V)):
            cols = pl.ds(pl.multiple_of((off + head) * HEAD, HEAD), HEAD)
            copies.append(pltpu.make_async_copy(out_ref.at[slot], dp_ref.at[:, cols], sems.at[slot]))
            copies[-1].start()
        for cp in copies:
            cp.wait()

    idx = lambda gi, h: gi * 4 + h
    blk = pl.BlockSpec((L, HEAD), lambda gi, h: (0, gi * 4 + h))
    hbm = pl.BlockSpec(memory_space=pl.ANY)
    return _pcall(body, name="attn_core_bwd", grid=(3, 4), in_specs=_qkv_specs(idx) + [blk, blk, blk, hbm],
                  out_specs=hbm, out_shape=_sds((L, NCOL), BF16),
                  scratch=[pltpu.VMEM((L, HEAD), BF16)] * 4 + [pltpu.VMEM((L, HEAD), F32)] * 6 +
                  [pltpu.VMEM((3, L, HEAD), BF16), pltpu.SemaphoreType.DMA((3,))],
                  aliases={6: 0}, vmem_mb=48)(proj, proj, proj, do, lse, cc, dproj)


def _merge_fwd_math(ys_b, ya_b, gs, ga, wbs_ref, wba_ref, wout_ref):
    bs = jnp.concatenate([_dot(ys_b, wbs_ref[s]) for s in range(NCHIP)], axis=1)
    ba = jnp.concatenate([_dot(ya_b, wba_ref[s]) for s in range(NCHIP)], axis=1)
    sgs, sga = _sigmoid(gs), _sigmoid(ga)
    merged = sgs * bs + sga * ba
    out = _dot(merged.astype(BF16), wout_ref[...])
    inv = lax.rsqrt(jnp.mean(out * out, axis=-1, keepdims=True) + RMS_EPS)
    return bs, ba, sgs, sga, merged, out, inv


def _merge_specs(tl):
    row = lambda w, c: pl.BlockSpec((tl, w), lambda i: (i, c))
    return [row(SSM, 0), row(AW, 0), row(D, COL_GS // 8), row(D, COL_GA // 8),
            pl.BlockSpec((NCHIP, SSM, D // NCHIP), lambda i: (0, 0, 0)),
            pl.BlockSpec((NCHIP, AW, D // NCHIP), lambda i: (0, 0, 0)),
            pl.BlockSpec((D, D), lambda i: (0, 0)),
            pl.BlockSpec((1, D), lambda i: (0, 0))]


def _merge_out_fwd(ys, ya, proj, wbs4, wba4, wout, g2, x, dep=None):
    tl = 512
    deps, dep_specs = _token_operand(dep)

    def body(ys_ref, ya_ref, gs_ref, ga_ref, wbs_ref, wba_ref, wout_ref, g2_ref, x_ref, *rest):
        o_ref = rest[-1]
        *_, out, inv = _merge_fwd_math(ys_ref[...].astype(BF16), ya_ref[...].astype(BF16), gs_ref[...],
                                       ga_ref[...], wbs_ref, wba_ref, wout_ref)
        o_ref[...] = x_ref[...] + out * inv * g2_ref[...]

    row = pl.BlockSpec((tl, D), lambda i: (i, 0))
    return _pcall(body, name="merge_out_fwd", grid=(L // tl,), in_specs=_merge_specs(tl) + [row] + dep_specs,
                  out_specs=row, out_shape=_sds((L, D)),
                  vmem_mb=48)(ys, ya, proj, proj, wbs4, wba4, wout, g2, x, *deps)


def _merge_out_bwd(dxn, ys, ya, proj, wbs4, wba4, wout, g2, dep=None):
    tl = 256
    ni = L // tl
    cw = D // NCHIP
    deps, dep_specs = _token_operand(dep)

    def body(dxn_ref, ys_ref, ya_ref, gs_ref, ga_ref, wbs_ref, wba_ref, wout_ref, g2_ref, *rest):
        (dys_ref, dya_ref, dgate_ref, dwbs_ref, dwba_ref, dwout_ref, dg2_ref,
         abs_ref, aba_ref, aout_ref) = rest[len(deps):]
        i = pl.program_id(0)

        @pl.when(i == 0)
        def _():
            abs_ref[...] = jnp.zeros(abs_ref.shape, F32)
            aba_ref[...] = jnp.zeros(aba_ref.shape, F32)
            aout_ref[...] = jnp.zeros(aout_ref.shape, F32)
            dg2_ref[...] = jnp.zeros(dg2_ref.shape, F32)

        ys_b, ya_b = ys_ref[...].astype(BF16), ya_ref[...].astype(BF16)
        bs, ba, sgs, sga, merged, out, inv = _merge_fwd_math(ys_b, ya_b, gs_ref[...], ga_ref[...],
                                                             wbs_ref, wba_ref, wout_ref)
        nrm = out * inv
        dxn = dxn_ref[...]
        dg2_ref[...] += jnp.sum(dxn * nrm, axis=0, keepdims=True)
        dn = dxn * g2_ref[...]
        dout = (inv * (dn - nrm * jnp.mean(dn * nrm, axis=-1, keepdims=True))).astype(BF16)
        aout_ref[...] += _dot_tn(merged.astype(BF16), dout)
        dm = _dot_nt(dout, wout_ref[...])
        dbs, dba = dm * sgs, dm * sga
        dgate_ref[:, :D] = (dm * bs * (sgs * (1.0 - sgs))).astype(BF16)
        dgate_ref[:, D:] = (dm * ba * (sga * (1.0 - sga))).astype(BF16)
        dbs_b, dba_b = dbs.astype(BF16), dba.astype(BF16)
        dys = None
        dya = None
        for s in range(NCHIP):
            cs = slice(s * cw, (s + 1) * cw)
            p_s = _dot_nt(dbs_b[:, cs], wbs_ref[s])
            p_a = _dot_nt(dba_b[:, cs], wba_ref[s])
            dys = p_s if dys is None else dys + p_s
            dya = p_a if dya is None else dya + p_a
            abs_ref[s] += _dot_tn(ys_b, dbs_b[:, cs])
            aba_ref[s] += _dot_tn(ya_b, dba_b[:, cs])

        dys_ref[...] = dys
        dya_ref[...] = dya

        @pl.when(i == ni - 1)
        def _():
            dwbs_ref[...] = abs_ref[...].astype(BF16)
            dwba_ref[...] = aba_ref[...].astype(BF16)
            dwout_ref[...] = aout_ref[...].astype(BF16)

    row = lambda w: pl.BlockSpec((tl, w), lambda i: (i, 0))
    w4 = pl.BlockSpec((NCHIP, SSM, cw), lambda i: (0, 0, 0))
    sq = pl.BlockSpec((D, D), lambda i: (0, 0))
    vec = pl.BlockSpec((1, D), lambda i: (0, 0))
    return _pcall(body, name="merge_out_bwd", grid=(ni,), in_specs=[row(D)] + _merge_specs(tl) + dep_specs,
                  out_specs=[row(SSM), row(AW), pl.BlockSpec((tl, 2 * D), lambda i: (i, COL_GS // 16)), w4, w4, sq,
                             vec],
                  out_shape=[_sds((L, SSM)), _sds((L, AW)), _sds((L, NCOL), BF16),
                             _sds((NCHIP, SSM, cw), BF16), _sds((NCHIP, AW, cw), BF16), _sds((D, D), BF16),
                             _sds((1, D))],
                  scratch=[pltpu.VMEM((NCHIP, SSM, cw), F32), pltpu.VMEM((NCHIP, AW, cw), F32),
                           pltpu.VMEM((D, D), F32)],
                  vmem_mb=56)(dxn, ys, ya, proj, proj, wbs4, wba4, wout, g2, *deps)


def _loss_head(y, target):
    tl = 512

    def body(y_ref, t_ref, loss_ref, dy_ref):
        i = pl.program_id(0)
        err = y_ref[...] - t_ref[...]
        dy_ref[...] = err / D
        part = 0.5 * jnp.sum(jnp.mean(err * err, axis=-1, keepdims=True), axis=0, keepdims=True)
        part = jnp.broadcast_to(part, (8, 128))

        @pl.when(i == 0)
        def _():
            loss_ref[...] = part

        @pl.when(i > 0)
        def _():
            loss_ref[...] += part

    row = pl.BlockSpec((tl, D), lambda i: (i, 0))
    return _pcall(body, name="loss_head", grid=(L // tl,), in_specs=[row, row],
                  out_specs=[pl.BlockSpec((8, 128), lambda i: (0, 0)), row],
                  out_shape=[_sds((8, 128)), _sds((L, D))], vmem_mb=40)(y, target)


def _adamw_math(w, g, m, v):
    m = ADAM_B1 * m + (1.0 - ADAM_B1) * g
    v = ADAM_B2 * v + (1.0 - ADAM_B2) * (g * g)
    m_hat = m / (1.0 - ADAM_B1 ** ADAM_STEP)
    v_hat = v / (1.0 - ADAM_B2 ** ADAM_STEP)
    delta = -ADAM_LR * (m_hat / (jnp.sqrt(v_hat) + ADAM_EPS) + ADAM_WD * w)
    return delta, m, v


def _adamw_big(layer, w, m, v, own, sib, prev):
    _, r, c = w.shape
    tr = min(r, 128)

    def body(w_ref, m_ref, v_ref, own_ref, sib_ref, *rest):
        g_ref, d_ref, nm_ref, nv_ref = rest[-4:]
        a = own_ref[0].astype(F32)
        b = sib_ref[0].astype(F32)
        for s in range(1, NCHIP):
            a = a + own_ref[s].astype(F32)
            b = b + sib_ref[s].astype(F32)
        g = a + b
        delta, nm, nv = _adamw_math(w_ref[...], g, m_ref[...], v_ref[...])
        g_ref[...] = g
        d_ref[...] = delta
        nm_ref[...] = nm
        nv_ref[...] = nv

    lay = pl.BlockSpec((None, tr, c), lambda i: (layer, i, 0))
    slots = pl.BlockSpec((NCHIP, tr, c), lambda i: (0, i, 0))
    ins = [w, m, v, own, sib]
    in_specs = [lay, lay, lay, slots, slots]
    aliases = {}
    if prev is not None:
        ins += list(prev)
        in_specs += [pl.BlockSpec(memory_space=pl.ANY)] * 4
        aliases = {5 + k: k for k in range(4)}
    return _pcall(body, name="adamw_big", grid=(r // tr,), in_specs=in_specs, out_specs=[lay] * 4,
                  out_shape=[_sds(w.shape)] * 4, aliases=aliases, vmem_mb=48)(*ins)


SMALL_TILE = 512


def _adamw_small(w, parts, m, v):
    r = w.shape[0]

    def body(w_ref, p_ref, m_ref, v_ref, g_ref, d_ref, nm_ref, nv_ref):
        g = p_ref[0].astype(F32)
        for dev in range(1, 8):
            g = g + p_ref[dev].astype(F32)
        delta, nm, nv = _adamw_math(w_ref[...], g, m_ref[...], v_ref[...])
        g_ref[...] = g
        d_ref[...] = delta
        nm_ref[...] = nm
        nv_ref[...] = nv

    row = pl.BlockSpec((SMALL_TILE, 128), lambda i: (i, 0))
    return _pcall(body, name="adamw_small", grid=(r // SMALL_TILE,),
                  in_specs=[row, pl.BlockSpec((8, SMALL_TILE, 128), lambda i: (0, i, 0)), row, row],
                  out_specs=[row] * 4, out_shape=[_sds((r, 128))] * 4, vmem_mb=40)(w, parts, m, v)


def _place():
    x, y, c = lax.axis_index("x"), lax.axis_index("y"), lax.axis_index("c")
    return x, y, c, 2 * x + y


def _chip_peer(x, y, j):
    return (1 - x if j & 2 else x), (1 - y if j & 1 else y)


_HBM = pl.BlockSpec(memory_space=pltpu.HBM)
_SEM = pl.BlockSpec(memory_space=pltpu.SEMAPHORE)
_EFFECT = pltpu.SideEffectType.DATAFLOW_SIDE_EFFECTING


def _in_hbm(a):
    return pltpu.with_memory_space_constraint(a, pltpu.HBM)


def _copies_start(name, plan, ncopy, srcs, lands, dep=None):
    n = len(srcs) + len(lands)
    deps, dep_specs = _token_operand(dep)

    def body(*refs):
        send_sems, recv_sems = refs[n + len(deps)], refs[n + len(deps) + 1]
        token = refs[-1]
        for k, (src, dst, dev) in enumerate(plan(refs[:len(srcs)], refs[len(srcs):n])):
            if dev is None:
                pltpu.make_async_copy(src, dst, recv_sems.at[k]).start()
            else:
                pltpu.make_async_remote_copy(src_ref=src, dst_ref=dst, send_sem=send_sems.at[k],
                                             recv_sem=recv_sems.at[k], device_id=dev, device_id_type=MESH).start()
        token[...] = jnp.zeros_like(token)

    bufs = list(srcs) + list(lands)
    outs = pl.pallas_call(
        body, name=name,
        out_shape=(pltpu.SemaphoreType.DMA((ncopy,)), pltpu.SemaphoreType.DMA((ncopy,)),
                   *[pltpu.HBM(a.shape, a.dtype) for a in bufs], _sds((8, 128))),
        in_specs=[_HBM] * n + dep_specs,
        out_specs=(_SEM, _SEM, *[_HBM] * n, pl.BlockSpec(memory_space=pltpu.VMEM)),
        input_output_aliases={i: 2 + i for i in range(n)},
        compiler_params=pltpu.CompilerParams(has_side_effects=_EFFECT),
    )(*[_in_hbm(a) for a in bufs], *deps)
    return outs[0], outs[1], list(outs[2:2 + len(srcs)]), list(outs[2 + len(srcs):2 + n]), outs[-1]


def _copies_wait(name, plan, send_sems, recv_sems, srcs, lands, after):
    n = len(srcs) + len(lands)
    after = list(after)

    def body(*refs):
        s_sems, r_sems = refs[n], refs[n + 1]
        for k, (src, dst, dev) in enumerate(plan(refs[:len(srcs)], refs[len(srcs):n])):
            if dev is None:
                pltpu.make_async_copy(src, dst, r_sems.at[k]).wait()
                continue
            cp = pltpu.make_async_remote_copy(src_ref=src, dst_ref=dst, send_sem=s_sems.at[k],
                                              recv_sem=r_sems.at[k], device_id=dev, device_id_type=MESH)
            cp.wait_send()
            cp.wait_recv()

    bufs = list(srcs) + list(lands)
    outs = pl.pallas_call(
        body, name=name,
        out_shape=tuple(pltpu.HBM(a.shape, a.dtype) for a in bufs),
        in_specs=[_HBM] * n + [_SEM, _SEM] + [pl.BlockSpec(memory_space=pl.ANY)] * len(after),
        out_specs=tuple([_HBM] * n),
        input_output_aliases={i: i for i in range(n)},
        compiler_params=pltpu.CompilerParams(has_side_effects=_EFFECT),
    )(*bufs, send_sems, recv_sems, *after)
    return list(outs[:len(srcs)]), list(outs[len(srcs):])


def _with_own_slot(block, slot, nslots=NCHIP):
    land = lax.empty((nslots,) + block.shape, block.dtype)
    return lax.dynamic_update_slice(land, block[None], (slot,) + (0,) * block.ndim)


def _my_half(land, slot, c):
    half = land.shape[1] // 2
    return land.at[slot, pl.ds(pl.multiple_of(c * half, 16), half)]


def _gather_ici_plan(srcs, lands):
    x, y, c, s = _place()
    return [(_my_half(land, s, c), _my_half(land, s, c), (*_chip_peer(x, y, j), c))
            for land in lands for j in range(1, NCHIP)]


def _gather_d2d_plan(srcs, lands):
    x, y, c, s = _place()
    return [(_my_half(land, s ^ j, c), _my_half(land, s ^ j, c), (x, y, 1 - c))
            for land in lands for j in range(1, NCHIP)]


def _exchange_plan(srcs, lands):
    x, y, c, s = _place()
    return [(src.at[s ^ j], land.at[s], (*_chip_peer(x, y, j), c) if j else None)
            for src, land in zip(srcs, lands) for j in range(NCHIP)]


def _sibling_plan(srcs, lands):
    x, y, c, _ = _place()
    return [(src, land, (x, y, 1 - c)) for src, land in zip(srcs, lands)]


def _everyone_plan(srcs, lands):
    x, y, c, _ = _place()
    me = 4 * x + 2 * y + c
    return [(srcs[0], lands[0].at[me], (*_chip_peer(x, y, j >> 1), 1 - c if j & 1 else c)) for j in range(1, 8)]


def _flatten_small(parts):
    flat = jnp.concatenate([p.reshape(-1) for p in parts])
    n = flat.shape[0]
    rows = -(-n // (128 * SMALL_TILE)) * SMALL_TILE
    return jnp.pad(flat, (0, rows * 128 - n)).reshape(rows, 128)


def _unflatten_small(buf, like):
    flat = buf.reshape(-1)
    out, at = [], 0
    for p in like:
        out.append(flat[at:at + p.size].reshape(p.shape))
        at += p.size
    return out


def kernel(x, pre_norm_g, w_in, lambda_re, lambda_im, log_dt, b_re, b_im, c_re, c_im, d_skip, w_glu, b_glu, w_branch_s, w_branch_a, w_out, post_norm_g, loss_target, m_pre_norm_g, m_w_in, m_lambda_re, m_lambda_im, m_log_dt, m_b_re, m_b_im, m_c_re, m_c_im, m_d_skip, m_w_glu, m_b_glu, m_w_branch_s, m_w_branch_a, m_w_out, m_post_norm_g, v_pre_norm_g, v_w_in, v_lambda_re, v_lambda_im, v_log_dt, v_b_re, v_b_im, v_c_re, v_c_im, v_d_skip, v_w_glu, v_b_glu, v_w_branch_s, v_w_branch_a, v_w_out, v_post_norm_g):
    nb = DEPTH * NCB
    lre_c = lambda_re.reshape(nb, 1, HW)
    lim_c = lambda_im.reshape(nb, 1, HW)
    ldt_c = jnp.broadcast_to(log_dt[:, :, None], (DEPTH, 32, 64)).reshape(nb, 1, HW)
    b_rows = lambda t: t.reshape(DEPTH, NCB, 8, 64, 16).transpose(0, 1, 4, 2, 3).reshape(nb, 16, HW)
    c_rows = lambda t: t.reshape(DEPTH, NCB, 8, 16, 64).transpose(0, 1, 3, 2, 4).reshape(nb, 16, HW)
    bre_c, bim_c = b_rows(b_re), b_rows(b_im)
    ar, ai, wb, wct = _s5_prep(lre_c, lim_c, ldt_c, bre_c, bim_c, c_rows(c_re), c_rows(c_im))
    ar4 = ar.reshape(DEPTH, NCB, 1, HW)
    ai4 = ai.reshape(DEPTH, NCB, 1, HW)
    wb = wb.reshape(DEPTH, NCB, 128, SW)
    wct = wct.reshape(DEPTH, NCB, 128, SW)

    chip = 2 * lax.axis_index("x") + lax.axis_index("y")
    device = 2 * chip + lax.axis_index("c")

    def gather_start(l, dep):
        blocks = [w_in[l], w_glu[l], w_branch_s[l], w_branch_a[l], w_out[l]]
        if l == 0:
            blocks = lax.optimization_barrier(blocks)
        lands = [_with_own_slot(a.astype(BF16), chip) for a in blocks]
        return _copies_start(f"gather_start_{l}", _gather_ici_plan, 15, [], lands, dep)

    def gather_forward(l, flight, after):
        send, recv, _, lands, _ = flight
        _, lands = _copies_wait(f"gather_wait_{l}", _gather_ici_plan, send, recv, [], lands, after)
        return _copies_start(f"forward_start_{l}", _gather_d2d_plan, 15, [], lands)

    def gather_done(l, forward, after):
        send, recv, _, lands, _ = forward
        return _copies_wait(f"forward_wait_{l}", _gather_d2d_plan, send, recv, [], lands, after)[1]

    small_w = [pre_norm_g, lambda_re, lambda_im, log_dt, b_re, b_im, c_re, c_im, d_skip, b_glu, post_norm_g]
    small_m = [m_pre_norm_g, m_lambda_re, m_lambda_im, m_log_dt, m_b_re, m_b_im, m_c_re, m_c_im, m_d_skip,
               m_b_glu, m_post_norm_g]
    small_v = [v_pre_norm_g, v_lambda_re, v_lambda_im, v_log_dt, v_b_re, v_b_im, v_c_re, v_c_im, v_d_skip,
               v_b_glu, v_post_norm_g]
    flat_w, flat_m, flat_v = _flatten_small(small_w), _flatten_small(small_m), _flatten_small(small_v)

    xs = [x[0]]
    saved = []
    forward = gather_forward(0, gather_start(0, None), [xs[0], flat_w, flat_m, flat_v, ar, ai, wb, wct])
    for l in range(DEPTH):
        w4, wglu4, wbs4, wba4, wout4 = gather_done(l, forward, [xs[l]])
        started = None
        if l + 1 < DEPTH:
            flight = gather_start(l + 1, wglu4)
            started = flight[4]
        wglu = wglu4.reshape(SSM, SSM)
        wout = wout4.reshape(D, D)
        wb4, wc4 = wb[l], wct[l]
        g1 = pre_norm_g[l].reshape(1, D)
        g2 = post_norm_g[l].reshape(1, D)
        dsk = d_skip[l].reshape(1, SSM)
        bgl = b_glu[l].reshape(1, SSM)
        proj, ht = _rms_proj_fwd(xs[l], g1, w4, started)
        states, yraw = _s5_scan_fwd(proj, wb4, wc4, ar4[l], ai4[l])
        o, lse = _attn_core_fwd(proj)
        forwarded = None
        if l + 1 < DEPTH:
            forward = gather_forward(l + 1, flight, [yraw, o])
            forwarded = forward[4]
        ys = _s5_tail_fwd(yraw, proj, dsk, wglu, bgl)
        ya = _attn_mix_fwd(o, lse, proj)
        xs.append(_merge_out_fwd(ys, ya, proj, wbs4, wba4, wout, g2, xs[l], forwarded))
        saved.append((w4, wglu, wbs4, wba4, wout, wb4, wc4, g1, g2, dsk, bgl, proj, ht, states, yraw, ys, ya, o, lse))

    loss_part, dx = _loss_head(xs[DEPTH], loss_target[0])
    loss = lax.psum(loss_part[0, 0], ("x", "y", "c"))

    big_w = (w_in, w_glu.reshape(DEPTH, 128, SSM), w_branch_s, w_branch_a, w_out)
    big_m = (m_w_in, m_w_glu, m_w_branch_s, m_w_branch_a, m_w_out)
    big_v = (v_w_in, v_w_glu, v_w_branch_s, v_w_branch_a, v_w_out)
    big_out = [None] * 5
    small = {k: [None] * DEPTH for k in ("g1", "da", "dbb", "dcc", "dsk", "bgl", "g2")}
    ici = [None] * DEPTH
    d2d = [None] * DEPTH

    def exchange_start(l, parts):
        lands = [lax.empty(p.shape, p.dtype) for p in parts]
        ici[l] = _copies_start(f"exchange_start_{l}", _exchange_plan, 20, parts, lands)

    def handoff_start(l, after):
        send, recv, srcs, lands, _ = ici[l]
        _, own = _copies_wait(f"exchange_wait_{l}", _exchange_plan, send, recv, srcs, lands, after)
        d2d[l] = _copies_start(f"handoff_start_{l}", _sibling_plan, 5, own,
                               [lax.empty(a.shape, a.dtype) for a in own])

    def update(l, after):
        send, recv, own, lands, _ = d2d[l]
        own, sib = _copies_wait(f"handoff_wait_{l}", _sibling_plan, send, recv, own, lands, after)
        for k in range(5):
            big_out[k] = _adamw_big(l, big_w[k], big_m[k], big_v[k], own[k], sib[k], big_out[k])

    for l in reversed(range(DEPTH)):
        w4, wglu, wbs4, wba4, wout, wb4, wc4, g1, g2, dsk, bgl, proj, ht, states, yraw, ys, ya, o, lse = saved[l]
        started = None
        if l + 1 < DEPTH:
            started = ici[l + 1][4] + d2d[l + 2][4] if l + 2 < DEPTH else ici[l + 1][4]
        dys, dya, dproj, dwbs, dwba, dwout, dg2 = _merge_out_bwd(dx, ys, ya, proj, wbs4, wba4, wout, g2, started)
        dyraw, du_skip, dproj, dwglu, dbgl, ddsk = _s5_tail_bwd(dys, yraw, proj, dsk, wglu, bgl, dproj)
        dproj, dbb, dcc, da = _s5_scan_bwd(dyraw, states, proj, du_skip, wb4, wc4, ar4[l], ai4[l], dproj)
        do, cterm, dproj = _attn_mix_bwd(dya, o, lse, proj, dproj)
        dproj = _attn_core_bwd(proj, do, lse, cterm, dproj)
        dh, dwin = _proj_bwd(dproj, ht, w4)
        parts = [dwin, dwglu.reshape(NCHIP, 128, SSM), dwbs, dwba, dwout.reshape(NCHIP, D // NCHIP, D)]
        exchange_start(l, parts)
        dx, dg1 = _rms_bwd(dh, xs[l], g1, dx, ici[l][4])
        if l + 1 < DEPTH:
            handoff_start(l + 1, [ici[l][4]])
        small["g1"][l], small["g2"][l], small["dsk"][l], small["bgl"][l] = dg1, dg2, ddsk, dbgl
        small["da"][l], small["dbb"][l], small["dcc"][l] = da, dbb, dcc

    dlre, dlim, dldt, dbre_c, dbim_c = _s5_prep_bwd(
        lre_c, lim_c, ldt_c, bre_c, bim_c, jnp.stack(small["da"]).reshape(nb, 1, SW),
        jnp.stack(small["dbb"]).reshape(nb, 16, SW))
    dcc = jnp.stack(small["dcc"]).reshape(nb, 16, SW)
    b_back = lambda t: t.reshape(DEPTH, NCB, 16, 8, 64).transpose(0, 1, 3, 4, 2).reshape(DEPTH, 32, 64, 16)
    c_back = lambda t: t.reshape(DEPTH, NCB, 16, 8, 64).transpose(0, 1, 3, 2, 4).reshape(DEPTH, 32, 16, 64)
    small_g = [jnp.stack(small["g1"]).reshape(DEPTH, D), dlre.reshape(DEPTH, 32, 64), dlim.reshape(DEPTH, 32, 64),
               dldt.reshape(DEPTH, 32, 64).sum(-1), b_back(dbre_c), b_back(dbim_c), c_back(dcc[:, :, :HW]),
               c_back(dcc[:, :, HW:]),
               jnp.stack(small["dsk"]).reshape(DEPTH, SSM), jnp.stack(small["bgl"]).reshape(DEPTH, SSM),
               jnp.stack(small["g2"]).reshape(DEPTH, D)]
    part = _flatten_small(small_g).astype(BF16)
    send, recv, srcs, lands, token = _copies_start("small_start", _everyone_plan, 7, [part],
                                                   [_with_own_slot(part, device, 8)])
    update(DEPTH - 1, [token, ici[0][4]])
    for l in range(DEPTH - 2, 0, -1):
        update(l, [out[0] for out in big_out])
    handoff_start(0, [out[0] for out in big_out])
    update(0, [d2d[0][4]])
    _, (parts8,) = _copies_wait("small_wait", _everyone_plan, send, recv, srcs, lands, [out[0] for out in big_out])
    g_flat, d_flat, nm_flat, nv_flat = _adamw_small(flat_w, parts8, flat_m, flat_v)
    sg = _unflatten_small(g_flat, small_w)
    sd = _unflatten_small(d_flat, small_w)
    snm = _unflatten_small(nm_flat, small_w)
    snv = _unflatten_small(nv_flat, small_w)

    def ordered(sm, which):
        bg = [big_out[k][which] for k in range(5)]
        bg[1] = bg[1].reshape(DEPTH, 128, SSM)
        return [sm[0], bg[0], sm[1], sm[2], sm[3], sm[4], sm[5], sm[6], sm[7], sm[8], bg[1], sm[9], bg[2], bg[3],
                bg[4], sm[10]]

    return (loss, dx[None], *ordered(sg, 0), *ordered(sd, 1), *ordered(snm, 2), *ordered(snv, 3))
```

```python
import math

import jax
import jax.numpy as jnp
from jax import lax
from jax.experimental import pallas as pl
from jax.experimental.pallas import tpu as pltpu

F32 = jnp.float32
BF16 = jnp.bfloat16
MESH = pl.DeviceIdType.MESH

DEPTH = 4
L = 2048
D = 1024
NCOL = 8192
SSM = 512
AW = 512
QKV = 1536
RMS_EPS = 1e-6
NCHIP = 4

COL_U, COL_ZS, COL_Q, COL_K, COL_V, COL_ZA, COL_GS, COL_GA = 0, 4, 8, 20, 32, 44, 48, 56

NSEG = 8
SEG = L // NSEG
NCB = 4
HW = 512
SW = 2 * HW

HEAD = 128
DILATIONS = (1, 4, 16)
SCALE = HEAD ** -0.5
NEG = -1e30

ADAM_LR, ADAM_B1, ADAM_B2, ADAM_EPS, ADAM_WD, ADAM_STEP = 0.001, 0.9, 0.999, 1e-08, 0.01, 10


def _sds(shape, dtype=F32):
    return jax.ShapeDtypeStruct(shape, dtype)


def _pcall(body, *, name, out_shape, grid=None, in_specs=None, out_specs=None, scratch=(), vmem_mb=None,
           aliases=None):
    params = {}
    if vmem_mb is not None:
        params["vmem_limit_bytes"] = vmem_mb << 20
    kw = {}
    if grid is not None:
        kw["grid"] = grid
    if in_specs is not None:
        kw["in_specs"] = in_specs
    if out_specs is not None:
        kw["out_specs"] = out_specs
    return pl.pallas_call(body, name=name, out_shape=out_shape, scratch_shapes=list(scratch),
                          compiler_params=pltpu.CompilerParams(**params),
                          input_output_aliases=aliases or {}, **kw)


def _dot(a, b):
    return jnp.dot(a, b, preferred_element_type=F32)


def _dot_nt(a, b):
    return lax.dot_general(a, b, (((1,), (1,)), ((), ())), preferred_element_type=F32)


def _dot_tn(a, b):
    return lax.dot_general(a, b, (((0,), (0,)), ((), ())), preferred_element_type=F32)


def _sigmoid(x):
    return jax.nn.sigmoid(x)


_GELU_K = math.sqrt(2.0 / math.pi)


def _gelu(x):
    return 0.5 * x * (1.0 + jnp.tanh(_GELU_K * (x + 0.044715 * (x * x * x))))


def _gelu_grad(x):
    t = jnp.tanh(_GELU_K * (x + 0.044715 * (x * x * x)))
    return 0.5 * (1.0 + t) + 0.5 * x * (1.0 - t * t) * (_GELU_K * (1.0 + 3.0 * 0.044715 * (x * x)))


def _token_operand(dep):
    if dep is None:
        return [], []
    return [dep], [pl.BlockSpec(memory_space=pl.ANY)]


def _rms_proj_fwd(x, g, w4, dep=None):
    tl, tn = 512, 1024
    ni = L // tl
    deps, dep_specs = _token_operand(dep)

    def body(x_ref, g_ref, w_ref, *rest):
        o_ref, h_ref = rest[len(deps):]
        rows = pl.ds(pl.multiple_of(pl.program_id(1) * tl, tl), tl)

        @pl.when(pl.program_id(0) == 0)
        def _():
            xx = x_ref[...]
            inv = lax.rsqrt(jnp.mean(xx * xx, axis=-1, keepdims=True) + RMS_EPS)
            h_ref[rows, :] = (xx * inv * g_ref[...]).astype(BF16)

        o_ref[...] = _dot(h_ref[rows, :], w_ref[...])

    proj, h = _pcall(
        body, name="rms_proj_fwd", grid=(NCOL // tn, ni),
        in_specs=[pl.BlockSpec((tl, D), lambda j, i: (jnp.where(j == 0, i, ni - 1), 0)),
                  pl.BlockSpec((1, D), lambda j, i: (0, 0)),
                  pl.BlockSpec((None, D, tn), lambda j, i: (lax.div(j, 2), 0, lax.rem(j, 2)))] + dep_specs,
        out_specs=[pl.BlockSpec((tl, tn), lambda j, i: (i, j)), pl.BlockSpec((L, D), lambda j, i: (0, 0))],
        out_shape=[_sds((L, NCOL)), _sds((L, D), BF16)], vmem_mb=48)(x, g, w4, *deps)

    def transpose(h_ref, ht_ref):
        ht_ref[...] = h_ref[...].astype(F32).T.astype(BF16)

    ht = _pcall(transpose, name="transpose_h", grid=(ni,), in_specs=[pl.BlockSpec((tl, D), lambda i: (i, 0))],
                out_specs=pl.BlockSpec((D, tl), lambda i: (0, i)), out_shape=_sds((D, L), BF16), vmem_mb=40)(h)
    return proj, ht


def _proj_bwd(dproj, ht, w4):
    tn = 1024

    def body(dp_ref, ht_ref, w_ref, dh_ref, dw_ref):
        @pl.when(pl.program_id(0) == 0)
        def _():
            dh_ref[...] = jnp.zeros((L, D), F32)

        dw_ref[...] = _dot(ht_ref[...], dp_ref[...]).astype(BF16)
        dh_ref[...] += _dot_nt(dp_ref[...], w_ref[...])

    return _pcall(
        body, name="proj_bwd", grid=(NCOL // tn,),
        in_specs=[pl.BlockSpec((L, tn), lambda j: (0, j)),
                  pl.BlockSpec((D, L), lambda j: (0, 0)),
                  pl.BlockSpec((None, D, tn), lambda j: (lax.div(j, 2), 0, lax.rem(j, 2)))],
        out_specs=[pl.BlockSpec((L, D), lambda j: (0, 0)),
                   pl.BlockSpec((None, D, tn), lambda j: (lax.div(j, 2), 0, lax.rem(j, 2)))],
        out_shape=[_sds((L, D)), _sds((NCHIP, D, NCOL // NCHIP), BF16)], vmem_mb=56)(dproj, ht, w4)


def _rms_bwd(dh, x, g, dxn, dep=None):
    tl = 512
    deps, dep_specs = _token_operand(dep)

    def body(dh_ref, x_ref, g_ref, dxn_ref, *rest):
        dx_ref, dg_ref = rest[len(deps):]
        i = pl.program_id(0)
        xx = x_ref[...]
        inv = lax.rsqrt(jnp.mean(xx * xx, axis=-1, keepdims=True) + RMS_EPS)
        nrm = xx * inv
        dh_v = dh_ref[...]
        dn = dh_v * g_ref[...]
        dx_ref[...] = dxn_ref[...] + inv * (dn - nrm * jnp.mean(dn * nrm, axis=-1, keepdims=True))
        part = jnp.sum(dh_v * nrm, axis=0, keepdims=True)

        @pl.when(i == 0)
        def _():
            dg_ref[...] = part

        @pl.when(i > 0)
        def _():
            dg_ref[...] += part

    row = pl.BlockSpec((tl, D), lambda i: (i, 0))
    vec = pl.BlockSpec((1, D), lambda i: (0, 0))
    return _pcall(body, name="rms_bwd", grid=(L // tl,), in_specs=[row, row, vec, row] + dep_specs,
                  out_specs=[row, vec], out_shape=[_sds((L, D)), _sds((1, D))], vmem_mb=40)(dh, x, g, dxn, *deps)


def _s5_param_math(lre, lim, ldt, bre, bim):
    lr = jnp.minimum(lre, -1e-4)
    dt = jnp.exp(ldt)
    mag = jnp.exp(lr * dt)
    ar = mag * jnp.cos(lim * dt)
    ai = mag * jnp.sin(lim * dt)
    den = lr * lr + lim * lim
    nr = ar - 1.0
    qr = (nr * lr + ai * lim) / den
    qi = (ai * lr - nr * lim) / den
    return ar, ai, qr * bre - qi * bim, qr * bim + qi * bre


def _group_of_lane(width):
    lane = lax.broadcasted_iota(jnp.int32, (16, width), 1)
    return lax.shift_right_logical(jnp.bitwise_and(lane, HW - 1), 6)


def _vec_spec():
    return pl.BlockSpec((None, 1, HW), lambda n: (n, 0, 0))


def _mat_spec():
    return pl.BlockSpec((None, 16, HW), lambda n: (n, 0, 0))


def _s5_prep(lre, lim, ldt, bre, bim, cre, cim):
    n = lre.shape[0]

    def body(lre_ref, lim_ref, ldt_ref, bre_ref, bim_ref, cre_ref, cim_ref, ar_ref, ai_ref, wb_ref, wct_ref):
        ar, ai, bbr, bbi = _s5_param_math(lre_ref[...], lim_ref[...], ldt_ref[...], bre_ref[...], bim_ref[...])
        ar_ref[...] = ar
        ai_ref[...] = ai
        group = _group_of_lane(SW)
        bb = jnp.concatenate([bbr, bbi], axis=1)
        cc = jnp.concatenate([cre_ref[...], -cim_ref[...]], axis=1)
        for g in range(8):
            rows = pl.ds(16 * g, 16)
            wb_ref[rows, :] = jnp.where(group == g, bb, 0.0).astype(BF16)
            wct_ref[rows, :] = jnp.where(group == g, cc, 0.0).astype(BF16)

    wide = pl.BlockSpec((None, 128, SW), lambda n: (n, 0, 0))
    return _pcall(body, name="s5_prep", grid=(n,),
                  in_specs=[_vec_spec()] * 3 + [_mat_spec()] * 4,
                  out_specs=[_vec_spec(), _vec_spec(), wide, wide],
                  out_shape=[_sds((n, 1, HW)), _sds((n, 1, HW)), _sds((n, 128, SW), BF16), _sds((n, 128, SW), BF16)])(
                      lre, lim, ldt, bre, bim, cre, cim)


def _s5_prep_bwd(lre, lim, ldt, bre, bim, ga, gbb):
    n = lre.shape[0]

    def body(lre_ref, lim_ref, ldt_ref, bre_ref, bim_ref, ga_ref, gbb_ref,
             dlre_ref, dlim_ref, dldt_ref, dbre_ref, dbim_ref):
        _, vjp = jax.vjp(_s5_param_math, lre_ref[...], lim_ref[...], ldt_ref[...], bre_ref[...], bim_ref[...])
        ga, gbb = ga_ref[...], gbb_ref[...]
        dlre, dlim, dldt, dbre, dbim = vjp((ga[:, :HW], ga[:, HW:], gbb[:, :HW], gbb[:, HW:]))
        dlre_ref[...] = dlre
        dlim_ref[...] = dlim
        dldt_ref[...] = dldt
        dbre_ref[...] = dbre
        dbim_ref[...] = dbim

    return _pcall(body, name="s5_prep_bwd", grid=(n,),
                  in_specs=[_vec_spec()] * 3 + [_mat_spec()] * 2 +
                  [pl.BlockSpec((None, 1, SW), lambda n: (n, 0, 0)), pl.BlockSpec((None, 16, SW), lambda n: (n, 0, 0))],
                  out_specs=[_vec_spec()] * 3 + [_mat_spec()] * 2,
                  out_shape=[_sds((n, 1, HW))] * 3 + [_sds((n, 16, HW))] * 2)(lre, lim, ldt, bre, bim, ga, gbb)


def _diag_blocks(full):
    group = _group_of_lane(SW)
    out = jnp.where(group == 0, full[0:16, :], 0.0)
    for g in range(1, 8):
        out = out + jnp.where(group == g, full[16 * g:16 * g + 16, :], 0.0)
    return out


def _permute_rows(dst_ref, src_ref):
    for s in range(NSEG):
        dst_ref[pl.ds(s, SEG, stride=NSEG), :] = src_ref[pl.ds(s * SEG, SEG), :]


def _unpermute_rows(src_ref, s):
    return src_ref[pl.ds(s, SEG, stride=NSEG), :]


def _row_of(slab, rows, j):
    return jnp.sum(jnp.where(rows == j, slab, 0.0), axis=0, keepdims=True)


def _s5_scan_fwd(proj, wb4, wc4, ar4, ai4):
    def body(u_ref, wb_ref, wc_ref, ar_ref, ai_ref, x_ref, y_ref, up_ref, d_ref, yp_ref):
        _permute_rows(up_ref, u_ref)
        d_ref[...] = _dot(up_ref[...].astype(BF16), wb_ref[...])
        ar = jnp.broadcast_to(ar_ref[...], (NSEG, HW))
        ai = jnp.broadcast_to(ai_ref[...], (NSEG, HW))
        zero = jnp.zeros((NSEG, HW), F32)

        def drive(t):
            row = pl.multiple_of(t * NSEG, NSEG)
            dd = d_ref[pl.ds(row, NSEG), :]
            return row, dd[:, :HW], dd[:, HW:]

        def local_step(t, c):
            xr, xi = c
            _, dr, di = drive(t)
            return ar * xr - ai * xi + dr, ar * xi + ai * xr + di

        fr, fi = lax.fori_loop(0, SEG, local_step, (zero, zero), unroll=8)
        pr, pi_ = ar_ref[...], ai_ref[...]
        for _ in range(int(math.log2(SEG))):
            pr, pi_ = pr * pr - pi_ * pi_, 2.0 * pr * pi_
        rows = lax.broadcasted_iota(jnp.int32, (NSEG, HW), 0)
        cr, ci = zero, zero
        for j in range(NSEG - 1):
            sr, si = _row_of(cr, rows, j), _row_of(ci, rows, j)
            gr, gi = _row_of(fr, rows, j), _row_of(fi, rows, j)
            nr = pr * sr - pi_ * si + gr
            ni = pr * si + pi_ * sr + gi
            cr = jnp.where(rows == j + 1, nr, cr)
            ci = jnp.where(rows == j + 1, ni, ci)

        def true_step(t, c):
            xr, xi = c
            row, dr, di = drive(t)
            nr = ar * xr - ai * xi + dr
            ni = ar * xi + ai * xr + di
            x_ref[pl.ds(row, NSEG), :] = jnp.concatenate([nr, ni], axis=1)
            return nr, ni

        lax.fori_loop(0, SEG, true_step, (cr, ci), unroll=8)
        yp_ref[...] = _dot_nt(x_ref[...].astype(BF16), wc_ref[...])
        for s in range(NSEG):
            y_ref[pl.ds(s * SEG, SEG), :] = _unpermute_rows(yp_ref, s)

    return _pcall(
        body, name="s5_scan_fwd", grid=(NCB,),
        in_specs=[pl.BlockSpec((L, 128), lambda b: (0, COL_U + b)),
                  pl.BlockSpec((None, 128, SW), lambda b: (b, 0, 0)),
                  pl.BlockSpec((None, 128, SW), lambda b: (b, 0, 0)),
                  pl.BlockSpec((None, 1, HW), lambda b: (b, 0, 0)),
                  pl.BlockSpec((None, 1, HW), lambda b: (b, 0, 0))],
        out_specs=[pl.BlockSpec((None, L, SW), lambda b: (b, 0, 0)),
                   pl.BlockSpec((L, 128), lambda b: (0, b))],
        out_shape=[_sds((NCB, L, SW)), _sds((L, SSM))],
        scratch=[pltpu.VMEM((L, 128), F32), pltpu.VMEM((L, SW), F32), pltpu.VMEM((L, 128), F32)],
        vmem_mb=56)(proj, wb4, wc4, ar4, ai4)


def _s5_scan_bwd(dy, xs, proj, du_skip, wb4, wc4, ar4, ai4, dproj):
    def body(dy_ref, x_ref, u_ref, dus_ref, wb_ref, wc_ref, ar_ref, ai_ref, _,
             du_ref, dbb_ref, dcc_ref, da_ref, dyp_ref, up_ref, g_ref, dup_ref):
        _permute_rows(dyp_ref, dy_ref)
        _permute_rows(up_ref, u_ref)
        dyp = dyp_ref[...].astype(BF16)
        g_ref[...] = _dot(dyp, wc_ref[...])
        dcc = _diag_blocks(_dot_tn(dyp, x_ref[...].astype(BF16)))
        dcc_ref[...] = jnp.concatenate([dcc[:, :HW], -dcc[:, HW:]], axis=1)
        ar = jnp.broadcast_to(ar_ref[...], (NSEG, HW))
        ai = jnp.broadcast_to(ai_ref[...], (NSEG, HW))
        zero = jnp.zeros((NSEG, HW), F32)

        def load(t):
            row = pl.multiple_of(t * NSEG, NSEG)
            gg = g_ref[pl.ds(row, NSEG), :]
            return row, gg[:, :HW], gg[:, HW:]

        def local_step(k, c):
            lr, li = c
            _, gr, gi = load(SEG - 1 - k)
            return ar * lr + ai * li + gr, ar * li - ai * lr + gi

        fr, fi = lax.fori_loop(0, SEG, local_step, (zero, zero), unroll=8)
        pr, pi_ = ar_ref[...], -ai_ref[...]
        for _ in range(int(math.log2(SEG))):
            pr, pi_ = pr * pr - pi_ * pi_, 2.0 * pr * pi_
        rows = lax.broadcasted_iota(jnp.int32, (NSEG, HW), 0)
        cr, ci = zero, zero
        for j in range(NSEG - 1, 0, -1):
            sr, si = _row_of(cr, rows, j), _row_of(ci, rows, j)
            gr, gi = _row_of(fr, rows, j), _row_of(fi, rows, j)
            nr = pr * sr - pi_ * si + gr
            ni = pr * si + pi_ * sr + gi
            cr = jnp.where(rows == j - 1, nr, cr)
            ci = jnp.where(rows == j - 1, ni, ci)

        def true_step(k, c):
            lr, li, gar, gai = c
            t = SEG - 1 - k
            row, gr, gi = load(t)
            nr = ar * lr + ai * li + gr
            ni = ar * li - ai * lr + gi
            g_ref[pl.ds(row, NSEG), :] = jnp.concatenate([nr, ni], axis=1)
            prow = pl.multiple_of(jnp.maximum(t - 1, 0) * NSEG, NSEG)
            xp = x_ref[pl.ds(prow, NSEG), :]
            live = (t > 0).astype(F32)
            xr, xi = xp[:, :HW] * live, xp[:, HW:] * live
            return nr, ni, gar + (nr * xr + ni * xi), gai + (ni * xr - nr * xi)

        _, _, gar, gai = lax.fori_loop(0, SEG, true_step, (cr, ci, zero, zero), unroll=4)
        l0 = g_ref[1:NSEG, :]
        xl = x_ref[(SEG - 1) * NSEG:(SEG - 1) * NSEG + NSEG - 1, :]
        l0r, l0i, xlr, xli = l0[:, :HW], l0[:, HW:], xl[:, :HW], xl[:, HW:]
        gar_t = jnp.sum(gar, axis=0, keepdims=True) + jnp.sum(l0r * xlr + l0i * xli, axis=0, keepdims=True)
        gai_t = jnp.sum(gai, axis=0, keepdims=True) + jnp.sum(l0i * xlr - l0r * xli, axis=0, keepdims=True)
        da_ref[...] = jnp.concatenate([gar_t, gai_t], axis=1)
        lam = g_ref[...].astype(BF16)
        dbb_ref[...] = _diag_blocks(_dot_tn(up_ref[...].astype(BF16), lam))
        dup_ref[...] = _dot_nt(lam, wb_ref[...])
        for s in range(NSEG):
            sl = pl.ds(s * SEG, SEG)
            du_ref[sl, :] = (_unpermute_rows(dup_ref, s) + dus_ref[sl, :]).astype(BF16)

    col = lambda off: pl.BlockSpec((L, 128), lambda b: (0, off + b))
    return _pcall(
        body, name="s5_scan_bwd", grid=(NCB,),
        in_specs=[col(0), pl.BlockSpec((None, L, SW), lambda b: (b, 0, 0)), col(COL_U), col(0),
                  pl.BlockSpec((None, 128, SW), lambda b: (b, 0, 0)),
                  pl.BlockSpec((None, 128, SW), lambda b: (b, 0, 0)),
                  pl.BlockSpec((None, 1, HW), lambda b: (b, 0, 0)),
                  pl.BlockSpec((None, 1, HW), lambda b: (b, 0, 0)),
                  pl.BlockSpec(memory_space=pl.ANY)],
        out_specs=[col(COL_U), pl.BlockSpec((None, 16, SW), lambda b: (b, 0, 0)),
                   pl.BlockSpec((None, 16, SW), lambda b: (b, 0, 0)),
                   pl.BlockSpec((None, 1, SW), lambda b: (b, 0, 0))],
        out_shape=[_sds((L, NCOL), BF16), _sds((NCB, 16, SW)), _sds((NCB, 16, SW)), _sds((NCB, 1, SW))],
        scratch=[pltpu.VMEM((L, 128), F32), pltpu.VMEM((L, 128), F32), pltpu.VMEM((L, SW), F32),
                 pltpu.VMEM((L, 128), F32)],
        aliases={8: 0}, vmem_mb=56)(dy, xs, proj, du_skip, wb4, wc4, ar4, ai4, dproj)


def _s5_tail_fwd(yraw, proj, dsk, wglu, bglu):
    tl = 512

    def body(y_ref, u_ref, z_ref, dsk_ref, w_ref, b_ref, o_ref):
        y1 = y_ref[...] + dsk_ref[...] * u_ref[...]
        y2 = _gelu(y1)
        gl = _dot(y2.astype(BF16), w_ref[...]) + b_ref[...]
        z = z_ref[...]
        o_ref[...] = (y2 * _sigmoid(gl)) * (z * _sigmoid(z))

    blk = lambda c: pl.BlockSpec((tl, SSM), lambda i: (i, c))
    vec = pl.BlockSpec((1, SSM), lambda i: (0, 0))
    return _pcall(body, name="s5_tail_fwd", grid=(L // tl,),
                  in_specs=[blk(0), blk(0), blk(1), vec, pl.BlockSpec((SSM, SSM), lambda i: (0, 0)), vec],
                  out_specs=blk(0), out_shape=_sds((L, SSM)), vmem_mb=40)(yraw, proj, proj, dsk, wglu, bglu)


def _s5_tail_bwd(dys, yraw, proj, dsk, wglu, bglu, dproj):
    tl = 512
    ni = L // tl

    def body(dys_ref, y_ref, u_ref, z_ref, dsk_ref, w_ref, b_ref, _,
             dy_ref, dus_ref, dz_ref, dw_ref, db_ref, dd_ref, acc_ref):
        i = pl.program_id(0)
        u = u_ref[...]
        y1 = y_ref[...] + dsk_ref[...] * u
        y2 = _gelu(y1)
        y2b = y2.astype(BF16)
        sg = _sigmoid(_dot(y2b, w_ref[...]) + b_ref[...])
        y3 = y2 * sg
        z = z_ref[...]
        sz = _sigmoid(z)
        dys = dys_ref[...]
        dy3 = dys * (z * sz)
        dz_ref[...] = (dys * y3 * (sz * (1.0 + z * (1.0 - sz)))).astype(BF16)
        dgl = (dy3 * y2) * (sg * (1.0 - sg))
        dglb = dgl.astype(BF16)
        dy2 = dy3 * sg + _dot_nt(dglb, w_ref[...])
        dy1 = dy2 * _gelu_grad(y1)
        dy_ref[...] = dy1
        dus_ref[...] = dsk_ref[...] * dy1
        dw = _dot_tn(y2b, dglb)
        db = jnp.sum(dgl, axis=0, keepdims=True)
        dd = jnp.sum(dy1 * u, axis=0, keepdims=True)

        @pl.when(i == 0)
        def _():
            acc_ref[...] = dw
            db_ref[...] = db
            dd_ref[...] = dd

        @pl.when(i > 0)
        def _():
            acc_ref[...] += dw
            db_ref[...] += db
            dd_ref[...] += dd

        @pl.when(i == ni - 1)
        def _():
            dw_ref[...] = acc_ref[...].astype(BF16)

    blk = lambda c: pl.BlockSpec((tl, SSM), lambda i: (i, c))
    vec = pl.BlockSpec((1, SSM), lambda i: (0, 0))
    mat = pl.BlockSpec((SSM, SSM), lambda i: (0, 0))
    return _pcall(body, name="s5_tail_bwd", grid=(ni,),
                  in_specs=[blk(0), blk(0), blk(0), blk(1), vec, mat, vec, pl.BlockSpec(memory_space=pl.ANY)],
                  out_specs=[blk(0), blk(0), blk(COL_ZS // 4), mat, vec, vec],
                  out_shape=[_sds((L, SSM)), _sds((L, SSM)), _sds((L, NCOL), BF16), _sds((SSM, SSM), BF16),
                             _sds((1, SSM)), _sds((1, SSM))],
                  scratch=[pltpu.VMEM((SSM, SSM), F32)], aliases={7: 2},
                  vmem_mb=40)(dys, yraw, proj, proj, dsk, wglu, bglu, dproj)


def _attn_blocks(dil):
    n = L // dil
    return [(i * HEAD * dil + r, r * n + i * HEAD, i == 0) for r in range(dil) for i in range(n // HEAD)]


def _rows(start, dil):
    return pl.ds(start, HEAD) if dil == 1 else pl.ds(start, HEAD, stride=dil)


def _residue_rows(r, dil):
    n = L // dil
    return pl.ds(0, L) if dil == 1 else pl.ds(r, n, stride=dil)


def _gather_residues(pairs, dil, dtype=BF16):
    n = L // dil
    for src, dst in pairs:
        for r in range(dil):
            dst[pl.ds(r * n, n), :] = src[_residue_rows(r, dil), :].astype(dtype)


def _band_masks():
    ri = lax.broadcasted_iota(jnp.int32, (HEAD, HEAD), 0)
    ci = lax.broadcasted_iota(jnp.int32, (HEAD, HEAD), 1)
    ri2 = lax.broadcasted_iota(jnp.int32, (HEAD, 2 * HEAD), 0)
    ci2 = lax.broadcasted_iota(jnp.int32, (HEAD, 2 * HEAD), 1)
    return ci <= ri, jnp.logical_and(ci2 >= ri2, ci2 - HEAD <= ri2)


def _qkv_specs(index):
    def spec(off):
        return pl.BlockSpec((L, HEAD), lambda gi, h: (0, off + index(gi, h)))
    return [spec(off) for off in (COL_Q, COL_K, COL_V)]


def _attn_core_fwd(proj):
    def body(q_ref, k_ref, v_ref, o_ref, lse_ref, qp_ref, kp_ref, vp_ref):
        gi = pl.program_id(0)
        m_cur, m_both = _band_masks()
        for g, dil in enumerate(DILATIONS):
            @pl.when(gi == g)
            def _(dil=dil):
                _gather_residues(((q_ref, qp_ref), (k_ref, kp_ref), (v_ref, vp_ref)), dil)
                for st, base, first in _attn_blocks(dil):
                    q = qp_ref[pl.ds(base, HEAD), :]
                    keys = pl.ds(base, HEAD) if first else pl.ds(base - HEAD, 2 * HEAD)
                    s = jnp.where(m_cur if first else m_both, _dot_nt(q, kp_ref[keys, :]) * SCALE, NEG)
                    mx = jnp.max(s, axis=-1, keepdims=True)
                    p = jnp.exp(s - mx)
                    den = jnp.sum(p, axis=-1, keepdims=True)
                    r = _rows(st, dil)
                    o_ref[r, :] = _dot(p.astype(BF16), vp_ref[keys, :]) / den
                    lse_ref[r, :] = jnp.broadcast_to(mx + jnp.log(den), (HEAD, HEAD))

    idx = lambda gi, h: gi * 4 + h
    out = pl.BlockSpec((L, HEAD), lambda gi, h: (0, gi * 4 + h))
    return _pcall(body, name="attn_core_fwd", grid=(3, 4), in_specs=_qkv_specs(idx), out_specs=[out, out],
                  out_shape=[_sds((L, QKV)), _sds((L, QKV))], scratch=[pltpu.VMEM((L, HEAD), BF16)] * 3,
                  vmem_mb=40)(proj, proj, proj)


def _attn_mix_fwd(o, lse, proj):
    tl = 512

    def body(o_ref, l_ref, z_ref, y_ref):
        for h in range(4):
            c = [pl.ds((g * 4 + h) * HEAD, HEAD) for g in range(3)]
            ls = [l_ref[:, c[g]] for g in range(3)]
            m = jnp.maximum(jnp.maximum(ls[0], ls[1]), ls[2])
            e = [jnp.exp(ls[g] - m) for g in range(3)]
            y = (e[0] * o_ref[:, c[0]] + e[1] * o_ref[:, c[1]] + e[2] * o_ref[:, c[2]]) / (e[0] + e[1] + e[2])
            z = z_ref[:, pl.ds(h * HEAD, HEAD)]
            y_ref[:, pl.ds(h * HEAD, HEAD)] = y * (z * _sigmoid(z))

    wide = pl.BlockSpec((tl, QKV), lambda i: (i, 0))
    return _pcall(body, name="attn_mix_fwd", grid=(L // tl,),
                  in_specs=[wide, wide, pl.BlockSpec((tl, AW), lambda i: (i, COL_ZA // 4))],
                  out_specs=pl.BlockSpec((tl, AW), lambda i: (i, 0)), out_shape=_sds((L, AW)),
                  vmem_mb=40)(o, lse, proj)


def _attn_mix_bwd(dya, o, lse, proj, dproj):
    tl = 512

    def body(dya_ref, o_ref, l_ref, z_ref, _, do_ref, c_ref, dz_ref):
        for h in range(4):
            c = [pl.ds((g * 4 + h) * HEAD, HEAD) for g in range(3)]
            hs = pl.ds(h * HEAD, HEAD)
            ls = [l_ref[:, c[g]] for g in range(3)]
            m = jnp.maximum(jnp.maximum(ls[0], ls[1]), ls[2])
            e = [jnp.exp(ls[g] - m) for g in range(3)]
            den = e[0] + e[1] + e[2]
            al = [e[g] / den for g in range(3)]
            y = al[0] * o_ref[:, c[0]] + al[1] * o_ref[:, c[1]] + al[2] * o_ref[:, c[2]]
            z = z_ref[:, hs]
            sz = _sigmoid(z)
            dya = dya_ref[:, hs]
            dz_ref[:, hs] = (dya * y * (sz * (1.0 + z * (1.0 - sz)))).astype(BF16)
            dy = dya * (z * sz)
            tot = jnp.sum(dy * y, axis=-1, keepdims=True)
            for g in range(3):
                do_ref[:, c[g]] = al[g] * dy
                c_ref[:, c[g]] = -(al[g] * tot)

    wide = pl.BlockSpec((tl, QKV), lambda i: (i, 0))
    nar = pl.BlockSpec((tl, AW), lambda i: (i, 0))
    za = pl.BlockSpec((tl, AW), lambda i: (i, COL_ZA // 4))
    return _pcall(body, name="attn_mix_bwd", grid=(L // tl,),
                  in_specs=[nar, wide, wide, za, pl.BlockSpec(memory_space=pl.ANY)],
                  out_specs=[wide, wide, za], out_shape=[_sds((L, QKV)), _sds((L, QKV)), _sds((L, NCOL), BF16)],
                  aliases={4: 2}, vmem_mb=48)(dya, o, lse, proj, dproj)


def _attn_core_bwd(proj, do, lse, cc, dproj):
    def body(q_ref, k_ref, v_ref, do_ref, lse_ref, c_ref, _, dp_ref,
             qp_ref, kp_ref, vp_ref, dop_ref, lsep_ref, cp_ref, dqp_ref, dkp_ref, dvp_ref, tok_ref, out_ref, sems):
        gi = pl.program_id(0)
        head = gi * 4 + pl.program_id(1)
        m_cur, m_both = _band_masks()
        dkp_ref[...] = jnp.zeros((L, HEAD), F32)
        dvp_ref[...] = jnp.zeros((L, HEAD), F32)
        copies = []
        for g, dil in enumerate(DILATIONS):
            @pl.when(gi == g)
            def _(dil=dil):
                _gather_residues(((q_ref, qp_ref), (k_ref, kp_ref), (v_ref, vp_ref), (do_ref, dop_ref)), dil)
                _gather_residues(((lse_ref, lsep_ref), (c_ref, cp_ref)), dil, F32)
                for _, base, first in _attn_blocks(dil):
                    mine = pl.ds(base, HEAD)
                    keys = mine if first else pl.ds(base - HEAD, 2 * HEAD)
                    q, do_b, kb = qp_ref[mine, :], dop_ref[mine, :], kp_ref[keys, :]
                    lse_b, c_b = lsep_ref[mine, :], cp_ref[mine, :]
                    if not first:
                        lse_b = jnp.concatenate([lse_b, lse_b], axis=1)
                        c_b = jnp.concatenate([c_b, c_b], axis=1)
                    s = _dot_nt(q, kb) * SCALE
                    p = jnp.where(m_cur if first else m_both, jnp.exp(s - lse_b), 0.0)
                    ds = (p * (_dot_nt(do_b, vp_ref[keys, :]) + c_b) * SCALE).astype(BF16)
                    dqp_ref[mine, :] = _dot(ds, kb)
                    dkp_ref[keys, :] += _dot_tn(ds, q)
                    dvp_ref[keys, :] += _dot_tn(p.astype(BF16), do_b)
                n = L // dil
                for slot, src in enumerate((dqp_ref, dkp_ref, dvp_ref)):
                    for r in range(dil):
                        tok_ref[_residue_rows(r, dil), :] = src[pl.ds(r * n, n), :]
                    out_ref[slot] = tok_ref[...].astype(BF16)

        for slot, off in enumerate((COL_Q, COL_K, COL_V)):
            cols = pl.ds(pl.multiple_of((off + head) * HEAD, HEAD), HEAD)
            copies.append(pltpu.make_async_copy(out_ref.at[slot], dp_ref.at[:, cols], sems.at[slot]))
            copies[-1].start()
        for cp in copies:
            cp.wait()

    idx = lambda gi, h: gi * 4 + h
    blk = pl.BlockSpec((L, HEAD), lambda gi, h: (0, gi * 4 + h))
    hbm = pl.BlockSpec(memory_space=pl.ANY)
    return _pcall(body, name="attn_core_bwd", grid=(3, 4), in_specs=_qkv_specs(idx) + [blk, blk, blk, hbm],
                  out_specs=hbm, out_shape=_sds((L, NCOL), BF16),
                  scratch=[pltpu.VMEM((L, HEAD), BF16)] * 4 + [pltpu.VMEM((L, HEAD), F32)] * 6 +
                  [pltpu.VMEM((3, L, HEAD), BF16), pltpu.SemaphoreType.DMA((3,))],
                  aliases={6: 0}, vmem_mb=48)(proj, proj, proj, do, lse, cc, dproj)


def _merge_fwd_math(ys_b, ya_b, gs, ga, wbs_ref, wba_ref, wout_ref):
    bs = jnp.concatenate([_dot(ys_b, wbs_ref[s]) for s in range(NCHIP)], axis=1)
    ba = jnp.concatenate([_dot(ya_b, wba_ref[s]) for s in range(NCHIP)], axis=1)
    sgs, sga = _sigmoid(gs), _sigmoid(ga)
    merged = sgs * bs + sga * ba
    out = _dot(merged.astype(BF16), wout_ref[...])
    inv = lax.rsqrt(jnp.mean(out * out, axis=-1, keepdims=True) + RMS_EPS)
    return bs, ba, sgs, sga, merged, out, inv


def _merge_specs(tl):
    row = lambda w, c: pl.BlockSpec((tl, w), lambda i: (i, c))
    return [row(SSM, 0), row(AW, 0), row(D, COL_GS // 8), row(D, COL_GA // 8),
            pl.BlockSpec((NCHIP, SSM, D // NCHIP), lambda i: (0, 0, 0)),
            pl.BlockSpec((NCHIP, AW, D // NCHIP), lambda i: (0, 0, 0)),
            pl.BlockSpec((D, D), lambda i: (0, 0)),
            pl.BlockSpec((1, D), lambda i: (0, 0))]


def _merge_out_fwd(ys, ya, proj, wbs4, wba4, wout, g2, x, dep=None):
    tl = 512
    deps, dep_specs = _token_operand(dep)

    def body(ys_ref, ya_ref, gs_ref, ga_ref, wbs_ref, wba_ref, wout_ref, g2_ref, x_ref, *rest):
        o_ref = rest[-1]
        *_, out, inv = _merge_fwd_math(ys_ref[...].astype(BF16), ya_ref[...].astype(BF16), gs_ref[...],
                                       ga_ref[...], wbs_ref, wba_ref, wout_ref)
        o_ref[...] = x_ref[...] + out * inv * g2_ref[...]

    row = pl.BlockSpec((tl, D), lambda i: (i, 0))
    return _pcall(body, name="merge_out_fwd", grid=(L // tl,), in_specs=_merge_specs(tl) + [row] + dep_specs,
                  out_specs=row, out_shape=_sds((L, D)),
                  vmem_mb=48)(ys, ya, proj, proj, wbs4, wba4, wout, g2, x, *deps)


def _merge_out_bwd(dxn, ys, ya, proj, wbs4, wba4, wout, g2, dep=None):
    tl = 512
    ni = L // tl
    cw = D // NCHIP
    deps, dep_specs = _token_operand(dep)

    def body(dxn_ref, ys_ref, ya_ref, gs_ref, ga_ref, wbs_ref, wba_ref, wout_ref, g2_ref, *rest):
        (dys_ref, dya_ref, dgate_ref, dwbs_ref, dwba_ref, dwout_ref, dg2_ref,
         abs_ref, aba_ref, aout_ref) = rest[len(deps):]
        i = pl.program_id(0)

        @pl.when(i == 0)
        def _():
            abs_ref[...] = jnp.zeros(abs_ref.shape, F32)
            aba_ref[...] = jnp.zeros(aba_ref.shape, F32)
            aout_ref[...] = jnp.zeros(aout_ref.shape, F32)
            dg2_ref[...] = jnp.zeros(dg2_ref.shape, F32)

        ys_b, ya_b = ys_ref[...].astype(BF16), ya_ref[...].astype(BF16)
        bs, ba, sgs, sga, merged, out, inv = _merge_fwd_math(ys_b, ya_b, gs_ref[...], ga_ref[...],
                                                             wbs_ref, wba_ref, wout_ref)
        nrm = out * inv
        dxn = dxn_ref[...]
        dg2_ref[...] += jnp.sum(dxn * nrm, axis=0, keepdims=True)
        dn = dxn * g2_ref[...]
        dout = (inv * (dn - nrm * jnp.mean(dn * nrm, axis=-1, keepdims=True))).astype(BF16)
        aout_ref[...] += _dot_tn(merged.astype(BF16), dout)
        dm = _dot_nt(dout, wout_ref[...])
        dbs, dba = dm * sgs, dm * sga
        dgate_ref[:, :D] = (dm * bs * (sgs * (1.0 - sgs))).astype(BF16)
        dgate_ref[:, D:] = (dm * ba * (sga * (1.0 - sga))).astype(BF16)
        dbs_b, dba_b = dbs.astype(BF16), dba.astype(BF16)
        dys = None
        dya = None
        for s in range(NCHIP):
            cs = slice(s * cw, (s + 1) * cw)
            p_s = _dot_nt(dbs_b[:, cs], wbs_ref[s])
            p_a = _dot_nt(dba_b[:, cs], wba_ref[s])
            dys = p_s if dys is None else dys + p_s
            dya = p_a if dya is None else dya + p_a
            abs_ref[s] += _dot_tn(ys_b, dbs_b[:, cs])
            aba_ref[s] += _dot_tn(ya_b, dba_b[:, cs])

        dys_ref[...] = dys
        dya_ref[...] = dya

        @pl.when(i == ni - 1)
        def _():
            dwbs_ref[...] = abs_ref[...].astype(BF16)
            dwba_ref[...] = aba_ref[...].astype(BF16)
            dwout_ref[...] = aout_ref[...].astype(BF16)

    row = lambda w: pl.BlockSpec((tl, w), lambda i: (i, 0))
    w4 = pl.BlockSpec((NCHIP, SSM, cw), lambda i: (0, 0, 0))
    sq = pl.BlockSpec((D, D), lambda i: (0, 0))
    vec = pl.BlockSpec((1, D), lambda i: (0, 0))
    return _pcall(body, name="merge_out_bwd", grid=(ni,), in_specs=[row(D)] + _merge_specs(tl) + dep_specs,
                  out_specs=[row(SSM), row(AW), pl.BlockSpec((tl, 2 * D), lambda i: (i, COL_GS // 16)), w4, w4, sq,
                             vec],
                  out_shape=[_sds((L, SSM)), _sds((L, AW)), _sds((L, NCOL), BF16),
                             _sds((NCHIP, SSM, cw), BF16), _sds((NCHIP, AW, cw), BF16), _sds((D, D), BF16),
                             _sds((1, D))],
                  scratch=[pltpu.VMEM((NCHIP, SSM, cw), F32), pltpu.VMEM((NCHIP, AW, cw), F32),
                           pltpu.VMEM((D, D), F32)],
                  vmem_mb=56)(dxn, ys, ya, proj, proj, wbs4, wba4, wout, g2, *deps)


def _loss_head(y, target):
    tl = 512

    def body(y_ref, t_ref, loss_ref, dy_ref):
        i = pl.program_id(0)
        err = y_ref[...] - t_ref[...]
        dy_ref[...] = err / D
        part = 0.5 * jnp.sum(jnp.mean(err * err, axis=-1, keepdims=True), axis=0, keepdims=True)
        part = jnp.broadcast_to(part, (8, 128))

        @pl.when(i == 0)
        def _():
            loss_ref[...] = part

        @pl.when(i > 0)
        def _():
            loss_ref[...] += part

    row = pl.BlockSpec((tl, D), lambda i: (i, 0))
    return _pcall(body, name="loss_head", grid=(L // tl,), in_specs=[row, row],
                  out_specs=[pl.BlockSpec((8, 128), lambda i: (0, 0)), row],
                  out_shape=[_sds((8, 128)), _sds((L, D))], vmem_mb=40)(y, target)


def _adamw_math(w, g, m, v):
    m = ADAM_B1 * m + (1.0 - ADAM_B1) * g
    v = ADAM_B2 * v + (1.0 - ADAM_B2) * (g * g)
    m_hat = m / (1.0 - ADAM_B1 ** ADAM_STEP)
    v_hat = v / (1.0 - ADAM_B2 ** ADAM_STEP)
    delta = -ADAM_LR * (m_hat / (jnp.sqrt(v_hat) + ADAM_EPS) + ADAM_WD * w)
    return delta, m, v


def _adamw_big(layer, w, m, v, own, sib, prev):
    _, r, c = w.shape
    tr = min(r, 128)

    def body(w_ref, m_ref, v_ref, own_ref, sib_ref, *rest):
        g_ref, d_ref, nm_ref, nv_ref = rest[-4:]
        a = own_ref[0].astype(F32)
        b = sib_ref[0].astype(F32)
        for s in range(1, NCHIP):
            a = a + own_ref[s].astype(F32)
            b = b + sib_ref[s].astype(F32)
        g = a + b
        delta, nm, nv = _adamw_math(w_ref[...], g, m_ref[...], v_ref[...])
        g_ref[...] = g
        d_ref[...] = delta
        nm_ref[...] = nm
        nv_ref[...] = nv

    lay = pl.BlockSpec((None, tr, c), lambda i: (layer, i, 0))
    slots = pl.BlockSpec((NCHIP, tr, c), lambda i: (0, i, 0))
    ins = [w, m, v, own, sib]
    in_specs = [lay, lay, lay, slots, slots]
    aliases = {}
    if prev is not None:
        ins += list(prev)
        in_specs += [pl.BlockSpec(memory_space=pl.ANY)] * 4
        aliases = {5 + k: k for k in range(4)}
    return _pcall(body, name="adamw_big", grid=(r // tr,), in_specs=in_specs, out_specs=[lay] * 4,
                  out_shape=[_sds(w.shape)] * 4, aliases=aliases, vmem_mb=48)(*ins)


SMALL_TILE = 512


def _adamw_small(w, parts, m, v):
    r = w.shape[0]

    def body(w_ref, p_ref, m_ref, v_ref, g_ref, d_ref, nm_ref, nv_ref):
        g = p_ref[0].astype(F32)
        for dev in range(1, 8):
            g = g + p_ref[dev].astype(F32)
        delta, nm, nv = _adamw_math(w_ref[...], g, m_ref[...], v_ref[...])
        g_ref[...] = g
        d_ref[...] = delta
        nm_ref[...] = nm
        nv_ref[...] = nv

    row = pl.BlockSpec((SMALL_TILE, 128), lambda i: (i, 0))
    return _pcall(body, name="adamw_small", grid=(r // SMALL_TILE,),
                  in_specs=[row, pl.BlockSpec((8, SMALL_TILE, 128), lambda i: (0, i, 0)), row, row],
                  out_specs=[row] * 4, out_shape=[_sds((r, 128))] * 4, vmem_mb=40)(w, parts, m, v)


def _place():
    x, y, c = lax.axis_index("x"), lax.axis_index("y"), lax.axis_index("c")
    return x, y, c, 2 * x + y


def _chip_peer(x, y, j):
    return (1 - x if j & 2 else x), (1 - y if j & 1 else y)


_HBM = pl.BlockSpec(memory_space=pltpu.HBM)
_SEM = pl.BlockSpec(memory_space=pltpu.SEMAPHORE)
_EFFECT = pltpu.SideEffectType.DATAFLOW_SIDE_EFFECTING


def _in_hbm(a):
    return pltpu.with_memory_space_constraint(a, pltpu.HBM)


def _copies_start(name, plan, ncopy, srcs, lands, dep=None):
    n = len(srcs) + len(lands)
    deps, dep_specs = _token_operand(dep)

    def body(*refs):
        send_sems, recv_sems = refs[n + len(deps)], refs[n + len(deps) + 1]
        token = refs[-1]
        for k, (src, dst, dev) in enumerate(plan(refs[:len(srcs)], refs[len(srcs):n])):
            if dev is None:
                pltpu.make_async_copy(src, dst, recv_sems.at[k]).start()
            else:
                pltpu.make_async_remote_copy(src_ref=src, dst_ref=dst, send_sem=send_sems.at[k],
                                             recv_sem=recv_sems.at[k], device_id=dev, device_id_type=MESH).start()
        token[...] = jnp.zeros_like(token)

    bufs = list(srcs) + list(lands)
    outs = pl.pallas_call(
        body, name=name,
        out_shape=(pltpu.SemaphoreType.DMA((ncopy,)), pltpu.SemaphoreType.DMA((ncopy,)),
                   *[pltpu.HBM(a.shape, a.dtype) for a in bufs], _sds((8, 128))),
        in_specs=[_HBM] * n + dep_specs,
        out_specs=(_SEM, _SEM, *[_HBM] * n, pl.BlockSpec(memory_space=pltpu.VMEM)),
        input_output_aliases={i: 2 + i for i in range(n)},
        compiler_params=pltpu.CompilerParams(has_side_effects=_EFFECT),
    )(*[_in_hbm(a) for a in bufs], *deps)
    return outs[0], outs[1], list(outs[2:2 + len(srcs)]), list(outs[2 + len(srcs):2 + n]), outs[-1]


def _copies_wait(name, plan, send_sems, recv_sems, srcs, lands, after):
    n = len(srcs) + len(lands)
    after = list(after)

    def body(*refs):
        s_sems, r_sems = refs[n], refs[n + 1]
        for k, (src, dst, dev) in enumerate(plan(refs[:len(srcs)], refs[len(srcs):n])):
            if dev is None:
                pltpu.make_async_copy(src, dst, r_sems.at[k]).wait()
                continue
            cp = pltpu.make_async_remote_copy(src_ref=src, dst_ref=dst, send_sem=s_sems.at[k],
                                              recv_sem=r_sems.at[k], device_id=dev, device_id_type=MESH)
            cp.wait_send()
            cp.wait_recv()

    bufs = list(srcs) + list(lands)
    outs = pl.pallas_call(
        body, name=name,
        out_shape=tuple(pltpu.HBM(a.shape, a.dtype) for a in bufs),
        in_specs=[_HBM] * n + [_SEM, _SEM] + [pl.BlockSpec(memory_space=pl.ANY)] * len(after),
        out_specs=tuple([_HBM] * n),
        input_output_aliases={i: i for i in range(n)},
        compiler_params=pltpu.CompilerParams(has_side_effects=_EFFECT),
    )(*bufs, send_sems, recv_sems, *after)
    return list(outs[:len(srcs)]), list(outs[len(srcs):])


def _with_own_slot(block, slot, nslots=NCHIP):
    land = lax.empty((nslots,) + block.shape, block.dtype)
    return lax.dynamic_update_slice(land, block[None], (slot,) + (0,) * block.ndim)


def _my_half(land, slot, c):
    half = land.shape[1] // 2
    return land.at[slot, pl.ds(pl.multiple_of(c * half, 16), half)]


def _gather_ici_plan(srcs, lands):
    x, y, c, s = _place()
    return [(_my_half(land, s, c), _my_half(land, s, c), (*_chip_peer(x, y, j), c))
            for land in lands for j in range(1, NCHIP)]


def _gather_d2d_plan(srcs, lands):
    x, y, c, s = _place()
    return [(_my_half(land, s ^ j, c), _my_half(land, s ^ j, c), (x, y, 1 - c))
            for land in lands for j in range(1, NCHIP)]


def _exchange_plan(srcs, lands):
    x, y, c, s = _place()
    return [(src.at[s ^ j], land.at[s], (*_chip_peer(x, y, j), c) if j else None)
            for src, land in zip(srcs, lands) for j in range(NCHIP)]


def _sibling_plan(srcs, lands):
    x, y, c, _ = _place()
    return [(src, land, (x, y, 1 - c)) for src, land in zip(srcs, lands)]


def _everyone_plan(srcs, lands):
    x, y, c, _ = _place()
    me = 4 * x + 2 * y + c
    return [(srcs[0], lands[0].at[me], (*_chip_peer(x, y, j >> 1), 1 - c if j & 1 else c)) for j in range(1, 8)]


def _flatten_small(parts):
    flat = jnp.concatenate([p.reshape(-1) for p in parts])
    n = flat.shape[0]
    rows = -(-n // (128 * SMALL_TILE)) * SMALL_TILE
    return jnp.pad(flat, (0, rows * 128 - n)).reshape(rows, 128)


def _unflatten_small(buf, like):
    flat = buf.reshape(-1)
    out, at = [], 0
    for p in like:
        out.append(flat[at:at + p.size].reshape(p.shape))
        at += p.size
    return out


def kernel(x, pre_norm_g, w_in, lambda_re, lambda_im, log_dt, b_re, b_im, c_re, c_im, d_skip, w_glu, b_glu, w_branch_s, w_branch_a, w_out, post_norm_g, loss_target, m_pre_norm_g, m_w_in, m_lambda_re, m_lambda_im, m_log_dt, m_b_re, m_b_im, m_c_re, m_c_im, m_d_skip, m_w_glu, m_b_glu, m_w_branch_s, m_w_branch_a, m_w_out, m_post_norm_g, v_pre_norm_g, v_w_in, v_lambda_re, v_lambda_im, v_log_dt, v_b_re, v_b_im, v_c_re, v_c_im, v_d_skip, v_w_glu, v_b_glu, v_w_branch_s, v_w_branch_a, v_w_out, v_post_norm_g):
    nb = DEPTH * NCB
    lre_c = lambda_re.reshape(nb, 1, HW)
    lim_c = lambda_im.reshape(nb, 1, HW)
    ldt_c = jnp.broadcast_to(log_dt[:, :, None], (DEPTH, 32, 64)).reshape(nb, 1, HW)
    b_rows = lambda t: t.reshape(DEPTH, NCB, 8, 64, 16).transpose(0, 1, 4, 2, 3).reshape(nb, 16, HW)
    c_rows = lambda t: t.reshape(DEPTH, NCB, 8, 16, 64).transpose(0, 1, 3, 2, 4).reshape(nb, 16, HW)
    bre_c, bim_c = b_rows(b_re), b_rows(b_im)
    ar, ai, wb, wct = _s5_prep(lre_c, lim_c, ldt_c, bre_c, bim_c, c_rows(c_re), c_rows(c_im))
    ar4 = ar.reshape(DEPTH, NCB, 1, HW)
    ai4 = ai.reshape(DEPTH, NCB, 1, HW)
    wb = wb.reshape(DEPTH, NCB, 128, SW)
    wct = wct.reshape(DEPTH, NCB, 128, SW)

    chip = 2 * lax.axis_index("x") + lax.axis_index("y")
    device = 2 * chip + lax.axis_index("c")

    def gather_start(l, dep):
        blocks = [w_in[l], w_glu[l], w_branch_s[l], w_branch_a[l], w_out[l]]
        if l == 0:
            blocks = lax.optimization_barrier(blocks)
        lands = [_with_own_slot(a.astype(BF16), chip) for a in blocks]
        return _copies_start(f"gather_start_{l}", _gather_ici_plan, 15, [], lands, dep)

    def gather_forward(l, flight, after):
        send, recv, _, lands, _ = flight
        _, lands = _copies_wait(f"gather_wait_{l}", _gather_ici_plan, send, recv, [], lands, after)
        return _copies_start(f"forward_start_{l}", _gather_d2d_plan, 15, [], lands)

    def gather_done(l, forward, after):
        send, recv, _, lands, _ = forward
        return _copies_wait(f"forward_wait_{l}", _gather_d2d_plan, send, recv, [], lands, after)[1]

    small_w = [pre_norm_g, lambda_re, lambda_im, log_dt, b_re, b_im, c_re, c_im, d_skip, b_glu, post_norm_g]
    small_m = [m_pre_norm_g, m_lambda_re, m_lambda_im, m_log_dt, m_b_re, m_b_im, m_c_re, m_c_im, m_d_skip,
               m_b_glu, m_post_norm_g]
    small_v = [v_pre_norm_g, v_lambda_re, v_lambda_im, v_log_dt, v_b_re, v_b_im, v_c_re, v_c_im, v_d_skip,
               v_b_glu, v_post_norm_g]
    flat_w, flat_m, flat_v = _flatten_small(small_w), _flatten_small(small_m), _flatten_small(small_v)

    xs = [x[0]]
    saved = []
    forward = gather_forward(0, gather_start(0, None), [xs[0], flat_w, flat_m, flat_v, ar, ai, wb, wct])
    for l in range(DEPTH):
        w4, wglu4, wbs4, wba4, wout4 = gather_done(l, forward, [xs[l]])
        started = None
        if l + 1 < DEPTH:
            flight = gather_start(l + 1, wglu4)
            started = flight[4]
        wglu = wglu4.reshape(SSM, SSM)
        wout = wout4.reshape(D, D)
        wb4, wc4 = wb[l], wct[l]
        g1 = pre_norm_g[l].reshape(1, D)
        g2 = post_norm_g[l].reshape(1, D)
        dsk = d_skip[l].reshape(1, SSM)
        bgl = b_glu[l].reshape(1, SSM)
        proj, ht = _rms_proj_fwd(xs[l], g1, w4, started)
        states, yraw = _s5_scan_fwd(proj, wb4, wc4, ar4[l], ai4[l])
        o, lse = _attn_core_fwd(proj)
        forwarded = None
        if l + 1 < DEPTH:
            forward = gather_forward(l + 1, flight, [yraw, o])
            forwarded = forward[4]
        ys = _s5_tail_fwd(yraw, proj, dsk, wglu, bgl)
        ya = _attn_mix_fwd(o, lse, proj)
        xs.append(_merge_out_fwd(ys, ya, proj, wbs4, wba4, wout, g2, xs[l], forwarded))
        saved.append((w4, wglu, wbs4, wba4, wout, wb4, wc4, g1, g2, dsk, bgl, proj, ht, states, yraw, ys, ya, o, lse))

    loss_part, dx = _loss_head(xs[DEPTH], loss_target[0])
    loss = lax.psum(loss_part[0, 0], ("x", "y", "c"))

    big_w = (w_in, w_glu.reshape(DEPTH, 128, SSM), w_branch_s, w_branch_a, w_out)
    big_m = (m_w_in, m_w_glu, m_w_branch_s, m_w_branch_a, m_w_out)
    big_v = (v_w_in, v_w_glu, v_w_branch_s, v_w_branch_a, v_w_out)
    big_out = [None] * 5
    small = {k: [None] * DEPTH for k in ("g1", "da", "dbb", "dcc", "dsk", "bgl", "g2")}
    ici = [None] * DEPTH
    d2d = [None] * DEPTH

    def exchange_start(l, parts):
        lands = [lax.empty(p.shape, p.dtype) for p in parts]
        ici[l] = _copies_start(f"exchange_start_{l}", _exchange_plan, 20, parts, lands)

    def handoff_start(l, after):
        send, recv, srcs, lands, _ = ici[l]
        _, own = _copies_wait(f"exchange_wait_{l}", _exchange_plan, send, recv, srcs, lands, after)
        d2d[l] = _copies_start(f"handoff_start_{l}", _sibling_plan, 5, own,
                               [lax.empty(a.shape, a.dtype) for a in own])

    def update(l, after):
        send, recv, own, lands, _ = d2d[l]
        own, sib = _copies_wait(f"handoff_wait_{l}", _sibling_plan, send, recv, own, lands, after)
        for k in range(5):
            big_out[k] = _adamw_big(l, big_w[k], big_m[k], big_v[k], own[k], sib[k], big_out[k])

    for l in reversed(range(DEPTH)):
        w4, wglu, wbs4, wba4, wout, wb4, wc4, g1, g2, dsk, bgl, proj, ht, states, yraw, ys, ya, o, lse = saved[l]
        started = None
        if l + 1 < DEPTH:
            started = ici[l + 1][4] + d2d[l + 2][4] if l + 2 < DEPTH else ici[l + 1][4]
        dys, dya, dproj, dwbs, dwba, dwout, dg2 = _merge_out_bwd(dx, ys, ya, proj, wbs4, wba4, wout, g2, started)
        dyraw, du_skip, dproj, dwglu, dbgl, ddsk = _s5_tail_bwd(dys, yraw, proj, dsk, wglu, bgl, dproj)
        dproj, dbb, dcc, da = _s5_scan_bwd(dyraw, states, proj, du_skip, wb4, wc4, ar4[l], ai4[l], dproj)
        do, cterm, dproj = _attn_mix_bwd(dya, o, lse, proj, dproj)
        dproj = _attn_core_bwd(proj, do, lse, cterm, dproj)
        dh, dwin = _proj_bwd(dproj, ht, w4)
        parts = [dwin, dwglu.reshape(NCHIP, 128, SSM), dwbs, dwba, dwout.reshape(NCHIP, D // NCHIP, D)]
        exchange_start(l, parts)
        dx, dg1 = _rms_bwd(dh, xs[l], g1, dx, ici[l][4])
        if l + 1 < DEPTH:
            handoff_start(l + 1, [ici[l][4]])
        small["g1"][l], small["g2"][l], small["dsk"][l], small["bgl"][l] = dg1, dg2, ddsk, dbgl
        small["da"][l], small["dbb"][l], small["dcc"][l] = da, dbb, dcc

    dlre, dlim, dldt, dbre_c, dbim_c = _s5_prep_bwd(
        lre_c, lim_c, ldt_c, bre_c, bim_c, jnp.stack(small["da"]).reshape(nb, 1, SW),
        jnp.stack(small["dbb"]).reshape(nb, 16, SW))
    dcc = jnp.stack(small["dcc"]).reshape(nb, 16, SW)
    b_back = lambda t: t.reshape(DEPTH, NCB, 16, 8, 64).transpose(0, 1, 3, 4, 2).reshape(DEPTH, 32, 64, 16)
    c_back = lambda t: t.reshape(DEPTH, NCB, 16, 8, 64).transpose(0, 1, 3, 2, 4).reshape(DEPTH, 32, 16, 64)
    small_g = [jnp.stack(small["g1"]).reshape(DEPTH, D), dlre.reshape(DEPTH, 32, 64), dlim.reshape(DEPTH, 32, 64),
               dldt.reshape(DEPTH, 32, 64).sum(-1), b_back(dbre_c), b_back(dbim_c), c_back(dcc[:, :, :HW]),
               c_back(dcc[:, :, HW:]),
               jnp.stack(small["dsk"]).reshape(DEPTH, SSM), jnp.stack(small["bgl"]).reshape(DEPTH, SSM),
               jnp.stack(small["g2"]).reshape(DEPTH, D)]
    part = _flatten_small(small_g).astype(BF16)
    send, recv, srcs, lands, token = _copies_start("small_start", _everyone_plan, 7, [part],
                                                   [_with_own_slot(part, device, 8)])
    update(DEPTH - 1, [token, ici[0][4]])
    for l in range(DEPTH - 2, 0, -1):
        update(l, [out[0] for out in big_out])
    handoff_start(0, [out[0] for out in big_out])
    _, (parts8,) = _copies_wait("small_wait", _everyone_plan, send, recv, srcs, lands, [d2d[0][4]])
    g_flat, d_flat, nm_flat, nv_flat = _adamw_small(flat_w, parts8, flat_m, flat_v)
    sg = _unflatten_small(g_flat, small_w)
    sd = _unflatten_small(d_flat, small_w)
    snm = _unflatten_small(nm_flat, small_w)
    snv = _unflatten_small(nv_flat, small_w)
    update(0, sg + sd + snm + snv)

    def ordered(sm, which):
        bg = [big_out[k][which] for k in range(5)]
        bg[1] = bg[1].reshape(DEPTH, 128, SSM)
        return [sm[0], bg[0], sm[1], sm[2], sm[3], sm[4], sm[5], sm[6], sm[7], sm[8], bg[1], sm[9], bg[2], bg[3],
                bg[4], sm[10]]

    return (loss, dx[None], *ordered(sg, 0), *ordered(sd, 1), *ordered(snm, 2), *ordered(snv, 3))
```

```python
import math

import jax
import jax.numpy as jnp
from jax import lax
from jax.experimental import pallas as pl
from jax.experimental.pallas import tpu as pltpu

F32 = jnp.float32
BF16 = jnp.bfloat16
MESH = pl.DeviceIdType.MESH

DEPTH = 4
L = 2048
D = 1024
NCOL = 8192
SSM = 512
AW = 512
QKV = 1536
RMS_EPS = 1e-6
NCHIP = 4

COL_U, COL_ZS, COL_Q, COL_K, COL_V, COL_ZA, COL_GS, COL_GA = 0, 4, 8, 20, 32, 44, 48, 56

NSEG = 8
SEG = L // NSEG
NCB = 4
HW = 512
SW = 2 * HW

HEAD = 128
DILATIONS = (1, 4, 16)
SCALE = HEAD ** -0.5
NEG = -1e30

ADAM_LR, ADAM_B1, ADAM_B2, ADAM_EPS, ADAM_WD, ADAM_STEP = 0.001, 0.9, 0.999, 1e-08, 0.01, 10


def _sds(shape, dtype=F32):
    return jax.ShapeDtypeStruct(shape, dtype)


def _pcall(body, *, name, out_shape, grid=None, in_specs=None, out_specs=None, scratch=(), vmem_mb=None,
           aliases=None):
    params = {}
    if vmem_mb is not None:
        params["vmem_limit_bytes"] = vmem_mb << 20
    kw = {}
    if grid is not None:
        kw["grid"] = grid
    if in_specs is not None:
        kw["in_specs"] = in_specs
    if out_specs is not None:
        kw["out_specs"] = out_specs
    return pl.pallas_call(body, name=name, out_shape=out_shape, scratch_shapes=list(scratch),
                          compiler_params=pltpu.CompilerParams(**params),
                          input_output_aliases=aliases or {}, **kw)


def _dot(a, b):
    return jnp.dot(a, b, preferred_element_type=F32)


def _dot_nt(a, b):
    return lax.dot_general(a, b, (((1,), (1,)), ((), ())), preferred_element_type=F32)


def _dot_tn(a, b):
    return lax.dot_general(a, b, (((0,), (0,)), ((), ())), preferred_element_type=F32)


def _sigmoid(x):
    return jax.nn.sigmoid(x)


_GELU_K = math.sqrt(2.0 / math.pi)


def _gelu(x):
    return 0.5 * x * (1.0 + jnp.tanh(_GELU_K * (x + 0.044715 * (x * x * x))))


def _gelu_grad(x):
    t = jnp.tanh(_GELU_K * (x + 0.044715 * (x * x * x)))
    return 0.5 * (1.0 + t) + 0.5 * x * (1.0 - t * t) * (_GELU_K * (1.0 + 3.0 * 0.044715 * (x * x)))


def _token_operand(dep):
    if dep is None:
        return [], []
    return [dep], [pl.BlockSpec(memory_space=pl.ANY)]


def _rms_proj_fwd(x, g, w4, dep=None):
    tl, tn = 512, 1024
    ni = L // tl
    deps, dep_specs = _token_operand(dep)

    def body(x_ref, g_ref, w_ref, *rest):
        o_ref, h_ref = rest[len(deps):]
        rows = pl.ds(pl.multiple_of(pl.program_id(1) * tl, tl), tl)

        @pl.when(pl.program_id(0) == 0)
        def _():
            xx = x_ref[...]
            inv = lax.rsqrt(jnp.mean(xx * xx, axis=-1, keepdims=True) + RMS_EPS)
            h_ref[rows, :] = (xx * inv * g_ref[...]).astype(BF16)

        o_ref[...] = _dot(h_ref[rows, :], w_ref[...])

    proj, h = _pcall(
        body, name="rms_proj_fwd", grid=(NCOL // tn, ni),
        in_specs=[pl.BlockSpec((tl, D), lambda j, i: (jnp.where(j == 0, i, ni - 1), 0)),
                  pl.BlockSpec((1, D), lambda j, i: (0, 0)),
                  pl.BlockSpec((None, D, tn), lambda j, i: (lax.div(j, 2), 0, lax.rem(j, 2)))] + dep_specs,
        out_specs=[pl.BlockSpec((tl, tn), lambda j, i: (i, j)), pl.BlockSpec((L, D), lambda j, i: (0, 0))],
        out_shape=[_sds((L, NCOL)), _sds((L, D), BF16)], vmem_mb=48)(x, g, w4, *deps)

    def transpose(h_ref, ht_ref):
        ht_ref[...] = h_ref[...].astype(F32).T.astype(BF16)

    ht = _pcall(transpose, name="transpose_h", grid=(ni,), in_specs=[pl.BlockSpec((tl, D), lambda i: (i, 0))],
                out_specs=pl.BlockSpec((D, tl), lambda i: (0, i)), out_shape=_sds((D, L), BF16), vmem_mb=40)(h)
    return proj, ht


def _proj_bwd(dproj, ht, w4):
    tn = 1024

    def body(dp_ref, ht_ref, w_ref, dh_ref, dw_ref):
        @pl.when(pl.program_id(0) == 0)
        def _():
            dh_ref[...] = jnp.zeros((L, D), F32)

        dw_ref[...] = _dot(ht_ref[...], dp_ref[...]).astype(BF16)
        dh_ref[...] += _dot_nt(dp_ref[...], w_ref[...])

    return _pcall(
        body, name="proj_bwd", grid=(NCOL // tn,),
        in_specs=[pl.BlockSpec((L, tn), lambda j: (0, j)),
                  pl.BlockSpec((D, L), lambda j: (0, 0)),
                  pl.BlockSpec((None, D, tn), lambda j: (lax.div(j, 2), 0, lax.rem(j, 2)))],
        out_specs=[pl.BlockSpec((L, D), lambda j: (0, 0)),
                   pl.BlockSpec((None, D, tn), lambda j: (lax.div(j, 2), 0, lax.rem(j, 2)))],
        out_shape=[_sds((L, D)), _sds((NCHIP, D, NCOL // NCHIP), BF16)], vmem_mb=56)(dproj, ht, w4)


def _rms_bwd(dh, x, g, dxn, dep=None):
    tl = 512
    deps, dep_specs = _token_operand(dep)

    def body(dh_ref, x_ref, g_ref, dxn_ref, *rest):
        dx_ref, dg_ref = rest[len(deps):]
        i = pl.program_id(0)
        xx = x_ref[...]
        inv = lax.rsqrt(jnp.mean(xx * xx, axis=-1, keepdims=True) + RMS_EPS)
        nrm = xx * inv
        dh_v = dh_ref[...]
        dn = dh_v * g_ref[...]
        dx_ref[...] = dxn_ref[...] + inv * (dn - nrm * jnp.mean(dn * nrm, axis=-1, keepdims=True))
        part = jnp.sum(dh_v * nrm, axis=0, keepdims=True)

        @pl.when(i == 0)
        def _():
            dg_ref[...] = part

        @pl.when(i > 0)
        def _():
            dg_ref[...] += part

    row = pl.BlockSpec((tl, D), lambda i: (i, 0))
    vec = pl.BlockSpec((1, D), lambda i: (0, 0))
    return _pcall(body, name="rms_bwd", grid=(L // tl,), in_specs=[row, row, vec, row] + dep_specs,
                  out_specs=[row, vec], out_shape=[_sds((L, D)), _sds((1, D))], vmem_mb=40)(dh, x, g, dxn, *deps)


def _s5_param_math(lre, lim, ldt, bre, bim):
    lr = jnp.minimum(lre, -1e-4)
    dt = jnp.exp(ldt)
    mag = jnp.exp(lr * dt)
    ar = mag * jnp.cos(lim * dt)
    ai = mag * jnp.sin(lim * dt)
    den = lr * lr + lim * lim
    nr = ar - 1.0
    qr = (nr * lr + ai * lim) / den
    qi = (ai * lr - nr * lim) / den
    return ar, ai, qr * bre - qi * bim, qr * bim + qi * bre


def _group_of_lane(width):
    lane = lax.broadcasted_iota(jnp.int32, (16, width), 1)
    return lax.shift_right_logical(jnp.bitwise_and(lane, HW - 1), 6)


def _vec_spec():
    return pl.BlockSpec((None, 1, HW), lambda n: (n, 0, 0))


def _mat_spec():
    return pl.BlockSpec((None, 16, HW), lambda n: (n, 0, 0))


def _s5_prep(lre, lim, ldt, bre, bim, cre, cim):
    n = lre.shape[0]

    def body(lre_ref, lim_ref, ldt_ref, bre_ref, bim_ref, cre_ref, cim_ref, ar_ref, ai_ref, wb_ref, wct_ref):
        ar, ai, bbr, bbi = _s5_param_math(lre_ref[...], lim_ref[...], ldt_ref[...], bre_ref[...], bim_ref[...])
        ar_ref[...] = ar
        ai_ref[...] = ai
        group = _group_of_lane(SW)
        bb = jnp.concatenate([bbr, bbi], axis=1)
        cc = jnp.concatenate([cre_ref[...], -cim_ref[...]], axis=1)
        for g in range(8):
            rows = pl.ds(16 * g, 16)
            wb_ref[rows, :] = jnp.where(group == g, bb, 0.0).astype(BF16)
            wct_ref[rows, :] = jnp.where(group == g, cc, 0.0).astype(BF16)

    wide = pl.BlockSpec((None, 128, SW), lambda n: (n, 0, 0))
    return _pcall(body, name="s5_prep", grid=(n,),
                  in_specs=[_vec_spec()] * 3 + [_mat_spec()] * 4,
                  out_specs=[_vec_spec(), _vec_spec(), wide, wide],
                  out_shape=[_sds((n, 1, HW)), _sds((n, 1, HW)), _sds((n, 128, SW), BF16), _sds((n, 128, SW), BF16)])(
                      lre, lim, ldt, bre, bim, cre, cim)


def _s5_prep_bwd(lre, lim, ldt, bre, bim, ga, gbb):
    n = lre.shape[0]

    def body(lre_ref, lim_ref, ldt_ref, bre_ref, bim_ref, ga_ref, gbb_ref,
             dlre_ref, dlim_ref, dldt_ref, dbre_ref, dbim_ref):
        _, vjp = jax.vjp(_s5_param_math, lre_ref[...], lim_ref[...], ldt_ref[...], bre_ref[...], bim_ref[...])
        ga, gbb = ga_ref[...], gbb_ref[...]
        dlre, dlim, dldt, dbre, dbim = vjp((ga[:, :HW], ga[:, HW:], gbb[:, :HW], gbb[:, HW:]))
        dlre_ref[...] = dlre
        dlim_ref[...] = dlim
        dldt_ref[...] = dldt
        dbre_ref[...] = dbre
        dbim_ref[...] = dbim

    return _pcall(body, name="s5_prep_bwd", grid=(n,),
                  in_specs=[_vec_spec()] * 3 + [_mat_spec()] * 2 +
                  [pl.BlockSpec((None, 1, SW), lambda n: (n, 0, 0)), pl.BlockSpec((None, 16, SW), lambda n: (n, 0, 0))],
                  out_specs=[_vec_spec()] * 3 + [_mat_spec()] * 2,
                  out_shape=[_sds((n, 1, HW))] * 3 + [_sds((n, 16, HW))] * 2)(lre, lim, ldt, bre, bim, ga, gbb)


def _diag_blocks(full):
    group = _group_of_lane(SW)
    out = jnp.where(group == 0, full[0:16, :], 0.0)
    for g in range(1, 8):
        out = out + jnp.where(group == g, full[16 * g:16 * g + 16, :], 0.0)
    return out


def _permute_rows(dst_ref, src_ref):
    for s in range(NSEG):
        dst_ref[pl.ds(s, SEG, stride=NSEG), :] = src_ref[pl.ds(s * SEG, SEG), :]


def _unpermute_rows(src_ref, s):
    return src_ref[pl.ds(s, SEG, stride=NSEG), :]


def _row_of(slab, rows, j):
    return jnp.sum(jnp.where(rows == j, slab, 0.0), axis=0, keepdims=True)


def _s5_scan_fwd(proj, wb4, wc4, ar4, ai4):
    def body(u_ref, wb_ref, wc_ref, ar_ref, ai_ref, x_ref, y_ref, up_ref, d_ref, yp_ref):
        _permute_rows(up_ref, u_ref)
        d_ref[...] = _dot(up_ref[...].astype(BF16), wb_ref[...])
        ar = jnp.broadcast_to(ar_ref[...], (NSEG, HW))
        ai = jnp.broadcast_to(ai_ref[...], (NSEG, HW))
        zero = jnp.zeros((NSEG, HW), F32)

        def drive(t):
            row = pl.multiple_of(t * NSEG, NSEG)
            dd = d_ref[pl.ds(row, NSEG), :]
            return row, dd[:, :HW], dd[:, HW:]

        def local_step(t, c):
            xr, xi = c
            _, dr, di = drive(t)
            return ar * xr - ai * xi + dr, ar * xi + ai * xr + di

        fr, fi = lax.fori_loop(0, SEG, local_step, (zero, zero), unroll=8)
        pr, pi_ = ar_ref[...], ai_ref[...]
        for _ in range(int(math.log2(SEG))):
            pr, pi_ = pr * pr - pi_ * pi_, 2.0 * pr * pi_
        rows = lax.broadcasted_iota(jnp.int32, (NSEG, HW), 0)
        cr, ci = zero, zero
        for j in range(NSEG - 1):
            sr, si = _row_of(cr, rows, j), _row_of(ci, rows, j)
            gr, gi = _row_of(fr, rows, j), _row_of(fi, rows, j)
            nr = pr * sr - pi_ * si + gr
            ni = pr * si + pi_ * sr + gi
            cr = jnp.where(rows == j + 1, nr, cr)
            ci = jnp.where(rows == j + 1, ni, ci)

        def true_step(t, c):
            xr, xi = c
            row, dr, di = drive(t)
            nr = ar * xr - ai * xi + dr
            ni = ar * xi + ai * xr + di
            x_ref[pl.ds(row, NSEG), :] = jnp.concatenate([nr, ni], axis=1)
            return nr, ni

        lax.fori_loop(0, SEG, true_step, (cr, ci), unroll=8)
        yp_ref[...] = _dot_nt(x_ref[...].astype(BF16), wc_ref[...])
        for s in range(NSEG):
            y_ref[pl.ds(s * SEG, SEG), :] = _unpermute_rows(yp_ref, s)

    return _pcall(
        body, name="s5_scan_fwd", grid=(NCB,),
        in_specs=[pl.BlockSpec((L, 128), lambda b: (0, COL_U + b)),
                  pl.BlockSpec((None, 128, SW), lambda b: (b, 0, 0)),
                  pl.BlockSpec((None, 128, SW), lambda b: (b, 0, 0)),
                  pl.BlockSpec((None, 1, HW), lambda b: (b, 0, 0)),
                  pl.BlockSpec((None, 1, HW), lambda b: (b, 0, 0))],
        out_specs=[pl.BlockSpec((None, L, SW), lambda b: (b, 0, 0)),
                   pl.BlockSpec((L, 128), lambda b: (0, b))],
        out_shape=[_sds((NCB, L, SW)), _sds((L, SSM))],
        scratch=[pltpu.VMEM((L, 128), F32), pltpu.VMEM((L, SW), F32), pltpu.VMEM((L, 128), F32)],
        vmem_mb=56)(proj, wb4, wc4, ar4, ai4)


def _s5_scan_bwd(dy, xs, proj, du_skip, wb4, wc4, ar4, ai4, dproj):
    def body(dy_ref, x_ref, u_ref, dus_ref, wb_ref, wc_ref, ar_ref, ai_ref, _,
             du_ref, dbb_ref, dcc_ref, da_ref, dyp_ref, up_ref, g_ref, dup_ref):
        _permute_rows(dyp_ref, dy_ref)
        _permute_rows(up_ref, u_ref)
        dyp = dyp_ref[...].astype(BF16)
        g_ref[...] = _dot(dyp, wc_ref[...])
        dcc = _diag_blocks(_dot_tn(dyp, x_ref[...].astype(BF16)))
        dcc_ref[...] = jnp.concatenate([dcc[:, :HW], -dcc[:, HW:]], axis=1)
        ar = jnp.broadcast_to(ar_ref[...], (NSEG, HW))
        ai = jnp.broadcast_to(ai_ref[...], (NSEG, HW))
        zero = jnp.zeros((NSEG, HW), F32)

        def load(t):
            row = pl.multiple_of(t * NSEG, NSEG)
            gg = g_ref[pl.ds(row, NSEG), :]
            return row, gg[:, :HW], gg[:, HW:]

        def local_step(k, c):
            lr, li = c
            _, gr, gi = load(SEG - 1 - k)
            return ar * lr + ai * li + gr, ar * li - ai * lr + gi

        fr, fi = lax.fori_loop(0, SEG, local_step, (zero, zero), unroll=8)
        pr, pi_ = ar_ref[...], -ai_ref[...]
        for _ in range(int(math.log2(SEG))):
            pr, pi_ = pr * pr - pi_ * pi_, 2.0 * pr * pi_
        rows = lax.broadcasted_iota(jnp.int32, (NSEG, HW), 0)
        cr, ci = zero, zero
        for j in range(NSEG - 1, 0, -1):
            sr, si = _row_of(cr, rows, j), _row_of(ci, rows, j)
            gr, gi = _row_of(fr, rows, j), _row_of(fi, rows, j)
            nr = pr * sr - pi_ * si + gr
            ni = pr * si + pi_ * sr + gi
            cr = jnp.where(rows == j - 1, nr, cr)
            ci = jnp.where(rows == j - 1, ni, ci)

        def true_step(k, c):
            lr, li, gar, gai = c
            t = SEG - 1 - k
            row, gr, gi = load(t)
            nr = ar * lr + ai * li + gr
            ni = ar * li - ai * lr + gi
            g_ref[pl.ds(row, NSEG), :] = jnp.concatenate([nr, ni], axis=1)
            prow = pl.multiple_of(jnp.maximum(t - 1, 0) * NSEG, NSEG)
            xp = x_ref[pl.ds(prow, NSEG), :]
            live = (t > 0).astype(F32)
            xr, xi = xp[:, :HW] * live, xp[:, HW:] * live
            return nr, ni, gar + (nr * xr + ni * xi), gai + (ni * xr - nr * xi)

        _, _, gar, gai = lax.fori_loop(0, SEG, true_step, (cr, ci, zero, zero), unroll=4)
        l0 = g_ref[1:NSEG, :]
        xl = x_ref[(SEG - 1) * NSEG:(SEG - 1) * NSEG + NSEG - 1, :]
        l0r, l0i, xlr, xli = l0[:, :HW], l0[:, HW:], xl[:, :HW], xl[:, HW:]
        gar_t = jnp.sum(gar, axis=0, keepdims=True) + jnp.sum(l0r * xlr + l0i * xli, axis=0, keepdims=True)
        gai_t = jnp.sum(gai, axis=0, keepdims=True) + jnp.sum(l0i * xlr - l0r * xli, axis=0, keepdims=True)
        da_ref[...] = jnp.concatenate([gar_t, gai_t], axis=1)
        lam = g_ref[...].astype(BF16)
        dbb_ref[...] = _diag_blocks(_dot_tn(up_ref[...].astype(BF16), lam))
        dup_ref[...] = _dot_nt(lam, wb_ref[...])
        for s in range(NSEG):
            sl = pl.ds(s * SEG, SEG)
            du_ref[sl, :] = (_unpermute_rows(dup_ref, s) + dus_ref[sl, :]).astype(BF16)

    col = lambda off: pl.BlockSpec((L, 128), lambda b: (0, off + b))
    return _pcall(
        body, name="s5_scan_bwd", grid=(NCB,),
        in_specs=[col(0), pl.BlockSpec((None, L, SW), lambda b: (b, 0, 0)), col(COL_U), col(0),
                  pl.BlockSpec((None, 128, SW), lambda b: (b, 0, 0)),
                  pl.BlockSpec((None, 128, SW), lambda b: (b, 0, 0)),
                  pl.BlockSpec((None, 1, HW), lambda b: (b, 0, 0)),
                  pl.BlockSpec((None, 1, HW), lambda b: (b, 0, 0)),
                  pl.BlockSpec(memory_space=pl.ANY)],
        out_specs=[col(COL_U), pl.BlockSpec((None, 16, SW), lambda b: (b, 0, 0)),
                   pl.BlockSpec((None, 16, SW), lambda b: (b, 0, 0)),
                   pl.BlockSpec((None, 1, SW), lambda b: (b, 0, 0))],
        out_shape=[_sds((L, NCOL), BF16), _sds((NCB, 16, SW)), _sds((NCB, 16, SW)), _sds((NCB, 1, SW))],
        scratch=[pltpu.VMEM((L, 128), F32), pltpu.VMEM((L, 128), F32), pltpu.VMEM((L, SW), F32),
                 pltpu.VMEM((L, 128), F32)],
        aliases={8: 0}, vmem_mb=56)(dy, xs, proj, du_skip, wb4, wc4, ar4, ai4, dproj)


def _s5_tail_fwd(yraw, proj, dsk, wglu, bglu):
    tl = 512

    def body(y_ref, u_ref, z_ref, dsk_ref, w_ref, b_ref, o_ref):
        y1 = y_ref[...] + dsk_ref[...] * u_ref[...]
        y2 = _gelu(y1)
        gl = _dot(y2.astype(BF16), w_ref[...]) + b_ref[...]
        z = z_ref[...]
        o_ref[...] = (y2 * _sigmoid(gl)) * (z * _sigmoid(z))

    blk = lambda c: pl.BlockSpec((tl, SSM), lambda i: (i, c))
    vec = pl.BlockSpec((1, SSM), lambda i: (0, 0))
    return _pcall(body, name="s5_tail_fwd", grid=(L // tl,),
                  in_specs=[blk(0), blk(0), blk(1), vec, pl.BlockSpec((SSM, SSM), lambda i: (0, 0)), vec],
                  out_specs=blk(0), out_shape=_sds((L, SSM)), vmem_mb=40)(yraw, proj, proj, dsk, wglu, bglu)


def _s5_tail_bwd(dys, yraw, proj, dsk, wglu, bglu, dproj):
    tl = 512
    ni = L // tl

    def body(dys_ref, y_ref, u_ref, z_ref, dsk_ref, w_ref, b_ref, _,
             dy_ref, dus_ref, dz_ref, dw_ref, db_ref, dd_ref, acc_ref):
        i = pl.program_id(0)
        u = u_ref[...]
        y1 = y_ref[...] + dsk_ref[...] * u
        y2 = _gelu(y1)
        y2b = y2.astype(BF16)
        sg = _sigmoid(_dot(y2b, w_ref[...]) + b_ref[...])
        y3 = y2 * sg
        z = z_ref[...]
        sz = _sigmoid(z)
        dys = dys_ref[...]
        dy3 = dys * (z * sz)
        dz_ref[...] = (dys * y3 * (sz * (1.0 + z * (1.0 - sz)))).astype(BF16)
        dgl = (dy3 * y2) * (sg * (1.0 - sg))
        dglb = dgl.astype(BF16)
        dy2 = dy3 * sg + _dot_nt(dglb, w_ref[...])
        dy1 = dy2 * _gelu_grad(y1)
        dy_ref[...] = dy1
        dus_ref[...] = dsk_ref[...] * dy1
        dw = _dot_tn(y2b, dglb)
        db = jnp.sum(dgl, axis=0, keepdims=True)
        dd = jnp.sum(dy1 * u, axis=0, keepdims=True)

        @pl.when(i == 0)
        def _():
            acc_ref[...] = dw
            db_ref[...] = db
            dd_ref[...] = dd

        @pl.when(i > 0)
        def _():
            acc_ref[...] += dw
            db_ref[...] += db
            dd_ref[...] += dd

        @pl.when(i == ni - 1)
        def _():
            dw_ref[...] = acc_ref[...].astype(BF16)

    blk = lambda c: pl.BlockSpec((tl, SSM), lambda i: (i, c))
    vec = pl.BlockSpec((1, SSM), lambda i: (0, 0))
    mat = pl.BlockSpec((SSM, SSM), lambda i: (0, 0))
    return _pcall(body, name="s5_tail_bwd", grid=(ni,),
                  in_specs=[blk(0), blk(0), blk(0), blk(1), vec, mat, vec, pl.BlockSpec(memory_space=pl.ANY)],
                  out_specs=[blk(0), blk(0), blk(COL_ZS // 4), mat, vec, vec],
                  out_shape=[_sds((L, SSM)), _sds((L, SSM)), _sds((L, NCOL), BF16), _sds((SSM, SSM), BF16),
                             _sds((1, SSM)), _sds((1, SSM))],
                  scratch=[pltpu.VMEM((SSM, SSM), F32)], aliases={7: 2},
                  vmem_mb=40)(dys, yraw, proj, proj, dsk, wglu, bglu, dproj)


def _attn_blocks(dil):
    n = L // dil
    return [(i * HEAD * dil + r, r * n + i * HEAD, i == 0) for r in range(dil) for i in range(n // HEAD)]


def _rows(start, dil):
    return pl.ds(start, HEAD) if dil == 1 else pl.ds(start, HEAD, stride=dil)


def _residue_rows(r, dil):
    n = L // dil
    return pl.ds(0, L) if dil == 1 else pl.ds(r, n, stride=dil)


def _gather_residues(pairs, dil, dtype=BF16):
    n = L // dil
    for src, dst in pairs:
        for r in range(dil):
            dst[pl.ds(r * n, n), :] = src[_residue_rows(r, dil), :].astype(dtype)


def _band_masks():
    ri = lax.broadcasted_iota(jnp.int32, (HEAD, HEAD), 0)
    ci = lax.broadcasted_iota(jnp.int32, (HEAD, HEAD), 1)
    ri2 = lax.broadcasted_iota(jnp.int32, (HEAD, 2 * HEAD), 0)
    ci2 = lax.broadcasted_iota(jnp.int32, (HEAD, 2 * HEAD), 1)
    return ci <= ri, jnp.logical_and(ci2 >= ri2, ci2 - HEAD <= ri2)


def _qkv_specs(index):
    def spec(off):
        return pl.BlockSpec((L, HEAD), lambda gi, h: (0, off + index(gi, h)))
    return [spec(off) for off in (COL_Q, COL_K, COL_V)]


def _attn_core_fwd(proj):
    def body(q_ref, k_ref, v_ref, o_ref, lse_ref, qp_ref, kp_ref, vp_ref):
        gi = pl.program_id(0)
        m_cur, m_both = _band_masks()
        for g, dil in enumerate(DILATIONS):
            @pl.when(gi == g)
            def _(dil=dil):
                _gather_residues(((q_ref, qp_ref), (k_ref, kp_ref), (v_ref, vp_ref)), dil)
                for st, base, first in _attn_blocks(dil):
                    q = qp_ref[pl.ds(base, HEAD), :]
                    keys = pl.ds(base, HEAD) if first else pl.ds(base - HEAD, 2 * HEAD)
                    s = jnp.where(m_cur if first else m_both, _dot_nt(q, kp_ref[keys, :]) * SCALE, NEG)
                    mx = jnp.max(s, axis=-1, keepdims=True)
                    p = jnp.exp(s - mx)
                    den = jnp.sum(p, axis=-1, keepdims=True)
                    r = _rows(st, dil)
                    o_ref[r, :] = _dot(p.astype(BF16), vp_ref[keys, :]) / den
                    lse_ref[r, :] = jnp.broadcast_to(mx + jnp.log(den), (HEAD, HEAD))

    idx = lambda gi, h: gi * 4 + h
    out = pl.BlockSpec((L, HEAD), lambda gi, h: (0, gi * 4 + h))
    return _pcall(body, name="attn_core_fwd", grid=(3, 4), in_specs=_qkv_specs(idx), out_specs=[out, out],
                  out_shape=[_sds((L, QKV)), _sds((L, QKV))], scratch=[pltpu.VMEM((L, HEAD), BF16)] * 3,
                  vmem_mb=40)(proj, proj, proj)


def _attn_mix_fwd(o, lse, proj):
    tl = 512

    def body(o_ref, l_ref, z_ref, y_ref):
        for h in range(4):
            c = [pl.ds((g * 4 + h) * HEAD, HEAD) for g in range(3)]
            ls = [l_ref[:, c[g]] for g in range(3)]
            m = jnp.maximum(jnp.maximum(ls[0], ls[1]), ls[2])
            e = [jnp.exp(ls[g] - m) for g in range(3)]
            y = (e[0] * o_ref[:, c[0]] + e[1] * o_ref[:, c[1]] + e[2] * o_ref[:, c[2]]) / (e[0] + e[1] + e[2])
            z = z_ref[:, pl.ds(h * HEAD, HEAD)]
            y_ref[:, pl.ds(h * HEAD, HEAD)] = y * (z * _sigmoid(z))

    wide = pl.BlockSpec((tl, QKV), lambda i: (i, 0))
    return _pcall(body, name="attn_mix_fwd", grid=(L // tl,),
                  in_specs=[wide, wide, pl.BlockSpec((tl, AW), lambda i: (i, COL_ZA // 4))],
                  out_specs=pl.BlockSpec((tl, AW), lambda i: (i, 0)), out_shape=_sds((L, AW)),
                  vmem_mb=40)(o, lse, proj)


def _attn_mix_bwd(dya, o, lse, proj, dproj):
    tl = 512

    def body(dya_ref, o_ref, l_ref, z_ref, _, do_ref, c_ref, dz_ref):
        for h in range(4):
            c = [pl.ds((g * 4 + h) * HEAD, HEAD) for g in range(3)]
            hs = pl.ds(h * HEAD, HEAD)
            ls = [l_ref[:, c[g]] for g in range(3)]
            m = jnp.maximum(jnp.maximum(ls[0], ls[1]), ls[2])
            e = [jnp.exp(ls[g] - m) for g in range(3)]
            den = e[0] + e[1] + e[2]
            al = [e[g] / den for g in range(3)]
            y = al[0] * o_ref[:, c[0]] + al[1] * o_ref[:, c[1]] + al[2] * o_ref[:, c[2]]
            z = z_ref[:, hs]
            sz = _sigmoid(z)
            dya = dya_ref[:, hs]
            dz_ref[:, hs] = (dya * y * (sz * (1.0 + z * (1.0 - sz)))).astype(BF16)
            dy = dya * (z * sz)
            tot = jnp.sum(dy * y, axis=-1, keepdims=True)
            for g in range(3):
                do_ref[:, c[g]] = al[g] * dy
                c_ref[:, c[g]] = -(al[g] * tot)

    wide = pl.BlockSpec((tl, QKV), lambda i: (i, 0))
    nar = pl.BlockSpec((tl, AW), lambda i: (i, 0))
    za = pl.BlockSpec((tl, AW), lambda i: (i, COL_ZA // 4))
    return _pcall(body, name="attn_mix_bwd", grid=(L // tl,),
                  in_specs=[nar, wide, wide, za, pl.BlockSpec(memory_space=pl.ANY)],
                  out_specs=[wide, wide, za], out_shape=[_sds((L, QKV)), _sds((L, QKV)), _sds((L, NCOL), BF16)],
                  aliases={4: 2}, vmem_mb=48)(dya, o, lse, proj, dproj)


def _attn_core_bwd(proj, do, lse, cc, dproj):
    def body(q_ref, k_ref, v_ref, do_ref, lse_ref, c_ref, _, dp_ref,
             qp_ref, kp_ref, vp_ref, dop_ref, lsep_ref, cp_ref, dqp_ref, dkp_ref, dvp_ref, tok_ref, out_ref, sems):
        gi = pl.program_id(0)
        head = gi * 4 + pl.program_id(1)
        m_cur, m_both = _band_masks()
        dkp_ref[...] = jnp.zeros((L, HEAD), F32)
        dvp_ref[...] = jnp.zeros((L, HEAD), F32)
        copies = []
        for g, dil in enumerate(DILATIONS):
            @pl.when(gi == g)
            def _(dil=dil):
                _gather_residues(((q_ref, qp_ref), (k_ref, kp_ref), (v_ref, vp_ref), (do_ref, dop_ref)), dil)
                _gather_residues(((lse_ref, lsep_ref), (c_ref, cp_ref)), dil, F32)
                for _, base, first in _attn_blocks(dil):
                    mine = pl.ds(base, HEAD)
                    keys = mine if first else pl.ds(base - HEAD, 2 * HEAD)
                    q, do_b, kb = qp_ref[mine, :], dop_ref[mine, :], kp_ref[keys, :]
                    lse_b, c_b = lsep_ref[mine, :], cp_ref[mine, :]
                    if not first:
                        lse_b = jnp.concatenate([lse_b, lse_b], axis=1)
                        c_b = jnp.concatenate([c_b, c_b], axis=1)
                    s = _dot_nt(q, kb) * SCALE
                    p = jnp.where(m_cur if first else m_both, jnp.exp(s - lse_b), 0.0)
                    ds = (p * (_dot_nt(do_b, vp_ref[keys, :]) + c_b) * SCALE).astype(BF16)
                    dqp_ref[mine, :] = _dot(ds, kb)
                    dkp_ref[keys, :] += _dot_tn(ds, q)
                    dvp_ref[keys, :] += _dot_tn(p.astype(BF16), do_b)
                n = L // dil
                for slot, src in enumerate((dqp_ref, dkp_ref, dvp_ref)):
                    for r in range(dil):
                        tok_ref[_residue_rows(r, dil), :] = src[pl.ds(r * n, n), :]
                    out_ref[slot] = tok_ref[...].astype(BF16)

        for slot, off in enumerate((COL_Q, COL_K, COL_V)):
            cols = pl.ds(pl.multiple_of((off + head) * HEAD, HEAD), HEAD)
            copies.append(pltpu.make_async_copy(out_ref.at[slot], dp_ref.at[:, cols], sems.at[slot]))
            copies[-1].start()
        for cp in copies:
            cp.wait()

    idx = lambda gi, h: gi * 4 + h
    blk = pl.BlockSpec((L, HEAD), lambda gi, h: (0, gi * 4 + h))
    hbm = pl.BlockSpec(memory_space=pl.ANY)
    return _pcall(body, name="attn_core_bwd", grid=(3, 4), in_specs=_qkv_specs(idx) + [blk, blk, blk, hbm],
                  out_specs=hbm, out_shape=_sds((L, NCOL), BF16),
                  scratch=[pltpu.VMEM((L, HEAD), BF16)] * 4 + [pltpu.VMEM((L, HEAD), F32)] * 6 +
                  [pltpu.VMEM((3, L, HEAD), BF16), pltpu.SemaphoreType.DMA((3,))],
                  aliases={6: 0}, vmem_mb=48)(proj, proj, proj, do, lse, cc, dproj)


def _merge_fwd_math(ys_b, ya_b, gs, ga, wbs_ref, wba_ref, wout_ref):
    bs = jnp.concatenate([_dot(ys_b, wbs_ref[s]) for s in range(NCHIP)], axis=1)
    ba = jnp.concatenate([_dot(ya_b, wba_ref[s]) for s in range(NCHIP)], axis=1)
    sgs, sga = _sigmoid(gs), _sigmoid(ga)
    merged = sgs * bs + sga * ba
    out = _dot(merged.astype(BF16), wout_ref[...])
    inv = lax.rsqrt(jnp.mean(out * out, axis=-1, keepdims=True) + RMS_EPS)
    return bs, ba, sgs, sga, merged, out, inv


def _merge_specs(tl):
    row = lambda w, c: pl.BlockSpec((tl, w), lambda i: (i, c))
    return [row(SSM, 0), row(AW, 0), row(D, COL_GS // 8), row(D, COL_GA // 8),
            pl.BlockSpec((NCHIP, SSM, D // NCHIP), lambda i: (0, 0, 0)),
            pl.BlockSpec((NCHIP, AW, D // NCHIP), lambda i: (0, 0, 0)),
            pl.BlockSpec((D, D), lambda i: (0, 0)),
            pl.BlockSpec((1, D), lambda i: (0, 0))]


def _merge_out_fwd(ys, ya, proj, wbs4, wba4, wout, g2, x, dep=None):
    tl = 512
    deps, dep_specs = _token_operand(dep)

    def body(ys_ref, ya_ref, gs_ref, ga_ref, wbs_ref, wba_ref, wout_ref, g2_ref, x_ref, *rest):
        o_ref = rest[-1]
        *_, out, inv = _merge_fwd_math(ys_ref[...].astype(BF16), ya_ref[...].astype(BF16), gs_ref[...],
                                       ga_ref[...], wbs_ref, wba_ref, wout_ref)
        o_ref[...] = x_ref[...] + out * inv * g2_ref[...]

    row = pl.BlockSpec((tl, D), lambda i: (i, 0))
    return _pcall(body, name="merge_out_fwd", grid=(L // tl,), in_specs=_merge_specs(tl) + [row] + dep_specs,
                  out_specs=row, out_shape=_sds((L, D)),
                  vmem_mb=48)(ys, ya, proj, proj, wbs4, wba4, wout, g2, x, *deps)


def _merge_out_bwd(dxn, ys, ya, proj, wbs4, wba4, wout, g2, dep=None):
    tl = 256
    ni = L // tl
    cw = D // NCHIP
    deps, dep_specs = _token_operand(dep)

    def body(dxn_ref, ys_ref, ya_ref, gs_ref, ga_ref, wbs_ref, wba_ref, wout_ref, g2_ref, *rest):
        (dys_ref, dya_ref, dgate_ref, dwbs_ref, dwba_ref, dwout_ref, dg2_ref,
         abs_ref, aba_ref, aout_ref) = rest[len(deps):]
        i = pl.program_id(0)

        @pl.when(i == 0)
        def _():
            abs_ref[...] = jnp.zeros(abs_ref.shape, F32)
            aba_ref[...] = jnp.zeros(aba_ref.shape, F32)
            aout_ref[...] = jnp.zeros(aout_ref.shape, F32)
            dg2_ref[...] = jnp.zeros(dg2_ref.shape, F32)

        ys_b, ya_b = ys_ref[...].astype(BF16), ya_ref[...].astype(BF16)
        bs, ba, sgs, sga, merged, out, inv = _merge_fwd_math(ys_b, ya_b, gs_ref[...], ga_ref[...],
                                                             wbs_ref, wba_ref, wout_ref)
        nrm = out * inv
        dxn = dxn_ref[...]
        dg2_ref[...] += jnp.sum(dxn * nrm, axis=0, keepdims=True)
        dn = dxn * g2_ref[...]
        dout = (inv * (dn - nrm * jnp.mean(dn * nrm, axis=-1, keepdims=True))).astype(BF16)
        aout_ref[...] += _dot_tn(merged.astype(BF16), dout)
        dm = _dot_nt(dout, wout_ref[...])
        dbs, dba = dm * sgs, dm * sga
        dgate_ref[:, :D] = (dm * bs * (sgs * (1.0 - sgs))).astype(BF16)
        dgate_ref[:, D:] = (dm * ba * (sga * (1.0 - sga))).astype(BF16)
        dbs_b, dba_b = dbs.astype(BF16), dba.astype(BF16)
        dys = None
        dya = None
        for s in range(NCHIP):
            cs = slice(s * cw, (s + 1) * cw)
            p_s = _dot_nt(dbs_b[:, cs], wbs_ref[s])
            p_a = _dot_nt(dba_b[:, cs], wba_ref[s])
            dys = p_s if dys is None else dys + p_s
            dya = p_a if dya is None else dya + p_a
            abs_ref[s] += _dot_tn(ys_b, dbs_b[:, cs])
            aba_ref[s] += _dot_tn(ya_b, dba_b[:, cs])

        dys_ref[...] = dys
        dya_ref[...] = dya

        @pl.when(i == ni - 1)
        def _():
            dwbs_ref[...] = abs_ref[...].astype(BF16)
            dwba_ref[...] = aba_ref[...].astype(BF16)
            dwout_ref[...] = aout_ref[...].astype(BF16)

    row = lambda w: pl.BlockSpec((tl, w), lambda i: (i, 0))
    w4 = pl.BlockSpec((NCHIP, SSM, cw), lambda i: (0, 0, 0))
    sq = pl.BlockSpec((D, D), lambda i: (0, 0))
    vec = pl.BlockSpec((1, D), lambda i: (0, 0))
    return _pcall(body, name="merge_out_bwd", grid=(ni,), in_specs=[row(D)] + _merge_specs(tl) + dep_specs,
                  out_specs=[row(SSM), row(AW), pl.BlockSpec((tl, 2 * D), lambda i: (i, COL_GS // 16)), w4, w4, sq,
                             vec],
                  out_shape=[_sds((L, SSM)), _sds((L, AW)), _sds((L, NCOL), BF16),
                             _sds((NCHIP, SSM, cw), BF16), _sds((NCHIP, AW, cw), BF16), _sds((D, D), BF16),
                             _sds((1, D))],
                  scratch=[pltpu.VMEM((NCHIP, SSM, cw), F32), pltpu.VMEM((NCHIP, AW, cw), F32),
                           pltpu.VMEM((D, D), F32)],
                  vmem_mb=56)(dxn, ys, ya, proj, proj, wbs4, wba4, wout, g2, *deps)


def _loss_head(y, target):
    tl = 512

    def body(y_ref, t_ref, loss_ref, dy_ref):
        i = pl.program_id(0)
        err = y_ref[...] - t_ref[...]
        dy_ref[...] = err / D
        part = 0.5 * jnp.sum(jnp.mean(err * err, axis=-1, keepdims=True), axis=0, keepdims=True)
        part = jnp.broadcast_to(part, (8, 128))

        @pl.when(i == 0)
        def _():
            loss_ref[...] = part

        @pl.when(i > 0)
        def _():
            loss_ref[...] += part

    row = pl.BlockSpec((tl, D), lambda i: (i, 0))
    return _pcall(body, name="loss_head", grid=(L // tl,), in_specs=[row, row],
                  out_specs=[pl.BlockSpec((8, 128), lambda i: (0, 0)), row],
                  out_shape=[_sds((8, 128)), _sds((L, D))], vmem_mb=40)(y, target)


def _adamw_math(w, g, m, v):
    m = ADAM_B1 * m + (1.0 - ADAM_B1) * g
    v = ADAM_B2 * v + (1.0 - ADAM_B2) * (g * g)
    m_hat = m / (1.0 - ADAM_B1 ** ADAM_STEP)
    v_hat = v / (1.0 - ADAM_B2 ** ADAM_STEP)
    delta = -ADAM_LR * (m_hat / (jnp.sqrt(v_hat) + ADAM_EPS) + ADAM_WD * w)
    return delta, m, v


def _adamw_big(layer, w, m, v, own, sib, prev):
    _, r, c = w.shape
    tr = min(r, 128)

    def body(w_ref, m_ref, v_ref, own_ref, sib_ref, *rest):
        g_ref, d_ref, nm_ref, nv_ref = rest[-4:]
        a = own_ref[0].astype(F32)
        b = sib_ref[0].astype(F32)
        for s in range(1, NCHIP):
            a = a + own_ref[s].astype(F32)
            b = b + sib_ref[s].astype(F32)
        g = a + b
        delta, nm, nv = _adamw_math(w_ref[...], g, m_ref[...], v_ref[...])
        g_ref[...] = g
        d_ref[...] = delta
        nm_ref[...] = nm
        nv_ref[...] = nv

    lay = pl.BlockSpec((None, tr, c), lambda i: (layer, i, 0))
    slots = pl.BlockSpec((NCHIP, tr, c), lambda i: (0, i, 0))
    ins = [w, m, v, own, sib]
    in_specs = [lay, lay, lay, slots, slots]
    aliases = {}
    if prev is not None:
        ins += list(prev)
        in_specs += [pl.BlockSpec(memory_space=pl.ANY)] * 4
        aliases = {5 + k: k for k in range(4)}
    return _pcall(body, name="adamw_big", grid=(r // tr,), in_specs=in_specs, out_specs=[lay] * 4,
                  out_shape=[_sds(w.shape)] * 4, aliases=aliases, vmem_mb=48)(*ins)


SMALL_TILE = 512


def _adamw_small(w, parts, m, v):
    r = w.shape[0]

    def body(w_ref, p_ref, m_ref, v_ref, g_ref, d_ref, nm_ref, nv_ref):
        g = p_ref[0].astype(F32)
        for dev in range(1, 8):
            g = g + p_ref[dev].astype(F32)
        delta, nm, nv = _adamw_math(w_ref[...], g, m_ref[...], v_ref[...])
        g_ref[...] = g
        d_ref[...] = delta
        nm_ref[...] = nm
        nv_ref[...] = nv

    row = pl.BlockSpec((SMALL_TILE, 128), lambda i: (i, 0))
    return _pcall(body, name="adamw_small", grid=(r // SMALL_TILE,),
                  in_specs=[row, pl.BlockSpec((8, SMALL_TILE, 128), lambda i: (0, i, 0)), row, row],
                  out_specs=[row] * 4, out_shape=[_sds((r, 128))] * 4, vmem_mb=40)(w, parts, m, v)


def _place():
    x, y, c = lax.axis_index("x"), lax.axis_index("y"), lax.axis_index("c")
    return x, y, c, 2 * x + y


def _chip_peer(x, y, j):
    return (1 - x if j & 2 else x), (1 - y if j & 1 else y)


_HBM = pl.BlockSpec(memory_space=pltpu.HBM)
_SEM = pl.BlockSpec(memory_space=pltpu.SEMAPHORE)
_EFFECT = pltpu.SideEffectType.DATAFLOW_SIDE_EFFECTING


def _in_hbm(a):
    return pltpu.with_memory_space_constraint(a, pltpu.HBM)


def _copies_start(name, plan, ncopy, srcs, lands, dep=None):
    n = len(srcs) + len(lands)
    deps, dep_specs = _token_operand(dep)

    def body(*refs):
        send_sems, recv_sems = refs[n + len(deps)], refs[n + len(deps) + 1]
        token = refs[-1]
        for k, (src, dst, dev) in enumerate(plan(refs[:len(srcs)], refs[len(srcs):n])):
            if dev is None:
                pltpu.make_async_copy(src, dst, recv_sems.at[k]).start()
            else:
                pltpu.make_async_remote_copy(src_ref=src, dst_ref=dst, send_sem=send_sems.at[k],
                                             recv_sem=recv_sems.at[k], device_id=dev, device_id_type=MESH).start()
        token[...] = jnp.zeros_like(token)

    bufs = list(srcs) + list(lands)
    outs = pl.pallas_call(
        body, name=name,
        out_shape=(pltpu.SemaphoreType.DMA((ncopy,)), pltpu.SemaphoreType.DMA((ncopy,)),
                   *[pltpu.HBM(a.shape, a.dtype) for a in bufs], _sds((8, 128))),
        in_specs=[_HBM] * n + dep_specs,
        out_specs=(_SEM, _SEM, *[_HBM] * n, pl.BlockSpec(memory_space=pltpu.VMEM)),
        input_output_aliases={i: 2 + i for i in range(n)},
        compiler_params=pltpu.CompilerParams(has_side_effects=_EFFECT),
    )(*[_in_hbm(a) for a in bufs], *deps)
    return outs[0], outs[1], list(outs[2:2 + len(srcs)]), list(outs[2 + len(srcs):2 + n]), outs[-1]


def _copies_wait(name, plan, send_sems, recv_sems, srcs, lands, after):
    n = len(srcs) + len(lands)
    after = list(after)

    def body(*refs):
        s_sems, r_sems = refs[n], refs[n + 1]
        for k, (src, dst, dev) in enumerate(plan(refs[:len(srcs)], refs[len(srcs):n])):
            if dev is None:
                pltpu.make_async_copy(src, dst, r_sems.at[k]).wait()
                continue
            cp = pltpu.make_async_remote_copy(src_ref=src, dst_ref=dst, send_sem=s_sems.at[k],
                                              recv_sem=r_sems.at[k], device_id=dev, device_id_type=MESH)
            cp.wait_send()
            cp.wait_recv()

    bufs = list(srcs) + list(lands)
    outs = pl.pallas_call(
        body, name=name,
        out_shape=tuple(pltpu.HBM(a.shape, a.dtype) for a in bufs),
        in_specs=[_HBM] * n + [_SEM, _SEM] + [pl.BlockSpec(memory_space=pl.ANY)] * len(after),
        out_specs=tuple([_HBM] * n),
        input_output_aliases={i: i for i in range(n)},
        compiler_params=pltpu.CompilerParams(has_side_effects=_EFFECT),
    )(*bufs, send_sems, recv_sems, *after)
    return list(outs[:len(srcs)]), list(outs[len(srcs):])


def _with_own_slot(block, slot, nslots=NCHIP):
    land = lax.empty((nslots,) + block.shape, block.dtype)
    return lax.dynamic_update_slice(land, block[None], (slot,) + (0,) * block.ndim)


def _my_half(land, slot, c):
    half = land.shape[1] // 2
    return land.at[slot, pl.ds(pl.multiple_of(c * half, 16), half)]


def _gather_ici_plan(srcs, lands):
    x, y, c, s = _place()
    return [(_my_half(land, s, c), _my_half(land, s, c), (*_chip_peer(x, y, j), c))
            for land in lands for j in range(1, NCHIP)]


def _gather_d2d_plan(srcs, lands):
    x, y, c, s = _place()
    return [(_my_half(land, s ^ j, c), _my_half(land, s ^ j, c), (x, y, 1 - c))
            for land in lands for j in range(1, NCHIP)]


def _exchange_plan(srcs, lands):
    x, y, c, s = _place()
    return [(src.at[s ^ j], land.at[s], (*_chip_peer(x, y, j), c) if j else None)
            for src, land in zip(srcs, lands) for j in range(NCHIP)]


def _sibling_plan(srcs, lands):
    x, y, c, _ = _place()
    return [(src, land, (x, y, 1 - c)) for src, land in zip(srcs, lands)]


def _everyone_plan(srcs, lands):
    x, y, c, _ = _place()
    me = 4 * x + 2 * y + c
    return [(srcs[0], lands[0].at[me], (*_chip_peer(x, y, j >> 1), 1 - c if j & 1 else c)) for j in range(1, 8)]


def _flatten_small(parts):
    flat = jnp.concatenate([p.reshape(-1) for p in parts])
    n = flat.shape[0]
    rows = -(-n // (128 * SMALL_TILE)) * SMALL_TILE
    return jnp.pad(flat, (0, rows * 128 - n)).reshape(rows, 128)


def _unflatten_small(buf, like):
    flat = buf.reshape(-1)
    out, at = [], 0
    for p in like:
        out.append(flat[at:at + p.size].reshape(p.shape))
        at += p.size
    return out


def kernel(x, pre_norm_g, w_in, lambda_re, lambda_im, log_dt, b_re, b_im, c_re, c_im, d_skip, w_glu, b_glu, w_branch_s, w_branch_a, w_out, post_norm_g, loss_target, m_pre_norm_g, m_w_in, m_lambda_re, m_lambda_im, m_log_dt, m_b_re, m_b_im, m_c_re, m_c_im, m_d_skip, m_w_glu, m_b_glu, m_w_branch_s, m_w_branch_a, m_w_out, m_post_norm_g, v_pre_norm_g, v_w_in, v_lambda_re, v_lambda_im, v_log_dt, v_b_re, v_b_im, v_c_re, v_c_im, v_d_skip, v_w_glu, v_b_glu, v_w_branch_s, v_w_branch_a, v_w_out, v_post_norm_g):
    nb = DEPTH * NCB
    lre_c = lambda_re.reshape(nb, 1, HW)
    lim_c = lambda_im.reshape(nb, 1, HW)
    ldt_c = jnp.broadcast_to(log_dt[:, :, None], (DEPTH, 32, 64)).reshape(nb, 1, HW)
    b_rows = lambda t: t.reshape(DEPTH, NCB, 8, 64, 16).transpose(0, 1, 4, 2, 3).reshape(nb, 16, HW)
    c_rows = lambda t: t.reshape(DEPTH, NCB, 8, 16, 64).transpose(0, 1, 3, 2, 4).reshape(nb, 16, HW)
    bre_c, bim_c = b_rows(b_re), b_rows(b_im)
    ar, ai, wb, wct = _s5_prep(lre_c, lim_c, ldt_c, bre_c, bim_c, c_rows(c_re), c_rows(c_im))
    ar4 = ar.reshape(DEPTH, NCB, 1, HW)
    ai4 = ai.reshape(DEPTH, NCB, 1, HW)
    wb = wb.reshape(DEPTH, NCB, 128, SW)
    wct = wct.reshape(DEPTH, NCB, 128, SW)

    chip = 2 * lax.axis_index("x") + lax.axis_index("y")
    device = 2 * chip + lax.axis_index("c")

    def gather_start(l, dep):
        blocks = [w_in[l], w_glu[l], w_branch_s[l], w_branch_a[l], w_out[l]]
        if l == 0:
            blocks = lax.optimization_barrier(blocks)
        lands = [_with_own_slot(a.astype(BF16), chip) for a in blocks]
        return _copies_start(f"gather_start_{l}", _gather_ici_plan, 15, [], lands, dep)

    def gather_forward(l, flight, after):
        send, recv, _, lands, _ = flight
        _, lands = _copies_wait(f"gather_wait_{l}", _gather_ici_plan, send, recv, [], lands, after)
        return _copies_start(f"forward_start_{l}", _gather_d2d_plan, 15, [], lands)

    def gather_done(l, forward, after):
        send, recv, _, lands, _ = forward
        return _copies_wait(f"forward_wait_{l}", _gather_d2d_plan, send, recv, [], lands, after)[1]

    small_w = [pre_norm_g, lambda_re, lambda_im, log_dt, b_re, b_im, c_re, c_im, d_skip, b_glu, post_norm_g]
    small_m = [m_pre_norm_g, m_lambda_re, m_lambda_im, m_log_dt, m_b_re, m_b_im, m_c_re, m_c_im, m_d_skip,
               m_b_glu, m_post_norm_g]
    small_v = [v_pre_norm_g, v_lambda_re, v_lambda_im, v_log_dt, v_b_re, v_b_im, v_c_re, v_c_im, v_d_skip,
               v_b_glu, v_post_norm_g]
    flat_w, flat_m, flat_v = _flatten_small(small_w), _flatten_small(small_m), _flatten_small(small_v)

    xs = [x[0]]
    saved = []
    forward = gather_forward(0, gather_start(0, None), [xs[0], flat_w, flat_m, ar, ai, wb, wct])
    for l in range(DEPTH):
        w4, wglu4, wbs4, wba4, wout4 = gather_done(l, forward, [xs[l]] + ([flat_v] if l == 0 else []))
        started = None
        if l + 1 < DEPTH:
            flight = gather_start(l + 1, wglu4)
            started = flight[4]
        wglu = wglu4.reshape(SSM, SSM)
        wout = wout4.reshape(D, D)
        wb4, wc4 = wb[l], wct[l]
        g1 = pre_norm_g[l].reshape(1, D)
        g2 = post_norm_g[l].reshape(1, D)
        dsk = d_skip[l].reshape(1, SSM)
        bgl = b_glu[l].reshape(1, SSM)
        proj, ht = _rms_proj_fwd(xs[l], g1, w4, started)
        states, yraw = _s5_scan_fwd(proj, wb4, wc4, ar4[l], ai4[l])
        o, lse = _attn_core_fwd(proj)
        forwarded = None
        if l + 1 < DEPTH:
            forward = gather_forward(l + 1, flight, [yraw, o])
            forwarded = forward[4]
        ys = _s5_tail_fwd(yraw, proj, dsk, wglu, bgl)
        ya = _attn_mix_fwd(o, lse, proj)
        xs.append(_merge_out_fwd(ys, ya, proj, wbs4, wba4, wout, g2, xs[l], forwarded))
        saved.append((w4, wglu, wbs4, wba4, wout, wb4, wc4, g1, g2, dsk, bgl, proj, ht, states, yraw, ys, ya, o, lse))

    loss_part, dx = _loss_head(xs[DEPTH], loss_target[0])
    loss = lax.psum(loss_part[0, 0], ("x", "y", "c"))

    big_w = (w_in, w_glu.reshape(DEPTH, 128, SSM), w_branch_s, w_branch_a, w_out)
    big_m = (m_w_in, m_w_glu, m_w_branch_s, m_w_branch_a, m_w_out)
    big_v = (v_w_in, v_w_glu, v_w_branch_s, v_w_branch_a, v_w_out)
    big_out = [None] * 5
    small = {k: [None] * DEPTH for k in ("g1", "da", "dbb", "dcc", "dsk", "bgl", "g2")}
    ici = [None] * DEPTH
    d2d = [None] * DEPTH

    def exchange_start(l, parts):
        lands = [lax.empty(p.shape, p.dtype) for p in parts]
        ici[l] = _copies_start(f"exchange_start_{l}", _exchange_plan, 20, parts, lands)

    def handoff_start(l, after):
        send, recv, srcs, lands, _ = ici[l]
        _, own = _copies_wait(f"exchange_wait_{l}", _exchange_plan, send, recv, srcs, lands, after)
        d2d[l] = _copies_start(f"handoff_start_{l}", _sibling_plan, 5, own,
                               [lax.empty(a.shape, a.dtype) for a in own])

    def update(l, after):
        send, recv, own, lands, _ = d2d[l]
        own, sib = _copies_wait(f"handoff_wait_{l}", _sibling_plan, send, recv, own, lands, after)
        for k in range(5):
            big_out[k] = _adamw_big(l, big_w[k], big_m[k], big_v[k], own[k], sib[k], big_out[k])

    for l in reversed(range(DEPTH)):
        w4, wglu, wbs4, wba4, wout, wb4, wc4, g1, g2, dsk, bgl, proj, ht, states, yraw, ys, ya, o, lse = saved[l]
        started = None
        if l + 1 < DEPTH:
            started = ici[l + 1][4] + d2d[l + 2][4] if l + 2 < DEPTH else ici[l + 1][4]
        dys, dya, dproj, dwbs, dwba, dwout, dg2 = _merge_out_bwd(dx, ys, ya, proj, wbs4, wba4, wout, g2, started)
        dyraw, du_skip, dproj, dwglu, dbgl, ddsk = _s5_tail_bwd(dys, yraw, proj, dsk, wglu, bgl, dproj)
        dproj, dbb, dcc, da = _s5_scan_bwd(dyraw, states, proj, du_skip, wb4, wc4, ar4[l], ai4[l], dproj)
        do, cterm, dproj = _attn_mix_bwd(dya, o, lse, proj, dproj)
        dproj = _attn_core_bwd(proj, do, lse, cterm, dproj)
        dh, dwin = _proj_bwd(dproj, ht, w4)
        parts = [dwin, dwglu.reshape(NCHIP, 128, SSM), dwbs, dwba, dwout.reshape(NCHIP, D // NCHIP, D)]
        exchange_start(l, parts)
        dx, dg1 = _rms_bwd(dh, xs[l], g1, dx, ici[l][4])
        if l + 1 < DEPTH:
            handoff_start(l + 1, [ici[l][4]])
        small["g1"][l], small["g2"][l], small["dsk"][l], small["bgl"][l] = dg1, dg2, ddsk, dbgl
        small["da"][l], small["dbb"][l], small["dcc"][l] = da, dbb, dcc

    dlre, dlim, dldt, dbre_c, dbim_c = _s5_prep_bwd(
        lre_c, lim_c, ldt_c, bre_c, bim_c, jnp.stack(small["da"]).reshape(nb, 1, SW),
        jnp.stack(small["dbb"]).reshape(nb, 16, SW))
    dcc = jnp.stack(small["dcc"]).reshape(nb, 16, SW)
    b_back = lambda t: t.reshape(DEPTH, NCB, 16, 8, 64).transpose(0, 1, 3, 4, 2).reshape(DEPTH, 32, 64, 16)
    c_back = lambda t: t.reshape(DEPTH, NCB, 16, 8, 64).transpose(0, 1, 3, 2, 4).reshape(DEPTH, 32, 16, 64)
    small_g = [jnp.stack(small["g1"]).reshape(DEPTH, D), dlre.reshape(DEPTH, 32, 64), dlim.reshape(DEPTH, 32, 64),
               dldt.reshape(DEPTH, 32, 64).sum(-1), b_back(dbre_c), b_back(dbim_c), c_back(dcc[:, :, :HW]),
               c_back(dcc[:, :, HW:]),
               jnp.stack(small["dsk"]).reshape(DEPTH, SSM), jnp.stack(small["bgl"]).reshape(DEPTH, SSM),
               jnp.stack(small["g2"]).reshape(DEPTH, D)]
    part = _flatten_small(small_g).astype(BF16)
    send, recv, srcs, lands, token = _copies_start("small_start", _everyone_plan, 7, [part],
                                                   [_with_own_slot(part, device, 8)])
    update(DEPTH - 1, [token, ici[0][4]])
    for l in range(DEPTH - 2, 0, -1):
        update(l, [out[0] for out in big_out])
    handoff_start(0, [out[0] for out in big_out])
    update(0, [d2d[0][4]])
    _, (parts8,) = _copies_wait("small_wait", _everyone_plan, send, recv, srcs, lands, [out[0] for out in big_out])
    g_flat, d_flat, nm_flat, nv_flat = _adamw_small(flat_w, parts8, flat_m, flat_v)
    sg = _unflatten_small(g_flat, small_w)
    sd = _unflatten_small(d_flat, small_w)
    snm = _unflatten_small(nm_flat, small_w)
    snv = _unflatten_small(nv_flat, small_w)

    def ordered(sm, which):
        bg = [big_out[k][which] for k in range(5)]
        bg[1] = bg[1].reshape(DEPTH, 128, SSM)
        return [sm[0], bg[0], sm[1], sm[2], sm[3], sm[4], sm[5], sm[6], sm[7], sm[8], bg[1], sm[9], bg[2], bg[3],
                bg[4], sm[10]]

    return (loss, dx[None], *ordered(sg, 0), *ordered(sd, 1), *ordered(snm, 2), *ordered(snv, 3))
```

```python
import math

import jax
import jax.numpy as jnp
from jax import lax
from jax.experimental import pallas as pl
from jax.experimental.pallas import tpu as pltpu

F32 = jnp.float32
BF16 = jnp.bfloat16
MESH = pl.DeviceIdType.MESH

DEPTH = 4
L = 2048
D = 1024
NCOL = 8192
SSM = 512
AW = 512
QKV = 1536
RMS_EPS = 1e-6
NCHIP = 4

COL_U, COL_ZS, COL_Q, COL_K, COL_V, COL_ZA, COL_GS, COL_GA = 0, 4, 8, 20, 32, 44, 48, 56

NSEG = 8
SEG = L // NSEG
NCB = 4
HW = 512
SW = 2 * HW

HEAD = 128
DILATIONS = (1, 4, 16)
SCALE = HEAD ** -0.5
NEG = -1e30

ADAM_LR, ADAM_B1, ADAM_B2, ADAM_EPS, ADAM_WD, ADAM_STEP = 0.001, 0.9, 0.999, 1e-08, 0.01, 10


def _sds(shape, dtype=F32):
    return jax.ShapeDtypeStruct(shape, dtype)


def _pcall(body, *, name, out_shape, grid=None, in_specs=None, out_specs=None, scratch=(), vmem_mb=None,
           aliases=None):
    params = {}
    if vmem_mb is not None:
        params["vmem_limit_bytes"] = vmem_mb << 20
    kw = {}
    if grid is not None:
        kw["grid"] = grid
    if in_specs is not None:
        kw["in_specs"] = in_specs
    if out_specs is not None:
        kw["out_specs"] = out_specs
    return pl.pallas_call(body, name=name, out_shape=out_shape, scratch_shapes=list(scratch),
                          compiler_params=pltpu.CompilerParams(**params),
                          input_output_aliases=aliases or {}, **kw)


def _dot(a, b):
    return jnp.dot(a, b, preferred_element_type=F32)


def _dot_nt(a, b):
    return lax.dot_general(a, b, (((1,), (1,)), ((), ())), preferred_element_type=F32)


def _dot_tn(a, b):
    return lax.dot_general(a, b, (((0,), (0,)), ((), ())), preferred_element_type=F32)


def _sigmoid(x):
    return jax.nn.sigmoid(x)


_GELU_K = math.sqrt(2.0 / math.pi)


def _gelu(x):
    return 0.5 * x * (1.0 + jnp.tanh(_GELU_K * (x + 0.044715 * (x * x * x))))


def _gelu_grad(x):
    t = jnp.tanh(_GELU_K * (x + 0.044715 * (x * x * x)))
    return 0.5 * (1.0 + t) + 0.5 * x * (1.0 - t * t) * (_GELU_K * (1.0 + 3.0 * 0.044715 * (x * x)))


def _token_operand(dep):
    if dep is None:
        return [], []
    return [dep], [pl.BlockSpec(memory_space=pl.ANY)]


def _rms_proj_fwd(x, g, w4, dep=None):
    tl, tn = 512, 1024
    ni = L // tl
    deps, dep_specs = _token_operand(dep)

    def body(x_ref, g_ref, w_ref, *rest):
        o_ref, h_ref = rest[len(deps):]
        rows = pl.ds(pl.multiple_of(pl.program_id(1) * tl, tl), tl)

        @pl.when(pl.program_id(0) == 0)
        def _():
            xx = x_ref[...]
            inv = lax.rsqrt(jnp.mean(xx * xx, axis=-1, keepdims=True) + RMS_EPS)
            h_ref[rows, :] = (xx * inv * g_ref[...]).astype(BF16)

        o_ref[...] = _dot(h_ref[rows, :], w_ref[...])

    proj, h = _pcall(
        body, name="rms_proj_fwd", grid=(NCOL // tn, ni),
        in_specs=[pl.BlockSpec((tl, D), lambda j, i: (jnp.where(j == 0, i, ni - 1), 0)),
                  pl.BlockSpec((1, D), lambda j, i: (0, 0)),
                  pl.BlockSpec((None, D, tn), lambda j, i: (lax.div(j, 2), 0, lax.rem(j, 2)))] + dep_specs,
        out_specs=[pl.BlockSpec((tl, tn), lambda j, i: (i, j)), pl.BlockSpec((L, D), lambda j, i: (0, 0))],
        out_shape=[_sds((L, NCOL)), _sds((L, D), BF16)], vmem_mb=48)(x, g, w4, *deps)

    def transpose(h_ref, ht_ref):
        ht_ref[...] = h_ref[...].astype(F32).T.astype(BF16)

    ht = _pcall(transpose, name="transpose_h", grid=(ni,), in_specs=[pl.BlockSpec((tl, D), lambda i: (i, 0))],
                out_specs=pl.BlockSpec((D, tl), lambda i: (0, i)), out_shape=_sds((D, L), BF16), vmem_mb=40)(h)
    return proj, ht


def _proj_bwd(dproj, ht, w4):
    tn = 1024

    def body(dp_ref, ht_ref, w_ref, dh_ref, dw_ref):
        @pl.when(pl.program_id(0) == 0)
        def _():
            dh_ref[...] = jnp.zeros((L, D), F32)

        dw_ref[...] = _dot(ht_ref[...], dp_ref[...]).astype(BF16)
        dh_ref[...] += _dot_nt(dp_ref[...], w_ref[...])

    return _pcall(
        body, name="proj_bwd", grid=(NCOL // tn,),
        in_specs=[pl.BlockSpec((L, tn), lambda j: (0, j)),
                  pl.BlockSpec((D, L), lambda j: (0, 0)),
                  pl.BlockSpec((None, D, tn), lambda j: (lax.div(j, 2), 0, lax.rem(j, 2)))],
        out_specs=[pl.BlockSpec((L, D), lambda j: (0, 0)),
                   pl.BlockSpec((None, D, tn), lambda j: (lax.div(j, 2), 0, lax.rem(j, 2)))],
        out_shape=[_sds((L, D)), _sds((NCHIP, D, NCOL // NCHIP), BF16)], vmem_mb=56)(dproj, ht, w4)


def _rms_bwd(dh, x, g, dxn, dep=None):
    tl = 512
    deps, dep_specs = _token_operand(dep)

    def body(dh_ref, x_ref, g_ref, dxn_ref, *rest):
        dx_ref, dg_ref = rest[len(deps):]
        i = pl.program_id(0)
        xx = x_ref[...]
        inv = lax.rsqrt(jnp.mean(xx * xx, axis=-1, keepdims=True) + RMS_EPS)
        nrm = xx * inv
        dh_v = dh_ref[...]
        dn = dh_v * g_ref[...]
        dx_ref[...] = dxn_ref[...] + inv * (dn - nrm * jnp.mean(dn * nrm, axis=-1, keepdims=True))
        part = jnp.sum(dh_v * nrm, axis=0, keepdims=True)

        @pl.when(i == 0)
        def _():
            dg_ref[...] = part

        @pl.when(i > 0)
        def _():
            dg_ref[...] += part

    row = pl.BlockSpec((tl, D), lambda i: (i, 0))
    vec = pl.BlockSpec((1, D), lambda i: (0, 0))
    return _pcall(body, name="rms_bwd", grid=(L // tl,), in_specs=[row, row, vec, row] + dep_specs,
                  out_specs=[row, vec], out_shape=[_sds((L, D)), _sds((1, D))], vmem_mb=40)(dh, x, g, dxn, *deps)


def _s5_param_math(lre, lim, ldt, bre, bim):
    lr = jnp.minimum(lre, -1e-4)
    dt = jnp.exp(ldt)
    mag = jnp.exp(lr * dt)
    ar = mag * jnp.cos(lim * dt)
    ai = mag * jnp.sin(lim * dt)
    den = lr * lr + lim * lim
    nr = ar - 1.0
    qr = (nr * lr + ai * lim) / den
    qi = (ai * lr - nr * lim) / den
    return ar, ai, qr * bre - qi * bim, qr * bim + qi * bre


def _group_of_lane(width):
    lane = lax.broadcasted_iota(jnp.int32, (16, width), 1)
    return lax.shift_right_logical(jnp.bitwise_and(lane, HW - 1), 6)


def _vec_spec():
    return pl.BlockSpec((None, 1, HW), lambda n: (n, 0, 0))


def _mat_spec():
    return pl.BlockSpec((None, 16, HW), lambda n: (n, 0, 0))


def _s5_prep(lre, lim, ldt, bre, bim, cre, cim):
    n = lre.shape[0]

    def body(lre_ref, lim_ref, ldt_ref, bre_ref, bim_ref, cre_ref, cim_ref, ar_ref, ai_ref, wb_ref, wct_ref):
        ar, ai, bbr, bbi = _s5_param_math(lre_ref[...], lim_ref[...], ldt_ref[...], bre_ref[...], bim_ref[...])
        ar_ref[...] = ar
        ai_ref[...] = ai
        group = _group_of_lane(SW)
        bb = jnp.concatenate([bbr, bbi], axis=1)
        cc = jnp.concatenate([cre_ref[...], -cim_ref[...]], axis=1)
        for g in range(8):
            rows = pl.ds(16 * g, 16)
            wb_ref[rows, :] = jnp.where(group == g, bb, 0.0).astype(BF16)
            wct_ref[rows, :] = jnp.where(group == g, cc, 0.0).astype(BF16)

    wide = pl.BlockSpec((None, 128, SW), lambda n: (n, 0, 0))
    return _pcall(body, name="s5_prep", grid=(n,),
                  in_specs=[_vec_spec()] * 3 + [_mat_spec()] * 4,
                  out_specs=[_vec_spec(), _vec_spec(), wide, wide],
                  out_shape=[_sds((n, 1, HW)), _sds((n, 1, HW)), _sds((n, 128, SW), BF16), _sds((n, 128, SW), BF16)])(
                      lre, lim, ldt, bre, bim, cre, cim)


def _s5_prep_bwd(lre, lim, ldt, bre, bim, ga, gbb):
    n = lre.shape[0]

    def body(lre_ref, lim_ref, ldt_ref, bre_ref, bim_ref, ga_ref, gbb_ref,
             dlre_ref, dlim_ref, dldt_ref, dbre_ref, dbim_ref):
        _, vjp = jax.vjp(_s5_param_math, lre_ref[...], lim_ref[...], ldt_ref[...], bre_ref[...], bim_ref[...])
        ga, gbb = ga_ref[...], gbb_ref[...]
        dlre, dlim, dldt, dbre, dbim = vjp((ga[:, :HW], ga[:, HW:], gbb[:, :HW], gbb[:, HW:]))
        dlre_ref[...] = dlre
        dlim_ref[...] = dlim
        dldt_ref[...] = dldt
        dbre_ref[...] = dbre
        dbim_ref[...] = dbim

    return _pcall(body, name="s5_prep_bwd", grid=(n,),
                  in_specs=[_vec_spec()] * 3 + [_mat_spec()] * 2 +
                  [pl.BlockSpec((None, 1, SW), lambda n: (n, 0, 0)), pl.BlockSpec((None, 16, SW), lambda n: (n, 0, 0))],
                  out_specs=[_vec_spec()] * 3 + [_mat_spec()] * 2,
                  out_shape=[_sds((n, 1, HW))] * 3 + [_sds((n, 16, HW))] * 2)(lre, lim, ldt, bre, bim, ga, gbb)


def _diag_blocks(full):
    group = _group_of_lane(SW)
    out = jnp.where(group == 0, full[0:16, :], 0.0)
    for g in range(1, 8):
        out = out + jnp.where(group == g, full[16 * g:16 * g + 16, :], 0.0)
    return out


def _permute_rows(dst_ref, src_ref):
    for s in range(NSEG):
        dst_ref[pl.ds(s, SEG, stride=NSEG), :] = src_ref[pl.ds(s * SEG, SEG), :]


def _unpermute_rows(src_ref, s):
    return src_ref[pl.ds(s, SEG, stride=NSEG), :]


def _row_of(slab, rows, j):
    return jnp.sum(jnp.where(rows == j, slab, 0.0), axis=0, keepdims=True)


def _s5_scan_fwd(proj, wb4, wc4, ar4, ai4):
    def body(u_ref, wb_ref, wc_ref, ar_ref, ai_ref, x_ref, y_ref, up_ref, d_ref, yp_ref):
        _permute_rows(up_ref, u_ref)
        d_ref[...] = _dot(up_ref[...].astype(BF16), wb_ref[...])
        ar = jnp.broadcast_to(ar_ref[...], (NSEG, HW))
        ai = jnp.broadcast_to(ai_ref[...], (NSEG, HW))
        zero = jnp.zeros((NSEG, HW), F32)

        def drive(t):
            row = pl.multiple_of(t * NSEG, NSEG)
            dd = d_ref[pl.ds(row, NSEG), :]
            return row, dd[:, :HW], dd[:, HW:]

        def local_step(t, c):
            xr, xi = c
            _, dr, di = drive(t)
            return ar * xr - ai * xi + dr, ar * xi + ai * xr + di

        fr, fi = lax.fori_loop(0, SEG, local_step, (zero, zero), unroll=8)
        pr, pi_ = ar_ref[...], ai_ref[...]
        for _ in range(int(math.log2(SEG))):
            pr, pi_ = pr * pr - pi_ * pi_, 2.0 * pr * pi_
        rows = lax.broadcasted_iota(jnp.int32, (NSEG, HW), 0)
        cr, ci = zero, zero
        for j in range(NSEG - 1):
            sr, si = _row_of(cr, rows, j), _row_of(ci, rows, j)
            gr, gi = _row_of(fr, rows, j), _row_of(fi, rows, j)
            nr = pr * sr - pi_ * si + gr
            ni = pr * si + pi_ * sr + gi
            cr = jnp.where(rows == j + 1, nr, cr)
            ci = jnp.where(rows == j + 1, ni, ci)

        def true_step(t, c):
            xr, xi = c
            row, dr, di = drive(t)
            nr = ar * xr - ai * xi + dr
            ni = ar * xi + ai * xr + di
            x_ref[pl.ds(row, NSEG), :] = jnp.concatenate([nr, ni], axis=1)
            return nr, ni

        lax.fori_loop(0, SEG, true_step, (cr, ci), unroll=8)
        yp_ref[...] = _dot_nt(x_ref[...].astype(BF16), wc_ref[...])
        for s in range(NSEG):
            y_ref[pl.ds(s * SEG, SEG), :] = _unpermute_rows(yp_ref, s)

    return _pcall(
        body, name="s5_scan_fwd", grid=(NCB,),
        in_specs=[pl.BlockSpec((L, 128), lambda b: (0, COL_U + b)),
                  pl.BlockSpec((None, 128, SW), lambda b: (b, 0, 0)),
                  pl.BlockSpec((None, 128, SW), lambda b: (b, 0, 0)),
                  pl.BlockSpec((None, 1, HW), lambda b: (b, 0, 0)),
                  pl.BlockSpec((None, 1, HW), lambda b: (b, 0, 0))],
        out_specs=[pl.BlockSpec((None, L, SW), lambda b: (b, 0, 0)),
                   pl.BlockSpec((L, 128), lambda b: (0, b))],
        out_shape=[_sds((NCB, L, SW)), _sds((L, SSM))],
        scratch=[pltpu.VMEM((L, 128), F32), pltpu.VMEM((L, SW), F32), pltpu.VMEM((L, 128), F32)],
        vmem_mb=56)(proj, wb4, wc4, ar4, ai4)


def _s5_scan_bwd(dy, xs, proj, du_skip, wb4, wc4, ar4, ai4, dproj):
    def body(dy_ref, x_ref, u_ref, dus_ref, wb_ref, wc_ref, ar_ref, ai_ref, _,
             du_ref, dbb_ref, dcc_ref, da_ref, dyp_ref, up_ref, g_ref, dup_ref):
        _permute_rows(dyp_ref, dy_ref)
        _permute_rows(up_ref, u_ref)
        dyp = dyp_ref[...].astype(BF16)
        g_ref[...] = _dot(dyp, wc_ref[...])
        dcc = _diag_blocks(_dot_tn(dyp, x_ref[...].astype(BF16)))
        dcc_ref[...] = jnp.concatenate([dcc[:, :HW], -dcc[:, HW:]], axis=1)
        ar = jnp.broadcast_to(ar_ref[...], (NSEG, HW))
        ai = jnp.broadcast_to(ai_ref[...], (NSEG, HW))
        zero = jnp.zeros((NSEG, HW), F32)

        def load(t):
            row = pl.multiple_of(t * NSEG, NSEG)
            gg = g_ref[pl.ds(row, NSEG), :]
            return row, gg[:, :HW], gg[:, HW:]

        def local_step(k, c):
            lr, li = c
            _, gr, gi = load(SEG - 1 - k)
            return ar * lr + ai * li + gr, ar * li - ai * lr + gi

        fr, fi = lax.fori_loop(0, SEG, local_step, (zero, zero), unroll=8)
        pr, pi_ = ar_ref[...], -ai_ref[...]
        for _ in range(int(math.log2(SEG))):
            pr, pi_ = pr * pr - pi_ * pi_, 2.0 * pr * pi_
        rows = lax.broadcasted_iota(jnp.int32, (NSEG, HW), 0)
        cr, ci = zero, zero
        for j in range(NSEG - 1, 0, -1):
            sr, si = _row_of(cr, rows, j), _row_of(ci, rows, j)
            gr, gi = _row_of(fr, rows, j), _row_of(fi, rows, j)
            nr = pr * sr - pi_ * si + gr
            ni = pr * si + pi_ * sr + gi
            cr = jnp.where(rows == j - 1, nr, cr)
            ci = jnp.where(rows == j - 1, ni, ci)

        def true_step(k, c):
            lr, li, gar, gai = c
            t = SEG - 1 - k
            row, gr, gi = load(t)
            nr = ar * lr + ai * li + gr
            ni = ar * li - ai * lr + gi
            g_ref[pl.ds(row, NSEG), :] = jnp.concatenate([nr, ni], axis=1)
            prow = pl.multiple_of(jnp.maximum(t - 1, 0) * NSEG, NSEG)
            xp = x_ref[pl.ds(prow, NSEG), :]
            live = (t > 0).astype(F32)
            xr, xi = xp[:, :HW] * live, xp[:, HW:] * live
            return nr, ni, gar + (nr * xr + ni * xi), gai + (ni * xr - nr * xi)

        _, _, gar, gai = lax.fori_loop(0, SEG, true_step, (cr, ci, zero, zero), unroll=4)
        l0 = g_ref[1:NSEG, :]
        xl = x_ref[(SEG - 1) * NSEG:(SEG - 1) * NSEG + NSEG - 1, :]
        l0r, l0i, xlr, xli = l0[:, :HW], l0[:, HW:], xl[:, :HW], xl[:, HW:]
        gar_t = jnp.sum(gar, axis=0, keepdims=True) + jnp.sum(l0r * xlr + l0i * xli, axis=0, keepdims=True)
        gai_t = jnp.sum(gai, axis=0, keepdims=True) + jnp.sum(l0i * xlr - l0r * xli, axis=0, keepdims=True)
        da_ref[...] = jnp.concatenate([gar_t, gai_t], axis=1)
        lam = g_ref[...].astype(BF16)
        dbb_ref[...] = _diag_blocks(_dot_tn(up_ref[...].astype(BF16), lam))
        dup_ref[...] = _dot_nt(lam, wb_ref[...])
        for s in range(NSEG):
            sl = pl.ds(s * SEG, SEG)
            du_ref[sl, :] = (_unpermute_rows(dup_ref, s) + dus_ref[sl, :]).astype(BF16)

    col = lambda off: pl.BlockSpec((L, 128), lambda b: (0, off + b))
    return _pcall(
        body, name="s5_scan_bwd", grid=(NCB,),
        in_specs=[col(0), pl.BlockSpec((None, L, SW), lambda b: (b, 0, 0)), col(COL_U), col(0),
                  pl.BlockSpec((None, 128, SW), lambda b: (b, 0, 0)),
                  pl.BlockSpec((None, 128, SW), lambda b: (b, 0, 0)),
                  pl.BlockSpec((None, 1, HW), lambda b: (b, 0, 0)),
                  pl.BlockSpec((None, 1, HW), lambda b: (b, 0, 0)),
                  pl.BlockSpec(memory_space=pl.ANY)],
        out_specs=[col(COL_U), pl.BlockSpec((None, 16, SW), lambda b: (b, 0, 0)),
                   pl.BlockSpec((None, 16, SW), lambda b: (b, 0, 0)),
                   pl.BlockSpec((None, 1, SW), lambda b: (b, 0, 0))],
        out_shape=[_sds((L, NCOL), BF16), _sds((NCB, 16, SW)), _sds((NCB, 16, SW)), _sds((NCB, 1, SW))],
        scratch=[pltpu.VMEM((L, 128), F32), pltpu.VMEM((L, 128), F32), pltpu.VMEM((L, SW), F32),
                 pltpu.VMEM((L, 128), F32)],
        aliases={8: 0}, vmem_mb=56)(dy, xs, proj, du_skip, wb4, wc4, ar4, ai4, dproj)


def _s5_tail_fwd(yraw, proj, dsk, wglu, bglu):
    tl = 512

    def body(y_ref, u_ref, z_ref, dsk_ref, w_ref, b_ref, o_ref):
        y1 = y_ref[...] + dsk_ref[...] * u_ref[...]
        y2 = _gelu(y1)
        gl = _dot(y2.astype(BF16), w_ref[...]) + b_ref[...]
        z = z_ref[...]
        o_ref[...] = (y2 * _sigmoid(gl)) * (z * _sigmoid(z))

    blk = lambda c: pl.BlockSpec((tl, SSM), lambda i: (i, c))
    vec = pl.BlockSpec((1, SSM), lambda i: (0, 0))
    return _pcall(body, name="s5_tail_fwd", grid=(L // tl,),
                  in_specs=[blk(0), blk(0), blk(1), vec, pl.BlockSpec((SSM, SSM), lambda i: (0, 0)), vec],
                  out_specs=blk(0), out_shape=_sds((L, SSM)), vmem_mb=40)(yraw, proj, proj, dsk, wglu, bglu)


def _s5_tail_bwd(dys, yraw, proj, dsk, wglu, bglu, dproj):
    tl = 512
    ni = L // tl

    def body(dys_ref, y_ref, u_ref, z_ref, dsk_ref, w_ref, b_ref, _,
             dy_ref, dus_ref, dz_ref, dw_ref, db_ref, dd_ref, acc_ref):
        i = pl.program_id(0)
        u = u_ref[...]
        y1 = y_ref[...] + dsk_ref[...] * u
        y2 = _gelu(y1)
        y2b = y2.astype(BF16)
        sg = _sigmoid(_dot(y2b, w_ref[...]) + b_ref[...])
        y3 = y2 * sg
        z = z_ref[...]
        sz = _sigmoid(z)
        dys = dys_ref[...]
        dy3 = dys * (z * sz)
        dz_ref[...] = (dys * y3 * (sz * (1.0 + z * (1.0 - sz)))).astype(BF16)
        dgl = (dy3 * y2) * (sg * (1.0 - sg))
        dglb = dgl.astype(BF16)
        dy2 = dy3 * sg + _dot_nt(dglb, w_ref[...])
        dy1 = dy2 * _gelu_grad(y1)
        dy_ref[...] = dy1
        dus_ref[...] = dsk_ref[...] * dy1
        dw = _dot_tn(y2b, dglb)
        db = jnp.sum(dgl, axis=0, keepdims=True)
        dd = jnp.sum(dy1 * u, axis=0, keepdims=True)

        @pl.when(i == 0)
        def _():
            acc_ref[...] = dw
            db_ref[...] = db
            dd_ref[...] = dd

        @pl.when(i > 0)
        def _():
            acc_ref[...] += dw
            db_ref[...] += db
            dd_ref[...] += dd

        @pl.when(i == ni - 1)
        def _():
            dw_ref[...] = acc_ref[...].astype(BF16)

    blk = lambda c: pl.BlockSpec((tl, SSM), lambda i: (i, c))
    vec = pl.BlockSpec((1, SSM), lambda i: (0, 0))
    mat = pl.BlockSpec((SSM, SSM), lambda i: (0, 0))
    return _pcall(body, name="s5_tail_bwd", grid=(ni,),
                  in_specs=[blk(0), blk(0), blk(0), blk(1), vec, mat, vec, pl.BlockSpec(memory_space=pl.ANY)],
                  out_specs=[blk(0), blk(0), blk(COL_ZS // 4), mat, vec, vec],
                  out_shape=[_sds((L, SSM)), _sds((L, SSM)), _sds((L, NCOL), BF16), _sds((SSM, SSM), BF16),
                             _sds((1, SSM)), _sds((1, SSM))],
                  scratch=[pltpu.VMEM((SSM, SSM), F32)], aliases={7: 2},
                  vmem_mb=40)(dys, yraw, proj, proj, dsk, wglu, bglu, dproj)


def _attn_blocks(dil):
    n = L // dil
    return [(i * HEAD * dil + r, r * n + i * HEAD, i == 0) for r in range(dil) for i in range(n // HEAD)]


def _rows(start, dil):
    return pl.ds(start, HEAD) if dil == 1 else pl.ds(start, HEAD, stride=dil)


def _residue_rows(r, dil):
    n = L // dil
    return pl.ds(0, L) if dil == 1 else pl.ds(r, n, stride=dil)


def _gather_residues(pairs, dil, dtype=BF16):
    n = L // dil
    for src, dst in pairs:
        for r in range(dil):
            dst[pl.ds(r * n, n), :] = src[_residue_rows(r, dil), :].astype(dtype)


def _band_masks():
    ri = lax.broadcasted_iota(jnp.int32, (HEAD, HEAD), 0)
    ci = lax.broadcasted_iota(jnp.int32, (HEAD, HEAD), 1)
    ri2 = lax.broadcasted_iota(jnp.int32, (HEAD, 2 * HEAD), 0)
    ci2 = lax.broadcasted_iota(jnp.int32, (HEAD, 2 * HEAD), 1)
    return ci <= ri, jnp.logical_and(ci2 >= ri2, ci2 - HEAD <= ri2)


def _qkv_specs(index):
    def spec(off):
        return pl.BlockSpec((L, HEAD), lambda gi, h: (0, off + index(gi, h)))
    return [spec(off) for off in (COL_Q, COL_K, COL_V)]


def _attn_core_fwd(proj):
    def body(q_ref, k_ref, v_ref, o_ref, lse_ref, qp_ref, kp_ref, vp_ref):
        gi = pl.program_id(0)
        m_cur, m_both = _band_masks()
        for g, dil in enumerate(DILATIONS):
            @pl.when(gi == g)
            def _(dil=dil):
                _gather_residues(((q_ref, qp_ref), (k_ref, kp_ref), (v_ref, vp_ref)), dil)
                for st, base, first in _attn_blocks(dil):
                    q = qp_ref[pl.ds(base, HEAD), :]
                    keys = pl.ds(base, HEAD) if first else pl.ds(base - HEAD, 2 * HEAD)
                    s = jnp.where(m_cur if first else m_both, _dot_nt(q, kp_ref[keys, :]) * SCALE, NEG)
                    mx = jnp.max(s, axis=-1, keepdims=True)
                    p = jnp.exp(s - mx)
                    den = jnp.sum(p, axis=-1, keepdims=True)
                    r = _rows(st, dil)
                    o_ref[r, :] = _dot(p.astype(BF16), vp_ref[keys, :]) / den
                    lse_ref[r, :] = jnp.broadcast_to(mx + jnp.log(den), (HEAD, HEAD))

    idx = lambda gi, h: gi * 4 + h
    out = pl.BlockSpec((L, HEAD), lambda gi, h: (0, gi * 4 + h))
    return _pcall(body, name="attn_core_fwd", grid=(3, 4), in_specs=_qkv_specs(idx), out_specs=[out, out],
                  out_shape=[_sds((L, QKV)), _sds((L, QKV))], scratch=[pltpu.VMEM((L, HEAD), BF16)] * 3,
                  vmem_mb=40)(proj, proj, proj)


def _attn_mix_fwd(o, lse, proj):
    tl = 512

    def body(o_ref, l_ref, z_ref, y_ref):
        for h in range(4):
            c = [pl.ds((g * 4 + h) * HEAD, HEAD) for g in range(3)]
            ls = [l_ref[:, c[g]] for g in range(3)]
            m = jnp.maximum(jnp.maximum(ls[0], ls[1]), ls[2])
            e = [jnp.exp(ls[g] - m) for g in range(3)]
            y = (e[0] * o_ref[:, c[0]] + e[1] * o_ref[:, c[1]] + e[2] * o_ref[:, c[2]]) / (e[0] + e[1] + e[2])
            z = z_ref[:, pl.ds(h * HEAD, HEAD)]
            y_ref[:, pl.ds(h * HEAD, HEAD)] = y * (z * _sigmoid(z))

    wide = pl.BlockSpec((tl, QKV), lambda i: (i, 0))
    return _pcall(body, name="attn_mix_fwd", grid=(L // tl,),
                  in_specs=[wide, wide, pl.BlockSpec((tl, AW), lambda i: (i, COL_ZA // 4))],
                  out_specs=pl.BlockSpec((tl, AW), lambda i: (i, 0)), out_shape=_sds((L, AW)),
                  vmem_mb=40)(o, lse, proj)


def _attn_mix_bwd(dya, o, lse, proj, dproj):
    tl = 512

    def body(dya_ref, o_ref, l_ref, z_ref, _, do_ref, c_ref, dz_ref):
        for h in range(4):
            c = [pl.ds((g * 4 + h) * HEAD, HEAD) for g in range(3)]
            hs = pl.ds(h * HEAD, HEAD)
            ls = [l_ref[:, c[g]] for g in range(3)]
            m = jnp.maximum(jnp.maximum(ls[0], ls[1]), ls[2])
            e = [jnp.exp(ls[g] - m) for g in range(3)]
            den = e[0] + e[1] + e[2]
            al = [e[g] / den for g in range(3)]
            y = al[0] * o_ref[:, c[0]] + al[1] * o_ref[:, c[1]] + al[2] * o_ref[:, c[2]]
            z = z_ref[:, hs]
            sz = _sigmoid(z)
            dya = dya_ref[:, hs]
            dz_ref[:, hs] = (dya * y * (sz * (1.0 + z * (1.0 - sz)))).astype(BF16)
            dy = dya * (z * sz)
            tot = jnp.sum(dy * y, axis=-1, keepdims=True)
            for g in range(3):
                do_ref[:, c[g]] = al[g] * dy
                c_ref[:, c[g]] = -(al[g] * tot)

    wide = pl.BlockSpec((tl, QKV), lambda i: (i, 0))
    nar = pl.BlockSpec((tl, AW), lambda i: (i, 0))
    za = pl.BlockSpec((tl, AW), lambda i: (i, COL_ZA // 4))
    return _pcall(body, name="attn_mix_bwd", grid=(L // tl,),
                  in_specs=[nar, wide, wide, za, pl.BlockSpec(memory_space=pl.ANY)],
                  out_specs=[wide, wide, za], out_shape=[_sds((L, QKV)), _sds((L, QKV)), _sds((L, NCOL), BF16)],
                  aliases={4: 2}, vmem_mb=48)(dya, o, lse, proj, dproj)


def _attn_core_bwd(proj, do, lse, cc, dproj):
    def body(q_ref, k_ref, v_ref, do_ref, lse_ref, c_ref, _, dp_ref,
             qp_ref, kp_ref, vp_ref, dop_ref, lsep_ref, cp_ref, dqp_ref, dkp_ref, dvp_ref, tok_ref, out_ref, sems):
        gi = pl.program_id(0)
        head = gi * 4 + pl.program_id(1)
        m_cur, m_both = _band_masks()
        dkp_ref[...] = jnp.zeros((L, HEAD), F32)
        dvp_ref[...] = jnp.zeros((L, HEAD), F32)
        copies = []
        for g, dil in enumerate(DILATIONS):
            @pl.when(gi == g)
            def _(dil=dil):
                _gather_residues(((q_ref, qp_ref), (k_ref, kp_ref), (v_ref, vp_ref), (do_ref, dop_ref)), dil)
                lse_src, c_src = (lse_ref, c_ref) if dil == 1 else (lsep_ref, cp_ref)
                if dil > 1:
                    _gather_residues(((lse_ref, lsep_ref), (c_ref, cp_ref)), dil, F32)
                for _, base, first in _attn_blocks(dil):
                    mine = pl.ds(base, HEAD)
                    keys = mine if first else pl.ds(base - HEAD, 2 * HEAD)
                    q, do_b, kb = qp_ref[mine, :], dop_ref[mine, :], kp_ref[keys, :]
                    lse_b, c_b = lse_src[mine, :], c_src[mine, :]
                    if not first:
                        lse_b = jnp.concatenate([lse_b, lse_b], axis=1)
                        c_b = jnp.concatenate([c_b, c_b], axis=1)
                    s = _dot_nt(q, kb) * SCALE
                    p = jnp.where(m_cur if first else m_both, jnp.exp(s - lse_b), 0.0)
                    ds = (p * (_dot_nt(do_b, vp_ref[keys, :]) + c_b) * SCALE).astype(BF16)
                    dqp_ref[mine, :] = _dot(ds, kb)
                    dkp_ref[keys, :] += _dot_tn(ds, q)
                    dvp_ref[keys, :] += _dot_tn(p.astype(BF16), do_b)
                n = L // dil
                for slot, src in enumerate((dqp_ref, dkp_ref, dvp_ref)):
                    if dil == 1:
                        out_ref[slot] = src[...].astype(BF16)
                        continue
                    for r in range(dil):
                        tok_ref[_residue_rows(r, dil), :] = src[pl.ds(r * n, n), :]
                    out_ref[slot] = tok_ref[...].astype(BF16)

        for slot, off in enumerate((COL_Q, COL_K, COL_V)):
            cols = pl.ds(pl.multiple_of((off + head) * HEAD, HEAD), HEAD)
            copies.append(pltpu.make_async_copy(out_ref.at[slot], dp_ref.at[:, cols], sems.at[slot]))
            copies[-1].start()
        for cp in copies:
            cp.wait()

    idx = lambda gi, h: gi * 4 + h
    blk = pl.BlockSpec((L, HEAD), lambda gi, h: (0, gi * 4 + h))
    hbm = pl.BlockSpec(memory_space=pl.ANY)
    return _pcall(body, name="attn_core_bwd", grid=(3, 4), in_specs=_qkv_specs(idx) + [blk, blk, blk, hbm],
                  out_specs=hbm, out_shape=_sds((L, NCOL), BF16),
                  scratch=[pltpu.VMEM((L, HEAD), BF16)] * 4 + [pltpu.VMEM((L, HEAD), F32)] * 6 +
                  [pltpu.VMEM((3, L, HEAD), BF16), pltpu.SemaphoreType.DMA((3,))],
                  aliases={6: 0}, vmem_mb=48)(proj, proj, proj, do, lse, cc, dproj)


def _merge_fwd_math(ys_b, ya_b, gs, ga, wbs_ref, wba_ref, wout_ref):
    bs = jnp.concatenate([_dot(ys_b, wbs_ref[s]) for s in range(NCHIP)], axis=1)
    ba = jnp.concatenate([_dot(ya_b, wba_ref[s]) for s in range(NCHIP)], axis=1)
    sgs, sga = _sigmoid(gs), _sigmoid(ga)
    merged = sgs * bs + sga * ba
    out = _dot(merged.astype(BF16), wout_ref[...])
    inv = lax.rsqrt(jnp.mean(out * out, axis=-1, keepdims=True) + RMS_EPS)
    return bs, ba, sgs, sga, merged, out, inv


def _merge_specs(tl):
    row = lambda w, c: pl.BlockSpec((tl, w), lambda i: (i, c))
    return [row(SSM, 0), row(AW, 0), row(D, COL_GS // 8), row(D, COL_GA // 8),
            pl.BlockSpec((NCHIP, SSM, D // NCHIP), lambda i: (0, 0, 0)),
            pl.BlockSpec((NCHIP, AW, D // NCHIP), lambda i: (0, 0, 0)),
            pl.BlockSpec((D, D), lambda i: (0, 0)),
            pl.BlockSpec((1, D), lambda i: (0, 0))]


def _merge_out_fwd(ys, ya, proj, wbs4, wba4, wout, g2, x, dep=None):
    tl = 512
    deps, dep_specs = _token_operand(dep)

    def body(ys_ref, ya_ref, gs_ref, ga_ref, wbs_ref, wba_ref, wout_ref, g2_ref, x_ref, *rest):
        o_ref = rest[-1]
        *_, out, inv = _merge_fwd_math(ys_ref[...].astype(BF16), ya_ref[...].astype(BF16), gs_ref[...],
                                       ga_ref[...], wbs_ref, wba_ref, wout_ref)
        o_ref[...] = x_ref[...] + out * inv * g2_ref[...]

    row = pl.BlockSpec((tl, D), lambda i: (i, 0))
    return _pcall(body, name="merge_out_fwd", grid=(L // tl,), in_specs=_merge_specs(tl) + [row] + dep_specs,
                  out_specs=row, out_shape=_sds((L, D)),
                  vmem_mb=48)(ys, ya, proj, proj, wbs4, wba4, wout, g2, x, *deps)


def _merge_out_bwd(dxn, ys, ya, proj, wbs4, wba4, wout, g2, dep=None):
    tl = 256
    ni = L // tl
    cw = D // NCHIP
    deps, dep_specs = _token_operand(dep)

    def body(dxn_ref, ys_ref, ya_ref, gs_ref, ga_ref, wbs_ref, wba_ref, wout_ref, g2_ref, *rest):
        (dys_ref, dya_ref, dgate_ref, dwbs_ref, dwba_ref, dwout_ref, dg2_ref,
         abs_ref, aba_ref, aout_ref) = rest[len(deps):]
        i = pl.program_id(0)

        @pl.when(i == 0)
        def _():
            abs_ref[...] = jnp.zeros(abs_ref.shape, F32)
            aba_ref[...] = jnp.zeros(aba_ref.shape, F32)
            aout_ref[...] = jnp.zeros(aout_ref.shape, F32)
            dg2_ref[...] = jnp.zeros(dg2_ref.shape, F32)

        ys_b, ya_b = ys_ref[...].astype(BF16), ya_ref[...].astype(BF16)
        bs, ba, sgs, sga, merged, out, inv = _merge_fwd_math(ys_b, ya_b, gs_ref[...], ga_ref[...],
                                                             wbs_ref, wba_ref, wout_ref)
        nrm = out * inv
        dxn = dxn_ref[...]
        dg2_ref[...] += jnp.sum(dxn * nrm, axis=0, keepdims=True)
        dn = dxn * g2_ref[...]
        dout = (inv * (dn - nrm * jnp.mean(dn * nrm, axis=-1, keepdims=True))).astype(BF16)
        aout_ref[...] += _dot_tn(merged.astype(BF16), dout)
        dm = _dot_nt(dout, wout_ref[...])
        dbs, dba = dm * sgs, dm * sga
        dgate_ref[:, :D] = (dm * bs * (sgs * (1.0 - sgs))).astype(BF16)
        dgate_ref[:, D:] = (dm * ba * (sga * (1.0 - sga))).astype(BF16)
        dbs_b, dba_b = dbs.astype(BF16), dba.astype(BF16)
        dys = None
        dya = None
        for s in range(NCHIP):
            cs = slice(s * cw, (s + 1) * cw)
            p_s = _dot_nt(dbs_b[:, cs], wbs_ref[s])
            p_a = _dot_nt(dba_b[:, cs], wba_ref[s])
            dys = p_s if dys is None else dys + p_s
            dya = p_a if dya is None else dya + p_a
            abs_ref[s] += _dot_tn(ys_b, dbs_b[:, cs])
            aba_ref[s] += _dot_tn(ya_b, dba_b[:, cs])

        dys_ref[...] = dys
        dya_ref[...] = dya

        @pl.when(i == ni - 1)
        def _():
            dwbs_ref[...] = abs_ref[...].astype(BF16)
            dwba_ref[...] = aba_ref[...].astype(BF16)
            dwout_ref[...] = aout_ref[...].astype(BF16)

    row = lambda w: pl.BlockSpec((tl, w), lambda i: (i, 0))
    w4 = pl.BlockSpec((NCHIP, SSM, cw), lambda i: (0, 0, 0))
    sq = pl.BlockSpec((D, D), lambda i: (0, 0))
    vec = pl.BlockSpec((1, D), lambda i: (0, 0))
    return _pcall(body, name="merge_out_bwd", grid=(ni,), in_specs=[row(D)] + _merge_specs(tl) + dep_specs,
                  out_specs=[row(SSM), row(AW), pl.BlockSpec((tl, 2 * D), lambda i: (i, COL_GS // 16)), w4, w4, sq,
                             vec],
                  out_shape=[_sds((L, SSM)), _sds((L, AW)), _sds((L, NCOL), BF16),
                             _sds((NCHIP, SSM, cw), BF16), _sds((NCHIP, AW, cw), BF16), _sds((D, D), BF16),
                             _sds((1, D))],
                  scratch=[pltpu.VMEM((NCHIP, SSM, cw), F32), pltpu.VMEM((NCHIP, AW, cw), F32),
                           pltpu.VMEM((D, D), F32)],
                  vmem_mb=56)(dxn, ys, ya, proj, proj, wbs4, wba4, wout, g2, *deps)


def _loss_head(y, target):
    tl = 512

    def body(y_ref, t_ref, loss_ref, dy_ref):
        i = pl.program_id(0)
        err = y_ref[...] - t_ref[...]
        dy_ref[...] = err / D
        part = 0.5 * jnp.sum(jnp.mean(err * err, axis=-1, keepdims=True), axis=0, keepdims=True)
        part = jnp.broadcast_to(part, (8, 128))

        @pl.when(i == 0)
        def _():
            loss_ref[...] = part

        @pl.when(i > 0)
        def _():
            loss_ref[...] += part

    row = pl.BlockSpec((tl, D), lambda i: (i, 0))
    return _pcall(body, name="loss_head", grid=(L // tl,), in_specs=[row, row],
                  out_specs=[pl.BlockSpec((8, 128), lambda i: (0, 0)), row],
                  out_shape=[_sds((8, 128)), _sds((L, D))], vmem_mb=40)(y, target)


def _adamw_math(w, g, m, v):
    m = ADAM_B1 * m + (1.0 - ADAM_B1) * g
    v = ADAM_B2 * v + (1.0 - ADAM_B2) * (g * g)
    m_hat = m / (1.0 - ADAM_B1 ** ADAM_STEP)
    v_hat = v / (1.0 - ADAM_B2 ** ADAM_STEP)
    delta = -ADAM_LR * (m_hat / (jnp.sqrt(v_hat) + ADAM_EPS) + ADAM_WD * w)
    return delta, m, v


def _adamw_big(layer, w, m, v, own, sib, prev):
    _, r, c = w.shape
    tr = min(r, 128)

    def body(w_ref, m_ref, v_ref, own_ref, sib_ref, *rest):
        g_ref, d_ref, nm_ref, nv_ref = rest[-4:]
        a = own_ref[0].astype(F32)
        b = sib_ref[0].astype(F32)
        for s in range(1, NCHIP):
            a = a + own_ref[s].astype(F32)
            b = b + sib_ref[s].astype(F32)
        g = a + b
        delta, nm, nv = _adamw_math(w_ref[...], g, m_ref[...], v_ref[...])
        g_ref[...] = g
        d_ref[...] = delta
        nm_ref[...] = nm
        nv_ref[...] = nv

    lay = pl.BlockSpec((None, tr, c), lambda i: (layer, i, 0))
    slots = pl.BlockSpec((NCHIP, tr, c), lambda i: (0, i, 0))
    ins = [w, m, v, own, sib]
    in_specs = [lay, lay, lay, slots, slots]
    aliases = {}
    if prev is not None:
        ins += list(prev)
        in_specs += [pl.BlockSpec(memory_space=pl.ANY)] * 4
        aliases = {5 + k: k for k in range(4)}
    return _pcall(body, name="adamw_big", grid=(r // tr,), in_specs=in_specs, out_specs=[lay] * 4,
                  out_shape=[_sds(w.shape)] * 4, aliases=aliases, vmem_mb=48)(*ins)


SMALL_TILE = 512


def _adamw_small(w, parts, m, v):
    r = w.shape[0]

    def body(w_ref, p_ref, m_ref, v_ref, g_ref, d_ref, nm_ref, nv_ref):
        g = p_ref[0].astype(F32)
        for dev in range(1, 8):
            g = g + p_ref[dev].astype(F32)
        delta, nm, nv = _adamw_math(w_ref[...], g, m_ref[...], v_ref[...])
        g_ref[...] = g
        d_ref[...] = delta
        nm_ref[...] = nm
        nv_ref[...] = nv

    row = pl.BlockSpec((SMALL_TILE, 128), lambda i: (i, 0))
    return _pcall(body, name="adamw_small", grid=(r // SMALL_TILE,),
                  in_specs=[row, pl.BlockSpec((8, SMALL_TILE, 128), lambda i: (0, i, 0)), row, row],
                  out_specs=[row] * 4, out_shape=[_sds((r, 128))] * 4, vmem_mb=40)(w, parts, m, v)


def _place():
    x, y, c = lax.axis_index("x"), lax.axis_index("y"), lax.axis_index("c")
    return x, y, c, 2 * x + y


def _chip_peer(x, y, j):
    return (1 - x if j & 2 else x), (1 - y if j & 1 else y)


_HBM = pl.BlockSpec(memory_space=pltpu.HBM)
_SEM = pl.BlockSpec(memory_space=pltpu.SEMAPHORE)
_EFFECT = pltpu.SideEffectType.DATAFLOW_SIDE_EFFECTING


def _in_hbm(a):
    return pltpu.with_memory_space_constraint(a, pltpu.HBM)


def _copies_start(name, plan, ncopy, srcs, lands, dep=None):
    n = len(srcs) + len(lands)
    deps, dep_specs = _token_operand(dep)

    def body(*refs):
        send_sems, recv_sems = refs[n + len(deps)], refs[n + len(deps) + 1]
        token = refs[-1]
        for k, (src, dst, dev) in enumerate(plan(refs[:len(srcs)], refs[len(srcs):n])):
            if dev is None:
                pltpu.make_async_copy(src, dst, recv_sems.at[k]).start()
            else:
                pltpu.make_async_remote_copy(src_ref=src, dst_ref=dst, send_sem=send_sems.at[k],
                                             recv_sem=recv_sems.at[k], device_id=dev, device_id_type=MESH).start()
        token[...] = jnp.zeros_like(token)

    bufs = list(srcs) + list(lands)
    outs = pl.pallas_call(
        body, name=name,
        out_shape=(pltpu.SemaphoreType.DMA((ncopy,)), pltpu.SemaphoreType.DMA((ncopy,)),
                   *[pltpu.HBM(a.shape, a.dtype) for a in bufs], _sds((8, 128))),
        in_specs=[_HBM] * n + dep_specs,
        out_specs=(_SEM, _SEM, *[_HBM] * n, pl.BlockSpec(memory_space=pltpu.VMEM)),
        input_output_aliases={i: 2 + i for i in range(n)},
        compiler_params=pltpu.CompilerParams(has_side_effects=_EFFECT),
    )(*[_in_hbm(a) for a in bufs], *deps)
    return outs[0], outs[1], list(outs[2:2 + len(srcs)]), list(outs[2 + len(srcs):2 + n]), outs[-1]


def _copies_wait(name, plan, send_sems, recv_sems, srcs, lands, after):
    n = len(srcs) + len(lands)
    after = list(after)

    def body(*refs):
        s_sems, r_sems = refs[n], refs[n + 1]
        for k, (src, dst, dev) in enumerate(plan(refs[:len(srcs)], refs[len(srcs):n])):
            if dev is None:
                pltpu.make_async_copy(src, dst, r_sems.at[k]).wait()
                continue
            cp = pltpu.make_async_remote_copy(src_ref=src, dst_ref=dst, send_sem=s_sems.at[k],
                                              recv_sem=r_sems.at[k], device_id=dev, device_id_type=MESH)
            cp.wait_send()
            cp.wait_recv()

    bufs = list(srcs) + list(lands)
    outs = pl.pallas_call(
        body, name=name,
        out_shape=tuple(pltpu.HBM(a.shape, a.dtype) for a in bufs),
        in_specs=[_HBM] * n + [_SEM, _SEM] + [pl.BlockSpec(memory_space=pl.ANY)] * len(after),
        out_specs=tuple([_HBM] * n),
        input_output_aliases={i: i for i in range(n)},
        compiler_params=pltpu.CompilerParams(has_side_effects=_EFFECT),
    )(*bufs, send_sems, recv_sems, *after)
    return list(outs[:len(srcs)]), list(outs[len(srcs):])


def _with_own_slot(block, slot, nslots=NCHIP):
    land = lax.empty((nslots,) + block.shape, block.dtype)
    return lax.dynamic_update_slice(land, block[None], (slot,) + (0,) * block.ndim)


def _my_half(land, slot, c):
    half = land.shape[1] // 2
    return land.at[slot, pl.ds(pl.multiple_of(c * half, 16), half)]


def _gather_ici_plan(srcs, lands):
    x, y, c, s = _place()
    return [(_my_half(land, s, c), _my_half(land, s, c), (*_chip_peer(x, y, j), c))
            for land in lands for j in range(1, NCHIP)]


def _gather_d2d_plan(srcs, lands):
    x, y, c, s = _place()
    return [(_my_half(land, s ^ j, c), _my_half(land, s ^ j, c), (x, y, 1 - c))
            for land in lands for j in range(1, NCHIP)]


def _exchange_plan(srcs, lands):
    x, y, c, s = _place()
    return [(src.at[s ^ j], land.at[s], (*_chip_peer(x, y, j), c) if j else None)
            for src, land in zip(srcs, lands) for j in range(NCHIP)]


def _sibling_plan(srcs, lands):
    x, y, c, _ = _place()
    return [(src, land, (x, y, 1 - c)) for src, land in zip(srcs, lands)]


def _everyone_plan(srcs, lands):
    x, y, c, _ = _place()
    me = 4 * x + 2 * y + c
    return [(srcs[0], lands[0].at[me], (*_chip_peer(x, y, j >> 1), 1 - c if j & 1 else c)) for j in range(1, 8)]


def _flatten_small(parts):
    flat = jnp.concatenate([p.reshape(-1) for p in parts])
    n = flat.shape[0]
    rows = -(-n // (128 * SMALL_TILE)) * SMALL_TILE
    return jnp.pad(flat, (0, rows * 128 - n)).reshape(rows, 128)


def _unflatten_small(buf, like):
    flat = buf.reshape(-1)
    out, at = [], 0
    for p in like:
        out.append(flat[at:at + p.size].reshape(p.shape))
        at += p.size
    return out


def kernel(x, pre_norm_g, w_in, lambda_re, lambda_im, log_dt, b_re, b_im, c_re, c_im, d_skip, w_glu, b_glu, w_branch_s, w_branch_a, w_out, post_norm_g, loss_target, m_pre_norm_g, m_w_in, m_lambda_re, m_lambda_im, m_log_dt, m_b_re, m_b_im, m_c_re, m_c_im, m_d_skip, m_w_glu, m_b_glu, m_w_branch_s, m_w_branch_a, m_w_out, m_post_norm_g, v_pre_norm_g, v_w_in, v_lambda_re, v_lambda_im, v_log_dt, v_b_re, v_b_im, v_c_re, v_c_im, v_d_skip, v_w_glu, v_b_glu, v_w_branch_s, v_w_branch_a, v_w_out, v_post_norm_g):
    nb = DEPTH * NCB
    lre_c = lambda_re.reshape(nb, 1, HW)
    lim_c = lambda_im.reshape(nb, 1, HW)
    ldt_c = jnp.broadcast_to(log_dt[:, :, None], (DEPTH, 32, 64)).reshape(nb, 1, HW)
    b_rows = lambda t: t.reshape(DEPTH, NCB, 8, 64, 16).transpose(0, 1, 4, 2, 3).reshape(nb, 16, HW)
    c_rows = lambda t: t.reshape(DEPTH, NCB, 8, 16, 64).transpose(0, 1, 3, 2, 4).reshape(nb, 16, HW)
    bre_c, bim_c = b_rows(b_re), b_rows(b_im)
    ar, ai, wb, wct = _s5_prep(lre_c, lim_c, ldt_c, bre_c, bim_c, c_rows(c_re), c_rows(c_im))
    ar4 = ar.reshape(DEPTH, NCB, 1, HW)
    ai4 = ai.reshape(DEPTH, NCB, 1, HW)
    wb = wb.reshape(DEPTH, NCB, 128, SW)
    wct = wct.reshape(DEPTH, NCB, 128, SW)

    chip = 2 * lax.axis_index("x") + lax.axis_index("y")
    device = 2 * chip + lax.axis_index("c")

    def gather_start(l, dep):
        blocks = [w_in[l], w_glu[l], w_branch_s[l], w_branch_a[l], w_out[l]]
        if l == 0:
            blocks = lax.optimization_barrier(blocks)
        lands = [_with_own_slot(a.astype(BF16), chip) for a in blocks]
        return _copies_start(f"gather_start_{l}", _gather_ici_plan, 15, [], lands, dep)

    def gather_forward(l, flight, after):
        send, recv, _, lands, _ = flight
        _, lands = _copies_wait(f"gather_wait_{l}", _gather_ici_plan, send, recv, [], lands, after)
        return _copies_start(f"forward_start_{l}", _gather_d2d_plan, 15, [], lands)

    def gather_done(l, forward, after):
        send, recv, _, lands, _ = forward
        return _copies_wait(f"forward_wait_{l}", _gather_d2d_plan, send, recv, [], lands, after)[1]

    small_w = [pre_norm_g, lambda_re, lambda_im, log_dt, b_re, b_im, c_re, c_im, d_skip, b_glu, post_norm_g]
    small_m = [m_pre_norm_g, m_lambda_re, m_lambda_im, m_log_dt, m_b_re, m_b_im, m_c_re, m_c_im, m_d_skip,
               m_b_glu, m_post_norm_g]
    small_v = [v_pre_norm_g, v_lambda_re, v_lambda_im, v_log_dt, v_b_re, v_b_im, v_c_re, v_c_im, v_d_skip,
               v_b_glu, v_post_norm_g]
    flat_w, flat_m, flat_v = _flatten_small(small_w), _flatten_small(small_m), _flatten_small(small_v)

    xs = [x[0]]
    saved = []
    forward = gather_forward(0, gather_start(0, None), [xs[0], flat_w, flat_m, ar, ai, wb, wct])
    for l in range(DEPTH):
        w4, wglu4, wbs4, wba4, wout4 = gather_done(l, forward, [xs[l]] + ([flat_v] if l == 0 else []))
        started = None
        if l + 1 < DEPTH:
            flight = gather_start(l + 1, wglu4)
            started = flight[4]
        wglu = wglu4.reshape(SSM, SSM)
        wout = wout4.reshape(D, D)
        wb4, wc4 = wb[l], wct[l]
        g1 = pre_norm_g[l].reshape(1, D)
        g2 = post_norm_g[l].reshape(1, D)
        dsk = d_skip[l].reshape(1, SSM)
        bgl = b_glu[l].reshape(1, SSM)
        proj, ht = _rms_proj_fwd(xs[l], g1, w4, started)
        states, yraw = _s5_scan_fwd(proj, wb4, wc4, ar4[l], ai4[l])
        o, lse = _attn_core_fwd(proj)
        forwarded = None
        if l + 1 < DEPTH:
            forward = gather_forward(l + 1, flight, [yraw, o])
            forwarded = forward[4]
        ys = _s5_tail_fwd(yraw, proj, dsk, wglu, bgl)
        ya = _attn_mix_fwd(o, lse, proj)
        xs.append(_merge_out_fwd(ys, ya, proj, wbs4, wba4, wout, g2, xs[l], forwarded))
        saved.append((w4, wglu, wbs4, wba4, wout, wb4, wc4, g1, g2, dsk, bgl, proj, ht, states, yraw, ys, ya, o, lse))

    loss_part, dx = _loss_head(xs[DEPTH], loss_target[0])
    loss = lax.psum(loss_part[0, 0], ("x", "y", "c"))

    big_w = (w_in, w_glu.reshape(DEPTH, 128, SSM), w_branch_s, w_branch_a, w_out)
    big_m = (m_w_in, m_w_glu, m_w_branch_s, m_w_branch_a, m_w_out)
    big_v = (v_w_in, v_w_glu, v_w_branch_s, v_w_branch_a, v_w_out)
    big_out = [None] * 5
    small = {k: [None] * DEPTH for k in ("g1", "da", "dbb", "dcc", "dsk", "bgl", "g2")}
    ici = [None] * DEPTH
    d2d = [None] * DEPTH

    def exchange_start(l, parts):
        lands = [lax.empty(p.shape, p.dtype) for p in parts]
        ici[l] = _copies_start(f"exchange_start_{l}", _exchange_plan, 20, parts, lands)

    def handoff_start(l, after):
        send, recv, srcs, lands, _ = ici[l]
        _, own = _copies_wait(f"exchange_wait_{l}", _exchange_plan, send, recv, srcs, lands, after)
        d2d[l] = _copies_start(f"handoff_start_{l}", _sibling_plan, 5, own,
                               [lax.empty(a.shape, a.dtype) for a in own])

    def update(l, after):
        send, recv, own, lands, _ = d2d[l]
        own, sib = _copies_wait(f"handoff_wait_{l}", _sibling_plan, send, recv, own, lands, after)
        for k in range(5):
            big_out[k] = _adamw_big(l, big_w[k], big_m[k], big_v[k], own[k], sib[k], big_out[k])

    for l in reversed(range(DEPTH)):
        w4, wglu, wbs4, wba4, wout, wb4, wc4, g1, g2, dsk, bgl, proj, ht, states, yraw, ys, ya, o, lse = saved[l]
        started = None
        if l + 1 < DEPTH:
            started = ici[l + 1][4] + d2d[l + 2][4] if l + 2 < DEPTH else ici[l + 1][4]
        dys, dya, dproj, dwbs, dwba, dwout, dg2 = _merge_out_bwd(dx, ys, ya, proj, wbs4, wba4, wout, g2, started)
        dyraw, du_skip, dproj, dwglu, dbgl, ddsk = _s5_tail_bwd(dys, yraw, proj, dsk, wglu, bgl, dproj)
        dproj, dbb, dcc, da = _s5_scan_bwd(dyraw, states, proj, du_skip, wb4, wc4, ar4[l], ai4[l], dproj)
        do, cterm, dproj = _attn_mix_bwd(dya, o, lse, proj, dproj)
        dproj = _attn_core_bwd(proj, do, lse, cterm, dproj)
        dh, dwin = _proj_bwd(dproj, ht, w4)
        parts = [dwin, dwglu.reshape(NCHIP, 128, SSM), dwbs, dwba, dwout.reshape(NCHIP, D // NCHIP, D)]
        exchange_start(l, parts)
        dx, dg1 = _rms_bwd(dh, xs[l], g1, dx, ici[l][4])
        if l + 1 < DEPTH:
            handoff_start(l + 1, [ici[l][4]])
        small["g1"][l], small["g2"][l], small["dsk"][l], small["bgl"][l] = dg1, dg2, ddsk, dbgl
        small["da"][l], small["dbb"][l], small["dcc"][l] = da, dbb, dcc

    dlre, dlim, dldt, dbre_c, dbim_c = _s5_prep_bwd(
        lre_c, lim_c, ldt_c, bre_c, bim_c, jnp.stack(small["da"]).reshape(nb, 1, SW),
        jnp.stack(small["dbb"]).reshape(nb, 16, SW))
    dcc = jnp.stack(small["dcc"]).reshape(nb, 16, SW)
    b_back = lambda t: t.reshape(DEPTH, NCB, 16, 8, 64).transpose(0, 1, 3, 4, 2).reshape(DEPTH, 32, 64, 16)
    c_back = lambda t: t.reshape(DEPTH, NCB, 16, 8, 64).transpose(0, 1, 3, 2, 4).reshape(DEPTH, 32, 16, 64)
    small_g = [jnp.stack(small["g1"]).reshape(DEPTH, D), dlre.reshape(DEPTH, 32, 64), dlim.reshape(DEPTH, 32, 64),
               dldt.reshape(DEPTH, 32, 64).sum(-1), b_back(dbre_c), b_back(dbim_c), c_back(dcc[:, :, :HW]),
               c_back(dcc[:, :, HW:]),
               jnp.stack(small["dsk"]).reshape(DEPTH, SSM), jnp.stack(small["bgl"]).reshape(DEPTH, SSM),
               jnp.stack(small["g2"]).reshape(DEPTH, D)]
    part = _flatten_small(small_g).astype(BF16)
    send, recv, srcs, lands, token = _copies_start("small_start", _everyone_plan, 7, [part],
                                                   [_with_own_slot(part, device, 8)])
    update(DEPTH - 1, [token, ici[0][4]])
    for l in range(DEPTH - 2, 0, -1):
        update(l, [out[0] for out in big_out])
    handoff_start(0, [out[0] for out in big_out])
    update(0, [d2d[0][4]])
    _, (parts8,) = _copies_wait("small_wait", _everyone_plan, send, recv, srcs, lands, [out[0] for out in big_out])
    g_flat, d_flat, nm_flat, nv_flat = _adamw_small(flat_w, parts8, flat_m, flat_v)
    sg = _unflatten_small(g_flat, small_w)
    sd = _unflatten_small(d_flat, small_w)
    snm = _unflatten_small(nm_flat, small_w)
    snv = _unflatten_small(nv_flat, small_w)

    def ordered(sm, which):
        bg = [big_out[k][which] for k in range(5)]
        bg[1] = bg[1].reshape(DEPTH, 128, SSM)
        return [sm[0], bg[0], sm[1], sm[2], sm[3], sm[4], sm[5], sm[6], sm[7], sm[8], bg[1], sm[9], bg[2], bg[3],
                bg[4], sm[10]]

    return (loss, dx[None], *ordered(sg, 0), *ordered(sd, 1), *ordered(snm, 2), *ordered(snv, 3))
```

```python
import math

import jax
import jax.numpy as jnp
from jax import lax
from jax.experimental import pallas as pl
from jax.experimental.pallas import tpu as pltpu

F32 = jnp.float32
BF16 = jnp.bfloat16
MESH = pl.DeviceIdType.MESH

DEPTH = 4
L = 2048
D = 1024
NCOL = 8192
SSM = 512
AW = 512
QKV = 1536
RMS_EPS = 1e-6
NCHIP = 4

COL_U, COL_ZS, COL_Q, COL_K, COL_V, COL_ZA, COL_GS, COL_GA = 0, 4, 8, 20, 32, 44, 48, 56

NSEG = 8
SEG = L // NSEG
NCB = 4
HW = 512
SW = 2 * HW

HEAD = 128
DILATIONS = (1, 4, 16)
SCALE = HEAD ** -0.5
NEG = -1e30

ADAM_LR, ADAM_B1, ADAM_B2, ADAM_EPS, ADAM_WD, ADAM_STEP = 0.001, 0.9, 0.999, 1e-08, 0.01, 10


def _sds(shape, dtype=F32):
    return jax.ShapeDtypeStruct(shape, dtype)


def _pcall(body, *, name, out_shape, grid=None, in_specs=None, out_specs=None, scratch=(), vmem_mb=None,
           aliases=None):
    params = {}
    if vmem_mb is not None:
        params["vmem_limit_bytes"] = vmem_mb << 20
    kw = {}
    if grid is not None:
        kw["grid"] = grid
    if in_specs is not None:
        kw["in_specs"] = in_specs
    if out_specs is not None:
        kw["out_specs"] = out_specs
    return pl.pallas_call(body, name=name, out_shape=out_shape, scratch_shapes=list(scratch),
                          compiler_params=pltpu.CompilerParams(**params),
                          input_output_aliases=aliases or {}, **kw)


def _dot(a, b):
    return jnp.dot(a, b, preferred_element_type=F32)


def _dot_nt(a, b):
    return lax.dot_general(a, b, (((1,), (1,)), ((), ())), preferred_element_type=F32)


def _dot_tn(a, b):
    return lax.dot_general(a, b, (((0,), (0,)), ((), ())), preferred_element_type=F32)


def _sigmoid(x):
    return jax.nn.sigmoid(x)


_GELU_K = math.sqrt(2.0 / math.pi)


def _gelu(x):
    return 0.5 * x * (1.0 + jnp.tanh(_GELU_K * (x + 0.044715 * (x * x * x))))


def _gelu_grad(x):
    t = jnp.tanh(_GELU_K * (x + 0.044715 * (x * x * x)))
    return 0.5 * (1.0 + t) + 0.5 * x * (1.0 - t * t) * (_GELU_K * (1.0 + 3.0 * 0.044715 * (x * x)))


def _token_operand(dep):
    if dep is None:
        return [], []
    return [dep], [pl.BlockSpec(memory_space=pl.ANY)]


def _rms_proj_fwd(x, g, w4, dep=None):
    tl, tn = 512, 1024
    ni = L // tl
    deps, dep_specs = _token_operand(dep)

    def body(x_ref, g_ref, w_ref, *rest):
        o_ref, h_ref = rest[len(deps):]
        rows = pl.ds(pl.multiple_of(pl.program_id(1) * tl, tl), tl)

        @pl.when(pl.program_id(0) == 0)
        def _():
            xx = x_ref[...]
            inv = lax.rsqrt(jnp.mean(xx * xx, axis=-1, keepdims=True) + RMS_EPS)
            h_ref[rows, :] = (xx * inv * g_ref[...]).astype(BF16)

        o_ref[...] = _dot(h_ref[rows, :], w_ref[...])

    proj, h = _pcall(
        body, name="rms_proj_fwd", grid=(NCOL // tn, ni),
        in_specs=[pl.BlockSpec((tl, D), lambda j, i: (jnp.where(j == 0, i, ni - 1), 0)),
                  pl.BlockSpec((1, D), lambda j, i: (0, 0)),
                  pl.BlockSpec((None, D, tn), lambda j, i: (lax.div(j, 2), 0, lax.rem(j, 2)))] + dep_specs,
        out_specs=[pl.BlockSpec((tl, tn), lambda j, i: (i, j)), pl.BlockSpec((L, D), lambda j, i: (0, 0))],
        out_shape=[_sds((L, NCOL)), _sds((L, D), BF16)], vmem_mb=48)(x, g, w4, *deps)

    def transpose(h_ref, ht_ref):
        ht_ref[...] = h_ref[...].astype(F32).T.astype(BF16)

    ht = _pcall(transpose, name="transpose_h", grid=(ni,), in_specs=[pl.BlockSpec((tl, D), lambda i: (i, 0))],
                out_specs=pl.BlockSpec((D, tl), lambda i: (0, i)), out_shape=_sds((D, L), BF16), vmem_mb=40)(h)
    return proj, ht


def _proj_bwd(dproj, ht, w4):
    tn = 1024

    def body(dp_ref, ht_ref, w_ref, dh_ref, dw_ref):
        @pl.when(pl.program_id(0) == 0)
        def _():
            dh_ref[...] = jnp.zeros((L, D), F32)

        dw_ref[...] = _dot(ht_ref[...], dp_ref[...]).astype(BF16)
        dh_ref[...] += _dot_nt(dp_ref[...], w_ref[...])

    return _pcall(
        body, name="proj_bwd", grid=(NCOL // tn,),
        in_specs=[pl.BlockSpec((L, tn), lambda j: (0, j)),
                  pl.BlockSpec((D, L), lambda j: (0, 0)),
                  pl.BlockSpec((None, D, tn), lambda j: (lax.div(j, 2), 0, lax.rem(j, 2)))],
        out_specs=[pl.BlockSpec((L, D), lambda j: (0, 0)),
                   pl.BlockSpec((None, D, tn), lambda j: (lax.div(j, 2), 0, lax.rem(j, 2)))],
        out_shape=[_sds((L, D)), _sds((NCHIP, D, NCOL // NCHIP), BF16)], vmem_mb=56)(dproj, ht, w4)


def _rms_bwd(dh, x, g, dxn, dep=None):
    tl = 512
    deps, dep_specs = _token_operand(dep)

    def body(dh_ref, x_ref, g_ref, dxn_ref, *rest):
        dx_ref, dg_ref = rest[len(deps):]
        i = pl.program_id(0)
        xx = x_ref[...]
        inv = lax.rsqrt(jnp.mean(xx * xx, axis=-1, keepdims=True) + RMS_EPS)
        nrm = xx * inv
        dh_v = dh_ref[...]
        dn = dh_v * g_ref[...]
        dx_ref[...] = dxn_ref[...] + inv * (dn - nrm * jnp.mean(dn * nrm, axis=-1, keepdims=True))
        part = jnp.sum(dh_v * nrm, axis=0, keepdims=True)

        @pl.when(i == 0)
        def _():
            dg_ref[...] = part

        @pl.when(i > 0)
        def _():
            dg_ref[...] += part

    row = pl.BlockSpec((tl, D), lambda i: (i, 0))
    vec = pl.BlockSpec((1, D), lambda i: (0, 0))
    return _pcall(body, name="rms_bwd", grid=(L // tl,), in_specs=[row, row, vec, row] + dep_specs,
                  out_specs=[row, vec], out_shape=[_sds((L, D)), _sds((1, D))], vmem_mb=40)(dh, x, g, dxn, *deps)


def _s5_param_math(lre, lim, ldt, bre, bim):
    lr = jnp.minimum(lre, -1e-4)
    dt = jnp.exp(ldt)
    mag = jnp.exp(lr * dt)
    ar = mag * jnp.cos(lim * dt)
    ai = mag * jnp.sin(lim * dt)
    den = lr * lr + lim * lim
    nr = ar - 1.0
    qr = (nr * lr + ai * lim) / den
    qi = (ai * lr - nr * lim) / den
    return ar, ai, qr * bre - qi * bim, qr * bim + qi * bre


def _group_of_lane(width):
    lane = lax.broadcasted_iota(jnp.int32, (16, width), 1)
    return lax.shift_right_logical(jnp.bitwise_and(lane, HW - 1), 6)


def _vec_spec():
    return pl.BlockSpec((None, 1, HW), lambda n: (n, 0, 0))


def _mat_spec():
    return pl.BlockSpec((None, 16, HW), lambda n: (n, 0, 0))


def _s5_prep(lre, lim, ldt, bre, bim, cre, cim):
    n = lre.shape[0]

    def body(lre_ref, lim_ref, ldt_ref, bre_ref, bim_ref, cre_ref, cim_ref, ar_ref, ai_ref, wb_ref, wct_ref):
        ar, ai, bbr, bbi = _s5_param_math(lre_ref[...], lim_ref[...], ldt_ref[...], bre_ref[...], bim_ref[...])
        ar_ref[...] = ar
        ai_ref[...] = ai
        group = _group_of_lane(SW)
        bb = jnp.concatenate([bbr, bbi], axis=1)
        cc = jnp.concatenate([cre_ref[...], -cim_ref[...]], axis=1)
        for g in range(8):
            rows = pl.ds(16 * g, 16)
            wb_ref[rows, :] = jnp.where(group == g, bb, 0.0).astype(BF16)
            wct_ref[rows, :] = jnp.where(group == g, cc, 0.0).astype(BF16)

    wide = pl.BlockSpec((None, 128, SW), lambda n: (n, 0, 0))
    return _pcall(body, name="s5_prep", grid=(n,),
                  in_specs=[_vec_spec()] * 3 + [_mat_spec()] * 4,
                  out_specs=[_vec_spec(), _vec_spec(), wide, wide],
                  out_shape=[_sds((n, 1, HW)), _sds((n, 1, HW)), _sds((n, 128, SW), BF16), _sds((n, 128, SW), BF16)])(
                      lre, lim, ldt, bre, bim, cre, cim)


def _s5_prep_bwd(lre, lim, ldt, bre, bim, ga, gbb):
    n = lre.shape[0]

    def body(lre_ref, lim_ref, ldt_ref, bre_ref, bim_ref, ga_ref, gbb_ref,
             dlre_ref, dlim_ref, dldt_ref, dbre_ref, dbim_ref):
        _, vjp = jax.vjp(_s5_param_math, lre_ref[...], lim_ref[...], ldt_ref[...], bre_ref[...], bim_ref[...])
        ga, gbb = ga_ref[...], gbb_ref[...]
        dlre, dlim, dldt, dbre, dbim = vjp((ga[:, :HW], ga[:, HW:], gbb[:, :HW], gbb[:, HW:]))
        dlre_ref[...] = dlre
        dlim_ref[...] = dlim
        dldt_ref[...] = dldt
        dbre_ref[...] = dbre
        dbim_ref[...] = dbim

    return _pcall(body, name="s5_prep_bwd", grid=(n,),
                  in_specs=[_vec_spec()] * 3 + [_mat_spec()] * 2 +
                  [pl.BlockSpec((None, 1, SW), lambda n: (n, 0, 0)), pl.BlockSpec((None, 16, SW), lambda n: (n, 0, 0))],
                  out_specs=[_vec_spec()] * 3 + [_mat_spec()] * 2,
                  out_shape=[_sds((n, 1, HW))] * 3 + [_sds((n, 16, HW))] * 2)(lre, lim, ldt, bre, bim, ga, gbb)


def _diag_blocks(full):
    group = _group_of_lane(SW)
    out = jnp.where(group == 0, full[0:16, :], 0.0)
    for g in range(1, 8):
        out = out + jnp.where(group == g, full[16 * g:16 * g + 16, :], 0.0)
    return out


def _permute_rows(dst_ref, src_ref):
    for s in range(NSEG):
        dst_ref[pl.ds(s, SEG, stride=NSEG), :] = src_ref[pl.ds(s * SEG, SEG), :]


def _unpermute_rows(src_ref, s):
    return src_ref[pl.ds(s, SEG, stride=NSEG), :]


def _row_of(slab, rows, j):
    return jnp.sum(jnp.where(rows == j, slab, 0.0), axis=0, keepdims=True)


def _s5_scan_fwd(proj, wb4, wc4, ar4, ai4):
    def body(u_ref, wb_ref, wc_ref, ar_ref, ai_ref, x_ref, y_ref, up_ref, d_ref, yp_ref):
        _permute_rows(up_ref, u_ref)
        d_ref[...] = _dot(up_ref[...].astype(BF16), wb_ref[...])
        ar = jnp.broadcast_to(ar_ref[...], (NSEG, HW))
        ai = jnp.broadcast_to(ai_ref[...], (NSEG, HW))
        zero = jnp.zeros((NSEG, HW), F32)

        def drive(t):
            row = pl.multiple_of(t * NSEG, NSEG)
            dd = d_ref[pl.ds(row, NSEG), :]
            return row, dd[:, :HW], dd[:, HW:]

        def local_step(t, c):
            xr, xi = c
            _, dr, di = drive(t)
            return ar * xr - ai * xi + dr, ar * xi + ai * xr + di

        fr, fi = lax.fori_loop(0, SEG, local_step, (zero, zero), unroll=8)
        pr, pi_ = ar_ref[...], ai_ref[...]
        for _ in range(int(math.log2(SEG))):
            pr, pi_ = pr * pr - pi_ * pi_, 2.0 * pr * pi_
        rows = lax.broadcasted_iota(jnp.int32, (NSEG, HW), 0)
        cr, ci = zero, zero
        for j in range(NSEG - 1):
            sr, si = _row_of(cr, rows, j), _row_of(ci, rows, j)
            gr, gi = _row_of(fr, rows, j), _row_of(fi, rows, j)
            nr = pr * sr - pi_ * si + gr
            ni = pr * si + pi_ * sr + gi
            cr = jnp.where(rows == j + 1, nr, cr)
            ci = jnp.where(rows == j + 1, ni, ci)

        def true_step(t, c):
            xr, xi = c
            row, dr, di = drive(t)
            nr = ar * xr - ai * xi + dr
            ni = ar * xi + ai * xr + di
            x_ref[pl.ds(row, NSEG), :] = jnp.concatenate([nr, ni], axis=1)
            return nr, ni

        lax.fori_loop(0, SEG, true_step, (cr, ci), unroll=8)
        yp_ref[...] = _dot_nt(x_ref[...].astype(BF16), wc_ref[...])
        for s in range(NSEG):
            y_ref[pl.ds(s * SEG, SEG), :] = _unpermute_rows(yp_ref, s)

    return _pcall(
        body, name="s5_scan_fwd", grid=(NCB,),
        in_specs=[pl.BlockSpec((L, 128), lambda b: (0, COL_U + b)),
                  pl.BlockSpec((None, 128, SW), lambda b: (b, 0, 0)),
                  pl.BlockSpec((None, 128, SW), lambda b: (b, 0, 0)),
                  pl.BlockSpec((None, 1, HW), lambda b: (b, 0, 0)),
                  pl.BlockSpec((None, 1, HW), lambda b: (b, 0, 0))],
        out_specs=[pl.BlockSpec((None, L, SW), lambda b: (b, 0, 0)),
                   pl.BlockSpec((L, 128), lambda b: (0, b))],
        out_shape=[_sds((NCB, L, SW)), _sds((L, SSM))],
        scratch=[pltpu.VMEM((L, 128), F32), pltpu.VMEM((L, SW), F32), pltpu.VMEM((L, 128), F32)],
        vmem_mb=56)(proj, wb4, wc4, ar4, ai4)


def _s5_scan_bwd(dy, xs, proj, du_skip, wb4, wc4, ar4, ai4, dproj):
    def body(dy_ref, x_ref, u_ref, dus_ref, wb_ref, wc_ref, ar_ref, ai_ref, _,
             du_ref, dbb_ref, dcc_ref, da_ref, dyp_ref, up_ref, g_ref, dup_ref):
        _permute_rows(dyp_ref, dy_ref)
        _permute_rows(up_ref, u_ref)
        dyp = dyp_ref[...].astype(BF16)
        g_ref[...] = _dot(dyp, wc_ref[...])
        dcc = _diag_blocks(_dot_tn(dyp, x_ref[...].astype(BF16)))
        dcc_ref[...] = jnp.concatenate([dcc[:, :HW], -dcc[:, HW:]], axis=1)
        ar = jnp.broadcast_to(ar_ref[...], (NSEG, HW))
        ai = jnp.broadcast_to(ai_ref[...], (NSEG, HW))
        zero = jnp.zeros((NSEG, HW), F32)

        def load(t):
            row = pl.multiple_of(t * NSEG, NSEG)
            gg = g_ref[pl.ds(row, NSEG), :]
            return row, gg[:, :HW], gg[:, HW:]

        def local_step(k, c):
            lr, li = c
            _, gr, gi = load(SEG - 1 - k)
            return ar * lr + ai * li + gr, ar * li - ai * lr + gi

        fr, fi = lax.fori_loop(0, SEG, local_step, (zero, zero), unroll=8)
        pr, pi_ = ar_ref[...], -ai_ref[...]
        for _ in range(int(math.log2(SEG))):
            pr, pi_ = pr * pr - pi_ * pi_, 2.0 * pr * pi_
        rows = lax.broadcasted_iota(jnp.int32, (NSEG, HW), 0)
        cr, ci = zero, zero
        for j in range(NSEG - 1, 0, -1):
            sr, si = _row_of(cr, rows, j), _row_of(ci, rows, j)
            gr, gi = _row_of(fr, rows, j), _row_of(fi, rows, j)
            nr = pr * sr - pi_ * si + gr
            ni = pr * si + pi_ * sr + gi
            cr = jnp.where(rows == j - 1, nr, cr)
            ci = jnp.where(rows == j - 1, ni, ci)

        def true_step(k, c):
            lr, li, gar, gai = c
            t = SEG - 1 - k
            row, gr, gi = load(t)
            nr = ar * lr + ai * li + gr
            ni = ar * li - ai * lr + gi
            g_ref[pl.ds(row, NSEG), :] = jnp.concatenate([nr, ni], axis=1)
            prow = pl.multiple_of(jnp.maximum(t - 1, 0) * NSEG, NSEG)
            xp = x_ref[pl.ds(prow, NSEG), :]
            live = (t > 0).astype(F32)
            xr, xi = xp[:, :HW] * live, xp[:, HW:] * live
            return nr, ni, gar + (nr * xr + ni * xi), gai + (ni * xr - nr * xi)

        _, _, gar, gai = lax.fori_loop(0, SEG, true_step, (cr, ci, zero, zero), unroll=4)
        l0 = g_ref[1:NSEG, :]
        xl = x_ref[(SEG - 1) * NSEG:(SEG - 1) * NSEG + NSEG - 1, :]
        l0r, l0i, xlr, xli = l0[:, :HW], l0[:, HW:], xl[:, :HW], xl[:, HW:]
        gar_t = jnp.sum(gar, axis=0, keepdims=True) + jnp.sum(l0r * xlr + l0i * xli, axis=0, keepdims=True)
        gai_t = jnp.sum(gai, axis=0, keepdims=True) + jnp.sum(l0i * xlr - l0r * xli, axis=0, keepdims=True)
        da_ref[...] = jnp.concatenate([gar_t, gai_t], axis=1)
        lam = g_ref[...].astype(BF16)
        dbb_ref[...] = _diag_blocks(_dot_tn(up_ref[...].astype(BF16), lam))
        dup_ref[...] = _dot_nt(lam, wb_ref[...])
        for s in range(NSEG):
            sl = pl.ds(s * SEG, SEG)
            du_ref[sl, :] = (_unpermute_rows(dup_ref, s) + dus_ref[sl, :]).astype(BF16)

    col = lambda off: pl.BlockSpec((L, 128), lambda b: (0, off + b))
    return _pcall(
        body, name="s5_scan_bwd", grid=(NCB,),
        in_specs=[col(0), pl.BlockSpec((None, L, SW), lambda b: (b, 0, 0)), col(COL_U), col(0),
                  pl.BlockSpec((None, 128, SW), lambda b: (b, 0, 0)),
                  pl.BlockSpec((None, 128, SW), lambda b: (b, 0, 0)),
                  pl.BlockSpec((None, 1, HW), lambda b: (b, 0, 0)),
                  pl.BlockSpec((None, 1, HW), lambda b: (b, 0, 0)),
                  pl.BlockSpec(memory_space=pl.ANY)],
        out_specs=[col(COL_U), pl.BlockSpec((None, 16, SW), lambda b: (b, 0, 0)),
                   pl.BlockSpec((None, 16, SW), lambda b: (b, 0, 0)),
                   pl.BlockSpec((None, 1, SW), lambda b: (b, 0, 0))],
        out_shape=[_sds((L, NCOL), BF16), _sds((NCB, 16, SW)), _sds((NCB, 16, SW)), _sds((NCB, 1, SW))],
        scratch=[pltpu.VMEM((L, 128), F32), pltpu.VMEM((L, 128), F32), pltpu.VMEM((L, SW), F32),
                 pltpu.VMEM((L, 128), F32)],
        aliases={8: 0}, vmem_mb=56)(dy, xs, proj, du_skip, wb4, wc4, ar4, ai4, dproj)


def _s5_tail_fwd(yraw, proj, dsk, wglu, bglu):
    tl = 512

    def body(y_ref, u_ref, z_ref, dsk_ref, w_ref, b_ref, o_ref):
        y1 = y_ref[...] + dsk_ref[...] * u_ref[...]
        y2 = _gelu(y1)
        gl = _dot(y2.astype(BF16), w_ref[...]) + b_ref[...]
        z = z_ref[...]
        o_ref[...] = (y2 * _sigmoid(gl)) * (z * _sigmoid(z))

    blk = lambda c: pl.BlockSpec((tl, SSM), lambda i: (i, c))
    vec = pl.BlockSpec((1, SSM), lambda i: (0, 0))
    return _pcall(body, name="s5_tail_fwd", grid=(L // tl,),
                  in_specs=[blk(0), blk(0), blk(1), vec, pl.BlockSpec((SSM, SSM), lambda i: (0, 0)), vec],
                  out_specs=blk(0), out_shape=_sds((L, SSM)), vmem_mb=40)(yraw, proj, proj, dsk, wglu, bglu)


def _s5_tail_bwd(dys, yraw, proj, dsk, wglu, bglu, dproj):
    tl = 512
    ni = L // tl

    def body(dys_ref, y_ref, u_ref, z_ref, dsk_ref, w_ref, b_ref, _,
             dy_ref, dus_ref, dz_ref, dw_ref, db_ref, dd_ref, acc_ref):
        i = pl.program_id(0)
        u = u_ref[...]
        y1 = y_ref[...] + dsk_ref[...] * u
        y2 = _gelu(y1)
        y2b = y2.astype(BF16)
        sg = _sigmoid(_dot(y2b, w_ref[...]) + b_ref[...])
        y3 = y2 * sg
        z = z_ref[...]
        sz = _sigmoid(z)
        dys = dys_ref[...]
        dy3 = dys * (z * sz)
        dz_ref[...] = (dys * y3 * (sz * (1.0 + z * (1.0 - sz)))).astype(BF16)
        dgl = (dy3 * y2) * (sg * (1.0 - sg))
        dglb = dgl.astype(BF16)
        dy2 = dy3 * sg + _dot_nt(dglb, w_ref[...])
        dy1 = dy2 * _gelu_grad(y1)
        dy_ref[...] = dy1
        dus_ref[...] = dsk_ref[...] * dy1
        dw = _dot_tn(y2b, dglb)
        db = jnp.sum(dgl, axis=0, keepdims=True)
        dd = jnp.sum(dy1 * u, axis=0, keepdims=True)

        @pl.when(i == 0)
        def _():
            acc_ref[...] = dw
            db_ref[...] = db
            dd_ref[...] = dd

        @pl.when(i > 0)
        def _():
            acc_ref[...] += dw
            db_ref[...] += db
            dd_ref[...] += dd

        @pl.when(i == ni - 1)
        def _():
            dw_ref[...] = acc_ref[...].astype(BF16)

    blk = lambda c: pl.BlockSpec((tl, SSM), lambda i: (i, c))
    vec = pl.BlockSpec((1, SSM), lambda i: (0, 0))
    mat = pl.BlockSpec((SSM, SSM), lambda i: (0, 0))
    return _pcall(body, name="s5_tail_bwd", grid=(ni,),
                  in_specs=[blk(0), blk(0), blk(0), blk(1), vec, mat, vec, pl.BlockSpec(memory_space=pl.ANY)],
                  out_specs=[blk(0), blk(0), blk(COL_ZS // 4), mat, vec, vec],
                  out_shape=[_sds((L, SSM)), _sds((L, SSM)), _sds((L, NCOL), BF16), _sds((SSM, SSM), BF16),
                             _sds((1, SSM)), _sds((1, SSM))],
                  scratch=[pltpu.VMEM((SSM, SSM), F32)], aliases={7: 2},
                  vmem_mb=40)(dys, yraw, proj, proj, dsk, wglu, bglu, dproj)


def _attn_blocks(dil):
    n = L // dil
    return [(i * HEAD * dil + r, r * n + i * HEAD, i == 0) for r in range(dil) for i in range(n // HEAD)]


def _rows(start, dil):
    return pl.ds(start, HEAD) if dil == 1 else pl.ds(start, HEAD, stride=dil)


def _residue_rows(r, dil):
    n = L // dil
    return pl.ds(0, L) if dil == 1 else pl.ds(r, n, stride=dil)


def _gather_residues(pairs, dil, dtype=BF16):
    n = L // dil
    for src, dst in pairs:
        for r in range(dil):
            dst[pl.ds(r * n, n), :] = src[_residue_rows(r, dil), :].astype(dtype)


def _band_masks():
    ri = lax.broadcasted_iota(jnp.int32, (HEAD, HEAD), 0)
    ci = lax.broadcasted_iota(jnp.int32, (HEAD, HEAD), 1)
    ri2 = lax.broadcasted_iota(jnp.int32, (HEAD, 2 * HEAD), 0)
    ci2 = lax.broadcasted_iota(jnp.int32, (HEAD, 2 * HEAD), 1)
    return ci <= ri, jnp.logical_and(ci2 >= ri2, ci2 - HEAD <= ri2)


def _qkv_specs(index):
    def spec(off):
        return pl.BlockSpec((L, HEAD), lambda gi, h: (0, off + index(gi, h)))
    return [spec(off) for off in (COL_Q, COL_K, COL_V)]


def _attn_core_fwd(proj):
    def body(q_ref, k_ref, v_ref, o_ref, lse_ref, qp_ref, kp_ref, vp_ref):
        gi = pl.program_id(0)
        m_cur, m_both = _band_masks()
        for g, dil in enumerate(DILATIONS):
            @pl.when(gi == g)
            def _(dil=dil):
                _gather_residues(((q_ref, qp_ref), (k_ref, kp_ref), (v_ref, vp_ref)), dil)
                for st, base, first in _attn_blocks(dil):
                    q = qp_ref[pl.ds(base, HEAD), :]
                    keys = pl.ds(base, HEAD) if first else pl.ds(base - HEAD, 2 * HEAD)
                    s = jnp.where(m_cur if first else m_both, _dot_nt(q, kp_ref[keys, :]) * SCALE, NEG)
                    mx = jnp.max(s, axis=-1, keepdims=True)
                    p = jnp.exp(s - mx)
                    den = jnp.sum(p, axis=-1, keepdims=True)
                    r = _rows(st, dil)
                    o_ref[r, :] = _dot(p.astype(BF16), vp_ref[keys, :]) / den
                    lse_ref[r, :] = jnp.broadcast_to(mx + jnp.log(den), (HEAD, HEAD))

    idx = lambda gi, h: gi * 4 + h
    out = pl.BlockSpec((L, HEAD), lambda gi, h: (0, gi * 4 + h))
    return _pcall(body, name="attn_core_fwd", grid=(3, 4), in_specs=_qkv_specs(idx), out_specs=[out, out],
                  out_shape=[_sds((L, QKV)), _sds((L, QKV))], scratch=[pltpu.VMEM((L, HEAD), BF16)] * 3,
                  vmem_mb=40)(proj, proj, proj)


def _attn_mix_fwd(o, lse, proj):
    tl = 512

    def body(o_ref, l_ref, z_ref, y_ref):
        for h in range(4):
            c = [pl.ds((g * 4 + h) * HEAD, HEAD) for g in range(3)]
            ls = [l_ref[:, c[g]] for g in range(3)]
            m = jnp.maximum(jnp.maximum(ls[0], ls[1]), ls[2])
            e = [jnp.exp(ls[g] - m) for g in range(3)]
            y = (e[0] * o_ref[:, c[0]] + e[1] * o_ref[:, c[1]] + e[2] * o_ref[:, c[2]]) / (e[0] + e[1] + e[2])
            z = z_ref[:, pl.ds(h * HEAD, HEAD)]
            y_ref[:, pl.ds(h * HEAD, HEAD)] = y * (z * _sigmoid(z))

    wide = pl.BlockSpec((tl, QKV), lambda i: (i, 0))
    return _pcall(body, name="attn_mix_fwd", grid=(L // tl,),
                  in_specs=[wide, wide, pl.BlockSpec((tl, AW), lambda i: (i, COL_ZA // 4))],
                  out_specs=pl.BlockSpec((tl, AW), lambda i: (i, 0)), out_shape=_sds((L, AW)),
                  vmem_mb=40)(o, lse, proj)


def _attn_mix_bwd(dya, o, lse, proj, dproj):
    tl = 512

    def body(dya_ref, o_ref, l_ref, z_ref, _, do_ref, c_ref, dz_ref):
        for h in range(4):
            c = [pl.ds((g * 4 + h) * HEAD, HEAD) for g in range(3)]
            hs = pl.ds(h * HEAD, HEAD)
            ls = [l_ref[:, c[g]] for g in range(3)]
            m = jnp.maximum(jnp.maximum(ls[0], ls[1]), ls[2])
            e = [jnp.exp(ls[g] - m) for g in range(3)]
            den = e[0] + e[1] + e[2]
            al = [e[g] / den for g in range(3)]
            y = al[0] * o_ref[:, c[0]] + al[1] * o_ref[:, c[1]] + al[2] * o_ref[:, c[2]]
            z = z_ref[:, hs]
            sz = _sigmoid(z)
            dya = dya_ref[:, hs]
            dz_ref[:, hs] = (dya * y * (sz * (1.0 + z * (1.0 - sz)))).astype(BF16)
            dy = dya * (z * sz)
            tot = jnp.sum(dy * y, axis=-1, keepdims=True)
            for g in range(3):
                do_ref[:, c[g]] = al[g] * dy
                c_ref[:, c[g]] = -(al[g] * tot)

    wide = pl.BlockSpec((tl, QKV), lambda i: (i, 0))
    nar = pl.BlockSpec((tl, AW), lambda i: (i, 0))
    za = pl.BlockSpec((tl, AW), lambda i: (i, COL_ZA // 4))
    return _pcall(body, name="attn_mix_bwd", grid=(L // tl,),
                  in_specs=[nar, wide, wide, za, pl.BlockSpec(memory_space=pl.ANY)],
                  out_specs=[wide, wide, za], out_shape=[_sds((L, QKV)), _sds((L, QKV)), _sds((L, NCOL), BF16)],
                  aliases={4: 2}, vmem_mb=48)(dya, o, lse, proj, dproj)


def _attn_core_bwd(proj, do, lse, cc, dproj):
    def body(q_ref, k_ref, v_ref, do_ref, lse_ref, c_ref, _, dp_ref,
             qp_ref, kp_ref, vp_ref, dop_ref, lsep_ref, cp_ref, dqp_ref, dkp_ref, dvp_ref, tok_ref, out_ref, sems):
        gi = pl.program_id(0)
        head = gi * 4 + pl.program_id(1)
        m_cur, m_both = _band_masks()
        dkp_ref[...] = jnp.zeros((L, HEAD), F32)
        dvp_ref[...] = jnp.zeros((L, HEAD), F32)
        copies = []
        for g, dil in enumerate(DILATIONS):
            @pl.when(gi == g)
            def _(dil=dil):
                _gather_residues(((q_ref, qp_ref), (k_ref, kp_ref), (v_ref, vp_ref), (do_ref, dop_ref)), dil)
                _gather_residues(((lse_ref, lsep_ref), (c_ref, cp_ref)), dil, F32)
                for _, base, first in _attn_blocks(dil):
                    mine = pl.ds(base, HEAD)
                    keys = mine if first else pl.ds(base - HEAD, 2 * HEAD)
                    q, do_b, kb = qp_ref[mine, :], dop_ref[mine, :], kp_ref[keys, :]
                    lse_b, c_b = lsep_ref[mine, :], cp_ref[mine, :]
                    if not first:
                        lse_b = jnp.concatenate([lse_b, lse_b], axis=1)
                        c_b = jnp.concatenate([c_b, c_b], axis=1)
                    s = _dot_nt(q, kb) * SCALE
                    p = jnp.where(m_cur if first else m_both, jnp.exp(s - lse_b), 0.0)
                    ds = (p * (_dot_nt(do_b, vp_ref[keys, :]) + c_b) * SCALE).astype(BF16)
                    dqp_ref[mine, :] = _dot(ds, kb)
                    dkp_ref[keys, :] += _dot_tn(ds, q)
                    dvp_ref[keys, :] += _dot_tn(p.astype(BF16), do_b)
                n = L // dil
                for slot, src in enumerate((dqp_ref, dkp_ref, dvp_ref)):
                    for r in range(dil):
                        tok_ref[_residue_rows(r, dil), :] = src[pl.ds(r * n, n), :]
                    out_ref[slot] = tok_ref[...].astype(BF16)

        for slot, off in enumerate((COL_Q, COL_K, COL_V)):
            cols = pl.ds(pl.multiple_of((off + head) * HEAD, HEAD), HEAD)
            copies.append(pltpu.make_async_copy(out_ref.at[slot], dp_ref.at[:, cols], sems.at[slot]))
            copies[-1].start()
        for cp in copies:
            cp.wait()

    idx = lambda gi, h: gi * 4 + h
    blk = pl.BlockSpec((L, HEAD), lambda gi, h: (0, gi * 4 + h))
    hbm = pl.BlockSpec(memory_space=pl.ANY)
    return _pcall(body, name="attn_core_bwd", grid=(3, 4), in_specs=_qkv_specs(idx) + [blk, blk, blk, hbm],
                  out_specs=hbm, out_shape=_sds((L, NCOL), BF16),
                  scratch=[pltpu.VMEM((L, HEAD), BF16)] * 4 + [pltpu.VMEM((L, HEAD), F32)] * 6 +
                  [pltpu.VMEM((3, L, HEAD), BF16), pltpu.SemaphoreType.DMA((3,))],
                  aliases={6: 0}, vmem_mb=48)(proj, proj, proj, do, lse, cc, dproj)


def _merge_fwd_math(ys_b, ya_b, gs, ga, wbs_ref, wba_ref, wout_ref):
    bs = jnp.concatenate([_dot(ys_b, wbs_ref[s]) for s in range(NCHIP)], axis=1)
    ba = jnp.concatenate([_dot(ya_b, wba_ref[s]) for s in range(NCHIP)], axis=1)
    sgs, sga = _sigmoid(gs), _sigmoid(ga)
    merged = sgs * bs + sga * ba
    out = _dot(merged.astype(BF16), wout_ref[...])
    inv = lax.rsqrt(jnp.mean(out * out, axis=-1, keepdims=True) + RMS_EPS)
    return bs, ba, sgs, sga, merged, out, inv


def _merge_specs(tl):
    row = lambda w, c: pl.BlockSpec((tl, w), lambda i: (i, c))
    return [row(SSM, 0), row(AW, 0), row(D, COL_GS // 8), row(D, COL_GA // 8),
            pl.BlockSpec((NCHIP, SSM, D // NCHIP), lambda i: (0, 0, 0)),
            pl.BlockSpec((NCHIP, AW, D // NCHIP), lambda i: (0, 0, 0)),
            pl.BlockSpec((D, D), lambda i: (0, 0)),
            pl.BlockSpec((1, D), lambda i: (0, 0))]


def _merge_out_fwd(ys, ya, proj, wbs4, wba4, wout, g2, x, dep=None):
    tl = 512
    deps, dep_specs = _token_operand(dep)

    def body(ys_ref, ya_ref, gs_ref, ga_ref, wbs_ref, wba_ref, wout_ref, g2_ref, x_ref, *rest):
        o_ref = rest[-1]
        *_, out, inv = _merge_fwd_math(ys_ref[...].astype(BF16), ya_ref[...].astype(BF16), gs_ref[...],
                                       ga_ref[...], wbs_ref, wba_ref, wout_ref)
        o_ref[...] = x_ref[...] + out * inv * g2_ref[...]

    row = pl.BlockSpec((tl, D), lambda i: (i, 0))
    return _pcall(body, name="merge_out_fwd", grid=(L // tl,), in_specs=_merge_specs(tl) + [row] + dep_specs,
                  out_specs=row, out_shape=_sds((L, D)),
                  vmem_mb=48)(ys, ya, proj, proj, wbs4, wba4, wout, g2, x, *deps)


def _merge_out_bwd(dxn, ys, ya, proj, wbs4, wba4, wout, g2, dep=None):
    tl = 256
    ni = L // tl
    cw = D // NCHIP
    deps, dep_specs = _token_operand(dep)

    def body(dxn_ref, ys_ref, ya_ref, gs_ref, ga_ref, wbs_ref, wba_ref, wout_ref, g2_ref, *rest):
        (dys_ref, dya_ref, dgate_ref, dwbs_ref, dwba_ref, dwout_ref, dg2_ref,
         abs_ref, aba_ref, aout_ref) = rest[len(deps):]
        i = pl.program_id(0)

        @pl.when(i == 0)
        def _():
            abs_ref[...] = jnp.zeros(abs_ref.shape, F32)
            aba_ref[...] = jnp.zeros(aba_ref.shape, F32)
            aout_ref[...] = jnp.zeros(aout_ref.shape, F32)
            dg2_ref[...] = jnp.zeros(dg2_ref.shape, F32)

        ys_b, ya_b = ys_ref[...].astype(BF16), ya_ref[...].astype(BF16)
        bs, ba, sgs, sga, merged, out, inv = _merge_fwd_math(ys_b, ya_b, gs_ref[...], ga_ref[...],
                                                             wbs_ref, wba_ref, wout_ref)
        nrm = out * inv
        dxn = dxn_ref[...]
        dg2_ref[...] += jnp.sum(dxn * nrm, axis=0, keepdims=True)
        dn = dxn * g2_ref[...]
        dout = (inv * (dn - nrm * jnp.mean(dn * nrm, axis=-1, keepdims=True))).astype(BF16)
        aout_ref[...] += _dot_tn(merged.astype(BF16), dout)
        dm = _dot_nt(dout, wout_ref[...])
        dbs, dba = dm * sgs, dm * sga
        dgate_ref[:, :D] = (dm * bs * (sgs * (1.0 - sgs))).astype(BF16)
        dgate_ref[:, D:] = (dm * ba * (sga * (1.0 - sga))).astype(BF16)
        dbs_b, dba_b = dbs.astype(BF16), dba.astype(BF16)
        dys = None
        dya = None
        for s in range(NCHIP):
            cs = slice(s * cw, (s + 1) * cw)
            p_s = _dot_nt(dbs_b[:, cs], wbs_ref[s])
            p_a = _dot_nt(dba_b[:, cs], wba_ref[s])
            dys = p_s if dys is None else dys + p_s
            dya = p_a if dya is None else dya + p_a
            abs_ref[s] += _dot_tn(ys_b, dbs_b[:, cs])
            aba_ref[s] += _dot_tn(ya_b, dba_b[:, cs])

        dys_ref[...] = dys
        dya_ref[...] = dya

        @pl.when(i == ni - 1)
        def _():
            dwbs_ref[...] = abs_ref[...].astype(BF16)
            dwba_ref[...] = aba_ref[...].astype(BF16)
            dwout_ref[...] = aout_ref[...].astype(BF16)

    row = lambda w: pl.BlockSpec((tl, w), lambda i: (i, 0))
    w4 = pl.BlockSpec((NCHIP, SSM, cw), lambda i: (0, 0, 0))
    sq = pl.BlockSpec((D, D), lambda i: (0, 0))
    vec = pl.BlockSpec((1, D), lambda i: (0, 0))
    return _pcall(body, name="merge_out_bwd", grid=(ni,), in_specs=[row(D)] + _merge_specs(tl) + dep_specs,
                  out_specs=[row(SSM), row(AW), pl.BlockSpec((tl, 2 * D), lambda i: (i, COL_GS // 16)), w4, w4, sq,
                             vec],
                  out_shape=[_sds((L, SSM)), _sds((L, AW)), _sds((L, NCOL), BF16),
                             _sds((NCHIP, SSM, cw), BF16), _sds((NCHIP, AW, cw), BF16), _sds((D, D), BF16),
                             _sds((1, D))],
                  scratch=[pltpu.VMEM((NCHIP, SSM, cw), F32), pltpu.VMEM((NCHIP, AW, cw), F32),
                           pltpu.VMEM((D, D), F32)],
                  vmem_mb=56)(dxn, ys, ya, proj, proj, wbs4, wba4, wout, g2, *deps)


def _loss_head(y, target):
    tl = 512

    def body(y_ref, t_ref, loss_ref, dy_ref):
        i = pl.program_id(0)
        err = y_ref[...] - t_ref[...]
        dy_ref[...] = err / D
        part = 0.5 * jnp.sum(jnp.mean(err * err, axis=-1, keepdims=True), axis=0, keepdims=True)
        part = jnp.broadcast_to(part, (8, 128))

        @pl.when(i == 0)
        def _():
            loss_ref[...] = part

        @pl.when(i > 0)
        def _():
            loss_ref[...] += part

    row = pl.BlockSpec((tl, D), lambda i: (i, 0))
    return _pcall(body, name="loss_head", grid=(L // tl,), in_specs=[row, row],
                  out_specs=[pl.BlockSpec((8, 128), lambda i: (0, 0)), row],
                  out_shape=[_sds((8, 128)), _sds((L, D))], vmem_mb=40)(y, target)


def _adamw_math(w, g, m, v):
    m = ADAM_B1 * m + (1.0 - ADAM_B1) * g
    v = ADAM_B2 * v + (1.0 - ADAM_B2) * (g * g)
    m_hat = m / (1.0 - ADAM_B1 ** ADAM_STEP)
    v_hat = v / (1.0 - ADAM_B2 ** ADAM_STEP)
    delta = -ADAM_LR * (m_hat / (jnp.sqrt(v_hat) + ADAM_EPS) + ADAM_WD * w)
    return delta, m, v


def _adamw_big(layer, w, m, v, own, sib, prev):
    _, r, c = w.shape
    tr = min(r, 128)

    def body(w_ref, m_ref, v_ref, own_ref, sib_ref, *rest):
        g_ref, d_ref, nm_ref, nv_ref = rest[-4:]
        a = own_ref[0].astype(F32)
        b = sib_ref[0].astype(F32)
        for s in range(1, NCHIP):
            a = a + own_ref[s].astype(F32)
            b = b + sib_ref[s].astype(F32)
        g = a + b
        delta, nm, nv = _adamw_math(w_ref[...], g, m_ref[...], v_ref[...])
        g_ref[...] = g
        d_ref[...] = delta
        nm_ref[...] = nm
        nv_ref[...] = nv

    lay = pl.BlockSpec((None, tr, c), lambda i: (layer, i, 0))
    slots = pl.BlockSpec((NCHIP, tr, c), lambda i: (0, i, 0))
    ins = [w, m, v, own, sib]
    in_specs = [lay, lay, lay, slots, slots]
    aliases = {}
    if prev is not None:
        ins += list(prev)
        in_specs += [pl.BlockSpec(memory_space=pl.ANY)] * 4
        aliases = {5 + k: k for k in range(4)}
    return _pcall(body, name="adamw_big", grid=(r // tr,), in_specs=in_specs, out_specs=[lay] * 4,
                  out_shape=[_sds(w.shape)] * 4, aliases=aliases, vmem_mb=48)(*ins)


SMALL_TILE = 512


def _adamw_small(w, parts, m, v):
    r = w.shape[0]

    def body(w_ref, p_ref, m_ref, v_ref, g_ref, d_ref, nm_ref, nv_ref):
        g = p_ref[0].astype(F32)
        for dev in range(1, 8):
            g = g + p_ref[dev].astype(F32)
        delta, nm, nv = _adamw_math(w_ref[...], g, m_ref[...], v_ref[...])
        g_ref[...] = g
        d_ref[...] = delta
        nm_ref[...] = nm
        nv_ref[...] = nv

    row = pl.BlockSpec((SMALL_TILE, 128), lambda i: (i, 0))
    return _pcall(body, name="adamw_small", grid=(r // SMALL_TILE,),
                  in_specs=[row, pl.BlockSpec((8, SMALL_TILE, 128), lambda i: (0, i, 0)), row, row],
                  out_specs=[row] * 4, out_shape=[_sds((r, 128))] * 4, vmem_mb=40)(w, parts, m, v)


def _place():
    x, y, c = lax.axis_index("x"), lax.axis_index("y"), lax.axis_index("c")
    return x, y, c, 2 * x + y


def _chip_peer(x, y, j):
    return (1 - x if j & 2 else x), (1 - y if j & 1 else y)


_HBM = pl.BlockSpec(memory_space=pltpu.HBM)
_SEM = pl.BlockSpec(memory_space=pltpu.SEMAPHORE)
_EFFECT = pltpu.SideEffectType.DATAFLOW_SIDE_EFFECTING


def _in_hbm(a):
    return pltpu.with_memory_space_constraint(a, pltpu.HBM)


def _copies_start(name, plan, ncopy, srcs, lands, dep=None):
    n = len(srcs) + len(lands)
    deps, dep_specs = _token_operand(dep)

    def body(*refs):
        send_sems, recv_sems = refs[n + len(deps)], refs[n + len(deps) + 1]
        token = refs[-1]
        for k, (src, dst, dev) in enumerate(plan(refs[:len(srcs)], refs[len(srcs):n])):
            if dev is None:
                pltpu.make_async_copy(src, dst, recv_sems.at[k]).start()
            else:
                pltpu.make_async_remote_copy(src_ref=src, dst_ref=dst, send_sem=send_sems.at[k],
                                             recv_sem=recv_sems.at[k], device_id=dev, device_id_type=MESH).start()
        token[...] = jnp.zeros_like(token)

    bufs = list(srcs) + list(lands)
    outs = pl.pallas_call(
        body, name=name,
        out_shape=(pltpu.SemaphoreType.DMA((ncopy,)), pltpu.SemaphoreType.DMA((ncopy,)),
                   *[pltpu.HBM(a.shape, a.dtype) for a in bufs], _sds((8, 128))),
        in_specs=[_HBM] * n + dep_specs,
        out_specs=(_SEM, _SEM, *[_HBM] * n, pl.BlockSpec(memory_space=pltpu.VMEM)),
        input_output_aliases={i: 2 + i for i in range(n)},
        compiler_params=pltpu.CompilerParams(has_side_effects=_EFFECT),
    )(*[_in_hbm(a) for a in bufs], *deps)
    return outs[0], outs[1], list(outs[2:2 + len(srcs)]), list(outs[2 + len(srcs):2 + n]), outs[-1]


def _copies_wait(name, plan, send_sems, recv_sems, srcs, lands, after):
    n = len(srcs) + len(lands)
    after = list(after)

    def body(*refs):
        s_sems, r_sems = refs[n], refs[n + 1]
        for k, (src, dst, dev) in enumerate(plan(refs[:len(srcs)], refs[len(srcs):n])):
            if dev is None:
                pltpu.make_async_copy(src, dst, r_sems.at[k]).wait()
                continue
            cp = pltpu.make_async_remote_copy(src_ref=src, dst_ref=dst, send_sem=s_sems.at[k],
                                              recv_sem=r_sems.at[k], device_id=dev, device_id_type=MESH)
            cp.wait_send()
            cp.wait_recv()

    bufs = list(srcs) + list(lands)
    outs = pl.pallas_call(
        body, name=name,
        out_shape=tuple(pltpu.HBM(a.shape, a.dtype) for a in bufs),
        in_specs=[_HBM] * n + [_SEM, _SEM] + [pl.BlockSpec(memory_space=pl.ANY)] * len(after),
        out_specs=tuple([_HBM] * n),
        input_output_aliases={i: i for i in range(n)},
        compiler_params=pltpu.CompilerParams(has_side_effects=_EFFECT),
    )(*bufs, send_sems, recv_sems, *after)
    return list(outs[:len(srcs)]), list(outs[len(srcs):])


def _with_own_slot(block, slot, nslots=NCHIP):
    land = lax.empty((nslots,) + block.shape, block.dtype)
    return lax.dynamic_update_slice(land, block[None], (slot,) + (0,) * block.ndim)


def _my_half(land, slot, c):
    half = land.shape[1] // 2
    return land.at[slot, pl.ds(pl.multiple_of(c * half, 16), half)]


def _gather_ici_plan(srcs, lands):
    x, y, c, s = _place()
    return [(_my_half(land, s, c), _my_half(land, s, c), (*_chip_peer(x, y, j), c))
            for land in lands for j in range(1, NCHIP)]


def _gather_d2d_plan(srcs, lands):
    x, y, c, s = _place()
    return [(_my_half(land, s ^ j, c), _my_half(land, s ^ j, c), (x, y, 1 - c))
            for land in lands for j in range(1, NCHIP)]


def _exchange_plan(srcs, lands):
    x, y, c, s = _place()
    return [(src.at[s ^ j], land.at[s], (*_chip_peer(x, y, j), c) if j else None)
            for src, land in zip(srcs, lands) for j in range(NCHIP)]


def _sibling_plan(srcs, lands):
    x, y, c, _ = _place()
    return [(src, land, (x, y, 1 - c)) for src, land in zip(srcs, lands)]


def _everyone_plan(srcs, lands):
    x, y, c, _ = _place()
    me = 4 * x + 2 * y + c
    return [(srcs[0], lands[0].at[me], (*_chip_peer(x, y, j >> 1), 1 - c if j & 1 else c)) for j in range(1, 8)]


def _flatten_small(parts):
    flat = jnp.concatenate([p.reshape(-1) for p in parts])
    n = flat.shape[0]
    rows = -(-n // (128 * SMALL_TILE)) * SMALL_TILE
    return jnp.pad(flat, (0, rows * 128 - n)).reshape(rows, 128)


def _unflatten_small(buf, like):
    flat = buf.reshape(-1)
    out, at = [], 0
    for p in like:
        out.append(flat[at:at + p.size].reshape(p.shape))
        at += p.size
    return out


def kernel(x, pre_norm_g, w_in, lambda_re, lambda_im, log_dt, b_re, b_im, c_re, c_im, d_skip, w_glu, b_glu, w_branch_s, w_branch_a, w_out, post_norm_g, loss_target, m_pre_norm_g, m_w_in, m_lambda_re, m_lambda_im, m_log_dt, m_b_re, m_b_im, m_c_re, m_c_im, m_d_skip, m_w_glu, m_b_glu, m_w_branch_s, m_w_branch_a, m_w_out, m_post_norm_g, v_pre_norm_g, v_w_in, v_lambda_re, v_lambda_im, v_log_dt, v_b_re, v_b_im, v_c_re, v_c_im, v_d_skip, v_w_glu, v_b_glu, v_w_branch_s, v_w_branch_a, v_w_out, v_post_norm_g):
    nb = DEPTH * NCB
    lre_c = lambda_re.reshape(nb, 1, HW)
    lim_c = lambda_im.reshape(nb, 1, HW)
    ldt_c = jnp.broadcast_to(log_dt[:, :, None], (DEPTH, 32, 64)).reshape(nb, 1, HW)
    b_rows = lambda t: t.reshape(DEPTH, NCB, 8, 64, 16).transpose(0, 1, 4, 2, 3).reshape(nb, 16, HW)
    c_rows = lambda t: t.reshape(DEPTH, NCB, 8, 16, 64).transpose(0, 1, 3, 2, 4).reshape(nb, 16, HW)
    bre_c, bim_c = b_rows(b_re), b_rows(b_im)
    ar, ai, wb, wct = _s5_prep(lre_c, lim_c, ldt_c, bre_c, bim_c, c_rows(c_re), c_rows(c_im))
    ar4 = ar.reshape(DEPTH, NCB, 1, HW)
    ai4 = ai.reshape(DEPTH, NCB, 1, HW)
    wb = wb.reshape(DEPTH, NCB, 128, SW)
    wct = wct.reshape(DEPTH, NCB, 128, SW)

    chip = 2 * lax.axis_index("x") + lax.axis_index("y")
    device = 2 * chip + lax.axis_index("c")

    def gather_start(l, dep):
        blocks = [w_in[l], w_glu[l], w_branch_s[l], w_branch_a[l], w_out[l]]
        if l == 0:
            blocks = lax.optimization_barrier(blocks)
        lands = [_with_own_slot(a.astype(BF16), chip) for a in blocks]
        return _copies_start(f"gather_start_{l}", _gather_ici_plan, 15, [], lands, dep)

    def gather_forward(l, flight, after):
        send, recv, _, lands, _ = flight
        _, lands = _copies_wait(f"gather_wait_{l}", _gather_ici_plan, send, recv, [], lands, after)
        return _copies_start(f"forward_start_{l}", _gather_d2d_plan, 15, [], lands)

    def gather_done(l, forward, after):
        send, recv, _, lands, _ = forward
        return _copies_wait(f"forward_wait_{l}", _gather_d2d_plan, send, recv, [], lands, after)[1]

    small_w = [pre_norm_g, lambda_re, lambda_im, log_dt, b_re, b_im, c_re, c_im, d_skip, b_glu, post_norm_g]
    small_m = [m_pre_norm_g, m_lambda_re, m_lambda_im, m_log_dt, m_b_re, m_b_im, m_c_re, m_c_im, m_d_skip,
               m_b_glu, m_post_norm_g]
    small_v = [v_pre_norm_g, v_lambda_re, v_lambda_im, v_log_dt, v_b_re, v_b_im, v_c_re, v_c_im, v_d_skip,
               v_b_glu, v_post_norm_g]
    flat_w, flat_m, flat_v = _flatten_small(small_w), _flatten_small(small_m), _flatten_small(small_v)

    xs = [x[0]]
    saved = []
    forward = gather_forward(0, gather_start(0, None), [xs[0], flat_w, flat_m, ar, ai, wb, wct])
    for l in range(DEPTH):
        w4, wglu4, wbs4, wba4, wout4 = gather_done(l, forward, [xs[l]] + ([flat_v] if l == 0 else []))
        started = None
        if l + 1 < DEPTH:
            flight = gather_start(l + 1, wglu4)
            started = flight[4]
        wglu = wglu4.reshape(SSM, SSM)
        wout = wout4.reshape(D, D)
        wb4, wc4 = wb[l], wct[l]
        g1 = pre_norm_g[l].reshape(1, D)
        g2 = post_norm_g[l].reshape(1, D)
        dsk = d_skip[l].reshape(1, SSM)
        bgl = b_glu[l].reshape(1, SSM)
        proj, ht = _rms_proj_fwd(xs[l], g1, w4, started)
        states, yraw = _s5_scan_fwd(proj, wb4, wc4, ar4[l], ai4[l])
        o, lse = (_in_hbm(a) for a in _attn_core_fwd(proj))
        forwarded = None
        if l + 1 < DEPTH:
            forward = gather_forward(l + 1, flight, [yraw, o])
            forwarded = forward[4]
        ys = _s5_tail_fwd(yraw, proj, dsk, wglu, bgl)
        ya = _attn_mix_fwd(o, lse, proj)
        xs.append(_merge_out_fwd(ys, ya, proj, wbs4, wba4, wout, g2, xs[l], forwarded))
        saved.append((w4, wglu, wbs4, wba4, wout, wb4, wc4, g1, g2, dsk, bgl, proj, ht, states, yraw, ys, ya, o, lse))

    loss_part, dx = _loss_head(xs[DEPTH], loss_target[0])
    loss = lax.psum(loss_part[0, 0], ("x", "y", "c"))

    big_w = (w_in, w_glu.reshape(DEPTH, 128, SSM), w_branch_s, w_branch_a, w_out)
    big_m = (m_w_in, m_w_glu, m_w_branch_s, m_w_branch_a, m_w_out)
    big_v = (v_w_in, v_w_glu, v_w_branch_s, v_w_branch_a, v_w_out)
    big_out = [None] * 5
    small = {k: [None] * DEPTH for k in ("g1", "da", "dbb", "dcc", "dsk", "bgl", "g2")}
    ici = [None] * DEPTH
    d2d = [None] * DEPTH

    def exchange_start(l, parts):
        lands = [lax.empty(p.shape, p.dtype) for p in parts]
        ici[l] = _copies_start(f"exchange_start_{l}", _exchange_plan, 20, parts, lands)

    def handoff_start(l, after):
        send, recv, srcs, lands, _ = ici[l]
        _, own = _copies_wait(f"exchange_wait_{l}", _exchange_plan, send, recv, srcs, lands, after)
        d2d[l] = _copies_start(f"handoff_start_{l}", _sibling_plan, 5, own,
                               [lax.empty(a.shape, a.dtype) for a in own])

    def update(l, after):
        send, recv, own, lands, _ = d2d[l]
        own, sib = _copies_wait(f"handoff_wait_{l}", _sibling_plan, send, recv, own, lands, after)
        for k in range(5):
            big_out[k] = _adamw_big(l, big_w[k], big_m[k], big_v[k], own[k], sib[k], big_out[k])

    for l in reversed(range(DEPTH)):
        w4, wglu, wbs4, wba4, wout, wb4, wc4, g1, g2, dsk, bgl, proj, ht, states, yraw, ys, ya, o, lse = saved[l]
        started = None
        if l + 1 < DEPTH:
            started = ici[l + 1][4] + d2d[l + 2][4] if l + 2 < DEPTH else ici[l + 1][4]
        dys, dya, dproj, dwbs, dwba, dwout, dg2 = _merge_out_bwd(dx, ys, ya, proj, wbs4, wba4, wout, g2, started)
        dyraw, du_skip, dproj, dwglu, dbgl, ddsk = _s5_tail_bwd(dys, yraw, proj, dsk, wglu, bgl, dproj)
        dproj, dbb, dcc, da = _s5_scan_bwd(dyraw, states, proj, du_skip, wb4, wc4, ar4[l], ai4[l], dproj)
        do, cterm, dproj = _attn_mix_bwd(dya, o, lse, proj, dproj)
        dproj = _attn_core_bwd(proj, do, lse, cterm, dproj)
        dh, dwin = _proj_bwd(dproj, ht, w4)
        parts = [dwin, dwglu.reshape(NCHIP, 128, SSM), dwbs, dwba, dwout.reshape(NCHIP, D // NCHIP, D)]
        exchange_start(l, parts)
        dx, dg1 = _rms_bwd(dh, xs[l], g1, dx, ici[l][4])
        if l + 1 < DEPTH:
            handoff_start(l + 1, [ici[l][4]])
        small["g1"][l], small["g2"][l], small["dsk"][l], small["bgl"][l] = dg1, dg2, ddsk, dbgl
        small["da"][l], small["dbb"][l], small["dcc"][l] = da, dbb, dcc

    dlre, dlim, dldt, dbre_c, dbim_c = _s5_prep_bwd(
        lre_c, lim_c, ldt_c, bre_c, bim_c, jnp.stack(small["da"]).reshape(nb, 1, SW),
        jnp.stack(small["dbb"]).reshape(nb, 16, SW))
    dcc = jnp.stack(small["dcc"]).reshape(nb, 16, SW)
    b_back = lambda t: t.reshape(DEPTH, NCB, 16, 8, 64).transpose(0, 1, 3, 4, 2).reshape(DEPTH, 32, 64, 16)
    c_back = lambda t: t.reshape(DEPTH, NCB, 16, 8, 64).transpose(0, 1, 3, 2, 4).reshape(DEPTH, 32, 16, 64)
    small_g = [jnp.stack(small["g1"]).reshape(DEPTH, D), dlre.reshape(DEPTH, 32, 64), dlim.reshape(DEPTH, 32, 64),
               dldt.reshape(DEPTH, 32, 64).sum(-1), b_back(dbre_c), b_back(dbim_c), c_back(dcc[:, :, :HW]),
               c_back(dcc[:, :, HW:]),
               jnp.stack(small["dsk"]).reshape(DEPTH, SSM), jnp.stack(small["bgl"]).reshape(DEPTH, SSM),
               jnp.stack(small["g2"]).reshape(DEPTH, D)]
    part = _flatten_small(small_g).astype(BF16)
    send, recv, srcs, lands, token = _copies_start("small_start", _everyone_plan, 7, [part],
                                                   [_with_own_slot(part, device, 8)])
    update(DEPTH - 1, [token, ici[0][4]])
    for l in range(DEPTH - 2, 0, -1):
        update(l, [out[0] for out in big_out])
    handoff_start(0, [out[0] for out in big_out])
    update(0, [d2d[0][4]])
    _, (parts8,) = _copies_wait("small_wait", _everyone_plan, send, recv, srcs, lands, [out[0] for out in big_out])
    g_flat, d_flat, nm_flat, nv_flat = _adamw_small(flat_w, parts8, flat_m, flat_v)
    sg = _unflatten_small(g_flat, small_w)
    sd = _unflatten_small(d_flat, small_w)
    snm = _unflatten_small(nm_flat, small_w)
    snv = _unflatten_small(nv_flat, small_w)

    def ordered(sm, which):
        bg = [big_out[k][which] for k in range(5)]
        bg[1] = bg[1].reshape(DEPTH, 128, SSM)
        return [sm[0], bg[0], sm[1], sm[2], sm[3], sm[4], sm[5], sm[6], sm[7], sm[8], bg[1], sm[9], bg[2], bg[3],
                bg[4], sm[10]]

    return (loss, dx[None], *ordered(sg, 0), *ordered(sd, 1), *ordered(snm, 2), *ordered(snv, 3))
```

```python
import math

import jax
import jax.numpy as jnp
from jax import lax
from jax.experimental import pallas as pl
from jax.experimental.pallas import tpu as pltpu

F32 = jnp.float32
BF16 = jnp.bfloat16
MESH = pl.DeviceIdType.MESH

DEPTH = 4
L = 2048
D = 1024
NCOL = 8192
SSM = 512
AW = 512
QKV = 1536
RMS_EPS = 1e-6
NCHIP = 4

COL_U, COL_ZS, COL_Q, COL_K, COL_V, COL_ZA, COL_GS, COL_GA = 0, 4, 8, 20, 32, 44, 48, 56

NSEG = 8
SEG = L // NSEG
NCB = 4
HW = 512
SW = 2 * HW

HEAD = 128
DILATIONS = (1, 4, 16)
SCALE = HEAD ** -0.5
NEG = -1e30

ADAM_LR, ADAM_B1, ADAM_B2, ADAM_EPS, ADAM_WD, ADAM_STEP = 0.001, 0.9, 0.999, 1e-08, 0.01, 10


def _sds(shape, dtype=F32):
    return jax.ShapeDtypeStruct(shape, dtype)


def _pcall(body, *, name, out_shape, grid=None, in_specs=None, out_specs=None, scratch=(), vmem_mb=None,
           aliases=None):
    params = {}
    if vmem_mb is not None:
        params["vmem_limit_bytes"] = vmem_mb << 20
    kw = {}
    if grid is not None:
        kw["grid"] = grid
    if in_specs is not None:
        kw["in_specs"] = in_specs
    if out_specs is not None:
        kw["out_specs"] = out_specs
    return pl.pallas_call(body, name=name, out_shape=out_shape, scratch_shapes=list(scratch),
                          compiler_params=pltpu.CompilerParams(**params),
                          input_output_aliases=aliases or {}, **kw)


def _dot(a, b):
    return jnp.dot(a, b, preferred_element_type=F32)


def _dot_nt(a, b):
    return lax.dot_general(a, b, (((1,), (1,)), ((), ())), preferred_element_type=F32)


def _dot_tn(a, b):
    return lax.dot_general(a, b, (((0,), (0,)), ((), ())), preferred_element_type=F32)


def _sigmoid(x):
    return jax.nn.sigmoid(x)


_GELU_K = math.sqrt(2.0 / math.pi)


def _gelu(x):
    return 0.5 * x * (1.0 + jnp.tanh(_GELU_K * (x + 0.044715 * (x * x * x))))


def _gelu_grad(x):
    t = jnp.tanh(_GELU_K * (x + 0.044715 * (x * x * x)))
    return 0.5 * (1.0 + t) + 0.5 * x * (1.0 - t * t) * (_GELU_K * (1.0 + 3.0 * 0.044715 * (x * x)))


def _token_operand(dep):
    if dep is None:
        return [], []
    return [dep], [pl.BlockSpec(memory_space=pl.ANY)]


def _rms_proj_fwd(x, g, w4, dep=None):
    tl, tn = 512, 1024
    ni = L // tl
    deps, dep_specs = _token_operand(dep)

    def body(x_ref, g_ref, w_ref, *rest):
        o_ref, h_ref = rest[len(deps):]
        rows = pl.ds(pl.multiple_of(pl.program_id(1) * tl, tl), tl)

        @pl.when(pl.program_id(0) == 0)
        def _():
            xx = x_ref[...]
            inv = lax.rsqrt(jnp.mean(xx * xx, axis=-1, keepdims=True) + RMS_EPS)
            h_ref[rows, :] = (xx * inv * g_ref[...]).astype(BF16)

        o_ref[...] = _dot(h_ref[rows, :], w_ref[...])

    proj, h = _pcall(
        body, name="rms_proj_fwd", grid=(NCOL // tn, ni),
        in_specs=[pl.BlockSpec((tl, D), lambda j, i: (jnp.where(j == 0, i, ni - 1), 0)),
                  pl.BlockSpec((1, D), lambda j, i: (0, 0)),
                  pl.BlockSpec((None, D, tn), lambda j, i: (lax.div(j, 2), 0, lax.rem(j, 2)))] + dep_specs,
        out_specs=[pl.BlockSpec((tl, tn), lambda j, i: (i, j)), pl.BlockSpec((L, D), lambda j, i: (0, 0))],
        out_shape=[_sds((L, NCOL)), _sds((L, D), BF16)], vmem_mb=48)(x, g, w4, *deps)

    return proj, h


def _proj_bwd(dproj, ht, w4):
    tn = 1024

    def body(dp_ref, ht_ref, w_ref, dh_ref, dw_ref):
        @pl.when(pl.program_id(0) == 0)
        def _():
            dh_ref[...] = jnp.zeros((L, D), F32)

        dw_ref[...] = _dot_tn(ht_ref[...], dp_ref[...]).astype(BF16)
        dh_ref[...] += _dot_nt(dp_ref[...], w_ref[...])

    return _pcall(
        body, name="proj_bwd", grid=(NCOL // tn,),
        in_specs=[pl.BlockSpec((L, tn), lambda j: (0, j)),
                  pl.BlockSpec((L, D), lambda j: (0, 0)),
                  pl.BlockSpec((None, D, tn), lambda j: (lax.div(j, 2), 0, lax.rem(j, 2)))],
        out_specs=[pl.BlockSpec((L, D), lambda j: (0, 0)),
                   pl.BlockSpec((None, D, tn), lambda j: (lax.div(j, 2), 0, lax.rem(j, 2)))],
        out_shape=[_sds((L, D)), _sds((NCHIP, D, NCOL // NCHIP), BF16)], vmem_mb=56)(dproj, ht, w4)


def _rms_bwd(dh, x, g, dxn, dep=None):
    tl = 512
    deps, dep_specs = _token_operand(dep)

    def body(dh_ref, x_ref, g_ref, dxn_ref, *rest):
        dx_ref, dg_ref = rest[len(deps):]
        i = pl.program_id(0)
        xx = x_ref[...]
        inv = lax.rsqrt(jnp.mean(xx * xx, axis=-1, keepdims=True) + RMS_EPS)
        nrm = xx * inv
        dh_v = dh_ref[...]
        dn = dh_v * g_ref[...]
        dx_ref[...] = dxn_ref[...] + inv * (dn - nrm * jnp.mean(dn * nrm, axis=-1, keepdims=True))
        part = jnp.sum(dh_v * nrm, axis=0, keepdims=True)

        @pl.when(i == 0)
        def _():
            dg_ref[...] = part

        @pl.when(i > 0)
        def _():
            dg_ref[...] += part

    row = pl.BlockSpec((tl, D), lambda i: (i, 0))
    vec = pl.BlockSpec((1, D), lambda i: (0, 0))
    return _pcall(body, name="rms_bwd", grid=(L // tl,), in_specs=[row, row, vec, row] + dep_specs,
                  out_specs=[row, vec], out_shape=[_sds((L, D)), _sds((1, D))], vmem_mb=40)(dh, x, g, dxn, *deps)


def _s5_param_math(lre, lim, ldt, bre, bim):
    lr = jnp.minimum(lre, -1e-4)
    dt = jnp.exp(ldt)
    mag = jnp.exp(lr * dt)
    ar = mag * jnp.cos(lim * dt)
    ai = mag * jnp.sin(lim * dt)
    den = lr * lr + lim * lim
    nr = ar - 1.0
    qr = (nr * lr + ai * lim) / den
    qi = (ai * lr - nr * lim) / den
    return ar, ai, qr * bre - qi * bim, qr * bim + qi * bre


def _group_of_lane(width):
    lane = lax.broadcasted_iota(jnp.int32, (16, width), 1)
    return lax.shift_right_logical(jnp.bitwise_and(lane, HW - 1), 6)


def _vec_spec():
    return pl.BlockSpec((None, 1, HW), lambda n: (n, 0, 0))


def _mat_spec():
    return pl.BlockSpec((None, 16, HW), lambda n: (n, 0, 0))


def _s5_prep(lre, lim, ldt, bre, bim, cre, cim):
    n = lre.shape[0]

    def body(lre_ref, lim_ref, ldt_ref, bre_ref, bim_ref, cre_ref, cim_ref, ar_ref, ai_ref, wb_ref, wct_ref):
        ar, ai, bbr, bbi = _s5_param_math(lre_ref[...], lim_ref[...], ldt_ref[...], bre_ref[...], bim_ref[...])
        ar_ref[...] = ar
        ai_ref[...] = ai
        group = _group_of_lane(SW)
        bb = jnp.concatenate([bbr, bbi], axis=1)
        cc = jnp.concatenate([cre_ref[...], -cim_ref[...]], axis=1)
        for g in range(8):
            rows = pl.ds(16 * g, 16)
            wb_ref[rows, :] = jnp.where(group == g, bb, 0.0).astype(BF16)
            wct_ref[rows, :] = jnp.where(group == g, cc, 0.0).astype(BF16)

    wide = pl.BlockSpec((None, 128, SW), lambda n: (n, 0, 0))
    return _pcall(body, name="s5_prep", grid=(n,),
                  in_specs=[_vec_spec()] * 3 + [_mat_spec()] * 4,
                  out_specs=[_vec_spec(), _vec_spec(), wide, wide],
                  out_shape=[_sds((n, 1, HW)), _sds((n, 1, HW)), _sds((n, 128, SW), BF16), _sds((n, 128, SW), BF16)])(
                      lre, lim, ldt, bre, bim, cre, cim)


def _s5_prep_bwd(lre, lim, ldt, bre, bim, ga, gbb):
    n = lre.shape[0]

    def body(lre_ref, lim_ref, ldt_ref, bre_ref, bim_ref, ga_ref, gbb_ref,
             dlre_ref, dlim_ref, dldt_ref, dbre_ref, dbim_ref):
        _, vjp = jax.vjp(_s5_param_math, lre_ref[...], lim_ref[...], ldt_ref[...], bre_ref[...], bim_ref[...])
        ga, gbb = ga_ref[...], gbb_ref[...]
        dlre, dlim, dldt, dbre, dbim = vjp((ga[:, :HW], ga[:, HW:], gbb[:, :HW], gbb[:, HW:]))
        dlre_ref[...] = dlre
        dlim_ref[...] = dlim
        dldt_ref[...] = dldt
        dbre_ref[...] = dbre
        dbim_ref[...] = dbim

    return _pcall(body, name="s5_prep_bwd", grid=(n,),
                  in_specs=[_vec_spec()] * 3 + [_mat_spec()] * 2 +
                  [pl.BlockSpec((None, 1, SW), lambda n: (n, 0, 0)), pl.BlockSpec((None, 16, SW), lambda n: (n, 0, 0))],
                  out_specs=[_vec_spec()] * 3 + [_mat_spec()] * 2,
                  out_shape=[_sds((n, 1, HW))] * 3 + [_sds((n, 16, HW))] * 2)(lre, lim, ldt, bre, bim, ga, gbb)


def _diag_blocks(full):
    group = _group_of_lane(SW)
    out = jnp.where(group == 0, full[0:16, :], 0.0)
    for g in range(1, 8):
        out = out + jnp.where(group == g, full[16 * g:16 * g + 16, :], 0.0)
    return out


def _permute_rows(dst_ref, src_ref):
    for s in range(NSEG):
        dst_ref[pl.ds(s, SEG, stride=NSEG), :] = src_ref[pl.ds(s * SEG, SEG), :]


def _unpermute_rows(src_ref, s):
    return src_ref[pl.ds(s, SEG, stride=NSEG), :]


def _row_of(slab, rows, j):
    return jnp.sum(jnp.where(rows == j, slab, 0.0), axis=0, keepdims=True)


def _s5_scan_fwd(proj, wb4, wc4, ar4, ai4):
    def body(u_ref, wb_ref, wc_ref, ar_ref, ai_ref, x_ref, y_ref, up_ref, d_ref, yp_ref):
        _permute_rows(up_ref, u_ref)
        d_ref[...] = _dot(up_ref[...].astype(BF16), wb_ref[...])
        ar = jnp.broadcast_to(ar_ref[...], (NSEG, HW))
        ai = jnp.broadcast_to(ai_ref[...], (NSEG, HW))
        zero = jnp.zeros((NSEG, HW), F32)

        def drive(t):
            row = pl.multiple_of(t * NSEG, NSEG)
            dd = d_ref[pl.ds(row, NSEG), :]
            return row, dd[:, :HW], dd[:, HW:]

        def local_step(t, c):
            xr, xi = c
            _, dr, di = drive(t)
            return ar * xr - ai * xi + dr, ar * xi + ai * xr + di

        fr, fi = lax.fori_loop(0, SEG, local_step, (zero, zero), unroll=8)
        pr, pi_ = ar_ref[...], ai_ref[...]
        for _ in range(int(math.log2(SEG))):
            pr, pi_ = pr * pr - pi_ * pi_, 2.0 * pr * pi_
        rows = lax.broadcasted_iota(jnp.int32, (NSEG, HW), 0)
        cr, ci = zero, zero
        for j in range(NSEG - 1):
            sr, si = _row_of(cr, rows, j), _row_of(ci, rows, j)
            gr, gi = _row_of(fr, rows, j), _row_of(fi, rows, j)
            nr = pr * sr - pi_ * si + gr
            ni = pr * si + pi_ * sr + gi
            cr = jnp.where(rows == j + 1, nr, cr)
            ci = jnp.where(rows == j + 1, ni, ci)

        def true_step(t, c):
            xr, xi = c
            row, dr, di = drive(t)
            nr = ar * xr - ai * xi + dr
            ni = ar * xi + ai * xr + di
            x_ref[pl.ds(row, NSEG), :] = jnp.concatenate([nr, ni], axis=1)
            return nr, ni

        lax.fori_loop(0, SEG, true_step, (cr, ci), unroll=8)
        yp_ref[...] = _dot_nt(x_ref[...].astype(BF16), wc_ref[...])
        for s in range(NSEG):
            y_ref[pl.ds(s * SEG, SEG), :] = _unpermute_rows(yp_ref, s)

    return _pcall(
        body, name="s5_scan_fwd", grid=(NCB,),
        in_specs=[pl.BlockSpec((L, 128), lambda b: (0, COL_U + b)),
                  pl.BlockSpec((None, 128, SW), lambda b: (b, 0, 0)),
                  pl.BlockSpec((None, 128, SW), lambda b: (b, 0, 0)),
                  pl.BlockSpec((None, 1, HW), lambda b: (b, 0, 0)),
                  pl.BlockSpec((None, 1, HW), lambda b: (b, 0, 0))],
        out_specs=[pl.BlockSpec((None, L, SW), lambda b: (b, 0, 0)),
                   pl.BlockSpec((L, 128), lambda b: (0, b))],
        out_shape=[_sds((NCB, L, SW)), _sds((L, SSM))],
        scratch=[pltpu.VMEM((L, 128), F32), pltpu.VMEM((L, SW), F32), pltpu.VMEM((L, 128), F32)],
        vmem_mb=56)(proj, wb4, wc4, ar4, ai4)


def _s5_scan_bwd(dy, xs, proj, du_skip, wb4, wc4, ar4, ai4, dproj):
    def body(dy_ref, x_ref, u_ref, dus_ref, wb_ref, wc_ref, ar_ref, ai_ref, _,
             du_ref, dbb_ref, dcc_ref, da_ref, dyp_ref, up_ref, g_ref, dup_ref):
        _permute_rows(dyp_ref, dy_ref)
        _permute_rows(up_ref, u_ref)
        dyp = dyp_ref[...].astype(BF16)
        g_ref[...] = _dot(dyp, wc_ref[...])
        dcc = _diag_blocks(_dot_tn(dyp, x_ref[...].astype(BF16)))
        dcc_ref[...] = jnp.concatenate([dcc[:, :HW], -dcc[:, HW:]], axis=1)
        ar = jnp.broadcast_to(ar_ref[...], (NSEG, HW))
        ai = jnp.broadcast_to(ai_ref[...], (NSEG, HW))
        zero = jnp.zeros((NSEG, HW), F32)

        def load(t):
            row = pl.multiple_of(t * NSEG, NSEG)
            gg = g_ref[pl.ds(row, NSEG), :]
            return row, gg[:, :HW], gg[:, HW:]

        def local_step(k, c):
            lr, li = c
            _, gr, gi = load(SEG - 1 - k)
            return ar * lr + ai * li + gr, ar * li - ai * lr + gi

        fr, fi = lax.fori_loop(0, SEG, local_step, (zero, zero), unroll=8)
        pr, pi_ = ar_ref[...], -ai_ref[...]
        for _ in range(int(math.log2(SEG))):
            pr, pi_ = pr * pr - pi_ * pi_, 2.0 * pr * pi_
        rows = lax.broadcasted_iota(jnp.int32, (NSEG, HW), 0)
        cr, ci = zero, zero
        for j in range(NSEG - 1, 0, -1):
            sr, si = _row_of(cr, rows, j), _row_of(ci, rows, j)
            gr, gi = _row_of(fr, rows, j), _row_of(fi, rows, j)
            nr = pr * sr - pi_ * si + gr
            ni = pr * si + pi_ * sr + gi
            cr = jnp.where(rows == j - 1, nr, cr)
            ci = jnp.where(rows == j - 1, ni, ci)

        def true_step(k, c):
            lr, li, gar, gai = c
            t = SEG - 1 - k
            row, gr, gi = load(t)
            nr = ar * lr + ai * li + gr
            ni = ar * li - ai * lr + gi
            g_ref[pl.ds(row, NSEG), :] = jnp.concatenate([nr, ni], axis=1)
            prow = pl.multiple_of(jnp.maximum(t - 1, 0) * NSEG, NSEG)
            xp = x_ref[pl.ds(prow, NSEG), :]
            live = (t > 0).astype(F32)
            xr, xi = xp[:, :HW] * live, xp[:, HW:] * live
            return nr, ni, gar + (nr * xr + ni * xi), gai + (ni * xr - nr * xi)

        _, _, gar, gai = lax.fori_loop(0, SEG, true_step, (cr, ci, zero, zero), unroll=4)
        l0 = g_ref[1:NSEG, :]
        xl = x_ref[(SEG - 1) * NSEG:(SEG - 1) * NSEG + NSEG - 1, :]
        l0r, l0i, xlr, xli = l0[:, :HW], l0[:, HW:], xl[:, :HW], xl[:, HW:]
        gar_t = jnp.sum(gar, axis=0, keepdims=True) + jnp.sum(l0r * xlr + l0i * xli, axis=0, keepdims=True)
        gai_t = jnp.sum(gai, axis=0, keepdims=True) + jnp.sum(l0i * xlr - l0r * xli, axis=0, keepdims=True)
        da_ref[...] = jnp.concatenate([gar_t, gai_t], axis=1)
        lam = g_ref[...].astype(BF16)
        dbb_ref[...] = _diag_blocks(_dot_tn(up_ref[...].astype(BF16), lam))
        dup_ref[...] = _dot_nt(lam, wb_ref[...])
        for s in range(NSEG):
            sl = pl.ds(s * SEG, SEG)
            du_ref[sl, :] = (_unpermute_rows(dup_ref, s) + dus_ref[sl, :]).astype(BF16)

    col = lambda off: pl.BlockSpec((L, 128), lambda b: (0, off + b))
    return _pcall(
        body, name="s5_scan_bwd", grid=(NCB,),
        in_specs=[col(0), pl.BlockSpec((None, L, SW), lambda b: (b, 0, 0)), col(COL_U), col(0),
                  pl.BlockSpec((None, 128, SW), lambda b: (b, 0, 0)),
                  pl.BlockSpec((None, 128, SW), lambda b: (b, 0, 0)),
                  pl.BlockSpec((None, 1, HW), lambda b: (b, 0, 0)),
                  pl.BlockSpec((None, 1, HW), lambda b: (b, 0, 0)),
                  pl.BlockSpec(memory_space=pl.ANY)],
        out_specs=[col(COL_U), pl.BlockSpec((None, 16, SW), lambda b: (b, 0, 0)),
                   pl.BlockSpec((None, 16, SW), lambda b: (b, 0, 0)),
                   pl.BlockSpec((None, 1, SW), lambda b: (b, 0, 0))],
        out_shape=[_sds((L, NCOL), BF16), _sds((NCB, 16, SW)), _sds((NCB, 16, SW)), _sds((NCB, 1, SW))],
        scratch=[pltpu.VMEM((L, 128), F32), pltpu.VMEM((L, 128), F32), pltpu.VMEM((L, SW), F32),
                 pltpu.VMEM((L, 128), F32)],
        aliases={8: 0}, vmem_mb=56)(dy, xs, proj, du_skip, wb4, wc4, ar4, ai4, dproj)


def _s5_tail_fwd(yraw, proj, dsk, wglu, bglu):
    tl = 512

    def body(y_ref, u_ref, z_ref, dsk_ref, w_ref, b_ref, o_ref):
        y1 = y_ref[...] + dsk_ref[...] * u_ref[...]
        y2 = _gelu(y1)
        gl = _dot(y2.astype(BF16), w_ref[...]) + b_ref[...]
        z = z_ref[...]
        o_ref[...] = (y2 * _sigmoid(gl)) * (z * _sigmoid(z))

    blk = lambda c: pl.BlockSpec((tl, SSM), lambda i: (i, c))
    vec = pl.BlockSpec((1, SSM), lambda i: (0, 0))
    return _pcall(body, name="s5_tail_fwd", grid=(L // tl,),
                  in_specs=[blk(0), blk(0), blk(1), vec, pl.BlockSpec((SSM, SSM), lambda i: (0, 0)), vec],
                  out_specs=blk(0), out_shape=_sds((L, SSM)), vmem_mb=40)(yraw, proj, proj, dsk, wglu, bglu)


def _s5_tail_bwd(dys, yraw, proj, dsk, wglu, bglu, dproj):
    tl = 512
    ni = L // tl

    def body(dys_ref, y_ref, u_ref, z_ref, dsk_ref, w_ref, b_ref, _,
             dy_ref, dus_ref, dz_ref, dw_ref, db_ref, dd_ref, acc_ref):
        i = pl.program_id(0)
        u = u_ref[...]
        y1 = y_ref[...] + dsk_ref[...] * u
        y2 = _gelu(y1)
        y2b = y2.astype(BF16)
        sg = _sigmoid(_dot(y2b, w_ref[...]) + b_ref[...])
        y3 = y2 * sg
        z = z_ref[...]
        sz = _sigmoid(z)
        dys = dys_ref[...]
        dy3 = dys * (z * sz)
        dz_ref[...] = (dys * y3 * (sz * (1.0 + z * (1.0 - sz)))).astype(BF16)
        dgl = (dy3 * y2) * (sg * (1.0 - sg))
        dglb = dgl.astype(BF16)
        dy2 = dy3 * sg + _dot_nt(dglb, w_ref[...])
        dy1 = dy2 * _gelu_grad(y1)
        dy_ref[...] = dy1
        dus_ref[...] = dsk_ref[...] * dy1
        dw = _dot_tn(y2b, dglb)
        db = jnp.sum(dgl, axis=0, keepdims=True)
        dd = jnp.sum(dy1 * u, axis=0, keepdims=True)

        @pl.when(i == 0)
        def _():
            acc_ref[...] = dw
            db_ref[...] = db
            dd_ref[...] = dd

        @pl.when(i > 0)
        def _():
            acc_ref[...] += dw
            db_ref[...] += db
            dd_ref[...] += dd

        @pl.when(i == ni - 1)
        def _():
            dw_ref[...] = acc_ref[...].astype(BF16)

    blk = lambda c: pl.BlockSpec((tl, SSM), lambda i: (i, c))
    vec = pl.BlockSpec((1, SSM), lambda i: (0, 0))
    mat = pl.BlockSpec((SSM, SSM), lambda i: (0, 0))
    return _pcall(body, name="s5_tail_bwd", grid=(ni,),
                  in_specs=[blk(0), blk(0), blk(0), blk(1), vec, mat, vec, pl.BlockSpec(memory_space=pl.ANY)],
                  out_specs=[blk(0), blk(0), blk(COL_ZS // 4), mat, vec, vec],
                  out_shape=[_sds((L, SSM)), _sds((L, SSM)), _sds((L, NCOL), BF16), _sds((SSM, SSM), BF16),
                             _sds((1, SSM)), _sds((1, SSM))],
                  scratch=[pltpu.VMEM((SSM, SSM), F32)], aliases={7: 2},
                  vmem_mb=40)(dys, yraw, proj, proj, dsk, wglu, bglu, dproj)


def _attn_blocks(dil):
    n = L // dil
    return [(i * HEAD * dil + r, r * n + i * HEAD, i == 0) for r in range(dil) for i in range(n // HEAD)]


def _rows(start, dil):
    return pl.ds(start, HEAD) if dil == 1 else pl.ds(start, HEAD, stride=dil)


def _residue_rows(r, dil):
    n = L // dil
    return pl.ds(0, L) if dil == 1 else pl.ds(r, n, stride=dil)


def _gather_residues(pairs, dil, dtype=BF16):
    n = L // dil
    for src, dst in pairs:
        for r in range(dil):
            dst[pl.ds(r * n, n), :] = src[_residue_rows(r, dil), :].astype(dtype)


def _band_masks():
    ri = lax.broadcasted_iota(jnp.int32, (HEAD, HEAD), 0)
    ci = lax.broadcasted_iota(jnp.int32, (HEAD, HEAD), 1)
    ri2 = lax.broadcasted_iota(jnp.int32, (HEAD, 2 * HEAD), 0)
    ci2 = lax.broadcasted_iota(jnp.int32, (HEAD, 2 * HEAD), 1)
    return ci <= ri, jnp.logical_and(ci2 >= ri2, ci2 - HEAD <= ri2)


def _qkv_specs(index):
    def spec(off):
        return pl.BlockSpec((L, HEAD), lambda gi, h: (0, off + index(gi, h)))
    return [spec(off) for off in (COL_Q, COL_K, COL_V)]


def _attn_core_fwd(proj):
    def body(q_ref, k_ref, v_ref, o_ref, lse_ref, qp_ref, kp_ref, vp_ref):
        gi = pl.program_id(0)
        m_cur, m_both = _band_masks()
        for g, dil in enumerate(DILATIONS):
            @pl.when(gi == g)
            def _(dil=dil):
                _gather_residues(((q_ref, qp_ref), (k_ref, kp_ref), (v_ref, vp_ref)), dil)
                for st, base, first in _attn_blocks(dil):
                    q = qp_ref[pl.ds(base, HEAD), :]
                    keys = pl.ds(base, HEAD) if first else pl.ds(base - HEAD, 2 * HEAD)
                    s = jnp.where(m_cur if first else m_both, _dot_nt(q, kp_ref[keys, :]) * SCALE, NEG)
                    mx = jnp.max(s, axis=-1, keepdims=True)
                    p = jnp.exp(s - mx)
                    den = jnp.sum(p, axis=-1, keepdims=True)
                    r = _rows(st, dil)
                    o_ref[r, :] = _dot(p.astype(BF16), vp_ref[keys, :]) / den
                    lse_ref[r, :] = jnp.broadcast_to(mx + jnp.log(den), (HEAD, HEAD))

    idx = lambda gi, h: gi * 4 + h
    out = pl.BlockSpec((L, HEAD), lambda gi, h: (0, gi * 4 + h))
    return _pcall(body, name="attn_core_fwd", grid=(3, 4), in_specs=_qkv_specs(idx), out_specs=[out, out],
                  out_shape=[_sds((L, QKV)), _sds((L, QKV))], scratch=[pltpu.VMEM((L, HEAD), BF16)] * 3,
                  vmem_mb=40)(proj, proj, proj)


def _attn_mix_fwd(o, lse, proj):
    tl = 512

    def body(o_ref, l_ref, z_ref, y_ref):
        for h in range(4):
            c = [pl.ds((g * 4 + h) * HEAD, HEAD) for g in range(3)]
            ls = [l_ref[:, c[g]] for g in range(3)]
            m = jnp.maximum(jnp.maximum(ls[0], ls[1]), ls[2])
            e = [jnp.exp(ls[g] - m) for g in range(3)]
            y = (e[0] * o_ref[:, c[0]] + e[1] * o_ref[:, c[1]] + e[2] * o_ref[:, c[2]]) / (e[0] + e[1] + e[2])
            z = z_ref[:, pl.ds(h * HEAD, HEAD)]
            y_ref[:, pl.ds(h * HEAD, HEAD)] = y * (z * _sigmoid(z))

    wide = pl.BlockSpec((tl, QKV), lambda i: (i, 0))
    return _pcall(body, name="attn_mix_fwd", grid=(L // tl,),
                  in_specs=[wide, wide, pl.BlockSpec((tl, AW), lambda i: (i, COL_ZA // 4))],
                  out_specs=pl.BlockSpec((tl, AW), lambda i: (i, 0)), out_shape=_sds((L, AW)),
                  vmem_mb=40)(o, lse, proj)


def _attn_mix_bwd(dya, o, lse, proj, dproj):
    tl = 512

    def body(dya_ref, o_ref, l_ref, z_ref, _, do_ref, c_ref, dz_ref):
        for h in range(4):
            c = [pl.ds((g * 4 + h) * HEAD, HEAD) for g in range(3)]
            hs = pl.ds(h * HEAD, HEAD)
            ls = [l_ref[:, c[g]] for g in range(3)]
            m = jnp.maximum(jnp.maximum(ls[0], ls[1]), ls[2])
            e = [jnp.exp(ls[g] - m) for g in range(3)]
            den = e[0] + e[1] + e[2]
            al = [e[g] / den for g in range(3)]
            y = al[0] * o_ref[:, c[0]] + al[1] * o_ref[:, c[1]] + al[2] * o_ref[:, c[2]]
            z = z_ref[:, hs]
            sz = _sigmoid(z)
            dya = dya_ref[:, hs]
            dz_ref[:, hs] = (dya * y * (sz * (1.0 + z * (1.0 - sz)))).astype(BF16)
            dy = dya * (z * sz)
            tot = jnp.sum(dy * y, axis=-1, keepdims=True)
            for g in range(3):
                do_ref[:, c[g]] = al[g] * dy
                c_ref[:, c[g]] = -(al[g] * tot)

    wide = pl.BlockSpec((tl, QKV), lambda i: (i, 0))
    nar = pl.BlockSpec((tl, AW), lambda i: (i, 0))
    za = pl.BlockSpec((tl, AW), lambda i: (i, COL_ZA // 4))
    return _pcall(body, name="attn_mix_bwd", grid=(L // tl,),
                  in_specs=[nar, wide, wide, za, pl.BlockSpec(memory_space=pl.ANY)],
                  out_specs=[wide, wide, za], out_shape=[_sds((L, QKV)), _sds((L, QKV)), _sds((L, NCOL), BF16)],
                  aliases={4: 2}, vmem_mb=48)(dya, o, lse, proj, dproj)


def _attn_core_bwd(proj, do, lse, cc, dproj):
    def body(q_ref, k_ref, v_ref, do_ref, lse_ref, c_ref, _, dp_ref,
             qp_ref, kp_ref, vp_ref, dop_ref, lsep_ref, cp_ref, dqp_ref, dkp_ref, dvp_ref, tok_ref, out_ref, sems):
        gi = pl.program_id(0)
        head = gi * 4 + pl.program_id(1)
        m_cur, m_both = _band_masks()
        dkp_ref[...] = jnp.zeros((L, HEAD), F32)
        dvp_ref[...] = jnp.zeros((L, HEAD), F32)
        copies = []
        for g, dil in enumerate(DILATIONS):
            @pl.when(gi == g)
            def _(dil=dil):
                _gather_residues(((q_ref, qp_ref), (k_ref, kp_ref), (v_ref, vp_ref), (do_ref, dop_ref)), dil)
                _gather_residues(((lse_ref, lsep_ref), (c_ref, cp_ref)), dil, F32)
                for _, base, first in _attn_blocks(dil):
                    mine = pl.ds(base, HEAD)
                    keys = mine if first else pl.ds(base - HEAD, 2 * HEAD)
                    q, do_b, kb = qp_ref[mine, :], dop_ref[mine, :], kp_ref[keys, :]
                    lse_b, c_b = lsep_ref[mine, :], cp_ref[mine, :]
                    if not first:
                        lse_b = jnp.concatenate([lse_b, lse_b], axis=1)
                        c_b = jnp.concatenate([c_b, c_b], axis=1)
                    s = _dot_nt(q, kb) * SCALE
                    p = jnp.where(m_cur if first else m_both, jnp.exp(s - lse_b), 0.0)
                    ds = (p * (_dot_nt(do_b, vp_ref[keys, :]) + c_b) * SCALE).astype(BF16)
                    dqp_ref[mine, :] = _dot(ds, kb)
                    dkp_ref[keys, :] += _dot_tn(ds, q)
                    dvp_ref[keys, :] += _dot_tn(p.astype(BF16), do_b)
                n = L // dil
                for slot, src in enumerate((dqp_ref, dkp_ref, dvp_ref)):
                    for r in range(dil):
                        tok_ref[_residue_rows(r, dil), :] = src[pl.ds(r * n, n), :]
                    out_ref[slot] = tok_ref[...].astype(BF16)

        for slot, off in enumerate((COL_Q, COL_K, COL_V)):
            cols = pl.ds(pl.multiple_of((off + head) * HEAD, HEAD), HEAD)
            copies.append(pltpu.make_async_copy(out_ref.at[slot], dp_ref.at[:, cols], sems.at[slot]))
            copies[-1].start()
        for cp in copies:
            cp.wait()

    idx = lambda gi, h: gi * 4 + h
    blk = pl.BlockSpec((L, HEAD), lambda gi, h: (0, gi * 4 + h))
    hbm = pl.BlockSpec(memory_space=pl.ANY)
    return _pcall(body, name="attn_core_bwd", grid=(3, 4), in_specs=_qkv_specs(idx) + [blk, blk, blk, hbm],
                  out_specs=hbm, out_shape=_sds((L, NCOL), BF16),
                  scratch=[pltpu.VMEM((L, HEAD), BF16)] * 4 + [pltpu.VMEM((L, HEAD), F32)] * 6 +
                  [pltpu.VMEM((3, L, HEAD), BF16), pltpu.SemaphoreType.DMA((3,))],
                  aliases={6: 0}, vmem_mb=48)(proj, proj, proj, do, lse, cc, dproj)


def _merge_fwd_math(ys_b, ya_b, gs, ga, wbs_ref, wba_ref, wout_ref):
    bs = jnp.concatenate([_dot(ys_b, wbs_ref[s]) for s in range(NCHIP)], axis=1)
    ba = jnp.concatenate([_dot(ya_b, wba_ref[s]) for s in range(NCHIP)], axis=1)
    sgs, sga = _sigmoid(gs), _sigmoid(ga)
    merged = sgs * bs + sga * ba
    out = _dot(merged.astype(BF16), wout_ref[...])
    inv = lax.rsqrt(jnp.mean(out * out, axis=-1, keepdims=True) + RMS_EPS)
    return bs, ba, sgs, sga, merged, out, inv


def _merge_specs(tl):
    row = lambda w, c: pl.BlockSpec((tl, w), lambda i: (i, c))
    return [row(SSM, 0), row(AW, 0), row(D, COL_GS // 8), row(D, COL_GA // 8),
            pl.BlockSpec((NCHIP, SSM, D // NCHIP), lambda i: (0, 0, 0)),
            pl.BlockSpec((NCHIP, AW, D // NCHIP), lambda i: (0, 0, 0)),
            pl.BlockSpec((D, D), lambda i: (0, 0)),
            pl.BlockSpec((1, D), lambda i: (0, 0))]


def _merge_out_fwd(ys, ya, proj, wbs4, wba4, wout, g2, x, dep=None):
    tl = 512
    deps, dep_specs = _token_operand(dep)

    def body(ys_ref, ya_ref, gs_ref, ga_ref, wbs_ref, wba_ref, wout_ref, g2_ref, x_ref, *rest):
        o_ref = rest[-1]
        *_, out, inv = _merge_fwd_math(ys_ref[...].astype(BF16), ya_ref[...].astype(BF16), gs_ref[...],
                                       ga_ref[...], wbs_ref, wba_ref, wout_ref)
        o_ref[...] = x_ref[...] + out * inv * g2_ref[...]

    row = pl.BlockSpec((tl, D), lambda i: (i, 0))
    return _pcall(body, name="merge_out_fwd", grid=(L // tl,), in_specs=_merge_specs(tl) + [row] + dep_specs,
                  out_specs=row, out_shape=_sds((L, D)),
                  vmem_mb=48)(ys, ya, proj, proj, wbs4, wba4, wout, g2, x, *deps)


def _merge_out_bwd(dxn, ys, ya, proj, wbs4, wba4, wout, g2, dep=None):
    tl = 256
    ni = L // tl
    cw = D // NCHIP
    deps, dep_specs = _token_operand(dep)

    def body(dxn_ref, ys_ref, ya_ref, gs_ref, ga_ref, wbs_ref, wba_ref, wout_ref, g2_ref, *rest):
        (dys_ref, dya_ref, dgate_ref, dwbs_ref, dwba_ref, dwout_ref, dg2_ref,
         abs_ref, aba_ref, aout_ref) = rest[len(deps):]
        i = pl.program_id(0)

        @pl.when(i == 0)
        def _():
            abs_ref[...] = jnp.zeros(abs_ref.shape, F32)
            aba_ref[...] = jnp.zeros(aba_ref.shape, F32)
            aout_ref[...] = jnp.zeros(aout_ref.shape, F32)
            dg2_ref[...] = jnp.zeros(dg2_ref.shape, F32)

        ys_b, ya_b = ys_ref[...].astype(BF16), ya_ref[...].astype(BF16)
        bs, ba, sgs, sga, merged, out, inv = _merge_fwd_math(ys_b, ya_b, gs_ref[...], ga_ref[...],
                                                             wbs_ref, wba_ref, wout_ref)
        nrm = out * inv
        dxn = dxn_ref[...]
        dg2_ref[...] += jnp.sum(dxn * nrm, axis=0, keepdims=True)
        dn = dxn * g2_ref[...]
        dout = (inv * (dn - nrm * jnp.mean(dn * nrm, axis=-1, keepdims=True))).astype(BF16)
        aout_ref[...] += _dot_tn(merged.astype(BF16), dout)
        dm = _dot_nt(dout, wout_ref[...])
        dbs, dba = dm * sgs, dm * sga
        dgate_ref[:, :D] = (dm * bs * (sgs * (1.0 - sgs))).astype(BF16)
        dgate_ref[:, D:] = (dm * ba * (sga * (1.0 - sga))).astype(BF16)
        dbs_b, dba_b = dbs.astype(BF16), dba.astype(BF16)
        dys = None
        dya = None
        for s in range(NCHIP):
            cs = slice(s * cw, (s + 1) * cw)
            p_s = _dot_nt(dbs_b[:, cs], wbs_ref[s])
            p_a = _dot_nt(dba_b[:, cs], wba_ref[s])
            dys = p_s if dys is None else dys + p_s
            dya = p_a if dya is None else dya + p_a
            abs_ref[s] += _dot_tn(ys_b, dbs_b[:, cs])
            aba_ref[s] += _dot_tn(ya_b, dba_b[:, cs])

        dys_ref[...] = dys
        dya_ref[...] = dya

        @pl.when(i == ni - 1)
        def _():
            dwbs_ref[...] = abs_ref[...].astype(BF16)
            dwba_ref[...] = aba_ref[...].astype(BF16)
            dwout_ref[...] = aout_ref[...].astype(BF16)

    row = lambda w: pl.BlockSpec((tl, w), lambda i: (i, 0))
    w4 = pl.BlockSpec((NCHIP, SSM, cw), lambda i: (0, 0, 0))
    sq = pl.BlockSpec((D, D), lambda i: (0, 0))
    vec = pl.BlockSpec((1, D), lambda i: (0, 0))
    return _pcall(body, name="merge_out_bwd", grid=(ni,), in_specs=[row(D)] + _merge_specs(tl) + dep_specs,
                  out_specs=[row(SSM), row(AW), pl.BlockSpec((tl, 2 * D), lambda i: (i, COL_GS // 16)), w4, w4, sq,
                             vec],
                  out_shape=[_sds((L, SSM)), _sds((L, AW)), _sds((L, NCOL), BF16),
                             _sds((NCHIP, SSM, cw), BF16), _sds((NCHIP, AW, cw), BF16), _sds((D, D), BF16),
                             _sds((1, D))],
                  scratch=[pltpu.VMEM((NCHIP, SSM, cw), F32), pltpu.VMEM((NCHIP, AW, cw), F32),
                           pltpu.VMEM((D, D), F32)],
                  vmem_mb=56)(dxn, ys, ya, proj, proj, wbs4, wba4, wout, g2, *deps)


def _loss_head(y, target):
    tl = 512

    def body(y_ref, t_ref, loss_ref, dy_ref):
        i = pl.program_id(0)
        err = y_ref[...] - t_ref[...]
        dy_ref[...] = err / D
        part = 0.5 * jnp.sum(jnp.mean(err * err, axis=-1, keepdims=True), axis=0, keepdims=True)
        part = jnp.broadcast_to(part, (8, 128))

        @pl.when(i == 0)
        def _():
            loss_ref[...] = part

        @pl.when(i > 0)
        def _():
            loss_ref[...] += part

    row = pl.BlockSpec((tl, D), lambda i: (i, 0))
    return _pcall(body, name="loss_head", grid=(L // tl,), in_specs=[row, row],
                  out_specs=[pl.BlockSpec((8, 128), lambda i: (0, 0)), row],
                  out_shape=[_sds((8, 128)), _sds((L, D))], vmem_mb=40)(y, target)


def _adamw_math(w, g, m, v):
    m = ADAM_B1 * m + (1.0 - ADAM_B1) * g
    v = ADAM_B2 * v + (1.0 - ADAM_B2) * (g * g)
    m_hat = m / (1.0 - ADAM_B1 ** ADAM_STEP)
    v_hat = v / (1.0 - ADAM_B2 ** ADAM_STEP)
    delta = -ADAM_LR * (m_hat / (jnp.sqrt(v_hat) + ADAM_EPS) + ADAM_WD * w)
    return delta, m, v


def _adamw_big(layer, w, m, v, own, sib, prev):
    _, r, c = w.shape
    tr = min(r, 128)

    def body(w_ref, m_ref, v_ref, own_ref, sib_ref, *rest):
        g_ref, d_ref, nm_ref, nv_ref = rest[-4:]
        a = own_ref[0].astype(F32)
        b = sib_ref[0].astype(F32)
        for s in range(1, NCHIP):
            a = a + own_ref[s].astype(F32)
            b = b + sib_ref[s].astype(F32)
        g = a + b
        delta, nm, nv = _adamw_math(w_ref[...], g, m_ref[...], v_ref[...])
        g_ref[...] = g
        d_ref[...] = delta
        nm_ref[...] = nm
        nv_ref[...] = nv

    lay = pl.BlockSpec((None, tr, c), lambda i: (layer, i, 0))
    slots = pl.BlockSpec((NCHIP, tr, c), lambda i: (0, i, 0))
    ins = [w, m, v, own, sib]
    in_specs = [lay, lay, lay, slots, slots]
    aliases = {}
    if prev is not None:
        ins += list(prev)
        in_specs += [pl.BlockSpec(memory_space=pl.ANY)] * 4
        aliases = {5 + k: k for k in range(4)}
    return _pcall(body, name="adamw_big", grid=(r // tr,), in_specs=in_specs, out_specs=[lay] * 4,
                  out_shape=[_sds(w.shape)] * 4, aliases=aliases, vmem_mb=48)(*ins)


SMALL_TILE = 512


def _adamw_small(w, parts, m, v):
    r = w.shape[0]

    def body(w_ref, p_ref, m_ref, v_ref, g_ref, d_ref, nm_ref, nv_ref):
        g = p_ref[0].astype(F32)
        for dev in range(1, 8):
            g = g + p_ref[dev].astype(F32)
        delta, nm, nv = _adamw_math(w_ref[...], g, m_ref[...], v_ref[...])
        g_ref[...] = g
        d_ref[...] = delta
        nm_ref[...] = nm
        nv_ref[...] = nv

    row = pl.BlockSpec((SMALL_TILE, 128), lambda i: (i, 0))
    return _pcall(body, name="adamw_small", grid=(r // SMALL_TILE,),
                  in_specs=[row, pl.BlockSpec((8, SMALL_TILE, 128), lambda i: (0, i, 0)), row, row],
                  out_specs=[row] * 4, out_shape=[_sds((r, 128))] * 4, vmem_mb=40)(w, parts, m, v)


def _place():
    x, y, c = lax.axis_index("x"), lax.axis_index("y"), lax.axis_index("c")
    return x, y, c, 2 * x + y


def _chip_peer(x, y, j):
    return (1 - x if j & 2 else x), (1 - y if j & 1 else y)


_HBM = pl.BlockSpec(memory_space=pltpu.HBM)
_SEM = pl.BlockSpec(memory_space=pltpu.SEMAPHORE)
_EFFECT = pltpu.SideEffectType.DATAFLOW_SIDE_EFFECTING


def _in_hbm(a):
    return pltpu.with_memory_space_constraint(a, pltpu.HBM)


def _copies_start(name, plan, ncopy, srcs, lands, dep=None):
    n = len(srcs) + len(lands)
    deps, dep_specs = _token_operand(dep)

    def body(*refs):
        send_sems, recv_sems = refs[n + len(deps)], refs[n + len(deps) + 1]
        token = refs[-1]
        for k, (src, dst, dev) in enumerate(plan(refs[:len(srcs)], refs[len(srcs):n])):
            if dev is None:
                pltpu.make_async_copy(src, dst, recv_sems.at[k]).start()
            else:
                pltpu.make_async_remote_copy(src_ref=src, dst_ref=dst, send_sem=send_sems.at[k],
                                             recv_sem=recv_sems.at[k], device_id=dev, device_id_type=MESH).start()
        token[...] = jnp.zeros_like(token)

    bufs = list(srcs) + list(lands)
    outs = pl.pallas_call(
        body, name=name,
        out_shape=(pltpu.SemaphoreType.DMA((ncopy,)), pltpu.SemaphoreType.DMA((ncopy,)),
                   *[pltpu.HBM(a.shape, a.dtype) for a in bufs], _sds((8, 128))),
        in_specs=[_HBM] * n + dep_specs,
        out_specs=(_SEM, _SEM, *[_HBM] * n, pl.BlockSpec(memory_space=pltpu.VMEM)),
        input_output_aliases={i: 2 + i for i in range(n)},
        compiler_params=pltpu.CompilerParams(has_side_effects=_EFFECT),
    )(*[_in_hbm(a) for a in bufs], *deps)
    return outs[0], outs[1], list(outs[2:2 + len(srcs)]), list(outs[2 + len(srcs):2 + n]), outs[-1]


def _copies_wait(name, plan, send_sems, recv_sems, srcs, lands, after):
    n = len(srcs) + len(lands)
    after = list(after)

    def body(*refs):
        s_sems, r_sems = refs[n], refs[n + 1]
        for k, (src, dst, dev) in enumerate(plan(refs[:len(srcs)], refs[len(srcs):n])):
            if dev is None:
                pltpu.make_async_copy(src, dst, r_sems.at[k]).wait()
                continue
            cp = pltpu.make_async_remote_copy(src_ref=src, dst_ref=dst, send_sem=s_sems.at[k],
                                              recv_sem=r_sems.at[k], device_id=dev, device_id_type=MESH)
            cp.wait_send()
            cp.wait_recv()

    bufs = list(srcs) + list(lands)
    outs = pl.pallas_call(
        body, name=name,
        out_shape=tuple(pltpu.HBM(a.shape, a.dtype) for a in bufs),
        in_specs=[_HBM] * n + [_SEM, _SEM] + [pl.BlockSpec(memory_space=pl.ANY)] * len(after),
        out_specs=tuple([_HBM] * n),
        input_output_aliases={i: i for i in range(n)},
        compiler_params=pltpu.CompilerParams(has_side_effects=_EFFECT),
    )(*bufs, send_sems, recv_sems, *after)
    return list(outs[:len(srcs)]), list(outs[len(srcs):])


def _with_own_slot(block, slot, nslots=NCHIP):
    land = lax.empty((nslots,) + block.shape, block.dtype)
    return lax.dynamic_update_slice(land, block[None], (slot,) + (0,) * block.ndim)


def _my_half(land, slot, c):
    half = land.shape[1] // 2
    return land.at[slot, pl.ds(pl.multiple_of(c * half, 16), half)]


def _gather_ici_plan(srcs, lands):
    x, y, c, s = _place()
    return [(_my_half(land, s, c), _my_half(land, s, c), (*_chip_peer(x, y, j), c))
            for land in lands for j in range(1, NCHIP)]


def _gather_d2d_plan(srcs, lands):
    x, y, c, s = _place()
    return [(_my_half(land, s ^ j, c), _my_half(land, s ^ j, c), (x, y, 1 - c))
            for land in lands for j in range(1, NCHIP)]


def _exchange_plan(srcs, lands):
    x, y, c, s = _place()
    return [(src.at[s ^ j], land.at[s], (*_chip_peer(x, y, j), c) if j else None)
            for src, land in zip(srcs, lands) for j in range(NCHIP)]


def _sibling_plan(srcs, lands):
    x, y, c, _ = _place()
    return [(src, land, (x, y, 1 - c)) for src, land in zip(srcs, lands)]


def _everyone_plan(srcs, lands):
    x, y, c, _ = _place()
    me = 4 * x + 2 * y + c
    return [(srcs[0], lands[0].at[me], (*_chip_peer(x, y, j >> 1), 1 - c if j & 1 else c)) for j in range(1, 8)]


def _flatten_small(parts):
    flat = jnp.concatenate([p.reshape(-1) for p in parts])
    n = flat.shape[0]
    rows = -(-n // (128 * SMALL_TILE)) * SMALL_TILE
    return jnp.pad(flat, (0, rows * 128 - n)).reshape(rows, 128)


def _unflatten_small(buf, like):
    flat = buf.reshape(-1)
    out, at = [], 0
    for p in like:
        out.append(flat[at:at + p.size].reshape(p.shape))
        at += p.size
    return out


def kernel(x, pre_norm_g, w_in, lambda_re, lambda_im, log_dt, b_re, b_im, c_re, c_im, d_skip, w_glu, b_glu, w_branch_s, w_branch_a, w_out, post_norm_g, loss_target, m_pre_norm_g, m_w_in, m_lambda_re, m_lambda_im, m_log_dt, m_b_re, m_b_im, m_c_re, m_c_im, m_d_skip, m_w_glu, m_b_glu, m_w_branch_s, m_w_branch_a, m_w_out, m_post_norm_g, v_pre_norm_g, v_w_in, v_lambda_re, v_lambda_im, v_log_dt, v_b_re, v_b_im, v_c_re, v_c_im, v_d_skip, v_w_glu, v_b_glu, v_w_branch_s, v_w_branch_a, v_w_out, v_post_norm_g):
    nb = DEPTH * NCB
    lre_c = lambda_re.reshape(nb, 1, HW)
    lim_c = lambda_im.reshape(nb, 1, HW)
    ldt_c = jnp.broadcast_to(log_dt[:, :, None], (DEPTH, 32, 64)).reshape(nb, 1, HW)
    b_rows = lambda t: t.reshape(DEPTH, NCB, 8, 64, 16).transpose(0, 1, 4, 2, 3).reshape(nb, 16, HW)
    c_rows = lambda t: t.reshape(DEPTH, NCB, 8, 16, 64).transpose(0, 1, 3, 2, 4).reshape(nb, 16, HW)
    bre_c, bim_c = b_rows(b_re), b_rows(b_im)
    ar, ai, wb, wct = _s5_prep(lre_c, lim_c, ldt_c, bre_c, bim_c, c_rows(c_re), c_rows(c_im))
    ar4 = ar.reshape(DEPTH, NCB, 1, HW)
    ai4 = ai.reshape(DEPTH, NCB, 1, HW)
    wb = wb.reshape(DEPTH, NCB, 128, SW)
    wct = wct.reshape(DEPTH, NCB, 128, SW)

    chip = 2 * lax.axis_index("x") + lax.axis_index("y")
    device = 2 * chip + lax.axis_index("c")

    def gather_start(l, dep):
        blocks = [w_in[l], w_glu[l], w_branch_s[l], w_branch_a[l], w_out[l]]
        if l == 0:
            blocks = lax.optimization_barrier(blocks)
        lands = [_with_own_slot(a.astype(BF16), chip) for a in blocks]
        return _copies_start(f"gather_start_{l}", _gather_ici_plan, 15, [], lands, dep)

    def gather_forward(l, flight, after):
        send, recv, _, lands, _ = flight
        _, lands = _copies_wait(f"gather_wait_{l}", _gather_ici_plan, send, recv, [], lands, after)
        return _copies_start(f"forward_start_{l}", _gather_d2d_plan, 15, [], lands)

    def gather_done(l, forward, after):
        send, recv, _, lands, _ = forward
        return _copies_wait(f"forward_wait_{l}", _gather_d2d_plan, send, recv, [], lands, after)[1]

    small_w = [pre_norm_g, lambda_re, lambda_im, log_dt, b_re, b_im, c_re, c_im, d_skip, b_glu, post_norm_g]
    small_m = [m_pre_norm_g, m_lambda_re, m_lambda_im, m_log_dt, m_b_re, m_b_im, m_c_re, m_c_im, m_d_skip,
               m_b_glu, m_post_norm_g]
    small_v = [v_pre_norm_g, v_lambda_re, v_lambda_im, v_log_dt, v_b_re, v_b_im, v_c_re, v_c_im, v_d_skip,
               v_b_glu, v_post_norm_g]
    flat_w, flat_m, flat_v = _flatten_small(small_w), _flatten_small(small_m), _flatten_small(small_v)

    xs = [x[0]]
    saved = []
    forward = gather_forward(0, gather_start(0, None), [xs[0], flat_w, flat_m, ar, ai, wb, wct])
    for l in range(DEPTH):
        w4, wglu4, wbs4, wba4, wout4 = gather_done(l, forward, [xs[l]] + ([flat_v] if l == 0 else []))
        started = None
        if l + 1 < DEPTH:
            flight = gather_start(l + 1, wglu4)
            started = flight[4]
        wglu = wglu4.reshape(SSM, SSM)
        wout = wout4.reshape(D, D)
        wb4, wc4 = wb[l], wct[l]
        g1 = pre_norm_g[l].reshape(1, D)
        g2 = post_norm_g[l].reshape(1, D)
        dsk = d_skip[l].reshape(1, SSM)
        bgl = b_glu[l].reshape(1, SSM)
        proj, ht = _rms_proj_fwd(xs[l], g1, w4, started)
        states, yraw = _s5_scan_fwd(proj, wb4, wc4, ar4[l], ai4[l])
        o, lse = (_in_hbm(a) for a in _attn_core_fwd(proj))
        forwarded = None
        if l + 1 < DEPTH:
            forward = gather_forward(l + 1, flight, [yraw, o])
            forwarded = forward[4]
        ys = _s5_tail_fwd(yraw, proj, dsk, wglu, bgl)
        ya = _attn_mix_fwd(o, lse, proj)
        xs.append(_merge_out_fwd(ys, ya, proj, wbs4, wba4, wout, g2, xs[l], forwarded))
        saved.append((w4, wglu, wbs4, wba4, wout, wb4, wc4, g1, g2, dsk, bgl, proj, ht, states, yraw, ys, ya, o, lse))

    loss_part, dx = _loss_head(xs[DEPTH], loss_target[0])
    loss = lax.psum(loss_part[0, 0], ("x", "y", "c"))

    big_w = (w_in, w_glu.reshape(DEPTH, 128, SSM), w_branch_s, w_branch_a, w_out)
    big_m = (m_w_in, m_w_glu, m_w_branch_s, m_w_branch_a, m_w_out)
    big_v = (v_w_in, v_w_glu, v_w_branch_s, v_w_branch_a, v_w_out)
    big_out = [None] * 5
    small = {k: [None] * DEPTH for k in ("g1", "da", "dbb", "dcc", "dsk", "bgl", "g2")}
    ici = [None] * DEPTH
    d2d = [None] * DEPTH

    def exchange_start(l, parts):
        lands = [lax.empty(p.shape, p.dtype) for p in parts]
        ici[l] = _copies_start(f"exchange_start_{l}", _exchange_plan, 20, parts, lands)

    def handoff_start(l, after):
        send, recv, srcs, lands, _ = ici[l]
        _, own = _copies_wait(f"exchange_wait_{l}", _exchange_plan, send, recv, srcs, lands, after)
        d2d[l] = _copies_start(f"handoff_start_{l}", _sibling_plan, 5, own,
                               [lax.empty(a.shape, a.dtype) for a in own])

    def update(l, after):
        send, recv, own, lands, _ = d2d[l]
        own, sib = _copies_wait(f"handoff_wait_{l}", _sibling_plan, send, recv, own, lands, after)
        for k in range(5):
            big_out[k] = _adamw_big(l, big_w[k], big_m[k], big_v[k], own[k], sib[k], big_out[k])

    for l in reversed(range(DEPTH)):
        w4, wglu, wbs4, wba4, wout, wb4, wc4, g1, g2, dsk, bgl, proj, ht, states, yraw, ys, ya, o, lse = saved[l]
        started = None
        if l + 1 < DEPTH:
            started = ici[l + 1][4] + d2d[l + 2][4] if l + 2 < DEPTH else ici[l + 1][4]
        dys, dya, dproj, dwbs, dwba, dwout, dg2 = _merge_out_bwd(dx, ys, ya, proj, wbs4, wba4, wout, g2, started)
        dyraw, du_skip, dproj, dwglu, dbgl, ddsk = _s5_tail_bwd(dys, yraw, proj, dsk, wglu, bgl, dproj)
        dproj, dbb, dcc, da = _s5_scan_bwd(dyraw, states, proj, du_skip, wb4, wc4, ar4[l], ai4[l], dproj)
        do, cterm, dproj = _attn_mix_bwd(dya, o, lse, proj, dproj)
        dproj = _attn_core_bwd(proj, do, lse, cterm, dproj)
        dh, dwin = _proj_bwd(dproj, ht, w4)
        parts = [dwin, dwglu.reshape(NCHIP, 128, SSM), dwbs, dwba, dwout.reshape(NCHIP, D // NCHIP, D)]
        exchange_start(l, parts)
        dx, dg1 = _rms_bwd(dh, xs[l], g1, dx, ici[l][4])
        if l + 1 < DEPTH:
            handoff_start(l + 1, [ici[l][4]])
        small["g1"][l], small["g2"][l], small["dsk"][l], small["bgl"][l] = dg1, dg2, ddsk, dbgl
        small["da"][l], small["dbb"][l], small["dcc"][l] = da, dbb, dcc

    dlre, dlim, dldt, dbre_c, dbim_c = _s5_prep_bwd(
        lre_c, lim_c, ldt_c, bre_c, bim_c, jnp.stack(small["da"]).reshape(nb, 1, SW),
        jnp.stack(small["dbb"]).reshape(nb, 16, SW))
    dcc = jnp.stack(small["dcc"]).reshape(nb, 16, SW)
    b_back = lambda t: t.reshape(DEPTH, NCB, 16, 8, 64).transpose(0, 1, 3, 4, 2).reshape(DEPTH, 32, 64, 16)
    c_back = lambda t: t.reshape(DEPTH, NCB, 16, 8, 64).transpose(0, 1, 3, 2, 4).reshape(DEPTH, 32, 16, 64)
    small_g = [jnp.stack(small["g1"]).reshape(DEPTH, D), dlre.reshape(DEPTH, 32, 64), dlim.reshape(DEPTH, 32, 64),
               dldt.reshape(DEPTH, 32, 64).sum(-1), b_back(dbre_c), b_back(dbim_c), c_back(dcc[:, :, :HW]),
               c_back(dcc[:, :, HW:]),
               jnp.stack(small["dsk"]).reshape(DEPTH, SSM), jnp.stack(small["bgl"]).reshape(DEPTH, SSM),
               jnp.stack(small["g2"]).reshape(DEPTH, D)]
    part = _flatten_small(small_g).astype(BF16)
    send, recv, srcs, lands, token = _copies_start("small_start", _everyone_plan, 7, [part],
                                                   [_with_own_slot(part, device, 8)])
    update(DEPTH - 1, [token, ici[0][4]])
    for l in range(DEPTH - 2, 0, -1):
        update(l, [out[0] for out in big_out])
    handoff_start(0, [out[0] for out in big_out])
    update(0, [d2d[0][4]])
    _, (parts8,) = _copies_wait("small_wait", _everyone_plan, send, recv, srcs, lands, [out[0] for out in big_out])
    g_flat, d_flat, nm_flat, nv_flat = _adamw_small(flat_w, parts8, flat_m, flat_v)
    sg = _unflatten_small(g_flat, small_w)
    sd = _unflatten_small(d_flat, small_w)
    snm = _unflatten_small(nm_flat, small_w)
    snv = _unflatten_small(nv_flat, small_w)

    def ordered(sm, which):
        bg = [big_out[k][which] for k in range(5)]
        bg[1] = bg[1].reshape(DEPTH, 128, SSM)
        return [sm[0], bg[0], sm[1], sm[2], sm[3], sm[4], sm[5], sm[6], sm[7], sm[8], bg[1], sm[9], bg[2], bg[3],
                bg[4], sm[10]]

    return (loss, dx[None], *ordered(sg, 0), *ordered(sd, 1), *ordered(snm, 2), *ordered(snv, 3))
```

```python
import math

import jax
import jax.numpy as jnp
from jax import lax
from jax.experimental import pallas as pl
from jax.experimental.pallas import tpu as pltpu

F32 = jnp.float32
BF16 = jnp.bfloat16
MESH = pl.DeviceIdType.MESH

DEPTH = 4
L = 2048
D = 1024
NCOL = 8192
SSM = 512
AW = 512
QKV = 1536
RMS_EPS = 1e-6
NCHIP = 4

COL_U, COL_ZS, COL_Q, COL_K, COL_V, COL_ZA, COL_GS, COL_GA = 0, 4, 8, 20, 32, 44, 48, 56

NSEG = 8
SEG = L // NSEG
NCB = 4
HW = 512
SW = 2 * HW

HEAD = 128
DILATIONS = (1, 4, 16)
SCALE = HEAD ** -0.5
NEG = -1e30

ADAM_LR, ADAM_B1, ADAM_B2, ADAM_EPS, ADAM_WD, ADAM_STEP = 0.001, 0.9, 0.999, 1e-08, 0.01, 10


def _sds(shape, dtype=F32):
    return jax.ShapeDtypeStruct(shape, dtype)


def _pcall(body, *, name, out_shape, grid=None, in_specs=None, out_specs=None, scratch=(), vmem_mb=None,
           aliases=None):
    params = {}
    if vmem_mb is not None:
        params["vmem_limit_bytes"] = vmem_mb << 20
    kw = {}
    if grid is not None:
        kw["grid"] = grid
    if in_specs is not None:
        kw["in_specs"] = in_specs
    if out_specs is not None:
        kw["out_specs"] = out_specs
    return pl.pallas_call(body, name=name, out_shape=out_shape, scratch_shapes=list(scratch),
                          compiler_params=pltpu.CompilerParams(**params),
                          input_output_aliases=aliases or {}, **kw)


def _dot(a, b):
    return jnp.dot(a, b, preferred_element_type=F32)


def _dot_nt(a, b):
    return lax.dot_general(a, b, (((1,), (1,)), ((), ())), preferred_element_type=F32)


def _dot_tn(a, b):
    return lax.dot_general(a, b, (((0,), (0,)), ((), ())), preferred_element_type=F32)


def _sigmoid(x):
    return jax.nn.sigmoid(x)


_GELU_K = math.sqrt(2.0 / math.pi)


def _gelu(x):
    return 0.5 * x * (1.0 + jnp.tanh(_GELU_K * (x + 0.044715 * (x * x * x))))


def _gelu_grad(x):
    t = jnp.tanh(_GELU_K * (x + 0.044715 * (x * x * x)))
    return 0.5 * (1.0 + t) + 0.5 * x * (1.0 - t * t) * (_GELU_K * (1.0 + 3.0 * 0.044715 * (x * x)))


def _token_operand(dep):
    if dep is None:
        return [], []
    return [dep], [pl.BlockSpec(memory_space=pl.ANY)]


def _rms_proj_fwd(x, g, w4, dep=None):
    tl, tn = 512, 1024
    ni = L // tl
    deps, dep_specs = _token_operand(dep)

    def body(x_ref, g_ref, w_ref, *rest):
        o_ref, h_ref = rest[len(deps):]
        rows = pl.ds(pl.multiple_of(pl.program_id(1) * tl, tl), tl)

        @pl.when(pl.program_id(0) == 0)
        def _():
            xx = x_ref[...]
            inv = lax.rsqrt(jnp.mean(xx * xx, axis=-1, keepdims=True) + RMS_EPS)
            h_ref[rows, :] = (xx * inv * g_ref[...]).astype(BF16)

        o_ref[...] = _dot(h_ref[rows, :], w_ref[...])

    proj, h = _pcall(
        body, name="rms_proj_fwd", grid=(NCOL // tn, ni),
        in_specs=[pl.BlockSpec((tl, D), lambda j, i: (jnp.where(j == 0, i, ni - 1), 0)),
                  pl.BlockSpec((1, D), lambda j, i: (0, 0)),
                  pl.BlockSpec((None, D, tn), lambda j, i: (lax.div(j, 2), 0, lax.rem(j, 2)))] + dep_specs,
        out_specs=[pl.BlockSpec((tl, tn), lambda j, i: (i, j)), pl.BlockSpec((L, D), lambda j, i: (0, 0))],
        out_shape=[_sds((L, NCOL)), _sds((L, D), BF16)], vmem_mb=48)(x, g, w4, *deps)

    return proj, h


def _proj_bwd(dproj, ht, w4):
    tn = 1024

    def body(dp_ref, ht_ref, w_ref, dh_ref, dw_ref):
        @pl.when(pl.program_id(0) == 0)
        def _():
            dh_ref[...] = jnp.zeros((L, D), F32)

        dw_ref[...] = _dot_tn(ht_ref[...], dp_ref[...]).astype(BF16)
        dh_ref[...] += _dot_nt(dp_ref[...], w_ref[...])

    return _pcall(
        body, name="proj_bwd", grid=(NCOL // tn,),
        in_specs=[pl.BlockSpec((L, tn), lambda j: (0, j)),
                  pl.BlockSpec((L, D), lambda j: (0, 0)),
                  pl.BlockSpec((None, D, tn), lambda j: (lax.div(j, 2), 0, lax.rem(j, 2)))],
        out_specs=[pl.BlockSpec((L, D), lambda j: (0, 0)),
                   pl.BlockSpec((None, D, tn), lambda j: (lax.div(j, 2), 0, lax.rem(j, 2)))],
        out_shape=[_sds((L, D)), _sds((NCHIP, D, NCOL // NCHIP), BF16)], vmem_mb=56)(dproj, ht, w4)


def _rms_bwd(dh, x, g, dxn, dep=None):
    tl = 512
    deps, dep_specs = _token_operand(dep)

    def body(dh_ref, x_ref, g_ref, dxn_ref, *rest):
        dx_ref, dg_ref = rest[len(deps):]
        i = pl.program_id(0)
        xx = x_ref[...]
        inv = lax.rsqrt(jnp.mean(xx * xx, axis=-1, keepdims=True) + RMS_EPS)
        nrm = xx * inv
        dh_v = dh_ref[...]
        dn = dh_v * g_ref[...]
        dx_ref[...] = dxn_ref[...] + inv * (dn - nrm * jnp.mean(dn * nrm, axis=-1, keepdims=True))
        part = jnp.sum(dh_v * nrm, axis=0, keepdims=True)

        @pl.when(i == 0)
        def _():
            dg_ref[...] = part

        @pl.when(i > 0)
        def _():
            dg_ref[...] += part

    row = pl.BlockSpec((tl, D), lambda i: (i, 0))
    vec = pl.BlockSpec((1, D), lambda i: (0, 0))
    return _pcall(body, name="rms_bwd", grid=(L // tl,), in_specs=[row, row, vec, row] + dep_specs,
                  out_specs=[row, vec], out_shape=[_sds((L, D)), _sds((1, D))], vmem_mb=40)(dh, x, g, dxn, *deps)


def _s5_param_math(lre, lim, ldt, bre, bim):
    lr = jnp.minimum(lre, -1e-4)
    dt = jnp.exp(ldt)
    mag = jnp.exp(lr * dt)
    ar = mag * jnp.cos(lim * dt)
    ai = mag * jnp.sin(lim * dt)
    den = lr * lr + lim * lim
    nr = ar - 1.0
    qr = (nr * lr + ai * lim) / den
    qi = (ai * lr - nr * lim) / den
    return ar, ai, qr * bre - qi * bim, qr * bim + qi * bre


def _group_of_lane(width):
    lane = lax.broadcasted_iota(jnp.int32, (16, width), 1)
    return lax.shift_right_logical(jnp.bitwise_and(lane, HW - 1), 6)


def _vec_spec():
    return pl.BlockSpec((None, 1, HW), lambda n: (n, 0, 0))


def _mat_spec():
    return pl.BlockSpec((None, 16, HW), lambda n: (n, 0, 0))


def _s5_prep(lre, lim, ldt, bre, bim, cre, cim):
    n = lre.shape[0]

    def body(lre_ref, lim_ref, ldt_ref, bre_ref, bim_ref, cre_ref, cim_ref, ar_ref, ai_ref, wb_ref, wct_ref):
        ar, ai, bbr, bbi = _s5_param_math(lre_ref[...], lim_ref[...], ldt_ref[...], bre_ref[...], bim_ref[...])
        ar_ref[...] = ar
        ai_ref[...] = ai
        group = _group_of_lane(SW)
        bb = jnp.concatenate([bbr, bbi], axis=1)
        cc = jnp.concatenate([cre_ref[...], -cim_ref[...]], axis=1)
        for g in range(8):
            rows = pl.ds(16 * g, 16)
            wb_ref[rows, :] = jnp.where(group == g, bb, 0.0).astype(BF16)
            wct_ref[rows, :] = jnp.where(group == g, cc, 0.0).astype(BF16)

    wide = pl.BlockSpec((None, 128, SW), lambda n: (n, 0, 0))
    return _pcall(body, name="s5_prep", grid=(n,),
                  in_specs=[_vec_spec()] * 3 + [_mat_spec()] * 4,
                  out_specs=[_vec_spec(), _vec_spec(), wide, wide],
                  out_shape=[_sds((n, 1, HW)), _sds((n, 1, HW)), _sds((n, 128, SW), BF16), _sds((n, 128, SW), BF16)])(
                      lre, lim, ldt, bre, bim, cre, cim)


def _s5_prep_bwd(lre, lim, ldt, bre, bim, ga, gbb):
    n = lre.shape[0]

    def body(lre_ref, lim_ref, ldt_ref, bre_ref, bim_ref, ga_ref, gbb_ref,
             dlre_ref, dlim_ref, dldt_ref, dbre_ref, dbim_ref):
        _, vjp = jax.vjp(_s5_param_math, lre_ref[...], lim_ref[...], ldt_ref[...], bre_ref[...], bim_ref[...])
        ga, gbb = ga_ref[...], gbb_ref[...]
        dlre, dlim, dldt, dbre, dbim = vjp((ga[:, :HW], ga[:, HW:], gbb[:, :HW], gbb[:, HW:]))
        dlre_ref[...] = dlre
        dlim_ref[...] = dlim
        dldt_ref[...] = dldt
        dbre_ref[...] = dbre
        dbim_ref[...] = dbim

    return _pcall(body, name="s5_prep_bwd", grid=(n,),
                  in_specs=[_vec_spec()] * 3 + [_mat_spec()] * 2 +
                  [pl.BlockSpec((None, 1, SW), lambda n: (n, 0, 0)), pl.BlockSpec((None, 16, SW), lambda n: (n, 0, 0))],
                  out_specs=[_vec_spec()] * 3 + [_mat_spec()] * 2,
                  out_shape=[_sds((n, 1, HW))] * 3 + [_sds((n, 16, HW))] * 2)(lre, lim, ldt, bre, bim, ga, gbb)


def _diag_blocks(full):
    group = _group_of_lane(SW)
    out = jnp.where(group == 0, full[0:16, :], 0.0)
    for g in range(1, 8):
        out = out + jnp.where(group == g, full[16 * g:16 * g + 16, :], 0.0)
    return out


def _permute_rows(dst_ref, src_ref):
    for s in range(NSEG):
        dst_ref[pl.ds(s, SEG, stride=NSEG), :] = src_ref[pl.ds(s * SEG, SEG), :]


def _unpermute_rows(src_ref, s):
    return src_ref[pl.ds(s, SEG, stride=NSEG), :]


def _row_of(slab, rows, j):
    return jnp.sum(jnp.where(rows == j, slab, 0.0), axis=0, keepdims=True)


def _s5_scan_fwd(proj, wb4, wc4, ar4, ai4):
    def body(u_ref, wb_ref, wc_ref, ar_ref, ai_ref, x_ref, y_ref, up_ref, d_ref, yp_ref):
        _permute_rows(up_ref, u_ref)
        d_ref[...] = _dot(up_ref[...].astype(BF16), wb_ref[...])
        ar = jnp.broadcast_to(ar_ref[...], (NSEG, HW))
        ai = jnp.broadcast_to(ai_ref[...], (NSEG, HW))
        zero = jnp.zeros((NSEG, HW), F32)

        def drive(t):
            row = pl.multiple_of(t * NSEG, NSEG)
            dd = d_ref[pl.ds(row, NSEG), :]
            return row, dd[:, :HW], dd[:, HW:]

        def local_step(t, c):
            xr, xi = c
            _, dr, di = drive(t)
            return ar * xr - ai * xi + dr, ar * xi + ai * xr + di

        fr, fi = lax.fori_loop(0, SEG, local_step, (zero, zero), unroll=8)
        pr, pi_ = ar_ref[...], ai_ref[...]
        for _ in range(int(math.log2(SEG))):
            pr, pi_ = pr * pr - pi_ * pi_, 2.0 * pr * pi_
        rows = lax.broadcasted_iota(jnp.int32, (NSEG, HW), 0)
        cr, ci = zero, zero
        for j in range(NSEG - 1):
            sr, si = _row_of(cr, rows, j), _row_of(ci, rows, j)
            gr, gi = _row_of(fr, rows, j), _row_of(fi, rows, j)
            nr = pr * sr - pi_ * si + gr
            ni = pr * si + pi_ * sr + gi
            cr = jnp.where(rows == j + 1, nr, cr)
            ci = jnp.where(rows == j + 1, ni, ci)

        def true_step(t, c):
            xr, xi = c
            row, dr, di = drive(t)
            nr = ar * xr - ai * xi + dr
            ni = ar * xi + ai * xr + di
            x_ref[pl.ds(row, NSEG), :] = jnp.concatenate([nr, ni], axis=1)
            return nr, ni

        lax.fori_loop(0, SEG, true_step, (cr, ci), unroll=8)
        yp_ref[...] = _dot_nt(x_ref[...].astype(BF16), wc_ref[...])
        for s in range(NSEG):
            y_ref[pl.ds(s * SEG, SEG), :] = _unpermute_rows(yp_ref, s)

    return _pcall(
        body, name="s5_scan_fwd", grid=(NCB,),
        in_specs=[pl.BlockSpec((L, 128), lambda b: (0, COL_U + b)),
                  pl.BlockSpec((None, 128, SW), lambda b: (b, 0, 0)),
                  pl.BlockSpec((None, 128, SW), lambda b: (b, 0, 0)),
                  pl.BlockSpec((None, 1, HW), lambda b: (b, 0, 0)),
                  pl.BlockSpec((None, 1, HW), lambda b: (b, 0, 0))],
        out_specs=[pl.BlockSpec((None, L, SW), lambda b: (b, 0, 0)),
                   pl.BlockSpec((L, 128), lambda b: (0, b))],
        out_shape=[_sds((NCB, L, SW)), _sds((L, SSM))],
        scratch=[pltpu.VMEM((L, 128), F32), pltpu.VMEM((L, SW), F32), pltpu.VMEM((L, 128), F32)],
        vmem_mb=56)(proj, wb4, wc4, ar4, ai4)


def _s5_scan_bwd(dy, xs, proj, du_skip, wb4, wc4, ar4, ai4, dproj):
    def body(dy_ref, x_ref, u_ref, dus_ref, wb_ref, wc_ref, ar_ref, ai_ref, _,
             du_ref, dbb_ref, dcc_ref, da_ref, dyp_ref, up_ref, g_ref, dup_ref):
        _permute_rows(dyp_ref, dy_ref)
        _permute_rows(up_ref, u_ref)
        dyp = dyp_ref[...].astype(BF16)
        g_ref[...] = _dot(dyp, wc_ref[...])
        dcc = _diag_blocks(_dot_tn(dyp, x_ref[...].astype(BF16)))
        dcc_ref[...] = jnp.concatenate([dcc[:, :HW], -dcc[:, HW:]], axis=1)
        ar = jnp.broadcast_to(ar_ref[...], (NSEG, HW))
        ai = jnp.broadcast_to(ai_ref[...], (NSEG, HW))
        zero = jnp.zeros((NSEG, HW), F32)

        def load(t):
            row = pl.multiple_of(t * NSEG, NSEG)
            gg = g_ref[pl.ds(row, NSEG), :]
            return row, gg[:, :HW], gg[:, HW:]

        def local_step(k, c):
            lr, li = c
            _, gr, gi = load(SEG - 1 - k)
            return ar * lr + ai * li + gr, ar * li - ai * lr + gi

        fr, fi = lax.fori_loop(0, SEG, local_step, (zero, zero), unroll=8)
        pr, pi_ = ar_ref[...], -ai_ref[...]
        for _ in range(int(math.log2(SEG))):
            pr, pi_ = pr * pr - pi_ * pi_, 2.0 * pr * pi_
        rows = lax.broadcasted_iota(jnp.int32, (NSEG, HW), 0)
        cr, ci = zero, zero
        for j in range(NSEG - 1, 0, -1):
            sr, si = _row_of(cr, rows, j), _row_of(ci, rows, j)
            gr, gi = _row_of(fr, rows, j), _row_of(fi, rows, j)
            nr = pr * sr - pi_ * si + gr
            ni = pr * si + pi_ * sr + gi
            cr = jnp.where(rows == j - 1, nr, cr)
            ci = jnp.where(rows == j - 1, ni, ci)

        def true_step(k, c):
            lr, li, gar, gai = c
            t = SEG - 1 - k
            row, gr, gi = load(t)
            nr = ar * lr + ai * li + gr
            ni = ar * li - ai * lr + gi
            g_ref[pl.ds(row, NSEG), :] = jnp.concatenate([nr, ni], axis=1)
            prow = pl.multiple_of(jnp.maximum(t - 1, 0) * NSEG, NSEG)
            xp = x_ref[pl.ds(prow, NSEG), :]
            live = (t > 0).astype(F32)
            xr, xi = xp[:, :HW] * live, xp[:, HW:] * live
            return nr, ni, gar + (nr * xr + ni * xi), gai + (ni * xr - nr * xi)

        _, _, gar, gai = lax.fori_loop(0, SEG, true_step, (cr, ci, zero, zero), unroll=4)
        l0 = g_ref[1:NSEG, :]
        xl = x_ref[(SEG - 1) * NSEG:(SEG - 1) * NSEG + NSEG - 1, :]
        l0r, l0i, xlr, xli = l0[:, :HW], l0[:, HW:], xl[:, :HW], xl[:, HW:]
        gar_t = jnp.sum(gar, axis=0, keepdims=True) + jnp.sum(l0r * xlr + l0i * xli, axis=0, keepdims=True)
        gai_t = jnp.sum(gai, axis=0, keepdims=True) + jnp.sum(l0i * xlr - l0r * xli, axis=0, keepdims=True)
        da_ref[...] = jnp.concatenate([gar_t, gai_t], axis=1)
        lam = g_ref[...].astype(BF16)
        dbb_ref[...] = _diag_blocks(_dot_tn(up_ref[...].astype(BF16), lam))
        dup_ref[...] = _dot_nt(lam, wb_ref[...])
        for s in range(NSEG):
            sl = pl.ds(s * SEG, SEG)
            du_ref[sl, :] = (_unpermute_rows(dup_ref, s) + dus_ref[sl, :]).astype(BF16)

    col = lambda off: pl.BlockSpec((L, 128), lambda b: (0, off + b))
    return _pcall(
        body, name="s5_scan_bwd", grid=(NCB,),
        in_specs=[col(0), pl.BlockSpec((None, L, SW), lambda b: (b, 0, 0)), col(COL_U), col(0),
                  pl.BlockSpec((None, 128, SW), lambda b: (b, 0, 0)),
                  pl.BlockSpec((None, 128, SW), lambda b: (b, 0, 0)),
                  pl.BlockSpec((None, 1, HW), lambda b: (b, 0, 0)),
                  pl.BlockSpec((None, 1, HW), lambda b: (b, 0, 0)),
                  pl.BlockSpec(memory_space=pl.ANY)],
        out_specs=[col(COL_U), pl.BlockSpec((None, 16, SW), lambda b: (b, 0, 0)),
                   pl.BlockSpec((None, 16, SW), lambda b: (b, 0, 0)),
                   pl.BlockSpec((None, 1, SW), lambda b: (b, 0, 0))],
        out_shape=[_sds((L, NCOL), BF16), _sds((NCB, 16, SW)), _sds((NCB, 16, SW)), _sds((NCB, 1, SW))],
        scratch=[pltpu.VMEM((L, 128), F32), pltpu.VMEM((L, 128), F32), pltpu.VMEM((L, SW), F32),
                 pltpu.VMEM((L, 128), F32)],
        aliases={8: 0}, vmem_mb=56)(dy, xs, proj, du_skip, wb4, wc4, ar4, ai4, dproj)


def _s5_tail_fwd(yraw, proj, dsk, wglu, bglu):
    tl = 512

    def body(y_ref, u_ref, z_ref, dsk_ref, w_ref, b_ref, o_ref):
        y1 = y_ref[...] + dsk_ref[...] * u_ref[...]
        y2 = _gelu(y1)
        gl = _dot(y2.astype(BF16), w_ref[...]) + b_ref[...]
        z = z_ref[...]
        o_ref[...] = (y2 * _sigmoid(gl)) * (z * _sigmoid(z))

    blk = lambda c: pl.BlockSpec((tl, SSM), lambda i: (i, c))
    vec = pl.BlockSpec((1, SSM), lambda i: (0, 0))
    return _pcall(body, name="s5_tail_fwd", grid=(L // tl,),
                  in_specs=[blk(0), blk(0), blk(1), vec, pl.BlockSpec((SSM, SSM), lambda i: (0, 0)), vec],
                  out_specs=blk(0), out_shape=_sds((L, SSM)), vmem_mb=40)(yraw, proj, proj, dsk, wglu, bglu)


def _s5_tail_bwd(dys, yraw, proj, dsk, wglu, bglu, dproj):
    tl = 512
    ni = L // tl

    def body(dys_ref, y_ref, u_ref, z_ref, dsk_ref, w_ref, b_ref, _,
             dy_ref, dus_ref, dz_ref, dw_ref, db_ref, dd_ref, acc_ref):
        i = pl.program_id(0)
        u = u_ref[...]
        y1 = y_ref[...] + dsk_ref[...] * u
        y2 = _gelu(y1)
        y2b = y2.astype(BF16)
        sg = _sigmoid(_dot(y2b, w_ref[...]) + b_ref[...])
        y3 = y2 * sg
        z = z_ref[...]
        sz = _sigmoid(z)
        dys = dys_ref[...]
        dy3 = dys * (z * sz)
        dz_ref[...] = (dys * y3 * (sz * (1.0 + z * (1.0 - sz)))).astype(BF16)
        dgl = (dy3 * y2) * (sg * (1.0 - sg))
        dglb = dgl.astype(BF16)
        dy2 = dy3 * sg + _dot_nt(dglb, w_ref[...])
        dy1 = dy2 * _gelu_grad(y1)
        dy_ref[...] = dy1
        dus_ref[...] = dsk_ref[...] * dy1
        dw = _dot_tn(y2b, dglb)
        db = jnp.sum(dgl, axis=0, keepdims=True)
        dd = jnp.sum(dy1 * u, axis=0, keepdims=True)

        @pl.when(i == 0)
        def _():
            acc_ref[...] = dw
            db_ref[...] = db
            dd_ref[...] = dd

        @pl.when(i > 0)
        def _():
            acc_ref[...] += dw
            db_ref[...] += db
            dd_ref[...] += dd

        @pl.when(i == ni - 1)
        def _():
            dw_ref[...] = acc_ref[...].astype(BF16)

    blk = lambda c: pl.BlockSpec((tl, SSM), lambda i: (i, c))
    vec = pl.BlockSpec((1, SSM), lambda i: (0, 0))
    mat = pl.BlockSpec((SSM, SSM), lambda i: (0, 0))
    return _pcall(body, name="s5_tail_bwd", grid=(ni,),
                  in_specs=[blk(0), blk(0), blk(0), blk(1), vec, mat, vec, pl.BlockSpec(memory_space=pl.ANY)],
                  out_specs=[blk(0), blk(0), blk(COL_ZS // 4), mat, vec, vec],
                  out_shape=[_sds((L, SSM)), _sds((L, SSM)), _sds((L, NCOL), BF16), _sds((SSM, SSM), BF16),
                             _sds((1, SSM)), _sds((1, SSM))],
                  scratch=[pltpu.VMEM((SSM, SSM), F32)], aliases={7: 2},
                  vmem_mb=40)(dys, yraw, proj, proj, dsk, wglu, bglu, dproj)


def _attn_blocks(dil):
    n = L // dil
    return [(i * HEAD * dil + r, r * n + i * HEAD, i == 0) for r in range(dil) for i in range(n // HEAD)]


def _rows(start, dil):
    return pl.ds(start, HEAD) if dil == 1 else pl.ds(start, HEAD, stride=dil)


def _residue_rows(r, dil):
    n = L // dil
    return pl.ds(0, L) if dil == 1 else pl.ds(r, n, stride=dil)


def _gather_residues(pairs, dil, dtype=BF16):
    n = L // dil
    for src, dst in pairs:
        for r in range(dil):
            dst[pl.ds(r * n, n), :] = src[_residue_rows(r, dil), :].astype(dtype)


def _band_masks():
    ri = lax.broadcasted_iota(jnp.int32, (HEAD, HEAD), 0)
    ci = lax.broadcasted_iota(jnp.int32, (HEAD, HEAD), 1)
    ri2 = lax.broadcasted_iota(jnp.int32, (HEAD, 2 * HEAD), 0)
    ci2 = lax.broadcasted_iota(jnp.int32, (HEAD, 2 * HEAD), 1)
    return ci <= ri, jnp.logical_and(ci2 >= ri2, ci2 - HEAD <= ri2)


def _qkv_specs(index):
    def spec(off):
        return pl.BlockSpec((L, HEAD), lambda gi, h: (0, off + index(gi, h)))
    return [spec(off) for off in (COL_Q, COL_K, COL_V)]


def _attn_core_fwd(proj):
    def body(q_ref, k_ref, v_ref, o_ref, lse_ref, qp_ref, kp_ref, vp_ref):
        gi = pl.program_id(0)
        m_cur, m_both = _band_masks()
        for g, dil in enumerate(DILATIONS):
            @pl.when(gi == g)
            def _(dil=dil):
                _gather_residues(((q_ref, qp_ref), (k_ref, kp_ref), (v_ref, vp_ref)), dil)
                for st, base, first in _attn_blocks(dil):
                    q = qp_ref[pl.ds(base, HEAD), :]
                    keys = pl.ds(base, HEAD) if first else pl.ds(base - HEAD, 2 * HEAD)
                    s = jnp.where(m_cur if first else m_both, _dot_nt(q, kp_ref[keys, :]) * SCALE, NEG)
                    mx = jnp.max(s, axis=-1, keepdims=True)
                    p = jnp.exp(s - mx)
                    den = jnp.sum(p, axis=-1, keepdims=True)
                    r = _rows(st, dil)
                    o_ref[r, :] = _dot(p.astype(BF16), vp_ref[keys, :]) / den
                    lse_ref[r, :] = jnp.broadcast_to(mx + jnp.log(den), (HEAD, HEAD))

    idx = lambda gi, h: gi * 4 + h
    out = pl.BlockSpec((L, HEAD), lambda gi, h: (0, gi * 4 + h))
    return _pcall(body, name="attn_core_fwd", grid=(3, 4), in_specs=_qkv_specs(idx), out_specs=[out, out],
                  out_shape=[_sds((L, QKV)), _sds((L, QKV))], scratch=[pltpu.VMEM((L, HEAD), BF16)] * 3,
                  vmem_mb=40)(proj, proj, proj)


def _attn_mix_fwd(o, lse, proj):
    tl = 512

    def body(o_ref, l_ref, z_ref, y_ref):
        for h in range(4):
            c = [pl.ds((g * 4 + h) * HEAD, HEAD) for g in range(3)]
            ls = [l_ref[:, c[g]] for g in range(3)]
            m = jnp.maximum(jnp.maximum(ls[0], ls[1]), ls[2])
            e = [jnp.exp(ls[g] - m) for g in range(3)]
            y = (e[0] * o_ref[:, c[0]] + e[1] * o_ref[:, c[1]] + e[2] * o_ref[:, c[2]]) / (e[0] + e[1] + e[2])
            z = z_ref[:, pl.ds(h * HEAD, HEAD)]
            y_ref[:, pl.ds(h * HEAD, HEAD)] = y * (z * _sigmoid(z))

    wide = pl.BlockSpec((tl, QKV), lambda i: (i, 0))
    return _pcall(body, name="attn_mix_fwd", grid=(L // tl,),
                  in_specs=[wide, wide, pl.BlockSpec((tl, AW), lambda i: (i, COL_ZA // 4))],
                  out_specs=pl.BlockSpec((tl, AW), lambda i: (i, 0)), out_shape=_sds((L, AW)),
                  vmem_mb=40)(o, lse, proj)


def _attn_mix_bwd(dya, o, lse, proj, dproj):
    tl = 512

    def body(dya_ref, o_ref, l_ref, z_ref, _, do_ref, c_ref, dz_ref):
        for h in range(4):
            c = [pl.ds((g * 4 + h) * HEAD, HEAD) for g in range(3)]
            hs = pl.ds(h * HEAD, HEAD)
            ls = [l_ref[:, c[g]] for g in range(3)]
            m = jnp.maximum(jnp.maximum(ls[0], ls[1]), ls[2])
            e = [jnp.exp(ls[g] - m) for g in range(3)]
            den = e[0] + e[1] + e[2]
            al = [e[g] / den for g in range(3)]
            y = al[0] * o_ref[:, c[0]] + al[1] * o_ref[:, c[1]] + al[2] * o_ref[:, c[2]]
            z = z_ref[:, hs]
            sz = _sigmoid(z)
            dya = dya_ref[:, hs]
            dz_ref[:, hs] = (dya * y * (sz * (1.0 + z * (1.0 - sz)))).astype(BF16)
            dy = dya * (z * sz)
            tot = jnp.sum(dy * y, axis=-1, keepdims=True)
            for g in range(3):
                do_ref[:, c[g]] = al[g] * dy
                c_ref[:, c[g]] = -(al[g] * tot)

    wide = pl.BlockSpec((tl, QKV), lambda i: (i, 0))
    nar = pl.BlockSpec((tl, AW), lambda i: (i, 0))
    za = pl.BlockSpec((tl, AW), lambda i: (i, COL_ZA // 4))
    return _pcall(body, name="attn_mix_bwd", grid=(L // tl,),
                  in_specs=[nar, wide, wide, za, pl.BlockSpec(memory_space=pl.ANY)],
                  out_specs=[wide, wide, za], out_shape=[_sds((L, QKV)), _sds((L, QKV)), _sds((L, NCOL), BF16)],
                  aliases={4: 2}, vmem_mb=48)(dya, o, lse, proj, dproj)


def _attn_core_bwd(proj, do, lse, cc, dproj):
    def body(q_ref, k_ref, v_ref, do_ref, lse_ref, c_ref, _, dp_ref,
             qp_ref, kp_ref, vp_ref, dop_ref, lsep_ref, cp_ref, dqp_ref, dkp_ref, dvp_ref, tok_ref, out_ref, sems):
        gi = pl.program_id(0)
        head = gi * 4 + pl.program_id(1)
        m_cur, m_both = _band_masks()
        dkp_ref[...] = jnp.zeros((L, HEAD), F32)
        dvp_ref[...] = jnp.zeros((L, HEAD), F32)
        copies = []
        for g, dil in enumerate(DILATIONS):
            @pl.when(gi == g)
            def _(dil=dil):
                _gather_residues(((q_ref, qp_ref), (k_ref, kp_ref), (v_ref, vp_ref), (do_ref, dop_ref)), dil)
                _gather_residues(((lse_ref, lsep_ref), (c_ref, cp_ref)), dil, F32)
                for _, base, first in _attn_blocks(dil):
                    mine = pl.ds(base, HEAD)
                    keys = mine if first else pl.ds(base - HEAD, 2 * HEAD)
                    q, do_b, kb = qp_ref[mine, :], dop_ref[mine, :], kp_ref[keys, :]
                    lse_b, c_b = lsep_ref[mine, :], cp_ref[mine, :]
                    if not first:
                        lse_b = jnp.concatenate([lse_b, lse_b], axis=1)
                        c_b = jnp.concatenate([c_b, c_b], axis=1)
                    s = _dot_nt(q, kb) * SCALE
                    p = jnp.where(m_cur if first else m_both, jnp.exp(s - lse_b), 0.0)
                    ds = (p * (_dot_nt(do_b, vp_ref[keys, :]) + c_b) * SCALE).astype(BF16)
                    dqp_ref[mine, :] = _dot(ds, kb)
                    dkp_ref[keys, :] += _dot_tn(ds, q)
                    dvp_ref[keys, :] += _dot_tn(p.astype(BF16), do_b)
                n = L // dil
                for slot, src in enumerate((dqp_ref, dkp_ref, dvp_ref)):
                    for r in range(dil):
                        tok_ref[_residue_rows(r, dil), :] = src[pl.ds(r * n, n), :]
                    out_ref[slot] = tok_ref[...].astype(BF16)

        for slot, off in enumerate((COL_Q, COL_K, COL_V)):
            cols = pl.ds(pl.multiple_of((off + head) * HEAD, HEAD), HEAD)
            copies.append(pltpu.make_async_copy(out_ref.at[slot], dp_ref.at[:, cols], sems.at[slot]))
            copies[-1].start()
        for cp in copies:
            cp.wait()

    idx = lambda gi, h: gi * 4 + h
    blk = pl.BlockSpec((L, HEAD), lambda gi, h: (0, gi * 4 + h))
    hbm = pl.BlockSpec(memory_space=pl.ANY)
    return _pcall(body, name="attn_core_bwd", grid=(3, 4), in_specs=_qkv_specs(idx) + [blk, blk, blk, hbm],
                  out_specs=hbm, out_shape=_sds((L, NCOL), BF16),
                  scratch=[pltpu.VMEM((L, HEAD), BF16)] * 4 + [pltpu.VMEM((L, HEAD), F32)] * 6 +
                  [pltpu.VMEM((3, L, HEAD), BF16), pltpu.SemaphoreType.DMA((3,))],
                  aliases={6: 0}, vmem_mb=48)(proj, proj, proj, do, lse, cc, dproj)


def _merge_fwd_math(ys_b, ya_b, gs, ga, wbs_ref, wba_ref, wout_ref):
    bs = jnp.concatenate([_dot(ys_b, wbs_ref[s]) for s in range(NCHIP)], axis=1)
    ba = jnp.concatenate([_dot(ya_b, wba_ref[s]) for s in range(NCHIP)], axis=1)
    sgs, sga = _sigmoid(gs), _sigmoid(ga)
    merged = sgs * bs + sga * ba
    out = _dot(merged.astype(BF16), wout_ref[...])
    inv = lax.rsqrt(jnp.mean(out * out, axis=-1, keepdims=True) + RMS_EPS)
    return bs, ba, sgs, sga, merged, out, inv


def _merge_specs(tl):
    row = lambda w, c: pl.BlockSpec((tl, w), lambda i: (i, c))
    return [row(SSM, 0), row(AW, 0), row(D, COL_GS // 8), row(D, COL_GA // 8),
            pl.BlockSpec((NCHIP, SSM, D // NCHIP), lambda i: (0, 0, 0)),
            pl.BlockSpec((NCHIP, AW, D // NCHIP), lambda i: (0, 0, 0)),
            pl.BlockSpec((D, D), lambda i: (0, 0)),
            pl.BlockSpec((1, D), lambda i: (0, 0))]


def _merge_out_fwd(ys, ya, proj, wbs4, wba4, wout, g2, x, dep=None):
    tl = 512
    deps, dep_specs = _token_operand(dep)

    def body(ys_ref, ya_ref, gs_ref, ga_ref, wbs_ref, wba_ref, wout_ref, g2_ref, x_ref, *rest):
        o_ref = rest[-1]
        *_, out, inv = _merge_fwd_math(ys_ref[...].astype(BF16), ya_ref[...].astype(BF16), gs_ref[...],
                                       ga_ref[...], wbs_ref, wba_ref, wout_ref)
        o_ref[...] = x_ref[...] + out * inv * g2_ref[...]

    row = pl.BlockSpec((tl, D), lambda i: (i, 0))
    return _pcall(body, name="merge_out_fwd", grid=(L // tl,), in_specs=_merge_specs(tl) + [row] + dep_specs,
                  out_specs=row, out_shape=_sds((L, D)),
                  vmem_mb=48)(ys, ya, proj, proj, wbs4, wba4, wout, g2, x, *deps)


def _merge_out_bwd(dxn, ys, ya, proj, wbs4, wba4, wout, g2, dep=None):
    tl = 256
    ni = L // tl
    cw = D // NCHIP
    deps, dep_specs = _token_operand(dep)

    def body(dxn_ref, ys_ref, ya_ref, gs_ref, ga_ref, wbs_ref, wba_ref, wout_ref, g2_ref, *rest):
        (dys_ref, dya_ref, dgate_ref, dwbs_ref, dwba_ref, dwout_ref, dg2_ref,
         abs_ref, aba_ref, aout_ref) = rest[len(deps):]
        i = pl.program_id(0)

        @pl.when(i == 0)
        def _():
            abs_ref[...] = jnp.zeros(abs_ref.shape, F32)
            aba_ref[...] = jnp.zeros(aba_ref.shape, F32)
            aout_ref[...] = jnp.zeros(aout_ref.shape, F32)
            dg2_ref[...] = jnp.zeros(dg2_ref.shape, F32)

        ys_b, ya_b = ys_ref[...].astype(BF16), ya_ref[...].astype(BF16)
        bs, ba, sgs, sga, merged, out, inv = _merge_fwd_math(ys_b, ya_b, gs_ref[...], ga_ref[...],
                                                             wbs_ref, wba_ref, wout_ref)
        nrm = out * inv
        dxn = dxn_ref[...]
        dg2_ref[...] += jnp.sum(dxn * nrm, axis=0, keepdims=True)
        dn = dxn * g2_ref[...]
        dout = (inv * (dn - nrm * jnp.mean(dn * nrm, axis=-1, keepdims=True))).astype(BF16)
        aout_ref[...] += _dot_tn(merged.astype(BF16), dout)
        dm = _dot_nt(dout, wout_ref[...])
        dbs, dba = dm * sgs, dm * sga
        dgate_ref[:, :D] = (dm * bs * (sgs * (1.0 - sgs))).astype(BF16)
        dgate_ref[:, D:] = (dm * ba * (sga * (1.0 - sga))).astype(BF16)
        dbs_b, dba_b = dbs.astype(BF16), dba.astype(BF16)
        dys = None
        dya = None
        for s in range(NCHIP):
            cs = slice(s * cw, (s + 1) * cw)
            p_s = _dot_nt(dbs_b[:, cs], wbs_ref[s])
            p_a = _dot_nt(dba_b[:, cs], wba_ref[s])
            dys = p_s if dys is None else dys + p_s
            dya = p_a if dya is None else dya + p_a
            abs_ref[s] += _dot_tn(ys_b, dbs_b[:, cs])
            aba_ref[s] += _dot_tn(ya_b, dba_b[:, cs])

        dys_ref[...] = dys
        dya_ref[...] = dya

        @pl.when(i == ni - 1)
        def _():
            dwbs_ref[...] = abs_ref[...].astype(BF16)
            dwba_ref[...] = aba_ref[...].astype(BF16)
            dwout_ref[...] = aout_ref[...].astype(BF16)

    row = lambda w: pl.BlockSpec((tl, w), lambda i: (i, 0))
    w4 = pl.BlockSpec((NCHIP, SSM, cw), lambda i: (0, 0, 0))
    sq = pl.BlockSpec((D, D), lambda i: (0, 0))
    vec = pl.BlockSpec((1, D), lambda i: (0, 0))
    return _pcall(body, name="merge_out_bwd", grid=(ni,), in_specs=[row(D)] + _merge_specs(tl) + dep_specs,
                  out_specs=[row(SSM), row(AW), pl.BlockSpec((tl, 2 * D), lambda i: (i, COL_GS // 16)), w4, w4, sq,
                             vec],
                  out_shape=[_sds((L, SSM)), _sds((L, AW)), _sds((L, NCOL), BF16),
                             _sds((NCHIP, SSM, cw), BF16), _sds((NCHIP, AW, cw), BF16), _sds((D, D), BF16),
                             _sds((1, D))],
                  scratch=[pltpu.VMEM((NCHIP, SSM, cw), F32), pltpu.VMEM((NCHIP, AW, cw), F32),
                           pltpu.VMEM((D, D), F32)],
                  vmem_mb=56)(dxn, ys, ya, proj, proj, wbs4, wba4, wout, g2, *deps)


def _loss_head(y, target):
    tl = 512

    def body(y_ref, t_ref, loss_ref, dy_ref):
        i = pl.program_id(0)
        err = y_ref[...] - t_ref[...]
        dy_ref[...] = err / D
        part = 0.5 * jnp.sum(jnp.mean(err * err, axis=-1, keepdims=True), axis=0, keepdims=True)
        part = jnp.broadcast_to(part, (8, 128))

        @pl.when(i == 0)
        def _():
            loss_ref[...] = part

        @pl.when(i > 0)
        def _():
            loss_ref[...] += part

    row = pl.BlockSpec((tl, D), lambda i: (i, 0))
    return _pcall(body, name="loss_head", grid=(L // tl,), in_specs=[row, row],
                  out_specs=[pl.BlockSpec((8, 128), lambda i: (0, 0)), row],
                  out_shape=[_sds((8, 128)), _sds((L, D))], vmem_mb=40)(y, target)


def _adamw_math(w, g, m, v):
    m = ADAM_B1 * m + (1.0 - ADAM_B1) * g
    v = ADAM_B2 * v + (1.0 - ADAM_B2) * (g * g)
    m_hat = m / (1.0 - ADAM_B1 ** ADAM_STEP)
    v_hat = v / (1.0 - ADAM_B2 ** ADAM_STEP)
    delta = -ADAM_LR * (m_hat / (jnp.sqrt(v_hat) + ADAM_EPS) + ADAM_WD * w)
    return delta, m, v


def _adamw_big(layer, w, m, v, own, sib, prev):
    _, r, c = w.shape
    tr = min(r, 128)

    def body(w_ref, m_ref, v_ref, own_ref, sib_ref, *rest):
        g_ref, d_ref, nm_ref, nv_ref = rest[-4:]
        a = own_ref[0].astype(F32)
        b = sib_ref[0].astype(F32)
        for s in range(1, NCHIP):
            a = a + own_ref[s].astype(F32)
            b = b + sib_ref[s].astype(F32)
        g = a + b
        delta, nm, nv = _adamw_math(w_ref[...], g, m_ref[...], v_ref[...])
        g_ref[...] = g
        d_ref[...] = delta
        nm_ref[...] = nm
        nv_ref[...] = nv

    lay = pl.BlockSpec((None, tr, c), lambda i: (layer, i, 0))
    slots = pl.BlockSpec((NCHIP, tr, c), lambda i: (0, i, 0))
    ins = [w, m, v, own, sib]
    in_specs = [lay, lay, lay, slots, slots]
    aliases = {}
    if prev is not None:
        ins += list(prev)
        in_specs += [pl.BlockSpec(memory_space=pl.ANY)] * 4
        aliases = {5 + k: k for k in range(4)}
    return _pcall(body, name="adamw_big", grid=(r // tr,), in_specs=in_specs, out_specs=[lay] * 4,
                  out_shape=[_sds(w.shape)] * 4, aliases=aliases, vmem_mb=48)(*ins)


SMALL_TILE = 512


def _adamw_small(w, parts, m, v):
    r = w.shape[0]

    def body(w_ref, p_ref, m_ref, v_ref, g_ref, d_ref, nm_ref, nv_ref):
        g = p_ref[0].astype(F32)
        for dev in range(1, 8):
            g = g + p_ref[dev].astype(F32)
        delta, nm, nv = _adamw_math(w_ref[...], g, m_ref[...], v_ref[...])
        g_ref[...] = g
        d_ref[...] = delta
        nm_ref[...] = nm
        nv_ref[...] = nv

    row = pl.BlockSpec((SMALL_TILE, 128), lambda i: (i, 0))
    return _pcall(body, name="adamw_small", grid=(r // SMALL_TILE,),
                  in_specs=[row, pl.BlockSpec((8, SMALL_TILE, 128), lambda i: (0, i, 0)), row, row],
                  out_specs=[row] * 4, out_shape=[_sds((r, 128))] * 4, vmem_mb=40)(w, parts, m, v)


def _place():
    x, y, c = lax.axis_index("x"), lax.axis_index("y"), lax.axis_index("c")
    return x, y, c, 2 * x + y


def _chip_peer(x, y, j):
    return (1 - x if j & 2 else x), (1 - y if j & 1 else y)


_HBM = pl.BlockSpec(memory_space=pltpu.HBM)
_SEM = pl.BlockSpec(memory_space=pltpu.SEMAPHORE)
_EFFECT = pltpu.SideEffectType.DATAFLOW_SIDE_EFFECTING


def _in_hbm(a):
    return pltpu.with_memory_space_constraint(a, pltpu.HBM)


def _copies_start(name, plan, ncopy, srcs, lands, dep=None):
    n = len(srcs) + len(lands)
    deps, dep_specs = _token_operand(dep)

    def body(*refs):
        send_sems, recv_sems = refs[n + len(deps)], refs[n + len(deps) + 1]
        token = refs[-1]
        for k, (src, dst, dev) in enumerate(plan(refs[:len(srcs)], refs[len(srcs):n])):
            if dev is None:
                pltpu.make_async_copy(src, dst, recv_sems.at[k]).start()
            else:
                pltpu.make_async_remote_copy(src_ref=src, dst_ref=dst, send_sem=send_sems.at[k],
                                             recv_sem=recv_sems.at[k], device_id=dev, device_id_type=MESH).start()
        token[...] = jnp.zeros_like(token)

    bufs = list(srcs) + list(lands)
    outs = pl.pallas_call(
        body, name=name,
        out_shape=(pltpu.SemaphoreType.DMA((ncopy,)), pltpu.SemaphoreType.DMA((ncopy,)),
                   *[pltpu.HBM(a.shape, a.dtype) for a in bufs], _sds((8, 128))),
        in_specs=[_HBM] * n + dep_specs,
        out_specs=(_SEM, _SEM, *[_HBM] * n, pl.BlockSpec(memory_space=pltpu.VMEM)),
        input_output_aliases={i: 2 + i for i in range(n)},
        compiler_params=pltpu.CompilerParams(has_side_effects=_EFFECT),
    )(*[_in_hbm(a) for a in bufs], *deps)
    return outs[0], outs[1], list(outs[2:2 + len(srcs)]), list(outs[2 + len(srcs):2 + n]), outs[-1]


def _copies_wait(name, plan, send_sems, recv_sems, srcs, lands, after):
    n = len(srcs) + len(lands)
    after = list(after)

    def body(*refs):
        s_sems, r_sems = refs[n], refs[n + 1]
        for k, (src, dst, dev) in enumerate(plan(refs[:len(srcs)], refs[len(srcs):n])):
            if dev is None:
                pltpu.make_async_copy(src, dst, r_sems.at[k]).wait()
                continue
            cp = pltpu.make_async_remote_copy(src_ref=src, dst_ref=dst, send_sem=s_sems.at[k],
                                              recv_sem=r_sems.at[k], device_id=dev, device_id_type=MESH)
            cp.wait_send()
            cp.wait_recv()

    bufs = list(srcs) + list(lands)
    outs = pl.pallas_call(
        body, name=name,
        out_shape=tuple(pltpu.HBM(a.shape, a.dtype) for a in bufs),
        in_specs=[_HBM] * n + [_SEM, _SEM] + [pl.BlockSpec(memory_space=pl.ANY)] * len(after),
        out_specs=tuple([_HBM] * n),
        input_output_aliases={i: i for i in range(n)},
        compiler_params=pltpu.CompilerParams(has_side_effects=_EFFECT),
    )(*bufs, send_sems, recv_sems, *after)
    return list(outs[:len(srcs)]), list(outs[len(srcs):])


def _with_own_slot(block, slot, nslots=NCHIP):
    land = lax.empty((nslots,) + block.shape, block.dtype)
    return lax.dynamic_update_slice(land, block[None], (slot,) + (0,) * block.ndim)


def _my_half(land, slot, c):
    half = land.shape[1] // 2
    return land.at[slot, pl.ds(pl.multiple_of(c * half, 16), half)]


def _gather_ici_plan(srcs, lands):
    x, y, c, s = _place()
    return [(_my_half(land, s, c), _my_half(land, s, c), (*_chip_peer(x, y, j), c))
            for land in lands for j in range(1, NCHIP)]


def _gather_d2d_plan(srcs, lands):
    x, y, c, s = _place()
    return [(_my_half(land, s ^ j, c), _my_half(land, s ^ j, c), (x, y, 1 - c))
            for land in lands for j in range(1, NCHIP)]


def _exchange_plan(srcs, lands):
    x, y, c, s = _place()
    return [(src.at[s ^ j], land.at[s], (*_chip_peer(x, y, j), c) if j else None)
            for src, land in zip(srcs, lands) for j in range(NCHIP)]


def _sibling_plan(srcs, lands):
    x, y, c, _ = _place()
    return [(src, land, (x, y, 1 - c)) for src, land in zip(srcs, lands)]


def _everyone_plan(srcs, lands):
    x, y, c, _ = _place()
    me = 4 * x + 2 * y + c
    return [(srcs[0], lands[0].at[me], (*_chip_peer(x, y, j >> 1), 1 - c if j & 1 else c)) for j in range(1, 8)]


def _flatten_small(parts):
    flat = jnp.concatenate([p.reshape(-1) for p in parts])
    n = flat.shape[0]
    rows = -(-n // (128 * SMALL_TILE)) * SMALL_TILE
    return jnp.pad(flat, (0, rows * 128 - n)).reshape(rows, 128)


def _unflatten_small(buf, like):
    flat = buf.reshape(-1)
    out, at = [], 0
    for p in like:
        out.append(flat[at:at + p.size].reshape(p.shape))
        at += p.size
    return out


def kernel(x, pre_norm_g, w_in, lambda_re, lambda_im, log_dt, b_re, b_im, c_re, c_im, d_skip, w_glu, b_glu, w_branch_s, w_branch_a, w_out, post_norm_g, loss_target, m_pre_norm_g, m_w_in, m_lambda_re, m_lambda_im, m_log_dt, m_b_re, m_b_im, m_c_re, m_c_im, m_d_skip, m_w_glu, m_b_glu, m_w_branch_s, m_w_branch_a, m_w_out, m_post_norm_g, v_pre_norm_g, v_w_in, v_lambda_re, v_lambda_im, v_log_dt, v_b_re, v_b_im, v_c_re, v_c_im, v_d_skip, v_w_glu, v_b_glu, v_w_branch_s, v_w_branch_a, v_w_out, v_post_norm_g):
    nb = DEPTH * NCB
    lre_c = lambda_re.reshape(nb, 1, HW)
    lim_c = lambda_im.reshape(nb, 1, HW)
    ldt_c = jnp.broadcast_to(log_dt[:, :, None], (DEPTH, 32, 64)).reshape(nb, 1, HW)
    b_rows = lambda t: t.reshape(DEPTH, NCB, 8, 64, 16).transpose(0, 1, 4, 2, 3).reshape(nb, 16, HW)
    c_rows = lambda t: t.reshape(DEPTH, NCB, 8, 16, 64).transpose(0, 1, 3, 2, 4).reshape(nb, 16, HW)
    bre_c, bim_c = b_rows(b_re), b_rows(b_im)
    ar, ai, wb, wct = _s5_prep(lre_c, lim_c, ldt_c, bre_c, bim_c, c_rows(c_re), c_rows(c_im))
    ar4 = ar.reshape(DEPTH, NCB, 1, HW)
    ai4 = ai.reshape(DEPTH, NCB, 1, HW)
    wb = wb.reshape(DEPTH, NCB, 128, SW)
    wct = wct.reshape(DEPTH, NCB, 128, SW)

    chip = 2 * lax.axis_index("x") + lax.axis_index("y")
    device = 2 * chip + lax.axis_index("c")

    def gather_start(l, dep):
        blocks = [w_in[l], w_glu[l], w_branch_s[l], w_branch_a[l], w_out[l]]
        if l == 0:
            blocks = lax.optimization_barrier(blocks)
        lands = [_with_own_slot(a.astype(BF16), chip) for a in blocks]
        return _copies_start(f"gather_start_{l}", _gather_ici_plan, 15, [], lands, dep)

    def gather_forward(l, flight, after):
        send, recv, _, lands, _ = flight
        _, lands = _copies_wait(f"gather_wait_{l}", _gather_ici_plan, send, recv, [], lands, after)
        return _copies_start(f"forward_start_{l}", _gather_d2d_plan, 15, [], lands)

    def gather_done(l, forward, after):
        send, recv, _, lands, _ = forward
        return _copies_wait(f"forward_wait_{l}", _gather_d2d_plan, send, recv, [], lands, after)[1]

    small_w = [pre_norm_g, lambda_re, lambda_im, log_dt, b_re, b_im, c_re, c_im, d_skip, b_glu, post_norm_g]
    small_m = [m_pre_norm_g, m_lambda_re, m_lambda_im, m_log_dt, m_b_re, m_b_im, m_c_re, m_c_im, m_d_skip,
               m_b_glu, m_post_norm_g]
    small_v = [v_pre_norm_g, v_lambda_re, v_lambda_im, v_log_dt, v_b_re, v_b_im, v_c_re, v_c_im, v_d_skip,
               v_b_glu, v_post_norm_g]
    flat_w, flat_m, flat_v = _flatten_small(small_w), _flatten_small(small_m), _flatten_small(small_v)

    xs = [x[0]]
    saved = []
    forward = gather_forward(0, gather_start(0, None), [xs[0], flat_w, flat_m, ar, ai, wb, wct])
    for l in range(DEPTH):
        w4, wglu4, wbs4, wba4, wout4 = gather_done(l, forward, [xs[l]] + ([flat_v] if l == 0 else []))
        started = None
        if l + 1 < DEPTH:
            flight = gather_start(l + 1, wglu4)
            started = flight[4]
        wglu = wglu4.reshape(SSM, SSM)
        wout = wout4.reshape(D, D)
        wb4, wc4 = wb[l], wct[l]
        g1 = pre_norm_g[l].reshape(1, D)
        g2 = post_norm_g[l].reshape(1, D)
        dsk = d_skip[l].reshape(1, SSM)
        bgl = b_glu[l].reshape(1, SSM)
        proj, ht = _rms_proj_fwd(xs[l], g1, w4, started)
        states, yraw = _s5_scan_fwd(proj, wb4, wc4, ar4[l], ai4[l])
        o, lse = (_in_hbm(a) for a in _attn_core_fwd(proj))
        forwarded = None
        if l + 1 < DEPTH:
            forward = gather_forward(l + 1, flight, [yraw, o])
            forwarded = forward[4]
        yraw = _in_hbm(yraw)
        ys = _in_hbm(_s5_tail_fwd(yraw, proj, dsk, wglu, bgl))
        ya = _in_hbm(_attn_mix_fwd(o, lse, proj))
        xs.append(_merge_out_fwd(ys, ya, proj, wbs4, wba4, wout, g2, xs[l], forwarded))
        saved.append((w4, wglu, wbs4, wba4, wout, wb4, wc4, g1, g2, dsk, bgl, proj, ht, states, yraw, ys, ya, o, lse))

    loss_part, dx = _loss_head(xs[DEPTH], loss_target[0])
    loss = lax.psum(loss_part[0, 0], ("x", "y", "c"))

    big_w = (w_in, w_glu.reshape(DEPTH, 128, SSM), w_branch_s, w_branch_a, w_out)
    big_m = (m_w_in, m_w_glu, m_w_branch_s, m_w_branch_a, m_w_out)
    big_v = (v_w_in, v_w_glu, v_w_branch_s, v_w_branch_a, v_w_out)
    big_out = [None] * 5
    small = {k: [None] * DEPTH for k in ("g1", "da", "dbb", "dcc", "dsk", "bgl", "g2")}
    ici = [None] * DEPTH
    d2d = [None] * DEPTH

    def exchange_start(l, parts):
        lands = [lax.empty(p.shape, p.dtype) for p in parts]
        ici[l] = _copies_start(f"exchange_start_{l}", _exchange_plan, 20, parts, lands)

    def handoff_start(l, after):
        send, recv, srcs, lands, _ = ici[l]
        _, own = _copies_wait(f"exchange_wait_{l}", _exchange_plan, send, recv, srcs, lands, after)
        d2d[l] = _copies_start(f"handoff_start_{l}", _sibling_plan, 5, own,
                               [lax.empty(a.shape, a.dtype) for a in own])

    def update(l, after):
        send, recv, own, lands, _ = d2d[l]
        own, sib = _copies_wait(f"handoff_wait_{l}", _sibling_plan, send, recv, own, lands, after)
        for k in range(5):
            big_out[k] = _adamw_big(l, big_w[k], big_m[k], big_v[k], own[k], sib[k], big_out[k])

    for l in reversed(range(DEPTH)):
        w4, wglu, wbs4, wba4, wout, wb4, wc4, g1, g2, dsk, bgl, proj, ht, states, yraw, ys, ya, o, lse = saved[l]
        started = None
        if l + 1 < DEPTH:
            started = ici[l + 1][4] + d2d[l + 2][4] if l + 2 < DEPTH else ici[l + 1][4]
        dys, dya, dproj, dwbs, dwba, dwout, dg2 = _merge_out_bwd(dx, ys, ya, proj, wbs4, wba4, wout, g2, started)
        dyraw, du_skip, dproj, dwglu, dbgl, ddsk = _s5_tail_bwd(dys, yraw, proj, dsk, wglu, bgl, dproj)
        dproj, dbb, dcc, da = _s5_scan_bwd(dyraw, states, proj, du_skip, wb4, wc4, ar4[l], ai4[l], dproj)
        do, cterm, dproj = _attn_mix_bwd(dya, o, lse, proj, dproj)
        dproj = _attn_core_bwd(proj, do, lse, cterm, dproj)
        dh, dwin = _proj_bwd(dproj, ht, w4)
        parts = [dwin, dwglu.reshape(NCHIP, 128, SSM), dwbs, dwba, dwout.reshape(NCHIP, D // NCHIP, D)]
        exchange_start(l, parts)
        dx, dg1 = _rms_bwd(dh, xs[l], g1, dx, ici[l][4])
        if l + 1 < DEPTH:
            handoff_start(l + 1, [ici[l][4]])
        small["g1"][l], small["g2"][l], small["dsk"][l], small["bgl"][l] = dg1, dg2, ddsk, dbgl
        small["da"][l], small["dbb"][l], small["dcc"][l] = da, dbb, dcc

    dlre, dlim, dldt, dbre_c, dbim_c = _s5_prep_bwd(
        lre_c, lim_c, ldt_c, bre_c, bim_c, jnp.stack(small["da"]).reshape(nb, 1, SW),
        jnp.stack(small["dbb"]).reshape(nb, 16, SW))
    dcc = jnp.stack(small["dcc"]).reshape(nb, 16, SW)
    b_back = lambda t: t.reshape(DEPTH, NCB, 16, 8, 64).transpose(0, 1, 3, 4, 2).reshape(DEPTH, 32, 64, 16)
    c_back = lambda t: t.reshape(DEPTH, NCB, 16, 8, 64).transpose(0, 1, 3, 2, 4).reshape(DEPTH, 32, 16, 64)
    small_g = [jnp.stack(small["g1"]).reshape(DEPTH, D), dlre.reshape(DEPTH, 32, 64), dlim.reshape(DEPTH, 32, 64),
               dldt.reshape(DEPTH, 32, 64).sum(-1), b_back(dbre_c), b_back(dbim_c), c_back(dcc[:, :, :HW]),
               c_back(dcc[:, :, HW:]),
               jnp.stack(small["dsk"]).reshape(DEPTH, SSM), jnp.stack(small["bgl"]).reshape(DEPTH, SSM),
               jnp.stack(small["g2"]).reshape(DEPTH, D)]
    part = _flatten_small(small_g).astype(BF16)
    send, recv, srcs, lands, token = _copies_start("small_start", _everyone_plan, 7, [part],
                                                   [_with_own_slot(part, device, 8)])
    update(DEPTH - 1, [token, ici[0][4]])
    for l in range(DEPTH - 2, 0, -1):
        update(l, [out[0] for out in big_out])
    handoff_start(0, [out[0] for out in big_out])
    update(0, [d2d[0][4]])
    _, (parts8,) = _copies_wait("small_wait", _everyone_plan, send, recv, srcs, lands, [out[0] for out in big_out])
    g_flat, d_flat, nm_flat, nv_flat = _adamw_small(flat_w, parts8, flat_m, flat_v)
    sg = _unflatten_small(g_flat, small_w)
    sd = _unflatten_small(d_flat, small_w)
    snm = _unflatten_small(nm_flat, small_w)
    snv = _unflatten_small(nv_flat, small_w)

    def ordered(sm, which):
        bg = [big_out[k][which] for k in range(5)]
        bg[1] = bg[1].reshape(DEPTH, 128, SSM)
        return [sm[0], bg[0], sm[1], sm[2], sm[3], sm[4], sm[5], sm[6], sm[7], sm[8], bg[1], sm[9], bg[2], bg[3],
                bg[4], sm[10]]

    return (loss, dx[None], *ordered(sg, 0), *ordered(sd, 1), *ordered(snm, 2), *ordered(snv, 3))
```
